```python
import math
import jax, jax.numpy as jnp
from jax import lax
import numpy as np

D_MODEL = 1024
BATCH = 16
SEQ = 4096
DEPTH = 1
DEC_BATCH = 8
DEC_SEQ = 32
PAST_LEN = 1024

CHUNK = 64
Q_BLOCK = 128
N_HEADS = 8
HEAD_DIM = 64
N_KV_HEADS = 2
GROUP = N_HEADS // N_KV_HEADS
ATTN_WIDTH = N_HEADS * HEAD_DIM
KV_WIDTH = N_KV_HEADS * HEAD_DIM
N_IDX_HEADS = 8
IDX_DIM = 32
TOPK_MAX = 256
CONV_WIDTH = D_MODEL - ATTN_WIDTH
CONV_GROUPS = 8
CONV_K = 3
N_BUCKETS = 32
MAX_DISTANCE = 128
N_EXPERTS = 32
TOP_K = 4
D_FF = D_MODEL
SWIGLU_LIMIT = 7.0
SWIGLU_ALPHA = 1.702
DN_ALPHA = (2 * DEPTH) ** 0.25
DN_BETA = (8 * DEPTH) ** -0.25
LN_EPS = 1e-5
IN_SIZES = (ATTN_WIDTH, KV_WIDTH, KV_WIDTH, N_IDX_HEADS * IDX_DIM, IDX_DIM, N_IDX_HEADS,
            CONV_WIDTH, CONV_WIDTH, CONV_WIDTH)
IN_WIDTH = sum(IN_SIZES)

kernel_name = "hybrid_dsa_shortconv_moe_stream_step"


def layer_norm(x, g, b):
    xf = x.astype(jnp.float32)
    mu = jnp.mean(xf, axis=-1, keepdims=True)
    var = jnp.mean(jnp.square(xf - mu), axis=-1, keepdims=True)
    y = (xf - mu) * lax.rsqrt(var + LN_EPS) * g.astype(jnp.float32) + b.astype(jnp.float32)
    return y.astype(x.dtype)


def rel_bucket(rel):
    nb = N_BUCKETS // 2
    max_exact = nb // 2
    ret = jnp.where(rel > 0, nb, 0)
    n = jnp.abs(rel)
    nf = jnp.maximum(n, 1).astype(jnp.float32)
    large = max_exact + (jnp.log(nf / max_exact) / math.log(MAX_DISTANCE / max_exact)
                         * (nb - max_exact)).astype(jnp.int32)
    large = jnp.minimum(large, nb - 1)
    return ret + jnp.where(n < max_exact, n, large)


def project(x, w_in):
    B, S, _ = x.shape
    proj = jnp.einsum('bsd,de->bse', x, w_in)
    parts = []
    o = 0
    for n in IN_SIZES:
        parts.append(proj[..., o:o + n])
        o += n
    q, k, v, q_idx, k_idx, w_idx, g_b, g_c, h = parts
    q = q.reshape(B, S, N_HEADS, HEAD_DIM)
    k = k.reshape(B, S, N_KV_HEADS, HEAD_DIM)
    v = v.reshape(B, S, N_KV_HEADS, HEAD_DIM)
    q_idx = q_idx.reshape(B, S, N_IDX_HEADS, IDX_DIM)
    return q, k, v, q_idx, k_idx, w_idx, g_b, g_c, h


def dsa_attend(q, q_idx, w_idx, q_pos, k_all, v_all, kidx_all, k_pos, topk, rel_table):
    B, Q = q.shape[0], q.shape[1]
    q_chunk = q_pos // CHUNK
    admissible = (k_pos[None, :] // CHUNK) <= q_chunk[:, None]
    dots = jnp.einsum('bqhd,bsd->bqhs', q_idx, kidx_all)
    score = jnp.einsum('bqh,bqhs->bqs', w_idx, jax.nn.relu(dots)).astype(jnp.float32)
    score = jnp.where(admissible[None], score, -jnp.inf)
    _, idx = lax.top_k(score, topk)
    sel_pos = k_pos[idx]
    valid = (sel_pos // CHUNK) <= q_chunk[None, :, None]
    gather = jax.vmap(lambda kb, ib: kb[ib])
    k_sel = gather(k_all, idx)
    v_sel = gather(v_all, idx)
    qg = q.reshape(B, Q, N_KV_HEADS, GROUP, HEAD_DIM)
    logits = jnp.einsum('bqngd,bqknd->bqngk', qg, k_sel).astype(jnp.float32) * (HEAD_DIM ** -0.5)
    bias = rel_table[rel_bucket(sel_pos - q_pos[None, :, None])].astype(jnp.float32)
    bias = bias.reshape(B, Q, topk, N_KV_HEADS, GROUP).transpose(0, 1, 3, 4, 2)
    logits = jnp.where(valid[:, :, None, None, :], logits + bias, -1e30)
    p = jax.nn.softmax(logits, axis=-1).astype(v_sel.dtype)
    out = jnp.einsum('bqngk,bqknd->bqngd', p, v_sel)
    return out.reshape(B, Q, ATTN_WIDTH)


def short_conv(u, buf, w):
    S = u.shape[1]
    up = jnp.concatenate([buf, u], axis=1)
    y = up[:, 0:S] * w[0] + up[:, 1:S + 1] * w[1] + up[:, 2:S + 2] * w[2]
    return y, up[:, -(CONV_K - 1):]


def token_mixer(x, past_k, past_v, past_kidx, conv_buf, rel_table, w_in, w_conv, w_out):
    B, S, _ = x.shape
    q, k, v, q_idx, k_idx, w_idx, g_b, g_c, h = project(x, w_in)
    if past_k is None:
        P = 0
        k_all, v_all, kidx_all = k, v, k_idx
        conv_buf = jnp.zeros((B, CONV_K - 1, CONV_WIDTH), x.dtype)
    else:
        P = past_k.shape[1]
        k_all = jnp.concatenate([past_k, k], axis=1)
        v_all = jnp.concatenate([past_v, v], axis=1)
        kidx_all = jnp.concatenate([past_kidx, k_idx], axis=1)
    L = P + S
    topk = min(TOPK_MAX, L // 4)
    k_pos = jnp.arange(L, dtype=jnp.int32)
    if past_k is None:
        def block(start):
            sl = lambda a: lax.dynamic_slice_in_dim(a, start, Q_BLOCK, axis=1)
            q_pos = start + jnp.arange(Q_BLOCK, dtype=jnp.int32)
            return dsa_attend(sl(q), sl(q_idx), sl(w_idx), q_pos, k_all, v_all, kidx_all,
                              k_pos, topk, rel_table)
        starts = jnp.arange(S // Q_BLOCK, dtype=jnp.int32) * Q_BLOCK
        attn = lax.map(block, starts)
        attn = attn.transpose(1, 0, 2, 3).reshape(B, S, ATTN_WIDTH)
    else:
        q_pos = P + jnp.arange(S, dtype=jnp.int32)
        attn = dsa_attend(q, q_idx, w_idx, q_pos, k_all, v_all, kidx_all, k_pos, topk, rel_table)
    conv_y, new_buf = short_conv(g_c * h, conv_buf, w_conv)
    conv_out = g_b * conv_y
    mixed = jnp.einsum('bse,ed->bsd', jnp.concatenate([attn, conv_out], axis=-1), w_out)
    return mixed, k, v, k_idx, new_buf


def moe(x, w_router, b_router, w_gu, b_gu, w_down, b_down):
    B, S, D = x.shape
    t = x.reshape(B * S, D)
    logits = (t @ w_router + b_router).astype(jnp.float32)
    vals, idx = lax.top_k(logits, TOP_K)
    gates = jax.nn.softmax(vals, axis=-1)
    comb = jnp.sum(jax.nn.one_hot(idx, N_EXPERTS, dtype=jnp.float32) * gates[..., None], axis=1)
    comb = comb.astype(t.dtype)
    out = jnp.zeros_like(t)
    for e in range(N_EXPERTS):
        gu = t @ w_gu[e] + b_gu[e]
        gate = jnp.minimum(gu[:, ::2], SWIGLU_LIMIT)
        up = jnp.clip(gu[:, 1::2], -SWIGLU_LIMIT, SWIGLU_LIMIT)
        act = (up + 1.0) * (gate * jax.nn.sigmoid(SWIGLU_ALPHA * gate))
        out = out + comb[:, e:e + 1] * (act @ w_down[e] + b_down[e])
    return out.reshape(B, S, D)


def layer(x, past_k, past_v, past_kidx, conv_buf, rel_table, w_in, w_conv, w_out,
          ln1_g, ln1_b, w_router, b_router, w_gu, b_gu, w_down, b_down, ln2_g, ln2_b):
    mixed, k, v, k_idx, new_buf = token_mixer(x, past_k, past_v, past_kidx, conv_buf, rel_table,
                                              w_in, w_conv, w_out)
    h = layer_norm(DN_ALPHA * x + mixed, ln1_g, ln1_b)
    y = layer_norm(DN_ALPHA * h + moe(h, w_router, b_router, w_gu, b_gu, w_down, b_down), ln2_g, ln2_b)
    return y, k, v, k_idx, new_buf


def setup_inputs(seed: int = 0) -> dict:
    key = jax.random.key(seed)
    ks = jax.random.split(key, 24)
    f32 = jnp.float32
    nrm = lambda k, shape, s: jax.random.normal(k, shape, f32) * s
    return {
        "x_prompt": nrm(ks[0], (BATCH, SEQ, D_MODEL), 1.0),
        "x_sample": nrm(ks[1], (DEC_BATCH, DEC_SEQ, D_MODEL), 1.0),
        "cache_k": nrm(ks[2], (DEPTH, DEC_BATCH, PAST_LEN, N_KV_HEADS, HEAD_DIM), 1.0),
        "cache_v": nrm(ks[3], (DEPTH, DEC_BATCH, PAST_LEN, N_KV_HEADS, HEAD_DIM), 1.0),
        "cache_kidx": nrm(ks[4], (DEPTH, DEC_BATCH, PAST_LEN, IDX_DIM), 1.0),
        "state_conv": nrm(ks[5], (DEPTH, DEC_BATCH, CONV_K - 1, CONV_WIDTH), 1.0),
        "rel_bias": nrm(ks[6], (N_BUCKETS, N_HEADS), 0.5),
        "w_in": nrm(ks[7], (DEPTH, D_MODEL, IN_WIDTH), D_MODEL ** -0.5),
        "w_conv": nrm(ks[8], (DEPTH, CONV_K, CONV_WIDTH), CONV_K ** -0.5),
        "w_out": nrm(ks[9], (DEPTH, D_MODEL, D_MODEL), DN_BETA * D_MODEL ** -0.5),
        "ln1_g": 1.0 + nrm(ks[10], (DEPTH, D_MODEL), 0.02),
        "ln1_b": nrm(ks[11], (DEPTH, D_MODEL), 0.02),
        "w_router": nrm(ks[12], (DEPTH, D_MODEL, N_EXPERTS), D_MODEL ** -0.5),
        "b_router": nrm(ks[13], (DEPTH, N_EXPERTS), 0.01),
        "w_gu": nrm(ks[14], (DEPTH, N_EXPERTS, D_MODEL, 2 * D_FF), D_MODEL ** -0.5),
        "b_gu": nrm(ks[15], (DEPTH, N_EXPERTS, 2 * D_FF), 0.01),
        "w_down": nrm(ks[16], (DEPTH, N_EXPERTS, D_FF, D_MODEL), DN_BETA * D_FF ** -0.5),
        "b_down": nrm(ks[17], (DEPTH, N_EXPERTS, D_MODEL), 0.01),
        "ln2_g": 1.0 + nrm(ks[18], (DEPTH, D_MODEL), 0.02),
        "ln2_b": nrm(ks[19], (DEPTH, D_MODEL), 0.02),
    }


def reference(x_prompt, x_sample, cache_k, cache_v, cache_kidx, state_conv, rel_bias,
              w_in, w_conv, w_out, ln1_g, ln1_b, w_router, b_router, w_gu, b_gu,
              w_down, b_down, ln2_g, ln2_b):
    yp, ys = x_prompt, x_sample
    kp, vp, ip, cp, kq, vq, iq, cq = [], [], [], [], [], [], [], []
    for l in range(DEPTH):
        lw = (w_in[l], w_conv[l], w_out[l], ln1_g[l], ln1_b[l], w_router[l], b_router[l],
              w_gu[l], b_gu[l], w_down[l], b_down[l], ln2_g[l], ln2_b[l])
        yp, k1, v1, i1, c1 = layer(yp, None, None, None, None, rel_bias, *lw)
        ys, k2, v2, i2, c2 = layer(ys, cache_k[l], cache_v[l], cache_kidx[l], state_conv[l],
                                   rel_bias, *lw)
        kp.append(k1); vp.append(v1); ip.append(i1); cp.append(c1)
        kq.append(k2); vq.append(v2); iq.append(i2); cq.append(c2)
    new_k_prompt = jnp.stack(kp)
    new_v_prompt = jnp.stack(vp)
    new_kidx_prompt = jnp.stack(ip)
    new_conv_prompt = jnp.stack(cp)
    new_k_sample = jnp.stack(kq)
    new_v_sample = jnp.stack(vq)
    new_kidx_sample = jnp.stack(iq)
    new_conv_sample = jnp.stack(cq)
    return (yp, ys, new_k_prompt, new_v_prompt, new_kidx_prompt, new_conv_prompt,
            new_k_sample, new_v_sample, new_kidx_sample, new_conv_sample)
```

```python
import functools
import math

import jax
import jax.numpy as jnp
from jax import lax
from jax.experimental import pallas as pl
from jax.experimental.pallas import tpu as pltpu

F32 = jnp.float32
BF16 = jnp.bfloat16
I32 = jnp.int32

CHUNK = 64
N_HEADS = 8
HEAD_DIM = 64
N_KV_HEADS = 2
GROUP = N_HEADS // N_KV_HEADS
N_IDX_HEADS = 8
IDX_DIM = 32
TOPK_MAX = 256
CONV_K = 3
N_BUCKETS = 32
MAX_DISTANCE = 128
N_EXPERTS = 32
TOP_K = 4
SWIGLU_LIMIT = 7.0
SWIGLU_ALPHA = 1.702
LN_EPS = 1e-5
MASK_VALUE = -1e30

LANES = 128
KEY_CHUNK = 512
INT_MIN = -(2 ** 31)
VMEM_LIMIT = 56 * 1024 * 1024


def _cparams(sem):
    return pltpu.CompilerParams(dimension_semantics=sem, vmem_limit_bytes=VMEM_LIMIT)


def _dot(a, b):
    return jnp.dot(a, b, preferred_element_type=F32)


def _layer_norm(z, g, b):
    mu = jnp.mean(z, axis=-1, keepdims=True)
    zc = z - mu
    var = jnp.mean(zc * zc, axis=-1, keepdims=True)
    return zc * lax.rsqrt(var + LN_EPS) * g + b


def _bucket_thresholds():
    nb = N_BUCKETS // 2
    max_exact = nb // 2
    out = []
    for j in range(1, nb - max_exact):
        out.append(math.ceil(max_exact * (MAX_DISTANCE / max_exact) ** (j / (nb - max_exact)) - 1e-9))
    return tuple(out)


def _bias_table_kernel(rel_ref, tab_ref, *, offsets, qb):
    nb = N_BUCKETS // 2
    max_exact = nb // 2
    thr = _bucket_thresholds()
    ii = lax.broadcasted_iota(I32, (qb, LANES), 0)
    jj = lax.broadcasted_iota(I32, (qb, LANES), 1)
    for d, off in enumerate(offsets):
        rel = off + jj - ii
        n = jnp.abs(rel)
        large = jnp.full((qb, LANES), max_exact, I32)
        for t in thr:
            large = large + jnp.where(n >= t, 1, 0)
        bucket = jnp.where(rel > 0, nb, 0) + jnp.where(n < max_exact, n, large)
        for h in range(N_HEADS):
            acc = jnp.zeros((qb, LANES), F32)
            for b in range(N_BUCKETS):
                acc = jnp.where(bucket == b, rel_ref[b, h], acc)
            tab_ref[d, h] = acc


def _bias_tables(rel_bias, qb, offsets):
    return pl.pallas_call(
        functools.partial(_bias_table_kernel, offsets=offsets, qb=qb),
        out_shape=jax.ShapeDtypeStruct((len(offsets), N_HEADS, qb, LANES), F32),
        in_specs=[pl.BlockSpec(memory_space=pltpu.SMEM)],
        out_specs=pl.BlockSpec(memory_space=pltpu.VMEM),
        name="bias_tables",
    )(rel_bias)


def _proj_kernel(x_ref, wa_ref, ws_ref, wc_ref, wconv_ref, buf_ref,
                 k_ref, v_ref, small_ref, q_ref, qi_ref, conv_ref, nbuf_ref, *rest, ts, emit_keys):
    if emit_keys:
        kT_ref, kidxT_ref, vb_ref, carry_ref = rest
    else:
        (carry_ref,) = rest
    s = pl.program_id(1)
    xb = x_ref[0].astype(BF16)
    pa = _dot(xb, wa_ref[...])
    ps = _dot(xb, ws_ref[...])
    pc = _dot(xb, wc_ref[...])
    aw = N_HEADS * HEAD_DIM
    iw = N_IDX_HEADS * IDX_DIM
    kv = N_KV_HEADS * HEAD_DIM
    k = pa[:, aw + iw:aw + iw + kv]
    v = pa[:, aw + iw + kv:aw + iw + 2 * kv]
    k_ref[0] = k
    v_ref[0] = v
    small_ref[0] = ps
    if emit_keys:
        kT_ref[0, 0] = k.T.astype(BF16)
        kidxT_ref[0, 0] = ps.T[0:IDX_DIM].astype(BF16)
        for n in range(N_KV_HEADS):
            vb_ref[0, 0, n] = v[:, n * HEAD_DIM:(n + 1) * HEAD_DIM].astype(BF16)
    scale = HEAD_DIM ** -0.5
    for h in range(N_HEADS):
        q_ref[0, h] = (pa[:, h * HEAD_DIM:(h + 1) * HEAD_DIM] * scale).astype(BF16)
    for h in range(N_IDX_HEADS):
        qi_ref[0, h] = pa[:, aw + h * IDX_DIM:aw + (h + 1) * IDX_DIM].astype(BF16)

    cw = pc.shape[1] // 3
    u = pc[:, cw:2 * cw] * pc[:, 2 * cw:3 * cw]

    @pl.when(s == 0)
    def _():
        carry_ref[6:8, :] = buf_ref[0]

    carry_ref[8:8 + ts, :] = u
    y = (carry_ref[6:6 + ts, :] * wconv_ref[0:1, :]
         + carry_ref[7:7 + ts, :] * wconv_ref[1:2, :]
         + u * wconv_ref[2:3, :])
    conv_ref[0] = (pc[:, 0:cw] * y).astype(BF16)
    nb = carry_ref[ts + 6:ts + 8, :]
    nbuf_ref[0] = nb
    carry_ref[6:8, :] = nb


def _project(x, conv_buf, wa, ws, wc, wconv, *, emit_keys):
    B, S, D = x.shape
    ts = min(KEY_CHUNK, S)
    assert S % ts == 0 and S >= CONV_K - 1
    ns = S // ts
    cw = wc.shape[1] // 3
    kv = N_KV_HEADS * HEAD_DIM
    out_shape = [
        jax.ShapeDtypeStruct((B, S, kv), F32),
        jax.ShapeDtypeStruct((B, S, kv), F32),
        jax.ShapeDtypeStruct((B, S, LANES), F32),
        jax.ShapeDtypeStruct((B, N_HEADS, S, HEAD_DIM), BF16),
        jax.ShapeDtypeStruct((B, N_IDX_HEADS, S, IDX_DIM), BF16),
        jax.ShapeDtypeStruct((B, S, cw), BF16),
        jax.ShapeDtypeStruct((B, CONV_K - 1, cw), F32),
    ]
    out_specs = [
        pl.BlockSpec((1, ts, kv), lambda b, s: (b, s, 0)),
        pl.BlockSpec((1, ts, kv), lambda b, s: (b, s, 0)),
        pl.BlockSpec((1, ts, LANES), lambda b, s: (b, s, 0)),
        pl.BlockSpec((1, N_HEADS, ts, HEAD_DIM), lambda b, s: (b, 0, s, 0)),
        pl.BlockSpec((1, N_IDX_HEADS, ts, IDX_DIM), lambda b, s: (b, 0, s, 0)),
        pl.BlockSpec((1, ts, cw), lambda b, s: (b, s, 0)),
        pl.BlockSpec((1, CONV_K - 1, cw), lambda b, s: (b, 0, 0)),
    ]
    if emit_keys:
        out_shape += [
            jax.ShapeDtypeStruct((B, ns, kv, ts), BF16),
            jax.ShapeDtypeStruct((B, ns, IDX_DIM, ts), BF16),
            jax.ShapeDtypeStruct((B, ns, N_KV_HEADS, ts, HEAD_DIM), BF16),
        ]
        out_specs += [
            pl.BlockSpec((1, 1, kv, ts), lambda b, s: (b, s, 0, 0)),
            pl.BlockSpec((1, 1, IDX_DIM, ts), lambda b, s: (b, s, 0, 0)),
            pl.BlockSpec((1, 1, N_KV_HEADS, ts, HEAD_DIM), lambda b, s: (b, s, 0, 0, 0)),
        ]
    const2 = lambda b, s: (0, 0)
    in_specs = [
        pl.BlockSpec((1, ts, D), lambda b, s: (b, s, 0)),
        pl.BlockSpec(wa.shape, const2),
        pl.BlockSpec(ws.shape, const2),
        pl.BlockSpec(wc.shape, const2),
        pl.BlockSpec(wconv.shape, const2),
        pl.BlockSpec((1, CONV_K - 1, cw), lambda b, s: (b, 0, 0)),
    ]
    return pl.pallas_call(
        functools.partial(_proj_kernel, ts=ts, emit_keys=emit_keys),
        grid=(B, ns),
        in_specs=in_specs,
        out_specs=tuple(out_specs),
        out_shape=tuple(out_shape),
        scratch_shapes=[pltpu.VMEM((ts + 8, cw), F32)],
        compiler_params=_cparams(("arbitrary", "arbitrary")),
        name="proj_conv",
    )(x, wa, ws, wc, wconv, conv_buf)


def _attn_kernel(bconst_ref, q_ref, qi_ref, small_ref, kidxT_ref, kT_ref, vb_ref, tab_ref,
                 o_ref, skey_ref, jb_ref, m_ref, l_ref, acc_ref,
                 *, qb, kc, past, l_true, topk, offsets):
    i = pl.program_id(1)
    qoff = past + i * qb
    adm_end = ((qoff + qb - 1) // CHUNK + 1) * CHUNK
    nck = (jnp.minimum(adm_end, l_true) + kc - 1) // kc
    idx_bits = int(l_true).bit_length()

    row = lax.broadcasted_iota(I32, (qb, kc), 0)
    lane = lax.broadcasted_iota(I32, (qb, kc), 1)
    q_chunk = (qoff + row) // CHUNK

    w_all = small_ref[0]

    def score_body(c, carry):
        kidx_t = kidxT_ref[0, c]
        sc = jnp.zeros((qb, kc), F32)
        for h in range(N_IDX_HEADS):
            d = _dot(qi_ref[0, h], kidx_t)
            sc = sc + w_all[:, IDX_DIM + h:IDX_DIM + h + 1] * jnp.maximum(d, 0.0)
        sc = sc + 0.0
        bits = pltpu.bitcast(sc, I32)
        skey = jnp.where(bits < 0, bits ^ jnp.int32(0x7FFFFFFF), bits)
        kpos = c * kc + lane
        adm = jnp.logical_and(kpos // CHUNK <= q_chunk, kpos < l_true)
        skey_ref[c] = jnp.where(adm, skey, jnp.int32(INT_MIN))
        return carry

    lax.fori_loop(0, nck, score_body, 0)

    def lane_fold(x):
        out = x[:, 0:LANES]
        for t in range(1, kc // LANES):
            out = out + x[:, t * LANES:(t + 1) * LANES]
        return out

    def count(pred_fn):
        def body(c, part):
            return part + lane_fold(jnp.where(pred_fn(c, skey_ref[c]), 1, 0))
        part = lax.fori_loop(0, nck, body, jnp.zeros((qb, LANES), I32))
        return jnp.sum(part, axis=1, keepdims=True)

    def thr_body(it, t_u):
        cand_u = t_u | (jnp.int32(1) << (31 - it))
        cand_s = cand_u ^ jnp.int32(INT_MIN)
        cnt = count(lambda c, sk: sk >= cand_s)
        return jnp.where(cnt >= topk, cand_u, t_u)

    t_u = lax.fori_loop(0, 32, thr_body, jnp.zeros((qb, 1), I32))
    thr = t_u ^ jnp.int32(INT_MIN)
    cnt_ge = count(lambda c, sk: sk >= thr)
    cnt_gt = count(lambda c, sk: sk > thr)

    jb_ref[...] = jnp.full((qb, LANES), 2 ** 31 - 1, I32)

    @pl.when(jnp.max(cnt_ge) > topk)
    def _():
        need = topk - cnt_gt

        def tie_body(it, jb):
            cand = jb | (jnp.int32(1) << (idx_bits - 1 - it))
            cnt = count(lambda c, sk: jnp.logical_and(sk == thr, c * kc + lane < cand))
            return jnp.where(cnt <= need, cand, jb)

        jb = lax.fori_loop(0, idx_bits, tie_body, jnp.zeros((qb, 1), I32))
        jb_ref[...] = jnp.broadcast_to(jb, (qb, LANES))

    jbound = jb_ref[:, 0:1]

    m_ref[...] = jnp.full(m_ref.shape, MASK_VALUE, F32)
    l_ref[...] = jnp.zeros(l_ref.shape, F32)
    acc_ref[...] = jnp.zeros(acc_ref.shape, F32)

    def attn_body(c, carry):
        sk = skey_ref[c]
        kpos = c * kc + lane
        sel = jnp.logical_or(sk > thr, jnp.logical_and(sk == thr, kpos < jbound))
        sel = jnp.logical_and(sel, sk != jnp.int32(INT_MIN))
        for h in range(N_HEADS):
            n = h // GROUP
            tiles = []
            for t in range(kc // LANES):
                d_t = c * kc + t * LANES - qoff
                tile = jnp.full((qb, LANES), bconst_ref[h], F32)
                for di, off in enumerate(offsets):
                    tile = jnp.where(d_t == off, tab_ref[di, h], tile)
                tiles.append(tile)
            bias = jnp.concatenate(tiles, axis=1)
            s = _dot(q_ref[0, h], kT_ref[0, c, n * HEAD_DIM:(n + 1) * HEAD_DIM, :])
            s = jnp.where(sel, s + bias, MASK_VALUE)
            m_old = m_ref[h]
            m_new = jnp.maximum(m_old, jnp.max(s, axis=1, keepdims=True))
            alpha = jnp.exp(m_old - m_new)
            p = jnp.exp(s - m_new)
            l_ref[h] = alpha * l_ref[h] + jnp.sum(p, axis=1, keepdims=True)
            acc_ref[h] = alpha * acc_ref[h] + _dot(p.astype(BF16), vb_ref[0, c, n])
            m_ref[h] = m_new
        return carry

    lax.fori_loop(0, nck, attn_body, 0)
    for h in range(N_HEADS):
        o_ref[0, :, h * HEAD_DIM:(h + 1) * HEAD_DIM] = (acc_ref[h] / l_ref[h]).astype(o_ref.dtype)


def _attend(q, qi, small, kidxT, kT, vb, tabs, bconst, *, qb, past, l_true, offsets):
    B, _, S, _ = q.shape
    nc, kc = kT.shape[1], kT.shape[3]
    assert S % qb == 0 and qb % CHUNK == 0 or qb == S
    assert past % LANES == 0 and (qb % LANES == 0 or S == qb)
    topk = min(TOPK_MAX, l_true // 4)
    kern = functools.partial(_attn_kernel, qb=qb, kc=kc, past=past, l_true=l_true,
                             topk=topk, offsets=offsets)
    in_specs = [
        pl.BlockSpec(memory_space=pltpu.SMEM),
        pl.BlockSpec((1, N_HEADS, qb, HEAD_DIM), lambda b, i: (b, 0, i, 0)),
        pl.BlockSpec((1, N_IDX_HEADS, qb, IDX_DIM), lambda b, i: (b, 0, i, 0)),
        pl.BlockSpec((1, qb, LANES), lambda b, i: (b, i, 0)),
        pl.BlockSpec((1,) + kidxT.shape[1:], lambda b, i: (b, 0, 0, 0)),
        pl.BlockSpec((1,) + kT.shape[1:], lambda b, i: (b, 0, 0, 0)),
        pl.BlockSpec((1,) + vb.shape[1:], lambda b, i: (b, 0, 0, 0, 0)),
        pl.BlockSpec(tabs.shape, lambda b, i: (0, 0, 0, 0)),
    ]
    return pl.pallas_call(
        kern,
        grid=(B, S // qb),
        in_specs=in_specs,
        out_specs=pl.BlockSpec((1, qb, N_HEADS * HEAD_DIM), lambda b, i: (b, i, 0)),
        out_shape=jax.ShapeDtypeStruct((B, S, N_HEADS * HEAD_DIM), BF16),
        scratch_shapes=[
            pltpu.VMEM((nc, qb, kc), I32),
            pltpu.VMEM((qb, LANES), I32),
            pltpu.VMEM((N_HEADS, qb, 1), F32),
            pltpu.VMEM((N_HEADS, qb, 1), F32),
            pltpu.VMEM((N_HEADS, qb, HEAD_DIM), F32),
        ],
        compiler_params=_cparams(("arbitrary", "arbitrary")),
        name="dsa_attend",
    )(bconst, q, qi, small, kidxT, kT, vb, tabs)


def _mix_kernel(x_ref, attn_ref, conv_ref, wo_ref, g_ref, b_ref, h_ref, *, dn_alpha):
    cat = jnp.concatenate([attn_ref[...], conv_ref[...]], axis=1)
    mixed = _dot(cat, wo_ref[...])
    h_ref[...] = _layer_norm(dn_alpha * x_ref[...] + mixed, g_ref[...], b_ref[...])


def _mix(x2, attn2, conv2, wo, g, b, *, dn_alpha):
    T, D = x2.shape
    tt = min(1024, T)
    assert T % tt == 0
    aw, cw = attn2.shape[1], conv2.shape[1]
    row = lambda i: (i, 0)
    const = lambda i: (0, 0)
    return pl.pallas_call(
        functools.partial(_mix_kernel, dn_alpha=dn_alpha),
        grid=(T // tt,),
        in_specs=[pl.BlockSpec((tt, D), row), pl.BlockSpec((tt, aw), row), pl.BlockSpec((tt, cw), row),
                  pl.BlockSpec(wo.shape, const), pl.BlockSpec((1, D), const), pl.BlockSpec((1, D), const)],
        out_specs=pl.BlockSpec((tt, D), row),
        out_shape=jax.ShapeDtypeStruct((T, D), F32),
        compiler_params=_cparams(("arbitrary",)),
        name="outproj_ln1",
    )(x2, attn2, conv2, wo, g, b)


def _route(logits):
    lane = lax.broadcasted_iota(I32, logits.shape, 1)
    work = logits
    vals, hots = [], []
    for _ in range(TOP_K):
        m = jnp.max(work, axis=1, keepdims=True)
        idx = jnp.min(jnp.where(work == m, lane, LANES), axis=1, keepdims=True)
        hot = lane == idx
        vals.append(m)
        hots.append(hot)
        work = jnp.where(hot, -jnp.inf, work)
    es = [jnp.exp(v - vals[0]) for v in vals]
    den = es[0]
    for e in es[1:]:
        den = den + e
    comb = jnp.zeros(logits.shape, F32)
    for e, hot in zip(es, hots):
        comb = jnp.where(hot, e / den, comb)
    return comb


def _moe_kernel(h_ref, wr_ref, br_ref, wg_ref, wu_ref, bg_ref, bu_ref, wd_ref, bd_ref, g_ref, b_ref,
                y_ref, hb_ref, comb_ref, acc_ref, *, dn_alpha):
    e = pl.program_id(1)

    @pl.when(e == 0)
    def _():
        hb = h_ref[...].astype(BF16)
        hb_ref[...] = hb
        comb_ref[...] = _route(_dot(hb, wr_ref[...]) + br_ref[...])
        acc_ref[...] = jnp.zeros(acc_ref.shape, F32)

    hb = hb_ref[...]
    gate = jnp.minimum(_dot(hb, wg_ref[0]) + bg_ref[0], SWIGLU_LIMIT)
    up = jnp.clip(_dot(hb, wu_ref[0]) + bu_ref[0], -SWIGLU_LIMIT, SWIGLU_LIMIT)
    act = (up + 1.0) * (gate * (1.0 / (1.0 + jnp.exp(-SWIGLU_ALPHA * gate))))
    y = _dot(act.astype(BF16), wd_ref[0]) + bd_ref[0]
    comb = comb_ref[...]
    lane = lax.broadcasted_iota(I32, comb.shape, 1)
    ce = jnp.sum(jnp.where(lane == e, comb, 0.0), axis=1, keepdims=True)
    acc_ref[...] += ce * y

    @pl.when(e == pl.num_programs(1) - 1)
    def _():
        y_ref[...] = _layer_norm(dn_alpha * h_ref[...] + acc_ref[...], g_ref[...], b_ref[...])


def _moe(h2, wr, br, wg, wu, bg, bu, wd, bd, g, b, *, dn_alpha):
    T, D = h2.shape
    E, _, F = wg.shape
    tt = min(512, T)
    assert T % tt == 0
    row = lambda i, e: (i, 0)
    const = lambda i, e: (0, 0)
    exp3 = lambda i, e: (e, 0, 0)
    return pl.pallas_call(
        functools.partial(_moe_kernel, dn_alpha=dn_alpha),
        grid=(T // tt, E),
        in_specs=[pl.BlockSpec((tt, D), row), pl.BlockSpec(wr.shape, const), pl.BlockSpec(br.shape, const),
                  pl.BlockSpec((1, D, F), exp3), pl.BlockSpec((1, D, F), exp3),
                  pl.BlockSpec((1, 1, F), exp3), pl.BlockSpec((1, 1, F), exp3),
                  pl.BlockSpec((1, F, D), exp3), pl.BlockSpec((1, 1, D), exp3),
                  pl.BlockSpec((1, D), const), pl.BlockSpec((1, D), const)],
        out_specs=pl.BlockSpec((tt, D), row),
        out_shape=jax.ShapeDtypeStruct((T, D), F32),
        scratch_shapes=[pltpu.VMEM((tt, D), BF16), pltpu.VMEM((tt, LANES), F32), pltpu.VMEM((tt, D), F32)],
        compiler_params=_cparams(("arbitrary", "arbitrary")),
        name="moe_ln2",
    )(h2, wr, br, wg, wu, bg, bu, wd, bd, g, b)


def _prep_weights(w_in, w_conv, w_out, ln1_g, ln1_b, w_router, b_router, w_gu, b_gu, w_down, b_down,
                  ln2_g, ln2_b):
    aw = N_HEADS * HEAD_DIM
    kv = N_KV_HEADS * HEAD_DIM
    iw = N_IDX_HEADS * IDX_DIM
    D = w_in.shape[0]
    cw = D - aw
    o_q, o_k, o_v = 0, aw, aw + kv
    o_qi = o_v + kv
    o_ki = o_qi + iw
    o_wi = o_ki + IDX_DIM
    o_c = o_wi + N_IDX_HEADS
    wa = jnp.concatenate([w_in[:, o_q:o_k], w_in[:, o_qi:o_ki], w_in[:, o_k:o_v], w_in[:, o_v:o_qi]],
                         axis=1).astype(BF16)
    ws = jnp.pad(w_in[:, o_ki:o_c], ((0, 0), (0, LANES - IDX_DIM - N_IDX_HEADS))).astype(BF16)
    wc = w_in[:, o_c:o_c + 3 * cw].astype(BF16)
    E = w_router.shape[1]
    wr = jnp.pad(w_router, ((0, 0), (0, LANES - E))).astype(BF16)
    br = jnp.pad(b_router, (0, LANES - E), constant_values=MASK_VALUE).reshape(1, LANES)
    F = w_gu.shape[2] // 2
    return dict(
        wa=wa, ws=ws, wc=wc, wconv=w_conv, wo=w_out.astype(BF16),
        ln1_g=ln1_g.reshape(1, D), ln1_b=ln1_b.reshape(1, D),
        wr=wr, br=br,
        wg=w_gu[:, :, 0::2].astype(BF16), wu=w_gu[:, :, 1::2].astype(BF16),
        bg=b_gu[:, 0::2].reshape(E, 1, F), bu=b_gu[:, 1::2].reshape(E, 1, F),
        wd=w_down.astype(BF16), bd=b_down.reshape(E, 1, D),
        ln2_g=ln2_g.reshape(1, D), ln2_b=ln2_b.reshape(1, D),
    )


def _chunked_keys(k_all, v_all, kidx_all, kc):
    B, L, kv = k_all.shape
    lp = -(-L // kc) * kc
    pad = lambda a: jnp.pad(a, ((0, 0), (0, lp - L), (0, 0)))
    nc = lp // kc
    kT = pad(k_all).reshape(B, nc, kc, kv).transpose(0, 1, 3, 2).astype(BF16)
    kidxT = pad(kidx_all).reshape(B, nc, kc, IDX_DIM).transpose(0, 1, 3, 2).astype(BF16)
    vb = pad(v_all).reshape(B, nc, kc, N_KV_HEADS, HEAD_DIM).transpose(0, 1, 3, 2, 4).astype(BF16)
    return kT, kidxT, vb


def _layer(x, past_k, past_v, past_kidx, conv_buf, rel_bias, w, *, dn_alpha):
    B, S, D = x.shape
    outs = _project(x, conv_buf, w["wa"], w["ws"], w["wc"], w["wconv"], emit_keys=past_k is None)
    k, v, small, q, qi, conv, new_buf = outs[:7]
    k_idx = small[:, :, 0:IDX_DIM]
    if past_k is None:
        past, l_true = 0, S
        kT, kidxT, vb = outs[7:]
    else:
        past = past_k.shape[1]
        l_true = past + S
        kT, kidxT, vb = _chunked_keys(jnp.concatenate([past_k, k], axis=1),
                                      jnp.concatenate([past_v, v], axis=1),
                                      jnp.concatenate([past_kidx, k_idx], axis=1), KEY_CHUNK)
    qb = min(LANES, S)
    offsets = (-LANES, 0)
    tabs = _bias_tables(rel_bias, qb, offsets)
    bconst = rel_bias[N_BUCKETS // 2 - 1]
    attn = _attend(q, qi, small, kidxT, kT, vb, tabs, bconst, qb=qb, past=past, l_true=l_true, offsets=offsets)
    h = _mix(x.reshape(B * S, D), attn.reshape(B * S, -1), conv.reshape(B * S, -1), w["wo"],
             w["ln1_g"], w["ln1_b"], dn_alpha=dn_alpha)
    y = _moe(h, w["wr"], w["br"], w["wg"], w["wu"], w["bg"], w["bu"], w["wd"], w["bd"],
             w["ln2_g"], w["ln2_b"], dn_alpha=dn_alpha)
    return (y.reshape(B, S, D), k.reshape(B, S, N_KV_HEADS, HEAD_DIM), v.reshape(B, S, N_KV_HEADS, HEAD_DIM),
            k_idx, new_buf)


def kernel(x_prompt, x_sample, cache_k, cache_v, cache_kidx, state_conv, rel_bias, w_in, w_conv, w_out,
           ln1_g, ln1_b, w_router, b_router, w_gu, b_gu, w_down, b_down, ln2_g, ln2_b):
    depth = w_in.shape[0]
    assert depth == 1
    dn_alpha = (2 * depth) ** 0.25
    kv = N_KV_HEADS * HEAD_DIM
    w = _prep_weights(w_in[0], w_conv[0], w_out[0], ln1_g[0], ln1_b[0], w_router[0], b_router[0],
                      w_gu[0], b_gu[0], w_down[0], b_down[0], ln2_g[0], ln2_b[0])
    Bp = x_prompt.shape[0]
    cw = w_conv.shape[2]
    zero_buf = jnp.zeros((Bp, CONV_K - 1, cw), F32)
    yp, k1, v1, i1, c1 = _layer(x_prompt, None, None, None, zero_buf, rel_bias, w, dn_alpha=dn_alpha)
    Bs, P = cache_k.shape[1], cache_k.shape[2]
    ys, k2, v2, i2, c2 = _layer(x_sample, cache_k[0].reshape(Bs, P, kv), cache_v[0].reshape(Bs, P, kv),
                                cache_kidx[0], state_conv[0], rel_bias, w, dn_alpha=dn_alpha)
    return (yp, ys, k1[None], v1[None], i1[None], c1[None], k2[None], v2[None], i2[None], c2[None])
```

```python
import functools
import math

import jax
import jax.numpy as jnp
from jax import lax
from jax.experimental import pallas as pl
from jax.experimental.pallas import tpu as pltpu

F32 = jnp.float32
BF16 = jnp.bfloat16
I32 = jnp.int32

CHUNK = 64
N_HEADS = 8
HEAD_DIM = 64
N_KV_HEADS = 2
GROUP = N_HEADS // N_KV_HEADS
N_IDX_HEADS = 8
IDX_DIM = 32
TOPK_MAX = 256
CONV_K = 3
N_BUCKETS = 32
MAX_DISTANCE = 128
N_EXPERTS = 32
TOP_K = 4
SWIGLU_LIMIT = 7.0
SWIGLU_ALPHA = 1.702
LN_EPS = 1e-5
MASK_VALUE = -1e30

LANES = 128
KEY_CHUNK = 512
INT_MIN = -(2 ** 31)
VMEM_LIMIT = 56 * 1024 * 1024


def _cparams(sem):
    return pltpu.CompilerParams(dimension_semantics=sem, vmem_limit_bytes=VMEM_LIMIT)


def _dot(a, b):
    return jnp.dot(a, b, preferred_element_type=F32)


def _layer_norm(z, g, b):
    mu = jnp.mean(z, axis=-1, keepdims=True)
    zc = z - mu
    var = jnp.mean(zc * zc, axis=-1, keepdims=True)
    return zc * lax.rsqrt(var + LN_EPS) * g + b


def _bucket_thresholds():
    nb = N_BUCKETS // 2
    max_exact = nb // 2
    out = []
    for j in range(1, nb - max_exact):
        out.append(math.ceil(max_exact * (MAX_DISTANCE / max_exact) ** (j / (nb - max_exact)) - 1e-9))
    return tuple(out)


def _bias_table_kernel(rel_ref, tab_ref, *, offsets, qb):
    nb = N_BUCKETS // 2
    max_exact = nb // 2
    thr = _bucket_thresholds()
    ii = lax.broadcasted_iota(I32, (qb, LANES), 0)
    jj = lax.broadcasted_iota(I32, (qb, LANES), 1)
    for d, off in enumerate(offsets):
        rel = off + jj - ii
        n = jnp.abs(rel)
        large = jnp.full((qb, LANES), max_exact, I32)
        for t in thr:
            large = large + jnp.where(n >= t, 1, 0)
        bucket = jnp.where(rel > 0, nb, 0) + jnp.where(n < max_exact, n, large)
        for h in range(N_HEADS):
            acc = jnp.zeros((qb, LANES), F32)
            for b in range(N_BUCKETS):
                acc = jnp.where(bucket == b, rel_ref[b, h], acc)
            tab_ref[d, h] = acc


def _bias_tables(rel_bias, qb, offsets):
    return pl.pallas_call(
        functools.partial(_bias_table_kernel, offsets=offsets, qb=qb),
        out_shape=jax.ShapeDtypeStruct((len(offsets), N_HEADS, qb, LANES), F32),
        in_specs=[pl.BlockSpec(memory_space=pltpu.SMEM)],
        out_specs=pl.BlockSpec(memory_space=pltpu.VMEM),
        name="bias_tables",
    )(rel_bias)


def _proj_kernel(x_ref, wa_ref, ws_ref, wc_ref, wconv_ref, buf_ref,
                 k_ref, v_ref, small_ref, q_ref, qi_ref, conv_ref, nbuf_ref, *rest, ts, emit_keys):
    if emit_keys:
        kT_ref, kidxT_ref, vb_ref, carry_ref = rest
    else:
        (carry_ref,) = rest
    s = pl.program_id(1)
    xb = x_ref[0].astype(BF16)
    pa = _dot(xb, wa_ref[...])
    ps = _dot(xb, ws_ref[...])
    pc = _dot(xb, wc_ref[...])
    aw = N_HEADS * HEAD_DIM
    iw = N_IDX_HEADS * IDX_DIM
    kv = N_KV_HEADS * HEAD_DIM
    k = pa[:, aw + iw:aw + iw + kv]
    v = pa[:, aw + iw + kv:aw + iw + 2 * kv]
    k_ref[0] = k
    v_ref[0] = v
    small_ref[0] = ps
    if emit_keys:
        kT_ref[0, 0] = k.T.astype(BF16)
        kidxT_ref[0, 0] = ps.T[0:IDX_DIM].astype(BF16)
        for n in range(N_KV_HEADS):
            vb_ref[0, 0, n] = v[:, n * HEAD_DIM:(n + 1) * HEAD_DIM].astype(BF16)
    scale = HEAD_DIM ** -0.5
    for h in range(N_HEADS):
        q_ref[0, h] = (pa[:, h * HEAD_DIM:(h + 1) * HEAD_DIM] * scale).astype(BF16)
    for h in range(N_IDX_HEADS):
        qi_ref[0, h] = pa[:, aw + h * IDX_DIM:aw + (h + 1) * IDX_DIM].astype(BF16)

    cw = pc.shape[1] // 3
    u = pc[:, cw:2 * cw] * pc[:, 2 * cw:3 * cw]

    @pl.when(s == 0)
    def _():
        carry_ref[6:8, :] = buf_ref[0]

    carry_ref[8:8 + ts, :] = u
    y = (carry_ref[6:6 + ts, :] * wconv_ref[0:1, :]
         + carry_ref[7:7 + ts, :] * wconv_ref[1:2, :]
         + u * wconv_ref[2:3, :])
    conv_ref[0] = (pc[:, 0:cw] * y).astype(BF16)
    nb = carry_ref[ts + 6:ts + 8, :]
    nbuf_ref[0] = nb
    carry_ref[6:8, :] = nb


def _project(x, conv_buf, wa, ws, wc, wconv, *, emit_keys):
    B, S, D = x.shape
    ts = min(KEY_CHUNK, S)
    assert S % ts == 0 and S >= CONV_K - 1
    ns = S // ts
    cw = wc.shape[1] // 3
    kv = N_KV_HEADS * HEAD_DIM
    out_shape = [
        jax.ShapeDtypeStruct((B, S, kv), F32),
        jax.ShapeDtypeStruct((B, S, kv), F32),
        jax.ShapeDtypeStruct((B, S, LANES), F32),
        jax.ShapeDtypeStruct((B, N_HEADS, S, HEAD_DIM), BF16),
        jax.ShapeDtypeStruct((B, N_IDX_HEADS, S, IDX_DIM), BF16),
        jax.ShapeDtypeStruct((B, S, cw), BF16),
        jax.ShapeDtypeStruct((B, CONV_K - 1, cw), F32),
    ]
    out_specs = [
        pl.BlockSpec((1, ts, kv), lambda b, s: (b, s, 0)),
        pl.BlockSpec((1, ts, kv), lambda b, s: (b, s, 0)),
        pl.BlockSpec((1, ts, LANES), lambda b, s: (b, s, 0)),
        pl.BlockSpec((1, N_HEADS, ts, HEAD_DIM), lambda b, s: (b, 0, s, 0)),
        pl.BlockSpec((1, N_IDX_HEADS, ts, IDX_DIM), lambda b, s: (b, 0, s, 0)),
        pl.BlockSpec((1, ts, cw), lambda b, s: (b, s, 0)),
        pl.BlockSpec((1, CONV_K - 1, cw), lambda b, s: (b, 0, 0)),
    ]
    if emit_keys:
        out_shape += [
            jax.ShapeDtypeStruct((B, ns, kv, ts), BF16),
            jax.ShapeDtypeStruct((B, ns, IDX_DIM, ts), BF16),
            jax.ShapeDtypeStruct((B, ns, N_KV_HEADS, ts, HEAD_DIM), BF16),
        ]
        out_specs += [
            pl.BlockSpec((1, 1, kv, ts), lambda b, s: (b, s, 0, 0)),
            pl.BlockSpec((1, 1, IDX_DIM, ts), lambda b, s: (b, s, 0, 0)),
            pl.BlockSpec((1, 1, N_KV_HEADS, ts, HEAD_DIM), lambda b, s: (b, s, 0, 0, 0)),
        ]
    const2 = lambda b, s: (0, 0)
    in_specs = [
        pl.BlockSpec((1, ts, D), lambda b, s: (b, s, 0)),
        pl.BlockSpec(wa.shape, const2),
        pl.BlockSpec(ws.shape, const2),
        pl.BlockSpec(wc.shape, const2),
        pl.BlockSpec(wconv.shape, const2),
        pl.BlockSpec((1, CONV_K - 1, cw), lambda b, s: (b, 0, 0)),
    ]
    return pl.pallas_call(
        functools.partial(_proj_kernel, ts=ts, emit_keys=emit_keys),
        grid=(B, ns),
        in_specs=in_specs,
        out_specs=tuple(out_specs),
        out_shape=tuple(out_shape),
        scratch_shapes=[pltpu.VMEM((ts + 8, cw), F32)],
        compiler_params=_cparams(("arbitrary", "arbitrary")),
        name="proj_conv",
    )(x, wa, ws, wc, wconv, conv_buf)


def _attn_kernel(bconst_ref, q_ref, qi_ref, small_ref, kidxT_ref, kT_ref, vb_ref, tab_ref,
                 o_ref, skey_ref, jb_ref, m_ref, l_ref, acc_ref,
                 *, qb, kc, past, l_true, topk, offsets):
    i = pl.program_id(1)
    qoff = past + i * qb
    adm_end = ((qoff + qb - 1) // CHUNK + 1) * CHUNK
    nck = (jnp.minimum(adm_end, l_true) + kc - 1) // kc
    idx_bits = int(l_true).bit_length()

    row = lax.broadcasted_iota(I32, (qb, kc), 0)
    lane = lax.broadcasted_iota(I32, (qb, kc), 1)
    q_chunk = (qoff + row) // CHUNK

    w_all = small_ref[0]

    def score_body(c, carry):
        kidx_t = kidxT_ref[0, c]
        sc = jnp.zeros((qb, kc), F32)
        for h in range(N_IDX_HEADS):
            d = _dot(qi_ref[0, h], kidx_t)
            sc = sc + w_all[:, IDX_DIM + h:IDX_DIM + h + 1] * jnp.maximum(d, 0.0)
        sc = sc + 0.0
        bits = pltpu.bitcast(sc, I32)
        skey = jnp.where(bits < 0, bits ^ jnp.int32(0x7FFFFFFF), bits)
        kpos = c * kc + lane
        adm = jnp.logical_and(kpos // CHUNK <= q_chunk, kpos < l_true)
        skey_ref[c] = jnp.where(adm, skey, jnp.int32(INT_MIN))
        return carry

    lax.fori_loop(0, nck, score_body, 0)

    def lane_fold(x):
        out = x[:, 0:LANES]
        for t in range(1, kc // LANES):
            out = out + x[:, t * LANES:(t + 1) * LANES]
        return out

    def count(pred_fn):
        def body(c, part):
            return part + lane_fold(jnp.where(pred_fn(c, skey_ref[c]), 1, 0))
        part = lax.fori_loop(0, nck, body, jnp.zeros((qb, LANES), I32))
        return jnp.sum(part, axis=1, keepdims=True)

    def thr_body(it, t_u):
        cand_u = t_u | (jnp.int32(1) << (31 - it))
        cand_s = cand_u ^ jnp.int32(INT_MIN)
        cnt = count(lambda c, sk: sk >= cand_s)
        return jnp.where(cnt >= topk, cand_u, t_u)

    t_u = lax.fori_loop(0, 32, thr_body, jnp.zeros((qb, 1), I32))
    thr = t_u ^ jnp.int32(INT_MIN)
    cnt_ge = count(lambda c, sk: sk >= thr)
    cnt_gt = count(lambda c, sk: sk > thr)

    jb_ref[...] = jnp.full((qb, LANES), 2 ** 31 - 1, I32)

    @pl.when(jnp.max(cnt_ge) > topk)
    def _():
        need = topk - cnt_gt

        def tie_body(it, jb):
            cand = jb | (jnp.int32(1) << (idx_bits - 1 - it))
            cnt = count(lambda c, sk: jnp.logical_and(sk == thr, c * kc + lane < cand))
            return jnp.where(cnt <= need, cand, jb)

        jb = lax.fori_loop(0, idx_bits, tie_body, jnp.zeros((qb, 1), I32))
        jb_ref[...] = jnp.broadcast_to(jb, (qb, LANES))

    jbound = jb_ref[:, 0:1]

    m_ref[...] = jnp.full(m_ref.shape, MASK_VALUE, F32)
    l_ref[...] = jnp.zeros(l_ref.shape, F32)
    acc_ref[...] = jnp.zeros(acc_ref.shape, F32)

    def attn_body(c, carry):
        sk = skey_ref[c]
        kpos = c * kc + lane
        sel = jnp.logical_or(sk > thr, jnp.logical_and(sk == thr, kpos < jbound))
        sel = jnp.logical_and(sel, sk != jnp.int32(INT_MIN))
        for h in range(N_HEADS):
            n = h // GROUP
            tiles = []
            for t in range(kc // LANES):
                d_t = c * kc + t * LANES - qoff
                tile = jnp.full((qb, LANES), bconst_ref[h], F32)
                for di, off in enumerate(offsets):
                    tile = jnp.where(d_t == off, tab_ref[di, h], tile)
                tiles.append(tile)
            bias = jnp.concatenate(tiles, axis=1)
            s = _dot(q_ref[0, h], kT_ref[0, c, n * HEAD_DIM:(n + 1) * HEAD_DIM, :])
            s = jnp.where(sel, s + bias, MASK_VALUE)
            m_old = m_ref[h]
            m_new = jnp.maximum(m_old, jnp.max(s, axis=1, keepdims=True))
            alpha = jnp.exp(m_old - m_new)
            p = jnp.exp(s - m_new)
            l_ref[h] = alpha * l_ref[h] + jnp.sum(p, axis=1, keepdims=True)
            acc_ref[h] = alpha * acc_ref[h] + _dot(p.astype(BF16), vb_ref[0, c, n])
            m_ref[h] = m_new
        return carry

    lax.fori_loop(0, nck, attn_body, 0)
    for h in range(N_HEADS):
        o_ref[0, :, h * HEAD_DIM:(h + 1) * HEAD_DIM] = (acc_ref[h] / l_ref[h]).astype(o_ref.dtype)


def _attend(q, qi, small, kidxT, kT, vb, tabs, bconst, *, qb, past, l_true, offsets):
    B, _, S, _ = q.shape
    nc, kc = kT.shape[1], kT.shape[3]
    assert S % qb == 0 and qb % CHUNK == 0 or qb == S
    assert past % LANES == 0 and (qb % LANES == 0 or S == qb)
    topk = min(TOPK_MAX, l_true // 4)
    kern = functools.partial(_attn_kernel, qb=qb, kc=kc, past=past, l_true=l_true,
                             topk=topk, offsets=offsets)
    in_specs = [
        pl.BlockSpec(memory_space=pltpu.SMEM),
        pl.BlockSpec((1, N_HEADS, qb, HEAD_DIM), lambda b, i: (b, 0, i, 0)),
        pl.BlockSpec((1, N_IDX_HEADS, qb, IDX_DIM), lambda b, i: (b, 0, i, 0)),
        pl.BlockSpec((1, qb, LANES), lambda b, i: (b, i, 0)),
        pl.BlockSpec((1,) + kidxT.shape[1:], lambda b, i: (b, 0, 0, 0)),
        pl.BlockSpec((1,) + kT.shape[1:], lambda b, i: (b, 0, 0, 0)),
        pl.BlockSpec((1,) + vb.shape[1:], lambda b, i: (b, 0, 0, 0, 0)),
        pl.BlockSpec(tabs.shape, lambda b, i: (0, 0, 0, 0)),
    ]
    return pl.pallas_call(
        kern,
        grid=(B, S // qb),
        in_specs=in_specs,
        out_specs=pl.BlockSpec((1, qb, N_HEADS * HEAD_DIM), lambda b, i: (b, i, 0)),
        out_shape=jax.ShapeDtypeStruct((B, S, N_HEADS * HEAD_DIM), BF16),
        scratch_shapes=[
            pltpu.VMEM((nc, qb, kc), I32),
            pltpu.VMEM((qb, LANES), I32),
            pltpu.VMEM((N_HEADS, qb, 1), F32),
            pltpu.VMEM((N_HEADS, qb, 1), F32),
            pltpu.VMEM((N_HEADS, qb, HEAD_DIM), F32),
        ],
        compiler_params=_cparams(("arbitrary", "arbitrary")),
        name="dsa_attend",
    )(bconst, q, qi, small, kidxT, kT, vb, tabs)


def _mix_kernel(x_ref, attn_ref, conv_ref, wo_ref, g_ref, b_ref, h_ref, *, dn_alpha):
    cat = jnp.concatenate([attn_ref[...], conv_ref[...]], axis=1)
    mixed = _dot(cat, wo_ref[...])
    h_ref[...] = _layer_norm(dn_alpha * x_ref[...] + mixed, g_ref[...], b_ref[...])


def _mix(x2, attn2, conv2, wo, g, b, *, dn_alpha):
    T, D = x2.shape
    tt = min(1024, T)
    assert T % tt == 0
    aw, cw = attn2.shape[1], conv2.shape[1]
    row = lambda i: (i, 0)
    const = lambda i: (0, 0)
    return pl.pallas_call(
        functools.partial(_mix_kernel, dn_alpha=dn_alpha),
        grid=(T // tt,),
        in_specs=[pl.BlockSpec((tt, D), row), pl.BlockSpec((tt, aw), row), pl.BlockSpec((tt, cw), row),
                  pl.BlockSpec(wo.shape, const), pl.BlockSpec((1, D), const), pl.BlockSpec((1, D), const)],
        out_specs=pl.BlockSpec((tt, D), row),
        out_shape=jax.ShapeDtypeStruct((T, D), F32),
        compiler_params=_cparams(("arbitrary",)),
        name="outproj_ln1",
    )(x2, attn2, conv2, wo, g, b)


def _route(logits):
    lane = lax.broadcasted_iota(I32, logits.shape, 1)
    work = logits
    vals, hots = [], []
    for _ in range(TOP_K):
        m = jnp.max(work, axis=1, keepdims=True)
        idx = jnp.min(jnp.where(work == m, lane, LANES), axis=1, keepdims=True)
        hot = lane == idx
        vals.append(m)
        hots.append(hot)
        work = jnp.where(hot, -jnp.inf, work)
    es = [jnp.exp(v - vals[0]) for v in vals]
    den = es[0]
    for e in es[1:]:
        den = den + e
    comb = jnp.zeros(logits.shape, F32)
    picked = jnp.zeros(logits.shape, F32)
    for e, hot in zip(es, hots):
        comb = jnp.where(hot, e / den, comb)
        picked = jnp.where(hot, 1.0, picked)
    return comb, picked


def _moe_kernel(h_ref, wr_ref, br_ref, wgu_ref, bgu_ref, wd_ref, bd_ref, g_ref, b_ref,
                y_ref, hb_ref, comb_ref, slot_ref, slotT_ref, *, dn_alpha, rows):
    e = pl.program_id(1)
    tt = h_ref.shape[0]
    ff = wd_ref.shape[1]

    @pl.when(e == 0)
    def _():
        hb = h_ref[...].astype(BF16)
        hb_ref[...] = hb
        comb, picked = _route(_dot(hb, wr_ref[...]) + br_ref[...])
        comb_ref[...] = comb
        before = (lax.broadcasted_iota(I32, (tt, tt), 1) < lax.broadcasted_iota(I32, (tt, tt), 0))
        rank = _dot(jnp.where(before, 1.0, 0.0).astype(BF16), picked.astype(BF16))
        slot = jnp.where(picked > 0.0, rank, -1.0)
        slot_ref[...] = slot
        slotT_ref[...] = slot.T
        y_ref[...] = jnp.zeros(y_ref.shape, F32)

    lane = lax.broadcasted_iota(I32, (tt, LANES), 1)
    pick_col = lambda ref: jnp.sum(jnp.where(lane == e, ref[...], 0.0), axis=1, keepdims=True)
    slot_col = pick_col(slot_ref)
    gate_col = pick_col(comb_ref)
    slot_row = slotT_ref[pl.ds(e, 1), :]
    count = jnp.sum(jnp.where(slot_row >= 0.0, 1, 0))
    sub_iota = lax.broadcasted_iota(I32, (rows, tt), 0).astype(F32)
    lane_iota = lax.broadcasted_iota(I32, (tt, rows), 1).astype(F32)

    def chunk_body(c, carry):
        base = (c * rows).astype(F32)
        gather = jnp.where(slot_row - base == sub_iota, 1.0, 0.0).astype(BF16)
        xg = _dot(gather, hb_ref[...]).astype(BF16)
        gu = _dot(xg, wgu_ref[0]) + bgu_ref[0]
        gate = jnp.minimum(gu[:, 0:ff], SWIGLU_LIMIT)
        up = jnp.clip(gu[:, ff:2 * ff], -SWIGLU_LIMIT, SWIGLU_LIMIT)
        act = (up + 1.0) * (gate * (1.0 / (1.0 + jnp.exp(-SWIGLU_ALPHA * gate))))
        y = _dot(act.astype(BF16), wd_ref[0]) + bd_ref[0]
        scatter = jnp.where(slot_col - base == lane_iota, 1.0, 0.0).astype(BF16)
        y_ref[...] += gate_col * _dot(scatter, y.astype(BF16))
        return carry

    lax.fori_loop(0, (count + rows - 1) // rows, chunk_body, 0)

    @pl.when(e == pl.num_programs(1) - 1)
    def _():
        y_ref[...] = _layer_norm(dn_alpha * h_ref[...] + y_ref[...], g_ref[...], b_ref[...])


def _moe(h2, wr, br, wgu, bgu, wd, bd, g, b, *, dn_alpha):
    T, D = h2.shape
    E, _, F2 = wgu.shape
    tt = min(1024, T)
    assert T % tt == 0
    rows = min(tt, -(-(tt * TOP_K * 5 // (E * 4)) // 16) * 16)
    row = lambda i, e: (i, 0)
    const = lambda i, e: (0, 0)
    exp3 = lambda i, e: (e, 0, 0)
    return pl.pallas_call(
        functools.partial(_moe_kernel, dn_alpha=dn_alpha, rows=rows),
        grid=(T // tt, E),
        in_specs=[pl.BlockSpec((tt, D), row), pl.BlockSpec(wr.shape, const), pl.BlockSpec(br.shape, const),
                  pl.BlockSpec((1, D, F2), exp3), pl.BlockSpec((1, 1, F2), exp3),
                  pl.BlockSpec((1, F2 // 2, D), exp3), pl.BlockSpec((1, 1, D), exp3),
                  pl.BlockSpec((1, D), const), pl.BlockSpec((1, D), const)],
        out_specs=pl.BlockSpec((tt, D), row),
        out_shape=jax.ShapeDtypeStruct((T, D), F32),
        scratch_shapes=[pltpu.VMEM((tt, D), BF16), pltpu.VMEM((tt, LANES), F32),
                        pltpu.VMEM((tt, LANES), F32), pltpu.VMEM((LANES, tt), F32)],
        compiler_params=_cparams(("arbitrary", "arbitrary")),
        name="moe_ln2",
    )(h2, wr, br, wgu, bgu, wd, bd, g, b)


def _split_gu_kernel(w_ref, perm_ref, o_ref):
    o_ref[0] = _dot(w_ref[0].astype(BF16), perm_ref[...]).astype(BF16)


def _split_gu(w_gu):
    E, D, N = w_gu.shape
    tr = min(512, D)
    j = jnp.arange(N, dtype=I32)
    src = jnp.where(j < N // 2, 2 * j, 2 * (j - N // 2) + 1)
    perm = (jnp.arange(N, dtype=I32)[:, None] == src[None, :]).astype(BF16)
    return pl.pallas_call(
        _split_gu_kernel,
        grid=(E, D // tr),
        in_specs=[pl.BlockSpec((1, tr, N), lambda e, r: (e, r, 0)),
                  pl.BlockSpec((N, N), lambda e, r: (0, 0))],
        out_specs=pl.BlockSpec((1, tr, N), lambda e, r: (e, r, 0)),
        out_shape=jax.ShapeDtypeStruct((E, D, N), BF16),
        compiler_params=_cparams(("arbitrary", "arbitrary")),
        name="split_gate_up",
    )(w_gu, perm)


def _prep_weights(w_in, w_conv, w_out, ln1_g, ln1_b, w_router, b_router, w_gu, b_gu, w_down, b_down,
                  ln2_g, ln2_b):
    aw = N_HEADS * HEAD_DIM
    kv = N_KV_HEADS * HEAD_DIM
    iw = N_IDX_HEADS * IDX_DIM
    D = w_in.shape[0]
    cw = D - aw
    o_q, o_k, o_v = 0, aw, aw + kv
    o_qi = o_v + kv
    o_ki = o_qi + iw
    o_wi = o_ki + IDX_DIM
    o_c = o_wi + N_IDX_HEADS
    wa = jnp.concatenate([w_in[:, o_q:o_k], w_in[:, o_qi:o_ki], w_in[:, o_k:o_v], w_in[:, o_v:o_qi]],
                         axis=1).astype(BF16)
    ws = jnp.pad(w_in[:, o_ki:o_c], ((0, 0), (0, LANES - IDX_DIM - N_IDX_HEADS))).astype(BF16)
    wc = w_in[:, o_c:o_c + 3 * cw].astype(BF16)
    E = w_router.shape[1]
    wr = jnp.pad(w_router, ((0, 0), (0, LANES - E))).astype(BF16)
    br = jnp.pad(b_router, (0, LANES - E), constant_values=MASK_VALUE).reshape(1, LANES)
    F = w_gu.shape[2] // 2
    return dict(
        wa=wa, ws=ws, wc=wc, wconv=w_conv, wo=w_out.astype(BF16),
        ln1_g=ln1_g.reshape(1, D), ln1_b=ln1_b.reshape(1, D),
        wr=wr, br=br,
        wgu=_split_gu(w_gu),
        bgu=jnp.concatenate([b_gu[:, 0::2], b_gu[:, 1::2]], axis=1).reshape(E, 1, 2 * F),
        wd=w_down.astype(BF16), bd=b_down.reshape(E, 1, D),
        ln2_g=ln2_g.reshape(1, D), ln2_b=ln2_b.reshape(1, D),
    )


def _chunked_keys(k_all, v_all, kidx_all, kc):
    B, L, kv = k_all.shape
    lp = -(-L // kc) * kc
    pad = lambda a: jnp.pad(a, ((0, 0), (0, lp - L), (0, 0)))
    nc = lp // kc
    kT = pad(k_all).reshape(B, nc, kc, kv).transpose(0, 1, 3, 2).astype(BF16)
    kidxT = pad(kidx_all).reshape(B, nc, kc, IDX_DIM).transpose(0, 1, 3, 2).astype(BF16)
    vb = pad(v_all).reshape(B, nc, kc, N_KV_HEADS, HEAD_DIM).transpose(0, 1, 3, 2, 4).astype(BF16)
    return kT, kidxT, vb


def _layer(x, past_k, past_v, past_kidx, conv_buf, rel_bias, w, *, dn_alpha):
    B, S, D = x.shape
    outs = _project(x, conv_buf, w["wa"], w["ws"], w["wc"], w["wconv"], emit_keys=past_k is None)
    k, v, small, q, qi, conv, new_buf = outs[:7]
    k_idx = small[:, :, 0:IDX_DIM]
    if past_k is None:
        past, l_true = 0, S
        kT, kidxT, vb = outs[7:]
    else:
        past = past_k.shape[1]
        l_true = past + S
        kT, kidxT, vb = _chunked_keys(jnp.concatenate([past_k, k], axis=1),
                                      jnp.concatenate([past_v, v], axis=1),
                                      jnp.concatenate([past_kidx, k_idx], axis=1), KEY_CHUNK)
    qb = min(LANES, S)
    offsets = (-LANES, 0)
    tabs = _bias_tables(rel_bias, qb, offsets)
    bconst = rel_bias[N_BUCKETS // 2 - 1]
    attn = _attend(q, qi, small, kidxT, kT, vb, tabs, bconst, qb=qb, past=past, l_true=l_true, offsets=offsets)
    h = _mix(x.reshape(B * S, D), attn.reshape(B * S, -1), conv.reshape(B * S, -1), w["wo"],
             w["ln1_g"], w["ln1_b"], dn_alpha=dn_alpha)
    y = _moe(h, w["wr"], w["br"], w["wgu"], w["bgu"], w["wd"], w["bd"],
             w["ln2_g"], w["ln2_b"], dn_alpha=dn_alpha)
    return (y.reshape(B, S, D), k.reshape(B, S, N_KV_HEADS, HEAD_DIM), v.reshape(B, S, N_KV_HEADS, HEAD_DIM),
            k_idx, new_buf)


def kernel(x_prompt, x_sample, cache_k, cache_v, cache_kidx, state_conv, rel_bias, w_in, w_conv, w_out,
           ln1_g, ln1_b, w_router, b_router, w_gu, b_gu, w_down, b_down, ln2_g, ln2_b):
    depth = w_in.shape[0]
    assert depth == 1
    dn_alpha = (2 * depth) ** 0.25
    kv = N_KV_HEADS * HEAD_DIM
    w = _prep_weights(w_in[0], w_conv[0], w_out[0], ln1_g[0], ln1_b[0], w_router[0], b_router[0],
                      w_gu[0], b_gu[0], w_down[0], b_down[0], ln2_g[0], ln2_b[0])
    Bp = x_prompt.shape[0]
    cw = w_conv.shape[2]
    zero_buf = jnp.zeros((Bp, CONV_K - 1, cw), F32)
    yp, k1, v1, i1, c1 = _layer(x_prompt, None, None, None, zero_buf, rel_bias, w, dn_alpha=dn_alpha)
    Bs, P = cache_k.shape[1], cache_k.shape[2]
    ys, k2, v2, i2, c2 = _layer(x_sample, cache_k[0].reshape(Bs, P, kv), cache_v[0].reshape(Bs, P, kv),
                                cache_kidx[0], state_conv[0], rel_bias, w, dn_alpha=dn_alpha)
    return (yp, ys, k1[None], v1[None], i1[None], c1[None], k2[None], v2[None], i2[None], c2[None])
```

```python
import functools
import math

import jax
import jax.numpy as jnp
from jax import lax
from jax.experimental import pallas as pl
from jax.experimental.pallas import tpu as pltpu

F32 = jnp.float32
BF16 = jnp.bfloat16
I32 = jnp.int32

CHUNK = 64
N_HEADS = 8
HEAD_DIM = 64
N_KV_HEADS = 2
GROUP = N_HEADS // N_KV_HEADS
N_IDX_HEADS = 8
IDX_DIM = 32
TOPK_MAX = 256
CONV_K = 3
N_BUCKETS = 32
MAX_DISTANCE = 128
N_EXPERTS = 32
TOP_K = 4
SWIGLU_LIMIT = 7.0
SWIGLU_ALPHA = 1.702
LN_EPS = 1e-5
MASK_VALUE = -1e30

LANES = 128
SUBLANES = 8
KEY_CHUNK = 512
QUERY_BLOCK = 256
SLAB = 64
INT_MIN = -(2 ** 31)
VMEM_LIMIT = 56 * 1024 * 1024


def _cparams(sem):
    return pltpu.CompilerParams(dimension_semantics=sem, vmem_limit_bytes=VMEM_LIMIT)


def _dot(a, b):
    return jnp.dot(a, b, preferred_element_type=F32)


def _layer_norm(z, g, b):
    mu = jnp.mean(z, axis=-1, keepdims=True)
    zc = z - mu
    var = jnp.mean(zc * zc, axis=-1, keepdims=True)
    return zc * lax.rsqrt(var + LN_EPS) * g + b


def _bucket_thresholds():
    nb = N_BUCKETS // 2
    max_exact = nb // 2
    out = []
    for j in range(1, nb - max_exact):
        out.append(math.ceil(max_exact * (MAX_DISTANCE / max_exact) ** (j / (nb - max_exact)) - 1e-9))
    return tuple(out)


def _bias_table_kernel(rel_ref, tab_ref, *, offsets):
    nb = N_BUCKETS // 2
    max_exact = nb // 2
    thr = _bucket_thresholds()
    ii = lax.broadcasted_iota(I32, (LANES, LANES), 0)
    jj = lax.broadcasted_iota(I32, (LANES, LANES), 1)
    for d, off in enumerate(offsets):
        rel = off + ii - jj
        n = jnp.abs(rel)
        large = jnp.full((LANES, LANES), max_exact, I32)
        for t in thr:
            large = large + jnp.where(n >= t, 1, 0)
        bucket = jnp.where(rel > 0, nb, 0) + jnp.where(n < max_exact, n, large)
        for h in range(N_HEADS):
            acc = jnp.zeros((LANES, LANES), F32)
            for b in range(N_BUCKETS):
                acc = jnp.where(bucket == b, rel_ref[b, h], acc)
            tab_ref[d, h] = acc - rel_ref[nb - 1, h]


def _bias_tables(rel_bias, offsets):
    return pl.pallas_call(
        functools.partial(_bias_table_kernel, offsets=offsets),
        out_shape=jax.ShapeDtypeStruct((len(offsets), N_HEADS, LANES, LANES), F32),
        in_specs=[pl.BlockSpec(memory_space=pltpu.SMEM)],
        out_specs=pl.BlockSpec(memory_space=pltpu.VMEM),
        name="bias_tables",
    )(rel_bias)


def _proj_kernel(x_ref, wa_ref, ws_ref, wc_ref, wconv_ref, buf_ref,
                 k_ref, v_ref, small_ref, conv_ref, nbuf_ref, *rest, ts, attn_layouts):
    s = pl.program_id(1)
    xb = x_ref[0].astype(BF16)
    pa = _dot(xb, wa_ref[...])
    ps = _dot(xb, ws_ref[...])
    pc = _dot(xb, wc_ref[...])
    aw = N_HEADS * HEAD_DIM
    iw = N_IDX_HEADS * IDX_DIM
    kv = N_KV_HEADS * HEAD_DIM
    k = pa[:, aw + iw:aw + iw + kv]
    v = pa[:, aw + iw + kv:aw + iw + 2 * kv]
    k_ref[0] = k
    v_ref[0] = v
    small_ref[0] = ps
    scale = HEAD_DIM ** -0.5
    if attn_layouts:
        kb_ref, vT_ref, kidxb_ref, qT_ref, qiT_ref, wT_ref, carry_ref = rest
        for n in range(N_KV_HEADS):
            kb_ref[0, 0, n] = k[:, n * HEAD_DIM:(n + 1) * HEAD_DIM].astype(BF16)
        vT_ref[0, 0] = v.T.astype(BF16)
        kidxb_ref[0, 0] = ps[:, 0:IDX_DIM].astype(BF16)
        qT = (pa[:, 0:aw] * scale).T.astype(BF16)
        for h in range(N_HEADS):
            qT_ref[0, h] = qT[h * HEAD_DIM:(h + 1) * HEAD_DIM]
        qiT = pa[:, aw:aw + iw].T.astype(BF16)
        for h in range(N_IDX_HEADS):
            qiT_ref[0, h] = qiT[h * IDX_DIM:(h + 1) * IDX_DIM]
        wT_ref[0] = ps.T[IDX_DIM:IDX_DIM + N_IDX_HEADS]
    else:
        pq_ref, carry_ref = rest
        pq_ref[0] = pa[:, 0:aw + iw]

    cw = pc.shape[1] // 3
    u = pc[:, cw:2 * cw] * pc[:, 2 * cw:3 * cw]

    @pl.when(s == 0)
    def _():
        carry_ref[6:8, :] = buf_ref[0]

    carry_ref[8:8 + ts, :] = u
    y = (carry_ref[6:6 + ts, :] * wconv_ref[0:1, :]
         + carry_ref[7:7 + ts, :] * wconv_ref[1:2, :]
         + u * wconv_ref[2:3, :])
    conv_ref[0] = (pc[:, 0:cw] * y).astype(BF16)
    nb = carry_ref[ts + 6:ts + 8, :]
    nbuf_ref[0] = nb
    carry_ref[6:8, :] = nb


def _project(x, conv_buf, wa, ws, wc, wconv, *, attn_layouts):
    B, S, D = x.shape
    ts = min(KEY_CHUNK, S)
    assert S % ts == 0 and S >= CONV_K - 1
    ns = S // ts
    cw = wc.shape[1] // 3
    kv = N_KV_HEADS * HEAD_DIM
    aw = N_HEADS * HEAD_DIM
    iw = N_IDX_HEADS * IDX_DIM
    out_shape = [
        jax.ShapeDtypeStruct((B, S, kv), F32),
        jax.ShapeDtypeStruct((B, S, kv), F32),
        jax.ShapeDtypeStruct((B, S, LANES), F32),
        jax.ShapeDtypeStruct((B, S, cw), BF16),
        jax.ShapeDtypeStruct((B, CONV_K - 1, cw), F32),
    ]
    out_specs = [
        pl.BlockSpec((1, ts, kv), lambda b, s: (b, s, 0)),
        pl.BlockSpec((1, ts, kv), lambda b, s: (b, s, 0)),
        pl.BlockSpec((1, ts, LANES), lambda b, s: (b, s, 0)),
        pl.BlockSpec((1, ts, cw), lambda b, s: (b, s, 0)),
        pl.BlockSpec((1, CONV_K - 1, cw), lambda b, s: (b, 0, 0)),
    ]
    if attn_layouts:
        out_shape += [
            jax.ShapeDtypeStruct((B, ns, N_KV_HEADS, ts, HEAD_DIM), BF16),
            jax.ShapeDtypeStruct((B, ns, kv, ts), BF16),
            jax.ShapeDtypeStruct((B, ns, ts, IDX_DIM), BF16),
            jax.ShapeDtypeStruct((B, N_HEADS, HEAD_DIM, S), BF16),
            jax.ShapeDtypeStruct((B, N_IDX_HEADS, IDX_DIM, S), BF16),
            jax.ShapeDtypeStruct((B, N_IDX_HEADS, S), F32),
        ]
        out_specs += [
            pl.BlockSpec((1, 1, N_KV_HEADS, ts, HEAD_DIM), lambda b, s: (b, s, 0, 0, 0)),
            pl.BlockSpec((1, 1, kv, ts), lambda b, s: (b, s, 0, 0)),
            pl.BlockSpec((1, 1, ts, IDX_DIM), lambda b, s: (b, s, 0, 0)),
            pl.BlockSpec((1, N_HEADS, HEAD_DIM, ts), lambda b, s: (b, 0, 0, s)),
            pl.BlockSpec((1, N_IDX_HEADS, IDX_DIM, ts), lambda b, s: (b, 0, 0, s)),
            pl.BlockSpec((1, N_IDX_HEADS, ts), lambda b, s: (b, 0, s)),
        ]
    else:
        out_shape += [jax.ShapeDtypeStruct((B, S, aw + iw), F32)]
        out_specs += [pl.BlockSpec((1, ts, aw + iw), lambda b, s: (b, s, 0))]
    const2 = lambda b, s: (0, 0)
    in_specs = [
        pl.BlockSpec((1, ts, D), lambda b, s: (b, s, 0)),
        pl.BlockSpec(wa.shape, const2),
        pl.BlockSpec(ws.shape, const2),
        pl.BlockSpec(wc.shape, const2),
        pl.BlockSpec(wconv.shape, const2),
        pl.BlockSpec((1, CONV_K - 1, cw), lambda b, s: (b, 0, 0)),
    ]
    return pl.pallas_call(
        functools.partial(_proj_kernel, ts=ts, attn_layouts=attn_layouts),
        grid=(B, ns),
        in_specs=in_specs,
        out_specs=tuple(out_specs),
        out_shape=tuple(out_shape),
        scratch_shapes=[pltpu.VMEM((ts + 8, cw), F32)],
        compiler_params=_cparams(("arbitrary", "arbitrary")),
        name="proj_conv",
    )(x, wa, ws, wc, wconv, conv_buf)


def _attn_kernel(qT_ref, qiT_ref, wT_ref, kidx_ref, kb_ref, vT_ref, tab_ref,
                 o_ref, skey_ref, jb_ref, madd_ref, *head_refs, qb, kc, past, l_true, topk, offsets):
    s_refs, p_refs, m_refs, l_refs, acc_refs = (head_refs[g * N_HEADS:(g + 1) * N_HEADS] for g in range(5))
    i = pl.program_id(1)
    qoff = past + i * qb
    adm_end = jnp.minimum(((qoff + qb - 1) // CHUNK + 1) * CHUNK, l_true)
    nck = (adm_end + kc - 1) // kc
    n_far = jnp.maximum(qoff - LANES, 0) // kc
    idx_bits = int(l_true).bit_length()
    groups = kc // SUBLANES

    rowi = lax.broadcasted_iota(I32, (kc, qb), 0)
    qpos = qoff + lax.broadcasted_iota(I32, (1, qb), 1)
    kmax = jnp.minimum((qpos // CHUNK + 1) * CHUNK, l_true)

    def score_body(c, carry):
        kidx_c = kidx_ref[0, c]
        sc = jnp.zeros((kc, qb), F32)
        for h in range(N_IDX_HEADS):
            d = _dot(kidx_c, qiT_ref[0, h])
            sc = sc + wT_ref[0, h:h + 1, :] * jnp.maximum(d, 0.0)
        bits = pltpu.bitcast(sc, I32)
        skey = jnp.where(bits < 0, bits ^ jnp.int32(0x7FFFFFFF), bits)
        skey = jnp.where(skey == -1, 0, skey)
        skey_ref[c] = jnp.where(rowi < kmax - c * kc, skey, jnp.int32(INT_MIN))
        return carry

    lax.fori_loop(0, nck, score_body, 0)

    def count(pred_fn):
        def body(c, part):
            ind = jnp.where(pred_fn(c, skey_ref[c]), 1, 0)
            return part + jnp.sum(ind.reshape(groups, SUBLANES, qb), axis=0)
        part = lax.fori_loop(0, nck, body, jnp.zeros((SUBLANES, qb), I32))
        return jnp.sum(part, axis=0, keepdims=True)

    def thr_body(it, t_u):
        cand_u = t_u | (jnp.int32(1) << (31 - it))
        cand_s = cand_u ^ jnp.int32(INT_MIN)
        cnt = count(lambda c, sk: sk >= cand_s)
        return jnp.where(cnt >= topk, cand_u, t_u)

    t_u = lax.fori_loop(0, 32, thr_body, jnp.zeros((1, qb), I32))
    thr = t_u ^ jnp.int32(INT_MIN)
    cnt_ge = count(lambda c, sk: sk >= thr)
    cnt_gt = count(lambda c, sk: sk > thr)

    jb_ref[...] = jnp.full(jb_ref.shape, 2 ** 31 - 1, I32)

    @pl.when(jnp.max(cnt_ge) > topk)
    def _():
        need = topk - cnt_gt

        def tie_body(it, jb):
            cand = jb | (jnp.int32(1) << (idx_bits - 1 - it))
            cnt = count(lambda c, sk: jnp.logical_and(sk == thr, c * kc + rowi < cand))
            return jnp.where(cnt <= need, cand, jb)

        jb = lax.fori_loop(0, idx_bits, tie_body, jnp.zeros((1, qb), I32))
        jb_ref[...] = jnp.broadcast_to(jb, jb_ref.shape)

    jbound = jnp.where(thr == jnp.int32(INT_MIN), 0, jb_ref[0:1, :])
    thr_m1 = thr - 1

    for h in range(N_HEADS):
        m_refs[h][...] = jnp.full((1, qb), MASK_VALUE, F32)
        l_refs[h][...] = jnp.zeros((1, qb), F32)
        acc_refs[h][...] = jnp.zeros((HEAD_DIM, qb), F32)
    n_slabs = kc // SLAB

    def fold(x, op):
        return op(x.reshape(SLAB // SUBLANES, SUBLANES, qb), axis=0)

    def attn_chunk(c, near):
        sk = skey_ref[c]
        t_eff = jnp.where(rowi < jbound - c * kc, thr_m1, thr)
        madd_ref[...] = jnp.where(sk > t_eff, 0.0, MASK_VALUE)
        for h in range(N_HEADS):
            s_refs[h][...] = _dot(kb_ref[0, c, h // GROUP], qT_ref[0, h])
        alphas = []
        for h in range(N_HEADS):
            s_ref, p_ref, m_ref, l_ref = s_refs[h], p_refs[h], m_refs[h], l_refs[h]
            macc = jnp.full((SUBLANES, qb), MASK_VALUE, F32)
            for j in range(n_slabs):
                rows = pl.ds(j * SLAB, SLAB)
                x = s_ref[rows, :] + madd_ref[rows, :]
                if near:
                    t, half = (j * SLAB) // LANES, (j * SLAB) % LANES
                    parts = []
                    for u in range(qb // LANES):
                        d_tu = c * kc + t * LANES - (qoff + u * LANES)
                        delta = jnp.zeros((SLAB, LANES), F32)
                        for di, off in enumerate(offsets):
                            delta = jnp.where(d_tu == off, tab_ref[di, h, half:half + SLAB, :], delta)
                        parts.append(delta)
                    x = x + jnp.concatenate(parts, axis=1)
                s_ref[rows, :] = x
                macc = jnp.maximum(macc, fold(x, jnp.max))
            m_old = m_ref[...]
            m_new = jnp.maximum(m_old, jnp.max(macc, axis=0, keepdims=True))
            alpha = jnp.exp(m_old - m_new)
            lacc = jnp.zeros((SUBLANES, qb), F32)
            for j in range(n_slabs):
                rows = pl.ds(j * SLAB, SLAB)
                p = jnp.exp(s_ref[rows, :] - m_new)
                lacc = lacc + fold(p, jnp.sum)
                p_ref[rows, :] = p.astype(BF16)
            l_ref[...] = alpha * l_ref[...] + jnp.sum(lacc, axis=0, keepdims=True)
            m_ref[...] = m_new
            alphas.append(alpha)
        for h in range(N_HEADS):
            n = h // GROUP
            pv = _dot(vT_ref[0, c, n * HEAD_DIM:(n + 1) * HEAD_DIM, :], p_refs[h][...])
            acc_refs[h][...] = alphas[h] * acc_refs[h][...] + pv

    def far_body(c, carry):
        attn_chunk(c, False)
        return carry

    def near_body(c, carry):
        attn_chunk(c, True)
        return carry

    lax.fori_loop(0, n_far, far_body, 0)
    lax.fori_loop(n_far, nck, near_body, 0)
    oT = jnp.concatenate([acc_refs[h][...] / l_refs[h][...] for h in range(N_HEADS)], axis=0)
    o_ref[0] = oT.T.astype(o_ref.dtype)


def _attend(qT, qiT, wT, kidxb, kb, vT, tabs, *, qb, past, l_true, offsets):
    B, _, _, S = qT.shape
    nc, kc = vT.shape[1], vT.shape[3]
    assert S % qb == 0 and qb % LANES == 0 and past % LANES == 0 and kc % LANES == 0
    topk = min(TOPK_MAX, l_true // 4)
    kern = functools.partial(_attn_kernel, qb=qb, kc=kc, past=past, l_true=l_true,
                             topk=topk, offsets=offsets)
    in_specs = [
        pl.BlockSpec((1, N_HEADS, HEAD_DIM, qb), lambda b, i: (b, 0, 0, i)),
        pl.BlockSpec((1, N_IDX_HEADS, IDX_DIM, qb), lambda b, i: (b, 0, 0, i)),
        pl.BlockSpec((1, N_IDX_HEADS, qb), lambda b, i: (b, 0, i)),
        pl.BlockSpec((1,) + kidxb.shape[1:], lambda b, i: (b, 0, 0, 0)),
        pl.BlockSpec((1,) + kb.shape[1:], lambda b, i: (b, 0, 0, 0, 0)),
        pl.BlockSpec((1,) + vT.shape[1:], lambda b, i: (b, 0, 0, 0)),
        pl.BlockSpec(tabs.shape, lambda b, i: (0, 0, 0, 0)),
    ]
    return pl.pallas_call(
        kern,
        grid=(B, S // qb),
        in_specs=in_specs,
        out_specs=pl.BlockSpec((1, qb, N_HEADS * HEAD_DIM), lambda b, i: (b, i, 0)),
        out_shape=jax.ShapeDtypeStruct((B, S, N_HEADS * HEAD_DIM), BF16),
        scratch_shapes=[
            pltpu.VMEM((nc, kc, qb), I32),
            pltpu.VMEM((SUBLANES, qb), I32),
            pltpu.VMEM((kc, qb), F32),
        ] + [pltpu.VMEM((kc, qb), F32)] * N_HEADS
          + [pltpu.VMEM((kc, qb), BF16)] * N_HEADS
          + [pltpu.VMEM((1, qb), F32)] * N_HEADS
          + [pltpu.VMEM((1, qb), F32)] * N_HEADS
          + [pltpu.VMEM((HEAD_DIM, qb), F32)] * N_HEADS,
        compiler_params=_cparams(("arbitrary", "arbitrary")),
        name="dsa_attend",
    )(qT, qiT, wT, kidxb, kb, vT, tabs)


def _mix_kernel(x_ref, attn_ref, conv_ref, wo_ref, g_ref, b_ref, h_ref, *, dn_alpha):
    cat = jnp.concatenate([attn_ref[...], conv_ref[...]], axis=1)
    mixed = _dot(cat, wo_ref[...])
    h_ref[...] = _layer_norm(dn_alpha * x_ref[...] + mixed, g_ref[...], b_ref[...])


def _mix(x2, attn2, conv2, wo, g, b, *, dn_alpha):
    T, D = x2.shape
    tt = min(1024, T)
    assert T % tt == 0
    aw, cw = attn2.shape[1], conv2.shape[1]
    row = lambda i: (i, 0)
    const = lambda i: (0, 0)
    return pl.pallas_call(
        functools.partial(_mix_kernel, dn_alpha=dn_alpha),
        grid=(T // tt,),
        in_specs=[pl.BlockSpec((tt, D), row), pl.BlockSpec((tt, aw), row), pl.BlockSpec((tt, cw), row),
                  pl.BlockSpec(wo.shape, const), pl.BlockSpec((1, D), const), pl.BlockSpec((1, D), const)],
        out_specs=pl.BlockSpec((tt, D), row),
        out_shape=jax.ShapeDtypeStruct((T, D), F32),
        compiler_params=_cparams(("arbitrary",)),
        name="outproj_ln1",
    )(x2, attn2, conv2, wo, g, b)


def _route(logits):
    lane = lax.broadcasted_iota(I32, logits.shape, 1)
    work = logits
    vals, hots = [], []
    for _ in range(TOP_K):
        m = jnp.max(work, axis=1, keepdims=True)
        idx = jnp.min(jnp.where(work == m, lane, LANES), axis=1, keepdims=True)
        hot = lane == idx
        vals.append(m)
        hots.append(hot)
        work = jnp.where(hot, -jnp.inf, work)
    es = [jnp.exp(v - vals[0]) for v in vals]
    den = es[0]
    for e in es[1:]:
        den = den + e
    comb = jnp.zeros(logits.shape, F32)
    picked = jnp.zeros(logits.shape, F32)
    for e, hot in zip(es, hots):
        comb = jnp.where(hot, e / den, comb)
        picked = jnp.where(hot, 1.0, picked)
    return comb, picked


def _moe_kernel(h_ref, wr_ref, br_ref, wgu_ref, bgu_ref, wd_ref, bd_ref, g_ref, b_ref,
                y_ref, hb_ref, comb_ref, slot_ref, slotT_ref, *, dn_alpha, rows):
    e = pl.program_id(1)
    tt = h_ref.shape[0]
    ff = wd_ref.shape[1]

    @pl.when(e == 0)
    def _():
        hb = h_ref[...].astype(BF16)
        hb_ref[...] = hb
        comb, picked = _route(_dot(hb, wr_ref[...]) + br_ref[...])
        comb_ref[...] = comb
        before = (lax.broadcasted_iota(I32, (tt, tt), 1) < lax.broadcasted_iota(I32, (tt, tt), 0))
        rank = _dot(jnp.where(before, 1.0, 0.0).astype(BF16), picked.astype(BF16))
        slot = jnp.where(picked > 0.0, rank, -1.0)
        slot_ref[...] = slot
        slotT_ref[...] = slot.T
        y_ref[...] = jnp.zeros(y_ref.shape, F32)

    lane = lax.broadcasted_iota(I32, (tt, LANES), 1)
    pick_col = lambda ref: jnp.sum(jnp.where(lane == e, ref[...], 0.0), axis=1, keepdims=True)
    slot_col = pick_col(slot_ref)
    gate_col = pick_col(comb_ref)
    slot_row = slotT_ref[pl.ds(e, 1), :]
    count = jnp.sum(jnp.where(slot_row >= 0.0, 1, 0))
    sub_iota = lax.broadcasted_iota(I32, (rows, tt), 0).astype(F32)
    lane_iota = lax.broadcasted_iota(I32, (tt, rows), 1).astype(F32)

    def chunk_body(c, carry):
        base = (c * rows).astype(F32)
        gather = jnp.where(slot_row - base == sub_iota, 1.0, 0.0).astype(BF16)
        xg = _dot(gather, hb_ref[...]).astype(BF16)
        gu = _dot(xg, wgu_ref[0]) + bgu_ref[0]
        gate = jnp.minimum(gu[:, 0:ff], SWIGLU_LIMIT)
        up = jnp.clip(gu[:, ff:2 * ff], -SWIGLU_LIMIT, SWIGLU_LIMIT)
        act = (up + 1.0) * (gate * (1.0 / (1.0 + jnp.exp(-SWIGLU_ALPHA * gate))))
        y = _dot(act.astype(BF16), wd_ref[0]) + bd_ref[0]
        scatter = jnp.where(slot_col - base == lane_iota, 1.0, 0.0).astype(BF16)
        y_ref[...] += gate_col * _dot(scatter, y.astype(BF16))
        return carry

    lax.fori_loop(0, (count + rows - 1) // rows, chunk_body, 0)

    @pl.when(e == pl.num_programs(1) - 1)
    def _():
        y_ref[...] = _layer_norm(dn_alpha * h_ref[...] + y_ref[...], g_ref[...], b_ref[...])


def _moe(h2, wr, br, wgu, bgu, wd, bd, g, b, *, dn_alpha):
    T, D = h2.shape
    E, _, F2 = wgu.shape
    tt = min(1024, T)
    assert T % tt == 0
    rows = min(tt, -(-(tt * TOP_K * 5 // (E * 4)) // 16) * 16)
    row = lambda i, e: (i, 0)
    const = lambda i, e: (0, 0)
    exp3 = lambda i, e: (e, 0, 0)
    return pl.pallas_call(
        functools.partial(_moe_kernel, dn_alpha=dn_alpha, rows=rows),
        grid=(T // tt, E),
        in_specs=[pl.BlockSpec((tt, D), row), pl.BlockSpec(wr.shape, const), pl.BlockSpec(br.shape, const),
                  pl.BlockSpec((1, D, F2), exp3), pl.BlockSpec((1, 1, F2), exp3),
                  pl.BlockSpec((1, F2 // 2, D), exp3), pl.BlockSpec((1, 1, D), exp3),
                  pl.BlockSpec((1, D), const), pl.BlockSpec((1, D), const)],
        out_specs=pl.BlockSpec((tt, D), row),
        out_shape=jax.ShapeDtypeStruct((T, D), F32),
        scratch_shapes=[pltpu.VMEM((tt, D), BF16), pltpu.VMEM((tt, LANES), F32),
                        pltpu.VMEM((tt, LANES), F32), pltpu.VMEM((LANES, tt), F32)],
        compiler_params=_cparams(("arbitrary", "arbitrary")),
        name="moe_ln2",
    )(h2, wr, br, wgu, bgu, wd, bd, g, b)


def _split_gu_kernel(w_ref, perm_ref, o_ref):
    o_ref[0] = _dot(w_ref[0].astype(BF16), perm_ref[...]).astype(BF16)


def _split_gu(w_gu):
    E, D, N = w_gu.shape
    tr = min(512, D)
    j = jnp.arange(N, dtype=I32)
    src = jnp.where(j < N // 2, 2 * j, 2 * (j - N // 2) + 1)
    perm = (jnp.arange(N, dtype=I32)[:, None] == src[None, :]).astype(BF16)
    return pl.pallas_call(
        _split_gu_kernel,
        grid=(E, D // tr),
        in_specs=[pl.BlockSpec((1, tr, N), lambda e, r: (e, r, 0)),
                  pl.BlockSpec((N, N), lambda e, r: (0, 0))],
        out_specs=pl.BlockSpec((1, tr, N), lambda e, r: (e, r, 0)),
        out_shape=jax.ShapeDtypeStruct((E, D, N), BF16),
        compiler_params=_cparams(("arbitrary", "arbitrary")),
        name="split_gate_up",
    )(w_gu, perm)


def _prep_weights(w_in, w_conv, w_out, ln1_g, ln1_b, w_router, b_router, w_gu, b_gu, w_down, b_down,
                  ln2_g, ln2_b):
    aw = N_HEADS * HEAD_DIM
    kv = N_KV_HEADS * HEAD_DIM
    iw = N_IDX_HEADS * IDX_DIM
    D = w_in.shape[0]
    cw = D - aw
    o_q, o_k, o_v = 0, aw, aw + kv
    o_qi = o_v + kv
    o_ki = o_qi + iw
    o_wi = o_ki + IDX_DIM
    o_c = o_wi + N_IDX_HEADS
    wa = jnp.concatenate([w_in[:, o_q:o_k], w_in[:, o_qi:o_ki], w_in[:, o_k:o_v], w_in[:, o_v:o_qi]],
                         axis=1).astype(BF16)
    ws = jnp.pad(w_in[:, o_ki:o_c], ((0, 0), (0, LANES - IDX_DIM - N_IDX_HEADS))).astype(BF16)
    wc = w_in[:, o_c:o_c + 3 * cw].astype(BF16)
    E = w_router.shape[1]
    wr = jnp.pad(w_router, ((0, 0), (0, LANES - E))).astype(BF16)
    br = jnp.pad(b_router, (0, LANES - E), constant_values=MASK_VALUE).reshape(1, LANES)
    F = w_gu.shape[2] // 2
    return dict(
        wa=wa, ws=ws, wc=wc, wconv=w_conv, wo=w_out.astype(BF16),
        ln1_g=ln1_g.reshape(1, D), ln1_b=ln1_b.reshape(1, D),
        wr=wr, br=br,
        wgu=_split_gu(w_gu),
        bgu=jnp.concatenate([b_gu[:, 0::2], b_gu[:, 1::2]], axis=1).reshape(E, 1, 2 * F),
        wd=w_down.astype(BF16), bd=b_down.reshape(E, 1, D),
        ln2_g=ln2_g.reshape(1, D), ln2_b=ln2_b.reshape(1, D),
    )


def _decode_layouts(pq, small, k_all, v_all, kidx_all, kc, qb):
    B, S, _ = pq.shape
    aw = N_HEADS * HEAD_DIM
    L, kv = k_all.shape[1], k_all.shape[2]
    lp = -(-L // kc) * kc
    nc = lp // kc
    padk = lambda a: jnp.pad(a, ((0, 0), (0, lp - L), (0, 0)))
    padq = lambda a: jnp.pad(a, ((0, 0), (0, 0), (0, 0), (0, qb - S)))
    q = pq[:, :, 0:aw] * (HEAD_DIM ** -0.5)
    qT = padq(q.reshape(B, S, N_HEADS, HEAD_DIM).transpose(0, 2, 3, 1)).astype(BF16)
    qiT = padq(pq[:, :, aw:].reshape(B, S, N_IDX_HEADS, IDX_DIM).transpose(0, 2, 3, 1)).astype(BF16)
    wT = jnp.pad(small[:, :, IDX_DIM:IDX_DIM + N_IDX_HEADS].transpose(0, 2, 1), ((0, 0), (0, 0), (0, qb - S)))
    kb = padk(k_all).reshape(B, nc, kc, N_KV_HEADS, HEAD_DIM).transpose(0, 1, 3, 2, 4).astype(BF16)
    vT = padk(v_all).reshape(B, nc, kc, kv).transpose(0, 1, 3, 2).astype(BF16)
    kidxb = padk(kidx_all).reshape(B, nc, kc, IDX_DIM).astype(BF16)
    return qT, qiT, wT, kidxb, kb, vT


def _layer(x, past_k, past_v, past_kidx, conv_buf, rel_bias, w, *, dn_alpha):
    B, S, D = x.shape
    prefill = past_k is None
    outs = _project(x, conv_buf, w["wa"], w["ws"], w["wc"], w["wconv"], attn_layouts=prefill)
    k, v, small, conv, new_buf = outs[:5]
    k_idx = small[:, :, 0:IDX_DIM]
    if prefill:
        past, l_true = 0, S
        qb = min(QUERY_BLOCK, S)
        kb, vT, kidxb, qT, qiT, wT = outs[5:]
    else:
        past = past_k.shape[1]
        l_true = past + S
        qb = -(-S // LANES) * LANES
        qT, qiT, wT, kidxb, kb, vT = _decode_layouts(
            outs[5], small, jnp.concatenate([past_k, k], axis=1), jnp.concatenate([past_v, v], axis=1),
            jnp.concatenate([past_kidx, k_idx], axis=1), KEY_CHUNK, qb)
    offsets = (-LANES, 0)
    tabs = _bias_tables(rel_bias, offsets)
    attn = _attend(qT, qiT, wT, kidxb, kb, vT, tabs, qb=qb, past=past, l_true=l_true, offsets=offsets)
    attn = attn[:, :S]
    h = _mix(x.reshape(B * S, D), attn.reshape(B * S, -1), conv.reshape(B * S, -1), w["wo"],
             w["ln1_g"], w["ln1_b"], dn_alpha=dn_alpha)
    y = _moe(h, w["wr"], w["br"], w["wgu"], w["bgu"], w["wd"], w["bd"],
             w["ln2_g"], w["ln2_b"], dn_alpha=dn_alpha)
    return (y.reshape(B, S, D), k.reshape(B, S, N_KV_HEADS, HEAD_DIM), v.reshape(B, S, N_KV_HEADS, HEAD_DIM),
            k_idx, new_buf)


def kernel(x_prompt, x_sample, cache_k, cache_v, cache_kidx, state_conv, rel_bias, w_in, w_conv, w_out,
           ln1_g, ln1_b, w_router, b_router, w_gu, b_gu, w_down, b_down, ln2_g, ln2_b):
    depth = w_in.shape[0]
    assert depth == 1
    dn_alpha = (2 * depth) ** 0.25
    kv = N_KV_HEADS * HEAD_DIM
    w = _prep_weights(w_in[0], w_conv[0], w_out[0], ln1_g[0], ln1_b[0], w_router[0], b_router[0],
                      w_gu[0], b_gu[0], w_down[0], b_down[0], ln2_g[0], ln2_b[0])
    Bp = x_prompt.shape[0]
    cw = w_conv.shape[2]
    zero_buf = jnp.zeros((Bp, CONV_K - 1, cw), F32)
    yp, k1, v1, i1, c1 = _layer(x_prompt, None, None, None, zero_buf, rel_bias, w, dn_alpha=dn_alpha)
    Bs, P = cache_k.shape[1], cache_k.shape[2]
    ys, k2, v2, i2, c2 = _layer(x_sample, cache_k[0].reshape(Bs, P, kv), cache_v[0].reshape(Bs, P, kv),
                                cache_kidx[0], state_conv[0], rel_bias, w, dn_alpha=dn_alpha)
    return (yp, ys, k1[None], v1[None], i1[None], c1[None], k2[None], v2[None], i2[None], c2[None])
```

```python
import functools
import math

import jax
import jax.numpy as jnp
from jax import lax
from jax.experimental import pallas as pl
from jax.experimental.pallas import tpu as pltpu

F32 = jnp.float32
BF16 = jnp.bfloat16
I32 = jnp.int32
I16 = jnp.int16

CHUNK = 64
N_HEADS = 8
HEAD_DIM = 64
N_KV_HEADS = 2
GROUP = N_HEADS // N_KV_HEADS
N_IDX_HEADS = 8
IDX_DIM = 32
TOPK_MAX = 256
CONV_K = 3
N_BUCKETS = 32
MAX_DISTANCE = 128
N_EXPERTS = 32
TOP_K = 4
SWIGLU_LIMIT = 7.0
SWIGLU_ALPHA = 1.702
LN_EPS = 1e-5
MASK_VALUE = -1e30

LANES = 128
SUBLANES = 8
KEY_CHUNK = 512
QUERY_BLOCK = 256
SLAB = 64
INT_MIN = -(2 ** 31)
VMEM_LIMIT = 56 * 1024 * 1024


def _cparams(sem):
    return pltpu.CompilerParams(dimension_semantics=sem, vmem_limit_bytes=VMEM_LIMIT)


def _dot(a, b):
    return jnp.dot(a, b, preferred_element_type=F32)


def _layer_norm(z, g, b):
    mu = jnp.mean(z, axis=-1, keepdims=True)
    zc = z - mu
    var = jnp.mean(zc * zc, axis=-1, keepdims=True)
    return zc * lax.rsqrt(var + LN_EPS) * g + b


def _bucket_thresholds():
    nb = N_BUCKETS // 2
    max_exact = nb // 2
    out = []
    for j in range(1, nb - max_exact):
        out.append(math.ceil(max_exact * (MAX_DISTANCE / max_exact) ** (j / (nb - max_exact)) - 1e-9))
    return tuple(out)


def _bias_table_kernel(rel_ref, tab_ref, *, offsets):
    nb = N_BUCKETS // 2
    max_exact = nb // 2
    thr = _bucket_thresholds()
    ii = lax.broadcasted_iota(I32, (LANES, LANES), 0)
    jj = lax.broadcasted_iota(I32, (LANES, LANES), 1)
    for d, off in enumerate(offsets):
        rel = off + ii - jj
        n = jnp.abs(rel)
        large = jnp.full((LANES, LANES), max_exact, I32)
        for t in thr:
            large = large + jnp.where(n >= t, 1, 0)
        bucket = jnp.where(rel > 0, nb, 0) + jnp.where(n < max_exact, n, large)
        for h in range(N_HEADS):
            acc = jnp.zeros((LANES, LANES), F32)
            for b in range(N_BUCKETS):
                acc = jnp.where(bucket == b, rel_ref[b, h], acc)
            tab_ref[d, h] = acc - rel_ref[nb - 1, h]


def _bias_tables(rel_bias, offsets):
    return pl.pallas_call(
        functools.partial(_bias_table_kernel, offsets=offsets),
        out_shape=jax.ShapeDtypeStruct((len(offsets), N_HEADS, LANES, LANES), F32),
        in_specs=[pl.BlockSpec(memory_space=pltpu.SMEM)],
        out_specs=pl.BlockSpec(memory_space=pltpu.VMEM),
        name="bias_tables",
    )(rel_bias)


def _proj_kernel(x_ref, wa_ref, ws_ref, wc_ref, wconv_ref, buf_ref,
                 k_ref, v_ref, small_ref, conv_ref, nbuf_ref, *rest, ts, attn_layouts):
    s = pl.program_id(1)
    xb = x_ref[0].astype(BF16)
    pa = _dot(xb, wa_ref[...])
    ps = _dot(xb, ws_ref[...])
    pc = _dot(xb, wc_ref[...])
    aw = N_HEADS * HEAD_DIM
    iw = N_IDX_HEADS * IDX_DIM
    kv = N_KV_HEADS * HEAD_DIM
    k = pa[:, aw + iw:aw + iw + kv]
    v = pa[:, aw + iw + kv:aw + iw + 2 * kv]
    k_ref[0] = k
    v_ref[0] = v
    small_ref[0] = ps
    scale = HEAD_DIM ** -0.5
    if attn_layouts:
        kb_ref, vT_ref, kidxb_ref, qT_ref, qiT_ref, wT_ref, carry_ref = rest
        for n in range(N_KV_HEADS):
            kb_ref[0, 0, n] = k[:, n * HEAD_DIM:(n + 1) * HEAD_DIM].astype(BF16)
        vT_ref[0, 0] = v.T.astype(BF16)
        kidxb_ref[0, 0] = ps[:, 0:IDX_DIM].astype(BF16)
        qT = (pa[:, 0:aw] * scale).T.astype(BF16)
        for h in range(N_HEADS):
            qT_ref[0, h] = qT[h * HEAD_DIM:(h + 1) * HEAD_DIM]
        qiT = pa[:, aw:aw + iw].T.astype(BF16)
        for h in range(N_IDX_HEADS):
            qiT_ref[0, h] = qiT[h * IDX_DIM:(h + 1) * IDX_DIM]
        wT_ref[0] = ps.T[IDX_DIM:IDX_DIM + N_IDX_HEADS]
    else:
        pq_ref, carry_ref = rest
        pq_ref[0] = pa[:, 0:aw + iw]

    cw = pc.shape[1] // 3
    u = pc[:, cw:2 * cw] * pc[:, 2 * cw:3 * cw]

    @pl.when(s == 0)
    def _():
        carry_ref[6:8, :] = buf_ref[0]

    carry_ref[8:8 + ts, :] = u
    y = (carry_ref[6:6 + ts, :] * wconv_ref[0:1, :]
         + carry_ref[7:7 + ts, :] * wconv_ref[1:2, :]
         + u * wconv_ref[2:3, :])
    conv_ref[0] = (pc[:, 0:cw] * y).astype(BF16)
    nb = carry_ref[ts + 6:ts + 8, :]
    nbuf_ref[0] = nb
    carry_ref[6:8, :] = nb


def _project(x, conv_buf, wa, ws, wc, wconv, *, attn_layouts):
    B, S, D = x.shape
    ts = min(KEY_CHUNK, S)
    assert S % ts == 0 and S >= CONV_K - 1
    ns = S // ts
    cw = wc.shape[1] // 3
    kv = N_KV_HEADS * HEAD_DIM
    aw = N_HEADS * HEAD_DIM
    iw = N_IDX_HEADS * IDX_DIM
    out_shape = [
        jax.ShapeDtypeStruct((B, S, kv), F32),
        jax.ShapeDtypeStruct((B, S, kv), F32),
        jax.ShapeDtypeStruct((B, S, LANES), F32),
        jax.ShapeDtypeStruct((B, S, cw), BF16),
        jax.ShapeDtypeStruct((B, CONV_K - 1, cw), F32),
    ]
    out_specs = [
        pl.BlockSpec((1, ts, kv), lambda b, s: (b, s, 0)),
        pl.BlockSpec((1, ts, kv), lambda b, s: (b, s, 0)),
        pl.BlockSpec((1, ts, LANES), lambda b, s: (b, s, 0)),
        pl.BlockSpec((1, ts, cw), lambda b, s: (b, s, 0)),
        pl.BlockSpec((1, CONV_K - 1, cw), lambda b, s: (b, 0, 0)),
    ]
    if attn_layouts:
        out_shape += [
            jax.ShapeDtypeStruct((B, ns, N_KV_HEADS, ts, HEAD_DIM), BF16),
            jax.ShapeDtypeStruct((B, ns, kv, ts), BF16),
            jax.ShapeDtypeStruct((B, ns, ts, IDX_DIM), BF16),
            jax.ShapeDtypeStruct((B, N_HEADS, HEAD_DIM, S), BF16),
            jax.ShapeDtypeStruct((B, N_IDX_HEADS, IDX_DIM, S), BF16),
            jax.ShapeDtypeStruct((B, N_IDX_HEADS, S), F32),
        ]
        out_specs += [
            pl.BlockSpec((1, 1, N_KV_HEADS, ts, HEAD_DIM), lambda b, s: (b, s, 0, 0, 0)),
            pl.BlockSpec((1, 1, kv, ts), lambda b, s: (b, s, 0, 0)),
            pl.BlockSpec((1, 1, ts, IDX_DIM), lambda b, s: (b, s, 0, 0)),
            pl.BlockSpec((1, N_HEADS, HEAD_DIM, ts), lambda b, s: (b, 0, 0, s)),
            pl.BlockSpec((1, N_IDX_HEADS, IDX_DIM, ts), lambda b, s: (b, 0, 0, s)),
            pl.BlockSpec((1, N_IDX_HEADS, ts), lambda b, s: (b, 0, s)),
        ]
    else:
        out_shape += [jax.ShapeDtypeStruct((B, S, aw + iw), F32)]
        out_specs += [pl.BlockSpec((1, ts, aw + iw), lambda b, s: (b, s, 0))]
    const2 = lambda b, s: (0, 0)
    in_specs = [
        pl.BlockSpec((1, ts, D), lambda b, s: (b, s, 0)),
        pl.BlockSpec(wa.shape, const2),
        pl.BlockSpec(ws.shape, const2),
        pl.BlockSpec(wc.shape, const2),
        pl.BlockSpec(wconv.shape, const2),
        pl.BlockSpec((1, CONV_K - 1, cw), lambda b, s: (b, 0, 0)),
    ]
    return pl.pallas_call(
        functools.partial(_proj_kernel, ts=ts, attn_layouts=attn_layouts),
        grid=(B, ns),
        in_specs=in_specs,
        out_specs=tuple(out_specs),
        out_shape=tuple(out_shape),
        scratch_shapes=[pltpu.VMEM((ts + 8, cw), F32)],
        compiler_params=_cparams(("arbitrary", "arbitrary")),
        name="proj_conv",
    )(x, wa, ws, wc, wconv, conv_buf)


def _attn_kernel(qT_ref, qiT_ref, wT_ref, kidx_ref, kb_ref, vT_ref, tab_ref,
                 o_ref, skey_ref, hi16_ref, lo16_ref, jb_ref, madd_ref, *head_refs,
                 qb, kc, past, l_true, topk, offsets):
    s_refs, p_refs, m_refs, l_refs, acc_refs = (head_refs[g * N_HEADS:(g + 1) * N_HEADS] for g in range(5))
    i = pl.program_id(1)
    qoff = past + i * qb
    adm_end = jnp.minimum(((qoff + qb - 1) // CHUNK + 1) * CHUNK, l_true)
    nck = (adm_end + kc - 1) // kc
    n_far = jnp.maximum(qoff - LANES, 0) // kc
    idx_bits = int(l_true).bit_length()
    groups = kc // SUBLANES

    rowi = lax.broadcasted_iota(I32, (kc, qb), 0)
    qpos = qoff + lax.broadcasted_iota(I32, (1, qb), 1)
    kmax = jnp.minimum((qpos // CHUNK + 1) * CHUNK, l_true)

    def score_body(c, carry):
        kidx_c = kidx_ref[0, c]
        sc = jnp.zeros((kc, qb), F32)
        for h in range(N_IDX_HEADS):
            d = _dot(kidx_c, qiT_ref[0, h])
            sc = sc + wT_ref[0, h:h + 1, :] * jnp.maximum(d, 0.0)
        bits = pltpu.bitcast(sc, I32)
        skey = jnp.where(bits < 0, bits ^ jnp.int32(0x7FFFFFFF), bits)
        skey = jnp.where(skey == -1, 0, skey)
        skey = jnp.where(rowi < kmax - c * kc, skey, jnp.int32(INT_MIN))
        skey_ref[c] = skey
        hi16_ref[c] = (skey >> 16).astype(I16)
        return carry

    lax.fori_loop(0, nck, score_body, 0)

    def count(pred_fn):
        def body(c, part):
            ind = jnp.where(pred_fn(c, skey_ref[c]), 1, 0)
            return part + jnp.sum(ind.reshape(groups, SUBLANES, qb), axis=0)
        part = lax.fori_loop(0, nck, body, jnp.zeros((SUBLANES, qb), I32))
        return jnp.sum(part, axis=0, keepdims=True)

    def count16(ref, pred_fn):
        pack = 2 * SUBLANES

        def body(c, part):
            ind = jnp.where(pred_fn(ref[c]), jnp.bfloat16(1), jnp.bfloat16(0)).reshape(kc // pack, pack, qb)
            terms = [ind[g] for g in range(kc // pack)]
            while len(terms) > 1:
                terms = [terms[g] + terms[g + 1] for g in range(0, len(terms), 2)]
            return part + terms[0].astype(F32)
        part = lax.fori_loop(0, nck, body, jnp.zeros((pack, qb), F32))
        return jnp.sum(part, axis=0, keepdims=True).astype(I32)

    def search16(ref, target):
        def body(it, t_u):
            cand_u = t_u | (jnp.int32(1) << (15 - it))
            cand = (cand_u - 2 ** 15).astype(I16)
            cnt = count16(ref, lambda k: k >= cand)
            return jnp.where(cnt >= target, cand_u, t_u)
        return lax.fori_loop(0, 16, body, jnp.zeros((1, qb), I32))

    hi_u = search16(hi16_ref, topk)
    hi_s = (hi_u - 2 ** 15).astype(I16)
    rest = topk - count16(hi16_ref, lambda k: k > hi_s)

    def low_body(c, carry):
        low = ((skey_ref[c] & 0xFFFF) - 2 ** 15).astype(I16)
        lo16_ref[c] = jnp.where(hi16_ref[c] == hi_s, low, jnp.int16(-2 ** 15))
        return carry

    lax.fori_loop(0, nck, low_body, 0)
    lo_u = search16(lo16_ref, rest)
    thr = (hi_u - 2 ** 15) * 2 ** 16 + lo_u
    cnt_ge = count(lambda c, sk: sk >= thr)
    cnt_gt = count(lambda c, sk: sk > thr)

    jb_ref[...] = jnp.full(jb_ref.shape, 2 ** 31 - 1, I32)

    @pl.when(jnp.max(cnt_ge) > topk)
    def _():
        need = topk - cnt_gt

        def tie_body(it, jb):
            cand = jb | (jnp.int32(1) << (idx_bits - 1 - it))
            cnt = count(lambda c, sk: jnp.logical_and(sk == thr, c * kc + rowi < cand))
            return jnp.where(cnt <= need, cand, jb)

        jb = lax.fori_loop(0, idx_bits, tie_body, jnp.zeros((1, qb), I32))
        jb_ref[...] = jnp.broadcast_to(jb, jb_ref.shape)

    jbound = jnp.where(thr == jnp.int32(INT_MIN), 0, jb_ref[0:1, :])
    thr_m1 = thr - 1

    for h in range(N_HEADS):
        m_refs[h][...] = jnp.full((1, qb), MASK_VALUE, F32)
        l_refs[h][...] = jnp.zeros((1, qb), F32)
        acc_refs[h][...] = jnp.zeros((HEAD_DIM, qb), F32)
    n_slabs = kc // SLAB

    def fold(x, op):
        return op(x.reshape(SLAB // SUBLANES, SUBLANES, qb), axis=0)

    def attn_chunk(c, near):
        sk = skey_ref[c]
        t_eff = jnp.where(rowi < jbound - c * kc, thr_m1, thr)
        madd_ref[...] = jnp.where(sk > t_eff, 0.0, MASK_VALUE)
        for h in range(N_HEADS):
            s_refs[h][...] = _dot(kb_ref[0, c, h // GROUP], qT_ref[0, h])
        alphas = []
        for h in range(N_HEADS):
            s_ref, p_ref, m_ref, l_ref = s_refs[h], p_refs[h], m_refs[h], l_refs[h]
            macc = jnp.full((SUBLANES, qb), MASK_VALUE, F32)
            for j in range(n_slabs):
                rows = pl.ds(j * SLAB, SLAB)
                x = s_ref[rows, :] + madd_ref[rows, :]
                if near:
                    t, half = (j * SLAB) // LANES, (j * SLAB) % LANES
                    parts = []
                    for u in range(qb // LANES):
                        d_tu = c * kc + t * LANES - (qoff + u * LANES)
                        delta = jnp.zeros((SLAB, LANES), F32)
                        for di, off in enumerate(offsets):
                            delta = jnp.where(d_tu == off, tab_ref[di, h, half:half + SLAB, :], delta)
                        parts.append(delta)
                    x = x + jnp.concatenate(parts, axis=1)
                s_ref[rows, :] = x
                macc = jnp.maximum(macc, fold(x, jnp.max))
            m_old = m_ref[...]
            m_new = jnp.maximum(m_old, jnp.max(macc, axis=0, keepdims=True))
            alpha = jnp.exp(m_old - m_new)
            lacc = jnp.zeros((SUBLANES, qb), F32)
            for j in range(n_slabs):
                rows = pl.ds(j * SLAB, SLAB)
                p = jnp.exp(s_ref[rows, :] - m_new)
                lacc = lacc + fold(p, jnp.sum)
                p_ref[rows, :] = p.astype(BF16)
            l_ref[...] = alpha * l_ref[...] + jnp.sum(lacc, axis=0, keepdims=True)
            m_ref[...] = m_new
            alphas.append(alpha)
        for h in range(N_HEADS):
            n = h // GROUP
            pv = _dot(vT_ref[0, c, n * HEAD_DIM:(n + 1) * HEAD_DIM, :], p_refs[h][...])
            acc_refs[h][...] = alphas[h] * acc_refs[h][...] + pv

    def far_body(c, carry):
        attn_chunk(c, False)
        return carry

    def near_body(c, carry):
        attn_chunk(c, True)
        return carry

    lax.fori_loop(0, n_far, far_body, 0)
    lax.fori_loop(n_far, nck, near_body, 0)
    oT = jnp.concatenate([acc_refs[h][...] / l_refs[h][...] for h in range(N_HEADS)], axis=0)
    o_ref[0] = oT.T.astype(o_ref.dtype)


def _attend(qT, qiT, wT, kidxb, kb, vT, tabs, *, qb, past, l_true, offsets):
    B, _, _, S = qT.shape
    nc, kc = vT.shape[1], vT.shape[3]
    assert S % qb == 0 and qb % LANES == 0 and past % LANES == 0 and kc % LANES == 0
    topk = min(TOPK_MAX, l_true // 4)
    kern = functools.partial(_attn_kernel, qb=qb, kc=kc, past=past, l_true=l_true,
                             topk=topk, offsets=offsets)
    in_specs = [
        pl.BlockSpec((1, N_HEADS, HEAD_DIM, qb), lambda b, i: (b, 0, 0, i)),
        pl.BlockSpec((1, N_IDX_HEADS, IDX_DIM, qb), lambda b, i: (b, 0, 0, i)),
        pl.BlockSpec((1, N_IDX_HEADS, qb), lambda b, i: (b, 0, i)),
        pl.BlockSpec((1,) + kidxb.shape[1:], lambda b, i: (b, 0, 0, 0)),
        pl.BlockSpec((1,) + kb.shape[1:], lambda b, i: (b, 0, 0, 0, 0)),
        pl.BlockSpec((1,) + vT.shape[1:], lambda b, i: (b, 0, 0, 0)),
        pl.BlockSpec(tabs.shape, lambda b, i: (0, 0, 0, 0)),
    ]
    return pl.pallas_call(
        kern,
        grid=(B, S // qb),
        in_specs=in_specs,
        out_specs=pl.BlockSpec((1, qb, N_HEADS * HEAD_DIM), lambda b, i: (b, i, 0)),
        out_shape=jax.ShapeDtypeStruct((B, S, N_HEADS * HEAD_DIM), BF16),
        scratch_shapes=[
            pltpu.VMEM((nc, kc, qb), I32),
            pltpu.VMEM((nc, kc, qb), I16),
            pltpu.VMEM((nc, kc, qb), I16),
            pltpu.VMEM((SUBLANES, qb), I32),
            pltpu.VMEM((kc, qb), F32),
        ] + [pltpu.VMEM((kc, qb), F32)] * N_HEADS
          + [pltpu.VMEM((kc, qb), BF16)] * N_HEADS
          + [pltpu.VMEM((1, qb), F32)] * N_HEADS
          + [pltpu.VMEM((1, qb), F32)] * N_HEADS
          + [pltpu.VMEM((HEAD_DIM, qb), F32)] * N_HEADS,
        compiler_params=_cparams(("arbitrary", "arbitrary")),
        name="dsa_attend",
    )(qT, qiT, wT, kidxb, kb, vT, tabs)


def _mix_kernel(x_ref, attn_ref, conv_ref, wo_ref, g_ref, b_ref, h_ref, *, dn_alpha):
    cat = jnp.concatenate([attn_ref[...], conv_ref[...]], axis=1)
    mixed = _dot(cat, wo_ref[...])
    h_ref[...] = _layer_norm(dn_alpha * x_ref[...] + mixed, g_ref[...], b_ref[...])


def _mix(x2, attn2, conv2, wo, g, b, *, dn_alpha):
    T, D = x2.shape
    tt = min(1024, T)
    assert T % tt == 0
    aw, cw = attn2.shape[1], conv2.shape[1]
    row = lambda i: (i, 0)
    const = lambda i: (0, 0)
    return pl.pallas_call(
        functools.partial(_mix_kernel, dn_alpha=dn_alpha),
        grid=(T // tt,),
        in_specs=[pl.BlockSpec((tt, D), row), pl.BlockSpec((tt, aw), row), pl.BlockSpec((tt, cw), row),
                  pl.BlockSpec(wo.shape, const), pl.BlockSpec((1, D), const), pl.BlockSpec((1, D), const)],
        out_specs=pl.BlockSpec((tt, D), row),
        out_shape=jax.ShapeDtypeStruct((T, D), F32),
        compiler_params=_cparams(("arbitrary",)),
        name="outproj_ln1",
    )(x2, attn2, conv2, wo, g, b)


def _route(logits):
    lane = lax.broadcasted_iota(I32, logits.shape, 1)
    work = logits
    vals, hots = [], []
    for _ in range(TOP_K):
        m = jnp.max(work, axis=1, keepdims=True)
        idx = jnp.min(jnp.where(work == m, lane, LANES), axis=1, keepdims=True)
        hot = lane == idx
        vals.append(m)
        hots.append(hot)
        work = jnp.where(hot, -jnp.inf, work)
    es = [jnp.exp(v - vals[0]) for v in vals]
    den = es[0]
    for e in es[1:]:
        den = den + e
    comb = jnp.zeros(logits.shape, F32)
    picked = jnp.zeros(logits.shape, F32)
    for e, hot in zip(es, hots):
        comb = jnp.where(hot, e / den, comb)
        picked = jnp.where(hot, 1.0, picked)
    return comb, picked


def _moe_kernel(h_ref, wr_ref, br_ref, wgu_ref, bgu_ref, wd_ref, bd_ref, g_ref, b_ref,
                y_ref, hb_ref, comb_ref, slot_ref, slotT_ref, *, dn_alpha, rows):
    e = pl.program_id(1)
    tt = h_ref.shape[0]
    ff = wd_ref.shape[1]

    @pl.when(e == 0)
    def _():
        hb = h_ref[...].astype(BF16)
        hb_ref[...] = hb
        comb, picked = _route(_dot(hb, wr_ref[...]) + br_ref[...])
        comb_ref[...] = comb
        before = (lax.broadcasted_iota(I32, (tt, tt), 1) < lax.broadcasted_iota(I32, (tt, tt), 0))
        rank = _dot(jnp.where(before, 1.0, 0.0).astype(BF16), picked.astype(BF16))
        slot = jnp.where(picked > 0.0, rank, -1.0)
        slot_ref[...] = slot
        slotT_ref[...] = slot.T
        y_ref[...] = jnp.zeros(y_ref.shape, F32)

    lane = lax.broadcasted_iota(I32, (tt, LANES), 1)
    pick_col = lambda ref: jnp.sum(jnp.where(lane == e, ref[...], 0.0), axis=1, keepdims=True)
    slot_col = pick_col(slot_ref)
    gate_col = pick_col(comb_ref)
    slot_row = slotT_ref[pl.ds(e, 1), :]
    count = jnp.sum(jnp.where(slot_row >= 0.0, 1, 0))
    sub_iota = lax.broadcasted_iota(I32, (rows, tt), 0).astype(F32)
    lane_iota = lax.broadcasted_iota(I32, (tt, rows), 1).astype(F32)

    def chunk_body(c, carry):
        base = (c * rows).astype(F32)
        gather = jnp.where(slot_row - base == sub_iota, 1.0, 0.0).astype(BF16)
        xg = _dot(gather, hb_ref[...]).astype(BF16)
        gu = _dot(xg, wgu_ref[0]) + bgu_ref[0]
        gate = jnp.minimum(gu[:, 0:ff], SWIGLU_LIMIT)
        up = jnp.clip(gu[:, ff:2 * ff], -SWIGLU_LIMIT, SWIGLU_LIMIT)
        act = (up + 1.0) * (gate * (1.0 / (1.0 + jnp.exp(-SWIGLU_ALPHA * gate))))
        y = _dot(act.astype(BF16), wd_ref[0]) + bd_ref[0]
        scatter = jnp.where(slot_col - base == lane_iota, 1.0, 0.0).astype(BF16)
        y_ref[...] += gate_col * _dot(scatter, y.astype(BF16))
        return carry

    lax.fori_loop(0, (count + rows - 1) // rows, chunk_body, 0)

    @pl.when(e == pl.num_programs(1) - 1)
    def _():
        y_ref[...] = _layer_norm(dn_alpha * h_ref[...] + y_ref[...], g_ref[...], b_ref[...])


def _moe(h2, wr, br, wgu, bgu, wd, bd, g, b, *, dn_alpha):
    T, D = h2.shape
    E, _, F2 = wgu.shape
    tt = min(1024, T)
    assert T % tt == 0
    rows = min(tt, -(-(tt * TOP_K * 5 // (E * 4)) // 16) * 16)
    row = lambda i, e: (i, 0)
    const = lambda i, e: (0, 0)
    exp3 = lambda i, e: (e, 0, 0)
    return pl.pallas_call(
        functools.partial(_moe_kernel, dn_alpha=dn_alpha, rows=rows),
        grid=(T // tt, E),
        in_specs=[pl.BlockSpec((tt, D), row), pl.BlockSpec(wr.shape, const), pl.BlockSpec(br.shape, const),
                  pl.BlockSpec((1, D, F2), exp3), pl.BlockSpec((1, 1, F2), exp3),
                  pl.BlockSpec((1, F2 // 2, D), exp3), pl.BlockSpec((1, 1, D), exp3),
                  pl.BlockSpec((1, D), const), pl.BlockSpec((1, D), const)],
        out_specs=pl.BlockSpec((tt, D), row),
        out_shape=jax.ShapeDtypeStruct((T, D), F32),
        scratch_shapes=[pltpu.VMEM((tt, D), BF16), pltpu.VMEM((tt, LANES), F32),
                        pltpu.VMEM((tt, LANES), F32), pltpu.VMEM((LANES, tt), F32)],
        compiler_params=_cparams(("arbitrary", "arbitrary")),
        name="moe_ln2",
    )(h2, wr, br, wgu, bgu, wd, bd, g, b)


def _split_gu_kernel(w_ref, perm_ref, o_ref):
    o_ref[0] = _dot(w_ref[0].astype(BF16), perm_ref[...]).astype(BF16)


def _split_gu(w_gu):
    E, D, N = w_gu.shape
    tr = min(512, D)
    j = jnp.arange(N, dtype=I32)
    src = jnp.where(j < N // 2, 2 * j, 2 * (j - N // 2) + 1)
    perm = (jnp.arange(N, dtype=I32)[:, None] == src[None, :]).astype(BF16)
    return pl.pallas_call(
        _split_gu_kernel,
        grid=(E, D // tr),
        in_specs=[pl.BlockSpec((1, tr, N), lambda e, r: (e, r, 0)),
                  pl.BlockSpec((N, N), lambda e, r: (0, 0))],
        out_specs=pl.BlockSpec((1, tr, N), lambda e, r: (e, r, 0)),
        out_shape=jax.ShapeDtypeStruct((E, D, N), BF16),
        compiler_params=_cparams(("arbitrary", "arbitrary")),
        name="split_gate_up",
    )(w_gu, perm)


def _prep_weights(w_in, w_conv, w_out, ln1_g, ln1_b, w_router, b_router, w_gu, b_gu, w_down, b_down,
                  ln2_g, ln2_b):
    aw = N_HEADS * HEAD_DIM
    kv = N_KV_HEADS * HEAD_DIM
    iw = N_IDX_HEADS * IDX_DIM
    D = w_in.shape[0]
    cw = D - aw
    o_q, o_k, o_v = 0, aw, aw + kv
    o_qi = o_v + kv
    o_ki = o_qi + iw
    o_wi = o_ki + IDX_DIM
    o_c = o_wi + N_IDX_HEADS
    wa = jnp.concatenate([w_in[:, o_q:o_k], w_in[:, o_qi:o_ki], w_in[:, o_k:o_v], w_in[:, o_v:o_qi]],
                         axis=1).astype(BF16)
    ws = jnp.pad(w_in[:, o_ki:o_c], ((0, 0), (0, LANES - IDX_DIM - N_IDX_HEADS))).astype(BF16)
    wc = w_in[:, o_c:o_c + 3 * cw].astype(BF16)
    E = w_router.shape[1]
    wr = jnp.pad(w_router, ((0, 0), (0, LANES - E))).astype(BF16)
    br = jnp.pad(b_router, (0, LANES - E), constant_values=MASK_VALUE).reshape(1, LANES)
    F = w_gu.shape[2] // 2
    return dict(
        wa=wa, ws=ws, wc=wc, wconv=w_conv, wo=w_out.astype(BF16),
        ln1_g=ln1_g.reshape(1, D), ln1_b=ln1_b.reshape(1, D),
        wr=wr, br=br,
        wgu=_split_gu(w_gu),
        bgu=jnp.concatenate([b_gu[:, 0::2], b_gu[:, 1::2]], axis=1).reshape(E, 1, 2 * F),
        wd=w_down.astype(BF16), bd=b_down.reshape(E, 1, D),
        ln2_g=ln2_g.reshape(1, D), ln2_b=ln2_b.reshape(1, D),
    )


def _decode_layouts(pq, small, k_all, v_all, kidx_all, kc, qb):
    B, S, _ = pq.shape
    aw = N_HEADS * HEAD_DIM
    L, kv = k_all.shape[1], k_all.shape[2]
    lp = -(-L // kc) * kc
    nc = lp // kc
    padk = lambda a: jnp.pad(a, ((0, 0), (0, lp - L), (0, 0)))
    padq = lambda a: jnp.pad(a, ((0, 0), (0, 0), (0, 0), (0, qb - S)))
    q = pq[:, :, 0:aw] * (HEAD_DIM ** -0.5)
    qT = padq(q.reshape(B, S, N_HEADS, HEAD_DIM).transpose(0, 2, 3, 1)).astype(BF16)
    qiT = padq(pq[:, :, aw:].reshape(B, S, N_IDX_HEADS, IDX_DIM).transpose(0, 2, 3, 1)).astype(BF16)
    wT = jnp.pad(small[:, :, IDX_DIM:IDX_DIM + N_IDX_HEADS].transpose(0, 2, 1), ((0, 0), (0, 0), (0, qb - S)))
    kb = padk(k_all).reshape(B, nc, kc, N_KV_HEADS, HEAD_DIM).transpose(0, 1, 3, 2, 4).astype(BF16)
    vT = padk(v_all).reshape(B, nc, kc, kv).transpose(0, 1, 3, 2).astype(BF16)
    kidxb = padk(kidx_all).reshape(B, nc, kc, IDX_DIM).astype(BF16)
    return qT, qiT, wT, kidxb, kb, vT


def _layer(x, past_k, past_v, past_kidx, conv_buf, rel_bias, w, *, dn_alpha):
    B, S, D = x.shape
    prefill = past_k is None
    outs = _project(x, conv_buf, w["wa"], w["ws"], w["wc"], w["wconv"], attn_layouts=prefill)
    k, v, small, conv, new_buf = outs[:5]
    k_idx = small[:, :, 0:IDX_DIM]
    if prefill:
        past, l_true = 0, S
        qb = min(QUERY_BLOCK, S)
        kb, vT, kidxb, qT, qiT, wT = outs[5:]
    else:
        past = past_k.shape[1]
        l_true = past + S
        qb = -(-S // LANES) * LANES
        qT, qiT, wT, kidxb, kb, vT = _decode_layouts(
            outs[5], small, jnp.concatenate([past_k, k], axis=1), jnp.concatenate([past_v, v], axis=1),
            jnp.concatenate([past_kidx, k_idx], axis=1), KEY_CHUNK, qb)
    offsets = (-LANES, 0)
    tabs = _bias_tables(rel_bias, offsets)
    attn = _attend(qT, qiT, wT, kidxb, kb, vT, tabs, qb=qb, past=past, l_true=l_true, offsets=offsets)
    attn = attn[:, :S]
    h = _mix(x.reshape(B * S, D), attn.reshape(B * S, -1), conv.reshape(B * S, -1), w["wo"],
             w["ln1_g"], w["ln1_b"], dn_alpha=dn_alpha)
    y = _moe(h, w["wr"], w["br"], w["wgu"], w["bgu"], w["wd"], w["bd"],
             w["ln2_g"], w["ln2_b"], dn_alpha=dn_alpha)
    return (y.reshape(B, S, D), k.reshape(B, S, N_KV_HEADS, HEAD_DIM), v.reshape(B, S, N_KV_HEADS, HEAD_DIM),
            k_idx, new_buf)


def kernel(x_prompt, x_sample, cache_k, cache_v, cache_kidx, state_conv, rel_bias, w_in, w_conv, w_out,
           ln1_g, ln1_b, w_router, b_router, w_gu, b_gu, w_down, b_down, ln2_g, ln2_b):
    depth = w_in.shape[0]
    assert depth == 1
    dn_alpha = (2 * depth) ** 0.25
    kv = N_KV_HEADS * HEAD_DIM
    w = _prep_weights(w_in[0], w_conv[0], w_out[0], ln1_g[0], ln1_b[0], w_router[0], b_router[0],
                      w_gu[0], b_gu[0], w_down[0], b_down[0], ln2_g[0], ln2_b[0])
    Bp = x_prompt.shape[0]
    cw = w_conv.shape[2]
    zero_buf = jnp.zeros((Bp, CONV_K - 1, cw), F32)
    yp, k1, v1, i1, c1 = _layer(x_prompt, None, None, None, zero_buf, rel_bias, w, dn_alpha=dn_alpha)
    Bs, P = cache_k.shape[1], cache_k.shape[2]
    ys, k2, v2, i2, c2 = _layer(x_sample, cache_k[0].reshape(Bs, P, kv), cache_v[0].reshape(Bs, P, kv),
                                cache_kidx[0], state_conv[0], rel_bias, w, dn_alpha=dn_alpha)
    return (yp, ys, k1[None], v1[None], i1[None], c1[None], k2[None], v2[None], i2[None], c2[None])
```

```python
import functools
import math

import jax
import jax.numpy as jnp
from jax import lax
from jax.experimental import pallas as pl
from jax.experimental.pallas import tpu as pltpu

F32 = jnp.float32
BF16 = jnp.bfloat16
I32 = jnp.int32
I16 = jnp.int16

CHUNK = 64
N_HEADS = 8
HEAD_DIM = 64
N_KV_HEADS = 2
GROUP = N_HEADS // N_KV_HEADS
N_IDX_HEADS = 8
IDX_DIM = 32
TOPK_MAX = 256
CONV_K = 3
N_BUCKETS = 32
MAX_DISTANCE = 128
N_EXPERTS = 32
TOP_K = 4
SWIGLU_LIMIT = 7.0
SWIGLU_ALPHA = 1.702
LN_EPS = 1e-5
MASK_VALUE = -1e30
QK_SCALE = HEAD_DIM ** -0.5 * math.log2(math.e)

LANES = 128
SUBLANES = 8
KEY_CHUNK = 512
QUERY_BLOCK = 256
SLAB = 64
INT_MIN = -(2 ** 31)
VMEM_LIMIT = 56 * 1024 * 1024


def _cparams(sem):
    return pltpu.CompilerParams(dimension_semantics=sem, vmem_limit_bytes=VMEM_LIMIT)


def _dot(a, b):
    return jnp.dot(a, b, preferred_element_type=F32)


def _layer_norm(z, g, b):
    mu = jnp.mean(z, axis=-1, keepdims=True)
    zc = z - mu
    var = jnp.mean(zc * zc, axis=-1, keepdims=True)
    return zc * lax.rsqrt(var + LN_EPS) * g + b


def _bucket_thresholds():
    nb = N_BUCKETS // 2
    max_exact = nb // 2
    out = []
    for j in range(1, nb - max_exact):
        out.append(math.ceil(max_exact * (MAX_DISTANCE / max_exact) ** (j / (nb - max_exact)) - 1e-9))
    return tuple(out)


def _bias_table_kernel(rel_ref, tab_ref, *, offsets):
    nb = N_BUCKETS // 2
    max_exact = nb // 2
    thr = _bucket_thresholds()
    ii = lax.broadcasted_iota(I32, (LANES, LANES), 0)
    jj = lax.broadcasted_iota(I32, (LANES, LANES), 1)
    for d, off in enumerate(offsets):
        rel = off + ii - jj
        n = jnp.abs(rel)
        large = jnp.full((LANES, LANES), max_exact, I32)
        for t in thr:
            large = large + jnp.where(n >= t, 1, 0)
        bucket = jnp.where(rel > 0, nb, 0) + jnp.where(n < max_exact, n, large)
        for h in range(N_HEADS):
            acc = jnp.zeros((LANES, LANES), F32)
            for b in range(N_BUCKETS):
                acc = jnp.where(bucket == b, rel_ref[b, h], acc)
            tab_ref[d, h] = (acc - rel_ref[nb - 1, h]) * math.log2(math.e)


def _bias_tables(rel_bias, offsets):
    return pl.pallas_call(
        functools.partial(_bias_table_kernel, offsets=offsets),
        out_shape=jax.ShapeDtypeStruct((len(offsets), N_HEADS, LANES, LANES), F32),
        in_specs=[pl.BlockSpec(memory_space=pltpu.SMEM)],
        out_specs=pl.BlockSpec(memory_space=pltpu.VMEM),
        name="bias_tables",
    )(rel_bias)


def _proj_kernel(x_ref, wa_ref, ws_ref, wc_ref, wconv_ref, buf_ref,
                 k_ref, v_ref, small_ref, conv_ref, nbuf_ref, *rest, ts, attn_layouts):
    s = pl.program_id(1)
    xb = x_ref[0].astype(BF16)
    pa = _dot(xb, wa_ref[...])
    ps = _dot(xb, ws_ref[...])
    pc = _dot(xb, wc_ref[...])
    aw = N_HEADS * HEAD_DIM
    iw = N_IDX_HEADS * IDX_DIM
    kv = N_KV_HEADS * HEAD_DIM
    k = pa[:, aw + iw:aw + iw + kv]
    v = pa[:, aw + iw + kv:aw + iw + 2 * kv]
    k_ref[0] = k
    v_ref[0] = v
    small_ref[0] = ps
    scale = QK_SCALE
    if attn_layouts:
        kb_ref, vT_ref, kidxb_ref, qT_ref, qiT_ref, wT_ref, carry_ref = rest
        for n in range(N_KV_HEADS):
            kb_ref[0, 0, n] = k[:, n * HEAD_DIM:(n + 1) * HEAD_DIM].astype(BF16)
        vT_ref[0, 0] = v.T.astype(BF16)
        kidxb_ref[0, 0] = ps[:, 0:IDX_DIM].astype(BF16)
        qT = (pa[:, 0:aw] * scale).T.astype(BF16)
        for h in range(N_HEADS):
            qT_ref[0, h] = qT[h * HEAD_DIM:(h + 1) * HEAD_DIM]
        qiT = pa[:, aw:aw + iw].T.astype(BF16)
        for h in range(N_IDX_HEADS):
            qiT_ref[0, h] = qiT[h * IDX_DIM:(h + 1) * IDX_DIM]
        wT_ref[0] = ps.T[IDX_DIM:IDX_DIM + N_IDX_HEADS]
    else:
        pq_ref, carry_ref = rest
        pq_ref[0] = pa[:, 0:aw + iw]

    cw = pc.shape[1] // 3
    u = pc[:, cw:2 * cw] * pc[:, 2 * cw:3 * cw]

    @pl.when(s == 0)
    def _():
        carry_ref[6:8, :] = buf_ref[0]

    carry_ref[8:8 + ts, :] = u
    y = (carry_ref[6:6 + ts, :] * wconv_ref[0:1, :]
         + carry_ref[7:7 + ts, :] * wconv_ref[1:2, :]
         + u * wconv_ref[2:3, :])
    conv_ref[0] = (pc[:, 0:cw] * y).astype(BF16)
    nb = carry_ref[ts + 6:ts + 8, :]
    nbuf_ref[0] = nb
    carry_ref[6:8, :] = nb


def _project(x, conv_buf, wa, ws, wc, wconv, *, attn_layouts):
    B, S, D = x.shape
    ts = min(KEY_CHUNK, S)
    assert S % ts == 0 and S >= CONV_K - 1
    ns = S // ts
    cw = wc.shape[1] // 3
    kv = N_KV_HEADS * HEAD_DIM
    aw = N_HEADS * HEAD_DIM
    iw = N_IDX_HEADS * IDX_DIM
    out_shape = [
        jax.ShapeDtypeStruct((B, S, kv), F32),
        jax.ShapeDtypeStruct((B, S, kv), F32),
        jax.ShapeDtypeStruct((B, S, LANES), F32),
        jax.ShapeDtypeStruct((B, S, cw), BF16),
        jax.ShapeDtypeStruct((B, CONV_K - 1, cw), F32),
    ]
    out_specs = [
        pl.BlockSpec((1, ts, kv), lambda b, s: (b, s, 0)),
        pl.BlockSpec((1, ts, kv), lambda b, s: (b, s, 0)),
        pl.BlockSpec((1, ts, LANES), lambda b, s: (b, s, 0)),
        pl.BlockSpec((1, ts, cw), lambda b, s: (b, s, 0)),
        pl.BlockSpec((1, CONV_K - 1, cw), lambda b, s: (b, 0, 0)),
    ]
    if attn_layouts:
        out_shape += [
            jax.ShapeDtypeStruct((B, ns, N_KV_HEADS, ts, HEAD_DIM), BF16),
            jax.ShapeDtypeStruct((B, ns, kv, ts), BF16),
            jax.ShapeDtypeStruct((B, ns, ts, IDX_DIM), BF16),
            jax.ShapeDtypeStruct((B, N_HEADS, HEAD_DIM, S), BF16),
            jax.ShapeDtypeStruct((B, N_IDX_HEADS, IDX_DIM, S), BF16),
            jax.ShapeDtypeStruct((B, N_IDX_HEADS, S), F32),
        ]
        out_specs += [
            pl.BlockSpec((1, 1, N_KV_HEADS, ts, HEAD_DIM), lambda b, s: (b, s, 0, 0, 0)),
            pl.BlockSpec((1, 1, kv, ts), lambda b, s: (b, s, 0, 0)),
            pl.BlockSpec((1, 1, ts, IDX_DIM), lambda b, s: (b, s, 0, 0)),
            pl.BlockSpec((1, N_HEADS, HEAD_DIM, ts), lambda b, s: (b, 0, 0, s)),
            pl.BlockSpec((1, N_IDX_HEADS, IDX_DIM, ts), lambda b, s: (b, 0, 0, s)),
            pl.BlockSpec((1, N_IDX_HEADS, ts), lambda b, s: (b, 0, s)),
        ]
    else:
        out_shape += [jax.ShapeDtypeStruct((B, S, aw + iw), F32)]
        out_specs += [pl.BlockSpec((1, ts, aw + iw), lambda b, s: (b, s, 0))]
    const2 = lambda b, s: (0, 0)
    in_specs = [
        pl.BlockSpec((1, ts, D), lambda b, s: (b, s, 0)),
        pl.BlockSpec(wa.shape, const2),
        pl.BlockSpec(ws.shape, const2),
        pl.BlockSpec(wc.shape, const2),
        pl.BlockSpec(wconv.shape, const2),
        pl.BlockSpec((1, CONV_K - 1, cw), lambda b, s: (b, 0, 0)),
    ]
    return pl.pallas_call(
        functools.partial(_proj_kernel, ts=ts, attn_layouts=attn_layouts),
        grid=(B, ns),
        in_specs=in_specs,
        out_specs=tuple(out_specs),
        out_shape=tuple(out_shape),
        scratch_shapes=[pltpu.VMEM((ts + 8, cw), F32)],
        compiler_params=_cparams(("arbitrary", "arbitrary")),
        name="proj_conv",
    )(x, wa, ws, wc, wconv, conv_buf)


def _attn_kernel(qT_ref, qiT_ref, wT_ref, kidx_ref, kb_ref, vT_ref, tab_ref,
                 o_ref, skey_ref, hi16_ref, lo16_ref, jb_ref, madd_ref, *head_refs,
                 qb, kc, past, l_true, topk, offsets):
    s_refs, p_refs, m_refs, l_refs, acc_refs = (head_refs[g * N_HEADS:(g + 1) * N_HEADS] for g in range(5))
    i = pl.program_id(1)
    qoff = past + i * qb
    adm_end = jnp.minimum(((qoff + qb - 1) // CHUNK + 1) * CHUNK, l_true)
    nck = (adm_end + kc - 1) // kc
    n_far = jnp.maximum(qoff - LANES, 0) // kc
    idx_bits = int(l_true).bit_length()
    groups = kc // SUBLANES

    rowi = lax.broadcasted_iota(I32, (kc, qb), 0)
    qpos = qoff + lax.broadcasted_iota(I32, (1, qb), 1)
    kmax = jnp.minimum((qpos // CHUNK + 1) * CHUNK, l_true)

    def score_body(c, carry):
        kidx_c = kidx_ref[0, c]
        sc = jnp.zeros((kc, qb), F32)
        for h in range(N_IDX_HEADS):
            d = _dot(kidx_c, qiT_ref[0, h])
            sc = sc + wT_ref[0, h:h + 1, :] * jnp.maximum(d, 0.0)
        bits = pltpu.bitcast(sc, I32)
        skey = jnp.where(bits < 0, bits ^ jnp.int32(0x7FFFFFFF), bits)
        skey = jnp.where(skey == -1, 0, skey)
        skey = jnp.where(rowi < kmax - c * kc, skey, jnp.int32(INT_MIN))
        skey_ref[c] = skey
        hi16_ref[c] = (skey >> 16).astype(I16)
        return carry

    lax.fori_loop(0, nck, score_body, 0)

    def count(pred_fn):
        def body(c, part):
            ind = jnp.where(pred_fn(c, skey_ref[c]), 1, 0)
            return part + jnp.sum(ind.reshape(groups, SUBLANES, qb), axis=0)
        part = lax.fori_loop(0, nck, body, jnp.zeros((SUBLANES, qb), I32))
        return jnp.sum(part, axis=0, keepdims=True)

    def count16(ref, pred_fn):
        pack = 2 * SUBLANES

        def body(c, part):
            ind = jnp.where(pred_fn(ref[c]), jnp.bfloat16(1), jnp.bfloat16(0)).reshape(kc // pack, pack, qb)
            terms = [ind[g] for g in range(kc // pack)]
            while len(terms) > 1:
                terms = [terms[g] + terms[g + 1] for g in range(0, len(terms), 2)]
            return part + terms[0].astype(F32)
        part = lax.fori_loop(0, nck, body, jnp.zeros((pack, qb), F32))
        return jnp.sum(part, axis=0, keepdims=True).astype(I32)

    def search16(ref, target):
        def body(it, t_u):
            cand_u = t_u | (jnp.int32(1) << (15 - it))
            cand = (cand_u - 2 ** 15).astype(I16)
            cnt = count16(ref, lambda k: k >= cand)
            return jnp.where(cnt >= target, cand_u, t_u)
        return lax.fori_loop(0, 16, body, jnp.zeros((1, qb), I32))

    hi_u = search16(hi16_ref, topk)
    hi_s = (hi_u - 2 ** 15).astype(I16)
    rest = topk - count16(hi16_ref, lambda k: k > hi_s)

    def low_body(c, carry):
        low = ((skey_ref[c] & 0xFFFF) - 2 ** 15).astype(I16)
        lo16_ref[c] = jnp.where(hi16_ref[c] == hi_s, low, jnp.int16(-2 ** 15))
        return carry

    lax.fori_loop(0, nck, low_body, 0)
    lo_u = search16(lo16_ref, rest)
    thr = (hi_u - 2 ** 15) * 2 ** 16 + lo_u
    cnt_ge = count(lambda c, sk: sk >= thr)
    cnt_gt = count(lambda c, sk: sk > thr)

    jb_ref[...] = jnp.full(jb_ref.shape, 2 ** 31 - 1, I32)

    @pl.when(jnp.max(cnt_ge) > topk)
    def _():
        need = topk - cnt_gt

        def tie_body(it, jb):
            cand = jb | (jnp.int32(1) << (idx_bits - 1 - it))
            cnt = count(lambda c, sk: jnp.logical_and(sk == thr, c * kc + rowi < cand))
            return jnp.where(cnt <= need, cand, jb)

        jb = lax.fori_loop(0, idx_bits, tie_body, jnp.zeros((1, qb), I32))
        jb_ref[...] = jnp.broadcast_to(jb, jb_ref.shape)

    jbound = jnp.where(thr == jnp.int32(INT_MIN), 0, jb_ref[0:1, :])
    thr_m1 = thr - 1

    for h in range(N_HEADS):
        m_refs[h][...] = jnp.full((1, qb), MASK_VALUE, F32)
        l_refs[h][...] = jnp.zeros((1, qb), F32)
        acc_refs[h][...] = jnp.zeros((HEAD_DIM, qb), F32)
    n_slabs = kc // SLAB

    def fold(x, op):
        return op(x.reshape(SLAB // SUBLANES, SUBLANES, qb), axis=0)

    def attn_chunk(c, near):
        sk = skey_ref[c]
        t_eff = jnp.where(rowi < jbound - c * kc, thr_m1, thr)
        madd_ref[...] = jnp.where(sk > t_eff, 0.0, MASK_VALUE)
        maccs = []
        for h in range(N_HEADS):
            x = _dot(kb_ref[0, c, h // GROUP], qT_ref[0, h]) + madd_ref[...]
            if near:
                tiles = []
                for t in range(kc // LANES):
                    row = []
                    for u in range(qb // LANES):
                        d_tu = c * kc + t * LANES - (qoff + u * LANES)
                        delta = jnp.zeros((LANES, LANES), F32)
                        for di, off in enumerate(offsets):
                            delta = jnp.where(d_tu == off, tab_ref[di, h], delta)
                        row.append(delta)
                    tiles.append(jnp.concatenate(row, axis=1))
                x = x + jnp.concatenate(tiles, axis=0)
            s_refs[h][...] = x
            maccs.append(jnp.max(x.reshape(groups, SUBLANES, qb), axis=0))
        alphas = []
        for h in range(N_HEADS):
            s_ref, p_ref, m_ref = s_refs[h], p_refs[h], m_refs[h]
            m_old = m_ref[...]
            m_new = jnp.maximum(m_old, jnp.max(maccs[h], axis=0, keepdims=True))
            for j in range(n_slabs):
                rows = pl.ds(j * SLAB, SLAB)
                p_ref[rows, :] = jnp.exp2(s_ref[rows, :] - m_new).astype(BF16)
            m_ref[...] = m_new
            alphas.append(jnp.exp2(m_old - m_new))
        ones = jnp.ones((2 * SUBLANES, kc), BF16)
        v_aug = [jnp.concatenate([vT_ref[0, c, n * HEAD_DIM:(n + 1) * HEAD_DIM, :], ones], axis=0)
                 for n in range(N_KV_HEADS)]
        for h in range(N_HEADS):
            pv = _dot(v_aug[h // GROUP], p_refs[h][...])
            acc_refs[h][...] = alphas[h] * acc_refs[h][...] + pv[0:HEAD_DIM]
            l_refs[h][...] = alphas[h] * l_refs[h][...] + pv[HEAD_DIM:HEAD_DIM + 1]

    def far_body(c, carry):
        attn_chunk(c, False)
        return carry

    def near_body(c, carry):
        attn_chunk(c, True)
        return carry

    lax.fori_loop(0, n_far, far_body, 0)
    lax.fori_loop(n_far, nck, near_body, 0)
    oT = jnp.concatenate([acc_refs[h][...] / l_refs[h][...] for h in range(N_HEADS)], axis=0)
    o_ref[0] = oT.T.astype(o_ref.dtype)


def _attend(qT, qiT, wT, kidxb, kb, vT, tabs, *, qb, past, l_true, offsets):
    B, _, _, S = qT.shape
    nc, kc = vT.shape[1], vT.shape[3]
    assert S % qb == 0 and qb % LANES == 0 and past % LANES == 0 and kc % LANES == 0
    topk = min(TOPK_MAX, l_true // 4)
    kern = functools.partial(_attn_kernel, qb=qb, kc=kc, past=past, l_true=l_true,
                             topk=topk, offsets=offsets)
    in_specs = [
        pl.BlockSpec((1, N_HEADS, HEAD_DIM, qb), lambda b, i: (b, 0, 0, i)),
        pl.BlockSpec((1, N_IDX_HEADS, IDX_DIM, qb), lambda b, i: (b, 0, 0, i)),
        pl.BlockSpec((1, N_IDX_HEADS, qb), lambda b, i: (b, 0, i)),
        pl.BlockSpec((1,) + kidxb.shape[1:], lambda b, i: (b, 0, 0, 0)),
        pl.BlockSpec((1,) + kb.shape[1:], lambda b, i: (b, 0, 0, 0, 0)),
        pl.BlockSpec((1,) + vT.shape[1:], lambda b, i: (b, 0, 0, 0)),
        pl.BlockSpec(tabs.shape, lambda b, i: (0, 0, 0, 0)),
    ]
    return pl.pallas_call(
        kern,
        grid=(B, S // qb),
        in_specs=in_specs,
        out_specs=pl.BlockSpec((1, qb, N_HEADS * HEAD_DIM), lambda b, i: (b, i, 0)),
        out_shape=jax.ShapeDtypeStruct((B, S, N_HEADS * HEAD_DIM), BF16),
        scratch_shapes=[
            pltpu.VMEM((nc, kc, qb), I32),
            pltpu.VMEM((nc, kc, qb), I16),
            pltpu.VMEM((nc, kc, qb), I16),
            pltpu.VMEM((SUBLANES, qb), I32),
            pltpu.VMEM((kc, qb), F32),
        ] + [pltpu.VMEM((kc, qb), F32)] * N_HEADS
          + [pltpu.VMEM((kc, qb), BF16)] * N_HEADS
          + [pltpu.VMEM((1, qb), F32)] * N_HEADS
          + [pltpu.VMEM((1, qb), F32)] * N_HEADS
          + [pltpu.VMEM((HEAD_DIM, qb), F32)] * N_HEADS,
        compiler_params=_cparams(("arbitrary", "arbitrary")),
        name="dsa_attend",
    )(qT, qiT, wT, kidxb, kb, vT, tabs)


def _mix_kernel(x_ref, attn_ref, conv_ref, wo_ref, g_ref, b_ref, h_ref, *, dn_alpha):
    cat = jnp.concatenate([attn_ref[...], conv_ref[...]], axis=1)
    mixed = _dot(cat, wo_ref[...])
    h_ref[...] = _layer_norm(dn_alpha * x_ref[...] + mixed, g_ref[...], b_ref[...])


def _mix(x2, attn2, conv2, wo, g, b, *, dn_alpha):
    T, D = x2.shape
    tt = min(1024, T)
    assert T % tt == 0
    aw, cw = attn2.shape[1], conv2.shape[1]
    row = lambda i: (i, 0)
    const = lambda i: (0, 0)
    return pl.pallas_call(
        functools.partial(_mix_kernel, dn_alpha=dn_alpha),
        grid=(T // tt,),
        in_specs=[pl.BlockSpec((tt, D), row), pl.BlockSpec((tt, aw), row), pl.BlockSpec((tt, cw), row),
                  pl.BlockSpec(wo.shape, const), pl.BlockSpec((1, D), const), pl.BlockSpec((1, D), const)],
        out_specs=pl.BlockSpec((tt, D), row),
        out_shape=jax.ShapeDtypeStruct((T, D), F32),
        compiler_params=_cparams(("arbitrary",)),
        name="outproj_ln1",
    )(x2, attn2, conv2, wo, g, b)


def _route(logits):
    lane = lax.broadcasted_iota(I32, logits.shape, 1)
    work = logits
    vals, hots = [], []
    for _ in range(TOP_K):
        m = jnp.max(work, axis=1, keepdims=True)
        idx = jnp.min(jnp.where(work == m, lane, LANES), axis=1, keepdims=True)
        hot = lane == idx
        vals.append(m)
        hots.append(hot)
        work = jnp.where(hot, -jnp.inf, work)
    es = [jnp.exp(v - vals[0]) for v in vals]
    den = es[0]
    for e in es[1:]:
        den = den + e
    comb = jnp.zeros(logits.shape, F32)
    picked = jnp.zeros(logits.shape, F32)
    for e, hot in zip(es, hots):
        comb = jnp.where(hot, e / den, comb)
        picked = jnp.where(hot, 1.0, picked)
    return comb, picked


def _moe_kernel(h_ref, wr_ref, br_ref, wgu_ref, bgu_ref, wd_ref, bd_ref, g_ref, b_ref,
                y_ref, hb_ref, combT_ref, slotT_ref, *, dn_alpha, rows):
    e = pl.program_id(1)
    tt = h_ref.shape[0]
    ff = wd_ref.shape[1]

    @pl.when(e == 0)
    def _():
        hb = h_ref[...].astype(BF16)
        hb_ref[...] = hb
        comb, picked = _route(_dot(hb, wr_ref[...]) + br_ref[...])
        combT_ref[...] = comb.T
        before = (lax.broadcasted_iota(I32, (tt, tt), 1) < lax.broadcasted_iota(I32, (tt, tt), 0))
        rank = _dot(jnp.where(before, 1.0, 0.0).astype(BF16), picked.astype(BF16))
        slotT_ref[...] = jnp.where(picked > 0.0, rank, -1.0).T
        y_ref[...] = jnp.zeros(y_ref.shape, F32)

    slot_row = slotT_ref[pl.ds(e, 1), :]
    gate_row = combT_ref[pl.ds(e, 1), :]
    count = jnp.sum(jnp.where(slot_row >= 0.0, 1, 0))
    sub_iota = lax.broadcasted_iota(I32, (rows, tt), 0).astype(F32)

    def chunk_body(c, carry):
        base = (c * rows).astype(F32)
        hit = slot_row - base == sub_iota
        gather = jnp.where(hit, 1.0, 0.0).astype(BF16)
        row_gate = jnp.sum(jnp.where(hit, gate_row, 0.0), axis=1, keepdims=True)
        xg = _dot(gather, hb_ref[...]).astype(BF16)
        gu = _dot(xg, wgu_ref[0]) + bgu_ref[0]
        gate = jnp.minimum(gu[:, 0:ff], SWIGLU_LIMIT)
        up = jnp.clip(gu[:, ff:2 * ff], -SWIGLU_LIMIT, SWIGLU_LIMIT)
        act = (up + 1.0) * (gate * (1.0 / (1.0 + jnp.exp(-SWIGLU_ALPHA * gate))))
        y = (_dot(act.astype(BF16), wd_ref[0]) + bd_ref[0]) * row_gate
        y_ref[...] += lax.dot_general(gather, y.astype(BF16), (((0,), (0,)), ((), ())),
                                      preferred_element_type=F32)
        return carry

    lax.fori_loop(0, (count + rows - 1) // rows, chunk_body, 0)

    @pl.when(e == pl.num_programs(1) - 1)
    def _():
        y_ref[...] = _layer_norm(dn_alpha * h_ref[...] + y_ref[...], g_ref[...], b_ref[...])


def _moe(h2, wr, br, wgu, bgu, wd, bd, g, b, *, dn_alpha):
    T, D = h2.shape
    E, _, F2 = wgu.shape
    tt = min(1024, T)
    assert T % tt == 0
    rows = min(tt, -(-(tt * TOP_K * 5 // (E * 4)) // 16) * 16)
    row = lambda i, e: (i, 0)
    const = lambda i, e: (0, 0)
    exp3 = lambda i, e: (e, 0, 0)
    return pl.pallas_call(
        functools.partial(_moe_kernel, dn_alpha=dn_alpha, rows=rows),
        grid=(T // tt, E),
        in_specs=[pl.BlockSpec((tt, D), row), pl.BlockSpec(wr.shape, const), pl.BlockSpec(br.shape, const),
                  pl.BlockSpec((1, D, F2), exp3), pl.BlockSpec((1, 1, F2), exp3),
                  pl.BlockSpec((1, F2 // 2, D), exp3), pl.BlockSpec((1, 1, D), exp3),
                  pl.BlockSpec((1, D), const), pl.BlockSpec((1, D), const)],
        out_specs=pl.BlockSpec((tt, D), row),
        out_shape=jax.ShapeDtypeStruct((T, D), F32),
        scratch_shapes=[pltpu.VMEM((tt, D), BF16), pltpu.VMEM((LANES, tt), F32), pltpu.VMEM((LANES, tt), F32)],
        compiler_params=_cparams(("arbitrary", "arbitrary")),
        name="moe_ln2",
    )(h2, wr, br, wgu, bgu, wd, bd, g, b)


def _split_gu_kernel(w_ref, perm_ref, o_ref, *, band):
    n = w_ref.shape[2]
    for m in range(n // band):
        both = _dot(w_ref[0, :, m * band:(m + 1) * band].astype(BF16), perm_ref[...]).astype(BF16)
        o_ref[0, :, m * (band // 2):(m + 1) * (band // 2)] = both[:, 0:band // 2]
        o_ref[0, :, n // 2 + m * (band // 2):n // 2 + (m + 1) * (band // 2)] = both[:, band // 2:band]


def _split_gu(w_gu):
    E, D, N = w_gu.shape
    tr = min(512, D)
    band = min(512, N)
    j = jnp.arange(band, dtype=I32)
    src = jnp.where(j < band // 2, 2 * j, 2 * (j - band // 2) + 1)
    perm = (jnp.arange(band, dtype=I32)[:, None] == src[None, :]).astype(BF16)
    return pl.pallas_call(
        functools.partial(_split_gu_kernel, band=band),
        grid=(E, D // tr),
        in_specs=[pl.BlockSpec((1, tr, N), lambda e, r: (e, r, 0)),
                  pl.BlockSpec((band, band), lambda e, r: (0, 0))],
        out_specs=pl.BlockSpec((1, tr, N), lambda e, r: (e, r, 0)),
        out_shape=jax.ShapeDtypeStruct((E, D, N), BF16),
        compiler_params=_cparams(("arbitrary", "arbitrary")),
        name="split_gate_up",
    )(w_gu, perm)


def _prep_weights(w_in, w_conv, w_out, ln1_g, ln1_b, w_router, b_router, w_gu, b_gu, w_down, b_down,
                  ln2_g, ln2_b):
    aw = N_HEADS * HEAD_DIM
    kv = N_KV_HEADS * HEAD_DIM
    iw = N_IDX_HEADS * IDX_DIM
    D = w_in.shape[0]
    cw = D - aw
    o_q, o_k, o_v = 0, aw, aw + kv
    o_qi = o_v + kv
    o_ki = o_qi + iw
    o_wi = o_ki + IDX_DIM
    o_c = o_wi + N_IDX_HEADS
    wa = jnp.concatenate([w_in[:, o_q:o_k], w_in[:, o_qi:o_ki], w_in[:, o_k:o_v], w_in[:, o_v:o_qi]],
                         axis=1).astype(BF16)
    ws = jnp.pad(w_in[:, o_ki:o_c], ((0, 0), (0, LANES - IDX_DIM - N_IDX_HEADS))).astype(BF16)
    wc = w_in[:, o_c:o_c + 3 * cw].astype(BF16)
    E = w_router.shape[1]
    wr = jnp.pad(w_router, ((0, 0), (0, LANES - E))).astype(BF16)
    br = jnp.pad(b_router, (0, LANES - E), constant_values=MASK_VALUE).reshape(1, LANES)
    F = w_gu.shape[2] // 2
    return dict(
        wa=wa, ws=ws, wc=wc, wconv=w_conv, wo=w_out.astype(BF16),
        ln1_g=ln1_g.reshape(1, D), ln1_b=ln1_b.reshape(1, D),
        wr=wr, br=br,
        wgu=_split_gu(w_gu),
        bgu=jnp.concatenate([b_gu[:, 0::2], b_gu[:, 1::2]], axis=1).reshape(E, 1, 2 * F),
        wd=w_down.astype(BF16), bd=b_down.reshape(E, 1, D),
        ln2_g=ln2_g.reshape(1, D), ln2_b=ln2_b.reshape(1, D),
    )


def _decode_layouts(pq, small, k_all, v_all, kidx_all, kc, qb):
    B, S, _ = pq.shape
    aw = N_HEADS * HEAD_DIM
    L, kv = k_all.shape[1], k_all.shape[2]
    lp = -(-L // kc) * kc
    nc = lp // kc
    padk = lambda a: jnp.pad(a, ((0, 0), (0, lp - L), (0, 0)))
    padq = lambda a: jnp.pad(a, ((0, 0), (0, 0), (0, 0), (0, qb - S)))
    q = pq[:, :, 0:aw] * QK_SCALE
    qT = padq(q.reshape(B, S, N_HEADS, HEAD_DIM).transpose(0, 2, 3, 1)).astype(BF16)
    qiT = padq(pq[:, :, aw:].reshape(B, S, N_IDX_HEADS, IDX_DIM).transpose(0, 2, 3, 1)).astype(BF16)
    wT = jnp.pad(small[:, :, IDX_DIM:IDX_DIM + N_IDX_HEADS].transpose(0, 2, 1), ((0, 0), (0, 0), (0, qb - S)))
    kb = padk(k_all).reshape(B, nc, kc, N_KV_HEADS, HEAD_DIM).transpose(0, 1, 3, 2, 4).astype(BF16)
    vT = padk(v_all).reshape(B, nc, kc, kv).transpose(0, 1, 3, 2).astype(BF16)
    kidxb = padk(kidx_all).reshape(B, nc, kc, IDX_DIM).astype(BF16)
    return qT, qiT, wT, kidxb, kb, vT


def _layer(x, past_k, past_v, past_kidx, conv_buf, rel_bias, w, *, dn_alpha):
    B, S, D = x.shape
    prefill = past_k is None
    outs = _project(x, conv_buf, w["wa"], w["ws"], w["wc"], w["wconv"], attn_layouts=prefill)
    k, v, small, conv, new_buf = outs[:5]
    k_idx = small[:, :, 0:IDX_DIM]
    if prefill:
        past, l_true = 0, S
        qb = min(QUERY_BLOCK, S)
        kb, vT, kidxb, qT, qiT, wT = outs[5:]
    else:
        past = past_k.shape[1]
        l_true = past + S
        qb = -(-S // LANES) * LANES
        qT, qiT, wT, kidxb, kb, vT = _decode_layouts(
            outs[5], small, jnp.concatenate([past_k, k], axis=1), jnp.concatenate([past_v, v], axis=1),
            jnp.concatenate([past_kidx, k_idx], axis=1), KEY_CHUNK, qb)
    offsets = (-LANES, 0)
    tabs = _bias_tables(rel_bias, offsets)
    attn = _attend(qT, qiT, wT, kidxb, kb, vT, tabs, qb=qb, past=past, l_true=l_true, offsets=offsets)
    attn = attn[:, :S]
    h = _mix(x.reshape(B * S, D), attn.reshape(B * S, -1), conv.reshape(B * S, -1), w["wo"],
             w["ln1_g"], w["ln1_b"], dn_alpha=dn_alpha)
    y = _moe(h, w["wr"], w["br"], w["wgu"], w["bgu"], w["wd"], w["bd"],
             w["ln2_g"], w["ln2_b"], dn_alpha=dn_alpha)
    return (y.reshape(B, S, D), k.reshape(B, S, N_KV_HEADS, HEAD_DIM), v.reshape(B, S, N_KV_HEADS, HEAD_DIM),
            k_idx, new_buf)


def kernel(x_prompt, x_sample, cache_k, cache_v, cache_kidx, state_conv, rel_bias, w_in, w_conv, w_out,
           ln1_g, ln1_b, w_router, b_router, w_gu, b_gu, w_down, b_down, ln2_g, ln2_b):
    depth = w_in.shape[0]
    assert depth == 1
    dn_alpha = (2 * depth) ** 0.25
    kv = N_KV_HEADS * HEAD_DIM
    w = _prep_weights(w_in[0], w_conv[0], w_out[0], ln1_g[0], ln1_b[0], w_router[0], b_router[0],
                      w_gu[0], b_gu[0], w_down[0], b_down[0], ln2_g[0], ln2_b[0])
    Bp = x_prompt.shape[0]
    cw = w_conv.shape[2]
    zero_buf = jnp.zeros((Bp, CONV_K - 1, cw), F32)
    yp, k1, v1, i1, c1 = _layer(x_prompt, None, None, None, zero_buf, rel_bias, w, dn_alpha=dn_alpha)
    Bs, P = cache_k.shape[1], cache_k.shape[2]
    ys, k2, v2, i2, c2 = _layer(x_sample, cache_k[0].reshape(Bs, P, kv), cache_v[0].reshape(Bs, P, kv),
                                cache_kidx[0], state_conv[0], rel_bias, w, dn_alpha=dn_alpha)
    return (yp, ys, k1[None], v1[None], i1[None], c1[None], k2[None], v2[None], i2[None], c2[None])
```

```python
import functools
import math

import jax
import jax.numpy as jnp
from jax import lax
from jax.experimental import pallas as pl
from jax.experimental.pallas import tpu as pltpu

F32 = jnp.float32
BF16 = jnp.bfloat16
I32 = jnp.int32
I16 = jnp.int16

CHUNK = 64
N_HEADS = 8
HEAD_DIM = 64
N_KV_HEADS = 2
GROUP = N_HEADS // N_KV_HEADS
N_IDX_HEADS = 8
IDX_DIM = 32
TOPK_MAX = 256
CONV_K = 3
N_BUCKETS = 32
MAX_DISTANCE = 128
N_EXPERTS = 32
TOP_K = 4
SWIGLU_LIMIT = 7.0
SWIGLU_ALPHA = 1.702
LN_EPS = 1e-5
MASK_VALUE = -1e30
QK_SCALE = HEAD_DIM ** -0.5 * math.log2(math.e)

LANES = 128
SUBLANES = 8
KEY_CHUNK = 512
QUERY_BLOCK = 256
SLAB = 64
WEIGHT_SPLIT = 4
INT_MIN = -(2 ** 31)
VMEM_LIMIT = 56 * 1024 * 1024


def _cparams(sem):
    return pltpu.CompilerParams(dimension_semantics=sem, vmem_limit_bytes=VMEM_LIMIT)


def _dot(a, b):
    return jnp.dot(a, b, preferred_element_type=F32)


def _layer_norm(z, g, b):
    mu = jnp.mean(z, axis=-1, keepdims=True)
    zc = z - mu
    var = jnp.mean(zc * zc, axis=-1, keepdims=True)
    return zc * lax.rsqrt(var + LN_EPS) * g + b


def _bucket_thresholds():
    nb = N_BUCKETS // 2
    max_exact = nb // 2
    out = []
    for j in range(1, nb - max_exact):
        out.append(math.ceil(max_exact * (MAX_DISTANCE / max_exact) ** (j / (nb - max_exact)) - 1e-9))
    return tuple(out)


def _bias_table_kernel(rel_ref, tab_ref, *, offsets):
    nb = N_BUCKETS // 2
    max_exact = nb // 2
    thr = _bucket_thresholds()
    ii = lax.broadcasted_iota(I32, (LANES, LANES), 0)
    jj = lax.broadcasted_iota(I32, (LANES, LANES), 1)
    for d, off in enumerate(offsets):
        rel = off + ii - jj
        n = jnp.abs(rel)
        large = jnp.full((LANES, LANES), max_exact, I32)
        for t in thr:
            large = large + jnp.where(n >= t, 1, 0)
        bucket = jnp.where(rel > 0, nb, 0) + jnp.where(n < max_exact, n, large)
        for h in range(N_HEADS):
            acc = jnp.zeros((LANES, LANES), F32)
            for b in range(N_BUCKETS):
                acc = jnp.where(bucket == b, rel_ref[b, h], acc)
            tab_ref[d, h] = (acc - rel_ref[nb - 1, h]) * math.log2(math.e)


def _bias_tables(rel_bias, offsets):
    return pl.pallas_call(
        functools.partial(_bias_table_kernel, offsets=offsets),
        out_shape=jax.ShapeDtypeStruct((len(offsets), N_HEADS, LANES, LANES), F32),
        in_specs=[pl.BlockSpec(memory_space=pltpu.SMEM)],
        out_specs=pl.BlockSpec(memory_space=pltpu.VMEM),
        name="bias_tables",
    )(rel_bias)


def _proj_kernel(x_ref, wa_ref, ws_ref, wc_ref, wconv_ref, buf_ref,
                 k_ref, v_ref, small_ref, conv_ref, nbuf_ref, *rest, ts, attn_layouts):
    s = pl.program_id(1)
    xb = x_ref[0].astype(BF16)
    pa = _dot(xb, wa_ref[...])
    ps = _dot(xb, ws_ref[...])
    pc = _dot(xb, wc_ref[...])
    aw = N_HEADS * HEAD_DIM
    iw = N_IDX_HEADS * IDX_DIM
    kv = N_KV_HEADS * HEAD_DIM
    k = pa[:, aw + iw:aw + iw + kv]
    v = pa[:, aw + iw + kv:aw + iw + 2 * kv]
    k_ref[0] = k
    v_ref[0] = v
    small_ref[0] = ps
    scale = QK_SCALE
    if attn_layouts:
        kb_ref, vT_ref, kidxb_ref, qT_ref, qiT_ref, wT_ref, carry_ref = rest
        for n in range(N_KV_HEADS):
            kb_ref[0, 0, n] = k[:, n * HEAD_DIM:(n + 1) * HEAD_DIM].astype(BF16)
        vT_ref[0, 0] = v.T.astype(BF16)
        kidxb_ref[0, 0] = ps[:, 0:IDX_DIM].astype(BF16)
        qT = (pa[:, 0:aw] * scale).T.astype(BF16)
        for h in range(N_HEADS):
            qT_ref[0, h] = qT[h * HEAD_DIM:(h + 1) * HEAD_DIM]
        qiT = pa[:, aw:aw + iw].T.astype(BF16)
        for h in range(N_IDX_HEADS):
            qiT_ref[0, h] = qiT[h * IDX_DIM:(h + 1) * IDX_DIM]
        wT_ref[0] = ps.T[IDX_DIM:IDX_DIM + N_IDX_HEADS]
    else:
        pq_ref, carry_ref = rest
        pq_ref[0] = pa[:, 0:aw + iw]

    cw = pc.shape[1] // 3
    u = pc[:, cw:2 * cw] * pc[:, 2 * cw:3 * cw]

    @pl.when(s == 0)
    def _():
        carry_ref[6:8, :] = buf_ref[0]

    carry_ref[8:8 + ts, :] = u
    y = (carry_ref[6:6 + ts, :] * wconv_ref[0:1, :]
         + carry_ref[7:7 + ts, :] * wconv_ref[1:2, :]
         + u * wconv_ref[2:3, :])
    conv_ref[0] = (pc[:, 0:cw] * y).astype(BF16)
    nb = carry_ref[ts + 6:ts + 8, :]
    nbuf_ref[0] = nb
    carry_ref[6:8, :] = nb


def _project(x, conv_buf, wa, ws, wc, wconv, *, attn_layouts):
    B, S, D = x.shape
    ts = min(KEY_CHUNK, S)
    assert S % ts == 0 and S >= CONV_K - 1
    ns = S // ts
    cw = wc.shape[1] // 3
    kv = N_KV_HEADS * HEAD_DIM
    aw = N_HEADS * HEAD_DIM
    iw = N_IDX_HEADS * IDX_DIM
    out_shape = [
        jax.ShapeDtypeStruct((B, S, kv), F32),
        jax.ShapeDtypeStruct((B, S, kv), F32),
        jax.ShapeDtypeStruct((B, S, LANES), F32),
        jax.ShapeDtypeStruct((B, S, cw), BF16),
        jax.ShapeDtypeStruct((B, CONV_K - 1, cw), F32),
    ]
    out_specs = [
        pl.BlockSpec((1, ts, kv), lambda b, s: (b, s, 0)),
        pl.BlockSpec((1, ts, kv), lambda b, s: (b, s, 0)),
        pl.BlockSpec((1, ts, LANES), lambda b, s: (b, s, 0)),
        pl.BlockSpec((1, ts, cw), lambda b, s: (b, s, 0)),
        pl.BlockSpec((1, CONV_K - 1, cw), lambda b, s: (b, 0, 0)),
    ]
    if attn_layouts:
        out_shape += [
            jax.ShapeDtypeStruct((B, ns, N_KV_HEADS, ts, HEAD_DIM), BF16),
            jax.ShapeDtypeStruct((B, ns, kv, ts), BF16),
            jax.ShapeDtypeStruct((B, ns, ts, IDX_DIM), BF16),
            jax.ShapeDtypeStruct((B, N_HEADS, HEAD_DIM, S), BF16),
            jax.ShapeDtypeStruct((B, N_IDX_HEADS, IDX_DIM, S), BF16),
            jax.ShapeDtypeStruct((B, N_IDX_HEADS, S), F32),
        ]
        out_specs += [
            pl.BlockSpec((1, 1, N_KV_HEADS, ts, HEAD_DIM), lambda b, s: (b, s, 0, 0, 0)),
            pl.BlockSpec((1, 1, kv, ts), lambda b, s: (b, s, 0, 0)),
            pl.BlockSpec((1, 1, ts, IDX_DIM), lambda b, s: (b, s, 0, 0)),
            pl.BlockSpec((1, N_HEADS, HEAD_DIM, ts), lambda b, s: (b, 0, 0, s)),
            pl.BlockSpec((1, N_IDX_HEADS, IDX_DIM, ts), lambda b, s: (b, 0, 0, s)),
            pl.BlockSpec((1, N_IDX_HEADS, ts), lambda b, s: (b, 0, s)),
        ]
    else:
        out_shape += [jax.ShapeDtypeStruct((B, S, aw + iw), F32)]
        out_specs += [pl.BlockSpec((1, ts, aw + iw), lambda b, s: (b, s, 0))]
    const2 = lambda b, s: (0, 0)
    in_specs = [
        pl.BlockSpec((1, ts, D), lambda b, s: (b, s, 0)),
        pl.BlockSpec(wa.shape, const2),
        pl.BlockSpec(ws.shape, const2),
        pl.BlockSpec(wc.shape, const2),
        pl.BlockSpec(wconv.shape, const2),
        pl.BlockSpec((1, CONV_K - 1, cw), lambda b, s: (b, 0, 0)),
    ]
    return pl.pallas_call(
        functools.partial(_proj_kernel, ts=ts, attn_layouts=attn_layouts),
        grid=(B, ns),
        in_specs=in_specs,
        out_specs=tuple(out_specs),
        out_shape=tuple(out_shape),
        scratch_shapes=[pltpu.VMEM((ts + 8, cw), F32)],
        compiler_params=_cparams(("arbitrary", "arbitrary")),
        name="proj_conv",
    )(x, wa, ws, wc, wconv, conv_buf)


def _attn_kernel(qT_ref, qiT_ref, wT_ref, kidx_ref, kb_ref, vT_ref, tab_ref,
                 o_ref, skey_ref, hi16_ref, lo16_ref, jb_ref, madd_ref, *head_refs,
                 qb, kc, past, l_true, topk, offsets):
    s_refs, p_refs, m_refs, l_refs, acc_refs = (head_refs[g * N_HEADS:(g + 1) * N_HEADS] for g in range(5))
    i = pl.program_id(1)
    qoff = past + i * qb
    adm_end = jnp.minimum(((qoff + qb - 1) // CHUNK + 1) * CHUNK, l_true)
    nck = (adm_end + kc - 1) // kc
    n_far = jnp.maximum(qoff - LANES, 0) // kc
    idx_bits = int(l_true).bit_length()
    groups = kc // SUBLANES

    rowi = lax.broadcasted_iota(I32, (kc, qb), 0)
    qpos = qoff + lax.broadcasted_iota(I32, (1, qb), 1)
    kmax = jnp.minimum((qpos // CHUNK + 1) * CHUNK, l_true)

    def score_body(c, carry):
        kidx_c = kidx_ref[0, c]
        sc = jnp.zeros((kc, qb), F32)
        for h in range(N_IDX_HEADS):
            d = _dot(kidx_c, qiT_ref[0, h])
            sc = sc + wT_ref[0, h:h + 1, :] * jnp.maximum(d, 0.0)
        bits = pltpu.bitcast(sc, I32)
        skey = jnp.where(bits < 0, bits ^ jnp.int32(0x7FFFFFFF), bits)
        skey = jnp.where(skey == -1, 0, skey)
        skey = jnp.where(rowi < kmax - c * kc, skey, jnp.int32(INT_MIN))
        skey_ref[c] = skey
        hi16_ref[c] = (skey >> 16).astype(I16)
        return carry

    lax.fori_loop(0, nck, score_body, 0)

    def count(pred_fn):
        def body(c, part):
            ind = jnp.where(pred_fn(c, skey_ref[c]), 1, 0)
            return part + jnp.sum(ind.reshape(groups, SUBLANES, qb), axis=0)
        part = lax.fori_loop(0, nck, body, jnp.zeros((SUBLANES, qb), I32))
        return jnp.sum(part, axis=0, keepdims=True)

    def count16(ref, pred_fn):
        pack = 2 * SUBLANES

        def body(c, part):
            ind = jnp.where(pred_fn(ref[c]), jnp.bfloat16(1), jnp.bfloat16(0)).reshape(kc // pack, pack, qb)
            terms = [ind[g] for g in range(kc // pack)]
            while len(terms) > 1:
                terms = [terms[g] + terms[g + 1] for g in range(0, len(terms), 2)]
            return part + terms[0].astype(F32)
        part = lax.fori_loop(0, nck, body, jnp.zeros((pack, qb), F32))
        return jnp.sum(part, axis=0, keepdims=True).astype(I32)

    def search16(ref, target):
        def body(it, t_u):
            cand_u = t_u | (jnp.int32(1) << (15 - it))
            cand = (cand_u - 2 ** 15).astype(I16)
            cnt = count16(ref, lambda k: k >= cand)
            return jnp.where(cnt >= target, cand_u, t_u)
        return lax.fori_loop(0, 16, body, jnp.zeros((1, qb), I32))

    hi_u = search16(hi16_ref, topk)
    hi_s = (hi_u - 2 ** 15).astype(I16)
    rest = topk - count16(hi16_ref, lambda k: k > hi_s)

    def low_body(c, carry):
        low = ((skey_ref[c] & 0xFFFF) - 2 ** 15).astype(I16)
        lo16_ref[c] = jnp.where(hi16_ref[c] == hi_s, low, jnp.int16(-2 ** 15))
        return carry

    lax.fori_loop(0, nck, low_body, 0)
    lo_u = search16(lo16_ref, rest)
    thr = (hi_u - 2 ** 15) * 2 ** 16 + lo_u
    cnt_ge = count(lambda c, sk: sk >= thr)
    cnt_gt = count(lambda c, sk: sk > thr)

    jb_ref[...] = jnp.full(jb_ref.shape, 2 ** 31 - 1, I32)

    @pl.when(jnp.max(cnt_ge) > topk)
    def _():
        need = topk - cnt_gt

        def tie_body(it, jb):
            cand = jb | (jnp.int32(1) << (idx_bits - 1 - it))
            cnt = count(lambda c, sk: jnp.logical_and(sk == thr, c * kc + rowi < cand))
            return jnp.where(cnt <= need, cand, jb)

        jb = lax.fori_loop(0, idx_bits, tie_body, jnp.zeros((1, qb), I32))
        jb_ref[...] = jnp.broadcast_to(jb, jb_ref.shape)

    jbound = jnp.where(thr == jnp.int32(INT_MIN), 0, jb_ref[0:1, :])
    thr_m1 = thr - 1

    for h in range(N_HEADS):
        m_refs[h][...] = jnp.full((1, qb), MASK_VALUE, F32)
        l_refs[h][...] = jnp.zeros((1, qb), F32)
        acc_refs[h][...] = jnp.zeros((HEAD_DIM, qb), F32)
    n_slabs = kc // SLAB

    def fold(x, op):
        return op(x.reshape(SLAB // SUBLANES, SUBLANES, qb), axis=0)

    def attn_chunk(c, near):
        sk = skey_ref[c]
        t_eff = jnp.where(rowi < jbound - c * kc, thr_m1, thr)
        madd_ref[...] = jnp.where(sk > t_eff, 0.0, MASK_VALUE)
        maccs = []
        for h in range(N_HEADS):
            x = _dot(kb_ref[0, c, h // GROUP], qT_ref[0, h]) + madd_ref[...]
            if near:
                tiles = []
                for t in range(kc // LANES):
                    row = []
                    for u in range(qb // LANES):
                        d_tu = c * kc + t * LANES - (qoff + u * LANES)
                        delta = jnp.zeros((LANES, LANES), F32)
                        for di, off in enumerate(offsets):
                            delta = jnp.where(d_tu == off, tab_ref[di, h], delta)
                        row.append(delta)
                    tiles.append(jnp.concatenate(row, axis=1))
                x = x + jnp.concatenate(tiles, axis=0)
            s_refs[h][...] = x
            maccs.append(jnp.max(x.reshape(groups, SUBLANES, qb), axis=0))
        alphas = []
        for h in range(N_HEADS):
            s_ref, p_ref, m_ref = s_refs[h], p_refs[h], m_refs[h]
            m_old = m_ref[...]
            m_new = jnp.maximum(m_old, jnp.max(maccs[h], axis=0, keepdims=True))
            for j in range(n_slabs):
                rows = pl.ds(j * SLAB, SLAB)
                p_ref[rows, :] = jnp.exp2(s_ref[rows, :] - m_new).astype(BF16)
            m_ref[...] = m_new
            alphas.append(jnp.exp2(m_old - m_new))
        ones = jnp.ones((2 * SUBLANES, kc), BF16)
        v_aug = [jnp.concatenate([vT_ref[0, c, n * HEAD_DIM:(n + 1) * HEAD_DIM, :], ones], axis=0)
                 for n in range(N_KV_HEADS)]
        for h in range(N_HEADS):
            pv = _dot(v_aug[h // GROUP], p_refs[h][...])
            acc_refs[h][...] = alphas[h] * acc_refs[h][...] + pv[0:HEAD_DIM]
            l_refs[h][...] = alphas[h] * l_refs[h][...] + pv[HEAD_DIM:HEAD_DIM + 1]

    def far_body(c, carry):
        attn_chunk(c, False)
        return carry

    def near_body(c, carry):
        attn_chunk(c, True)
        return carry

    lax.fori_loop(0, n_far, far_body, 0)
    lax.fori_loop(n_far, nck, near_body, 0)
    oT = jnp.concatenate([acc_refs[h][...] / l_refs[h][...] for h in range(N_HEADS)], axis=0)
    o_ref[0] = oT.T.astype(o_ref.dtype)


def _attend(qT, qiT, wT, kidxb, kb, vT, tabs, *, qb, past, l_true, offsets):
    B, _, _, S = qT.shape
    nc, kc = vT.shape[1], vT.shape[3]
    assert S % qb == 0 and qb % LANES == 0 and past % LANES == 0 and kc % LANES == 0
    topk = min(TOPK_MAX, l_true // 4)
    kern = functools.partial(_attn_kernel, qb=qb, kc=kc, past=past, l_true=l_true,
                             topk=topk, offsets=offsets)
    in_specs = [
        pl.BlockSpec((1, N_HEADS, HEAD_DIM, qb), lambda b, i: (b, 0, 0, i)),
        pl.BlockSpec((1, N_IDX_HEADS, IDX_DIM, qb), lambda b, i: (b, 0, 0, i)),
        pl.BlockSpec((1, N_IDX_HEADS, qb), lambda b, i: (b, 0, i)),
        pl.BlockSpec((1,) + kidxb.shape[1:], lambda b, i: (b, 0, 0, 0)),
        pl.BlockSpec((1,) + kb.shape[1:], lambda b, i: (b, 0, 0, 0, 0)),
        pl.BlockSpec((1,) + vT.shape[1:], lambda b, i: (b, 0, 0, 0)),
        pl.BlockSpec(tabs.shape, lambda b, i: (0, 0, 0, 0)),
    ]
    return pl.pallas_call(
        kern,
        grid=(B, S // qb),
        in_specs=in_specs,
        out_specs=pl.BlockSpec((1, qb, N_HEADS * HEAD_DIM), lambda b, i: (b, i, 0)),
        out_shape=jax.ShapeDtypeStruct((B, S, N_HEADS * HEAD_DIM), BF16),
        scratch_shapes=[
            pltpu.VMEM((nc, kc, qb), I32),
            pltpu.VMEM((nc, kc, qb), I16),
            pltpu.VMEM((nc, kc, qb), I16),
            pltpu.VMEM((SUBLANES, qb), I32),
            pltpu.VMEM((kc, qb), F32),
        ] + [pltpu.VMEM((kc, qb), F32)] * N_HEADS
          + [pltpu.VMEM((kc, qb), BF16)] * N_HEADS
          + [pltpu.VMEM((1, qb), F32)] * N_HEADS
          + [pltpu.VMEM((1, qb), F32)] * N_HEADS
          + [pltpu.VMEM((HEAD_DIM, qb), F32)] * N_HEADS,
        compiler_params=_cparams(("arbitrary", "arbitrary")),
        name="dsa_attend",
    )(qT, qiT, wT, kidxb, kb, vT, tabs)


def _mix_kernel(x_ref, attn_ref, conv_ref, wo_ref, g_ref, b_ref, h_ref, *, dn_alpha):
    cat = jnp.concatenate([attn_ref[...], conv_ref[...]], axis=1)
    mixed = _dot(cat, wo_ref[...])
    h_ref[...] = _layer_norm(dn_alpha * x_ref[...] + mixed, g_ref[...], b_ref[...])


def _mix(x2, attn2, conv2, wo, g, b, *, dn_alpha):
    T, D = x2.shape
    tt = min(1024, T)
    assert T % tt == 0
    aw, cw = attn2.shape[1], conv2.shape[1]
    row = lambda i: (i, 0)
    const = lambda i: (0, 0)
    return pl.pallas_call(
        functools.partial(_mix_kernel, dn_alpha=dn_alpha),
        grid=(T // tt,),
        in_specs=[pl.BlockSpec((tt, D), row), pl.BlockSpec((tt, aw), row), pl.BlockSpec((tt, cw), row),
                  pl.BlockSpec(wo.shape, const), pl.BlockSpec((1, D), const), pl.BlockSpec((1, D), const)],
        out_specs=pl.BlockSpec((tt, D), row),
        out_shape=jax.ShapeDtypeStruct((T, D), F32),
        compiler_params=_cparams(("arbitrary",)),
        name="outproj_ln1",
    )(x2, attn2, conv2, wo, g, b)


def _route(logits):
    lane = lax.broadcasted_iota(I32, logits.shape, 1)
    work = logits
    vals, hots = [], []
    for _ in range(TOP_K):
        m = jnp.max(work, axis=1, keepdims=True)
        idx = jnp.min(jnp.where(work == m, lane, LANES), axis=1, keepdims=True)
        hot = lane == idx
        vals.append(m)
        hots.append(hot)
        work = jnp.where(hot, -jnp.inf, work)
    es = [jnp.exp(v - vals[0]) for v in vals]
    den = es[0]
    for e in es[1:]:
        den = den + e
    comb = jnp.zeros(logits.shape, F32)
    picked = jnp.zeros(logits.shape, F32)
    for e, hot in zip(es, hots):
        comb = jnp.where(hot, e / den, comb)
        picked = jnp.where(hot, 1.0, picked)
    return comb, picked


def _moe_kernel(h_ref, wr_ref, br_ref, *rest, dn_alpha, rows):
    wgu_refs, rest = rest[:WEIGHT_SPLIT], rest[WEIGHT_SPLIT:]
    bgu_ref, rest = rest[0], rest[1:]
    wd_refs, rest = rest[:WEIGHT_SPLIT], rest[WEIGHT_SPLIT:]
    bd_ref, g_ref, b_ref, y_ref, hb_ref, combT_ref, slotT_ref = rest
    e = pl.program_id(1)
    tt = h_ref.shape[0]
    ff = wd_refs[0].shape[1]

    @pl.when(e == 0)
    def _():
        hb = h_ref[...].astype(BF16)
        hb_ref[...] = hb
        comb, picked = _route(_dot(hb, wr_ref[...]) + br_ref[...])
        combT_ref[...] = comb.T
        before = (lax.broadcasted_iota(I32, (tt, tt), 1) < lax.broadcasted_iota(I32, (tt, tt), 0))
        rank = _dot(jnp.where(before, 1.0, 0.0).astype(BF16), picked.astype(BF16))
        slotT_ref[...] = jnp.where(picked > 0.0, rank, -1.0).T
        y_ref[...] = jnp.zeros(y_ref.shape, F32)

    slot_row = slotT_ref[pl.ds(e, 1), :]
    gate_row = combT_ref[pl.ds(e, 1), :]
    count = jnp.sum(jnp.where(slot_row >= 0.0, 1, 0))
    sub_iota = lax.broadcasted_iota(I32, (rows, tt), 0).astype(F32)

    def chunk_body(c, carry):
        base = (c * rows).astype(F32)
        hit = slot_row - base == sub_iota
        gather = jnp.where(hit, 1.0, 0.0).astype(BF16)
        row_gate = jnp.sum(jnp.where(hit, gate_row, 0.0), axis=1, keepdims=True)
        xg = _dot(gather, hb_ref[...]).astype(BF16)
        gu = jnp.concatenate([_dot(xg, w[0]) for w in wgu_refs], axis=1) + bgu_ref[0]
        gate = jnp.minimum(gu[:, 0:ff], SWIGLU_LIMIT)
        up = jnp.clip(gu[:, ff:2 * ff], -SWIGLU_LIMIT, SWIGLU_LIMIT)
        act = ((up + 1.0) * (gate * (1.0 / (1.0 + jnp.exp(-SWIGLU_ALPHA * gate))))).astype(BF16)
        y = (jnp.concatenate([_dot(act, w[0]) for w in wd_refs], axis=1) + bd_ref[0]) * row_gate
        y_ref[...] += lax.dot_general(gather, y.astype(BF16), (((0,), (0,)), ((), ())),
                                      preferred_element_type=F32)
        return carry

    lax.fori_loop(0, (count + rows - 1) // rows, chunk_body, 0)

    @pl.when(e == pl.num_programs(1) - 1)
    def _():
        y_ref[...] = _layer_norm(dn_alpha * h_ref[...] + y_ref[...], g_ref[...], b_ref[...])


def _moe(h2, wr, br, wgu, bgu, wd, bd, g, b, *, dn_alpha):
    T, D = h2.shape
    E, _, F2 = wgu.shape
    tt = min(1024, T)
    assert T % tt == 0
    rows = min(tt, -(-(tt * TOP_K * 5 // (E * 4)) // 16) * 16)
    row = lambda i, e: (i, 0)
    const = lambda i, e: (0, 0)
    exp3 = lambda i, e: (e, 0, 0)
    ns = WEIGHT_SPLIT
    col = lambda j: (lambda i, e: (e, 0, j))
    return pl.pallas_call(
        functools.partial(_moe_kernel, dn_alpha=dn_alpha, rows=rows),
        grid=(T // tt, E),
        in_specs=[pl.BlockSpec((tt, D), row), pl.BlockSpec(wr.shape, const), pl.BlockSpec(br.shape, const)]
                 + [pl.BlockSpec((1, D, F2 // ns), col(j)) for j in range(ns)]
                 + [pl.BlockSpec((1, 1, F2), exp3)]
                 + [pl.BlockSpec((1, F2 // 2, D // ns), col(j)) for j in range(ns)]
                 + [pl.BlockSpec((1, 1, D), exp3),
                    pl.BlockSpec((1, D), const), pl.BlockSpec((1, D), const)],
        out_specs=pl.BlockSpec((tt, D), row),
        out_shape=jax.ShapeDtypeStruct((T, D), F32),
        scratch_shapes=[pltpu.VMEM((tt, D), BF16), pltpu.VMEM((LANES, tt), F32), pltpu.VMEM((LANES, tt), F32)],
        compiler_params=_cparams(("arbitrary", "arbitrary")),
        name="moe_ln2",
    )(h2, wr, br, *([wgu] * ns), bgu, *([wd] * ns), bd, g, b)


def _split_gu_kernel(w_ref, perm_ref, o_ref, *, band):
    n = w_ref.shape[2]
    for m in range(n // band):
        both = _dot(w_ref[0, :, m * band:(m + 1) * band].astype(BF16), perm_ref[...]).astype(BF16)
        o_ref[0, :, m * (band // 2):(m + 1) * (band // 2)] = both[:, 0:band // 2]
        o_ref[0, :, n // 2 + m * (band // 2):n // 2 + (m + 1) * (band // 2)] = both[:, band // 2:band]


def _split_gu(w_gu):
    E, D, N = w_gu.shape
    tr = min(512, D)
    band = min(512, N)
    j = jnp.arange(band, dtype=I32)
    src = jnp.where(j < band // 2, 2 * j, 2 * (j - band // 2) + 1)
    perm = (jnp.arange(band, dtype=I32)[:, None] == src[None, :]).astype(BF16)
    return pl.pallas_call(
        functools.partial(_split_gu_kernel, band=band),
        grid=(E, D // tr),
        in_specs=[pl.BlockSpec((1, tr, N), lambda e, r: (e, r, 0)),
                  pl.BlockSpec((band, band), lambda e, r: (0, 0))],
        out_specs=pl.BlockSpec((1, tr, N), lambda e, r: (e, r, 0)),
        out_shape=jax.ShapeDtypeStruct((E, D, N), BF16),
        compiler_params=_cparams(("arbitrary", "arbitrary")),
        name="split_gate_up",
    )(w_gu, perm)


def _prep_weights(w_in, w_conv, w_out, ln1_g, ln1_b, w_router, b_router, w_gu, b_gu, w_down, b_down,
                  ln2_g, ln2_b):
    aw = N_HEADS * HEAD_DIM
    kv = N_KV_HEADS * HEAD_DIM
    iw = N_IDX_HEADS * IDX_DIM
    D = w_in.shape[0]
    cw = D - aw
    o_q, o_k, o_v = 0, aw, aw + kv
    o_qi = o_v + kv
    o_ki = o_qi + iw
    o_wi = o_ki + IDX_DIM
    o_c = o_wi + N_IDX_HEADS
    wa = jnp.concatenate([w_in[:, o_q:o_k], w_in[:, o_qi:o_ki], w_in[:, o_k:o_v], w_in[:, o_v:o_qi]],
                         axis=1).astype(BF16)
    ws = jnp.pad(w_in[:, o_ki:o_c], ((0, 0), (0, LANES - IDX_DIM - N_IDX_HEADS))).astype(BF16)
    wc = w_in[:, o_c:o_c + 3 * cw].astype(BF16)
    E = w_router.shape[1]
    wr = jnp.pad(w_router, ((0, 0), (0, LANES - E))).astype(BF16)
    br = jnp.pad(b_router, (0, LANES - E), constant_values=MASK_VALUE).reshape(1, LANES)
    F = w_gu.shape[2] // 2
    return dict(
        wa=wa, ws=ws, wc=wc, wconv=w_conv, wo=w_out.astype(BF16),
        ln1_g=ln1_g.reshape(1, D), ln1_b=ln1_b.reshape(1, D),
        wr=wr, br=br,
        wgu=_split_gu(w_gu),
        bgu=jnp.concatenate([b_gu[:, 0::2], b_gu[:, 1::2]], axis=1).reshape(E, 1, 2 * F),
        wd=w_down.astype(BF16), bd=b_down.reshape(E, 1, D),
        ln2_g=ln2_g.reshape(1, D), ln2_b=ln2_b.reshape(1, D),
    )


def _decode_layouts(pq, small, k_all, v_all, kidx_all, kc, qb):
    B, S, _ = pq.shape
    aw = N_HEADS * HEAD_DIM
    L, kv = k_all.shape[1], k_all.shape[2]
    lp = -(-L // kc) * kc
    nc = lp // kc
    padk = lambda a: jnp.pad(a, ((0, 0), (0, lp - L), (0, 0)))
    padq = lambda a: jnp.pad(a, ((0, 0), (0, 0), (0, 0), (0, qb - S)))
    q = pq[:, :, 0:aw] * QK_SCALE
    qT = padq(q.reshape(B, S, N_HEADS, HEAD_DIM).transpose(0, 2, 3, 1)).astype(BF16)
    qiT = padq(pq[:, :, aw:].reshape(B, S, N_IDX_HEADS, IDX_DIM).transpose(0, 2, 3, 1)).astype(BF16)
    wT = jnp.pad(small[:, :, IDX_DIM:IDX_DIM + N_IDX_HEADS].transpose(0, 2, 1), ((0, 0), (0, 0), (0, qb - S)))
    kb = padk(k_all).reshape(B, nc, kc, N_KV_HEADS, HEAD_DIM).transpose(0, 1, 3, 2, 4).astype(BF16)
    vT = padk(v_all).reshape(B, nc, kc, kv).transpose(0, 1, 3, 2).astype(BF16)
    kidxb = padk(kidx_all).reshape(B, nc, kc, IDX_DIM).astype(BF16)
    return qT, qiT, wT, kidxb, kb, vT


def _layer(x, past_k, past_v, past_kidx, conv_buf, rel_bias, w, *, dn_alpha):
    B, S, D = x.shape
    prefill = past_k is None
    outs = _project(x, conv_buf, w["wa"], w["ws"], w["wc"], w["wconv"], attn_layouts=prefill)
    k, v, small, conv, new_buf = outs[:5]
    k_idx = small[:, :, 0:IDX_DIM]
    if prefill:
        past, l_true = 0, S
        qb = min(QUERY_BLOCK, S)
        kb, vT, kidxb, qT, qiT, wT = outs[5:]
    else:
        past = past_k.shape[1]
        l_true = past + S
        qb = -(-S // LANES) * LANES
        qT, qiT, wT, kidxb, kb, vT = _decode_layouts(
            outs[5], small, jnp.concatenate([past_k, k], axis=1), jnp.concatenate([past_v, v], axis=1),
            jnp.concatenate([past_kidx, k_idx], axis=1), KEY_CHUNK, qb)
    offsets = (-LANES, 0)
    tabs = _bias_tables(rel_bias, offsets)
    attn = _attend(qT, qiT, wT, kidxb, kb, vT, tabs, qb=qb, past=past, l_true=l_true, offsets=offsets)
    attn = attn[:, :S]
    h = _mix(x.reshape(B * S, D), attn.reshape(B * S, -1), conv.reshape(B * S, -1), w["wo"],
             w["ln1_g"], w["ln1_b"], dn_alpha=dn_alpha)
    y = _moe(h, w["wr"], w["br"], w["wgu"], w["bgu"], w["wd"], w["bd"],
             w["ln2_g"], w["ln2_b"], dn_alpha=dn_alpha)
    return (y.reshape(B, S, D), k.reshape(B, S, N_KV_HEADS, HEAD_DIM), v.reshape(B, S, N_KV_HEADS, HEAD_DIM),
            k_idx, new_buf)


def kernel(x_prompt, x_sample, cache_k, cache_v, cache_kidx, state_conv, rel_bias, w_in, w_conv, w_out,
           ln1_g, ln1_b, w_router, b_router, w_gu, b_gu, w_down, b_down, ln2_g, ln2_b):
    depth = w_in.shape[0]
    assert depth == 1
    dn_alpha = (2 * depth) ** 0.25
    kv = N_KV_HEADS * HEAD_DIM
    w = _prep_weights(w_in[0], w_conv[0], w_out[0], ln1_g[0], ln1_b[0], w_router[0], b_router[0],
                      w_gu[0], b_gu[0], w_down[0], b_down[0], ln2_g[0], ln2_b[0])
    Bp = x_prompt.shape[0]
    cw = w_conv.shape[2]
    zero_buf = jnp.zeros((Bp, CONV_K - 1, cw), F32)
    yp, k1, v1, i1, c1 = _layer(x_prompt, None, None, None, zero_buf, rel_bias, w, dn_alpha=dn_alpha)
    Bs, P = cache_k.shape[1], cache_k.shape[2]
    ys, k2, v2, i2, c2 = _layer(x_sample, cache_k[0].reshape(Bs, P, kv), cache_v[0].reshape(Bs, P, kv),
                                cache_kidx[0], state_conv[0], rel_bias, w, dn_alpha=dn_alpha)
    return (yp, ys, k1[None], v1[None], i1[None], c1[None], k2[None], v2[None], i2[None], c2[None])
```

```python
import functools
import math

import jax
import jax.numpy as jnp
from jax import lax
from jax.experimental import pallas as pl
from jax.experimental.pallas import tpu as pltpu

F32 = jnp.float32
BF16 = jnp.bfloat16
I32 = jnp.int32
I16 = jnp.int16

CHUNK = 64
N_HEADS = 8
HEAD_DIM = 64
N_KV_HEADS = 2
GROUP = N_HEADS // N_KV_HEADS
N_IDX_HEADS = 8
IDX_DIM = 32
TOPK_MAX = 256
CONV_K = 3
N_BUCKETS = 32
MAX_DISTANCE = 128
N_EXPERTS = 32
TOP_K = 4
SWIGLU_LIMIT = 7.0
SWIGLU_ALPHA = 1.702
LN_EPS = 1e-5
MASK_VALUE = -1e30
QK_SCALE = HEAD_DIM ** -0.5 * math.log2(math.e)
EXP2_SAFE = 96.0

LANES = 128
SUBLANES = 8
KEY_CHUNK = 512
QUERY_BLOCK = 256
SLAB = 64
WEIGHT_SPLIT = 4
INT_MIN = -(2 ** 31)
VMEM_LIMIT = 56 * 1024 * 1024


def _cparams(sem):
    return pltpu.CompilerParams(dimension_semantics=sem, vmem_limit_bytes=VMEM_LIMIT)


def _dot(a, b):
    return jnp.dot(a, b, preferred_element_type=F32)


def _layer_norm(z, g, b):
    mu = jnp.mean(z, axis=-1, keepdims=True)
    zc = z - mu
    var = jnp.mean(zc * zc, axis=-1, keepdims=True)
    return zc * lax.rsqrt(var + LN_EPS) * g + b


def _bucket_thresholds():
    nb = N_BUCKETS // 2
    max_exact = nb // 2
    out = []
    for j in range(1, nb - max_exact):
        out.append(math.ceil(max_exact * (MAX_DISTANCE / max_exact) ** (j / (nb - max_exact)) - 1e-9))
    return tuple(out)


def _bias_table_kernel(rel_ref, tab_ref, *, offsets):
    nb = N_BUCKETS // 2
    max_exact = nb // 2
    thr = _bucket_thresholds()
    ii = lax.broadcasted_iota(I32, (LANES, LANES), 0)
    jj = lax.broadcasted_iota(I32, (LANES, LANES), 1)
    for d, off in enumerate(offsets):
        rel = off + ii - jj
        n = jnp.abs(rel)
        large = jnp.full((LANES, LANES), max_exact, I32)
        for t in thr:
            large = large + jnp.where(n >= t, 1, 0)
        bucket = jnp.where(rel > 0, nb, 0) + jnp.where(n < max_exact, n, large)
        for h in range(N_HEADS):
            acc = jnp.zeros((LANES, LANES), F32)
            for b in range(N_BUCKETS):
                acc = jnp.where(bucket == b, rel_ref[b, h], acc)
            tab_ref[d, h] = (acc - rel_ref[nb - 1, h]) * math.log2(math.e)


def _bias_tables(rel_bias, offsets):
    return pl.pallas_call(
        functools.partial(_bias_table_kernel, offsets=offsets),
        out_shape=jax.ShapeDtypeStruct((len(offsets), N_HEADS, LANES, LANES), F32),
        in_specs=[pl.BlockSpec(memory_space=pltpu.SMEM)],
        out_specs=pl.BlockSpec(memory_space=pltpu.VMEM),
        name="bias_tables",
    )(rel_bias)


def _proj_kernel(x_ref, wa_ref, ws_ref, wc_ref, wconv_ref, buf_ref,
                 k_ref, v_ref, small_ref, conv_ref, nbuf_ref, *rest, ts, attn_layouts):
    s = pl.program_id(1)
    xb = x_ref[0].astype(BF16)
    pa = _dot(xb, wa_ref[...])
    ps = _dot(xb, ws_ref[...])
    pc = _dot(xb, wc_ref[...])
    aw = N_HEADS * HEAD_DIM
    iw = N_IDX_HEADS * IDX_DIM
    kv = N_KV_HEADS * HEAD_DIM
    k = pa[:, aw + iw:aw + iw + kv]
    v = pa[:, aw + iw + kv:aw + iw + 2 * kv]
    k_ref[0] = k
    v_ref[0] = v
    small_ref[0] = ps
    scale = QK_SCALE
    if attn_layouts:
        kb_ref, vT_ref, kidxb_ref, qT_ref, qiT_ref, wT_ref, knorm_ref, carry_ref = rest
        for n in range(N_KV_HEADS):
            kb_ref[0, 0, n] = k[:, n * HEAD_DIM:(n + 1) * HEAD_DIM].astype(BF16)
        kTf = k.astype(BF16).astype(F32).T
        knorm_ref[0, 0] = jnp.concatenate(
            [jnp.sum(kTf[n * HEAD_DIM:(n + 1) * HEAD_DIM] ** 2, axis=0, keepdims=True) for n in range(N_KV_HEADS)],
            axis=0)
        vT_ref[0, 0] = v.T.astype(BF16)
        kidxb_ref[0, 0] = ps[:, 0:IDX_DIM].astype(BF16)
        qT = (pa[:, 0:aw] * scale).T.astype(BF16)
        for h in range(N_HEADS):
            qT_ref[0, h] = qT[h * HEAD_DIM:(h + 1) * HEAD_DIM]
        qiT = pa[:, aw:aw + iw].T.astype(BF16)
        for h in range(N_IDX_HEADS):
            qiT_ref[0, h] = qiT[h * IDX_DIM:(h + 1) * IDX_DIM]
        wT_ref[0] = ps.T[IDX_DIM:IDX_DIM + N_IDX_HEADS]
    else:
        pq_ref, carry_ref = rest
        pq_ref[0] = pa[:, 0:aw + iw]

    cw = pc.shape[1] // 3
    u = pc[:, cw:2 * cw] * pc[:, 2 * cw:3 * cw]

    @pl.when(s == 0)
    def _():
        carry_ref[6:8, :] = buf_ref[0]

    carry_ref[8:8 + ts, :] = u
    y = (carry_ref[6:6 + ts, :] * wconv_ref[0:1, :]
         + carry_ref[7:7 + ts, :] * wconv_ref[1:2, :]
         + u * wconv_ref[2:3, :])
    conv_ref[0] = (pc[:, 0:cw] * y).astype(BF16)
    nb = carry_ref[ts + 6:ts + 8, :]
    nbuf_ref[0] = nb
    carry_ref[6:8, :] = nb


def _project(x, conv_buf, wa, ws, wc, wconv, *, attn_layouts):
    B, S, D = x.shape
    ts = min(KEY_CHUNK, S)
    assert S % ts == 0 and S >= CONV_K - 1
    ns = S // ts
    cw = wc.shape[1] // 3
    kv = N_KV_HEADS * HEAD_DIM
    aw = N_HEADS * HEAD_DIM
    iw = N_IDX_HEADS * IDX_DIM
    out_shape = [
        jax.ShapeDtypeStruct((B, S, kv), F32),
        jax.ShapeDtypeStruct((B, S, kv), F32),
        jax.ShapeDtypeStruct((B, S, LANES), F32),
        jax.ShapeDtypeStruct((B, S, cw), BF16),
        jax.ShapeDtypeStruct((B, CONV_K - 1, cw), F32),
    ]
    out_specs = [
        pl.BlockSpec((1, ts, kv), lambda b, s: (b, s, 0)),
        pl.BlockSpec((1, ts, kv), lambda b, s: (b, s, 0)),
        pl.BlockSpec((1, ts, LANES), lambda b, s: (b, s, 0)),
        pl.BlockSpec((1, ts, cw), lambda b, s: (b, s, 0)),
        pl.BlockSpec((1, CONV_K - 1, cw), lambda b, s: (b, 0, 0)),
    ]
    if attn_layouts:
        out_shape += [
            jax.ShapeDtypeStruct((B, ns, N_KV_HEADS, ts, HEAD_DIM), BF16),
            jax.ShapeDtypeStruct((B, ns, kv, ts), BF16),
            jax.ShapeDtypeStruct((B, ns, ts, IDX_DIM), BF16),
            jax.ShapeDtypeStruct((B, N_HEADS, HEAD_DIM, S), BF16),
            jax.ShapeDtypeStruct((B, N_IDX_HEADS, IDX_DIM, S), BF16),
            jax.ShapeDtypeStruct((B, N_IDX_HEADS, S), F32),
            jax.ShapeDtypeStruct((B, ns, N_KV_HEADS, ts), F32),
        ]
        out_specs += [
            pl.BlockSpec((1, 1, N_KV_HEADS, ts, HEAD_DIM), lambda b, s: (b, s, 0, 0, 0)),
            pl.BlockSpec((1, 1, kv, ts), lambda b, s: (b, s, 0, 0)),
            pl.BlockSpec((1, 1, ts, IDX_DIM), lambda b, s: (b, s, 0, 0)),
            pl.BlockSpec((1, N_HEADS, HEAD_DIM, ts), lambda b, s: (b, 0, 0, s)),
            pl.BlockSpec((1, N_IDX_HEADS, IDX_DIM, ts), lambda b, s: (b, 0, 0, s)),
            pl.BlockSpec((1, N_IDX_HEADS, ts), lambda b, s: (b, 0, s)),
            pl.BlockSpec((1, 1, N_KV_HEADS, ts), lambda b, s: (b, s, 0, 0)),
        ]
    else:
        out_shape += [jax.ShapeDtypeStruct((B, S, aw + iw), F32)]
        out_specs += [pl.BlockSpec((1, ts, aw + iw), lambda b, s: (b, s, 0))]
    const2 = lambda b, s: (0, 0)
    in_specs = [
        pl.BlockSpec((1, ts, D), lambda b, s: (b, s, 0)),
        pl.BlockSpec(wa.shape, const2),
        pl.BlockSpec(ws.shape, const2),
        pl.BlockSpec(wc.shape, const2),
        pl.BlockSpec(wconv.shape, const2),
        pl.BlockSpec((1, CONV_K - 1, cw), lambda b, s: (b, 0, 0)),
    ]
    return pl.pallas_call(
        functools.partial(_proj_kernel, ts=ts, attn_layouts=attn_layouts),
        grid=(B, ns),
        in_specs=in_specs,
        out_specs=tuple(out_specs),
        out_shape=tuple(out_shape),
        scratch_shapes=[pltpu.VMEM((ts + 8, cw), F32)],
        compiler_params=_cparams(("arbitrary", "arbitrary")),
        name="proj_conv",
    )(x, wa, ws, wc, wconv, conv_buf)


def _attn_kernel(qT_ref, qiT_ref, wT_ref, kidx_ref, kb_ref, vT_ref, knorm_ref, tab_ref,
                 o_ref, skey_ref, hi16_ref, lo16_ref, jb_ref, madd_ref, *head_refs,
                 qb, kc, past, l_true, topk, offsets):
    s_refs, p_refs, m_refs, l_refs, acc_refs = (head_refs[g * N_HEADS:(g + 1) * N_HEADS] for g in range(5))
    i = pl.program_id(1)
    qoff = past + i * qb
    adm_end = jnp.minimum(((qoff + qb - 1) // CHUNK + 1) * CHUNK, l_true)
    nck = (adm_end + kc - 1) // kc
    n_far = jnp.maximum(qoff - LANES, 0) // kc
    idx_bits = int(l_true).bit_length()
    groups = kc // SUBLANES

    rowi = lax.broadcasted_iota(I32, (kc, qb), 0)
    qpos = qoff + lax.broadcasted_iota(I32, (1, qb), 1)
    kmax = jnp.minimum((qpos // CHUNK + 1) * CHUNK, l_true)

    def score_body(c, carry):
        kidx_c = kidx_ref[0, c]
        sc = jnp.zeros((kc, qb), F32)
        for h in range(N_IDX_HEADS):
            d = _dot(kidx_c, qiT_ref[0, h])
            sc = sc + wT_ref[0, h:h + 1, :] * jnp.maximum(d, 0.0)
        bits = pltpu.bitcast(sc, I32)
        skey = jnp.where(bits < 0, bits ^ jnp.int32(0x7FFFFFFF), bits)
        skey = jnp.where(skey == -1, 0, skey)
        skey = jnp.where(rowi < kmax - c * kc, skey, jnp.int32(INT_MIN))
        skey_ref[c] = skey
        hi16_ref[c] = (skey >> 16).astype(I16)
        return carry

    lax.fori_loop(0, nck, score_body, 0)

    def count(pred_fn):
        def body(c, part):
            ind = jnp.where(pred_fn(c, skey_ref[c]), 1, 0)
            return part + jnp.sum(ind.reshape(groups, SUBLANES, qb), axis=0)
        part = lax.fori_loop(0, nck, body, jnp.zeros((SUBLANES, qb), I32))
        return jnp.sum(part, axis=0, keepdims=True)

    def count16(ref, pred_fn):
        pack = 2 * SUBLANES

        def body(c, part):
            ind = jnp.where(pred_fn(ref[c]), jnp.bfloat16(1), jnp.bfloat16(0)).reshape(kc // pack, pack, qb)
            terms = [ind[g] for g in range(kc // pack)]
            while len(terms) > 1:
                terms = [terms[g] + terms[g + 1] for g in range(0, len(terms), 2)]
            return part + terms[0].astype(F32)
        part = lax.fori_loop(0, nck, body, jnp.zeros((pack, qb), F32))
        return jnp.sum(part, axis=0, keepdims=True).astype(I32)

    def search16(ref, target):
        def body(it, t_u):
            cand_u = t_u | (jnp.int32(1) << (15 - it))
            cand = (cand_u - 2 ** 15).astype(I16)
            cnt = count16(ref, lambda k: k >= cand)
            return jnp.where(cnt >= target, cand_u, t_u)
        return lax.fori_loop(0, 16, body, jnp.zeros((1, qb), I32))

    hi_u = search16(hi16_ref, topk)
    hi_s = (hi_u - 2 ** 15).astype(I16)
    rest = topk - count16(hi16_ref, lambda k: k > hi_s)

    def low_body(c, carry):
        low = ((skey_ref[c] & 0xFFFF) - 2 ** 15).astype(I16)
        lo16_ref[c] = jnp.where(hi16_ref[c] == hi_s, low, jnp.int16(-2 ** 15))
        return carry

    lax.fori_loop(0, nck, low_body, 0)
    lo_u = search16(lo16_ref, rest)
    thr = (hi_u - 2 ** 15) * 2 ** 16 + lo_u
    cnt_ge = count(lambda c, sk: sk >= thr)
    cnt_gt = count(lambda c, sk: sk > thr)

    jb_ref[...] = jnp.full(jb_ref.shape, 2 ** 31 - 1, I32)

    @pl.when(jnp.max(cnt_ge) > topk)
    def _():
        need = topk - cnt_gt

        def tie_body(it, jb):
            cand = jb | (jnp.int32(1) << (idx_bits - 1 - it))
            cnt = count(lambda c, sk: jnp.logical_and(sk == thr, c * kc + rowi < cand))
            return jnp.where(cnt <= need, cand, jb)

        jb = lax.fori_loop(0, idx_bits, tie_body, jnp.zeros((1, qb), I32))
        jb_ref[...] = jnp.broadcast_to(jb, jb_ref.shape)

    jbound = jnp.where(thr == jnp.int32(INT_MIN), 0, jb_ref[0:1, :])
    thr_m1 = thr - 1

    for h in range(N_HEADS):
        l_refs[h][...] = jnp.zeros((1, qb), F32)
        acc_refs[h][...] = jnp.zeros((HEAD_DIM, qb), F32)
    n_slabs = kc // SLAB

    def fold(x, op):
        return op(x.reshape(SLAB // SUBLANES, SUBLANES, qb), axis=0)

    def near_bias(c, h):
        tiles = []
        for t in range(kc // LANES):
            row = []
            for u in range(qb // LANES):
                d_tu = c * kc + t * LANES - (qoff + u * LANES)
                delta = jnp.zeros((LANES, LANES), F32)
                for di, off in enumerate(offsets):
                    delta = jnp.where(d_tu == off, tab_ref[di, h], delta)
                row.append(delta)
            tiles.append(jnp.concatenate(row, axis=1))
        return jnp.concatenate(tiles, axis=0)

    def attn_chunk(c, near):
        sk = skey_ref[c]
        t_eff = jnp.where(rowi < jbound - c * kc, thr_m1, thr)
        madd_ref[...] = jnp.where(sk > t_eff, 0.0, MASK_VALUE)
        maccs = []
        for h in range(N_HEADS):
            x = _dot(kb_ref[0, c, h // GROUP], qT_ref[0, h]) + madd_ref[...]
            if near:
                x = x + near_bias(c, h)
            s_refs[h][...] = x
            maccs.append(jnp.max(x.reshape(groups, SUBLANES, qb), axis=0))
        alphas = []
        for h in range(N_HEADS):
            s_ref, p_ref, m_ref = s_refs[h], p_refs[h], m_refs[h]
            m_old = m_ref[...]
            m_new = jnp.maximum(m_old, jnp.max(maccs[h], axis=0, keepdims=True))
            for j in range(n_slabs):
                rows = pl.ds(j * SLAB, SLAB)
                p_ref[rows, :] = jnp.exp2(s_ref[rows, :] - m_new).astype(BF16)
            m_ref[...] = m_new
            alphas.append(jnp.exp2(m_old - m_new))
        ones = jnp.ones((2 * SUBLANES, kc), BF16)
        v_aug = [jnp.concatenate([vT_ref[0, c, n * HEAD_DIM:(n + 1) * HEAD_DIM, :], ones], axis=0)
                 for n in range(N_KV_HEADS)]
        for h in range(N_HEADS):
            pv = _dot(v_aug[h // GROUP], p_refs[h][...])
            acc_refs[h][...] = alphas[h] * acc_refs[h][...] + pv[0:HEAD_DIM]
            l_refs[h][...] = alphas[h] * l_refs[h][...] + pv[HEAD_DIM:HEAD_DIM + 1]

    def attn_chunk_fixed(c, near):
        sk = skey_ref[c]
        t_eff = jnp.where(rowi < jbound - c * kc, thr_m1, thr)
        madd_ref[...] = jnp.where(sk > t_eff, 0.0, MASK_VALUE)
        ones = jnp.ones((2 * SUBLANES, kc), BF16)
        v_aug = [jnp.concatenate([vT_ref[0, c, n * HEAD_DIM:(n + 1) * HEAD_DIM, :], ones], axis=0)
                 for n in range(N_KV_HEADS)]
        for h in range(N_HEADS):
            x = _dot(kb_ref[0, c, h // GROUP], qT_ref[0, h]) + madd_ref[...]
            if near:
                x = x + near_bias(c, h)
            p_refs[h][...] = jnp.exp2(x).astype(BF16)
        for h in range(N_HEADS):
            pv = _dot(v_aug[h // GROUP], p_refs[h][...])
            acc_refs[h][...] += pv[0:HEAD_DIM]
            l_refs[h][...] += pv[HEAD_DIM:HEAD_DIM + 1]

    def run(chunk_fn):
        lax.fori_loop(0, n_far, lambda c, carry: (chunk_fn(c, False), carry)[1], 0)
        lax.fori_loop(n_far, nck, lambda c, carry: (chunk_fn(c, True), carry)[1], 0)

    q_sq = jnp.zeros((1, qb), F32)
    for h in range(N_HEADS):
        qf = qT_ref[0, h].astype(F32)
        q_sq = jnp.maximum(q_sq, jnp.sum(qf * qf, axis=0, keepdims=True))
    bound = jnp.sqrt(jnp.max(q_sq) * jnp.max(knorm_ref[0])) + jnp.max(jnp.abs(tab_ref[...]))
    fixed_shift = bound < EXP2_SAFE

    @pl.when(fixed_shift)
    def _():
        run(attn_chunk_fixed)

    @pl.when(jnp.logical_not(fixed_shift))
    def _():
        for h in range(N_HEADS):
            m_refs[h][...] = jnp.full((1, qb), MASK_VALUE, F32)
        run(attn_chunk)

    oT = jnp.concatenate([acc_refs[h][...] / l_refs[h][...] for h in range(N_HEADS)], axis=0)
    o_ref[0] = oT.T.astype(o_ref.dtype)


def _attend(qT, qiT, wT, kidxb, kb, vT, knorm, tabs, *, qb, past, l_true, offsets):
    B, _, _, S = qT.shape
    nc, kc = vT.shape[1], vT.shape[3]
    assert S % qb == 0 and qb % LANES == 0 and past % LANES == 0 and kc % LANES == 0
    topk = min(TOPK_MAX, l_true // 4)
    kern = functools.partial(_attn_kernel, qb=qb, kc=kc, past=past, l_true=l_true,
                             topk=topk, offsets=offsets)
    in_specs = [
        pl.BlockSpec((1, N_HEADS, HEAD_DIM, qb), lambda b, i: (b, 0, 0, i)),
        pl.BlockSpec((1, N_IDX_HEADS, IDX_DIM, qb), lambda b, i: (b, 0, 0, i)),
        pl.BlockSpec((1, N_IDX_HEADS, qb), lambda b, i: (b, 0, i)),
        pl.BlockSpec((1,) + kidxb.shape[1:], lambda b, i: (b, 0, 0, 0)),
        pl.BlockSpec((1,) + kb.shape[1:], lambda b, i: (b, 0, 0, 0, 0)),
        pl.BlockSpec((1,) + vT.shape[1:], lambda b, i: (b, 0, 0, 0)),
        pl.BlockSpec((1,) + knorm.shape[1:], lambda b, i: (b, 0, 0, 0)),
        pl.BlockSpec(tabs.shape, lambda b, i: (0, 0, 0, 0)),
    ]
    return pl.pallas_call(
        kern,
        grid=(B, S // qb),
        in_specs=in_specs,
        out_specs=pl.BlockSpec((1, qb, N_HEADS * HEAD_DIM), lambda b, i: (b, i, 0)),
        out_shape=jax.ShapeDtypeStruct((B, S, N_HEADS * HEAD_DIM), BF16),
        scratch_shapes=[
            pltpu.VMEM((nc, kc, qb), I32),
            pltpu.VMEM((nc, kc, qb), I16),
            pltpu.VMEM((nc, kc, qb), I16),
            pltpu.VMEM((SUBLANES, qb), I32),
            pltpu.VMEM((kc, qb), F32),
        ] + [pltpu.VMEM((kc, qb), F32)] * N_HEADS
          + [pltpu.VMEM((kc, qb), BF16)] * N_HEADS
          + [pltpu.VMEM((1, qb), F32)] * N_HEADS
          + [pltpu.VMEM((1, qb), F32)] * N_HEADS
          + [pltpu.VMEM((HEAD_DIM, qb), F32)] * N_HEADS,
        compiler_params=_cparams(("arbitrary", "arbitrary")),
        name="dsa_attend",
    )(qT, qiT, wT, kidxb, kb, vT, knorm, tabs)


def _mix_kernel(x_ref, attn_ref, conv_ref, wo_ref, g_ref, b_ref, h_ref, *, dn_alpha):
    cat = jnp.concatenate([attn_ref[...], conv_ref[...]], axis=1)
    mixed = _dot(cat, wo_ref[...])
    h_ref[...] = _layer_norm(dn_alpha * x_ref[...] + mixed, g_ref[...], b_ref[...])


def _mix(x2, attn2, conv2, wo, g, b, *, dn_alpha):
    T, D = x2.shape
    tt = min(1024, T)
    assert T % tt == 0
    aw, cw = attn2.shape[1], conv2.shape[1]
    row = lambda i: (i, 0)
    const = lambda i: (0, 0)
    return pl.pallas_call(
        functools.partial(_mix_kernel, dn_alpha=dn_alpha),
        grid=(T // tt,),
        in_specs=[pl.BlockSpec((tt, D), row), pl.BlockSpec((tt, aw), row), pl.BlockSpec((tt, cw), row),
                  pl.BlockSpec(wo.shape, const), pl.BlockSpec((1, D), const), pl.BlockSpec((1, D), const)],
        out_specs=pl.BlockSpec((tt, D), row),
        out_shape=jax.ShapeDtypeStruct((T, D), F32),
        compiler_params=_cparams(("arbitrary",)),
        name="outproj_ln1",
    )(x2, attn2, conv2, wo, g, b)


def _route(logits):
    lane = lax.broadcasted_iota(I32, logits.shape, 1)
    work = logits
    vals, hots = [], []
    for _ in range(TOP_K):
        m = jnp.max(work, axis=1, keepdims=True)
        idx = jnp.min(jnp.where(work == m, lane, LANES), axis=1, keepdims=True)
        hot = lane == idx
        vals.append(m)
        hots.append(hot)
        work = jnp.where(hot, -jnp.inf, work)
    es = [jnp.exp(v - vals[0]) for v in vals]
    den = es[0]
    for e in es[1:]:
        den = den + e
    comb = jnp.zeros(logits.shape, F32)
    picked = jnp.zeros(logits.shape, F32)
    for e, hot in zip(es, hots):
        comb = jnp.where(hot, e / den, comb)
        picked = jnp.where(hot, 1.0, picked)
    return comb, picked


def _moe_kernel(h_ref, wr_ref, br_ref, *rest, dn_alpha, rows):
    wgu_refs, rest = rest[:WEIGHT_SPLIT], rest[WEIGHT_SPLIT:]
    bgu_ref, rest = rest[0], rest[1:]
    wd_refs, rest = rest[:WEIGHT_SPLIT], rest[WEIGHT_SPLIT:]
    bd_ref, g_ref, b_ref, y_ref, hb_ref, combT_ref, slotT_ref = rest
    e = pl.program_id(1)
    tt = h_ref.shape[0]
    ff = wd_refs[0].shape[1]

    @pl.when(e == 0)
    def _():
        hb = h_ref[...].astype(BF16)
        hb_ref[...] = hb
        comb, picked = _route(_dot(hb, wr_ref[...]) + br_ref[...])
        combT_ref[...] = comb.T
        before = (lax.broadcasted_iota(I32, (tt, tt), 1) < lax.broadcasted_iota(I32, (tt, tt), 0))
        rank = _dot(jnp.where(before, 1.0, 0.0).astype(BF16), picked.astype(BF16))
        slotT_ref[...] = jnp.where(picked > 0.0, rank, -1.0).T
        y_ref[...] = jnp.zeros(y_ref.shape, F32)

    slot_row = slotT_ref[pl.ds(e, 1), :]
    gate_row = combT_ref[pl.ds(e, 1), :]
    count = jnp.sum(jnp.where(slot_row >= 0.0, 1, 0))
    sub_iota = lax.broadcasted_iota(I32, (rows, tt), 0).astype(F32)

    def chunk_body(c, carry):
        base = (c * rows).astype(F32)
        hit = slot_row - base == sub_iota
        gather = jnp.where(hit, 1.0, 0.0).astype(BF16)
        row_gate = jnp.sum(jnp.where(hit, gate_row, 0.0), axis=1, keepdims=True)
        xg = _dot(gather, hb_ref[...]).astype(BF16)
        gu = jnp.concatenate([_dot(xg, w[0]) for w in wgu_refs], axis=1) + bgu_ref[0]
        gate = jnp.minimum(gu[:, 0:ff], SWIGLU_LIMIT)
        up = jnp.clip(gu[:, ff:2 * ff], -SWIGLU_LIMIT, SWIGLU_LIMIT)
        act = ((up + 1.0) * (gate * (1.0 / (1.0 + jnp.exp(-SWIGLU_ALPHA * gate))))).astype(BF16)
        y = (jnp.concatenate([_dot(act, w[0]) for w in wd_refs], axis=1) + bd_ref[0]) * row_gate
        y_ref[...] += lax.dot_general(gather, y.astype(BF16), (((0,), (0,)), ((), ())),
                                      preferred_element_type=F32)
        return carry

    lax.fori_loop(0, (count + rows - 1) // rows, chunk_body, 0)

    @pl.when(e == pl.num_programs(1) - 1)
    def _():
        y_ref[...] = _layer_norm(dn_alpha * h_ref[...] + y_ref[...], g_ref[...], b_ref[...])


def _moe(h2, wr, br, wgu, bgu, wd, bd, g, b, *, dn_alpha):
    T, D = h2.shape
    E, _, F2 = wgu.shape
    tt = min(1024, T)
    assert T % tt == 0
    rows = min(tt, -(-(tt * TOP_K * 5 // (E * 4)) // 16) * 16)
    row = lambda i, e: (i, 0)
    const = lambda i, e: (0, 0)
    exp3 = lambda i, e: (e, 0, 0)
    ns = WEIGHT_SPLIT
    col = lambda j: (lambda i, e: (e, 0, j))
    return pl.pallas_call(
        functools.partial(_moe_kernel, dn_alpha=dn_alpha, rows=rows),
        grid=(T // tt, E),
        in_specs=[pl.BlockSpec((tt, D), row), pl.BlockSpec(wr.shape, const), pl.BlockSpec(br.shape, const)]
                 + [pl.BlockSpec((1, D, F2 // ns), col(j)) for j in range(ns)]
                 + [pl.BlockSpec((1, 1, F2), exp3)]
                 + [pl.BlockSpec((1, F2 // 2, D // ns), col(j)) for j in range(ns)]
                 + [pl.BlockSpec((1, 1, D), exp3),
                    pl.BlockSpec((1, D), const), pl.BlockSpec((1, D), const)],
        out_specs=pl.BlockSpec((tt, D), row),
        out_shape=jax.ShapeDtypeStruct((T, D), F32),
        scratch_shapes=[pltpu.VMEM((tt, D), BF16), pltpu.VMEM((LANES, tt), F32), pltpu.VMEM((LANES, tt), F32)],
        compiler_params=_cparams(("arbitrary", "arbitrary")),
        name="moe_ln2",
    )(h2, wr, br, *([wgu] * ns), bgu, *([wd] * ns), bd, g, b)


def _split_gu_kernel(w_ref, perm_ref, o_ref, *, band):
    n = w_ref.shape[2]
    for m in range(n // band):
        both = _dot(w_ref[0, :, m * band:(m + 1) * band].astype(BF16), perm_ref[...]).astype(BF16)
        o_ref[0, :, m * (band // 2):(m + 1) * (band // 2)] = both[:, 0:band // 2]
        o_ref[0, :, n // 2 + m * (band // 2):n // 2 + (m + 1) * (band // 2)] = both[:, band // 2:band]


def _split_gu(w_gu):
    E, D, N = w_gu.shape
    tr = min(512, D)
    band = min(512, N)
    j = jnp.arange(band, dtype=I32)
    src = jnp.where(j < band // 2, 2 * j, 2 * (j - band // 2) + 1)
    perm = (jnp.arange(band, dtype=I32)[:, None] == src[None, :]).astype(BF16)
    return pl.pallas_call(
        functools.partial(_split_gu_kernel, band=band),
        grid=(E, D // tr),
        in_specs=[pl.BlockSpec((1, tr, N), lambda e, r: (e, r, 0)),
                  pl.BlockSpec((band, band), lambda e, r: (0, 0))],
        out_specs=pl.BlockSpec((1, tr, N), lambda e, r: (e, r, 0)),
        out_shape=jax.ShapeDtypeStruct((E, D, N), BF16),
        compiler_params=_cparams(("arbitrary", "arbitrary")),
        name="split_gate_up",
    )(w_gu, perm)


def _prep_weights(w_in, w_conv, w_out, ln1_g, ln1_b, w_router, b_router, w_gu, b_gu, w_down, b_down,
                  ln2_g, ln2_b):
    aw = N_HEADS * HEAD_DIM
    kv = N_KV_HEADS * HEAD_DIM
    iw = N_IDX_HEADS * IDX_DIM
    D = w_in.shape[0]
    cw = D - aw
    o_q, o_k, o_v = 0, aw, aw + kv
    o_qi = o_v + kv
    o_ki = o_qi + iw
    o_wi = o_ki + IDX_DIM
    o_c = o_wi + N_IDX_HEADS
    wa = jnp.concatenate([w_in[:, o_q:o_k], w_in[:, o_qi:o_ki], w_in[:, o_k:o_v], w_in[:, o_v:o_qi]],
                         axis=1).astype(BF16)
    ws = jnp.pad(w_in[:, o_ki:o_c], ((0, 0), (0, LANES - IDX_DIM - N_IDX_HEADS))).astype(BF16)
    wc = w_in[:, o_c:o_c + 3 * cw].astype(BF16)
    E = w_router.shape[1]
    wr = jnp.pad(w_router, ((0, 0), (0, LANES - E))).astype(BF16)
    br = jnp.pad(b_router, (0, LANES - E), constant_values=MASK_VALUE).reshape(1, LANES)
    F = w_gu.shape[2] // 2
    return dict(
        wa=wa, ws=ws, wc=wc, wconv=w_conv, wo=w_out.astype(BF16),
        ln1_g=ln1_g.reshape(1, D), ln1_b=ln1_b.reshape(1, D),
        wr=wr, br=br,
        wgu=_split_gu(w_gu),
        bgu=jnp.concatenate([b_gu[:, 0::2], b_gu[:, 1::2]], axis=1).reshape(E, 1, 2 * F),
        wd=w_down.astype(BF16), bd=b_down.reshape(E, 1, D),
        ln2_g=ln2_g.reshape(1, D), ln2_b=ln2_b.reshape(1, D),
    )


def _decode_layouts(pq, small, k_all, v_all, kidx_all, kc, qb):
    B, S, _ = pq.shape
    aw = N_HEADS * HEAD_DIM
    L, kv = k_all.shape[1], k_all.shape[2]
    lp = -(-L // kc) * kc
    nc = lp // kc
    padk = lambda a: jnp.pad(a, ((0, 0), (0, lp - L), (0, 0)))
    padq = lambda a: jnp.pad(a, ((0, 0), (0, 0), (0, 0), (0, qb - S)))
    q = pq[:, :, 0:aw] * QK_SCALE
    qT = padq(q.reshape(B, S, N_HEADS, HEAD_DIM).transpose(0, 2, 3, 1)).astype(BF16)
    qiT = padq(pq[:, :, aw:].reshape(B, S, N_IDX_HEADS, IDX_DIM).transpose(0, 2, 3, 1)).astype(BF16)
    wT = jnp.pad(small[:, :, IDX_DIM:IDX_DIM + N_IDX_HEADS].transpose(0, 2, 1), ((0, 0), (0, 0), (0, qb - S)))
    kb = padk(k_all).reshape(B, nc, kc, N_KV_HEADS, HEAD_DIM).transpose(0, 1, 3, 2, 4).astype(BF16)
    vT = padk(v_all).reshape(B, nc, kc, kv).transpose(0, 1, 3, 2).astype(BF16)
    kidxb = padk(kidx_all).reshape(B, nc, kc, IDX_DIM).astype(BF16)
    knorm = jnp.sum(kb.astype(F32) ** 2, axis=-1)
    return qT, qiT, wT, kidxb, kb, vT, knorm


def _layer(x, past_k, past_v, past_kidx, conv_buf, rel_bias, w, *, dn_alpha):
    B, S, D = x.shape
    prefill = past_k is None
    outs = _project(x, conv_buf, w["wa"], w["ws"], w["wc"], w["wconv"], attn_layouts=prefill)
    k, v, small, conv, new_buf = outs[:5]
    k_idx = small[:, :, 0:IDX_DIM]
    if prefill:
        past, l_true = 0, S
        qb = min(QUERY_BLOCK, S)
        kb, vT, kidxb, qT, qiT, wT, knorm = outs[5:]
    else:
        past = past_k.shape[1]
        l_true = past + S
        qb = -(-S // LANES) * LANES
        qT, qiT, wT, kidxb, kb, vT, knorm = _decode_layouts(
            outs[5], small, jnp.concatenate([past_k, k], axis=1), jnp.concatenate([past_v, v], axis=1),
            jnp.concatenate([past_kidx, k_idx], axis=1), KEY_CHUNK, qb)
    offsets = (-LANES, 0)
    tabs = _bias_tables(rel_bias, offsets)
    attn = _attend(qT, qiT, wT, kidxb, kb, vT, knorm, tabs, qb=qb, past=past, l_true=l_true, offsets=offsets)
    attn = attn[:, :S]
    h = _mix(x.reshape(B * S, D), attn.reshape(B * S, -1), conv.reshape(B * S, -1), w["wo"],
             w["ln1_g"], w["ln1_b"], dn_alpha=dn_alpha)
    y = _moe(h, w["wr"], w["br"], w["wgu"], w["bgu"], w["wd"], w["bd"],
             w["ln2_g"], w["ln2_b"], dn_alpha=dn_alpha)
    return (y.reshape(B, S, D), k.reshape(B, S, N_KV_HEADS, HEAD_DIM), v.reshape(B, S, N_KV_HEADS, HEAD_DIM),
            k_idx, new_buf)


def kernel(x_prompt, x_sample, cache_k, cache_v, cache_kidx, state_conv, rel_bias, w_in, w_conv, w_out,
           ln1_g, ln1_b, w_router, b_router, w_gu, b_gu, w_down, b_down, ln2_g, ln2_b):
    depth = w_in.shape[0]
    assert depth == 1
    dn_alpha = (2 * depth) ** 0.25
    kv = N_KV_HEADS * HEAD_DIM
    w = _prep_weights(w_in[0], w_conv[0], w_out[0], ln1_g[0], ln1_b[0], w_router[0], b_router[0],
                      w_gu[0], b_gu[0], w_down[0], b_down[0], ln2_g[0], ln2_b[0])
    Bp = x_prompt.shape[0]
    cw = w_conv.shape[2]
    zero_buf = jnp.zeros((Bp, CONV_K - 1, cw), F32)
    yp, k1, v1, i1, c1 = _layer(x_prompt, None, None, None, zero_buf, rel_bias, w, dn_alpha=dn_alpha)
    Bs, P = cache_k.shape[1], cache_k.shape[2]
    ys, k2, v2, i2, c2 = _layer(x_sample, cache_k[0].reshape(Bs, P, kv), cache_v[0].reshape(Bs, P, kv),
                                cache_kidx[0], state_conv[0], rel_bias, w, dn_alpha=dn_alpha)
    return (yp, ys, k1[None], v1[None], i1[None], c1[None], k2[None], v2[None], i2[None], c2[None])
```

```python
import functools
import math

import jax
import jax.numpy as jnp
from jax import lax
from jax.experimental import pallas as pl
from jax.experimental.pallas import tpu as pltpu
from jax.experimental.pallas import tpu_sc as plsc

F32 = jnp.float32
BF16 = jnp.bfloat16
I32 = jnp.int32
I16 = jnp.int16

CHUNK = 64
N_HEADS = 8
HEAD_DIM = 64
N_KV_HEADS = 2
GROUP = N_HEADS // N_KV_HEADS
N_IDX_HEADS = 8
IDX_DIM = 32
TOPK_MAX = 256
CONV_K = 3
N_BUCKETS = 32
MAX_DISTANCE = 128
N_EXPERTS = 32
TOP_K = 4
SWIGLU_LIMIT = 7.0
SWIGLU_ALPHA = 1.702
LN_EPS = 1e-5
MASK_VALUE = -1e30
QK_SCALE = HEAD_DIM ** -0.5 * math.log2(math.e)
EXP2_SAFE = 96.0

LANES = 128
SUBLANES = 8
KEY_CHUNK = 512
QUERY_BLOCK = 256
SLAB = 64
WEIGHT_SPLIT = 4
ROW_TILE = 256
SC_CORES = 2
SC_SUBCORES = 16
SC_ROWS = 64
SORTED_MIN_TOKENS = 2048
INT_MIN = -(2 ** 31)
VMEM_LIMIT = 56 * 1024 * 1024


def _cparams(sem):
    return pltpu.CompilerParams(dimension_semantics=sem, vmem_limit_bytes=VMEM_LIMIT)


def _dot(a, b):
    return jnp.dot(a, b, preferred_element_type=F32)


def _layer_norm(z, g, b):
    mu = jnp.mean(z, axis=-1, keepdims=True)
    zc = z - mu
    var = jnp.mean(zc * zc, axis=-1, keepdims=True)
    return zc * lax.rsqrt(var + LN_EPS) * g + b


def _bucket_thresholds():
    nb = N_BUCKETS // 2
    max_exact = nb // 2
    out = []
    for j in range(1, nb - max_exact):
        out.append(math.ceil(max_exact * (MAX_DISTANCE / max_exact) ** (j / (nb - max_exact)) - 1e-9))
    return tuple(out)


def _bias_table_kernel(rel_ref, tab_ref, *, offsets):
    nb = N_BUCKETS // 2
    max_exact = nb // 2
    thr = _bucket_thresholds()
    ii = lax.broadcasted_iota(I32, (LANES, LANES), 0)
    jj = lax.broadcasted_iota(I32, (LANES, LANES), 1)
    for d, off in enumerate(offsets):
        rel = off + ii - jj
        n = jnp.abs(rel)
        large = jnp.full((LANES, LANES), max_exact, I32)
        for t in thr:
            large = large + jnp.where(n >= t, 1, 0)
        bucket = jnp.where(rel > 0, nb, 0) + jnp.where(n < max_exact, n, large)
        for h in range(N_HEADS):
            acc = jnp.zeros((LANES, LANES), F32)
            for b in range(N_BUCKETS):
                acc = jnp.where(bucket == b, rel_ref[b, h], acc)
            tab_ref[d, h] = (acc - rel_ref[nb - 1, h]) * math.log2(math.e)


def _bias_tables(rel_bias, offsets):
    return pl.pallas_call(
        functools.partial(_bias_table_kernel, offsets=offsets),
        out_shape=jax.ShapeDtypeStruct((len(offsets), N_HEADS, LANES, LANES), F32),
        in_specs=[pl.BlockSpec(memory_space=pltpu.SMEM)],
        out_specs=pl.BlockSpec(memory_space=pltpu.VMEM),
        name="bias_tables",
    )(rel_bias)


def _proj_kernel(x_ref, wa_ref, ws_ref, wc_ref, wconv_ref, buf_ref,
                 k_ref, v_ref, small_ref, conv_ref, nbuf_ref, *rest, ts, attn_layouts):
    s = pl.program_id(1)
    xb = x_ref[0].astype(BF16)
    pa = _dot(xb, wa_ref[...])
    ps = _dot(xb, ws_ref[...])
    pc = _dot(xb, wc_ref[...])
    aw = N_HEADS * HEAD_DIM
    iw = N_IDX_HEADS * IDX_DIM
    kv = N_KV_HEADS * HEAD_DIM
    k = pa[:, aw + iw:aw + iw + kv]
    v = pa[:, aw + iw + kv:aw + iw + 2 * kv]
    k_ref[0] = k
    v_ref[0] = v
    small_ref[0] = ps
    scale = QK_SCALE
    if attn_layouts:
        kb_ref, vT_ref, kidxb_ref, qT_ref, qiT_ref, wT_ref, knorm_ref, carry_ref = rest
        for n in range(N_KV_HEADS):
            kb_ref[0, 0, n] = k[:, n * HEAD_DIM:(n + 1) * HEAD_DIM].astype(BF16)
        kTf = k.astype(BF16).astype(F32).T
        knorm_ref[0, 0] = jnp.concatenate(
            [jnp.sum(kTf[n * HEAD_DIM:(n + 1) * HEAD_DIM] ** 2, axis=0, keepdims=True) for n in range(N_KV_HEADS)],
            axis=0)
        vT_ref[0, 0] = v.T.astype(BF16)
        kidxb_ref[0, 0] = ps[:, 0:IDX_DIM].astype(BF16)
        qT = (pa[:, 0:aw] * scale).T.astype(BF16)
        for h in range(N_HEADS):
            qT_ref[0, h] = qT[h * HEAD_DIM:(h + 1) * HEAD_DIM]
        qiT = pa[:, aw:aw + iw].T.astype(BF16)
        for h in range(N_IDX_HEADS):
            qiT_ref[0, h] = qiT[h * IDX_DIM:(h + 1) * IDX_DIM]
        wT_ref[0] = ps.T[IDX_DIM:IDX_DIM + N_IDX_HEADS]
    else:
        pq_ref, carry_ref = rest
        pq_ref[0] = pa[:, 0:aw + iw]

    cw = pc.shape[1] // 3
    u = pc[:, cw:2 * cw] * pc[:, 2 * cw:3 * cw]

    @pl.when(s == 0)
    def _():
        carry_ref[6:8, :] = buf_ref[0]

    carry_ref[8:8 + ts, :] = u
    y = (carry_ref[6:6 + ts, :] * wconv_ref[0:1, :]
         + carry_ref[7:7 + ts, :] * wconv_ref[1:2, :]
         + u * wconv_ref[2:3, :])
    conv_ref[0] = (pc[:, 0:cw] * y).astype(BF16)
    nb = carry_ref[ts + 6:ts + 8, :]
    nbuf_ref[0] = nb
    carry_ref[6:8, :] = nb


def _project(x, conv_buf, wa, ws, wc, wconv, *, attn_layouts):
    B, S, D = x.shape
    ts = min(KEY_CHUNK, S)
    assert S % ts == 0 and S >= CONV_K - 1
    ns = S // ts
    cw = wc.shape[1] // 3
    kv = N_KV_HEADS * HEAD_DIM
    aw = N_HEADS * HEAD_DIM
    iw = N_IDX_HEADS * IDX_DIM
    out_shape = [
        jax.ShapeDtypeStruct((B, S, kv), F32),
        jax.ShapeDtypeStruct((B, S, kv), F32),
        jax.ShapeDtypeStruct((B, S, LANES), F32),
        jax.ShapeDtypeStruct((B, S, cw), BF16),
        jax.ShapeDtypeStruct((B, CONV_K - 1, cw), F32),
    ]
    out_specs = [
        pl.BlockSpec((1, ts, kv), lambda b, s: (b, s, 0)),
        pl.BlockSpec((1, ts, kv), lambda b, s: (b, s, 0)),
        pl.BlockSpec((1, ts, LANES), lambda b, s: (b, s, 0)),
        pl.BlockSpec((1, ts, cw), lambda b, s: (b, s, 0)),
        pl.BlockSpec((1, CONV_K - 1, cw), lambda b, s: (b, 0, 0)),
    ]
    if attn_layouts:
        out_shape += [
            jax.ShapeDtypeStruct((B, ns, N_KV_HEADS, ts, HEAD_DIM), BF16),
            jax.ShapeDtypeStruct((B, ns, kv, ts), BF16),
            jax.ShapeDtypeStruct((B, ns, ts, IDX_DIM), BF16),
            jax.ShapeDtypeStruct((B, N_HEADS, HEAD_DIM, S), BF16),
            jax.ShapeDtypeStruct((B, N_IDX_HEADS, IDX_DIM, S), BF16),
            jax.ShapeDtypeStruct((B, N_IDX_HEADS, S), F32),
            jax.ShapeDtypeStruct((B, ns, N_KV_HEADS, ts), F32),
        ]
        out_specs += [
            pl.BlockSpec((1, 1, N_KV_HEADS, ts, HEAD_DIM), lambda b, s: (b, s, 0, 0, 0)),
            pl.BlockSpec((1, 1, kv, ts), lambda b, s: (b, s, 0, 0)),
            pl.BlockSpec((1, 1, ts, IDX_DIM), lambda b, s: (b, s, 0, 0)),
            pl.BlockSpec((1, N_HEADS, HEAD_DIM, ts), lambda b, s: (b, 0, 0, s)),
            pl.BlockSpec((1, N_IDX_HEADS, IDX_DIM, ts), lambda b, s: (b, 0, 0, s)),
            pl.BlockSpec((1, N_IDX_HEADS, ts), lambda b, s: (b, 0, s)),
            pl.BlockSpec((1, 1, N_KV_HEADS, ts), lambda b, s: (b, s, 0, 0)),
        ]
    else:
        out_shape += [jax.ShapeDtypeStruct((B, S, aw + iw), F32)]
        out_specs += [pl.BlockSpec((1, ts, aw + iw), lambda b, s: (b, s, 0))]
    const2 = lambda b, s: (0, 0)
    in_specs = [
        pl.BlockSpec((1, ts, D), lambda b, s: (b, s, 0)),
        pl.BlockSpec(wa.shape, const2),
        pl.BlockSpec(ws.shape, const2),
        pl.BlockSpec(wc.shape, const2),
        pl.BlockSpec(wconv.shape, const2),
        pl.BlockSpec((1, CONV_K - 1, cw), lambda b, s: (b, 0, 0)),
    ]
    return pl.pallas_call(
        functools.partial(_proj_kernel, ts=ts, attn_layouts=attn_layouts),
        grid=(B, ns),
        in_specs=in_specs,
        out_specs=tuple(out_specs),
        out_shape=tuple(out_shape),
        scratch_shapes=[pltpu.VMEM((ts + 8, cw), F32)],
        compiler_params=_cparams(("arbitrary", "arbitrary")),
        name="proj_conv",
    )(x, wa, ws, wc, wconv, conv_buf)


def _attn_kernel(qT_ref, qiT_ref, wT_ref, kidx_ref, kb_ref, vT_ref, knorm_ref, tab_ref,
                 o_ref, skey_ref, hi16_ref, lo16_ref, jb_ref, madd_ref, *head_refs,
                 qb, kc, past, l_true, topk, offsets):
    s_refs, p_refs, m_refs, l_refs, acc_refs = (head_refs[g * N_HEADS:(g + 1) * N_HEADS] for g in range(5))
    i = pl.program_id(1)
    qoff = past + i * qb
    adm_end = jnp.minimum(((qoff + qb - 1) // CHUNK + 1) * CHUNK, l_true)
    nck = (adm_end + kc - 1) // kc
    n_far = jnp.maximum(qoff - LANES, 0) // kc
    idx_bits = int(l_true).bit_length()
    groups = kc // SUBLANES

    rowi = lax.broadcasted_iota(I32, (kc, qb), 0)
    qpos = qoff + lax.broadcasted_iota(I32, (1, qb), 1)
    kmax = jnp.minimum((qpos // CHUNK + 1) * CHUNK, l_true)

    def score_body(c, carry):
        kidx_c = kidx_ref[0, c]
        sc = jnp.zeros((kc, qb), F32)
        for h in range(N_IDX_HEADS):
            d = _dot(kidx_c, qiT_ref[0, h])
            sc = sc + wT_ref[0, h:h + 1, :] * jnp.maximum(d, 0.0)
        bits = pltpu.bitcast(sc, I32)
        skey = jnp.where(bits < 0, bits ^ jnp.int32(0x7FFFFFFF), bits)
        skey = jnp.where(skey == -1, 0, skey)
        skey = jnp.where(rowi < kmax - c * kc, skey, jnp.int32(INT_MIN))
        skey_ref[c] = skey
        hi16_ref[c] = (skey >> 16).astype(I16)
        return carry

    lax.fori_loop(0, nck, score_body, 0)

    def count(pred_fn):
        def body(c, part):
            ind = jnp.where(pred_fn(c, skey_ref[c]), 1, 0)
            return part + jnp.sum(ind.reshape(groups, SUBLANES, qb), axis=0)
        part = lax.fori_loop(0, nck, body, jnp.zeros((SUBLANES, qb), I32))
        return jnp.sum(part, axis=0, keepdims=True)

    def count16(ref, pred_fn):
        pack = 2 * SUBLANES

        def body(c, part):
            ind = jnp.where(pred_fn(ref[c]), jnp.bfloat16(1), jnp.bfloat16(0)).reshape(kc // pack, pack, qb)
            terms = [ind[g] for g in range(kc // pack)]
            while len(terms) > 1:
                terms = [terms[g] + terms[g + 1] for g in range(0, len(terms), 2)]
            return part + terms[0].astype(F32)
        part = lax.fori_loop(0, nck, body, jnp.zeros((pack, qb), F32))
        return jnp.sum(part, axis=0, keepdims=True).astype(I32)

    def search16(ref, target):
        def body(it, t_u):
            cand_u = t_u | (jnp.int32(1) << (15 - it))
            cand = (cand_u - 2 ** 15).astype(I16)
            cnt = count16(ref, lambda k: k >= cand)
            return jnp.where(cnt >= target, cand_u, t_u)
        return lax.fori_loop(0, 16, body, jnp.zeros((1, qb), I32))

    hi_u = search16(hi16_ref, topk)
    hi_s = (hi_u - 2 ** 15).astype(I16)
    rest = topk - count16(hi16_ref, lambda k: k > hi_s)

    def low_body(c, carry):
        low = ((skey_ref[c] & 0xFFFF) - 2 ** 15).astype(I16)
        lo16_ref[c] = jnp.where(hi16_ref[c] == hi_s, low, jnp.int16(-2 ** 15))
        return carry

    lax.fori_loop(0, nck, low_body, 0)
    lo_u = search16(lo16_ref, rest)
    thr = (hi_u - 2 ** 15) * 2 ** 16 + lo_u
    cnt_ge = count(lambda c, sk: sk >= thr)
    cnt_gt = count(lambda c, sk: sk > thr)

    jb_ref[...] = jnp.full(jb_ref.shape, 2 ** 31 - 1, I32)

    @pl.when(jnp.max(cnt_ge) > topk)
    def _():
        need = topk - cnt_gt

        def tie_body(it, jb):
            cand = jb | (jnp.int32(1) << (idx_bits - 1 - it))
            cnt = count(lambda c, sk: jnp.logical_and(sk == thr, c * kc + rowi < cand))
            return jnp.where(cnt <= need, cand, jb)

        jb = lax.fori_loop(0, idx_bits, tie_body, jnp.zeros((1, qb), I32))
        jb_ref[...] = jnp.broadcast_to(jb, jb_ref.shape)

    jbound = jnp.where(thr == jnp.int32(INT_MIN), 0, jb_ref[0:1, :])
    thr_m1 = thr - 1

    for h in range(N_HEADS):
        l_refs[h][...] = jnp.zeros((1, qb), F32)
        acc_refs[h][...] = jnp.zeros((HEAD_DIM, qb), F32)
    n_slabs = kc // SLAB

    def fold(x, op):
        return op(x.reshape(SLAB // SUBLANES, SUBLANES, qb), axis=0)

    def near_bias(c, h):
        tiles = []
        for t in range(kc // LANES):
            row = []
            for u in range(qb // LANES):
                d_tu = c * kc + t * LANES - (qoff + u * LANES)
                delta = jnp.zeros((LANES, LANES), F32)
                for di, off in enumerate(offsets):
                    delta = jnp.where(d_tu == off, tab_ref[di, h], delta)
                row.append(delta)
            tiles.append(jnp.concatenate(row, axis=1))
        return jnp.concatenate(tiles, axis=0)

    def attn_chunk(c, near):
        sk = skey_ref[c]
        t_eff = jnp.where(rowi < jbound - c * kc, thr_m1, thr)
        madd_ref[...] = jnp.where(sk > t_eff, 0.0, MASK_VALUE)
        maccs = []
        for h in range(N_HEADS):
            x = _dot(kb_ref[0, c, h // GROUP], qT_ref[0, h]) + madd_ref[...]
            if near:
                x = x + near_bias(c, h)
            s_refs[h][...] = x
            maccs.append(jnp.max(x.reshape(groups, SUBLANES, qb), axis=0))
        alphas = []
        for h in range(N_HEADS):
            s_ref, p_ref, m_ref = s_refs[h], p_refs[h], m_refs[h]
            m_old = m_ref[...]
            m_new = jnp.maximum(m_old, jnp.max(maccs[h], axis=0, keepdims=True))
            for j in range(n_slabs):
                rows = pl.ds(j * SLAB, SLAB)
                p_ref[rows, :] = jnp.exp2(s_ref[rows, :] - m_new).astype(BF16)
            m_ref[...] = m_new
            alphas.append(jnp.exp2(m_old - m_new))
        ones = jnp.ones((2 * SUBLANES, kc), BF16)
        v_aug = [jnp.concatenate([vT_ref[0, c, n * HEAD_DIM:(n + 1) * HEAD_DIM, :], ones], axis=0)
                 for n in range(N_KV_HEADS)]
        for h in range(N_HEADS):
            pv = _dot(v_aug[h // GROUP], p_refs[h][...])
            acc_refs[h][...] = alphas[h] * acc_refs[h][...] + pv[0:HEAD_DIM]
            l_refs[h][...] = alphas[h] * l_refs[h][...] + pv[HEAD_DIM:HEAD_DIM + 1]

    def attn_chunk_fixed(c, near):
        sk = skey_ref[c]
        t_eff = jnp.where(rowi < jbound - c * kc, thr_m1, thr)
        madd_ref[...] = jnp.where(sk > t_eff, 0.0, MASK_VALUE)
        ones = jnp.ones((2 * SUBLANES, kc), BF16)
        v_aug = [jnp.concatenate([vT_ref[0, c, n * HEAD_DIM:(n + 1) * HEAD_DIM, :], ones], axis=0)
                 for n in range(N_KV_HEADS)]
        for h in range(N_HEADS):
            x = _dot(kb_ref[0, c, h // GROUP], qT_ref[0, h]) + madd_ref[...]
            if near:
                x = x + near_bias(c, h)
            p_refs[h][...] = jnp.exp2(x).astype(BF16)
        for h in range(N_HEADS):
            pv = _dot(v_aug[h // GROUP], p_refs[h][...])
            acc_refs[h][...] += pv[0:HEAD_DIM]
            l_refs[h][...] += pv[HEAD_DIM:HEAD_DIM + 1]

    def run(chunk_fn):
        lax.fori_loop(0, n_far, lambda c, carry: (chunk_fn(c, False), carry)[1], 0)
        lax.fori_loop(n_far, nck, lambda c, carry: (chunk_fn(c, True), carry)[1], 0)

    q_sq = jnp.zeros((1, qb), F32)
    for h in range(N_HEADS):
        qf = qT_ref[0, h].astype(F32)
        q_sq = jnp.maximum(q_sq, jnp.sum(qf * qf, axis=0, keepdims=True))
    bound = jnp.sqrt(jnp.max(q_sq) * jnp.max(knorm_ref[0])) + jnp.max(jnp.abs(tab_ref[...]))
    fixed_shift = bound < EXP2_SAFE

    @pl.when(fixed_shift)
    def _():
        run(attn_chunk_fixed)

    @pl.when(jnp.logical_not(fixed_shift))
    def _():
        for h in range(N_HEADS):
            m_refs[h][...] = jnp.full((1, qb), MASK_VALUE, F32)
        run(attn_chunk)

    oT = jnp.concatenate([acc_refs[h][...] / l_refs[h][...] for h in range(N_HEADS)], axis=0)
    o_ref[0] = oT.T.astype(o_ref.dtype)


def _attend(qT, qiT, wT, kidxb, kb, vT, knorm, tabs, *, qb, past, l_true, offsets):
    B, _, _, S = qT.shape
    nc, kc = vT.shape[1], vT.shape[3]
    assert S % qb == 0 and qb % LANES == 0 and past % LANES == 0 and kc % LANES == 0
    topk = min(TOPK_MAX, l_true // 4)
    kern = functools.partial(_attn_kernel, qb=qb, kc=kc, past=past, l_true=l_true,
                             topk=topk, offsets=offsets)
    in_specs = [
        pl.BlockSpec((1, N_HEADS, HEAD_DIM, qb), lambda b, i: (b, 0, 0, i)),
        pl.BlockSpec((1, N_IDX_HEADS, IDX_DIM, qb), lambda b, i: (b, 0, 0, i)),
        pl.BlockSpec((1, N_IDX_HEADS, qb), lambda b, i: (b, 0, i)),
        pl.BlockSpec((1,) + kidxb.shape[1:], lambda b, i: (b, 0, 0, 0)),
        pl.BlockSpec((1,) + kb.shape[1:], lambda b, i: (b, 0, 0, 0, 0)),
        pl.BlockSpec((1,) + vT.shape[1:], lambda b, i: (b, 0, 0, 0)),
        pl.BlockSpec((1,) + knorm.shape[1:], lambda b, i: (b, 0, 0, 0)),
        pl.BlockSpec(tabs.shape, lambda b, i: (0, 0, 0, 0)),
    ]
    return pl.pallas_call(
        kern,
        grid=(B, S // qb),
        in_specs=in_specs,
        out_specs=pl.BlockSpec((1, qb, N_HEADS * HEAD_DIM), lambda b, i: (b, i, 0)),
        out_shape=jax.ShapeDtypeStruct((B, S, N_HEADS * HEAD_DIM), BF16),
        scratch_shapes=[
            pltpu.VMEM((nc, kc, qb), I32),
            pltpu.VMEM((nc, kc, qb), I16),
            pltpu.VMEM((nc, kc, qb), I16),
            pltpu.VMEM((SUBLANES, qb), I32),
            pltpu.VMEM((kc, qb), F32),
        ] + [pltpu.VMEM((kc, qb), F32)] * N_HEADS
          + [pltpu.VMEM((kc, qb), BF16)] * N_HEADS
          + [pltpu.VMEM((1, qb), F32)] * N_HEADS
          + [pltpu.VMEM((1, qb), F32)] * N_HEADS
          + [pltpu.VMEM((HEAD_DIM, qb), F32)] * N_HEADS,
        compiler_params=_cparams(("arbitrary", "arbitrary")),
        name="dsa_attend",
    )(qT, qiT, wT, kidxb, kb, vT, knorm, tabs)


def _mix_kernel(x_ref, attn_ref, conv_ref, wo_ref, g_ref, b_ref, h_ref, *, dn_alpha):
    cat = jnp.concatenate([attn_ref[...], conv_ref[...]], axis=1)
    mixed = _dot(cat, wo_ref[...])
    h_ref[...] = _layer_norm(dn_alpha * x_ref[...] + mixed, g_ref[...], b_ref[...])


def _mix(x2, attn2, conv2, wo, g, b, *, dn_alpha):
    T, D = x2.shape
    tt = min(1024, T)
    assert T % tt == 0
    aw, cw = attn2.shape[1], conv2.shape[1]
    row = lambda i: (i, 0)
    const = lambda i: (0, 0)
    return pl.pallas_call(
        functools.partial(_mix_kernel, dn_alpha=dn_alpha),
        grid=(T // tt,),
        in_specs=[pl.BlockSpec((tt, D), row), pl.BlockSpec((tt, aw), row), pl.BlockSpec((tt, cw), row),
                  pl.BlockSpec(wo.shape, const), pl.BlockSpec((1, D), const), pl.BlockSpec((1, D), const)],
        out_specs=pl.BlockSpec((tt, D), row),
        out_shape=jax.ShapeDtypeStruct((T, D), F32),
        compiler_params=_cparams(("arbitrary",)),
        name="outproj_ln1",
    )(x2, attn2, conv2, wo, g, b)


def _route_picks(logits):
    lane = lax.broadcasted_iota(I32, logits.shape, 1)
    work = logits
    vals, hots = [], []
    for _ in range(TOP_K):
        m = jnp.max(work, axis=1, keepdims=True)
        idx = jnp.min(jnp.where(work == m, lane, LANES), axis=1, keepdims=True)
        hot = lane == idx
        vals.append(m)
        hots.append(hot)
        work = jnp.where(hot, -jnp.inf, work)
    es = [jnp.exp(v - vals[0]) for v in vals]
    den = es[0]
    for e in es[1:]:
        den = den + e
    return hots, [e / den for e in es]


def _route(logits):
    hots, gates = _route_picks(logits)
    comb = jnp.zeros(logits.shape, F32)
    picked = jnp.zeros(logits.shape, F32)
    for gate, hot in zip(gates, hots):
        comb = jnp.where(hot, gate, comb)
        picked = jnp.where(hot, 1.0, picked)
    return comb, picked


def _moe_kernel(h_ref, wr_ref, br_ref, *rest, dn_alpha, rows):
    wgu_refs, rest = rest[:WEIGHT_SPLIT], rest[WEIGHT_SPLIT:]
    bgu_ref, rest = rest[0], rest[1:]
    wd_refs, rest = rest[:WEIGHT_SPLIT], rest[WEIGHT_SPLIT:]
    bd_ref, g_ref, b_ref, y_ref, hb_ref, combT_ref, slotT_ref = rest
    e = pl.program_id(1)
    tt = h_ref.shape[0]
    ff = wd_refs[0].shape[1]

    @pl.when(e == 0)
    def _():
        hb = h_ref[...].astype(BF16)
        hb_ref[...] = hb
        comb, picked = _route(_dot(hb, wr_ref[...]) + br_ref[...])
        combT_ref[...] = comb.T
        before = (lax.broadcasted_iota(I32, (tt, tt), 1) < lax.broadcasted_iota(I32, (tt, tt), 0))
        rank = _dot(jnp.where(before, 1.0, 0.0).astype(BF16), picked.astype(BF16))
        slotT_ref[...] = jnp.where(picked > 0.0, rank, -1.0).T
        y_ref[...] = jnp.zeros(y_ref.shape, F32)

    slot_row = slotT_ref[pl.ds(e, 1), :]
    gate_row = combT_ref[pl.ds(e, 1), :]
    count = jnp.sum(jnp.where(slot_row >= 0.0, 1, 0))
    sub_iota = lax.broadcasted_iota(I32, (rows, tt), 0).astype(F32)

    def chunk_body(c, carry):
        base = (c * rows).astype(F32)
        hit = slot_row - base == sub_iota
        gather = jnp.where(hit, 1.0, 0.0).astype(BF16)
        row_gate = jnp.sum(jnp.where(hit, gate_row, 0.0), axis=1, keepdims=True)
        xg = _dot(gather, hb_ref[...]).astype(BF16)
        gu = jnp.concatenate([_dot(xg, w[0]) for w in wgu_refs], axis=1) + bgu_ref[0]
        gate = jnp.minimum(gu[:, 0:ff], SWIGLU_LIMIT)
        up = jnp.clip(gu[:, ff:2 * ff], -SWIGLU_LIMIT, SWIGLU_LIMIT)
        act = ((up + 1.0) * (gate * (1.0 / (1.0 + jnp.exp(-SWIGLU_ALPHA * gate))))).astype(BF16)
        y = (jnp.concatenate([_dot(act, w[0]) for w in wd_refs], axis=1) + bd_ref[0]) * row_gate
        y_ref[...] += lax.dot_general(gather, y.astype(BF16), (((0,), (0,)), ((), ())),
                                      preferred_element_type=F32)
        return carry

    lax.fori_loop(0, (count + rows - 1) // rows, chunk_body, 0)

    @pl.when(e == pl.num_programs(1) - 1)
    def _():
        y_ref[...] = _layer_norm(dn_alpha * h_ref[...] + y_ref[...], g_ref[...], b_ref[...])


def _moe(h2, wr, br, wgu, bgu, wd, bd, g, b, *, dn_alpha):
    T, D = h2.shape
    E, _, F2 = wgu.shape
    tt = min(1024, T)
    assert T % tt == 0
    rows = min(tt, -(-(tt * TOP_K * 5 // (E * 4)) // 16) * 16)
    row = lambda i, e: (i, 0)
    const = lambda i, e: (0, 0)
    exp3 = lambda i, e: (e, 0, 0)
    ns = WEIGHT_SPLIT
    col = lambda j: (lambda i, e: (e, 0, j))
    return pl.pallas_call(
        functools.partial(_moe_kernel, dn_alpha=dn_alpha, rows=rows),
        grid=(T // tt, E),
        in_specs=[pl.BlockSpec((tt, D), row), pl.BlockSpec(wr.shape, const), pl.BlockSpec(br.shape, const)]
                 + [pl.BlockSpec((1, D, F2 // ns), col(j)) for j in range(ns)]
                 + [pl.BlockSpec((1, 1, F2), exp3)]
                 + [pl.BlockSpec((1, F2 // 2, D // ns), col(j)) for j in range(ns)]
                 + [pl.BlockSpec((1, 1, D), exp3),
                    pl.BlockSpec((1, D), const), pl.BlockSpec((1, D), const)],
        out_specs=pl.BlockSpec((tt, D), row),
        out_shape=jax.ShapeDtypeStruct((T, D), F32),
        scratch_shapes=[pltpu.VMEM((tt, D), BF16), pltpu.VMEM((LANES, tt), F32), pltpu.VMEM((LANES, tt), F32)],
        compiler_params=_cparams(("arbitrary", "arbitrary")),
        name="moe_ln2",
    )(h2, wr, br, *([wgu] * ns), bgu, *([wd] * ns), bd, g, b)


def _pack_pairs(x):
    w = x.shape[1] // 2
    bits = pltpu.bitcast(x.astype(BF16).astype(F32), I32)
    return (bits[:, 0:w] & jnp.int32(-65536)) | lax.shift_right_logical(bits[:, w:2 * w], 16)


def _unpack_pairs(p):
    hi = pltpu.bitcast(p & jnp.int32(-65536), F32)
    lo = pltpu.bitcast(lax.shift_left(p, 16), F32)
    return jnp.concatenate([hi, lo], axis=1)


def _router_kernel(h_ref, wr_ref, br_ref, hpk_ref, info_ref, cnt_ref):
    tt = h_ref.shape[0]
    h = h_ref[...]
    hpk_ref[...] = _pack_pairs(h)
    hots, gates = _route_picks(_dot(h.astype(BF16), wr_ref[...]) + br_ref[...])
    picked = jnp.zeros((tt, LANES), F32)
    for hot in hots:
        picked = jnp.where(hot, 1.0, picked)
    before = (lax.broadcasted_iota(I32, (tt, tt), 1) < lax.broadcasted_iota(I32, (tt, tt), 0))
    rank = _dot(jnp.where(before, 1.0, 0.0).astype(BF16), picked.astype(BF16))
    lane = lax.broadcasted_iota(I32, (tt, LANES), 1)
    info = jnp.zeros((tt, LANES), F32)
    for k, (hot, gate) in enumerate(zip(hots, gates)):
        eid = jnp.sum(jnp.where(hot, lane, 0), axis=1, keepdims=True).astype(F32)
        rk = jnp.sum(jnp.where(hot, rank, 0.0), axis=1, keepdims=True)
        info = jnp.where(lane == k, eid, info)
        info = jnp.where(lane == TOP_K + k, gate, info)
        info = jnp.where(lane == 2 * TOP_K + k, rk, info)
    info_ref[...] = info
    cnt_ref[0] = jnp.broadcast_to(jnp.sum(picked, axis=0, keepdims=True), (SUBLANES, LANES))


def _router(h2, wr, br):
    T, D = h2.shape
    tt = 1024
    row = lambda i: (i, 0)
    const = lambda i: (0, 0)
    return pl.pallas_call(
        _router_kernel,
        grid=(T // tt,),
        in_specs=[pl.BlockSpec((tt, D), row), pl.BlockSpec(wr.shape, const), pl.BlockSpec(br.shape, const)],
        out_specs=(pl.BlockSpec((tt, D // 2), row), pl.BlockSpec((tt, LANES), row),
                   pl.BlockSpec((1, SUBLANES, LANES), lambda i: (i, 0, 0))),
        out_shape=(jax.ShapeDtypeStruct((T, D // 2), I32), jax.ShapeDtypeStruct((T, LANES), F32),
                   jax.ShapeDtypeStruct((T // tt, SUBLANES, LANES), F32)),
        compiler_params=_cparams(("arbitrary",)),
        name="moe_router",
    )(h2, wr, br)


def _sc_gather_rows(table, idx):
    M, W = idx.shape[0], table.shape[1]
    n_workers = SC_CORES * SC_SUBCORES
    per_w = M // n_workers
    n_ch = per_w // SC_ROWS
    assert M == n_workers * n_ch * SC_ROWS
    mesh = plsc.VectorSubcoreMesh(core_axis_name="c", subcore_axis_name="s")

    @functools.partial(
        pl.kernel, mesh=mesh,
        out_type=jax.ShapeDtypeStruct((M, W), table.dtype),
        scratch_types=[pltpu.VMEM((n_ch, SC_ROWS), I32), pltpu.VMEM((SC_ROWS, W), table.dtype),
                       pltpu.SemaphoreType.DMA],
    )
    def gather(table_hbm, idx_hbm, out_hbm, idx_v, rows_v, sem):
        wid = lax.axis_index("s") * SC_CORES + lax.axis_index("c")
        pltpu.sync_copy(idx_hbm.at[wid], idx_v)

        @pl.loop(0, n_ch)
        def _(j):
            pltpu.async_copy(table_hbm.at[idx_v.at[j]], rows_v, sem).wait()
            pltpu.sync_copy(rows_v, out_hbm.at[pl.ds(wid * per_w + j * SC_ROWS, SC_ROWS)])

    return gather(table, idx.reshape(n_workers, n_ch, SC_ROWS))


def _expert_rows_kernel(te_ref, x_ref, *rest):
    wgu_refs, rest = rest[:WEIGHT_SPLIT], rest[WEIGHT_SPLIT:]
    bgu_ref, rest = rest[0], rest[1:]
    wd_refs, rest = rest[:WEIGHT_SPLIT], rest[WEIGHT_SPLIT:]
    bd_ref, y_ref = rest
    ff = wd_refs[0].shape[1]
    xg = _unpack_pairs(x_ref[...]).astype(BF16)
    gu = jnp.concatenate([_dot(xg, w[0]) for w in wgu_refs], axis=1) + bgu_ref[0]
    gate = jnp.minimum(gu[:, 0:ff], SWIGLU_LIMIT)
    up = jnp.clip(gu[:, ff:2 * ff], -SWIGLU_LIMIT, SWIGLU_LIMIT)
    act = ((up + 1.0) * (gate * (1.0 / (1.0 + jnp.exp(-SWIGLU_ALPHA * gate))))).astype(BF16)
    y_ref[...] = _pack_pairs(jnp.concatenate([_dot(act, w[0]) for w in wd_refs], axis=1) + bd_ref[0])


def _expert_rows(tile_expert, xs, wgu, bgu, wd, bd):
    R, W = xs.shape
    E, D, F2 = wgu.shape
    ns = WEIGHT_SPLIT
    col = lambda c: (lambda j, te: (te[j], 0, c))
    exp3 = lambda j, te: (te[j], 0, 0)
    grid_spec = pltpu.PrefetchScalarGridSpec(
        num_scalar_prefetch=1,
        grid=(R // ROW_TILE,),
        in_specs=[pl.BlockSpec((ROW_TILE, W), lambda j, te: (j, 0))]
                 + [pl.BlockSpec((1, D, F2 // ns), col(c)) for c in range(ns)]
                 + [pl.BlockSpec((1, 1, F2), exp3)]
                 + [pl.BlockSpec((1, F2 // 2, D // ns), col(c)) for c in range(ns)]
                 + [pl.BlockSpec((1, 1, D), exp3)],
        out_specs=pl.BlockSpec((ROW_TILE, W), lambda j, te: (j, 0)),
    )
    return pl.pallas_call(
        _expert_rows_kernel,
        grid_spec=grid_spec,
        out_shape=jax.ShapeDtypeStruct((R, W), I32),
        compiler_params=_cparams(("arbitrary",)),
        name="moe_expert_rows",
    )(tile_expert, xs, *([wgu] * ns), bgu, *([wd] * ns), bd)


def _combine_kernel(h_ref, yg_ref, info_ref, g_ref, b_ref, o_ref, *, dn_alpha):
    w = yg_ref.shape[1] // TOP_K
    acc = dn_alpha * h_ref[...]
    for k in range(TOP_K):
        acc = acc + info_ref[:, TOP_K + k:TOP_K + k + 1] * _unpack_pairs(yg_ref[:, k * w:(k + 1) * w])
    o_ref[...] = _layer_norm(acc, g_ref[...], b_ref[...])


def _combine(h2, yg, info, g, b, *, dn_alpha):
    T, D = h2.shape
    tt = 512
    row = lambda i: (i, 0)
    const = lambda i: (0, 0)
    return pl.pallas_call(
        functools.partial(_combine_kernel, dn_alpha=dn_alpha),
        grid=(T // tt,),
        in_specs=[pl.BlockSpec((tt, D), row), pl.BlockSpec((tt, yg.shape[1]), row), pl.BlockSpec((tt, LANES), row),
                  pl.BlockSpec((1, D), const), pl.BlockSpec((1, D), const)],
        out_specs=pl.BlockSpec((tt, D), row),
        out_shape=jax.ShapeDtypeStruct((T, D), F32),
        compiler_params=_cparams(("arbitrary",)),
        name="moe_combine_ln2",
    )(h2, yg, info, g, b)


def _moe_sorted(h2, wr, br, wgu, bgu, wd, bd, g, b, *, dn_alpha):
    T, D = h2.shape
    E = wgu.shape[0]
    hpk, info, cnt = _router(h2, wr, br)
    eid = info[:, 0:TOP_K].astype(I32)
    rank = info[:, 2 * TOP_K:3 * TOP_K].astype(I32)
    cnt = cnt[:, 0, 0:E].astype(I32)
    tile_base = jnp.cumsum(cnt, axis=0) - cnt
    total = jnp.sum(cnt, axis=0)
    padded = -(-total // ROW_TILE) * ROW_TILE
    off = jnp.cumsum(padded) - padded
    tile_of = jnp.arange(T, dtype=I32) // 1024
    dest = off[eid] + tile_base[tile_of[:, None], eid] + rank
    R = T * TOP_K + E * ROW_TILE
    tok = (jnp.arange(R, dtype=I32) % T).at[dest.reshape(-1)].set(
        jnp.repeat(jnp.arange(T, dtype=I32), TOP_K))
    tile_expert = jnp.minimum(
        jnp.searchsorted(off + padded, jnp.arange(R // ROW_TILE, dtype=I32) * ROW_TILE, side="right"),
        E - 1).astype(I32)
    xs = _sc_gather_rows(hpk, tok)
    ys = _expert_rows(tile_expert, xs, wgu, bgu, wd, bd)
    yg = _sc_gather_rows(ys, dest.reshape(-1)).reshape(T, TOP_K * (D // 2))
    return _combine(h2, yg, info, g, b, dn_alpha=dn_alpha)


def _split_gu_kernel(w_ref, perm_ref, o_ref, *, band):
    n = w_ref.shape[2]
    for m in range(n // band):
        both = _dot(w_ref[0, :, m * band:(m + 1) * band].astype(BF16), perm_ref[...]).astype(BF16)
        o_ref[0, :, m * (band // 2):(m + 1) * (band // 2)] = both[:, 0:band // 2]
        o_ref[0, :, n // 2 + m * (band // 2):n // 2 + (m + 1) * (band // 2)] = both[:, band // 2:band]


def _split_gu(w_gu):
    E, D, N = w_gu.shape
    tr = min(512, D)
    band = min(512, N)
    j = jnp.arange(band, dtype=I32)
    src = jnp.where(j < band // 2, 2 * j, 2 * (j - band // 2) + 1)
    perm = (jnp.arange(band, dtype=I32)[:, None] == src[None, :]).astype(BF16)
    return pl.pallas_call(
        functools.partial(_split_gu_kernel, band=band),
        grid=(E, D // tr),
        in_specs=[pl.BlockSpec((1, tr, N), lambda e, r: (e, r, 0)),
                  pl.BlockSpec((band, band), lambda e, r: (0, 0))],
        out_specs=pl.BlockSpec((1, tr, N), lambda e, r: (e, r, 0)),
        out_shape=jax.ShapeDtypeStruct((E, D, N), BF16),
        compiler_params=_cparams(("arbitrary", "arbitrary")),
        name="split_gate_up",
    )(w_gu, perm)


def _prep_weights(w_in, w_conv, w_out, ln1_g, ln1_b, w_router, b_router, w_gu, b_gu, w_down, b_down,
                  ln2_g, ln2_b):
    aw = N_HEADS * HEAD_DIM
    kv = N_KV_HEADS * HEAD_DIM
    iw = N_IDX_HEADS * IDX_DIM
    D = w_in.shape[0]
    cw = D - aw
    o_q, o_k, o_v = 0, aw, aw + kv
    o_qi = o_v + kv
    o_ki = o_qi + iw
    o_wi = o_ki + IDX_DIM
    o_c = o_wi + N_IDX_HEADS
    wa = jnp.concatenate([w_in[:, o_q:o_k], w_in[:, o_qi:o_ki], w_in[:, o_k:o_v], w_in[:, o_v:o_qi]],
                         axis=1).astype(BF16)
    ws = jnp.pad(w_in[:, o_ki:o_c], ((0, 0), (0, LANES - IDX_DIM - N_IDX_HEADS))).astype(BF16)
    wc = w_in[:, o_c:o_c + 3 * cw].astype(BF16)
    E = w_router.shape[1]
    wr = jnp.pad(w_router, ((0, 0), (0, LANES - E))).astype(BF16)
    br = jnp.pad(b_router, (0, LANES - E), constant_values=MASK_VALUE).reshape(1, LANES)
    F = w_gu.shape[2] // 2
    return dict(
        wa=wa, ws=ws, wc=wc, wconv=w_conv, wo=w_out.astype(BF16),
        ln1_g=ln1_g.reshape(1, D), ln1_b=ln1_b.reshape(1, D),
        wr=wr, br=br,
        wgu=_split_gu(w_gu),
        bgu=jnp.concatenate([b_gu[:, 0::2], b_gu[:, 1::2]], axis=1).reshape(E, 1, 2 * F),
        wd=w_down.astype(BF16), bd=b_down.reshape(E, 1, D),
        ln2_g=ln2_g.reshape(1, D), ln2_b=ln2_b.reshape(1, D),
    )


def _decode_layouts(pq, small, k_all, v_all, kidx_all, kc, qb):
    B, S, _ = pq.shape
    aw = N_HEADS * HEAD_DIM
    L, kv = k_all.shape[1], k_all.shape[2]
    lp = -(-L // kc) * kc
    nc = lp // kc
    padk = lambda a: jnp.pad(a, ((0, 0), (0, lp - L), (0, 0)))
    padq = lambda a: jnp.pad(a, ((0, 0), (0, 0), (0, 0), (0, qb - S)))
    q = pq[:, :, 0:aw] * QK_SCALE
    qT = padq(q.reshape(B, S, N_HEADS, HEAD_DIM).transpose(0, 2, 3, 1)).astype(BF16)
    qiT = padq(pq[:, :, aw:].reshape(B, S, N_IDX_HEADS, IDX_DIM).transpose(0, 2, 3, 1)).astype(BF16)
    wT = jnp.pad(small[:, :, IDX_DIM:IDX_DIM + N_IDX_HEADS].transpose(0, 2, 1), ((0, 0), (0, 0), (0, qb - S)))
    kb = padk(k_all).reshape(B, nc, kc, N_KV_HEADS, HEAD_DIM).transpose(0, 1, 3, 2, 4).astype(BF16)
    vT = padk(v_all).reshape(B, nc, kc, kv).transpose(0, 1, 3, 2).astype(BF16)
    kidxb = padk(kidx_all).reshape(B, nc, kc, IDX_DIM).astype(BF16)
    knorm = jnp.sum(kb.astype(F32) ** 2, axis=-1)
    return qT, qiT, wT, kidxb, kb, vT, knorm


def _layer(x, past_k, past_v, past_kidx, conv_buf, rel_bias, w, *, dn_alpha):
    B, S, D = x.shape
    prefill = past_k is None
    outs = _project(x, conv_buf, w["wa"], w["ws"], w["wc"], w["wconv"], attn_layouts=prefill)
    k, v, small, conv, new_buf = outs[:5]
    k_idx = small[:, :, 0:IDX_DIM]
    if prefill:
        past, l_true = 0, S
        qb = min(QUERY_BLOCK, S)
        kb, vT, kidxb, qT, qiT, wT, knorm = outs[5:]
    else:
        past = past_k.shape[1]
        l_true = past + S
        qb = -(-S // LANES) * LANES
        qT, qiT, wT, kidxb, kb, vT, knorm = _decode_layouts(
            outs[5], small, jnp.concatenate([past_k, k], axis=1), jnp.concatenate([past_v, v], axis=1),
            jnp.concatenate([past_kidx, k_idx], axis=1), KEY_CHUNK, qb)
    offsets = (-LANES, 0)
    tabs = _bias_tables(rel_bias, offsets)
    attn = _attend(qT, qiT, wT, kidxb, kb, vT, knorm, tabs, qb=qb, past=past, l_true=l_true, offsets=offsets)
    attn = attn[:, :S]
    h = _mix(x.reshape(B * S, D), attn.reshape(B * S, -1), conv.reshape(B * S, -1), w["wo"],
             w["ln1_g"], w["ln1_b"], dn_alpha=dn_alpha)
    sc_unit = SC_CORES * SC_SUBCORES * SC_ROWS
    sortable = (B * S) % 1024 == 0 and (B * S * TOP_K) % sc_unit == 0 and (N_EXPERTS * ROW_TILE) % sc_unit == 0
    moe = _moe_sorted if sortable and B * S >= SORTED_MIN_TOKENS else _moe
    y = moe(h, w["wr"], w["br"], w["wgu"], w["bgu"], w["wd"], w["bd"],
            w["ln2_g"], w["ln2_b"], dn_alpha=dn_alpha)
    return (y.reshape(B, S, D), k.reshape(B, S, N_KV_HEADS, HEAD_DIM), v.reshape(B, S, N_KV_HEADS, HEAD_DIM),
            k_idx, new_buf)


def kernel(x_prompt, x_sample, cache_k, cache_v, cache_kidx, state_conv, rel_bias, w_in, w_conv, w_out,
           ln1_g, ln1_b, w_router, b_router, w_gu, b_gu, w_down, b_down, ln2_g, ln2_b):
    depth = w_in.shape[0]
    assert depth == 1
    dn_alpha = (2 * depth) ** 0.25
    kv = N_KV_HEADS * HEAD_DIM
    w = _prep_weights(w_in[0], w_conv[0], w_out[0], ln1_g[0], ln1_b[0], w_router[0], b_router[0],
                      w_gu[0], b_gu[0], w_down[0], b_down[0], ln2_g[0], ln2_b[0])
    Bp = x_prompt.shape[0]
    cw = w_conv.shape[2]
    zero_buf = jnp.zeros((Bp, CONV_K - 1, cw), F32)
    yp, k1, v1, i1, c1 = _layer(x_prompt, None, None, None, zero_buf, rel_bias, w, dn_alpha=dn_alpha)
    Bs, P = cache_k.shape[1], cache_k.shape[2]
    ys, k2, v2, i2, c2 = _layer(x_sample, cache_k[0].reshape(Bs, P, kv), cache_v[0].reshape(Bs, P, kv),
                                cache_kidx[0], state_conv[0], rel_bias, w, dn_alpha=dn_alpha)
    return (yp, ys, k1[None], v1[None], i1[None], c1[None], k2[None], v2[None], i2[None], c2[None])
```

```python
import functools
import math

import jax
import jax.numpy as jnp
from jax import lax
from jax.experimental import pallas as pl
from jax.experimental.pallas import tpu as pltpu
from jax.experimental.pallas import tpu_sc as plsc

F32 = jnp.float32
BF16 = jnp.bfloat16
I32 = jnp.int32
I16 = jnp.int16

CHUNK = 64
N_HEADS = 8
HEAD_DIM = 64
N_KV_HEADS = 2
GROUP = N_HEADS // N_KV_HEADS
N_IDX_HEADS = 8
IDX_DIM = 32
TOPK_MAX = 256
CONV_K = 3
N_BUCKETS = 32
MAX_DISTANCE = 128
N_EXPERTS = 32
TOP_K = 4
SWIGLU_LIMIT = 7.0
SWIGLU_ALPHA = 1.702
LN_EPS = 1e-5
MASK_VALUE = -1e30
QK_SCALE = HEAD_DIM ** -0.5 * math.log2(math.e)
EXP2_SAFE = 96.0

LANES = 128
SUBLANES = 8
KEY_CHUNK = 512
QUERY_BLOCK = 256
SLAB = 64
WEIGHT_SPLIT = 4
ROW_TILE = 256
SC_CORES = 2
SC_SUBCORES = 16
SC_ROWS = 64
SORTED_MIN_TOKENS = 2048
INT_MIN = -(2 ** 31)
VMEM_LIMIT = 56 * 1024 * 1024


def _cparams(sem):
    return pltpu.CompilerParams(dimension_semantics=sem, vmem_limit_bytes=VMEM_LIMIT)


def _dot(a, b):
    return jnp.dot(a, b, preferred_element_type=F32)


def _layer_norm(z, g, b):
    mu = jnp.mean(z, axis=-1, keepdims=True)
    zc = z - mu
    var = jnp.mean(zc * zc, axis=-1, keepdims=True)
    return zc * lax.rsqrt(var + LN_EPS) * g + b


def _bucket_thresholds():
    nb = N_BUCKETS // 2
    max_exact = nb // 2
    out = []
    for j in range(1, nb - max_exact):
        out.append(math.ceil(max_exact * (MAX_DISTANCE / max_exact) ** (j / (nb - max_exact)) - 1e-9))
    return tuple(out)


def _bias_table_kernel(rel_ref, tab_ref, *, offsets):
    nb = N_BUCKETS // 2
    max_exact = nb // 2
    thr = _bucket_thresholds()
    ii = lax.broadcasted_iota(I32, (LANES, LANES), 0)
    jj = lax.broadcasted_iota(I32, (LANES, LANES), 1)
    for d, off in enumerate(offsets):
        rel = off + ii - jj
        n = jnp.abs(rel)
        large = jnp.full((LANES, LANES), max_exact, I32)
        for t in thr:
            large = large + jnp.where(n >= t, 1, 0)
        bucket = jnp.where(rel > 0, nb, 0) + jnp.where(n < max_exact, n, large)
        for h in range(N_HEADS):
            acc = jnp.zeros((LANES, LANES), F32)
            for b in range(N_BUCKETS):
                acc = jnp.where(bucket == b, rel_ref[b, h], acc)
            tab_ref[d, h] = (acc - rel_ref[nb - 1, h]) * math.log2(math.e)


def _bias_tables(rel_bias, offsets):
    return pl.pallas_call(
        functools.partial(_bias_table_kernel, offsets=offsets),
        out_shape=jax.ShapeDtypeStruct((len(offsets), N_HEADS, LANES, LANES), F32),
        in_specs=[pl.BlockSpec(memory_space=pltpu.SMEM)],
        out_specs=pl.BlockSpec(memory_space=pltpu.VMEM),
        name="bias_tables",
    )(rel_bias)


def _proj_kernel(x_ref, wa_ref, ws_ref, wc_ref, wconv_ref, buf_ref,
                 k_ref, v_ref, small_ref, conv_ref, nbuf_ref, *rest, ts, attn_layouts):
    s = pl.program_id(1)
    xb = x_ref[0].astype(BF16)
    pa = _dot(xb, wa_ref[...])
    ps = _dot(xb, ws_ref[...])
    pc = _dot(xb, wc_ref[...])
    aw = N_HEADS * HEAD_DIM
    iw = N_IDX_HEADS * IDX_DIM
    kv = N_KV_HEADS * HEAD_DIM
    k = pa[:, aw + iw:aw + iw + kv]
    v = pa[:, aw + iw + kv:aw + iw + 2 * kv]
    k_ref[0] = k
    v_ref[0] = v
    small_ref[0] = ps
    scale = QK_SCALE
    if attn_layouts:
        kb_ref, vT_ref, kidxb_ref, qT_ref, qiT_ref, wT_ref, knorm_ref, carry_ref = rest
        for n in range(N_KV_HEADS):
            kb_ref[0, 0, n] = k[:, n * HEAD_DIM:(n + 1) * HEAD_DIM].astype(BF16)
        kTf = k.astype(BF16).astype(F32).T
        knorm_ref[0, 0] = jnp.concatenate(
            [jnp.sum(kTf[n * HEAD_DIM:(n + 1) * HEAD_DIM] ** 2, axis=0, keepdims=True) for n in range(N_KV_HEADS)],
            axis=0)
        vT_ref[0, 0] = v.T.astype(BF16)
        kidxb_ref[0, 0] = ps[:, 0:IDX_DIM].astype(BF16)
        qT = (pa[:, 0:aw] * scale).T.astype(BF16)
        for h in range(N_HEADS):
            qT_ref[0, h] = qT[h * HEAD_DIM:(h + 1) * HEAD_DIM]
        qiT = pa[:, aw:aw + iw].T.astype(BF16)
        for h in range(N_IDX_HEADS):
            qiT_ref[0, h] = qiT[h * IDX_DIM:(h + 1) * IDX_DIM]
        wT_ref[0] = ps.T[IDX_DIM:IDX_DIM + N_IDX_HEADS]
    else:
        pq_ref, carry_ref = rest
        pq_ref[0] = pa[:, 0:aw + iw]

    cw = pc.shape[1] // 3
    u = pc[:, cw:2 * cw] * pc[:, 2 * cw:3 * cw]

    @pl.when(s == 0)
    def _():
        carry_ref[6:8, :] = buf_ref[0]

    carry_ref[8:8 + ts, :] = u
    y = (carry_ref[6:6 + ts, :] * wconv_ref[0:1, :]
         + carry_ref[7:7 + ts, :] * wconv_ref[1:2, :]
         + u * wconv_ref[2:3, :])
    conv_ref[0] = (pc[:, 0:cw] * y).astype(BF16)
    nb = carry_ref[ts + 6:ts + 8, :]
    nbuf_ref[0] = nb
    carry_ref[6:8, :] = nb


def _project(x, conv_buf, wa, ws, wc, wconv, *, attn_layouts):
    B, S, D = x.shape
    ts = min(KEY_CHUNK, S)
    assert S % ts == 0 and S >= CONV_K - 1
    ns = S // ts
    cw = wc.shape[1] // 3
    kv = N_KV_HEADS * HEAD_DIM
    aw = N_HEADS * HEAD_DIM
    iw = N_IDX_HEADS * IDX_DIM
    out_shape = [
        jax.ShapeDtypeStruct((B, S, kv), F32),
        jax.ShapeDtypeStruct((B, S, kv), F32),
        jax.ShapeDtypeStruct((B, S, LANES), F32),
        jax.ShapeDtypeStruct((B, S, cw), BF16),
        jax.ShapeDtypeStruct((B, CONV_K - 1, cw), F32),
    ]
    out_specs = [
        pl.BlockSpec((1, ts, kv), lambda b, s: (b, s, 0)),
        pl.BlockSpec((1, ts, kv), lambda b, s: (b, s, 0)),
        pl.BlockSpec((1, ts, LANES), lambda b, s: (b, s, 0)),
        pl.BlockSpec((1, ts, cw), lambda b, s: (b, s, 0)),
        pl.BlockSpec((1, CONV_K - 1, cw), lambda b, s: (b, 0, 0)),
    ]
    if attn_layouts:
        out_shape += [
            jax.ShapeDtypeStruct((B, ns, N_KV_HEADS, ts, HEAD_DIM), BF16),
            jax.ShapeDtypeStruct((B, ns, kv, ts), BF16),
            jax.ShapeDtypeStruct((B, ns, ts, IDX_DIM), BF16),
            jax.ShapeDtypeStruct((B, N_HEADS, HEAD_DIM, S), BF16),
            jax.ShapeDtypeStruct((B, N_IDX_HEADS, IDX_DIM, S), BF16),
            jax.ShapeDtypeStruct((B, N_IDX_HEADS, S), F32),
            jax.ShapeDtypeStruct((B, ns, N_KV_HEADS, ts), F32),
        ]
        out_specs += [
            pl.BlockSpec((1, 1, N_KV_HEADS, ts, HEAD_DIM), lambda b, s: (b, s, 0, 0, 0)),
            pl.BlockSpec((1, 1, kv, ts), lambda b, s: (b, s, 0, 0)),
            pl.BlockSpec((1, 1, ts, IDX_DIM), lambda b, s: (b, s, 0, 0)),
            pl.BlockSpec((1, N_HEADS, HEAD_DIM, ts), lambda b, s: (b, 0, 0, s)),
            pl.BlockSpec((1, N_IDX_HEADS, IDX_DIM, ts), lambda b, s: (b, 0, 0, s)),
            pl.BlockSpec((1, N_IDX_HEADS, ts), lambda b, s: (b, 0, s)),
            pl.BlockSpec((1, 1, N_KV_HEADS, ts), lambda b, s: (b, s, 0, 0)),
        ]
    else:
        out_shape += [jax.ShapeDtypeStruct((B, S, aw + iw), F32)]
        out_specs += [pl.BlockSpec((1, ts, aw + iw), lambda b, s: (b, s, 0))]
    const2 = lambda b, s: (0, 0)
    in_specs = [
        pl.BlockSpec((1, ts, D), lambda b, s: (b, s, 0)),
        pl.BlockSpec(wa.shape, const2),
        pl.BlockSpec(ws.shape, const2),
        pl.BlockSpec(wc.shape, const2),
        pl.BlockSpec(wconv.shape, const2),
        pl.BlockSpec((1, CONV_K - 1, cw), lambda b, s: (b, 0, 0)),
    ]
    return pl.pallas_call(
        functools.partial(_proj_kernel, ts=ts, attn_layouts=attn_layouts),
        grid=(B, ns),
        in_specs=in_specs,
        out_specs=tuple(out_specs),
        out_shape=tuple(out_shape),
        scratch_shapes=[pltpu.VMEM((ts + 8, cw), F32)],
        compiler_params=_cparams(("arbitrary", "arbitrary")),
        name="proj_conv",
    )(x, wa, ws, wc, wconv, conv_buf)


def _attn_kernel(qT_ref, qiT_ref, wT_ref, kidx_ref, kb_ref, vT_ref, knorm_ref, tab_ref,
                 o_ref, skey_ref, hi16_ref, lo16_ref, jb_ref, madd_ref, *head_refs,
                 qb, kc, past, l_true, topk, offsets):
    s_refs, p_refs, m_refs, l_refs, acc_refs = (head_refs[g * N_HEADS:(g + 1) * N_HEADS] for g in range(5))
    i = pl.program_id(1)
    qoff = past + i * qb
    adm_end = jnp.minimum(((qoff + qb - 1) // CHUNK + 1) * CHUNK, l_true)
    nck = (adm_end + kc - 1) // kc
    n_far = jnp.maximum(qoff - LANES, 0) // kc
    idx_bits = int(l_true).bit_length()
    groups = kc // SUBLANES

    rowi = lax.broadcasted_iota(I32, (kc, qb), 0)
    qpos = qoff + lax.broadcasted_iota(I32, (1, qb), 1)
    kmax = jnp.minimum((qpos // CHUNK + 1) * CHUNK, l_true)

    def score_body(c, carry):
        kidx_c = kidx_ref[0, c]
        sc = jnp.zeros((kc, qb), F32)
        for h in range(N_IDX_HEADS):
            d = _dot(kidx_c, qiT_ref[0, h])
            sc = sc + wT_ref[0, h:h + 1, :] * jnp.maximum(d, 0.0)
        bits = pltpu.bitcast(sc, I32)
        skey = jnp.where(bits < 0, bits ^ jnp.int32(0x7FFFFFFF), bits)
        skey = jnp.where(skey == -1, 0, skey)
        skey = jnp.where(rowi < kmax - c * kc, skey, jnp.int32(INT_MIN))
        skey_ref[c] = skey
        hi16_ref[c] = (skey >> 16).astype(I16)
        return carry

    lax.fori_loop(0, nck, score_body, 0)

    def count(pred_fn):
        def body(c, part):
            ind = jnp.where(pred_fn(c, skey_ref[c]), 1, 0)
            return part + jnp.sum(ind.reshape(groups, SUBLANES, qb), axis=0)
        part = lax.fori_loop(0, nck, body, jnp.zeros((SUBLANES, qb), I32))
        return jnp.sum(part, axis=0, keepdims=True)

    def count16(ref, pred_fn):
        pack = 2 * SUBLANES

        def body(c, part):
            ind = jnp.where(pred_fn(ref[c]), jnp.bfloat16(1), jnp.bfloat16(0)).reshape(kc // pack, pack, qb)
            terms = [ind[g] for g in range(kc // pack)]
            while len(terms) > 1:
                terms = [terms[g] + terms[g + 1] for g in range(0, len(terms), 2)]
            return part + terms[0].astype(F32)
        part = lax.fori_loop(0, nck, body, jnp.zeros((pack, qb), F32))
        return jnp.sum(part, axis=0, keepdims=True).astype(I32)

    def search16(ref, target):
        def body(it, t_u):
            cand_u = t_u | (jnp.int32(1) << (15 - it))
            cand = (cand_u - 2 ** 15).astype(I16)
            cnt = count16(ref, lambda k: k >= cand)
            return jnp.where(cnt >= target, cand_u, t_u)
        return lax.fori_loop(0, 16, body, jnp.zeros((1, qb), I32))

    hi_u = search16(hi16_ref, topk)
    hi_s = (hi_u - 2 ** 15).astype(I16)
    rest = topk - count16(hi16_ref, lambda k: k > hi_s)

    def low_body(c, carry):
        low = ((skey_ref[c] & 0xFFFF) - 2 ** 15).astype(I16)
        lo16_ref[c] = jnp.where(hi16_ref[c] == hi_s, low, jnp.int16(-2 ** 15))
        return carry

    lax.fori_loop(0, nck, low_body, 0)
    lo_u = search16(lo16_ref, rest)
    thr = (hi_u - 2 ** 15) * 2 ** 16 + lo_u
    cnt_ge = count(lambda c, sk: sk >= thr)
    cnt_gt = count(lambda c, sk: sk > thr)

    jb_ref[...] = jnp.full(jb_ref.shape, 2 ** 31 - 1, I32)

    @pl.when(jnp.max(cnt_ge) > topk)
    def _():
        need = topk - cnt_gt

        def tie_body(it, jb):
            cand = jb | (jnp.int32(1) << (idx_bits - 1 - it))
            cnt = count(lambda c, sk: jnp.logical_and(sk == thr, c * kc + rowi < cand))
            return jnp.where(cnt <= need, cand, jb)

        jb = lax.fori_loop(0, idx_bits, tie_body, jnp.zeros((1, qb), I32))
        jb_ref[...] = jnp.broadcast_to(jb, jb_ref.shape)

    jbound = jnp.where(thr == jnp.int32(INT_MIN), 0, jb_ref[0:1, :])
    thr_m1 = thr - 1

    for h in range(N_HEADS):
        l_refs[h][...] = jnp.zeros((1, qb), F32)
        acc_refs[h][...] = jnp.zeros((HEAD_DIM, qb), F32)
    n_slabs = kc // SLAB

    def fold(x, op):
        return op(x.reshape(SLAB // SUBLANES, SUBLANES, qb), axis=0)

    def near_bias(c, h):
        tiles = []
        for t in range(kc // LANES):
            row = []
            for u in range(qb // LANES):
                d_tu = c * kc + t * LANES - (qoff + u * LANES)
                delta = jnp.zeros((LANES, LANES), F32)
                for di, off in enumerate(offsets):
                    delta = jnp.where(d_tu == off, tab_ref[di, h], delta)
                row.append(delta)
            tiles.append(jnp.concatenate(row, axis=1))
        return jnp.concatenate(tiles, axis=0)

    def attn_chunk(c, near):
        sk = skey_ref[c]
        t_eff = jnp.where(rowi < jbound - c * kc, thr_m1, thr)
        madd_ref[...] = jnp.where(sk > t_eff, 0.0, MASK_VALUE)
        maccs = []
        for h in range(N_HEADS):
            x = _dot(kb_ref[0, c, h // GROUP], qT_ref[0, h]) + madd_ref[...]
            if near:
                x = x + near_bias(c, h)
            s_refs[h][...] = x
            maccs.append(jnp.max(x.reshape(groups, SUBLANES, qb), axis=0))
        alphas = []
        for h in range(N_HEADS):
            s_ref, p_ref, m_ref = s_refs[h], p_refs[h], m_refs[h]
            m_old = m_ref[...]
            m_new = jnp.maximum(m_old, jnp.max(maccs[h], axis=0, keepdims=True))
            for j in range(n_slabs):
                rows = pl.ds(j * SLAB, SLAB)
                p_ref[rows, :] = jnp.exp2(s_ref[rows, :] - m_new).astype(BF16)
            m_ref[...] = m_new
            alphas.append(jnp.exp2(m_old - m_new))
        ones = jnp.ones((2 * SUBLANES, kc), BF16)
        v_aug = [jnp.concatenate([vT_ref[0, c, n * HEAD_DIM:(n + 1) * HEAD_DIM, :], ones], axis=0)
                 for n in range(N_KV_HEADS)]
        for h in range(N_HEADS):
            pv = _dot(v_aug[h // GROUP], p_refs[h][...])
            acc_refs[h][...] = alphas[h] * acc_refs[h][...] + pv[0:HEAD_DIM]
            l_refs[h][...] = alphas[h] * l_refs[h][...] + pv[HEAD_DIM:HEAD_DIM + 1]

    def attn_chunk_fixed(c, near):
        sk = skey_ref[c]
        t_eff = jnp.where(rowi < jbound - c * kc, thr_m1, thr)
        madd_ref[...] = jnp.where(sk > t_eff, 0.0, MASK_VALUE)
        ones = jnp.ones((2 * SUBLANES, kc), BF16)
        v_aug = [jnp.concatenate([vT_ref[0, c, n * HEAD_DIM:(n + 1) * HEAD_DIM, :], ones], axis=0)
                 for n in range(N_KV_HEADS)]
        for h in range(N_HEADS):
            x = _dot(kb_ref[0, c, h // GROUP], qT_ref[0, h]) + madd_ref[...]
            if near:
                x = x + near_bias(c, h)
            p_refs[h][...] = jnp.exp2(x).astype(BF16)
        for h in range(N_HEADS):
            pv = _dot(v_aug[h // GROUP], p_refs[h][...])
            acc_refs[h][...] += pv[0:HEAD_DIM]
            l_refs[h][...] += pv[HEAD_DIM:HEAD_DIM + 1]

    def run(chunk_fn):
        lax.fori_loop(0, n_far, lambda c, carry: (chunk_fn(c, False), carry)[1], 0)
        lax.fori_loop(n_far, nck, lambda c, carry: (chunk_fn(c, True), carry)[1], 0)

    q_sq = jnp.zeros((1, qb), F32)
    for h in range(N_HEADS):
        qf = qT_ref[0, h].astype(F32)
        q_sq = jnp.maximum(q_sq, jnp.sum(qf * qf, axis=0, keepdims=True))
    bound = jnp.sqrt(jnp.max(q_sq) * jnp.max(knorm_ref[0])) + jnp.max(jnp.abs(tab_ref[...]))
    fixed_shift = bound < EXP2_SAFE

    @pl.when(fixed_shift)
    def _():
        run(attn_chunk_fixed)

    @pl.when(jnp.logical_not(fixed_shift))
    def _():
        for h in range(N_HEADS):
            m_refs[h][...] = jnp.full((1, qb), MASK_VALUE, F32)
        run(attn_chunk)

    oT = jnp.concatenate([acc_refs[h][...] / l_refs[h][...] for h in range(N_HEADS)], axis=0)
    o_ref[0] = oT.T.astype(o_ref.dtype)


def _attend(qT, qiT, wT, kidxb, kb, vT, knorm, tabs, *, qb, past, l_true, offsets):
    B, _, _, S = qT.shape
    nc, kc = vT.shape[1], vT.shape[3]
    assert S % qb == 0 and qb % LANES == 0 and past % LANES == 0 and kc % LANES == 0
    topk = min(TOPK_MAX, l_true // 4)
    kern = functools.partial(_attn_kernel, qb=qb, kc=kc, past=past, l_true=l_true,
                             topk=topk, offsets=offsets)
    in_specs = [
        pl.BlockSpec((1, N_HEADS, HEAD_DIM, qb), lambda b, i: (b, 0, 0, i)),
        pl.BlockSpec((1, N_IDX_HEADS, IDX_DIM, qb), lambda b, i: (b, 0, 0, i)),
        pl.BlockSpec((1, N_IDX_HEADS, qb), lambda b, i: (b, 0, i)),
        pl.BlockSpec((1,) + kidxb.shape[1:], lambda b, i: (b, 0, 0, 0)),
        pl.BlockSpec((1,) + kb.shape[1:], lambda b, i: (b, 0, 0, 0, 0)),
        pl.BlockSpec((1,) + vT.shape[1:], lambda b, i: (b, 0, 0, 0)),
        pl.BlockSpec((1,) + knorm.shape[1:], lambda b, i: (b, 0, 0, 0)),
        pl.BlockSpec(tabs.shape, lambda b, i: (0, 0, 0, 0)),
    ]
    return pl.pallas_call(
        kern,
        grid=(B, S // qb),
        in_specs=in_specs,
        out_specs=pl.BlockSpec((1, qb, N_HEADS * HEAD_DIM), lambda b, i: (b, i, 0)),
        out_shape=jax.ShapeDtypeStruct((B, S, N_HEADS * HEAD_DIM), BF16),
        scratch_shapes=[
            pltpu.VMEM((nc, kc, qb), I32),
            pltpu.VMEM((nc, kc, qb), I16),
            pltpu.VMEM((nc, kc, qb), I16),
            pltpu.VMEM((SUBLANES, qb), I32),
            pltpu.VMEM((kc, qb), F32),
        ] + [pltpu.VMEM((kc, qb), F32)] * N_HEADS
          + [pltpu.VMEM((kc, qb), BF16)] * N_HEADS
          + [pltpu.VMEM((1, qb), F32)] * N_HEADS
          + [pltpu.VMEM((1, qb), F32)] * N_HEADS
          + [pltpu.VMEM((HEAD_DIM, qb), F32)] * N_HEADS,
        compiler_params=_cparams(("arbitrary", "arbitrary")),
        name="dsa_attend",
    )(qT, qiT, wT, kidxb, kb, vT, knorm, tabs)


def _mix_kernel(x_ref, attn_ref, conv_ref, wo_ref, g_ref, b_ref, h_ref, *, dn_alpha):
    cat = jnp.concatenate([attn_ref[...], conv_ref[...]], axis=1)
    mixed = _dot(cat, wo_ref[...])
    h_ref[...] = _layer_norm(dn_alpha * x_ref[...] + mixed, g_ref[...], b_ref[...])


def _mix(x2, attn2, conv2, wo, g, b, *, dn_alpha):
    T, D = x2.shape
    tt = min(1024, T)
    assert T % tt == 0
    aw, cw = attn2.shape[1], conv2.shape[1]
    row = lambda i: (i, 0)
    const = lambda i: (0, 0)
    return pl.pallas_call(
        functools.partial(_mix_kernel, dn_alpha=dn_alpha),
        grid=(T // tt,),
        in_specs=[pl.BlockSpec((tt, D), row), pl.BlockSpec((tt, aw), row), pl.BlockSpec((tt, cw), row),
                  pl.BlockSpec(wo.shape, const), pl.BlockSpec((1, D), const), pl.BlockSpec((1, D), const)],
        out_specs=pl.BlockSpec((tt, D), row),
        out_shape=jax.ShapeDtypeStruct((T, D), F32),
        compiler_params=_cparams(("arbitrary",)),
        name="outproj_ln1",
    )(x2, attn2, conv2, wo, g, b)


def _route_picks(logits):
    lane = lax.broadcasted_iota(I32, logits.shape, 1)
    work = logits
    vals, hots = [], []
    for _ in range(TOP_K):
        m = jnp.max(work, axis=1, keepdims=True)
        idx = jnp.min(jnp.where(work == m, lane, LANES), axis=1, keepdims=True)
        hot = lane == idx
        vals.append(m)
        hots.append(hot)
        work = jnp.where(hot, -jnp.inf, work)
    es = [jnp.exp(v - vals[0]) for v in vals]
    den = es[0]
    for e in es[1:]:
        den = den + e
    return hots, [e / den for e in es]


def _route(logits):
    hots, gates = _route_picks(logits)
    comb = jnp.zeros(logits.shape, F32)
    picked = jnp.zeros(logits.shape, F32)
    for gate, hot in zip(gates, hots):
        comb = jnp.where(hot, gate, comb)
        picked = jnp.where(hot, 1.0, picked)
    return comb, picked


def _moe_kernel(h_ref, wr_ref, br_ref, *rest, dn_alpha, rows):
    wgu_refs, rest = rest[:WEIGHT_SPLIT], rest[WEIGHT_SPLIT:]
    bgu_ref, rest = rest[0], rest[1:]
    wd_refs, rest = rest[:WEIGHT_SPLIT], rest[WEIGHT_SPLIT:]
    bd_ref, g_ref, b_ref, y_ref, hb_ref, combT_ref, slotT_ref = rest
    e = pl.program_id(1)
    tt = h_ref.shape[0]
    ff = wd_refs[0].shape[1]

    @pl.when(e == 0)
    def _():
        hb = h_ref[...].astype(BF16)
        hb_ref[...] = hb
        comb, picked = _route(_dot(hb, wr_ref[...]) + br_ref[...])
        combT_ref[...] = comb.T
        before = (lax.broadcasted_iota(I32, (tt, tt), 1) < lax.broadcasted_iota(I32, (tt, tt), 0))
        rank = _dot(jnp.where(before, 1.0, 0.0).astype(BF16), picked.astype(BF16))
        slotT_ref[...] = jnp.where(picked > 0.0, rank, -1.0).T
        y_ref[...] = jnp.zeros(y_ref.shape, F32)

    slot_row = slotT_ref[pl.ds(e, 1), :]
    gate_row = combT_ref[pl.ds(e, 1), :]
    count = jnp.sum(jnp.where(slot_row >= 0.0, 1, 0))
    sub_iota = lax.broadcasted_iota(I32, (rows, tt), 0).astype(F32)

    def chunk_body(c, carry):
        base = (c * rows).astype(F32)
        hit = slot_row - base == sub_iota
        gather = jnp.where(hit, 1.0, 0.0).astype(BF16)
        row_gate = jnp.sum(jnp.where(hit, gate_row, 0.0), axis=1, keepdims=True)
        xg = _dot(gather, hb_ref[...]).astype(BF16)
        gu = jnp.concatenate([_dot(xg, w[0]) for w in wgu_refs], axis=1) + bgu_ref[0]
        gate = jnp.minimum(gu[:, 0:ff], SWIGLU_LIMIT)
        up = jnp.clip(gu[:, ff:2 * ff], -SWIGLU_LIMIT, SWIGLU_LIMIT)
        act = ((up + 1.0) * (gate * (1.0 / (1.0 + jnp.exp(-SWIGLU_ALPHA * gate))))).astype(BF16)
        y = (jnp.concatenate([_dot(act, w[0]) for w in wd_refs], axis=1) + bd_ref[0]) * row_gate
        y_ref[...] += lax.dot_general(gather, y.astype(BF16), (((0,), (0,)), ((), ())),
                                      preferred_element_type=F32)
        return carry

    lax.fori_loop(0, (count + rows - 1) // rows, chunk_body, 0)

    @pl.when(e == pl.num_programs(1) - 1)
    def _():
        y_ref[...] = _layer_norm(dn_alpha * h_ref[...] + y_ref[...], g_ref[...], b_ref[...])


def _moe(h2, wr, br, wgu, bgu, wd, bd, g, b, *, dn_alpha):
    T, D = h2.shape
    E, _, F2 = wgu.shape
    tt = min(1024, T)
    assert T % tt == 0
    rows = min(tt, -(-(tt * TOP_K * 5 // (E * 4)) // 16) * 16)
    row = lambda i, e: (i, 0)
    const = lambda i, e: (0, 0)
    exp3 = lambda i, e: (e, 0, 0)
    ns = WEIGHT_SPLIT
    col = lambda j: (lambda i, e: (e, 0, j))
    return pl.pallas_call(
        functools.partial(_moe_kernel, dn_alpha=dn_alpha, rows=rows),
        grid=(T // tt, E),
        in_specs=[pl.BlockSpec((tt, D), row), pl.BlockSpec(wr.shape, const), pl.BlockSpec(br.shape, const)]
                 + [pl.BlockSpec((1, D, F2 // ns), col(j)) for j in range(ns)]
                 + [pl.BlockSpec((1, 1, F2), exp3)]
                 + [pl.BlockSpec((1, F2 // 2, D // ns), col(j)) for j in range(ns)]
                 + [pl.BlockSpec((1, 1, D), exp3),
                    pl.BlockSpec((1, D), const), pl.BlockSpec((1, D), const)],
        out_specs=pl.BlockSpec((tt, D), row),
        out_shape=jax.ShapeDtypeStruct((T, D), F32),
        scratch_shapes=[pltpu.VMEM((tt, D), BF16), pltpu.VMEM((LANES, tt), F32), pltpu.VMEM((LANES, tt), F32)],
        compiler_params=_cparams(("arbitrary", "arbitrary")),
        name="moe_ln2",
    )(h2, wr, br, *([wgu] * ns), bgu, *([wd] * ns), bd, g, b)


def _pack_pairs(x):
    w = x.shape[1] // 2
    bits = pltpu.bitcast(x.astype(BF16).astype(F32), I32)
    return (bits[:, 0:w] & jnp.int32(-65536)) | lax.shift_right_logical(bits[:, w:2 * w], 16)


def _unpack_pairs(p):
    hi = pltpu.bitcast(p & jnp.int32(-65536), F32)
    lo = pltpu.bitcast(lax.shift_left(p, 16), F32)
    return jnp.concatenate([hi, lo], axis=1)


def _router_kernel(h_ref, wr_ref, br_ref, hpk_ref, info_ref, cnt_ref):
    tt = h_ref.shape[0]
    h = h_ref[...]
    hpk_ref[...] = _pack_pairs(h)
    hots, gates = _route_picks(_dot(h.astype(BF16), wr_ref[...]) + br_ref[...])
    picked = jnp.zeros((tt, LANES), F32)
    for hot in hots:
        picked = jnp.where(hot, 1.0, picked)
    before = (lax.broadcasted_iota(I32, (tt, tt), 1) < lax.broadcasted_iota(I32, (tt, tt), 0))
    rank = _dot(jnp.where(before, 1.0, 0.0).astype(BF16), picked.astype(BF16))
    lane = lax.broadcasted_iota(I32, (tt, LANES), 1)
    info = jnp.zeros((tt, LANES), F32)
    for k, (hot, gate) in enumerate(zip(hots, gates)):
        eid = jnp.sum(jnp.where(hot, lane, 0), axis=1, keepdims=True).astype(F32)
        rk = jnp.sum(jnp.where(hot, rank, 0.0), axis=1, keepdims=True)
        info = jnp.where(lane == k, eid, info)
        info = jnp.where(lane == TOP_K + k, gate, info)
        info = jnp.where(lane == 2 * TOP_K + k, rk, info)
    info_ref[...] = info
    cnt_ref[0] = jnp.broadcast_to(jnp.sum(picked, axis=0, keepdims=True), (SUBLANES, LANES))


def _router(h2, wr, br):
    T, D = h2.shape
    tt = 1024
    row = lambda i: (i, 0)
    const = lambda i: (0, 0)
    return pl.pallas_call(
        _router_kernel,
        grid=(T // tt,),
        in_specs=[pl.BlockSpec((tt, D), row), pl.BlockSpec(wr.shape, const), pl.BlockSpec(br.shape, const)],
        out_specs=(pl.BlockSpec((tt, D // 2), row), pl.BlockSpec((tt, LANES), row),
                   pl.BlockSpec((1, SUBLANES, LANES), lambda i: (i, 0, 0))),
        out_shape=(jax.ShapeDtypeStruct((T, D // 2), I32), jax.ShapeDtypeStruct((T, LANES), F32),
                   jax.ShapeDtypeStruct((T // tt, SUBLANES, LANES), F32)),
        compiler_params=_cparams(("arbitrary",)),
        name="moe_router",
    )(h2, wr, br)


def _sc_gather_rows(table, idx):
    M, W = idx.shape[0], table.shape[1]
    n_workers = SC_CORES * SC_SUBCORES
    per_w = M // n_workers
    n_ch = per_w // SC_ROWS
    assert M == n_workers * n_ch * SC_ROWS
    mesh = plsc.VectorSubcoreMesh(core_axis_name="c", subcore_axis_name="s")

    @functools.partial(
        pl.kernel, mesh=mesh,
        out_type=jax.ShapeDtypeStruct((M, W), table.dtype),
        scratch_types=[pltpu.VMEM((n_ch, SC_ROWS), I32), pltpu.VMEM((SC_ROWS, W), table.dtype),
                       pltpu.SemaphoreType.DMA],
    )
    def gather(table_hbm, idx_hbm, out_hbm, idx_v, rows_v, sem):
        wid = lax.axis_index("s") * SC_CORES + lax.axis_index("c")
        pltpu.sync_copy(idx_hbm.at[wid], idx_v)

        @pl.loop(0, n_ch)
        def _(j):
            pltpu.async_copy(table_hbm.at[idx_v.at[j]], rows_v, sem).wait()
            pltpu.sync_copy(rows_v, out_hbm.at[pl.ds(wid * per_w + j * SC_ROWS, SC_ROWS)])

    return gather(table, idx.reshape(n_workers, n_ch, SC_ROWS))


def _expert_rows_kernel(te_ref, x_ref, *rest):
    wgu_refs, rest = rest[:WEIGHT_SPLIT], rest[WEIGHT_SPLIT:]
    bgu_ref, rest = rest[0], rest[1:]
    wd_refs, rest = rest[:WEIGHT_SPLIT], rest[WEIGHT_SPLIT:]
    bd_ref, y_ref = rest
    ff = wd_refs[0].shape[1]
    xg = _unpack_pairs(x_ref[...]).astype(BF16)
    gu = jnp.concatenate([_dot(xg, w[0]) for w in wgu_refs], axis=1) + bgu_ref[0]
    gate = jnp.minimum(gu[:, 0:ff], SWIGLU_LIMIT)
    up = jnp.clip(gu[:, ff:2 * ff], -SWIGLU_LIMIT, SWIGLU_LIMIT)
    act = ((up + 1.0) * (gate * (1.0 / (1.0 + jnp.exp(-SWIGLU_ALPHA * gate))))).astype(BF16)
    y_ref[...] = _pack_pairs(jnp.concatenate([_dot(act, w[0]) for w in wd_refs], axis=1) + bd_ref[0])


def _expert_rows(tile_expert, xs, wgu, bgu, wd, bd):
    R, W = xs.shape
    E, D, F2 = wgu.shape
    ns = WEIGHT_SPLIT
    col = lambda c: (lambda j, te: (te[j], 0, c))
    exp3 = lambda j, te: (te[j], 0, 0)
    grid_spec = pltpu.PrefetchScalarGridSpec(
        num_scalar_prefetch=1,
        grid=(R // ROW_TILE,),
        in_specs=[pl.BlockSpec((ROW_TILE, W), lambda j, te: (j, 0))]
                 + [pl.BlockSpec((1, D, F2 // ns), col(c)) for c in range(ns)]
                 + [pl.BlockSpec((1, 1, F2), exp3)]
                 + [pl.BlockSpec((1, F2 // 2, D // ns), col(c)) for c in range(ns)]
                 + [pl.BlockSpec((1, 1, D), exp3)],
        out_specs=pl.BlockSpec((ROW_TILE, W), lambda j, te: (j, 0)),
    )
    return pl.pallas_call(
        _expert_rows_kernel,
        grid_spec=grid_spec,
        out_shape=jax.ShapeDtypeStruct((R, W), I32),
        compiler_params=_cparams(("arbitrary",)),
        name="moe_expert_rows",
    )(tile_expert, xs, *([wgu] * ns), bgu, *([wd] * ns), bd)


def _combine_kernel(h_ref, yg_ref, info_ref, g_ref, b_ref, o_ref, *, dn_alpha):
    acc = dn_alpha * h_ref[...]
    for k in range(TOP_K):
        acc = acc + info_ref[:, TOP_K + k:TOP_K + k + 1] * _unpack_pairs(yg_ref[k])
    o_ref[...] = _layer_norm(acc, g_ref[...], b_ref[...])


def _combine(h2, yg, info, g, b, *, dn_alpha):
    T, D = h2.shape
    tt = 512
    row = lambda i: (i, 0)
    const = lambda i: (0, 0)
    return pl.pallas_call(
        functools.partial(_combine_kernel, dn_alpha=dn_alpha),
        grid=(T // tt,),
        in_specs=[pl.BlockSpec((tt, D), row), pl.BlockSpec((TOP_K, tt, D // 2), lambda i: (0, i, 0)),
                  pl.BlockSpec((tt, LANES), row),
                  pl.BlockSpec((1, D), const), pl.BlockSpec((1, D), const)],
        out_specs=pl.BlockSpec((tt, D), row),
        out_shape=jax.ShapeDtypeStruct((T, D), F32),
        compiler_params=_cparams(("arbitrary",)),
        name="moe_combine_ln2",
    )(h2, yg, info, g, b)


def _dest_kernel(info_ref, base_ref, dest_ref):
    info = info_ref[...]
    tt = info.shape[0]
    lane = lax.broadcasted_iota(I32, (tt, LANES), 1)
    base = base_ref[0, 0:1, :]
    out = jnp.zeros((tt, LANES), F32)
    for k in range(TOP_K):
        hot = lane == info[:, k:k + 1].astype(I32)
        row0 = jnp.sum(jnp.where(hot, base, 0.0), axis=1, keepdims=True)
        out = jnp.where(lane == k, row0 + info[:, 2 * TOP_K + k:2 * TOP_K + k + 1], out)
    dest_ref[...] = out.astype(I32)


def _dest_rows(info, base):
    T = info.shape[0]
    tt = 1024
    return pl.pallas_call(
        _dest_kernel,
        grid=(T // tt,),
        in_specs=[pl.BlockSpec((tt, LANES), lambda i: (i, 0)), pl.BlockSpec((1, SUBLANES, LANES), lambda i: (i, 0, 0))],
        out_specs=pl.BlockSpec((tt, LANES), lambda i: (i, 0)),
        out_shape=jax.ShapeDtypeStruct((T, LANES), I32),
        compiler_params=_cparams(("arbitrary",)),
        name="moe_dest_rows",
    )(info, base)


def _sc_scatter_rows(src, dest_km, n_rows):
    T, W = src.shape
    K = dest_km.shape[0]
    n_workers = SC_CORES * SC_SUBCORES
    per_w = T // n_workers
    n_ch = per_w // SC_ROWS
    assert T == n_workers * n_ch * SC_ROWS
    mesh = plsc.VectorSubcoreMesh(core_axis_name="c", subcore_axis_name="s")
    idx = dest_km.reshape(K, n_workers, n_ch, SC_ROWS).transpose(1, 0, 2, 3).reshape(n_workers, K * n_ch, SC_ROWS)

    @functools.partial(
        pl.kernel, mesh=mesh,
        out_type=jax.ShapeDtypeStruct((n_rows, W), src.dtype),
        scratch_types=[pltpu.VMEM((K * n_ch, SC_ROWS), I32), pltpu.VMEM((SC_ROWS, W), src.dtype),
                       pltpu.SemaphoreType.DMA],
    )
    def scatter(src_hbm, idx_hbm, out_hbm, idx_v, rows_v, sem):
        wid = lax.axis_index("s") * SC_CORES + lax.axis_index("c")
        pltpu.sync_copy(idx_hbm.at[wid], idx_v)

        @pl.loop(0, n_ch)
        def _(j):
            pltpu.sync_copy(src_hbm.at[pl.ds(wid * per_w + j * SC_ROWS, SC_ROWS)], rows_v)
            for k in range(K):
                pltpu.async_copy(rows_v, out_hbm.at[idx_v.at[k * n_ch + j]], sem).wait()

    return scatter(src, idx)


def _moe_sorted(h2, wr, br, wgu, bgu, wd, bd, g, b, *, dn_alpha):
    T, D = h2.shape
    E = wgu.shape[0]
    hpk, info, cnt = _router(h2, wr, br)
    cnt = cnt[:, 0, :].astype(I32)
    total = jnp.sum(cnt, axis=0)
    padded = -(-total // ROW_TILE) * ROW_TILE
    ends = jnp.cumsum(padded)
    base = (ends - padded)[None, :] + jnp.cumsum(cnt, axis=0) - cnt
    base = jnp.broadcast_to(base.astype(F32)[:, None, :], (cnt.shape[0], SUBLANES, LANES))
    dest_km = _dest_rows(info, base)[:, 0:TOP_K].T
    R = T * TOP_K + E * ROW_TILE
    tile_start = jnp.arange(R // ROW_TILE, dtype=I32) * ROW_TILE
    tile_expert = jnp.minimum(jnp.sum((ends[None, 0:E] <= tile_start[:, None]).astype(I32), axis=1), E - 1)
    xs = _sc_scatter_rows(hpk, dest_km, R)
    ys = _expert_rows(tile_expert, xs, wgu, bgu, wd, bd)
    yg = _sc_gather_rows(ys, dest_km.reshape(-1))
    return _combine(h2, yg.reshape(TOP_K, T, D // 2), info, g, b, dn_alpha=dn_alpha)


def _split_gu_kernel(w_ref, perm_ref, o_ref, *, band):
    n = w_ref.shape[2]
    for m in range(n // band):
        both = _dot(w_ref[0, :, m * band:(m + 1) * band].astype(BF16), perm_ref[...]).astype(BF16)
        o_ref[0, :, m * (band // 2):(m + 1) * (band // 2)] = both[:, 0:band // 2]
        o_ref[0, :, n // 2 + m * (band // 2):n // 2 + (m + 1) * (band // 2)] = both[:, band // 2:band]


def _split_gu(w_gu):
    E, D, N = w_gu.shape
    tr = min(512, D)
    band = min(512, N)
    j = jnp.arange(band, dtype=I32)
    src = jnp.where(j < band // 2, 2 * j, 2 * (j - band // 2) + 1)
    perm = (jnp.arange(band, dtype=I32)[:, None] == src[None, :]).astype(BF16)
    return pl.pallas_call(
        functools.partial(_split_gu_kernel, band=band),
        grid=(E, D // tr),
        in_specs=[pl.BlockSpec((1, tr, N), lambda e, r: (e, r, 0)),
                  pl.BlockSpec((band, band), lambda e, r: (0, 0))],
        out_specs=pl.BlockSpec((1, tr, N), lambda e, r: (e, r, 0)),
        out_shape=jax.ShapeDtypeStruct((E, D, N), BF16),
        compiler_params=_cparams(("arbitrary", "arbitrary")),
        name="split_gate_up",
    )(w_gu, perm)


def _prep_weights(w_in, w_conv, w_out, ln1_g, ln1_b, w_router, b_router, w_gu, b_gu, w_down, b_down,
                  ln2_g, ln2_b):
    aw = N_HEADS * HEAD_DIM
    kv = N_KV_HEADS * HEAD_DIM
    iw = N_IDX_HEADS * IDX_DIM
    D = w_in.shape[0]
    cw = D - aw
    o_q, o_k, o_v = 0, aw, aw + kv
    o_qi = o_v + kv
    o_ki = o_qi + iw
    o_wi = o_ki + IDX_DIM
    o_c = o_wi + N_IDX_HEADS
    wa = jnp.concatenate([w_in[:, o_q:o_k], w_in[:, o_qi:o_ki], w_in[:, o_k:o_v], w_in[:, o_v:o_qi]],
                         axis=1).astype(BF16)
    ws = jnp.pad(w_in[:, o_ki:o_c], ((0, 0), (0, LANES - IDX_DIM - N_IDX_HEADS))).astype(BF16)
    wc = w_in[:, o_c:o_c + 3 * cw].astype(BF16)
    E = w_router.shape[1]
    wr = jnp.pad(w_router, ((0, 0), (0, LANES - E))).astype(BF16)
    br = jnp.pad(b_router, (0, LANES - E), constant_values=MASK_VALUE).reshape(1, LANES)
    F = w_gu.shape[2] // 2
    return dict(
        wa=wa, ws=ws, wc=wc, wconv=w_conv, wo=w_out.astype(BF16),
        ln1_g=ln1_g.reshape(1, D), ln1_b=ln1_b.reshape(1, D),
        wr=wr, br=br,
        wgu=_split_gu(w_gu),
        bgu=jnp.concatenate([b_gu[:, 0::2], b_gu[:, 1::2]], axis=1).reshape(E, 1, 2 * F),
        wd=w_down.astype(BF16), bd=b_down.reshape(E, 1, D),
        ln2_g=ln2_g.reshape(1, D), ln2_b=ln2_b.reshape(1, D),
    )


def _decode_layouts(pq, small, k_all, v_all, kidx_all, kc, qb):
    B, S, _ = pq.shape
    aw = N_HEADS * HEAD_DIM
    L, kv = k_all.shape[1], k_all.shape[2]
    lp = -(-L // kc) * kc
    nc = lp // kc
    padk = lambda a: jnp.pad(a, ((0, 0), (0, lp - L), (0, 0)))
    padq = lambda a: jnp.pad(a, ((0, 0), (0, 0), (0, 0), (0, qb - S)))
    q = pq[:, :, 0:aw] * QK_SCALE
    qT = padq(q.reshape(B, S, N_HEADS, HEAD_DIM).transpose(0, 2, 3, 1)).astype(BF16)
    qiT = padq(pq[:, :, aw:].reshape(B, S, N_IDX_HEADS, IDX_DIM).transpose(0, 2, 3, 1)).astype(BF16)
    wT = jnp.pad(small[:, :, IDX_DIM:IDX_DIM + N_IDX_HEADS].transpose(0, 2, 1), ((0, 0), (0, 0), (0, qb - S)))
    kb = padk(k_all).reshape(B, nc, kc, N_KV_HEADS, HEAD_DIM).transpose(0, 1, 3, 2, 4).astype(BF16)
    vT = padk(v_all).reshape(B, nc, kc, kv).transpose(0, 1, 3, 2).astype(BF16)
    kidxb = padk(kidx_all).reshape(B, nc, kc, IDX_DIM).astype(BF16)
    knorm = jnp.sum(kb.astype(F32) ** 2, axis=-1)
    return qT, qiT, wT, kidxb, kb, vT, knorm


def _layer(x, past_k, past_v, past_kidx, conv_buf, rel_bias, w, *, dn_alpha):
    B, S, D = x.shape
    prefill = past_k is None
    outs = _project(x, conv_buf, w["wa"], w["ws"], w["wc"], w["wconv"], attn_layouts=prefill)
    k, v, small, conv, new_buf = outs[:5]
    k_idx = small[:, :, 0:IDX_DIM]
    if prefill:
        past, l_true = 0, S
        qb = min(QUERY_BLOCK, S)
        kb, vT, kidxb, qT, qiT, wT, knorm = outs[5:]
    else:
        past = past_k.shape[1]
        l_true = past + S
        qb = -(-S // LANES) * LANES
        qT, qiT, wT, kidxb, kb, vT, knorm = _decode_layouts(
            outs[5], small, jnp.concatenate([past_k, k], axis=1), jnp.concatenate([past_v, v], axis=1),
            jnp.concatenate([past_kidx, k_idx], axis=1), KEY_CHUNK, qb)
    offsets = (-LANES, 0)
    tabs = _bias_tables(rel_bias, offsets)
    attn = _attend(qT, qiT, wT, kidxb, kb, vT, knorm, tabs, qb=qb, past=past, l_true=l_true, offsets=offsets)
    attn = attn[:, :S]
    h = _mix(x.reshape(B * S, D), attn.reshape(B * S, -1), conv.reshape(B * S, -1), w["wo"],
             w["ln1_g"], w["ln1_b"], dn_alpha=dn_alpha)
    sc_unit = SC_CORES * SC_SUBCORES * SC_ROWS
    sortable = (B * S) % 1024 == 0 and (B * S * TOP_K) % sc_unit == 0 and (N_EXPERTS * ROW_TILE) % sc_unit == 0
    moe = _moe_sorted if sortable and B * S >= SORTED_MIN_TOKENS else _moe
    y = moe(h, w["wr"], w["br"], w["wgu"], w["bgu"], w["wd"], w["bd"],
            w["ln2_g"], w["ln2_b"], dn_alpha=dn_alpha)
    return (y.reshape(B, S, D), k.reshape(B, S, N_KV_HEADS, HEAD_DIM), v.reshape(B, S, N_KV_HEADS, HEAD_DIM),
            k_idx, new_buf)


def kernel(x_prompt, x_sample, cache_k, cache_v, cache_kidx, state_conv, rel_bias, w_in, w_conv, w_out,
           ln1_g, ln1_b, w_router, b_router, w_gu, b_gu, w_down, b_down, ln2_g, ln2_b):
    depth = w_in.shape[0]
    assert depth == 1
    dn_alpha = (2 * depth) ** 0.25
    kv = N_KV_HEADS * HEAD_DIM
    w = _prep_weights(w_in[0], w_conv[0], w_out[0], ln1_g[0], ln1_b[0], w_router[0], b_router[0],
                      w_gu[0], b_gu[0], w_down[0], b_down[0], ln2_g[0], ln2_b[0])
    Bp = x_prompt.shape[0]
    cw = w_conv.shape[2]
    zero_buf = jnp.zeros((Bp, CONV_K - 1, cw), F32)
    yp, k1, v1, i1, c1 = _layer(x_prompt, None, None, None, zero_buf, rel_bias, w, dn_alpha=dn_alpha)
    Bs, P = cache_k.shape[1], cache_k.shape[2]
    ys, k2, v2, i2, c2 = _layer(x_sample, cache_k[0].reshape(Bs, P, kv), cache_v[0].reshape(Bs, P, kv),
                                cache_kidx[0], state_conv[0], rel_bias, w, dn_alpha=dn_alpha)
    return (yp, ys, k1[None], v1[None], i1[None], c1[None], k2[None], v2[None], i2[None], c2[None])
```

```python
import functools
import math

import jax
import jax.numpy as jnp
from jax import lax
from jax.experimental import pallas as pl
from jax.experimental.pallas import tpu as pltpu
from jax.experimental.pallas import tpu_sc as plsc

F32 = jnp.float32
BF16 = jnp.bfloat16
I32 = jnp.int32
I16 = jnp.int16

CHUNK = 64
N_HEADS = 8
HEAD_DIM = 64
N_KV_HEADS = 2
GROUP = N_HEADS // N_KV_HEADS
N_IDX_HEADS = 8
IDX_DIM = 32
TOPK_MAX = 256
CONV_K = 3
N_BUCKETS = 32
MAX_DISTANCE = 128
N_EXPERTS = 32
TOP_K = 4
SWIGLU_LIMIT = 7.0
SWIGLU_ALPHA = 1.702
LN_EPS = 1e-5
MASK_VALUE = -1e30
QK_SCALE = HEAD_DIM ** -0.5 * math.log2(math.e)
EXP2_SAFE = 96.0

LANES = 128
SUBLANES = 8
KEY_CHUNK = 512
QUERY_BLOCK = 256
SLAB = 64
WEIGHT_SPLIT = 4
ROW_TILE = 512
SC_CORES = 2
SC_SUBCORES = 16
SC_ROWS = 64
SORTED_MIN_TOKENS = 2048
INT_MIN = -(2 ** 31)
VMEM_LIMIT = 56 * 1024 * 1024


def _cparams(sem):
    return pltpu.CompilerParams(dimension_semantics=sem, vmem_limit_bytes=VMEM_LIMIT)


def _dot(a, b):
    return jnp.dot(a, b, preferred_element_type=F32)


def _layer_norm(z, g, b):
    mu = jnp.mean(z, axis=-1, keepdims=True)
    zc = z - mu
    var = jnp.mean(zc * zc, axis=-1, keepdims=True)
    return zc * lax.rsqrt(var + LN_EPS) * g + b


def _bucket_thresholds():
    nb = N_BUCKETS // 2
    max_exact = nb // 2
    out = []
    for j in range(1, nb - max_exact):
        out.append(math.ceil(max_exact * (MAX_DISTANCE / max_exact) ** (j / (nb - max_exact)) - 1e-9))
    return tuple(out)


def _bias_table_kernel(rel_ref, tab_ref, *, offsets):
    nb = N_BUCKETS // 2
    max_exact = nb // 2
    thr = _bucket_thresholds()
    ii = lax.broadcasted_iota(I32, (LANES, LANES), 0)
    jj = lax.broadcasted_iota(I32, (LANES, LANES), 1)
    for d, off in enumerate(offsets):
        rel = off + ii - jj
        n = jnp.abs(rel)
        large = jnp.full((LANES, LANES), max_exact, I32)
        for t in thr:
            large = large + jnp.where(n >= t, 1, 0)
        bucket = jnp.where(rel > 0, nb, 0) + jnp.where(n < max_exact, n, large)
        for h in range(N_HEADS):
            acc = jnp.zeros((LANES, LANES), F32)
            for b in range(N_BUCKETS):
                acc = jnp.where(bucket == b, rel_ref[b, h], acc)
            tab_ref[d, h] = (acc - rel_ref[nb - 1, h]) * math.log2(math.e)


def _bias_tables(rel_bias, offsets):
    return pl.pallas_call(
        functools.partial(_bias_table_kernel, offsets=offsets),
        out_shape=jax.ShapeDtypeStruct((len(offsets), N_HEADS, LANES, LANES), F32),
        in_specs=[pl.BlockSpec(memory_space=pltpu.SMEM)],
        out_specs=pl.BlockSpec(memory_space=pltpu.VMEM),
        name="bias_tables",
    )(rel_bias)


def _proj_kernel(x_ref, wa_ref, ws_ref, wc_ref, wconv_ref, buf_ref,
                 k_ref, v_ref, small_ref, conv_ref, nbuf_ref, *rest, ts, attn_layouts):
    s = pl.program_id(1)
    xb = x_ref[0].astype(BF16)
    pa = _dot(xb, wa_ref[...])
    ps = _dot(xb, ws_ref[...])
    pc = _dot(xb, wc_ref[...])
    aw = N_HEADS * HEAD_DIM
    iw = N_IDX_HEADS * IDX_DIM
    kv = N_KV_HEADS * HEAD_DIM
    k = pa[:, aw + iw:aw + iw + kv]
    v = pa[:, aw + iw + kv:aw + iw + 2 * kv]
    k_ref[0] = k
    v_ref[0] = v
    small_ref[0] = ps
    scale = QK_SCALE
    if attn_layouts:
        kb_ref, vT_ref, kidxb_ref, qT_ref, qiT_ref, wT_ref, knorm_ref, carry_ref = rest
        for n in range(N_KV_HEADS):
            kb_ref[0, 0, n] = k[:, n * HEAD_DIM:(n + 1) * HEAD_DIM].astype(BF16)
        kTf = k.astype(BF16).astype(F32).T
        knorm_ref[0, 0] = jnp.concatenate(
            [jnp.sum(kTf[n * HEAD_DIM:(n + 1) * HEAD_DIM] ** 2, axis=0, keepdims=True) for n in range(N_KV_HEADS)],
            axis=0)
        vT_ref[0, 0] = v.T.astype(BF16)
        kidxb_ref[0, 0] = ps[:, 0:IDX_DIM].astype(BF16)
        qT = (pa[:, 0:aw] * scale).T.astype(BF16)
        for h in range(N_HEADS):
            qT_ref[0, h] = qT[h * HEAD_DIM:(h + 1) * HEAD_DIM]
        qiT = pa[:, aw:aw + iw].T.astype(BF16)
        for h in range(N_IDX_HEADS):
            qiT_ref[0, h] = qiT[h * IDX_DIM:(h + 1) * IDX_DIM]
        wT_ref[0] = ps.T[IDX_DIM:IDX_DIM + N_IDX_HEADS]
    else:
        pq_ref, carry_ref = rest
        pq_ref[0] = pa[:, 0:aw + iw]

    cw = pc.shape[1] // 3
    u = pc[:, cw:2 * cw] * pc[:, 2 * cw:3 * cw]

    @pl.when(s == 0)
    def _():
        carry_ref[6:8, :] = buf_ref[0]

    carry_ref[8:8 + ts, :] = u
    y = (carry_ref[6:6 + ts, :] * wconv_ref[0:1, :]
         + carry_ref[7:7 + ts, :] * wconv_ref[1:2, :]
         + u * wconv_ref[2:3, :])
    conv_ref[0] = (pc[:, 0:cw] * y).astype(BF16)
    nb = carry_ref[ts + 6:ts + 8, :]
    nbuf_ref[0] = nb
    carry_ref[6:8, :] = nb


def _project(x, conv_buf, wa, ws, wc, wconv, *, attn_layouts):
    B, S, D = x.shape
    ts = min(KEY_CHUNK, S)
    assert S % ts == 0 and S >= CONV_K - 1
    ns = S // ts
    cw = wc.shape[1] // 3
    kv = N_KV_HEADS * HEAD_DIM
    aw = N_HEADS * HEAD_DIM
    iw = N_IDX_HEADS * IDX_DIM
    out_shape = [
        jax.ShapeDtypeStruct((B, S, kv), F32),
        jax.ShapeDtypeStruct((B, S, kv), F32),
        jax.ShapeDtypeStruct((B, S, LANES), F32),
        jax.ShapeDtypeStruct((B, S, cw), BF16),
        jax.ShapeDtypeStruct((B, CONV_K - 1, cw), F32),
    ]
    out_specs = [
        pl.BlockSpec((1, ts, kv), lambda b, s: (b, s, 0)),
        pl.BlockSpec((1, ts, kv), lambda b, s: (b, s, 0)),
        pl.BlockSpec((1, ts, LANES), lambda b, s: (b, s, 0)),
        pl.BlockSpec((1, ts, cw), lambda b, s: (b, s, 0)),
        pl.BlockSpec((1, CONV_K - 1, cw), lambda b, s: (b, 0, 0)),
    ]
    if attn_layouts:
        out_shape += [
            jax.ShapeDtypeStruct((B, ns, N_KV_HEADS, ts, HEAD_DIM), BF16),
            jax.ShapeDtypeStruct((B, ns, kv, ts), BF16),
            jax.ShapeDtypeStruct((B, ns, ts, IDX_DIM), BF16),
            jax.ShapeDtypeStruct((B, N_HEADS, HEAD_DIM, S), BF16),
            jax.ShapeDtypeStruct((B, N_IDX_HEADS, IDX_DIM, S), BF16),
            jax.ShapeDtypeStruct((B, N_IDX_HEADS, S), F32),
            jax.ShapeDtypeStruct((B, ns, N_KV_HEADS, ts), F32),
        ]
        out_specs += [
            pl.BlockSpec((1, 1, N_KV_HEADS, ts, HEAD_DIM), lambda b, s: (b, s, 0, 0, 0)),
            pl.BlockSpec((1, 1, kv, ts), lambda b, s: (b, s, 0, 0)),
            pl.BlockSpec((1, 1, ts, IDX_DIM), lambda b, s: (b, s, 0, 0)),
            pl.BlockSpec((1, N_HEADS, HEAD_DIM, ts), lambda b, s: (b, 0, 0, s)),
            pl.BlockSpec((1, N_IDX_HEADS, IDX_DIM, ts), lambda b, s: (b, 0, 0, s)),
            pl.BlockSpec((1, N_IDX_HEADS, ts), lambda b, s: (b, 0, s)),
            pl.BlockSpec((1, 1, N_KV_HEADS, ts), lambda b, s: (b, s, 0, 0)),
        ]
    else:
        out_shape += [jax.ShapeDtypeStruct((B, S, aw + iw), F32)]
        out_specs += [pl.BlockSpec((1, ts, aw + iw), lambda b, s: (b, s, 0))]
    const2 = lambda b, s: (0, 0)
    in_specs = [
        pl.BlockSpec((1, ts, D), lambda b, s: (b, s, 0)),
        pl.BlockSpec(wa.shape, const2),
        pl.BlockSpec(ws.shape, const2),
        pl.BlockSpec(wc.shape, const2),
        pl.BlockSpec(wconv.shape, const2),
        pl.BlockSpec((1, CONV_K - 1, cw), lambda b, s: (b, 0, 0)),
    ]
    return pl.pallas_call(
        functools.partial(_proj_kernel, ts=ts, attn_layouts=attn_layouts),
        grid=(B, ns),
        in_specs=in_specs,
        out_specs=tuple(out_specs),
        out_shape=tuple(out_shape),
        scratch_shapes=[pltpu.VMEM((ts + 8, cw), F32)],
        compiler_params=_cparams(("arbitrary", "arbitrary")),
        name="proj_conv",
    )(x, wa, ws, wc, wconv, conv_buf)


def _attn_kernel(qT_ref, qiT_ref, wT_ref, kidx_ref, kb_ref, vT_ref, knorm_ref, tab_ref,
                 o_ref, skey_ref, hi16_ref, lo16_ref, jb_ref, madd_ref, *head_refs,
                 qb, kc, past, l_true, topk, offsets):
    s_refs, p_refs, m_refs, l_refs, acc_refs = (head_refs[g * N_HEADS:(g + 1) * N_HEADS] for g in range(5))
    i = pl.program_id(1)
    qoff = past + i * qb
    adm_end = jnp.minimum(((qoff + qb - 1) // CHUNK + 1) * CHUNK, l_true)
    nck = (adm_end + kc - 1) // kc
    n_far = jnp.maximum(qoff - LANES, 0) // kc
    idx_bits = int(l_true).bit_length()
    groups = kc // SUBLANES

    rowi = lax.broadcasted_iota(I32, (kc, qb), 0)
    qpos = qoff + lax.broadcasted_iota(I32, (1, qb), 1)
    kmax = jnp.minimum((qpos // CHUNK + 1) * CHUNK, l_true)

    def score_body(c, carry):
        kidx_c = kidx_ref[0, c]
        sc = jnp.zeros((kc, qb), F32)
        for h in range(N_IDX_HEADS):
            d = _dot(kidx_c, qiT_ref[0, h])
            sc = sc + wT_ref[0, h:h + 1, :] * jnp.maximum(d, 0.0)
        bits = pltpu.bitcast(sc, I32)
        skey = jnp.where(bits < 0, bits ^ jnp.int32(0x7FFFFFFF), bits)
        skey = jnp.where(skey == -1, 0, skey)
        skey = jnp.where(rowi < kmax - c * kc, skey, jnp.int32(INT_MIN))
        skey_ref[c] = skey
        hi16_ref[c] = (skey >> 16).astype(I16)
        return carry

    lax.fori_loop(0, nck, score_body, 0)

    def count(pred_fn):
        def body(c, part):
            ind = jnp.where(pred_fn(c, skey_ref[c]), 1, 0)
            return part + jnp.sum(ind.reshape(groups, SUBLANES, qb), axis=0)
        part = lax.fori_loop(0, nck, body, jnp.zeros((SUBLANES, qb), I32))
        return jnp.sum(part, axis=0, keepdims=True)

    def count16(ref, pred_fn):
        pack = 2 * SUBLANES

        def body(c, part):
            ind = jnp.where(pred_fn(ref[c]), jnp.bfloat16(1), jnp.bfloat16(0)).reshape(kc // pack, pack, qb)
            terms = [ind[g] for g in range(kc // pack)]
            while len(terms) > 1:
                terms = [terms[g] + terms[g + 1] for g in range(0, len(terms), 2)]
            return part + terms[0].astype(F32)
        part = lax.fori_loop(0, nck, body, jnp.zeros((pack, qb), F32))
        return jnp.sum(part, axis=0, keepdims=True).astype(I32)

    def search16(ref, target):
        def body(it, t_u):
            cand_u = t_u | (jnp.int32(1) << (15 - it))
            cand = (cand_u - 2 ** 15).astype(I16)
            cnt = count16(ref, lambda k: k >= cand)
            return jnp.where(cnt >= target, cand_u, t_u)
        return lax.fori_loop(0, 16, body, jnp.zeros((1, qb), I32))

    hi_u = search16(hi16_ref, topk)
    hi_s = (hi_u - 2 ** 15).astype(I16)
    rest = topk - count16(hi16_ref, lambda k: k > hi_s)

    def low_body(c, carry):
        low = ((skey_ref[c] & 0xFFFF) - 2 ** 15).astype(I16)
        lo16_ref[c] = jnp.where(hi16_ref[c] == hi_s, low, jnp.int16(-2 ** 15))
        return carry

    lax.fori_loop(0, nck, low_body, 0)
    lo_u = search16(lo16_ref, rest)
    thr = (hi_u - 2 ** 15) * 2 ** 16 + lo_u
    cnt_ge = count(lambda c, sk: sk >= thr)
    cnt_gt = count(lambda c, sk: sk > thr)

    jb_ref[...] = jnp.full(jb_ref.shape, 2 ** 31 - 1, I32)

    @pl.when(jnp.max(cnt_ge) > topk)
    def _():
        need = topk - cnt_gt

        def tie_body(it, jb):
            cand = jb | (jnp.int32(1) << (idx_bits - 1 - it))
            cnt = count(lambda c, sk: jnp.logical_and(sk == thr, c * kc + rowi < cand))
            return jnp.where(cnt <= need, cand, jb)

        jb = lax.fori_loop(0, idx_bits, tie_body, jnp.zeros((1, qb), I32))
        jb_ref[...] = jnp.broadcast_to(jb, jb_ref.shape)

    jbound = jnp.where(thr == jnp.int32(INT_MIN), 0, jb_ref[0:1, :])
    thr_m1 = thr - 1

    for h in range(N_HEADS):
        l_refs[h][...] = jnp.zeros((1, qb), F32)
        acc_refs[h][...] = jnp.zeros((HEAD_DIM, qb), F32)
    n_slabs = kc // SLAB

    def fold(x, op):
        return op(x.reshape(SLAB // SUBLANES, SUBLANES, qb), axis=0)

    def near_bias(c, h):
        tiles = []
        for t in range(kc // LANES):
            row = []
            for u in range(qb // LANES):
                d_tu = c * kc + t * LANES - (qoff + u * LANES)
                delta = jnp.zeros((LANES, LANES), F32)
                for di, off in enumerate(offsets):
                    delta = jnp.where(d_tu == off, tab_ref[di, h], delta)
                row.append(delta)
            tiles.append(jnp.concatenate(row, axis=1))
        return jnp.concatenate(tiles, axis=0)

    def attn_chunk(c, near):
        sk = skey_ref[c]
        t_eff = jnp.where(rowi < jbound - c * kc, thr_m1, thr)
        madd_ref[...] = jnp.where(sk > t_eff, 0.0, MASK_VALUE)
        maccs = []
        for h in range(N_HEADS):
            x = _dot(kb_ref[0, c, h // GROUP], qT_ref[0, h]) + madd_ref[...]
            if near:
                x = x + near_bias(c, h)
            s_refs[h][...] = x
            maccs.append(jnp.max(x.reshape(groups, SUBLANES, qb), axis=0))
        alphas = []
        for h in range(N_HEADS):
            s_ref, p_ref, m_ref = s_refs[h], p_refs[h], m_refs[h]
            m_old = m_ref[...]
            m_new = jnp.maximum(m_old, jnp.max(maccs[h], axis=0, keepdims=True))
            for j in range(n_slabs):
                rows = pl.ds(j * SLAB, SLAB)
                p_ref[rows, :] = jnp.exp2(s_ref[rows, :] - m_new).astype(BF16)
            m_ref[...] = m_new
            alphas.append(jnp.exp2(m_old - m_new))
        ones = jnp.ones((2 * SUBLANES, kc), BF16)
        v_aug = [jnp.concatenate([vT_ref[0, c, n * HEAD_DIM:(n + 1) * HEAD_DIM, :], ones], axis=0)
                 for n in range(N_KV_HEADS)]
        for h in range(N_HEADS):
            pv = _dot(v_aug[h // GROUP], p_refs[h][...])
            acc_refs[h][...] = alphas[h] * acc_refs[h][...] + pv[0:HEAD_DIM]
            l_refs[h][...] = alphas[h] * l_refs[h][...] + pv[HEAD_DIM:HEAD_DIM + 1]

    def attn_chunk_fixed(c, near):
        sk = skey_ref[c]
        t_eff = jnp.where(rowi < jbound - c * kc, thr_m1, thr)
        madd_ref[...] = jnp.where(sk > t_eff, 0.0, MASK_VALUE)
        ones = jnp.ones((2 * SUBLANES, kc), BF16)
        v_aug = [jnp.concatenate([vT_ref[0, c, n * HEAD_DIM:(n + 1) * HEAD_DIM, :], ones], axis=0)
                 for n in range(N_KV_HEADS)]
        for h in range(N_HEADS):
            x = _dot(kb_ref[0, c, h // GROUP], qT_ref[0, h]) + madd_ref[...]
            if near:
                x = x + near_bias(c, h)
            p_refs[h][...] = jnp.exp2(x).astype(BF16)
        for h in range(N_HEADS):
            pv = _dot(v_aug[h // GROUP], p_refs[h][...])
            acc_refs[h][...] += pv[0:HEAD_DIM]
            l_refs[h][...] += pv[HEAD_DIM:HEAD_DIM + 1]

    def run(chunk_fn):
        lax.fori_loop(0, n_far, lambda c, carry: (chunk_fn(c, False), carry)[1], 0)
        lax.fori_loop(n_far, nck, lambda c, carry: (chunk_fn(c, True), carry)[1], 0)

    q_sq = jnp.zeros((1, qb), F32)
    for h in range(N_HEADS):
        qf = qT_ref[0, h].astype(F32)
        q_sq = jnp.maximum(q_sq, jnp.sum(qf * qf, axis=0, keepdims=True))
    bound = jnp.sqrt(jnp.max(q_sq) * jnp.max(knorm_ref[0])) + jnp.max(jnp.abs(tab_ref[...]))
    fixed_shift = bound < EXP2_SAFE

    @pl.when(fixed_shift)
    def _():
        run(attn_chunk_fixed)

    @pl.when(jnp.logical_not(fixed_shift))
    def _():
        for h in range(N_HEADS):
            m_refs[h][...] = jnp.full((1, qb), MASK_VALUE, F32)
        run(attn_chunk)

    oT = jnp.concatenate([acc_refs[h][...] / l_refs[h][...] for h in range(N_HEADS)], axis=0)
    o_ref[0] = oT.T.astype(o_ref.dtype)


def _attend(qT, qiT, wT, kidxb, kb, vT, knorm, tabs, *, qb, past, l_true, offsets):
    B, _, _, S = qT.shape
    nc, kc = vT.shape[1], vT.shape[3]
    assert S % qb == 0 and qb % LANES == 0 and past % LANES == 0 and kc % LANES == 0
    topk = min(TOPK_MAX, l_true // 4)
    kern = functools.partial(_attn_kernel, qb=qb, kc=kc, past=past, l_true=l_true,
                             topk=topk, offsets=offsets)
    in_specs = [
        pl.BlockSpec((1, N_HEADS, HEAD_DIM, qb), lambda b, i: (b, 0, 0, i)),
        pl.BlockSpec((1, N_IDX_HEADS, IDX_DIM, qb), lambda b, i: (b, 0, 0, i)),
        pl.BlockSpec((1, N_IDX_HEADS, qb), lambda b, i: (b, 0, i)),
        pl.BlockSpec((1,) + kidxb.shape[1:], lambda b, i: (b, 0, 0, 0)),
        pl.BlockSpec((1,) + kb.shape[1:], lambda b, i: (b, 0, 0, 0, 0)),
        pl.BlockSpec((1,) + vT.shape[1:], lambda b, i: (b, 0, 0, 0)),
        pl.BlockSpec((1,) + knorm.shape[1:], lambda b, i: (b, 0, 0, 0)),
        pl.BlockSpec(tabs.shape, lambda b, i: (0, 0, 0, 0)),
    ]
    return pl.pallas_call(
        kern,
        grid=(B, S // qb),
        in_specs=in_specs,
        out_specs=pl.BlockSpec((1, qb, N_HEADS * HEAD_DIM), lambda b, i: (b, i, 0)),
        out_shape=jax.ShapeDtypeStruct((B, S, N_HEADS * HEAD_DIM), BF16),
        scratch_shapes=[
            pltpu.VMEM((nc, kc, qb), I32),
            pltpu.VMEM((nc, kc, qb), I16),
            pltpu.VMEM((nc, kc, qb), I16),
            pltpu.VMEM((SUBLANES, qb), I32),
            pltpu.VMEM((kc, qb), F32),
        ] + [pltpu.VMEM((kc, qb), F32)] * N_HEADS
          + [pltpu.VMEM((kc, qb), BF16)] * N_HEADS
          + [pltpu.VMEM((1, qb), F32)] * N_HEADS
          + [pltpu.VMEM((1, qb), F32)] * N_HEADS
          + [pltpu.VMEM((HEAD_DIM, qb), F32)] * N_HEADS,
        compiler_params=_cparams(("arbitrary", "arbitrary")),
        name="dsa_attend",
    )(qT, qiT, wT, kidxb, kb, vT, knorm, tabs)


def _mix_kernel(x_ref, attn_ref, conv_ref, wo_ref, g_ref, b_ref, h_ref, *, dn_alpha):
    cat = jnp.concatenate([attn_ref[...], conv_ref[...]], axis=1)
    mixed = _dot(cat, wo_ref[...])
    h_ref[...] = _layer_norm(dn_alpha * x_ref[...] + mixed, g_ref[...], b_ref[...])


def _mix(x2, attn2, conv2, wo, g, b, *, dn_alpha):
    T, D = x2.shape
    tt = min(1024, T)
    assert T % tt == 0
    aw, cw = attn2.shape[1], conv2.shape[1]
    row = lambda i: (i, 0)
    const = lambda i: (0, 0)
    return pl.pallas_call(
        functools.partial(_mix_kernel, dn_alpha=dn_alpha),
        grid=(T // tt,),
        in_specs=[pl.BlockSpec((tt, D), row), pl.BlockSpec((tt, aw), row), pl.BlockSpec((tt, cw), row),
                  pl.BlockSpec(wo.shape, const), pl.BlockSpec((1, D), const), pl.BlockSpec((1, D), const)],
        out_specs=pl.BlockSpec((tt, D), row),
        out_shape=jax.ShapeDtypeStruct((T, D), F32),
        compiler_params=_cparams(("arbitrary",)),
        name="outproj_ln1",
    )(x2, attn2, conv2, wo, g, b)


def _route_picks(logits):
    lane = lax.broadcasted_iota(I32, logits.shape, 1)
    work = logits
    vals, hots = [], []
    for _ in range(TOP_K):
        m = jnp.max(work, axis=1, keepdims=True)
        idx = jnp.min(jnp.where(work == m, lane, LANES), axis=1, keepdims=True)
        hot = lane == idx
        vals.append(m)
        hots.append(hot)
        work = jnp.where(hot, -jnp.inf, work)
    es = [jnp.exp(v - vals[0]) for v in vals]
    den = es[0]
    for e in es[1:]:
        den = den + e
    return hots, [e / den for e in es]


def _route(logits):
    hots, gates = _route_picks(logits)
    comb = jnp.zeros(logits.shape, F32)
    picked = jnp.zeros(logits.shape, F32)
    for gate, hot in zip(gates, hots):
        comb = jnp.where(hot, gate, comb)
        picked = jnp.where(hot, 1.0, picked)
    return comb, picked


def _moe_kernel(h_ref, wr_ref, br_ref, *rest, dn_alpha, rows):
    wgu_refs, rest = rest[:WEIGHT_SPLIT], rest[WEIGHT_SPLIT:]
    bgu_ref, rest = rest[0], rest[1:]
    wd_refs, rest = rest[:WEIGHT_SPLIT], rest[WEIGHT_SPLIT:]
    bd_ref, g_ref, b_ref, y_ref, hb_ref, combT_ref, slotT_ref = rest
    e = pl.program_id(1)
    tt = h_ref.shape[0]
    ff = wd_refs[0].shape[1]

    @pl.when(e == 0)
    def _():
        hb = h_ref[...].astype(BF16)
        hb_ref[...] = hb
        comb, picked = _route(_dot(hb, wr_ref[...]) + br_ref[...])
        combT_ref[...] = comb.T
        before = (lax.broadcasted_iota(I32, (tt, tt), 1) < lax.broadcasted_iota(I32, (tt, tt), 0))
        rank = _dot(jnp.where(before, 1.0, 0.0).astype(BF16), picked.astype(BF16))
        slotT_ref[...] = jnp.where(picked > 0.0, rank, -1.0).T
        y_ref[...] = jnp.zeros(y_ref.shape, F32)

    slot_row = slotT_ref[pl.ds(e, 1), :]
    gate_row = combT_ref[pl.ds(e, 1), :]
    count = jnp.sum(jnp.where(slot_row >= 0.0, 1, 0))
    sub_iota = lax.broadcasted_iota(I32, (rows, tt), 0).astype(F32)

    def chunk_body(c, carry):
        base = (c * rows).astype(F32)
        hit = slot_row - base == sub_iota
        gather = jnp.where(hit, 1.0, 0.0).astype(BF16)
        row_gate = jnp.sum(jnp.where(hit, gate_row, 0.0), axis=1, keepdims=True)
        xg = _dot(gather, hb_ref[...]).astype(BF16)
        gu = jnp.concatenate([_dot(xg, w[0]) for w in wgu_refs], axis=1) + bgu_ref[0]
        gate = jnp.minimum(gu[:, 0:ff], SWIGLU_LIMIT)
        up = jnp.clip(gu[:, ff:2 * ff], -SWIGLU_LIMIT, SWIGLU_LIMIT)
        act = ((up + 1.0) * (gate * (1.0 / (1.0 + jnp.exp(-SWIGLU_ALPHA * gate))))).astype(BF16)
        y = (jnp.concatenate([_dot(act, w[0]) for w in wd_refs], axis=1) + bd_ref[0]) * row_gate
        y_ref[...] += lax.dot_general(gather, y.astype(BF16), (((0,), (0,)), ((), ())),
                                      preferred_element_type=F32)
        return carry

    lax.fori_loop(0, (count + rows - 1) // rows, chunk_body, 0)

    @pl.when(e == pl.num_programs(1) - 1)
    def _():
        y_ref[...] = _layer_norm(dn_alpha * h_ref[...] + y_ref[...], g_ref[...], b_ref[...])


def _moe(h2, wr, br, wgu, bgu, wd, bd, g, b, *, dn_alpha):
    T, D = h2.shape
    E, _, F2 = wgu.shape
    tt = min(1024, T)
    assert T % tt == 0
    rows = min(tt, -(-(tt * TOP_K * 5 // (E * 4)) // 16) * 16)
    row = lambda i, e: (i, 0)
    const = lambda i, e: (0, 0)
    exp3 = lambda i, e: (e, 0, 0)
    ns = WEIGHT_SPLIT
    col = lambda j: (lambda i, e: (e, 0, j))
    return pl.pallas_call(
        functools.partial(_moe_kernel, dn_alpha=dn_alpha, rows=rows),
        grid=(T // tt, E),
        in_specs=[pl.BlockSpec((tt, D), row), pl.BlockSpec(wr.shape, const), pl.BlockSpec(br.shape, const)]
                 + [pl.BlockSpec((1, D, F2 // ns), col(j)) for j in range(ns)]
                 + [pl.BlockSpec((1, 1, F2), exp3)]
                 + [pl.BlockSpec((1, F2 // 2, D // ns), col(j)) for j in range(ns)]
                 + [pl.BlockSpec((1, 1, D), exp3),
                    pl.BlockSpec((1, D), const), pl.BlockSpec((1, D), const)],
        out_specs=pl.BlockSpec((tt, D), row),
        out_shape=jax.ShapeDtypeStruct((T, D), F32),
        scratch_shapes=[pltpu.VMEM((tt, D), BF16), pltpu.VMEM((LANES, tt), F32), pltpu.VMEM((LANES, tt), F32)],
        compiler_params=_cparams(("arbitrary", "arbitrary")),
        name="moe_ln2",
    )(h2, wr, br, *([wgu] * ns), bgu, *([wd] * ns), bd, g, b)


def _pack_pairs(x):
    w = x.shape[1] // 2
    bits = pltpu.bitcast(x.astype(BF16).astype(F32), I32)
    return (bits[:, 0:w] & jnp.int32(-65536)) | lax.shift_right_logical(bits[:, w:2 * w], 16)


def _unpack_pairs(p):
    hi = pltpu.bitcast(p & jnp.int32(-65536), F32)
    lo = pltpu.bitcast(lax.shift_left(p, 16), F32)
    return jnp.concatenate([hi, lo], axis=1)


def _router_kernel(h_ref, wrT_ref, brT_ref, hpk_ref, infoT_ref, gcol_ref, cnt_ref):
    tt = h_ref.shape[0]
    ne = cnt_ref.shape[1]
    h = h_ref[...]
    hpk_ref[...] = _pack_pairs(h)
    logits = lax.dot_general(wrT_ref[...], h.astype(BF16), (((1,), (1,)), ((), ())),
                             preferred_element_type=F32)
    work = logits[0:ne] + jnp.tile(brT_ref[0:ne, :], (1, tt // LANES))
    sub = lax.broadcasted_iota(I32, (ne, tt), 0)
    vals, idxs, hots = [], [], []
    for _ in range(TOP_K):
        m = jnp.max(work, axis=0, keepdims=True)
        idx = jnp.min(jnp.where(work == m, sub, ne), axis=0, keepdims=True)
        hot = sub == idx
        vals.append(m)
        idxs.append(idx)
        hots.append(hot)
        work = jnp.where(hot, -jnp.inf, work)
    es = [jnp.exp(v - vals[0]) for v in vals]
    den = es[0]
    for e in es[1:]:
        den = den + e
    gates = [e / den for e in es]
    picked = jnp.zeros((ne, tt), F32)
    for hot in hots:
        picked = jnp.where(hot, 1.0, picked)
    earlier = lax.broadcasted_iota(I32, (tt, tt), 0) < lax.broadcasted_iota(I32, (tt, tt), 1)
    rank = _dot(picked.astype(BF16), jnp.where(earlier, 1.0, 0.0).astype(BF16))
    ranks = [jnp.sum(jnp.where(hot, rank, 0.0), axis=0, keepdims=True) for hot in hots]
    pad = jnp.zeros((2 * SUBLANES - 3 * TOP_K, tt), F32)
    infoT_ref[...] = jnp.concatenate([i.astype(F32) for i in idxs] + gates + ranks + [pad], axis=0)
    gcol_ref[...] = jnp.concatenate(gates + [jnp.zeros((LANES - TOP_K, tt), F32)], axis=0).T
    cnt_ref[0] = jnp.broadcast_to(jnp.sum(picked, axis=1, keepdims=True), (ne, LANES))


def _router(h2, wr, br, ne):
    T, D = h2.shape
    tt = 1024
    row = lambda i: (i, 0)
    const = lambda i: (0, 0)
    wrT = wr.T
    brT = jnp.broadcast_to(br.reshape(LANES, 1), (LANES, LANES))
    return pl.pallas_call(
        _router_kernel,
        grid=(T // tt,),
        in_specs=[pl.BlockSpec((tt, D), row), pl.BlockSpec(wrT.shape, const), pl.BlockSpec(brT.shape, const)],
        out_specs=(pl.BlockSpec((tt, D // 2), row), pl.BlockSpec((2 * SUBLANES, tt), lambda i: (0, i)),
                   pl.BlockSpec((tt, LANES), row), pl.BlockSpec((1, ne, LANES), lambda i: (i, 0, 0))),
        out_shape=(jax.ShapeDtypeStruct((T, D // 2), I32), jax.ShapeDtypeStruct((2 * SUBLANES, T), F32),
                   jax.ShapeDtypeStruct((T, LANES), F32), jax.ShapeDtypeStruct((T // tt, ne, LANES), F32)),
        compiler_params=_cparams(("arbitrary",)),
        name="moe_router",
    )(h2, wrT, brT)


def _sc_gather_rows(table, idx):
    M, W = idx.shape[0], table.shape[1]
    n_workers = SC_CORES * SC_SUBCORES
    per_w = M // n_workers
    n_ch = per_w // SC_ROWS
    assert M == n_workers * n_ch * SC_ROWS
    mesh = plsc.VectorSubcoreMesh(core_axis_name="c", subcore_axis_name="s")

    @functools.partial(
        pl.kernel, mesh=mesh,
        out_type=jax.ShapeDtypeStruct((M, W), table.dtype),
        scratch_types=[pltpu.VMEM((n_ch, SC_ROWS), I32), pltpu.VMEM((SC_ROWS, W), table.dtype),
                       pltpu.SemaphoreType.DMA],
    )
    def gather(table_hbm, idx_hbm, out_hbm, idx_v, rows_v, sem):
        wid = lax.axis_index("s") * SC_CORES + lax.axis_index("c")
        pltpu.sync_copy(idx_hbm.at[wid], idx_v)

        @pl.loop(0, n_ch)
        def _(j):
            pltpu.async_copy(table_hbm.at[idx_v.at[j]], rows_v, sem).wait()
            pltpu.sync_copy(rows_v, out_hbm.at[pl.ds(wid * per_w + j * SC_ROWS, SC_ROWS)])

    return gather(table, idx.reshape(n_workers, n_ch, SC_ROWS))


def _expert_rows_kernel(te_ref, used_ref, x_ref, *rest):
    wgu_refs, rest = rest[:WEIGHT_SPLIT], rest[WEIGHT_SPLIT:]
    bgu_ref, rest = rest[0], rest[1:]
    wd_refs, rest = rest[:WEIGHT_SPLIT], rest[WEIGHT_SPLIT:]
    bd_ref, y_ref = rest
    ff = wd_refs[0].shape[1]

    @pl.when(pl.program_id(0) < used_ref[0])
    def _():
        xg = _unpack_pairs(x_ref[...]).astype(BF16)
        gu = jnp.concatenate([_dot(xg, w[0]) for w in wgu_refs], axis=1) + bgu_ref[0]
        gate = jnp.minimum(gu[:, 0:ff], SWIGLU_LIMIT)
        up = jnp.clip(gu[:, ff:2 * ff], -SWIGLU_LIMIT, SWIGLU_LIMIT)
        act = ((up + 1.0) * (gate * (1.0 / (1.0 + jnp.exp(-SWIGLU_ALPHA * gate))))).astype(BF16)
        y_ref[...] = _pack_pairs(jnp.concatenate([_dot(act, w[0]) for w in wd_refs], axis=1) + bd_ref[0])


def _expert_rows(tile_expert, used_tiles, xs, wgu, bgu, wd, bd):
    R, W = xs.shape
    E, D, F2 = wgu.shape
    ns = WEIGHT_SPLIT
    col = lambda c: (lambda j, te, used: (te[j], 0, c))
    exp3 = lambda j, te, used: (te[j], 0, 0)
    rows = lambda j, te, used: (j, 0)
    grid_spec = pltpu.PrefetchScalarGridSpec(
        num_scalar_prefetch=2,
        grid=(R // ROW_TILE,),
        in_specs=[pl.BlockSpec((ROW_TILE, W), rows)]
                 + [pl.BlockSpec((1, D, F2 // ns), col(c)) for c in range(ns)]
                 + [pl.BlockSpec((1, 1, F2), exp3)]
                 + [pl.BlockSpec((1, F2 // 2, D // ns), col(c)) for c in range(ns)]
                 + [pl.BlockSpec((1, 1, D), exp3)],
        out_specs=pl.BlockSpec((ROW_TILE, W), rows),
    )
    return pl.pallas_call(
        _expert_rows_kernel,
        grid_spec=grid_spec,
        out_shape=jax.ShapeDtypeStruct((R, W), I32),
        compiler_params=_cparams(("arbitrary",)),
        name="moe_expert_rows",
    )(tile_expert, used_tiles, xs, *([wgu] * ns), bgu, *([wd] * ns), bd)


def _combine_kernel(h_ref, yg_ref, gcol_ref, g_ref, b_ref, o_ref, *, dn_alpha):
    acc = dn_alpha * h_ref[...]
    for k in range(TOP_K):
        acc = acc + gcol_ref[:, k:k + 1] * _unpack_pairs(yg_ref[k])
    o_ref[...] = _layer_norm(acc, g_ref[...], b_ref[...])


def _combine(h2, yg, info, g, b, *, dn_alpha):
    T, D = h2.shape
    tt = 512
    row = lambda i: (i, 0)
    const = lambda i: (0, 0)
    return pl.pallas_call(
        functools.partial(_combine_kernel, dn_alpha=dn_alpha),
        grid=(T // tt,),
        in_specs=[pl.BlockSpec((tt, D), row), pl.BlockSpec((TOP_K, tt, D // 2), lambda i: (0, i, 0)),
                  pl.BlockSpec((tt, LANES), row),
                  pl.BlockSpec((1, D), const), pl.BlockSpec((1, D), const)],
        out_specs=pl.BlockSpec((tt, D), row),
        out_shape=jax.ShapeDtypeStruct((T, D), F32),
        compiler_params=_cparams(("arbitrary",)),
        name="moe_combine_ln2",
    )(h2, yg, info, g, b)


def _dest_kernel(infoT_ref, base_ref, dest_ref):
    tt = infoT_ref.shape[1]
    ne = base_ref.shape[1]
    base = jnp.tile(base_ref[0], (1, tt // LANES))
    sub = lax.broadcasted_iota(I32, (ne, tt), 0)
    rows = []
    for k in range(TOP_K):
        hot = sub == infoT_ref[k:k + 1, :].astype(I32)
        rows.append(jnp.sum(jnp.where(hot, base, 0.0), axis=0, keepdims=True)
                    + infoT_ref[2 * TOP_K + k:2 * TOP_K + k + 1, :])
    rows.append(jnp.zeros((SUBLANES - TOP_K, tt), F32))
    dest_ref[...] = jnp.concatenate(rows, axis=0).astype(I32)


def _dest_rows(infoT, base):
    T = infoT.shape[1]
    tt = 1024
    ne = base.shape[1]
    return pl.pallas_call(
        _dest_kernel,
        grid=(T // tt,),
        in_specs=[pl.BlockSpec((2 * SUBLANES, tt), lambda i: (0, i)), pl.BlockSpec((1, ne, LANES), lambda i: (i, 0, 0))],
        out_specs=pl.BlockSpec((SUBLANES, tt), lambda i: (0, i)),
        out_shape=jax.ShapeDtypeStruct((SUBLANES, T), I32),
        compiler_params=_cparams(("arbitrary",)),
        name="moe_dest_rows",
    )(infoT, base)


def _sc_scatter_rows(src, dest_km, n_rows):
    T, W = src.shape
    K = dest_km.shape[0]
    n_workers = SC_CORES * SC_SUBCORES
    per_w = T // n_workers
    n_ch = per_w // SC_ROWS
    assert T == n_workers * n_ch * SC_ROWS
    mesh = plsc.VectorSubcoreMesh(core_axis_name="c", subcore_axis_name="s")
    idx = dest_km.reshape(K, n_workers, n_ch, SC_ROWS).transpose(1, 0, 2, 3).reshape(n_workers, K * n_ch, SC_ROWS)

    @functools.partial(
        pl.kernel, mesh=mesh,
        out_type=jax.ShapeDtypeStruct((n_rows, W), src.dtype),
        scratch_types=[pltpu.VMEM((K * n_ch, SC_ROWS), I32), pltpu.VMEM((SC_ROWS, W), src.dtype),
                       pltpu.SemaphoreType.DMA],
    )
    def scatter(src_hbm, idx_hbm, out_hbm, idx_v, rows_v, sem):
        wid = lax.axis_index("s") * SC_CORES + lax.axis_index("c")
        pltpu.sync_copy(idx_hbm.at[wid], idx_v)

        @pl.loop(0, n_ch)
        def _(j):
            pltpu.sync_copy(src_hbm.at[pl.ds(wid * per_w + j * SC_ROWS, SC_ROWS)], rows_v)
            for k in range(K):
                pltpu.async_copy(rows_v, out_hbm.at[idx_v.at[k * n_ch + j]], sem).wait()

    return scatter(src, idx)


def _moe_sorted(h2, wr, br, wgu, bgu, wd, bd, g, b, *, dn_alpha):
    T, D = h2.shape
    E = wgu.shape[0]
    hpk, infoT, gcol, cnt = _router(h2, wr, br, E)
    cnt = cnt[:, :, 0].astype(I32)
    total = jnp.sum(cnt, axis=0)
    padded = -(-total // ROW_TILE) * ROW_TILE
    ends = jnp.cumsum(padded)
    base = (ends - padded)[None, :] + jnp.cumsum(cnt, axis=0) - cnt
    base = jnp.broadcast_to(base.astype(F32)[:, :, None], cnt.shape + (LANES,))
    dest_km = _dest_rows(infoT, base)[0:TOP_K]
    R = T * TOP_K + E * ROW_TILE
    tile_start = jnp.arange(R // ROW_TILE, dtype=I32) * ROW_TILE
    tile_expert = jnp.minimum(jnp.sum((ends[None, :] <= tile_start[:, None]).astype(I32), axis=1), E - 1)
    used_tiles = (ends[E - 1] // ROW_TILE).reshape(1)
    xs = _sc_scatter_rows(hpk, dest_km, R)
    ys = _expert_rows(tile_expert, used_tiles, xs, wgu, bgu, wd, bd)
    yg = _sc_gather_rows(ys, dest_km.reshape(-1))
    return _combine(h2, yg.reshape(TOP_K, T, D // 2), gcol, g, b, dn_alpha=dn_alpha)


def _split_gu_kernel(w_ref, perm_ref, o_ref, *, band):
    n = w_ref.shape[2]
    for m in range(n // band):
        both = _dot(w_ref[0, :, m * band:(m + 1) * band].astype(BF16), perm_ref[...]).astype(BF16)
        o_ref[0, :, m * (band // 2):(m + 1) * (band // 2)] = both[:, 0:band // 2]
        o_ref[0, :, n // 2 + m * (band // 2):n // 2 + (m + 1) * (band // 2)] = both[:, band // 2:band]


def _split_gu(w_gu):
    E, D, N = w_gu.shape
    tr = min(512, D)
    band = min(512, N)
    j = jnp.arange(band, dtype=I32)
    src = jnp.where(j < band // 2, 2 * j, 2 * (j - band // 2) + 1)
    perm = (jnp.arange(band, dtype=I32)[:, None] == src[None, :]).astype(BF16)
    return pl.pallas_call(
        functools.partial(_split_gu_kernel, band=band),
        grid=(E, D // tr),
        in_specs=[pl.BlockSpec((1, tr, N), lambda e, r: (e, r, 0)),
                  pl.BlockSpec((band, band), lambda e, r: (0, 0))],
        out_specs=pl.BlockSpec((1, tr, N), lambda e, r: (e, r, 0)),
        out_shape=jax.ShapeDtypeStruct((E, D, N), BF16),
        compiler_params=_cparams(("arbitrary", "arbitrary")),
        name="split_gate_up",
    )(w_gu, perm)


def _prep_weights(w_in, w_conv, w_out, ln1_g, ln1_b, w_router, b_router, w_gu, b_gu, w_down, b_down,
                  ln2_g, ln2_b):
    aw = N_HEADS * HEAD_DIM
    kv = N_KV_HEADS * HEAD_DIM
    iw = N_IDX_HEADS * IDX_DIM
    D = w_in.shape[0]
    cw = D - aw
    o_q, o_k, o_v = 0, aw, aw + kv
    o_qi = o_v + kv
    o_ki = o_qi + iw
    o_wi = o_ki + IDX_DIM
    o_c = o_wi + N_IDX_HEADS
    wa = jnp.concatenate([w_in[:, o_q:o_k], w_in[:, o_qi:o_ki], w_in[:, o_k:o_v], w_in[:, o_v:o_qi]],
                         axis=1).astype(BF16)
    ws = jnp.pad(w_in[:, o_ki:o_c], ((0, 0), (0, LANES - IDX_DIM - N_IDX_HEADS))).astype(BF16)
    wc = w_in[:, o_c:o_c + 3 * cw].astype(BF16)
    E = w_router.shape[1]
    wr = jnp.pad(w_router, ((0, 0), (0, LANES - E))).astype(BF16)
    br = jnp.pad(b_router, (0, LANES - E), constant_values=MASK_VALUE).reshape(1, LANES)
    F = w_gu.shape[2] // 2
    return dict(
        wa=wa, ws=ws, wc=wc, wconv=w_conv, wo=w_out.astype(BF16),
        ln1_g=ln1_g.reshape(1, D), ln1_b=ln1_b.reshape(1, D),
        wr=wr, br=br,
        wgu=_split_gu(w_gu),
        bgu=jnp.concatenate([b_gu[:, 0::2], b_gu[:, 1::2]], axis=1).reshape(E, 1, 2 * F),
        wd=w_down.astype(BF16), bd=b_down.reshape(E, 1, D),
        ln2_g=ln2_g.reshape(1, D), ln2_b=ln2_b.reshape(1, D),
    )


def _decode_layouts(pq, small, k_all, v_all, kidx_all, kc, qb):
    B, S, _ = pq.shape
    aw = N_HEADS * HEAD_DIM
    L, kv = k_all.shape[1], k_all.shape[2]
    lp = -(-L // kc) * kc
    nc = lp // kc
    padk = lambda a: jnp.pad(a, ((0, 0), (0, lp - L), (0, 0)))
    padq = lambda a: jnp.pad(a, ((0, 0), (0, 0), (0, 0), (0, qb - S)))
    q = pq[:, :, 0:aw] * QK_SCALE
    qT = padq(q.reshape(B, S, N_HEADS, HEAD_DIM).transpose(0, 2, 3, 1)).astype(BF16)
    qiT = padq(pq[:, :, aw:].reshape(B, S, N_IDX_HEADS, IDX_DIM).transpose(0, 2, 3, 1)).astype(BF16)
    wT = jnp.pad(small[:, :, IDX_DIM:IDX_DIM + N_IDX_HEADS].transpose(0, 2, 1), ((0, 0), (0, 0), (0, qb - S)))
    kb = padk(k_all).reshape(B, nc, kc, N_KV_HEADS, HEAD_DIM).transpose(0, 1, 3, 2, 4).astype(BF16)
    vT = padk(v_all).reshape(B, nc, kc, kv).transpose(0, 1, 3, 2).astype(BF16)
    kidxb = padk(kidx_all).reshape(B, nc, kc, IDX_DIM).astype(BF16)
    knorm = jnp.sum(kb.astype(F32) ** 2, axis=-1)
    return qT, qiT, wT, kidxb, kb, vT, knorm


def _layer(x, past_k, past_v, past_kidx, conv_buf, rel_bias, w, *, dn_alpha):
    B, S, D = x.shape
    prefill = past_k is None
    outs = _project(x, conv_buf, w["wa"], w["ws"], w["wc"], w["wconv"], attn_layouts=prefill)
    k, v, small, conv, new_buf = outs[:5]
    k_idx = small[:, :, 0:IDX_DIM]
    if prefill:
        past, l_true = 0, S
        qb = min(QUERY_BLOCK, S)
        kb, vT, kidxb, qT, qiT, wT, knorm = outs[5:]
    else:
        past = past_k.shape[1]
        l_true = past + S
        qb = -(-S // LANES) * LANES
        qT, qiT, wT, kidxb, kb, vT, knorm = _decode_layouts(
            outs[5], small, jnp.concatenate([past_k, k], axis=1), jnp.concatenate([past_v, v], axis=1),
            jnp.concatenate([past_kidx, k_idx], axis=1), KEY_CHUNK, qb)
    offsets = (-LANES, 0)
    tabs = _bias_tables(rel_bias, offsets)
    attn = _attend(qT, qiT, wT, kidxb, kb, vT, knorm, tabs, qb=qb, past=past, l_true=l_true, offsets=offsets)
    attn = attn[:, :S]
    h = _mix(x.reshape(B * S, D), attn.reshape(B * S, -1), conv.reshape(B * S, -1), w["wo"],
             w["ln1_g"], w["ln1_b"], dn_alpha=dn_alpha)
    sc_unit = SC_CORES * SC_SUBCORES * SC_ROWS
    sortable = (B * S) % 1024 == 0 and (B * S * TOP_K) % sc_unit == 0 and (N_EXPERTS * ROW_TILE) % sc_unit == 0
    moe = _moe_sorted if sortable and B * S >= SORTED_MIN_TOKENS else _moe
    y = moe(h, w["wr"], w["br"], w["wgu"], w["bgu"], w["wd"], w["bd"],
            w["ln2_g"], w["ln2_b"], dn_alpha=dn_alpha)
    return (y.reshape(B, S, D), k.reshape(B, S, N_KV_HEADS, HEAD_DIM), v.reshape(B, S, N_KV_HEADS, HEAD_DIM),
            k_idx, new_buf)


def kernel(x_prompt, x_sample, cache_k, cache_v, cache_kidx, state_conv, rel_bias, w_in, w_conv, w_out,
           ln1_g, ln1_b, w_router, b_router, w_gu, b_gu, w_down, b_down, ln2_g, ln2_b):
    depth = w_in.shape[0]
    assert depth == 1
    dn_alpha = (2 * depth) ** 0.25
    kv = N_KV_HEADS * HEAD_DIM
    w = _prep_weights(w_in[0], w_conv[0], w_out[0], ln1_g[0], ln1_b[0], w_router[0], b_router[0],
                      w_gu[0], b_gu[0], w_down[0], b_down[0], ln2_g[0], ln2_b[0])
    Bp = x_prompt.shape[0]
    cw = w_conv.shape[2]
    zero_buf = jnp.zeros((Bp, CONV_K - 1, cw), F32)
    yp, k1, v1, i1, c1 = _layer(x_prompt, None, None, None, zero_buf, rel_bias, w, dn_alpha=dn_alpha)
    Bs, P = cache_k.shape[1], cache_k.shape[2]
    ys, k2, v2, i2, c2 = _layer(x_sample, cache_k[0].reshape(Bs, P, kv), cache_v[0].reshape(Bs, P, kv),
                                cache_kidx[0], state_conv[0], rel_bias, w, dn_alpha=dn_alpha)
    return (yp, ys, k1[None], v1[None], i1[None], c1[None], k2[None], v2[None], i2[None], c2[None])
```

```python
import functools
import math

import jax
import jax.numpy as jnp
from jax import lax
from jax.experimental import pallas as pl
from jax.experimental.pallas import tpu as pltpu
from jax.experimental.pallas import tpu_sc as plsc

F32 = jnp.float32
BF16 = jnp.bfloat16
I32 = jnp.int32
I16 = jnp.int16

CHUNK = 64
N_HEADS = 8
HEAD_DIM = 64
N_KV_HEADS = 2
GROUP = N_HEADS // N_KV_HEADS
N_IDX_HEADS = 8
IDX_DIM = 32
TOPK_MAX = 256
CONV_K = 3
N_BUCKETS = 32
MAX_DISTANCE = 128
N_EXPERTS = 32
TOP_K = 4
SWIGLU_LIMIT = 7.0
SWIGLU_ALPHA = 1.702
LN_EPS = 1e-5
MASK_VALUE = -1e30
QK_SCALE = HEAD_DIM ** -0.5 * math.log2(math.e)
EXP2_SAFE = 96.0

LANES = 128
SUBLANES = 8
KEY_CHUNK = 512
QUERY_BLOCK = 256
SLAB = 64
WEIGHT_SPLIT = 4
ROW_TILE = 512
ROUTE_TILE = 1024
SC_CORES = 2
SC_SUBCORES = 16
SC_ROWS = 64
SORTED_MIN_TOKENS = 2048
INT_MIN = -(2 ** 31)
VMEM_LIMIT = 56 * 1024 * 1024


def _cparams(sem):
    return pltpu.CompilerParams(dimension_semantics=sem, vmem_limit_bytes=VMEM_LIMIT)


def _dot(a, b):
    return jnp.dot(a, b, preferred_element_type=F32)


def _layer_norm(z, g, b):
    mu = jnp.mean(z, axis=-1, keepdims=True)
    zc = z - mu
    var = jnp.mean(zc * zc, axis=-1, keepdims=True)
    return zc * lax.rsqrt(var + LN_EPS) * g + b


def _bucket_thresholds():
    nb = N_BUCKETS // 2
    max_exact = nb // 2
    out = []
    for j in range(1, nb - max_exact):
        out.append(math.ceil(max_exact * (MAX_DISTANCE / max_exact) ** (j / (nb - max_exact)) - 1e-9))
    return tuple(out)


def _bias_table_kernel(rel_ref, tab_ref, *, offsets):
    nb = N_BUCKETS // 2
    max_exact = nb // 2
    thr = _bucket_thresholds()
    ii = lax.broadcasted_iota(I32, (LANES, LANES), 0)
    jj = lax.broadcasted_iota(I32, (LANES, LANES), 1)
    for d, off in enumerate(offsets):
        rel = off + ii - jj
        n = jnp.abs(rel)
        large = jnp.full((LANES, LANES), max_exact, I32)
        for t in thr:
            large = large + jnp.where(n >= t, 1, 0)
        bucket = jnp.where(rel > 0, nb, 0) + jnp.where(n < max_exact, n, large)
        for h in range(N_HEADS):
            acc = jnp.zeros((LANES, LANES), F32)
            for b in range(N_BUCKETS):
                acc = jnp.where(bucket == b, rel_ref[b, h], acc)
            tab_ref[d, h] = (acc - rel_ref[nb - 1, h]) * math.log2(math.e)


def _bias_tables(rel_bias, offsets):
    return pl.pallas_call(
        functools.partial(_bias_table_kernel, offsets=offsets),
        out_shape=jax.ShapeDtypeStruct((len(offsets), N_HEADS, LANES, LANES), F32),
        in_specs=[pl.BlockSpec(memory_space=pltpu.SMEM)],
        out_specs=pl.BlockSpec(memory_space=pltpu.VMEM),
        name="bias_tables",
    )(rel_bias)


def _proj_kernel(x_ref, wa_ref, ws_ref, wc_ref, wconv_ref, buf_ref,
                 k_ref, v_ref, small_ref, conv_ref, nbuf_ref, *rest, ts, attn_layouts):
    s = pl.program_id(1)
    xb = x_ref[0].astype(BF16)
    pa = _dot(xb, wa_ref[...])
    ps = _dot(xb, ws_ref[...])
    pc = _dot(xb, wc_ref[...])
    aw = N_HEADS * HEAD_DIM
    iw = N_IDX_HEADS * IDX_DIM
    kv = N_KV_HEADS * HEAD_DIM
    k = pa[:, aw + iw:aw + iw + kv]
    v = pa[:, aw + iw + kv:aw + iw + 2 * kv]
    k_ref[0] = k
    v_ref[0] = v
    small_ref[0] = ps
    scale = QK_SCALE
    if attn_layouts:
        kb_ref, vT_ref, kidxb_ref, qT_ref, qiT_ref, wT_ref, knorm_ref, carry_ref = rest
        for n in range(N_KV_HEADS):
            kb_ref[0, 0, n] = k[:, n * HEAD_DIM:(n + 1) * HEAD_DIM].astype(BF16)
        kTf = k.astype(BF16).astype(F32).T
        knorm_ref[0, 0] = jnp.concatenate(
            [jnp.sum(kTf[n * HEAD_DIM:(n + 1) * HEAD_DIM] ** 2, axis=0, keepdims=True) for n in range(N_KV_HEADS)],
            axis=0)
        vT_ref[0, 0] = v.T.astype(BF16)
        kidxb_ref[0, 0] = ps[:, 0:IDX_DIM].astype(BF16)
        qT = (pa[:, 0:aw] * scale).T.astype(BF16)
        for h in range(N_HEADS):
            qT_ref[0, h] = qT[h * HEAD_DIM:(h + 1) * HEAD_DIM]
        qiT = pa[:, aw:aw + iw].T.astype(BF16)
        for h in range(N_IDX_HEADS):
            qiT_ref[0, h] = qiT[h * IDX_DIM:(h + 1) * IDX_DIM]
        wT_ref[0] = ps.T[IDX_DIM:IDX_DIM + N_IDX_HEADS]
    else:
        pq_ref, carry_ref = rest
        pq_ref[0] = pa[:, 0:aw + iw]

    cw = pc.shape[1] // 3
    u = pc[:, cw:2 * cw] * pc[:, 2 * cw:3 * cw]

    @pl.when(s == 0)
    def _():
        carry_ref[6:8, :] = buf_ref[0]

    carry_ref[8:8 + ts, :] = u
    y = (carry_ref[6:6 + ts, :] * wconv_ref[0:1, :]
         + carry_ref[7:7 + ts, :] * wconv_ref[1:2, :]
         + u * wconv_ref[2:3, :])
    conv_ref[0] = (pc[:, 0:cw] * y).astype(BF16)
    nb = carry_ref[ts + 6:ts + 8, :]
    nbuf_ref[0] = nb
    carry_ref[6:8, :] = nb


def _project(x, conv_buf, wa, ws, wc, wconv, *, attn_layouts):
    B, S, D = x.shape
    ts = min(KEY_CHUNK, S)
    assert S % ts == 0 and S >= CONV_K - 1
    ns = S // ts
    cw = wc.shape[1] // 3
    kv = N_KV_HEADS * HEAD_DIM
    aw = N_HEADS * HEAD_DIM
    iw = N_IDX_HEADS * IDX_DIM
    out_shape = [
        jax.ShapeDtypeStruct((B, S, kv), F32),
        jax.ShapeDtypeStruct((B, S, kv), F32),
        jax.ShapeDtypeStruct((B, S, LANES), F32),
        jax.ShapeDtypeStruct((B, S, cw), BF16),
        jax.ShapeDtypeStruct((B, CONV_K - 1, cw), F32),
    ]
    out_specs = [
        pl.BlockSpec((1, ts, kv), lambda b, s: (b, s, 0)),
        pl.BlockSpec((1, ts, kv), lambda b, s: (b, s, 0)),
        pl.BlockSpec((1, ts, LANES), lambda b, s: (b, s, 0)),
        pl.BlockSpec((1, ts, cw), lambda b, s: (b, s, 0)),
        pl.BlockSpec((1, CONV_K - 1, cw), lambda b, s: (b, 0, 0)),
    ]
    if attn_layouts:
        out_shape += [
            jax.ShapeDtypeStruct((B, ns, N_KV_HEADS, ts, HEAD_DIM), BF16),
            jax.ShapeDtypeStruct((B, ns, kv, ts), BF16),
            jax.ShapeDtypeStruct((B, ns, ts, IDX_DIM), BF16),
            jax.ShapeDtypeStruct((B, N_HEADS, HEAD_DIM, S), BF16),
            jax.ShapeDtypeStruct((B, N_IDX_HEADS, IDX_DIM, S), BF16),
            jax.ShapeDtypeStruct((B, N_IDX_HEADS, S), F32),
            jax.ShapeDtypeStruct((B, ns, N_KV_HEADS, ts), F32),
        ]
        out_specs += [
            pl.BlockSpec((1, 1, N_KV_HEADS, ts, HEAD_DIM), lambda b, s: (b, s, 0, 0, 0)),
            pl.BlockSpec((1, 1, kv, ts), lambda b, s: (b, s, 0, 0)),
            pl.BlockSpec((1, 1, ts, IDX_DIM), lambda b, s: (b, s, 0, 0)),
            pl.BlockSpec((1, N_HEADS, HEAD_DIM, ts), lambda b, s: (b, 0, 0, s)),
            pl.BlockSpec((1, N_IDX_HEADS, IDX_DIM, ts), lambda b, s: (b, 0, 0, s)),
            pl.BlockSpec((1, N_IDX_HEADS, ts), lambda b, s: (b, 0, s)),
            pl.BlockSpec((1, 1, N_KV_HEADS, ts), lambda b, s: (b, s, 0, 0)),
        ]
    else:
        out_shape += [jax.ShapeDtypeStruct((B, S, aw + iw), F32)]
        out_specs += [pl.BlockSpec((1, ts, aw + iw), lambda b, s: (b, s, 0))]
    const2 = lambda b, s: (0, 0)
    in_specs = [
        pl.BlockSpec((1, ts, D), lambda b, s: (b, s, 0)),
        pl.BlockSpec(wa.shape, const2),
        pl.BlockSpec(ws.shape, const2),
        pl.BlockSpec(wc.shape, const2),
        pl.BlockSpec(wconv.shape, const2),
        pl.BlockSpec((1, CONV_K - 1, cw), lambda b, s: (b, 0, 0)),
    ]
    return pl.pallas_call(
        functools.partial(_proj_kernel, ts=ts, attn_layouts=attn_layouts),
        grid=(B, ns),
        in_specs=in_specs,
        out_specs=tuple(out_specs),
        out_shape=tuple(out_shape),
        scratch_shapes=[pltpu.VMEM((ts + 8, cw), F32)],
        compiler_params=_cparams(("arbitrary", "arbitrary")),
        name="proj_conv",
    )(x, wa, ws, wc, wconv, conv_buf)


def _attn_kernel(qT_ref, qiT_ref, wT_ref, kidx_ref, kb_ref, vT_ref, knorm_ref, tab_ref,
                 o_ref, skey_ref, hi16_ref, lo16_ref, jb_ref, madd_ref, *head_refs,
                 qb, kc, past, l_true, topk, offsets):
    s_refs, p_refs, m_refs, l_refs, acc_refs = (head_refs[g * N_HEADS:(g + 1) * N_HEADS] for g in range(5))
    i = pl.program_id(1)
    qoff = past + i * qb
    adm_end = jnp.minimum(((qoff + qb - 1) // CHUNK + 1) * CHUNK, l_true)
    nck = (adm_end + kc - 1) // kc
    n_far = jnp.maximum(qoff - LANES, 0) // kc
    idx_bits = int(l_true).bit_length()
    groups = kc // SUBLANES

    rowi = lax.broadcasted_iota(I32, (kc, qb), 0)
    qpos = qoff + lax.broadcasted_iota(I32, (1, qb), 1)
    kmax = jnp.minimum((qpos // CHUNK + 1) * CHUNK, l_true)

    def score_body(c, carry):
        kidx_c = kidx_ref[0, c]
        sc = jnp.zeros((kc, qb), F32)
        for h in range(N_IDX_HEADS):
            d = _dot(kidx_c, qiT_ref[0, h])
            sc = sc + wT_ref[0, h:h + 1, :] * jnp.maximum(d, 0.0)
        bits = pltpu.bitcast(sc, I32)
        skey = jnp.where(bits < 0, bits ^ jnp.int32(0x7FFFFFFF), bits)
        skey = jnp.where(skey == -1, 0, skey)
        skey = jnp.where(rowi < kmax - c * kc, skey, jnp.int32(INT_MIN))
        skey_ref[c] = skey
        hi16_ref[c] = (skey >> 16).astype(I16)
        return carry

    lax.fori_loop(0, nck, score_body, 0)

    def count(pred_fn):
        def body(c, part):
            ind = jnp.where(pred_fn(c, skey_ref[c]), 1, 0)
            return part + jnp.sum(ind.reshape(groups, SUBLANES, qb), axis=0)
        part = lax.fori_loop(0, nck, body, jnp.zeros((SUBLANES, qb), I32))
        return jnp.sum(part, axis=0, keepdims=True)

    def count16(ref, pred_fn):
        pack = 2 * SUBLANES

        def body(c, part):
            ind = jnp.where(pred_fn(ref[c]), jnp.bfloat16(1), jnp.bfloat16(0)).reshape(kc // pack, pack, qb)
            terms = [ind[g] for g in range(kc // pack)]
            while len(terms) > 1:
                terms = [terms[g] + terms[g + 1] for g in range(0, len(terms), 2)]
            return part + terms[0].astype(F32)
        part = lax.fori_loop(0, nck, body, jnp.zeros((pack, qb), F32))
        return jnp.sum(part, axis=0, keepdims=True).astype(I32)

    def search16(ref, target):
        def body(it, t_u):
            cand_u = t_u | (jnp.int32(1) << (15 - it))
            cand = (cand_u - 2 ** 15).astype(I16)
            cnt = count16(ref, lambda k: k >= cand)
            return jnp.where(cnt >= target, cand_u, t_u)
        return lax.fori_loop(0, 16, body, jnp.zeros((1, qb), I32))

    hi_u = search16(hi16_ref, topk)
    hi_s = (hi_u - 2 ** 15).astype(I16)
    rest = topk - count16(hi16_ref, lambda k: k > hi_s)

    def low_body(c, carry):
        low = ((skey_ref[c] & 0xFFFF) - 2 ** 15).astype(I16)
        lo16_ref[c] = jnp.where(hi16_ref[c] == hi_s, low, jnp.int16(-2 ** 15))
        return carry

    lax.fori_loop(0, nck, low_body, 0)
    lo_u = search16(lo16_ref, rest)
    thr = (hi_u - 2 ** 15) * 2 ** 16 + lo_u
    cnt_ge = count(lambda c, sk: sk >= thr)
    cnt_gt = count(lambda c, sk: sk > thr)

    jb_ref[...] = jnp.full(jb_ref.shape, 2 ** 31 - 1, I32)

    @pl.when(jnp.max(cnt_ge) > topk)
    def _():
        need = topk - cnt_gt

        def tie_body(it, jb):
            cand = jb | (jnp.int32(1) << (idx_bits - 1 - it))
            cnt = count(lambda c, sk: jnp.logical_and(sk == thr, c * kc + rowi < cand))
            return jnp.where(cnt <= need, cand, jb)

        jb = lax.fori_loop(0, idx_bits, tie_body, jnp.zeros((1, qb), I32))
        jb_ref[...] = jnp.broadcast_to(jb, jb_ref.shape)

    jbound = jnp.where(thr == jnp.int32(INT_MIN), 0, jb_ref[0:1, :])
    thr_m1 = thr - 1

    for h in range(N_HEADS):
        l_refs[h][...] = jnp.zeros((1, qb), F32)
        acc_refs[h][...] = jnp.zeros((HEAD_DIM, qb), F32)
    n_slabs = kc // SLAB

    def fold(x, op):
        return op(x.reshape(SLAB // SUBLANES, SUBLANES, qb), axis=0)

    def near_bias(c, h):
        tiles = []
        for t in range(kc // LANES):
            row = []
            for u in range(qb // LANES):
                d_tu = c * kc + t * LANES - (qoff + u * LANES)
                delta = jnp.zeros((LANES, LANES), F32)
                for di, off in enumerate(offsets):
                    delta = jnp.where(d_tu == off, tab_ref[di, h], delta)
                row.append(delta)
            tiles.append(jnp.concatenate(row, axis=1))
        return jnp.concatenate(tiles, axis=0)

    def attn_chunk(c, near):
        sk = skey_ref[c]
        t_eff = jnp.where(rowi < jbound - c * kc, thr_m1, thr)
        madd_ref[...] = jnp.where(sk > t_eff, 0.0, MASK_VALUE)
        maccs = []
        for h in range(N_HEADS):
            x = _dot(kb_ref[0, c, h // GROUP], qT_ref[0, h]) + madd_ref[...]
            if near:
                x = x + near_bias(c, h)
            s_refs[h][...] = x
            maccs.append(jnp.max(x.reshape(groups, SUBLANES, qb), axis=0))
        alphas = []
        for h in range(N_HEADS):
            s_ref, p_ref, m_ref = s_refs[h], p_refs[h], m_refs[h]
            m_old = m_ref[...]
            m_new = jnp.maximum(m_old, jnp.max(maccs[h], axis=0, keepdims=True))
            for j in range(n_slabs):
                rows = pl.ds(j * SLAB, SLAB)
                p_ref[rows, :] = jnp.exp2(s_ref[rows, :] - m_new).astype(BF16)
            m_ref[...] = m_new
            alphas.append(jnp.exp2(m_old - m_new))
        ones = jnp.ones((2 * SUBLANES, kc), BF16)
        v_aug = [jnp.concatenate([vT_ref[0, c, n * HEAD_DIM:(n + 1) * HEAD_DIM, :], ones], axis=0)
                 for n in range(N_KV_HEADS)]
        for h in range(N_HEADS):
            pv = _dot(v_aug[h // GROUP], p_refs[h][...])
            acc_refs[h][...] = alphas[h] * acc_refs[h][...] + pv[0:HEAD_DIM]
            l_refs[h][...] = alphas[h] * l_refs[h][...] + pv[HEAD_DIM:HEAD_DIM + 1]

    def attn_chunk_fixed(c, near):
        sk = skey_ref[c]
        t_eff = jnp.where(rowi < jbound - c * kc, thr_m1, thr)
        madd_ref[...] = jnp.where(sk > t_eff, 0.0, MASK_VALUE)
        ones = jnp.ones((2 * SUBLANES, kc), BF16)
        v_aug = [jnp.concatenate([vT_ref[0, c, n * HEAD_DIM:(n + 1) * HEAD_DIM, :], ones], axis=0)
                 for n in range(N_KV_HEADS)]
        for h in range(N_HEADS):
            x = _dot(kb_ref[0, c, h // GROUP], qT_ref[0, h]) + madd_ref[...]
            if near:
                x = x + near_bias(c, h)
            p_refs[h][...] = jnp.exp2(x).astype(BF16)
        for h in range(N_HEADS):
            pv = _dot(v_aug[h // GROUP], p_refs[h][...])
            acc_refs[h][...] += pv[0:HEAD_DIM]
            l_refs[h][...] += pv[HEAD_DIM:HEAD_DIM + 1]

    def run(chunk_fn):
        lax.fori_loop(0, n_far, lambda c, carry: (chunk_fn(c, False), carry)[1], 0)
        lax.fori_loop(n_far, nck, lambda c, carry: (chunk_fn(c, True), carry)[1], 0)

    q_sq = jnp.zeros((1, qb), F32)
    for h in range(N_HEADS):
        qf = qT_ref[0, h].astype(F32)
        q_sq = jnp.maximum(q_sq, jnp.sum(qf * qf, axis=0, keepdims=True))
    bound = jnp.sqrt(jnp.max(q_sq) * jnp.max(knorm_ref[0])) + jnp.max(jnp.abs(tab_ref[...]))
    fixed_shift = bound < EXP2_SAFE

    @pl.when(fixed_shift)
    def _():
        run(attn_chunk_fixed)

    @pl.when(jnp.logical_not(fixed_shift))
    def _():
        for h in range(N_HEADS):
            m_refs[h][...] = jnp.full((1, qb), MASK_VALUE, F32)
        run(attn_chunk)

    oT = jnp.concatenate([acc_refs[h][...] / l_refs[h][...] for h in range(N_HEADS)], axis=0)
    o_ref[0] = oT.T.astype(o_ref.dtype)


def _attend(qT, qiT, wT, kidxb, kb, vT, knorm, tabs, *, qb, past, l_true, offsets):
    B, _, _, S = qT.shape
    nc, kc = vT.shape[1], vT.shape[3]
    assert S % qb == 0 and qb % LANES == 0 and past % LANES == 0 and kc % LANES == 0
    topk = min(TOPK_MAX, l_true // 4)
    kern = functools.partial(_attn_kernel, qb=qb, kc=kc, past=past, l_true=l_true,
                             topk=topk, offsets=offsets)
    in_specs = [
        pl.BlockSpec((1, N_HEADS, HEAD_DIM, qb), lambda b, i: (b, 0, 0, i)),
        pl.BlockSpec((1, N_IDX_HEADS, IDX_DIM, qb), lambda b, i: (b, 0, 0, i)),
        pl.BlockSpec((1, N_IDX_HEADS, qb), lambda b, i: (b, 0, i)),
        pl.BlockSpec((1,) + kidxb.shape[1:], lambda b, i: (b, 0, 0, 0)),
        pl.BlockSpec((1,) + kb.shape[1:], lambda b, i: (b, 0, 0, 0, 0)),
        pl.BlockSpec((1,) + vT.shape[1:], lambda b, i: (b, 0, 0, 0)),
        pl.BlockSpec((1,) + knorm.shape[1:], lambda b, i: (b, 0, 0, 0)),
        pl.BlockSpec(tabs.shape, lambda b, i: (0, 0, 0, 0)),
    ]
    return pl.pallas_call(
        kern,
        grid=(B, S // qb),
        in_specs=in_specs,
        out_specs=pl.BlockSpec((1, qb, N_HEADS * HEAD_DIM), lambda b, i: (b, i, 0)),
        out_shape=jax.ShapeDtypeStruct((B, S, N_HEADS * HEAD_DIM), BF16),
        scratch_shapes=[
            pltpu.VMEM((nc, kc, qb), I32),
            pltpu.VMEM((nc, kc, qb), I16),
            pltpu.VMEM((nc, kc, qb), I16),
            pltpu.VMEM((SUBLANES, qb), I32),
            pltpu.VMEM((kc, qb), F32),
        ] + [pltpu.VMEM((kc, qb), F32)] * N_HEADS
          + [pltpu.VMEM((kc, qb), BF16)] * N_HEADS
          + [pltpu.VMEM((1, qb), F32)] * N_HEADS
          + [pltpu.VMEM((1, qb), F32)] * N_HEADS
          + [pltpu.VMEM((HEAD_DIM, qb), F32)] * N_HEADS,
        compiler_params=_cparams(("arbitrary", "arbitrary")),
        name="dsa_attend",
    )(qT, qiT, wT, kidxb, kb, vT, knorm, tabs)


def _mix_kernel(x_ref, attn_ref, conv_ref, wo_ref, g_ref, b_ref, h_ref, *, dn_alpha):
    cat = jnp.concatenate([attn_ref[...], conv_ref[...]], axis=1)
    mixed = _dot(cat, wo_ref[...])
    h_ref[...] = _layer_norm(dn_alpha * x_ref[...] + mixed, g_ref[...], b_ref[...])


def _mix(x2, attn2, conv2, wo, g, b, *, dn_alpha):
    T, D = x2.shape
    tt = min(1024, T)
    assert T % tt == 0
    aw, cw = attn2.shape[1], conv2.shape[1]
    row = lambda i: (i, 0)
    const = lambda i: (0, 0)
    return pl.pallas_call(
        functools.partial(_mix_kernel, dn_alpha=dn_alpha),
        grid=(T // tt,),
        in_specs=[pl.BlockSpec((tt, D), row), pl.BlockSpec((tt, aw), row), pl.BlockSpec((tt, cw), row),
                  pl.BlockSpec(wo.shape, const), pl.BlockSpec((1, D), const), pl.BlockSpec((1, D), const)],
        out_specs=pl.BlockSpec((tt, D), row),
        out_shape=jax.ShapeDtypeStruct((T, D), F32),
        compiler_params=_cparams(("arbitrary",)),
        name="outproj_ln1",
    )(x2, attn2, conv2, wo, g, b)


def _route_picks(logits):
    lane = lax.broadcasted_iota(I32, logits.shape, 1)
    work = logits
    vals, hots = [], []
    for _ in range(TOP_K):
        m = jnp.max(work, axis=1, keepdims=True)
        idx = jnp.min(jnp.where(work == m, lane, LANES), axis=1, keepdims=True)
        hot = lane == idx
        vals.append(m)
        hots.append(hot)
        work = jnp.where(hot, -jnp.inf, work)
    es = [jnp.exp(v - vals[0]) for v in vals]
    den = es[0]
    for e in es[1:]:
        den = den + e
    return hots, [e / den for e in es]


def _route(logits):
    hots, gates = _route_picks(logits)
    comb = jnp.zeros(logits.shape, F32)
    picked = jnp.zeros(logits.shape, F32)
    for gate, hot in zip(gates, hots):
        comb = jnp.where(hot, gate, comb)
        picked = jnp.where(hot, 1.0, picked)
    return comb, picked


def _moe_kernel(h_ref, wr_ref, br_ref, *rest, dn_alpha):
    wgu_refs, rest = rest[:WEIGHT_SPLIT], rest[WEIGHT_SPLIT:]
    bgu_ref, rest = rest[0], rest[1:]
    wd_refs, rest = rest[:WEIGHT_SPLIT], rest[WEIGHT_SPLIT:]
    bd_ref, g_ref, b_ref, y_ref, hb_ref, comb_ref = rest
    e = pl.program_id(1)
    tt = h_ref.shape[0]
    ff = wd_refs[0].shape[1]

    @pl.when(e == 0)
    def _():
        hb = h_ref[...].astype(BF16)
        hb_ref[...] = hb
        comb_ref[...] = _route(_dot(hb, wr_ref[...]) + br_ref[...])[0]
        y_ref[...] = jnp.zeros(y_ref.shape, F32)

    hb = hb_ref[...]
    gu = jnp.concatenate([_dot(hb, w[0]) for w in wgu_refs], axis=1) + bgu_ref[0]
    gate = jnp.minimum(gu[:, 0:ff], SWIGLU_LIMIT)
    up = jnp.clip(gu[:, ff:2 * ff], -SWIGLU_LIMIT, SWIGLU_LIMIT)
    act = ((up + 1.0) * (gate * (1.0 / (1.0 + jnp.exp(-SWIGLU_ALPHA * gate))))).astype(BF16)
    y = jnp.concatenate([_dot(act, w[0]) for w in wd_refs], axis=1) + bd_ref[0]
    lane = lax.broadcasted_iota(I32, (tt, LANES), 1)
    y_ref[...] += jnp.sum(jnp.where(lane == e, comb_ref[...], 0.0), axis=1, keepdims=True) * y

    @pl.when(e == pl.num_programs(1) - 1)
    def _():
        y_ref[...] = _layer_norm(dn_alpha * h_ref[...] + y_ref[...], g_ref[...], b_ref[...])


def _moe(h2, wr, br, wgu, bgu, wd, bd, g, b, *, dn_alpha):
    T, D = h2.shape
    E, _, F2 = wgu.shape
    tt = min(512, T)
    assert T % tt == 0
    row = lambda i, e: (i, 0)
    const = lambda i, e: (0, 0)
    exp3 = lambda i, e: (e, 0, 0)
    ns = WEIGHT_SPLIT
    col = lambda j: (lambda i, e: (e, 0, j))
    return pl.pallas_call(
        functools.partial(_moe_kernel, dn_alpha=dn_alpha),
        grid=(T // tt, E),
        in_specs=[pl.BlockSpec((tt, D), row), pl.BlockSpec(wr.shape, const), pl.BlockSpec(br.shape, const)]
                 + [pl.BlockSpec((1, D, F2 // ns), col(j)) for j in range(ns)]
                 + [pl.BlockSpec((1, 1, F2), exp3)]
                 + [pl.BlockSpec((1, F2 // 2, D // ns), col(j)) for j in range(ns)]
                 + [pl.BlockSpec((1, 1, D), exp3),
                    pl.BlockSpec((1, D), const), pl.BlockSpec((1, D), const)],
        out_specs=pl.BlockSpec((tt, D), row),
        out_shape=jax.ShapeDtypeStruct((T, D), F32),
        scratch_shapes=[pltpu.VMEM((tt, D), BF16), pltpu.VMEM((tt, LANES), F32)],
        compiler_params=_cparams(("arbitrary", "arbitrary")),
        name="moe_ln2",
    )(h2, wr, br, *([wgu] * ns), bgu, *([wd] * ns), bd, g, b)


def _pack_pairs(x):
    w = x.shape[1] // 2
    bits = pltpu.bitcast(x.astype(BF16).astype(F32), I32)
    return (bits[:, 0:w] & jnp.int32(-65536)) | lax.shift_right_logical(bits[:, w:2 * w], 16)


def _unpack_pairs(p):
    hi = pltpu.bitcast(p & jnp.int32(-65536), F32)
    lo = pltpu.bitcast(lax.shift_left(p, 16), F32)
    return jnp.concatenate([hi, lo], axis=1)


def _mix_route_kernel(x_ref, attn_ref, conv_ref, wo_ref, g_ref, b_ref, wrT_ref, brT_ref,
                      h_ref, hpk_ref, infoT_ref, gcol_ref, cnt_ref, *, dn_alpha):
    cat = jnp.concatenate([attn_ref[...], conv_ref[...]], axis=1)
    h = _layer_norm(dn_alpha * x_ref[...] + _dot(cat, wo_ref[...]), g_ref[...], b_ref[...])
    h_ref[...] = h
    tt = h.shape[0]
    ne = cnt_ref.shape[1]
    hpk_ref[...] = _pack_pairs(h)
    logits = lax.dot_general(wrT_ref[...], h.astype(BF16), (((1,), (1,)), ((), ())),
                             preferred_element_type=F32)
    work = logits[0:ne] + jnp.tile(brT_ref[0:ne, :], (1, tt // LANES))
    sub = lax.broadcasted_iota(I32, (ne, tt), 0)
    vals, idxs, hots = [], [], []
    for _ in range(TOP_K):
        m = jnp.max(work, axis=0, keepdims=True)
        idx = jnp.min(jnp.where(work == m, sub, ne), axis=0, keepdims=True)
        hot = sub == idx
        vals.append(m)
        idxs.append(idx)
        hots.append(hot)
        work = jnp.where(hot, -jnp.inf, work)
    es = [jnp.exp(v - vals[0]) for v in vals]
    den = es[0]
    for e in es[1:]:
        den = den + e
    gates = [e / den for e in es]
    picked = jnp.zeros((ne, tt), F32)
    for hot in hots:
        picked = jnp.where(hot, 1.0, picked)
    earlier = lax.broadcasted_iota(I32, (tt, tt), 0) < lax.broadcasted_iota(I32, (tt, tt), 1)
    rank = _dot(picked.astype(BF16), jnp.where(earlier, 1.0, 0.0).astype(BF16))
    ranks = [jnp.sum(jnp.where(hot, rank, 0.0), axis=0, keepdims=True) for hot in hots]
    pad = jnp.zeros((2 * SUBLANES - 3 * TOP_K, tt), F32)
    infoT_ref[...] = jnp.concatenate([i.astype(F32) for i in idxs] + gates + ranks + [pad], axis=0)
    gcol_ref[...] = jnp.concatenate(gates + [jnp.zeros((LANES - TOP_K, tt), F32)], axis=0).T
    cnt_ref[0] = jnp.broadcast_to(jnp.sum(picked, axis=1, keepdims=True), (ne, LANES))


def _mix_route(x2, attn2, conv2, wo, g, b, wr, br, ne, *, dn_alpha):
    T, D = x2.shape
    tt = ROUTE_TILE
    aw, cw = attn2.shape[1], conv2.shape[1]
    row = lambda i: (i, 0)
    const = lambda i: (0, 0)
    wrT = wr.T
    brT = jnp.broadcast_to(br.reshape(LANES, 1), (LANES, LANES))
    return pl.pallas_call(
        functools.partial(_mix_route_kernel, dn_alpha=dn_alpha),
        grid=(T // tt,),
        in_specs=[pl.BlockSpec((tt, D), row), pl.BlockSpec((tt, aw), row), pl.BlockSpec((tt, cw), row),
                  pl.BlockSpec(wo.shape, const), pl.BlockSpec((1, D), const), pl.BlockSpec((1, D), const),
                  pl.BlockSpec(wrT.shape, const), pl.BlockSpec(brT.shape, const)],
        out_specs=(pl.BlockSpec((tt, D), row), pl.BlockSpec((tt, D // 2), row),
                   pl.BlockSpec((2 * SUBLANES, tt), lambda i: (0, i)),
                   pl.BlockSpec((tt, LANES), row), pl.BlockSpec((1, ne, LANES), lambda i: (i, 0, 0))),
        out_shape=(jax.ShapeDtypeStruct((T, D), F32), jax.ShapeDtypeStruct((T, D // 2), I32),
                   jax.ShapeDtypeStruct((2 * SUBLANES, T), F32),
                   jax.ShapeDtypeStruct((T, LANES), F32), jax.ShapeDtypeStruct((T // tt, ne, LANES), F32)),
        compiler_params=_cparams(("arbitrary",)),
        name="outproj_ln1_route",
    )(x2, attn2, conv2, wo, g, b, wrT, brT)


def _sc_gather_rows(table, idx):
    M, W = idx.shape[0], table.shape[1]
    n_workers = SC_CORES * SC_SUBCORES
    per_w = M // n_workers
    n_ch = per_w // SC_ROWS
    assert M == n_workers * n_ch * SC_ROWS
    mesh = plsc.VectorSubcoreMesh(core_axis_name="c", subcore_axis_name="s")

    @functools.partial(
        pl.kernel, mesh=mesh,
        out_type=jax.ShapeDtypeStruct((M, W), table.dtype),
        scratch_types=[pltpu.VMEM((n_ch, SC_ROWS), I32), pltpu.VMEM((SC_ROWS, W), table.dtype),
                       pltpu.SemaphoreType.DMA],
    )
    def gather(table_hbm, idx_hbm, out_hbm, idx_v, rows_v, sem):
        wid = lax.axis_index("s") * SC_CORES + lax.axis_index("c")
        pltpu.sync_copy(idx_hbm.at[wid], idx_v)

        @pl.loop(0, n_ch)
        def _(j):
            pltpu.async_copy(table_hbm.at[idx_v.at[j]], rows_v, sem).wait()
            pltpu.sync_copy(rows_v, out_hbm.at[pl.ds(wid * per_w + j * SC_ROWS, SC_ROWS)])

    return gather(table, idx.reshape(n_workers, n_ch, SC_ROWS))


def _expert_rows_kernel(te_ref, used_ref, x_ref, *rest):
    wgu_refs, rest = rest[:WEIGHT_SPLIT], rest[WEIGHT_SPLIT:]
    bgu_ref, rest = rest[0], rest[1:]
    wd_refs, rest = rest[:WEIGHT_SPLIT], rest[WEIGHT_SPLIT:]
    bd_ref, y_ref = rest
    ff = wd_refs[0].shape[1]

    @pl.when(pl.program_id(0) < used_ref[0])
    def _():
        xg = _unpack_pairs(x_ref[...]).astype(BF16)
        gu = jnp.concatenate([_dot(xg, w[0]) for w in wgu_refs], axis=1) + bgu_ref[0]
        gate = jnp.minimum(gu[:, 0:ff], SWIGLU_LIMIT)
        up = jnp.clip(gu[:, ff:2 * ff], -SWIGLU_LIMIT, SWIGLU_LIMIT)
        act = ((up + 1.0) * (gate * (1.0 / (1.0 + jnp.exp(-SWIGLU_ALPHA * gate))))).astype(BF16)
        y_ref[...] = _pack_pairs(jnp.concatenate([_dot(act, w[0]) for w in wd_refs], axis=1) + bd_ref[0])


def _expert_rows(tile_expert, used_tiles, xs, wgu, bgu, wd, bd):
    R, W = xs.shape
    E, D, F2 = wgu.shape
    ns = WEIGHT_SPLIT
    col = lambda c: (lambda j, te, used: (te[j], 0, c))
    exp3 = lambda j, te, used: (te[j], 0, 0)
    rows = lambda j, te, used: (j, 0)
    grid_spec = pltpu.PrefetchScalarGridSpec(
        num_scalar_prefetch=2,
        grid=(R // ROW_TILE,),
        in_specs=[pl.BlockSpec((ROW_TILE, W), rows)]
                 + [pl.BlockSpec((1, D, F2 // ns), col(c)) for c in range(ns)]
                 + [pl.BlockSpec((1, 1, F2), exp3)]
                 + [pl.BlockSpec((1, F2 // 2, D // ns), col(c)) for c in range(ns)]
                 + [pl.BlockSpec((1, 1, D), exp3)],
        out_specs=pl.BlockSpec((ROW_TILE, W), rows),
    )
    return pl.pallas_call(
        _expert_rows_kernel,
        grid_spec=grid_spec,
        out_shape=jax.ShapeDtypeStruct((R, W), I32),
        compiler_params=_cparams(("arbitrary",)),
        name="moe_expert_rows",
    )(tile_expert, used_tiles, xs, *([wgu] * ns), bgu, *([wd] * ns), bd)


def _combine_kernel(h_ref, yg_ref, gcol_ref, g_ref, b_ref, o_ref, *, dn_alpha):
    acc = dn_alpha * h_ref[...]
    for k in range(TOP_K):
        acc = acc + gcol_ref[:, k:k + 1] * _unpack_pairs(yg_ref[k])
    o_ref[...] = _layer_norm(acc, g_ref[...], b_ref[...])


def _combine(h2, yg, info, g, b, *, dn_alpha):
    T, D = h2.shape
    tt = 512
    row = lambda i: (i, 0)
    const = lambda i: (0, 0)
    return pl.pallas_call(
        functools.partial(_combine_kernel, dn_alpha=dn_alpha),
        grid=(T // tt,),
        in_specs=[pl.BlockSpec((tt, D), row), pl.BlockSpec((TOP_K, tt, D // 2), lambda i: (0, i, 0)),
                  pl.BlockSpec((tt, LANES), row),
                  pl.BlockSpec((1, D), const), pl.BlockSpec((1, D), const)],
        out_specs=pl.BlockSpec((tt, D), row),
        out_shape=jax.ShapeDtypeStruct((T, D), F32),
        compiler_params=_cparams(("arbitrary",)),
        name="moe_combine_ln2",
    )(h2, yg, info, g, b)


def _dest_kernel(infoT_ref, base_ref, dest_ref):
    tt = infoT_ref.shape[1]
    ne = base_ref.shape[1]
    base = jnp.tile(base_ref[0], (1, tt // LANES))
    sub = lax.broadcasted_iota(I32, (ne, tt), 0)
    rows = []
    for k in range(TOP_K):
        hot = sub == infoT_ref[k:k + 1, :].astype(I32)
        rows.append(jnp.sum(jnp.where(hot, base, 0.0), axis=0, keepdims=True)
                    + infoT_ref[2 * TOP_K + k:2 * TOP_K + k + 1, :])
    rows.append(jnp.zeros((SUBLANES - TOP_K, tt), F32))
    dest_ref[...] = jnp.concatenate(rows, axis=0).astype(I32)


def _dest_rows(infoT, base):
    T = infoT.shape[1]
    tt = ROUTE_TILE
    ne = base.shape[1]
    return pl.pallas_call(
        _dest_kernel,
        grid=(T // tt,),
        in_specs=[pl.BlockSpec((2 * SUBLANES, tt), lambda i: (0, i)), pl.BlockSpec((1, ne, LANES), lambda i: (i, 0, 0))],
        out_specs=pl.BlockSpec((SUBLANES, tt), lambda i: (0, i)),
        out_shape=jax.ShapeDtypeStruct((SUBLANES, T), I32),
        compiler_params=_cparams(("arbitrary",)),
        name="moe_dest_rows",
    )(infoT, base)


def _sc_scatter_rows(src, dest_km, n_rows):
    T, W = src.shape
    K = dest_km.shape[0]
    n_workers = SC_CORES * SC_SUBCORES
    per_w = T // n_workers
    n_ch = per_w // SC_ROWS
    assert T == n_workers * n_ch * SC_ROWS
    mesh = plsc.VectorSubcoreMesh(core_axis_name="c", subcore_axis_name="s")
    idx = dest_km.reshape(K, n_workers, n_ch, SC_ROWS).transpose(1, 0, 2, 3).reshape(n_workers, K * n_ch, SC_ROWS)

    @functools.partial(
        pl.kernel, mesh=mesh,
        out_type=jax.ShapeDtypeStruct((n_rows, W), src.dtype),
        scratch_types=[pltpu.VMEM((K * n_ch, SC_ROWS), I32), pltpu.VMEM((SC_ROWS, W), src.dtype),
                       pltpu.SemaphoreType.DMA],
    )
    def scatter(src_hbm, idx_hbm, out_hbm, idx_v, rows_v, sem):
        wid = lax.axis_index("s") * SC_CORES + lax.axis_index("c")
        pltpu.sync_copy(idx_hbm.at[wid], idx_v)

        @pl.loop(0, n_ch)
        def _(j):
            pltpu.sync_copy(src_hbm.at[pl.ds(wid * per_w + j * SC_ROWS, SC_ROWS)], rows_v)
            for k in range(K):
                pltpu.async_copy(rows_v, out_hbm.at[idx_v.at[k * n_ch + j]], sem).wait()

    return scatter(src, idx)


def _moe_sorted(h2, routed, wgu, bgu, wd, bd, g, b, *, dn_alpha):
    T, D = h2.shape
    E = wgu.shape[0]
    hpk, infoT, gcol, cnt = routed
    cnt = cnt[:, :, 0].astype(I32)
    total = jnp.sum(cnt, axis=0)
    padded = -(-total // ROW_TILE) * ROW_TILE
    ends = jnp.cumsum(padded)
    base = (ends - padded)[None, :] + jnp.cumsum(cnt, axis=0) - cnt
    base = jnp.broadcast_to(base.astype(F32)[:, :, None], cnt.shape + (LANES,))
    dest_km = _dest_rows(infoT, base)[0:TOP_K]
    R = T * TOP_K + E * ROW_TILE
    tile_start = jnp.arange(R // ROW_TILE, dtype=I32) * ROW_TILE
    tile_expert = jnp.minimum(jnp.sum((ends[None, :] <= tile_start[:, None]).astype(I32), axis=1), E - 1)
    used_tiles = (ends[E - 1] // ROW_TILE).reshape(1)
    xs = _sc_scatter_rows(hpk, dest_km, R)
    ys = _expert_rows(tile_expert, used_tiles, xs, wgu, bgu, wd, bd)
    yg = _sc_gather_rows(ys, dest_km.reshape(-1))
    return _combine(h2, yg.reshape(TOP_K, T, D // 2), gcol, g, b, dn_alpha=dn_alpha)


def _split_gu_kernel(w_ref, perm_ref, o_ref, *, band):
    n = w_ref.shape[2]
    for m in range(n // band):
        both = _dot(w_ref[0, :, m * band:(m + 1) * band].astype(BF16), perm_ref[...]).astype(BF16)
        o_ref[0, :, m * (band // 2):(m + 1) * (band // 2)] = both[:, 0:band // 2]
        o_ref[0, :, n // 2 + m * (band // 2):n // 2 + (m + 1) * (band // 2)] = both[:, band // 2:band]


def _split_gu(w_gu):
    E, D, N = w_gu.shape
    tr = min(512, D)
    band = min(512, N)
    j = jnp.arange(band, dtype=I32)
    src = jnp.where(j < band // 2, 2 * j, 2 * (j - band // 2) + 1)
    perm = (jnp.arange(band, dtype=I32)[:, None] == src[None, :]).astype(BF16)
    return pl.pallas_call(
        functools.partial(_split_gu_kernel, band=band),
        grid=(E, D // tr),
        in_specs=[pl.BlockSpec((1, tr, N), lambda e, r: (e, r, 0)),
                  pl.BlockSpec((band, band), lambda e, r: (0, 0))],
        out_specs=pl.BlockSpec((1, tr, N), lambda e, r: (e, r, 0)),
        out_shape=jax.ShapeDtypeStruct((E, D, N), BF16),
        compiler_params=_cparams(("arbitrary", "arbitrary")),
        name="split_gate_up",
    )(w_gu, perm)


def _prep_weights(w_in, w_conv, w_out, ln1_g, ln1_b, w_router, b_router, w_gu, b_gu, w_down, b_down,
                  ln2_g, ln2_b):
    aw = N_HEADS * HEAD_DIM
    kv = N_KV_HEADS * HEAD_DIM
    iw = N_IDX_HEADS * IDX_DIM
    D = w_in.shape[0]
    cw = D - aw
    o_q, o_k, o_v = 0, aw, aw + kv
    o_qi = o_v + kv
    o_ki = o_qi + iw
    o_wi = o_ki + IDX_DIM
    o_c = o_wi + N_IDX_HEADS
    wa = jnp.concatenate([w_in[:, o_q:o_k], w_in[:, o_qi:o_ki], w_in[:, o_k:o_v], w_in[:, o_v:o_qi]],
                         axis=1).astype(BF16)
    ws = jnp.pad(w_in[:, o_ki:o_c], ((0, 0), (0, LANES - IDX_DIM - N_IDX_HEADS))).astype(BF16)
    wc = w_in[:, o_c:o_c + 3 * cw].astype(BF16)
    E = w_router.shape[1]
    wr = jnp.pad(w_router, ((0, 0), (0, LANES - E))).astype(BF16)
    br = jnp.pad(b_router, (0, LANES - E), constant_values=MASK_VALUE).reshape(1, LANES)
    F = w_gu.shape[2] // 2
    return dict(
        wa=wa, ws=ws, wc=wc, wconv=w_conv, wo=w_out.astype(BF16),
        ln1_g=ln1_g.reshape(1, D), ln1_b=ln1_b.reshape(1, D),
        wr=wr, br=br,
        wgu=_split_gu(w_gu),
        bgu=jnp.concatenate([b_gu[:, 0::2], b_gu[:, 1::2]], axis=1).reshape(E, 1, 2 * F),
        wd=w_down.astype(BF16), bd=b_down.reshape(E, 1, D),
        ln2_g=ln2_g.reshape(1, D), ln2_b=ln2_b.reshape(1, D),
    )


def _decode_layouts(pq, small, k_all, v_all, kidx_all, kc, qb):
    B, S, _ = pq.shape
    aw = N_HEADS * HEAD_DIM
    L, kv = k_all.shape[1], k_all.shape[2]
    lp = -(-L // kc) * kc
    nc = lp // kc
    padk = lambda a: jnp.pad(a, ((0, 0), (0, lp - L), (0, 0)))
    padq = lambda a: jnp.pad(a, ((0, 0), (0, 0), (0, 0), (0, qb - S)))
    q = pq[:, :, 0:aw] * QK_SCALE
    qT = padq(q.reshape(B, S, N_HEADS, HEAD_DIM).transpose(0, 2, 3, 1)).astype(BF16)
    qiT = padq(pq[:, :, aw:].reshape(B, S, N_IDX_HEADS, IDX_DIM).transpose(0, 2, 3, 1)).astype(BF16)
    wT = jnp.pad(small[:, :, IDX_DIM:IDX_DIM + N_IDX_HEADS].transpose(0, 2, 1), ((0, 0), (0, 0), (0, qb - S)))
    kb = padk(k_all).reshape(B, nc, kc, N_KV_HEADS, HEAD_DIM).transpose(0, 1, 3, 2, 4).astype(BF16)
    vT = padk(v_all).reshape(B, nc, kc, kv).transpose(0, 1, 3, 2).astype(BF16)
    kidxb = padk(kidx_all).reshape(B, nc, kc, IDX_DIM).astype(BF16)
    knorm = jnp.sum(kb.astype(F32) ** 2, axis=-1)
    return qT, qiT, wT, kidxb, kb, vT, knorm


def _layer(x, past_k, past_v, past_kidx, conv_buf, rel_bias, w, *, dn_alpha):
    B, S, D = x.shape
    prefill = past_k is None
    outs = _project(x, conv_buf, w["wa"], w["ws"], w["wc"], w["wconv"], attn_layouts=prefill)
    k, v, small, conv, new_buf = outs[:5]
    k_idx = small[:, :, 0:IDX_DIM]
    if prefill:
        past, l_true = 0, S
        qb = min(QUERY_BLOCK, S)
        kb, vT, kidxb, qT, qiT, wT, knorm = outs[5:]
    else:
        past = past_k.shape[1]
        l_true = past + S
        qb = -(-S // LANES) * LANES
        qT, qiT, wT, kidxb, kb, vT, knorm = _decode_layouts(
            outs[5], small, jnp.concatenate([past_k, k], axis=1), jnp.concatenate([past_v, v], axis=1),
            jnp.concatenate([past_kidx, k_idx], axis=1), KEY_CHUNK, qb)
    offsets = (-LANES, 0)
    tabs = _bias_tables(rel_bias, offsets)
    attn = _attend(qT, qiT, wT, kidxb, kb, vT, knorm, tabs, qb=qb, past=past, l_true=l_true, offsets=offsets)
    attn = attn[:, :S]
    mix_args = (x.reshape(B * S, D), attn.reshape(B * S, -1), conv.reshape(B * S, -1), w["wo"],
                w["ln1_g"], w["ln1_b"])
    experts = (w["wgu"], w["bgu"], w["wd"], w["bd"], w["ln2_g"], w["ln2_b"])
    sc_unit = SC_CORES * SC_SUBCORES * SC_ROWS
    sortable = ((B * S) % ROUTE_TILE == 0 and (B * S * TOP_K) % sc_unit == 0
                and (N_EXPERTS * ROW_TILE) % sc_unit == 0)
    if sortable and B * S >= SORTED_MIN_TOKENS:
        h, *routed = _mix_route(*mix_args, w["wr"], w["br"], N_EXPERTS, dn_alpha=dn_alpha)
        y = _moe_sorted(h, routed, *experts, dn_alpha=dn_alpha)
    else:
        h = _mix(*mix_args, dn_alpha=dn_alpha)
        y = _moe(h, w["wr"], w["br"], *experts, dn_alpha=dn_alpha)
    return (y.reshape(B, S, D), k.reshape(B, S, N_KV_HEADS, HEAD_DIM), v.reshape(B, S, N_KV_HEADS, HEAD_DIM),
            k_idx, new_buf)


def kernel(x_prompt, x_sample, cache_k, cache_v, cache_kidx, state_conv, rel_bias, w_in, w_conv, w_out,
           ln1_g, ln1_b, w_router, b_router, w_gu, b_gu, w_down, b_down, ln2_g, ln2_b):
    depth = w_in.shape[0]
    assert depth == 1
    dn_alpha = (2 * depth) ** 0.25
    kv = N_KV_HEADS * HEAD_DIM
    w = _prep_weights(w_in[0], w_conv[0], w_out[0], ln1_g[0], ln1_b[0], w_router[0], b_router[0],
                      w_gu[0], b_gu[0], w_down[0], b_down[0], ln2_g[0], ln2_b[0])
    Bp = x_prompt.shape[0]
    cw = w_conv.shape[2]
    zero_buf = jnp.zeros((Bp, CONV_K - 1, cw), F32)
    yp, k1, v1, i1, c1 = _layer(x_prompt, None, None, None, zero_buf, rel_bias, w, dn_alpha=dn_alpha)
    Bs, P = cache_k.shape[1], cache_k.shape[2]
    ys, k2, v2, i2, c2 = _layer(x_sample, cache_k[0].reshape(Bs, P, kv), cache_v[0].reshape(Bs, P, kv),
                                cache_kidx[0], state_conv[0], rel_bias, w, dn_alpha=dn_alpha)
    return (yp, ys, k1[None], v1[None], i1[None], c1[None], k2[None], v2[None], i2[None], c2[None])
```

```python
import functools
import math

import jax
import jax.numpy as jnp
from jax import lax
from jax.experimental import pallas as pl
from jax.experimental.pallas import tpu as pltpu
from jax.experimental.pallas import tpu_sc as plsc

F32 = jnp.float32
BF16 = jnp.bfloat16
I32 = jnp.int32
I16 = jnp.int16

CHUNK = 64
N_HEADS = 8
HEAD_DIM = 64
N_KV_HEADS = 2
GROUP = N_HEADS // N_KV_HEADS
N_IDX_HEADS = 8
IDX_DIM = 32
TOPK_MAX = 256
CONV_K = 3
N_BUCKETS = 32
MAX_DISTANCE = 128
N_EXPERTS = 32
TOP_K = 4
SWIGLU_LIMIT = 7.0
SWIGLU_ALPHA = 1.702
LN_EPS = 1e-5
MASK_VALUE = -1e30
QK_SCALE = HEAD_DIM ** -0.5 * math.log2(math.e)
EXP2_SAFE = 96.0

LANES = 128
SUBLANES = 8
KEY_CHUNK = 512
QUERY_BLOCK = 256
SLAB = 64
WEIGHT_SPLIT = 1
ROW_TILE = 512
ROUTE_TILE = 1024
SC_CORES = 2
SC_SUBCORES = 16
SC_ROWS = 64
SORTED_MIN_TOKENS = 2048
INT_MIN = -(2 ** 31)
VMEM_LIMIT = 56 * 1024 * 1024


def _cparams(sem):
    return pltpu.CompilerParams(dimension_semantics=sem, vmem_limit_bytes=VMEM_LIMIT)


def _dot(a, b):
    return jnp.dot(a, b, preferred_element_type=F32)


def _layer_norm(z, g, b):
    mu = jnp.mean(z, axis=-1, keepdims=True)
    zc = z - mu
    var = jnp.mean(zc * zc, axis=-1, keepdims=True)
    return zc * lax.rsqrt(var + LN_EPS) * g + b


def _bucket_thresholds():
    nb = N_BUCKETS // 2
    max_exact = nb // 2
    out = []
    for j in range(1, nb - max_exact):
        out.append(math.ceil(max_exact * (MAX_DISTANCE / max_exact) ** (j / (nb - max_exact)) - 1e-9))
    return tuple(out)


def _bias_table_kernel(rel_ref, tab_ref, *, offsets):
    nb = N_BUCKETS // 2
    max_exact = nb // 2
    thr = _bucket_thresholds()
    ii = lax.broadcasted_iota(I32, (LANES, LANES), 0)
    jj = lax.broadcasted_iota(I32, (LANES, LANES), 1)
    for d, off in enumerate(offsets):
        rel = off + ii - jj
        n = jnp.abs(rel)
        large = jnp.full((LANES, LANES), max_exact, I32)
        for t in thr:
            large = large + jnp.where(n >= t, 1, 0)
        bucket = jnp.where(rel > 0, nb, 0) + jnp.where(n < max_exact, n, large)
        for h in range(N_HEADS):
            acc = jnp.zeros((LANES, LANES), F32)
            for b in range(N_BUCKETS):
                acc = jnp.where(bucket == b, rel_ref[b, h], acc)
            tab_ref[d, h] = (acc - rel_ref[nb - 1, h]) * math.log2(math.e)


def _bias_tables(rel_bias, offsets):
    return pl.pallas_call(
        functools.partial(_bias_table_kernel, offsets=offsets),
        out_shape=jax.ShapeDtypeStruct((len(offsets), N_HEADS, LANES, LANES), F32),
        in_specs=[pl.BlockSpec(memory_space=pltpu.SMEM)],
        out_specs=pl.BlockSpec(memory_space=pltpu.VMEM),
        name="bias_tables",
    )(rel_bias)


def _proj_kernel(x_ref, wa_ref, ws_ref, wc_ref, wconv_ref, buf_ref,
                 k_ref, v_ref, small_ref, conv_ref, nbuf_ref, *rest, ts, attn_layouts):
    s = pl.program_id(1)
    xb = x_ref[0].astype(BF16)
    pa = _dot(xb, wa_ref[...])
    ps = _dot(xb, ws_ref[...])
    pc = _dot(xb, wc_ref[...])
    aw = N_HEADS * HEAD_DIM
    iw = N_IDX_HEADS * IDX_DIM
    kv = N_KV_HEADS * HEAD_DIM
    k = pa[:, aw + iw:aw + iw + kv]
    v = pa[:, aw + iw + kv:aw + iw + 2 * kv]
    k_ref[0] = k
    v_ref[0] = v
    small_ref[0] = ps[:, 0:small_ref.shape[2]]
    scale = QK_SCALE
    if attn_layouts:
        kb_ref, vT_ref, kidxb_ref, qT_ref, qiT_ref, wT_ref, knorm_ref, carry_ref = rest
        for n in range(N_KV_HEADS):
            kb_ref[0, 0, n] = k[:, n * HEAD_DIM:(n + 1) * HEAD_DIM].astype(BF16)
        kTf = k.astype(BF16).astype(F32).T
        knorm_ref[0, 0] = jnp.concatenate(
            [jnp.sum(kTf[n * HEAD_DIM:(n + 1) * HEAD_DIM] ** 2, axis=0, keepdims=True) for n in range(N_KV_HEADS)],
            axis=0)
        vT_ref[0, 0] = v.T.astype(BF16)
        kidxb_ref[0, 0] = ps[:, 0:IDX_DIM].astype(BF16)
        qT = (pa[:, 0:aw] * scale).T.astype(BF16)
        for h in range(N_HEADS):
            qT_ref[0, h] = qT[h * HEAD_DIM:(h + 1) * HEAD_DIM]
        qiT = pa[:, aw:aw + iw].T.astype(BF16)
        for h in range(N_IDX_HEADS):
            qiT_ref[0, h] = qiT[h * IDX_DIM:(h + 1) * IDX_DIM]
        wT_ref[0] = ps.T[IDX_DIM:IDX_DIM + N_IDX_HEADS]
    else:
        pq_ref, carry_ref = rest
        pq_ref[0] = pa[:, 0:aw + iw]

    cw = pc.shape[1] // 3
    u = pc[:, cw:2 * cw] * pc[:, 2 * cw:3 * cw]

    @pl.when(s == 0)
    def _():
        carry_ref[6:8, :] = buf_ref[0]

    carry_ref[8:8 + ts, :] = u
    y = (carry_ref[6:6 + ts, :] * wconv_ref[0:1, :]
         + carry_ref[7:7 + ts, :] * wconv_ref[1:2, :]
         + u * wconv_ref[2:3, :])
    conv_ref[0] = (pc[:, 0:cw] * y).astype(BF16)
    nb = carry_ref[ts + 6:ts + 8, :]
    nbuf_ref[0] = nb
    carry_ref[6:8, :] = nb


def _project(x, conv_buf, wa, ws, wc, wconv, *, attn_layouts):
    B, S, D = x.shape
    ts = min(KEY_CHUNK, S)
    assert S % ts == 0 and S >= CONV_K - 1
    ns = S // ts
    cw = wc.shape[1] // 3
    sw = IDX_DIM if attn_layouts else LANES
    kv = N_KV_HEADS * HEAD_DIM
    aw = N_HEADS * HEAD_DIM
    iw = N_IDX_HEADS * IDX_DIM
    out_shape = [
        jax.ShapeDtypeStruct((B, S, kv), F32),
        jax.ShapeDtypeStruct((B, S, kv), F32),
        jax.ShapeDtypeStruct((B, S, sw), F32),
        jax.ShapeDtypeStruct((B, S, cw), BF16),
        jax.ShapeDtypeStruct((B, CONV_K - 1, cw), F32),
    ]
    out_specs = [
        pl.BlockSpec((1, ts, kv), lambda b, s: (b, s, 0)),
        pl.BlockSpec((1, ts, kv), lambda b, s: (b, s, 0)),
        pl.BlockSpec((1, ts, sw), lambda b, s: (b, s, 0)),
        pl.BlockSpec((1, ts, cw), lambda b, s: (b, s, 0)),
        pl.BlockSpec((1, CONV_K - 1, cw), lambda b, s: (b, 0, 0)),
    ]
    if attn_layouts:
        out_shape += [
            jax.ShapeDtypeStruct((B, ns, N_KV_HEADS, ts, HEAD_DIM), BF16),
            jax.ShapeDtypeStruct((B, ns, kv, ts), BF16),
            jax.ShapeDtypeStruct((B, ns, ts, IDX_DIM), BF16),
            jax.ShapeDtypeStruct((B, N_HEADS, HEAD_DIM, S), BF16),
            jax.ShapeDtypeStruct((B, N_IDX_HEADS, IDX_DIM, S), BF16),
            jax.ShapeDtypeStruct((B, N_IDX_HEADS, S), F32),
            jax.ShapeDtypeStruct((B, ns, N_KV_HEADS, ts), F32),
        ]
        out_specs += [
            pl.BlockSpec((1, 1, N_KV_HEADS, ts, HEAD_DIM), lambda b, s: (b, s, 0, 0, 0)),
            pl.BlockSpec((1, 1, kv, ts), lambda b, s: (b, s, 0, 0)),
            pl.BlockSpec((1, 1, ts, IDX_DIM), lambda b, s: (b, s, 0, 0)),
            pl.BlockSpec((1, N_HEADS, HEAD_DIM, ts), lambda b, s: (b, 0, 0, s)),
            pl.BlockSpec((1, N_IDX_HEADS, IDX_DIM, ts), lambda b, s: (b, 0, 0, s)),
            pl.BlockSpec((1, N_IDX_HEADS, ts), lambda b, s: (b, 0, s)),
            pl.BlockSpec((1, 1, N_KV_HEADS, ts), lambda b, s: (b, s, 0, 0)),
        ]
    else:
        out_shape += [jax.ShapeDtypeStruct((B, S, aw + iw), F32)]
        out_specs += [pl.BlockSpec((1, ts, aw + iw), lambda b, s: (b, s, 0))]
    const2 = lambda b, s: (0, 0)
    in_specs = [
        pl.BlockSpec((1, ts, D), lambda b, s: (b, s, 0)),
        pl.BlockSpec(wa.shape, const2),
        pl.BlockSpec(ws.shape, const2),
        pl.BlockSpec(wc.shape, const2),
        pl.BlockSpec(wconv.shape, const2),
        pl.BlockSpec((1, CONV_K - 1, cw), lambda b, s: (b, 0, 0)),
    ]
    return pl.pallas_call(
        functools.partial(_proj_kernel, ts=ts, attn_layouts=attn_layouts),
        grid=(B, ns),
        in_specs=in_specs,
        out_specs=tuple(out_specs),
        out_shape=tuple(out_shape),
        scratch_shapes=[pltpu.VMEM((ts + 8, cw), F32)],
        compiler_params=_cparams(("arbitrary", "arbitrary")),
        name="proj_conv",
    )(x, wa, ws, wc, wconv, conv_buf)


def _attn_kernel(qT_ref, qiT_ref, wT_ref, kidx_ref, kb_ref, vT_ref, knorm_ref, tab_ref,
                 o_ref, skey_ref, hi16_ref, lo16_ref, jb_ref, madd_ref, *head_refs,
                 qb, kc, past, l_true, topk, offsets):
    s_refs, p_refs, m_refs, l_refs, acc_refs = (head_refs[g * N_HEADS:(g + 1) * N_HEADS] for g in range(5))
    i = pl.program_id(1)
    qoff = past + i * qb
    adm_end = jnp.minimum(((qoff + qb - 1) // CHUNK + 1) * CHUNK, l_true)
    nck = (adm_end + kc - 1) // kc
    n_far = jnp.maximum(qoff - LANES, 0) // kc
    idx_bits = int(l_true).bit_length()
    groups = kc // SUBLANES

    rowi = lax.broadcasted_iota(I32, (kc, qb), 0)
    qpos = qoff + lax.broadcasted_iota(I32, (1, qb), 1)
    kmax = jnp.minimum((qpos // CHUNK + 1) * CHUNK, l_true)

    def score_body(c, carry):
        kidx_c = kidx_ref[0, c]
        sc = jnp.zeros((kc, qb), F32)
        for h in range(N_IDX_HEADS):
            d = _dot(kidx_c, qiT_ref[0, h])
            sc = sc + wT_ref[0, h:h + 1, :] * jnp.maximum(d, 0.0)
        bits = pltpu.bitcast(sc, I32)
        skey = jnp.where(bits < 0, bits ^ jnp.int32(0x7FFFFFFF), bits)
        skey = jnp.where(skey == -1, 0, skey)
        skey = jnp.where(rowi < kmax - c * kc, skey, jnp.int32(INT_MIN))
        skey_ref[c] = skey
        hi16_ref[c] = (skey >> 16).astype(I16)
        return carry

    lax.fori_loop(0, nck, score_body, 0)

    def count(pred_fn):
        def body(c, part):
            ind = jnp.where(pred_fn(c, skey_ref[c]), 1, 0)
            return part + jnp.sum(ind.reshape(groups, SUBLANES, qb), axis=0)
        part = lax.fori_loop(0, nck, body, jnp.zeros((SUBLANES, qb), I32))
        return jnp.sum(part, axis=0, keepdims=True)

    def count16(ref, pred_fn):
        pack = 2 * SUBLANES

        def body(c, part):
            ind = jnp.where(pred_fn(ref[c]), jnp.bfloat16(1), jnp.bfloat16(0)).reshape(kc // pack, pack, qb)
            terms = [ind[g] for g in range(kc // pack)]
            while len(terms) > 1:
                terms = [terms[g] + terms[g + 1] for g in range(0, len(terms), 2)]
            return part + terms[0].astype(F32)
        part = lax.fori_loop(0, nck, body, jnp.zeros((pack, qb), F32))
        return jnp.sum(part, axis=0, keepdims=True).astype(I32)

    def search16(ref, target):
        def body(it, t_u):
            cand_u = t_u | (jnp.int32(1) << (15 - it))
            cand = (cand_u - 2 ** 15).astype(I16)
            cnt = count16(ref, lambda k: k >= cand)
            return jnp.where(cnt >= target, cand_u, t_u)
        return lax.fori_loop(0, 16, body, jnp.zeros((1, qb), I32))

    hi_u = search16(hi16_ref, topk)
    hi_s = (hi_u - 2 ** 15).astype(I16)
    rest = topk - count16(hi16_ref, lambda k: k > hi_s)

    def low_body(c, carry):
        low = ((skey_ref[c] & 0xFFFF) - 2 ** 15).astype(I16)
        lo16_ref[c] = jnp.where(hi16_ref[c] == hi_s, low, jnp.int16(-2 ** 15))
        return carry

    lax.fori_loop(0, nck, low_body, 0)
    lo_u = search16(lo16_ref, rest)
    thr = (hi_u - 2 ** 15) * 2 ** 16 + lo_u
    cnt_ge = count(lambda c, sk: sk >= thr)
    cnt_gt = count(lambda c, sk: sk > thr)

    jb_ref[...] = jnp.full(jb_ref.shape, 2 ** 31 - 1, I32)

    @pl.when(jnp.max(cnt_ge) > topk)
    def _():
        need = topk - cnt_gt

        def tie_body(it, jb):
            cand = jb | (jnp.int32(1) << (idx_bits - 1 - it))
            cnt = count(lambda c, sk: jnp.logical_and(sk == thr, c * kc + rowi < cand))
            return jnp.where(cnt <= need, cand, jb)

        jb = lax.fori_loop(0, idx_bits, tie_body, jnp.zeros((1, qb), I32))
        jb_ref[...] = jnp.broadcast_to(jb, jb_ref.shape)

    jbound = jnp.where(thr == jnp.int32(INT_MIN), 0, jb_ref[0:1, :])
    thr_m1 = thr - 1

    for h in range(N_HEADS):
        l_refs[h][...] = jnp.zeros((1, qb), F32)
        acc_refs[h][...] = jnp.zeros((HEAD_DIM, qb), F32)
    n_slabs = kc // SLAB

    def fold(x, op):
        return op(x.reshape(SLAB // SUBLANES, SUBLANES, qb), axis=0)

    def near_bias(c, h):
        tiles = []
        for t in range(kc // LANES):
            row = []
            for u in range(qb // LANES):
                d_tu = c * kc + t * LANES - (qoff + u * LANES)
                delta = jnp.zeros((LANES, LANES), F32)
                for di, off in enumerate(offsets):
                    delta = jnp.where(d_tu == off, tab_ref[di, h], delta)
                row.append(delta)
            tiles.append(jnp.concatenate(row, axis=1))
        return jnp.concatenate(tiles, axis=0)

    def attn_chunk(c, near):
        sk = skey_ref[c]
        t_eff = jnp.where(rowi < jbound - c * kc, thr_m1, thr)
        madd_ref[...] = jnp.where(sk > t_eff, 0.0, MASK_VALUE)
        maccs = []
        for h in range(N_HEADS):
            x = _dot(kb_ref[0, c, h // GROUP], qT_ref[0, h]) + madd_ref[...]
            if near:
                x = x + near_bias(c, h)
            s_refs[h][...] = x
            maccs.append(jnp.max(x.reshape(groups, SUBLANES, qb), axis=0))
        alphas = []
        for h in range(N_HEADS):
            s_ref, p_ref, m_ref = s_refs[h], p_refs[h], m_refs[h]
            m_old = m_ref[...]
            m_new = jnp.maximum(m_old, jnp.max(maccs[h], axis=0, keepdims=True))
            for j in range(n_slabs):
                rows = pl.ds(j * SLAB, SLAB)
                p_ref[rows, :] = jnp.exp2(s_ref[rows, :] - m_new).astype(BF16)
            m_ref[...] = m_new
            alphas.append(jnp.exp2(m_old - m_new))
        ones = jnp.ones((2 * SUBLANES, kc), BF16)
        v_aug = [jnp.concatenate([vT_ref[0, c, n * HEAD_DIM:(n + 1) * HEAD_DIM, :], ones], axis=0)
                 for n in range(N_KV_HEADS)]
        for h in range(N_HEADS):
            pv = _dot(v_aug[h // GROUP], p_refs[h][...])
            acc_refs[h][...] = alphas[h] * acc_refs[h][...] + pv[0:HEAD_DIM]
            l_refs[h][...] = alphas[h] * l_refs[h][...] + pv[HEAD_DIM:HEAD_DIM + 1]

    def attn_chunk_fixed(c, near):
        sk = skey_ref[c]
        t_eff = jnp.where(rowi < jbound - c * kc, thr_m1, thr)
        madd_ref[...] = jnp.where(sk > t_eff, 0.0, MASK_VALUE)
        ones = jnp.ones((2 * SUBLANES, kc), BF16)
        v_aug = [jnp.concatenate([vT_ref[0, c, n * HEAD_DIM:(n + 1) * HEAD_DIM, :], ones], axis=0)
                 for n in range(N_KV_HEADS)]
        for h in range(N_HEADS):
            x = _dot(kb_ref[0, c, h // GROUP], qT_ref[0, h]) + madd_ref[...]
            if near:
                x = x + near_bias(c, h)
            p_refs[h][...] = jnp.exp2(x).astype(BF16)
        for h in range(N_HEADS):
            pv = _dot(v_aug[h // GROUP], p_refs[h][...])
            acc_refs[h][...] += pv[0:HEAD_DIM]
            l_refs[h][...] += pv[HEAD_DIM:HEAD_DIM + 1]

    def run(chunk_fn):
        lax.fori_loop(0, n_far, lambda c, carry: (chunk_fn(c, False), carry)[1], 0)
        lax.fori_loop(n_far, nck, lambda c, carry: (chunk_fn(c, True), carry)[1], 0)

    q_sq = jnp.zeros((1, qb), F32)
    for h in range(N_HEADS):
        qf = qT_ref[0, h].astype(F32)
        q_sq = jnp.maximum(q_sq, jnp.sum(qf * qf, axis=0, keepdims=True))
    bound = jnp.sqrt(jnp.max(q_sq) * jnp.max(knorm_ref[0])) + jnp.max(jnp.abs(tab_ref[...]))
    fixed_shift = bound < EXP2_SAFE

    @pl.when(fixed_shift)
    def _():
        run(attn_chunk_fixed)

    @pl.when(jnp.logical_not(fixed_shift))
    def _():
        for h in range(N_HEADS):
            m_refs[h][...] = jnp.full((1, qb), MASK_VALUE, F32)
        run(attn_chunk)

    oT = jnp.concatenate([acc_refs[h][...] / l_refs[h][...] for h in range(N_HEADS)], axis=0)
    o_ref[0] = oT.T.astype(o_ref.dtype)


def _attend(qT, qiT, wT, kidxb, kb, vT, knorm, tabs, *, qb, past, l_true, offsets):
    B, _, _, S = qT.shape
    nc, kc = vT.shape[1], vT.shape[3]
    assert S % qb == 0 and qb % LANES == 0 and past % LANES == 0 and kc % LANES == 0
    topk = min(TOPK_MAX, l_true // 4)
    kern = functools.partial(_attn_kernel, qb=qb, kc=kc, past=past, l_true=l_true,
                             topk=topk, offsets=offsets)
    in_specs = [
        pl.BlockSpec((1, N_HEADS, HEAD_DIM, qb), lambda b, i: (b, 0, 0, i)),
        pl.BlockSpec((1, N_IDX_HEADS, IDX_DIM, qb), lambda b, i: (b, 0, 0, i)),
        pl.BlockSpec((1, N_IDX_HEADS, qb), lambda b, i: (b, 0, i)),
        pl.BlockSpec((1,) + kidxb.shape[1:], lambda b, i: (b, 0, 0, 0)),
        pl.BlockSpec((1,) + kb.shape[1:], lambda b, i: (b, 0, 0, 0, 0)),
        pl.BlockSpec((1,) + vT.shape[1:], lambda b, i: (b, 0, 0, 0)),
        pl.BlockSpec((1,) + knorm.shape[1:], lambda b, i: (b, 0, 0, 0)),
        pl.BlockSpec(tabs.shape, lambda b, i: (0, 0, 0, 0)),
    ]
    return pl.pallas_call(
        kern,
        grid=(B, S // qb),
        in_specs=in_specs,
        out_specs=pl.BlockSpec((1, qb, N_HEADS * HEAD_DIM), lambda b, i: (b, i, 0)),
        out_shape=jax.ShapeDtypeStruct((B, S, N_HEADS * HEAD_DIM), BF16),
        scratch_shapes=[
            pltpu.VMEM((nc, kc, qb), I32),
            pltpu.VMEM((nc, kc, qb), I16),
            pltpu.VMEM((nc, kc, qb), I16),
            pltpu.VMEM((SUBLANES, qb), I32),
            pltpu.VMEM((kc, qb), F32),
        ] + [pltpu.VMEM((kc, qb), F32)] * N_HEADS
          + [pltpu.VMEM((kc, qb), BF16)] * N_HEADS
          + [pltpu.VMEM((1, qb), F32)] * N_HEADS
          + [pltpu.VMEM((1, qb), F32)] * N_HEADS
          + [pltpu.VMEM((HEAD_DIM, qb), F32)] * N_HEADS,
        compiler_params=_cparams(("arbitrary", "arbitrary")),
        name="dsa_attend",
    )(qT, qiT, wT, kidxb, kb, vT, knorm, tabs)


def _mix_kernel(x_ref, attn_ref, conv_ref, wo_ref, g_ref, b_ref, h_ref, *, dn_alpha):
    cat = jnp.concatenate([attn_ref[...], conv_ref[...]], axis=1)
    mixed = _dot(cat, wo_ref[...])
    h_ref[...] = _layer_norm(dn_alpha * x_ref[...] + mixed, g_ref[...], b_ref[...])


def _mix(x2, attn2, conv2, wo, g, b, *, dn_alpha):
    T, D = x2.shape
    tt = min(1024, T)
    assert T % tt == 0
    aw, cw = attn2.shape[1], conv2.shape[1]
    row = lambda i: (i, 0)
    const = lambda i: (0, 0)
    return pl.pallas_call(
        functools.partial(_mix_kernel, dn_alpha=dn_alpha),
        grid=(T // tt,),
        in_specs=[pl.BlockSpec((tt, D), row), pl.BlockSpec((tt, aw), row), pl.BlockSpec((tt, cw), row),
                  pl.BlockSpec(wo.shape, const), pl.BlockSpec((1, D), const), pl.BlockSpec((1, D), const)],
        out_specs=pl.BlockSpec((tt, D), row),
        out_shape=jax.ShapeDtypeStruct((T, D), F32),
        compiler_params=_cparams(("arbitrary",)),
        name="outproj_ln1",
    )(x2, attn2, conv2, wo, g, b)


def _route_picks(logits):
    lane = lax.broadcasted_iota(I32, logits.shape, 1)
    work = logits
    vals, hots = [], []
    for _ in range(TOP_K):
        m = jnp.max(work, axis=1, keepdims=True)
        idx = jnp.min(jnp.where(work == m, lane, LANES), axis=1, keepdims=True)
        hot = lane == idx
        vals.append(m)
        hots.append(hot)
        work = jnp.where(hot, -jnp.inf, work)
    es = [jnp.exp(v - vals[0]) for v in vals]
    den = es[0]
    for e in es[1:]:
        den = den + e
    return hots, [e / den for e in es]


def _route(logits):
    hots, gates = _route_picks(logits)
    comb = jnp.zeros(logits.shape, F32)
    picked = jnp.zeros(logits.shape, F32)
    for gate, hot in zip(gates, hots):
        comb = jnp.where(hot, gate, comb)
        picked = jnp.where(hot, 1.0, picked)
    return comb, picked


def _moe_kernel(h_ref, wr_ref, br_ref, *rest, dn_alpha):
    wgu_refs, bgu_ref, wd_refs, rest = rest[:1], rest[1], rest[2:3], rest[3:]
    bd_ref, g_ref, b_ref, y_ref, hb_ref, comb_ref = rest
    e = pl.program_id(1)
    tt = h_ref.shape[0]
    ff = wd_refs[0].shape[1]

    @pl.when(e == 0)
    def _():
        hb = h_ref[...].astype(BF16)
        hb_ref[...] = hb
        comb_ref[...] = _route(_dot(hb, wr_ref[...]) + br_ref[...])[0]
        y_ref[...] = jnp.zeros(y_ref.shape, F32)

    hb = hb_ref[...]
    gu = jnp.concatenate([_dot(hb, w[0]) for w in wgu_refs], axis=1) + bgu_ref[0]
    gate = jnp.minimum(gu[:, 0:ff], SWIGLU_LIMIT)
    up = jnp.clip(gu[:, ff:2 * ff], -SWIGLU_LIMIT, SWIGLU_LIMIT)
    act = ((up + 1.0) * (gate * (1.0 / (1.0 + jnp.exp(-SWIGLU_ALPHA * gate))))).astype(BF16)
    y = jnp.concatenate([_dot(act, w[0]) for w in wd_refs], axis=1) + bd_ref[0]
    lane = lax.broadcasted_iota(I32, (tt, LANES), 1)
    y_ref[...] += jnp.sum(jnp.where(lane == e, comb_ref[...], 0.0), axis=1, keepdims=True) * y

    @pl.when(e == pl.num_programs(1) - 1)
    def _():
        y_ref[...] = _layer_norm(dn_alpha * h_ref[...] + y_ref[...], g_ref[...], b_ref[...])


def _moe(h2, wr, br, wgu, bgu, wd, bd, g, b, *, dn_alpha):
    T, D = h2.shape
    E, _, F2 = wgu.shape
    tt = min(512, T)
    assert T % tt == 0
    row = lambda i, e: (i, 0)
    const = lambda i, e: (0, 0)
    exp3 = lambda i, e: (e, 0, 0)
    ns = 1
    col = lambda j: (lambda i, e: (e, 0, j))
    return pl.pallas_call(
        functools.partial(_moe_kernel, dn_alpha=dn_alpha),
        grid=(T // tt, E),
        in_specs=[pl.BlockSpec((tt, D), row), pl.BlockSpec(wr.shape, const), pl.BlockSpec(br.shape, const)]
                 + [pl.BlockSpec((1, D, F2 // ns), col(j)) for j in range(ns)]
                 + [pl.BlockSpec((1, 1, F2), exp3)]
                 + [pl.BlockSpec((1, F2 // 2, D // ns), col(j)) for j in range(ns)]
                 + [pl.BlockSpec((1, 1, D), exp3),
                    pl.BlockSpec((1, D), const), pl.BlockSpec((1, D), const)],
        out_specs=pl.BlockSpec((tt, D), row),
        out_shape=jax.ShapeDtypeStruct((T, D), F32),
        scratch_shapes=[pltpu.VMEM((tt, D), BF16), pltpu.VMEM((tt, LANES), F32)],
        compiler_params=_cparams(("arbitrary", "arbitrary")),
        name="moe_ln2",
    )(h2, wr, br, *([wgu] * ns), bgu, *([wd] * ns), bd, g, b)


def _pack_pairs(x):
    w = x.shape[1] // 2
    bits = pltpu.bitcast(x.astype(BF16).astype(F32), I32)
    return (bits[:, 0:w] & jnp.int32(-65536)) | lax.shift_right_logical(bits[:, w:2 * w], 16)


def _unpack_pairs(p):
    hi = pltpu.bitcast(p & jnp.int32(-65536), F32)
    lo = pltpu.bitcast(lax.shift_left(p, 16), F32)
    return jnp.concatenate([hi, lo], axis=1)


def _mix_route_kernel(x_ref, attn_ref, conv_ref, wo_ref, g_ref, b_ref, wrT_ref, brT_ref,
                      h_ref, hpk_ref, infoT_ref, gcol_ref, cnt_ref, *, dn_alpha):
    cat = jnp.concatenate([attn_ref[...], conv_ref[...]], axis=1)
    h = _layer_norm(dn_alpha * x_ref[...] + _dot(cat, wo_ref[...]), g_ref[...], b_ref[...])
    h_ref[...] = h
    tt = h.shape[0]
    ne = cnt_ref.shape[1]
    hpk_ref[...] = _pack_pairs(h)
    logits = lax.dot_general(wrT_ref[...], h.astype(BF16), (((1,), (1,)), ((), ())),
                             preferred_element_type=F32)
    work = logits[0:ne] + jnp.tile(brT_ref[0:ne, :], (1, tt // LANES))
    sub = lax.broadcasted_iota(I32, (ne, tt), 0)
    vals, idxs, hots = [], [], []
    for _ in range(TOP_K):
        m = jnp.max(work, axis=0, keepdims=True)
        idx = jnp.min(jnp.where(work == m, sub, ne), axis=0, keepdims=True)
        hot = sub == idx
        vals.append(m)
        idxs.append(idx)
        hots.append(hot)
        work = jnp.where(hot, -jnp.inf, work)
    es = [jnp.exp(v - vals[0]) for v in vals]
    den = es[0]
    for e in es[1:]:
        den = den + e
    gates = [e / den for e in es]
    picked = jnp.zeros((ne, tt), F32)
    for hot in hots:
        picked = jnp.where(hot, 1.0, picked)
    earlier = lax.broadcasted_iota(I32, (tt, tt), 0) < lax.broadcasted_iota(I32, (tt, tt), 1)
    rank = _dot(picked.astype(BF16), jnp.where(earlier, 1.0, 0.0).astype(BF16))
    ranks = [jnp.sum(jnp.where(hot, rank, 0.0), axis=0, keepdims=True) for hot in hots]
    pad = jnp.zeros((2 * SUBLANES - 3 * TOP_K, tt), F32)
    infoT_ref[...] = jnp.concatenate([i.astype(F32) for i in idxs] + gates + ranks + [pad], axis=0)
    gcol_ref[...] = jnp.concatenate(gates + [jnp.zeros((LANES - TOP_K, tt), F32)], axis=0).T
    cnt_ref[0] = jnp.broadcast_to(jnp.sum(picked, axis=1, keepdims=True), (ne, LANES))


def _mix_route(x2, attn2, conv2, wo, g, b, wr, br, ne, *, dn_alpha):
    T, D = x2.shape
    tt = ROUTE_TILE
    aw, cw = attn2.shape[1], conv2.shape[1]
    row = lambda i: (i, 0)
    const = lambda i: (0, 0)
    wrT = wr.T
    brT = jnp.broadcast_to(br.reshape(LANES, 1), (LANES, LANES))
    return pl.pallas_call(
        functools.partial(_mix_route_kernel, dn_alpha=dn_alpha),
        grid=(T // tt,),
        in_specs=[pl.BlockSpec((tt, D), row), pl.BlockSpec((tt, aw), row), pl.BlockSpec((tt, cw), row),
                  pl.BlockSpec(wo.shape, const), pl.BlockSpec((1, D), const), pl.BlockSpec((1, D), const),
                  pl.BlockSpec(wrT.shape, const), pl.BlockSpec(brT.shape, const)],
        out_specs=(pl.BlockSpec((tt, D), row), pl.BlockSpec((tt, D // 2), row),
                   pl.BlockSpec((2 * SUBLANES, tt), lambda i: (0, i)),
                   pl.BlockSpec((tt, LANES), row), pl.BlockSpec((1, ne, LANES), lambda i: (i, 0, 0))),
        out_shape=(jax.ShapeDtypeStruct((T, D), F32), jax.ShapeDtypeStruct((T, D // 2), I32),
                   jax.ShapeDtypeStruct((2 * SUBLANES, T), F32),
                   jax.ShapeDtypeStruct((T, LANES), F32), jax.ShapeDtypeStruct((T // tt, ne, LANES), F32)),
        compiler_params=_cparams(("arbitrary",)),
        name="outproj_ln1_route",
    )(x2, attn2, conv2, wo, g, b, wrT, brT)


def _sc_gather_rows(table, idx):
    M, W = idx.shape[0], table.shape[1]
    n_workers = SC_CORES * SC_SUBCORES
    per_w = M // n_workers
    n_ch = per_w // SC_ROWS
    assert M == n_workers * n_ch * SC_ROWS
    mesh = plsc.VectorSubcoreMesh(core_axis_name="c", subcore_axis_name="s")

    @functools.partial(
        pl.kernel, mesh=mesh,
        out_type=jax.ShapeDtypeStruct((M, W), table.dtype),
        scratch_types=[pltpu.VMEM((n_ch, SC_ROWS), I32), pltpu.VMEM((SC_ROWS, W), table.dtype),
                       pltpu.SemaphoreType.DMA],
    )
    def gather(table_hbm, idx_hbm, out_hbm, idx_v, rows_v, sem):
        wid = lax.axis_index("s") * SC_CORES + lax.axis_index("c")
        pltpu.sync_copy(idx_hbm.at[wid], idx_v)

        @pl.loop(0, n_ch)
        def _(j):
            pltpu.async_copy(table_hbm.at[idx_v.at[j]], rows_v, sem).wait()
            pltpu.sync_copy(rows_v, out_hbm.at[pl.ds(wid * per_w + j * SC_ROWS, SC_ROWS)])

    return gather(table, idx.reshape(n_workers, n_ch, SC_ROWS))


def _expert_rows_kernel(te_ref, used_ref, x_ref, *rest):
    wgu_refs, rest = rest[:WEIGHT_SPLIT], rest[WEIGHT_SPLIT:]
    bgu_ref, rest = rest[0], rest[1:]
    wd_refs, rest = rest[:WEIGHT_SPLIT], rest[WEIGHT_SPLIT:]
    bd_ref, y_ref = rest
    ff = wd_refs[0].shape[1]

    @pl.when(pl.program_id(0) < used_ref[0])
    def _():
        xg = _unpack_pairs(x_ref[...]).astype(BF16)
        gu = jnp.concatenate([_dot(xg, w[0]) for w in wgu_refs], axis=1) + bgu_ref[0]
        gate = jnp.minimum(gu[:, 0:ff], SWIGLU_LIMIT)
        up = jnp.clip(gu[:, ff:2 * ff], -SWIGLU_LIMIT, SWIGLU_LIMIT)
        act = ((up + 1.0) * (gate * (1.0 / (1.0 + jnp.exp(-SWIGLU_ALPHA * gate))))).astype(BF16)
        y_ref[...] = _pack_pairs(jnp.concatenate([_dot(act, w[0]) for w in wd_refs], axis=1) + bd_ref[0])


def _expert_rows(tile_expert, used_tiles, xs, wgu, bgu, wd, bd):
    R, W = xs.shape
    E, D, F2 = wgu.shape
    ns = WEIGHT_SPLIT
    col = lambda c: (lambda j, te, used: (te[j], 0, c))
    exp3 = lambda j, te, used: (te[j], 0, 0)
    rows = lambda j, te, used: (j, 0)
    grid_spec = pltpu.PrefetchScalarGridSpec(
        num_scalar_prefetch=2,
        grid=(R // ROW_TILE,),
        in_specs=[pl.BlockSpec((ROW_TILE, W), rows)]
                 + [pl.BlockSpec((1, D, F2 // ns), col(c)) for c in range(ns)]
                 + [pl.BlockSpec((1, 1, F2), exp3)]
                 + [pl.BlockSpec((1, F2 // 2, D // ns), col(c)) for c in range(ns)]
                 + [pl.BlockSpec((1, 1, D), exp3)],
        out_specs=pl.BlockSpec((ROW_TILE, W), rows),
    )
    return pl.pallas_call(
        _expert_rows_kernel,
        grid_spec=grid_spec,
        out_shape=jax.ShapeDtypeStruct((R, W), I32),
        compiler_params=_cparams(("arbitrary",)),
        name="moe_expert_rows",
    )(tile_expert, used_tiles, xs, *([wgu] * ns), bgu, *([wd] * ns), bd)


def _combine_kernel(h_ref, yg_ref, gcol_ref, g_ref, b_ref, o_ref, *, dn_alpha):
    acc = dn_alpha * h_ref[...]
    for k in range(TOP_K):
        acc = acc + gcol_ref[:, k:k + 1] * _unpack_pairs(yg_ref[k])
    o_ref[...] = _layer_norm(acc, g_ref[...], b_ref[...])


def _combine(h2, yg, info, g, b, *, dn_alpha):
    T, D = h2.shape
    tt = 512
    row = lambda i: (i, 0)
    const = lambda i: (0, 0)
    return pl.pallas_call(
        functools.partial(_combine_kernel, dn_alpha=dn_alpha),
        grid=(T // tt,),
        in_specs=[pl.BlockSpec((tt, D), row), pl.BlockSpec((TOP_K, tt, D // 2), lambda i: (0, i, 0)),
                  pl.BlockSpec((tt, LANES), row),
                  pl.BlockSpec((1, D), const), pl.BlockSpec((1, D), const)],
        out_specs=pl.BlockSpec((tt, D), row),
        out_shape=jax.ShapeDtypeStruct((T, D), F32),
        compiler_params=_cparams(("arbitrary",)),
        name="moe_combine_ln2",
    )(h2, yg, info, g, b)


def _dest_kernel(infoT_ref, base_ref, dest_ref):
    tt = infoT_ref.shape[1]
    ne = base_ref.shape[1]
    base = jnp.tile(base_ref[0], (1, tt // LANES))
    sub = lax.broadcasted_iota(I32, (ne, tt), 0)
    rows = []
    for k in range(TOP_K):
        hot = sub == infoT_ref[k:k + 1, :].astype(I32)
        rows.append(jnp.sum(jnp.where(hot, base, 0.0), axis=0, keepdims=True)
                    + infoT_ref[2 * TOP_K + k:2 * TOP_K + k + 1, :])
    rows.append(jnp.zeros((SUBLANES - TOP_K, tt), F32))
    dest_ref[...] = jnp.concatenate(rows, axis=0).astype(I32)


def _dest_rows(infoT, base):
    T = infoT.shape[1]
    tt = ROUTE_TILE
    ne = base.shape[1]
    return pl.pallas_call(
        _dest_kernel,
        grid=(T // tt,),
        in_specs=[pl.BlockSpec((2 * SUBLANES, tt), lambda i: (0, i)), pl.BlockSpec((1, ne, LANES), lambda i: (i, 0, 0))],
        out_specs=pl.BlockSpec((SUBLANES, tt), lambda i: (0, i)),
        out_shape=jax.ShapeDtypeStruct((SUBLANES, T), I32),
        compiler_params=_cparams(("arbitrary",)),
        name="moe_dest_rows",
    )(infoT, base)


def _sc_scatter_rows(src, dest_km, n_rows):
    T, W = src.shape
    K = dest_km.shape[0]
    n_workers = SC_CORES * SC_SUBCORES
    per_w = T // n_workers
    n_ch = per_w // SC_ROWS
    assert T == n_workers * n_ch * SC_ROWS
    mesh = plsc.VectorSubcoreMesh(core_axis_name="c", subcore_axis_name="s")
    idx = dest_km.reshape(K, n_workers, n_ch, SC_ROWS).transpose(1, 0, 2, 3).reshape(n_workers, K * n_ch, SC_ROWS)

    @functools.partial(
        pl.kernel, mesh=mesh,
        out_type=jax.ShapeDtypeStruct((n_rows, W), src.dtype),
        scratch_types=[pltpu.VMEM((K * n_ch, SC_ROWS), I32), pltpu.VMEM((SC_ROWS, W), src.dtype),
                       pltpu.SemaphoreType.DMA],
    )
    def scatter(src_hbm, idx_hbm, out_hbm, idx_v, rows_v, sem):
        wid = lax.axis_index("s") * SC_CORES + lax.axis_index("c")
        pltpu.sync_copy(idx_hbm.at[wid], idx_v)

        @pl.loop(0, n_ch)
        def _(j):
            pltpu.sync_copy(src_hbm.at[pl.ds(wid * per_w + j * SC_ROWS, SC_ROWS)], rows_v)
            for k in range(K):
                pltpu.async_copy(rows_v, out_hbm.at[idx_v.at[k * n_ch + j]], sem).wait()

    return scatter(src, idx)


def _moe_sorted(h2, routed, wgu, bgu, wd, bd, g, b, *, dn_alpha):
    T, D = h2.shape
    E = wgu.shape[0]
    hpk, infoT, gcol, cnt = routed
    cnt = cnt[:, :, 0].astype(I32)
    total = jnp.sum(cnt, axis=0)
    padded = -(-total // ROW_TILE) * ROW_TILE
    ends = jnp.cumsum(padded)
    base = (ends - padded)[None, :] + jnp.cumsum(cnt, axis=0) - cnt
    base = jnp.broadcast_to(base.astype(F32)[:, :, None], cnt.shape + (LANES,))
    dest_km = _dest_rows(infoT, base)[0:TOP_K]
    R = T * TOP_K + E * ROW_TILE
    tile_start = jnp.arange(R // ROW_TILE, dtype=I32) * ROW_TILE
    tile_expert = jnp.minimum(jnp.sum((ends[None, :] <= tile_start[:, None]).astype(I32), axis=1), E - 1)
    used_tiles = (ends[E - 1] // ROW_TILE).reshape(1)
    xs = _sc_scatter_rows(hpk, dest_km, R)
    ys = _expert_rows(tile_expert, used_tiles, xs, wgu, bgu, wd, bd)
    yg = _sc_gather_rows(ys, dest_km.reshape(-1))
    return _combine(h2, yg.reshape(TOP_K, T, D // 2), gcol, g, b, dn_alpha=dn_alpha)


def _split_gu_kernel(w_ref, perm_ref, o_ref, *, band):
    n = w_ref.shape[2]
    for m in range(n // band):
        both = _dot(w_ref[0, :, m * band:(m + 1) * band].astype(BF16), perm_ref[...]).astype(BF16)
        o_ref[0, :, m * (band // 2):(m + 1) * (band // 2)] = both[:, 0:band // 2]
        o_ref[0, :, n // 2 + m * (band // 2):n // 2 + (m + 1) * (band // 2)] = both[:, band // 2:band]


def _split_gu(w_gu):
    E, D, N = w_gu.shape
    tr = min(512, D)
    band = min(512, N)
    j = jnp.arange(band, dtype=I32)
    src = jnp.where(j < band // 2, 2 * j, 2 * (j - band // 2) + 1)
    perm = (jnp.arange(band, dtype=I32)[:, None] == src[None, :]).astype(BF16)
    return pl.pallas_call(
        functools.partial(_split_gu_kernel, band=band),
        grid=(E, D // tr),
        in_specs=[pl.BlockSpec((1, tr, N), lambda e, r: (e, r, 0)),
                  pl.BlockSpec((band, band), lambda e, r: (0, 0))],
        out_specs=pl.BlockSpec((1, tr, N), lambda e, r: (e, r, 0)),
        out_shape=jax.ShapeDtypeStruct((E, D, N), BF16),
        compiler_params=_cparams(("arbitrary", "arbitrary")),
        name="split_gate_up",
    )(w_gu, perm)


def _prep_weights(w_in, w_conv, w_out, ln1_g, ln1_b, w_router, b_router, w_gu, b_gu, w_down, b_down,
                  ln2_g, ln2_b):
    aw = N_HEADS * HEAD_DIM
    kv = N_KV_HEADS * HEAD_DIM
    iw = N_IDX_HEADS * IDX_DIM
    D = w_in.shape[0]
    cw = D - aw
    o_q, o_k, o_v = 0, aw, aw + kv
    o_qi = o_v + kv
    o_ki = o_qi + iw
    o_wi = o_ki + IDX_DIM
    o_c = o_wi + N_IDX_HEADS
    wa = jnp.concatenate([w_in[:, o_q:o_k], w_in[:, o_qi:o_ki], w_in[:, o_k:o_v], w_in[:, o_v:o_qi]],
                         axis=1).astype(BF16)
    ws = jnp.pad(w_in[:, o_ki:o_c], ((0, 0), (0, LANES - IDX_DIM - N_IDX_HEADS))).astype(BF16)
    wc = w_in[:, o_c:o_c + 3 * cw].astype(BF16)
    E = w_router.shape[1]
    wr = jnp.pad(w_router, ((0, 0), (0, LANES - E))).astype(BF16)
    br = jnp.pad(b_router, (0, LANES - E), constant_values=MASK_VALUE).reshape(1, LANES)
    F = w_gu.shape[2] // 2
    return dict(
        wa=wa, ws=ws, wc=wc, wconv=w_conv, wo=w_out.astype(BF16),
        ln1_g=ln1_g.reshape(1, D), ln1_b=ln1_b.reshape(1, D),
        wr=wr, br=br,
        wgu=_split_gu(w_gu),
        bgu=jnp.concatenate([b_gu[:, 0::2], b_gu[:, 1::2]], axis=1).reshape(E, 1, 2 * F),
        wd=w_down.astype(BF16), bd=b_down.reshape(E, 1, D),
        ln2_g=ln2_g.reshape(1, D), ln2_b=ln2_b.reshape(1, D),
    )


def _decode_layouts(pq, small, k_all, v_all, kidx_all, kc, qb):
    B, S, _ = pq.shape
    aw = N_HEADS * HEAD_DIM
    L, kv = k_all.shape[1], k_all.shape[2]
    lp = -(-L // kc) * kc
    nc = lp // kc
    padk = lambda a: jnp.pad(a, ((0, 0), (0, lp - L), (0, 0)))
    padq = lambda a: jnp.pad(a, ((0, 0), (0, 0), (0, 0), (0, qb - S)))
    q = pq[:, :, 0:aw] * QK_SCALE
    qT = padq(q.reshape(B, S, N_HEADS, HEAD_DIM).transpose(0, 2, 3, 1)).astype(BF16)
    qiT = padq(pq[:, :, aw:].reshape(B, S, N_IDX_HEADS, IDX_DIM).transpose(0, 2, 3, 1)).astype(BF16)
    wT = jnp.pad(small[:, :, IDX_DIM:IDX_DIM + N_IDX_HEADS].transpose(0, 2, 1), ((0, 0), (0, 0), (0, qb - S)))
    kb = padk(k_all).reshape(B, nc, kc, N_KV_HEADS, HEAD_DIM).transpose(0, 1, 3, 2, 4).astype(BF16)
    vT = padk(v_all).reshape(B, nc, kc, kv).transpose(0, 1, 3, 2).astype(BF16)
    kidxb = padk(kidx_all).reshape(B, nc, kc, IDX_DIM).astype(BF16)
    knorm = jnp.sum(kb.astype(F32) ** 2, axis=-1)
    return qT, qiT, wT, kidxb, kb, vT, knorm


def _layer(x, past_k, past_v, past_kidx, conv_buf, rel_bias, w, *, dn_alpha):
    B, S, D = x.shape
    prefill = past_k is None
    outs = _project(x, conv_buf, w["wa"], w["ws"], w["wc"], w["wconv"], attn_layouts=prefill)
    k, v, small, conv, new_buf = outs[:5]
    k_idx = small if prefill else small[:, :, 0:IDX_DIM]
    if prefill:
        past, l_true = 0, S
        qb = min(QUERY_BLOCK, S)
        kb, vT, kidxb, qT, qiT, wT, knorm = outs[5:]
    else:
        past = past_k.shape[1]
        l_true = past + S
        qb = -(-S // LANES) * LANES
        qT, qiT, wT, kidxb, kb, vT, knorm = _decode_layouts(
            outs[5], small, jnp.concatenate([past_k, k], axis=1), jnp.concatenate([past_v, v], axis=1),
            jnp.concatenate([past_kidx, k_idx], axis=1), KEY_CHUNK, qb)
    offsets = (-LANES, 0)
    tabs = _bias_tables(rel_bias, offsets)
    attn = _attend(qT, qiT, wT, kidxb, kb, vT, knorm, tabs, qb=qb, past=past, l_true=l_true, offsets=offsets)
    attn = attn[:, :S]
    mix_args = (x.reshape(B * S, D), attn.reshape(B * S, -1), conv.reshape(B * S, -1), w["wo"],
                w["ln1_g"], w["ln1_b"])
    experts = (w["wgu"], w["bgu"], w["wd"], w["bd"], w["ln2_g"], w["ln2_b"])
    sc_unit = SC_CORES * SC_SUBCORES * SC_ROWS
    sortable = ((B * S) % ROUTE_TILE == 0 and (B * S * TOP_K) % sc_unit == 0
                and (N_EXPERTS * ROW_TILE) % sc_unit == 0)
    if sortable and B * S >= SORTED_MIN_TOKENS:
        h, *routed = _mix_route(*mix_args, w["wr"], w["br"], N_EXPERTS, dn_alpha=dn_alpha)
        y = _moe_sorted(h, routed, *experts, dn_alpha=dn_alpha)
    else:
        h = _mix(*mix_args, dn_alpha=dn_alpha)
        y = _moe(h, w["wr"], w["br"], *experts, dn_alpha=dn_alpha)
    return (y.reshape(B, S, D), k.reshape(B, S, N_KV_HEADS, HEAD_DIM), v.reshape(B, S, N_KV_HEADS, HEAD_DIM),
            k_idx, new_buf)


def kernel(x_prompt, x_sample, cache_k, cache_v, cache_kidx, state_conv, rel_bias, w_in, w_conv, w_out,
           ln1_g, ln1_b, w_router, b_router, w_gu, b_gu, w_down, b_down, ln2_g, ln2_b):
    depth = w_in.shape[0]
    assert depth == 1
    dn_alpha = (2 * depth) ** 0.25
    kv = N_KV_HEADS * HEAD_DIM
    w = _prep_weights(w_in[0], w_conv[0], w_out[0], ln1_g[0], ln1_b[0], w_router[0], b_router[0],
                      w_gu[0], b_gu[0], w_down[0], b_down[0], ln2_g[0], ln2_b[0])
    Bp = x_prompt.shape[0]
    cw = w_conv.shape[2]
    zero_buf = jnp.zeros((Bp, CONV_K - 1, cw), F32)
    yp, k1, v1, i1, c1 = _layer(x_prompt, None, None, None, zero_buf, rel_bias, w, dn_alpha=dn_alpha)
    Bs, P = cache_k.shape[1], cache_k.shape[2]
    ys, k2, v2, i2, c2 = _layer(x_sample, cache_k[0].reshape(Bs, P, kv), cache_v[0].reshape(Bs, P, kv),
                                cache_kidx[0], state_conv[0], rel_bias, w, dn_alpha=dn_alpha)
    return (yp, ys, k1[None], v1[None], i1[None], c1[None], k2[None], v2[None], i2[None], c2[None])
```

```python
import functools
import math

import jax
import jax.numpy as jnp
from jax import lax
from jax.experimental import pallas as pl
from jax.experimental.pallas import tpu as pltpu
from jax.experimental.pallas import tpu_sc as plsc

F32 = jnp.float32
BF16 = jnp.bfloat16
I32 = jnp.int32
I16 = jnp.int16

CHUNK = 64
N_HEADS = 8
HEAD_DIM = 64
N_KV_HEADS = 2
GROUP = N_HEADS // N_KV_HEADS
N_IDX_HEADS = 8
IDX_DIM = 32
TOPK_MAX = 256
CONV_K = 3
N_BUCKETS = 32
MAX_DISTANCE = 128
N_EXPERTS = 32
TOP_K = 4
SWIGLU_LIMIT = 7.0
SWIGLU_ALPHA = 1.702
LN_EPS = 1e-5
MASK_VALUE = -1e30
QK_SCALE = HEAD_DIM ** -0.5 * math.log2(math.e)
EXP2_SAFE = 96.0

LANES = 128
SUBLANES = 8
KEY_CHUNK = 512
QUERY_BLOCK = 256
SLAB = 64
ROW_TILE = 512
ROUTE_TILE = 1024
SC_CORES = 2
SC_SUBCORES = 16
SC_ROWS = 64
SORTED_MIN_TOKENS = 2048
INT_MIN = -(2 ** 31)
VMEM_LIMIT = 56 * 1024 * 1024


def _cparams(sem):
    return pltpu.CompilerParams(dimension_semantics=sem, vmem_limit_bytes=VMEM_LIMIT)


def _dot(a, b):
    return jnp.dot(a, b, preferred_element_type=F32)


def _layer_norm(z, g, b):
    mu = jnp.mean(z, axis=-1, keepdims=True)
    zc = z - mu
    var = jnp.mean(zc * zc, axis=-1, keepdims=True)
    return zc * lax.rsqrt(var + LN_EPS) * g + b


def _bucket_thresholds():
    nb = N_BUCKETS // 2
    max_exact = nb // 2
    out = []
    for j in range(1, nb - max_exact):
        out.append(math.ceil(max_exact * (MAX_DISTANCE / max_exact) ** (j / (nb - max_exact)) - 1e-9))
    return tuple(out)


def _bias_table_kernel(rel_ref, tab_ref, *, offsets):
    nb = N_BUCKETS // 2
    max_exact = nb // 2
    thr = _bucket_thresholds()
    ii = lax.broadcasted_iota(I32, (LANES, LANES), 0)
    jj = lax.broadcasted_iota(I32, (LANES, LANES), 1)
    for d, off in enumerate(offsets):
        rel = off + ii - jj
        n = jnp.abs(rel)
        large = jnp.full((LANES, LANES), max_exact, I32)
        for t in thr:
            large = large + jnp.where(n >= t, 1, 0)
        bucket = jnp.where(rel > 0, nb, 0) + jnp.where(n < max_exact, n, large)
        for h in range(N_HEADS):
            acc = jnp.zeros((LANES, LANES), F32)
            for b in range(N_BUCKETS):
                acc = jnp.where(bucket == b, rel_ref[b, h], acc)
            tab_ref[d, h] = (acc - rel_ref[nb - 1, h]) * math.log2(math.e)


def _bias_tables(rel_bias, offsets):
    return pl.pallas_call(
        functools.partial(_bias_table_kernel, offsets=offsets),
        out_shape=jax.ShapeDtypeStruct((len(offsets), N_HEADS, LANES, LANES), F32),
        in_specs=[pl.BlockSpec(memory_space=pltpu.SMEM)],
        out_specs=pl.BlockSpec(memory_space=pltpu.VMEM),
        name="bias_tables",
    )(rel_bias)


def _proj_kernel(x_ref, wa_ref, ws_ref, wc_ref, wconv_ref, buf_ref,
                 k_ref, v_ref, small_ref, conv_ref, nbuf_ref, *rest, ts, attn_layouts):
    s = pl.program_id(1)
    xb = x_ref[0].astype(BF16)
    pa = _dot(xb, wa_ref[...])
    ps = _dot(xb, ws_ref[...])
    pc = _dot(xb, wc_ref[...])
    aw = N_HEADS * HEAD_DIM
    iw = N_IDX_HEADS * IDX_DIM
    kv = N_KV_HEADS * HEAD_DIM
    k = pa[:, aw + iw:aw + iw + kv]
    v = pa[:, aw + iw + kv:aw + iw + 2 * kv]
    k_ref[0] = k
    v_ref[0] = v
    small_ref[0] = ps[:, 0:small_ref.shape[2]]
    scale = QK_SCALE
    if attn_layouts:
        kb_ref, vT_ref, kidxb_ref, qT_ref, qiT_ref, wT_ref, knorm_ref, carry_ref = rest
        for n in range(N_KV_HEADS):
            kb_ref[0, 0, n] = k[:, n * HEAD_DIM:(n + 1) * HEAD_DIM].astype(BF16)
        kTf = k.astype(BF16).astype(F32).T
        knorm_ref[0, 0] = jnp.concatenate(
            [jnp.sum(kTf[n * HEAD_DIM:(n + 1) * HEAD_DIM] ** 2, axis=0, keepdims=True) for n in range(N_KV_HEADS)],
            axis=0)
        vT_ref[0, 0] = v.T.astype(BF16)
        kidxb_ref[0, 0] = ps[:, 0:IDX_DIM].astype(BF16)
        qT = (pa[:, 0:aw] * scale).T.astype(BF16)
        for h in range(N_HEADS):
            qT_ref[0, h] = qT[h * HEAD_DIM:(h + 1) * HEAD_DIM]
        qiT = pa[:, aw:aw + iw].T.astype(BF16)
        for h in range(N_IDX_HEADS):
            qiT_ref[0, h] = qiT[h * IDX_DIM:(h + 1) * IDX_DIM]
        wT_ref[0] = ps.T[IDX_DIM:IDX_DIM + N_IDX_HEADS]
    else:
        pq_ref, carry_ref = rest
        pq_ref[0] = pa[:, 0:aw + iw]

    cw = pc.shape[1] // 3
    u = pc[:, cw:2 * cw] * pc[:, 2 * cw:3 * cw]

    @pl.when(s == 0)
    def _():
        carry_ref[6:8, :] = buf_ref[0]

    carry_ref[8:8 + ts, :] = u
    y = (carry_ref[6:6 + ts, :] * wconv_ref[0:1, :]
         + carry_ref[7:7 + ts, :] * wconv_ref[1:2, :]
         + u * wconv_ref[2:3, :])
    conv_ref[0] = (pc[:, 0:cw] * y).astype(BF16)
    nb = carry_ref[ts + 6:ts + 8, :]
    nbuf_ref[0] = nb
    carry_ref[6:8, :] = nb


def _project(x, conv_buf, wa, ws, wc, wconv, *, attn_layouts):
    B, S, D = x.shape
    ts = min(KEY_CHUNK, S)
    assert S % ts == 0 and S >= CONV_K - 1
    ns = S // ts
    cw = wc.shape[1] // 3
    sw = IDX_DIM if attn_layouts else LANES
    kv = N_KV_HEADS * HEAD_DIM
    aw = N_HEADS * HEAD_DIM
    iw = N_IDX_HEADS * IDX_DIM
    out_shape = [
        jax.ShapeDtypeStruct((B, S, kv), F32),
        jax.ShapeDtypeStruct((B, S, kv), F32),
        jax.ShapeDtypeStruct((B, S, sw), F32),
        jax.ShapeDtypeStruct((B, S, cw), BF16),
        jax.ShapeDtypeStruct((B, CONV_K - 1, cw), F32),
    ]
    out_specs = [
        pl.BlockSpec((1, ts, kv), lambda b, s: (b, s, 0)),
        pl.BlockSpec((1, ts, kv), lambda b, s: (b, s, 0)),
        pl.BlockSpec((1, ts, sw), lambda b, s: (b, s, 0)),
        pl.BlockSpec((1, ts, cw), lambda b, s: (b, s, 0)),
        pl.BlockSpec((1, CONV_K - 1, cw), lambda b, s: (b, 0, 0)),
    ]
    if attn_layouts:
        out_shape += [
            jax.ShapeDtypeStruct((B, ns, N_KV_HEADS, ts, HEAD_DIM), BF16),
            jax.ShapeDtypeStruct((B, ns, kv, ts), BF16),
            jax.ShapeDtypeStruct((B, ns, ts, IDX_DIM), BF16),
            jax.ShapeDtypeStruct((B, N_HEADS, HEAD_DIM, S), BF16),
            jax.ShapeDtypeStruct((B, N_IDX_HEADS, IDX_DIM, S), BF16),
            jax.ShapeDtypeStruct((B, N_IDX_HEADS, S), F32),
            jax.ShapeDtypeStruct((B, ns, N_KV_HEADS, ts), F32),
        ]
        out_specs += [
            pl.BlockSpec((1, 1, N_KV_HEADS, ts, HEAD_DIM), lambda b, s: (b, s, 0, 0, 0)),
            pl.BlockSpec((1, 1, kv, ts), lambda b, s: (b, s, 0, 0)),
            pl.BlockSpec((1, 1, ts, IDX_DIM), lambda b, s: (b, s, 0, 0)),
            pl.BlockSpec((1, N_HEADS, HEAD_DIM, ts), lambda b, s: (b, 0, 0, s)),
            pl.BlockSpec((1, N_IDX_HEADS, IDX_DIM, ts), lambda b, s: (b, 0, 0, s)),
            pl.BlockSpec((1, N_IDX_HEADS, ts), lambda b, s: (b, 0, s)),
            pl.BlockSpec((1, 1, N_KV_HEADS, ts), lambda b, s: (b, s, 0, 0)),
        ]
    else:
        out_shape += [jax.ShapeDtypeStruct((B, S, aw + iw), F32)]
        out_specs += [pl.BlockSpec((1, ts, aw + iw), lambda b, s: (b, s, 0))]
    const2 = lambda b, s: (0, 0)
    in_specs = [
        pl.BlockSpec((1, ts, D), lambda b, s: (b, s, 0)),
        pl.BlockSpec(wa.shape, const2),
        pl.BlockSpec(ws.shape, const2),
        pl.BlockSpec(wc.shape, const2),
        pl.BlockSpec(wconv.shape, const2),
        pl.BlockSpec((1, CONV_K - 1, cw), lambda b, s: (b, 0, 0)),
    ]
    return pl.pallas_call(
        functools.partial(_proj_kernel, ts=ts, attn_layouts=attn_layouts),
        grid=(B, ns),
        in_specs=in_specs,
        out_specs=tuple(out_specs),
        out_shape=tuple(out_shape),
        scratch_shapes=[pltpu.VMEM((ts + 8, cw), F32)],
        compiler_params=_cparams(("arbitrary", "arbitrary")),
        name="proj_conv",
    )(x, wa, ws, wc, wconv, conv_buf)


def _attn_kernel(qT_ref, qiT_ref, wT_ref, kidx_ref, kb_ref, vT_ref, knorm_ref, tab_ref,
                 o_ref, skey_ref, hi16_ref, lo16_ref, jb_ref, madd_ref, *head_refs,
                 qb, kc, past, l_true, topk, offsets):
    s_refs, p_refs, m_refs, l_refs, acc_refs = (head_refs[g * N_HEADS:(g + 1) * N_HEADS] for g in range(5))
    i = pl.program_id(1)
    qoff = past + i * qb
    adm_end = jnp.minimum(((qoff + qb - 1) // CHUNK + 1) * CHUNK, l_true)
    nck = (adm_end + kc - 1) // kc
    n_far = jnp.maximum(qoff - LANES, 0) // kc
    idx_bits = int(l_true).bit_length()
    groups = kc // SUBLANES

    rowi = lax.broadcasted_iota(I32, (kc, qb), 0)
    qpos = qoff + lax.broadcasted_iota(I32, (1, qb), 1)
    kmax = jnp.minimum((qpos // CHUNK + 1) * CHUNK, l_true)

    def score_body(c, carry):
        kidx_c = kidx_ref[0, c]
        sc = jnp.zeros((kc, qb), F32)
        for h in range(N_IDX_HEADS):
            d = _dot(kidx_c, qiT_ref[0, h])
            sc = sc + wT_ref[0, h:h + 1, :] * jnp.maximum(d, 0.0)
        bits = pltpu.bitcast(sc, I32)
        skey = jnp.where(bits < 0, bits ^ jnp.int32(0x7FFFFFFF), bits)
        skey = jnp.where(skey == -1, 0, skey)
        skey = jnp.where(rowi < kmax - c * kc, skey, jnp.int32(INT_MIN))
        skey_ref[c] = skey
        hi16_ref[c] = (skey >> 16).astype(I16)
        return carry

    lax.fori_loop(0, nck, score_body, 0)

    def count(pred_fn):
        def body(c, part):
            ind = jnp.where(pred_fn(c, skey_ref[c]), 1, 0)
            return part + jnp.sum(ind.reshape(groups, SUBLANES, qb), axis=0)
        part = lax.fori_loop(0, nck, body, jnp.zeros((SUBLANES, qb), I32))
        return jnp.sum(part, axis=0, keepdims=True)

    def count16(ref, pred_fn):
        pack = 2 * SUBLANES

        def body(c, part):
            ind = jnp.where(pred_fn(ref[c]), jnp.bfloat16(1), jnp.bfloat16(0)).reshape(kc // pack, pack, qb)
            terms = [ind[g] for g in range(kc // pack)]
            while len(terms) > 1:
                terms = [terms[g] + terms[g + 1] for g in range(0, len(terms), 2)]
            return part + terms[0].astype(F32)
        part = lax.fori_loop(0, nck, body, jnp.zeros((pack, qb), F32))
        return jnp.sum(part, axis=0, keepdims=True).astype(I32)

    def search16(ref, target):
        def body(it, t_u):
            cand_u = t_u | (jnp.int32(1) << (15 - it))
            cand = (cand_u - 2 ** 15).astype(I16)
            cnt = count16(ref, lambda k: k >= cand)
            return jnp.where(cnt >= target, cand_u, t_u)
        return lax.fori_loop(0, 16, body, jnp.zeros((1, qb), I32))

    hi_u = search16(hi16_ref, topk)
    hi_s = (hi_u - 2 ** 15).astype(I16)
    rest = topk - count16(hi16_ref, lambda k: k > hi_s)

    def low_body(c, carry):
        low = ((skey_ref[c] & 0xFFFF) - 2 ** 15).astype(I16)
        lo16_ref[c] = jnp.where(hi16_ref[c] == hi_s, low, jnp.int16(-2 ** 15))
        return carry

    lax.fori_loop(0, nck, low_body, 0)
    lo_u = search16(lo16_ref, rest)
    thr = (hi_u - 2 ** 15) * 2 ** 16 + lo_u
    cnt_ge = count(lambda c, sk: sk >= thr)
    cnt_gt = count(lambda c, sk: sk > thr)

    jb_ref[...] = jnp.full(jb_ref.shape, 2 ** 31 - 1, I32)

    @pl.when(jnp.max(cnt_ge) > topk)
    def _():
        need = topk - cnt_gt

        def tie_body(it, jb):
            cand = jb | (jnp.int32(1) << (idx_bits - 1 - it))
            cnt = count(lambda c, sk: jnp.logical_and(sk == thr, c * kc + rowi < cand))
            return jnp.where(cnt <= need, cand, jb)

        jb = lax.fori_loop(0, idx_bits, tie_body, jnp.zeros((1, qb), I32))
        jb_ref[...] = jnp.broadcast_to(jb, jb_ref.shape)

    jbound = jnp.where(thr == jnp.int32(INT_MIN), 0, jb_ref[0:1, :])
    thr_m1 = thr - 1

    for h in range(N_HEADS):
        l_refs[h][...] = jnp.zeros((1, qb), F32)
        acc_refs[h][...] = jnp.zeros((HEAD_DIM, qb), F32)
    n_slabs = kc // SLAB

    def fold(x, op):
        return op(x.reshape(SLAB // SUBLANES, SUBLANES, qb), axis=0)

    def near_bias(c, h):
        tiles = []
        for t in range(kc // LANES):
            row = []
            for u in range(qb // LANES):
                d_tu = c * kc + t * LANES - (qoff + u * LANES)
                delta = jnp.zeros((LANES, LANES), F32)
                for di, off in enumerate(offsets):
                    delta = jnp.where(d_tu == off, tab_ref[di, h], delta)
                row.append(delta)
            tiles.append(jnp.concatenate(row, axis=1))
        return jnp.concatenate(tiles, axis=0)

    def attn_chunk(c, near):
        sk = skey_ref[c]
        t_eff = jnp.where(rowi < jbound - c * kc, thr_m1, thr)
        madd_ref[...] = jnp.where(sk > t_eff, 0.0, MASK_VALUE)
        maccs = []
        for h in range(N_HEADS):
            x = _dot(kb_ref[0, c, h // GROUP], qT_ref[0, h]) + madd_ref[...]
            if near:
                x = x + near_bias(c, h)
            s_refs[h][...] = x
            maccs.append(jnp.max(x.reshape(groups, SUBLANES, qb), axis=0))
        alphas = []
        for h in range(N_HEADS):
            s_ref, p_ref, m_ref = s_refs[h], p_refs[h], m_refs[h]
            m_old = m_ref[...]
            m_new = jnp.maximum(m_old, jnp.max(maccs[h], axis=0, keepdims=True))
            for j in range(n_slabs):
                rows = pl.ds(j * SLAB, SLAB)
                p_ref[rows, :] = jnp.exp2(s_ref[rows, :] - m_new).astype(BF16)
            m_ref[...] = m_new
            alphas.append(jnp.exp2(m_old - m_new))
        ones = jnp.ones((2 * SUBLANES, kc), BF16)
        v_aug = [jnp.concatenate([vT_ref[0, c, n * HEAD_DIM:(n + 1) * HEAD_DIM, :], ones], axis=0)
                 for n in range(N_KV_HEADS)]
        for h in range(N_HEADS):
            pv = _dot(v_aug[h // GROUP], p_refs[h][...])
            acc_refs[h][...] = alphas[h] * acc_refs[h][...] + pv[0:HEAD_DIM]
            l_refs[h][...] = alphas[h] * l_refs[h][...] + pv[HEAD_DIM:HEAD_DIM + 1]

    def attn_chunk_fixed(c, near):
        sk = skey_ref[c]
        t_eff = jnp.where(rowi < jbound - c * kc, thr_m1, thr)
        madd_ref[...] = jnp.where(sk > t_eff, 0.0, MASK_VALUE)
        ones = jnp.ones((2 * SUBLANES, kc), BF16)
        v_aug = [jnp.concatenate([vT_ref[0, c, n * HEAD_DIM:(n + 1) * HEAD_DIM, :], ones], axis=0)
                 for n in range(N_KV_HEADS)]
        for h in range(N_HEADS):
            x = _dot(kb_ref[0, c, h // GROUP], qT_ref[0, h]) + madd_ref[...]
            if near:
                x = x + near_bias(c, h)
            p_refs[h][...] = jnp.exp2(x).astype(BF16)
        for h in range(N_HEADS):
            pv = _dot(v_aug[h // GROUP], p_refs[h][...])
            acc_refs[h][...] += pv[0:HEAD_DIM]
            l_refs[h][...] += pv[HEAD_DIM:HEAD_DIM + 1]

    def run(chunk_fn):
        lax.fori_loop(0, n_far, lambda c, carry: (chunk_fn(c, False), carry)[1], 0)
        lax.fori_loop(n_far, nck, lambda c, carry: (chunk_fn(c, True), carry)[1], 0)

    q_sq = jnp.zeros((1, qb), F32)
    for h in range(N_HEADS):
        qf = qT_ref[0, h].astype(F32)
        q_sq = jnp.maximum(q_sq, jnp.sum(qf * qf, axis=0, keepdims=True))
    bound = jnp.sqrt(jnp.max(q_sq) * jnp.max(knorm_ref[0])) + jnp.max(jnp.abs(tab_ref[...]))
    fixed_shift = bound < EXP2_SAFE

    @pl.when(fixed_shift)
    def _():
        run(attn_chunk_fixed)

    @pl.when(jnp.logical_not(fixed_shift))
    def _():
        for h in range(N_HEADS):
            m_refs[h][...] = jnp.full((1, qb), MASK_VALUE, F32)
        run(attn_chunk)

    oT = jnp.concatenate([acc_refs[h][...] / l_refs[h][...] for h in range(N_HEADS)], axis=0)
    o_ref[0] = oT.T.astype(o_ref.dtype)


def _attend(qT, qiT, wT, kidxb, kb, vT, knorm, tabs, *, qb, past, l_true, offsets):
    B, _, _, S = qT.shape
    nc, kc = vT.shape[1], vT.shape[3]
    assert S % qb == 0 and qb % LANES == 0 and past % LANES == 0 and kc % LANES == 0
    topk = min(TOPK_MAX, l_true // 4)
    kern = functools.partial(_attn_kernel, qb=qb, kc=kc, past=past, l_true=l_true,
                             topk=topk, offsets=offsets)
    in_specs = [
        pl.BlockSpec((1, N_HEADS, HEAD_DIM, qb), lambda b, i: (b, 0, 0, i)),
        pl.BlockSpec((1, N_IDX_HEADS, IDX_DIM, qb), lambda b, i: (b, 0, 0, i)),
        pl.BlockSpec((1, N_IDX_HEADS, qb), lambda b, i: (b, 0, i)),
        pl.BlockSpec((1,) + kidxb.shape[1:], lambda b, i: (b, 0, 0, 0)),
        pl.BlockSpec((1,) + kb.shape[1:], lambda b, i: (b, 0, 0, 0, 0)),
        pl.BlockSpec((1,) + vT.shape[1:], lambda b, i: (b, 0, 0, 0)),
        pl.BlockSpec((1,) + knorm.shape[1:], lambda b, i: (b, 0, 0, 0)),
        pl.BlockSpec(tabs.shape, lambda b, i: (0, 0, 0, 0)),
    ]
    return pl.pallas_call(
        kern,
        grid=(B, S // qb),
        in_specs=in_specs,
        out_specs=pl.BlockSpec((1, qb, N_HEADS * HEAD_DIM), lambda b, i: (b, i, 0)),
        out_shape=jax.ShapeDtypeStruct((B, S, N_HEADS * HEAD_DIM), BF16),
        scratch_shapes=[
            pltpu.VMEM((nc, kc, qb), I32),
            pltpu.VMEM((nc, kc, qb), I16),
            pltpu.VMEM((nc, kc, qb), I16),
            pltpu.VMEM((SUBLANES, qb), I32),
            pltpu.VMEM((kc, qb), F32),
        ] + [pltpu.VMEM((kc, qb), F32)] * N_HEADS
          + [pltpu.VMEM((kc, qb), BF16)] * N_HEADS
          + [pltpu.VMEM((1, qb), F32)] * N_HEADS
          + [pltpu.VMEM((1, qb), F32)] * N_HEADS
          + [pltpu.VMEM((HEAD_DIM, qb), F32)] * N_HEADS,
        compiler_params=_cparams(("arbitrary", "arbitrary")),
        name="dsa_attend",
    )(qT, qiT, wT, kidxb, kb, vT, knorm, tabs)


def _mix_kernel(x_ref, attn_ref, conv_ref, wo_ref, g_ref, b_ref, h_ref, *, dn_alpha):
    cat = jnp.concatenate([attn_ref[...], conv_ref[...]], axis=1)
    mixed = _dot(cat, wo_ref[...])
    h_ref[...] = _layer_norm(dn_alpha * x_ref[...] + mixed, g_ref[...], b_ref[...])


def _mix(x2, attn2, conv2, wo, g, b, *, dn_alpha):
    T, D = x2.shape
    tt = min(1024, T)
    assert T % tt == 0
    aw, cw = attn2.shape[1], conv2.shape[1]
    row = lambda i: (i, 0)
    const = lambda i: (0, 0)
    return pl.pallas_call(
        functools.partial(_mix_kernel, dn_alpha=dn_alpha),
        grid=(T // tt,),
        in_specs=[pl.BlockSpec((tt, D), row), pl.BlockSpec((tt, aw), row), pl.BlockSpec((tt, cw), row),
                  pl.BlockSpec(wo.shape, const), pl.BlockSpec((1, D), const), pl.BlockSpec((1, D), const)],
        out_specs=pl.BlockSpec((tt, D), row),
        out_shape=jax.ShapeDtypeStruct((T, D), F32),
        compiler_params=_cparams(("arbitrary",)),
        name="outproj_ln1",
    )(x2, attn2, conv2, wo, g, b)


def _route_picks(logits):
    lane = lax.broadcasted_iota(I32, logits.shape, 1)
    work = logits
    vals, hots = [], []
    for _ in range(TOP_K):
        m = jnp.max(work, axis=1, keepdims=True)
        idx = jnp.min(jnp.where(work == m, lane, LANES), axis=1, keepdims=True)
        hot = lane == idx
        vals.append(m)
        hots.append(hot)
        work = jnp.where(hot, -jnp.inf, work)
    es = [jnp.exp(v - vals[0]) for v in vals]
    den = es[0]
    for e in es[1:]:
        den = den + e
    return hots, [e / den for e in es]


def _route(logits):
    hots, gates = _route_picks(logits)
    comb = jnp.zeros(logits.shape, F32)
    for gate, hot in zip(gates, hots):
        comb = jnp.where(hot, gate, comb)
    return comb


def _expert_mlp(xb, wgu_ref, bgu_ref, wd_ref, bd_ref):
    ff = wd_ref.shape[1]
    gu = _dot(xb, wgu_ref[0]) + bgu_ref[0]
    gate = jnp.minimum(gu[:, 0:ff], SWIGLU_LIMIT)
    up = jnp.clip(gu[:, ff:2 * ff], -SWIGLU_LIMIT, SWIGLU_LIMIT)
    act = (up + 1.0) * (gate * (1.0 / (1.0 + jnp.exp(-SWIGLU_ALPHA * gate))))
    return _dot(act.astype(BF16), wd_ref[0]) + bd_ref[0]


def _moe_kernel(h_ref, wr_ref, br_ref, wgu_ref, bgu_ref, wd_ref, bd_ref, g_ref, b_ref,
                y_ref, hb_ref, comb_ref, *, dn_alpha):
    e = pl.program_id(1)
    tt = h_ref.shape[0]

    @pl.when(e == 0)
    def _():
        hb = h_ref[...].astype(BF16)
        hb_ref[...] = hb
        comb_ref[...] = _route(_dot(hb, wr_ref[...]) + br_ref[...])
        y_ref[...] = jnp.zeros(y_ref.shape, F32)

    y = _expert_mlp(hb_ref[...], wgu_ref, bgu_ref, wd_ref, bd_ref)
    lane = lax.broadcasted_iota(I32, (tt, LANES), 1)
    y_ref[...] += jnp.sum(jnp.where(lane == e, comb_ref[...], 0.0), axis=1, keepdims=True) * y

    @pl.when(e == pl.num_programs(1) - 1)
    def _():
        y_ref[...] = _layer_norm(dn_alpha * h_ref[...] + y_ref[...], g_ref[...], b_ref[...])


def _moe(h2, wr, br, wgu, bgu, wd, bd, g, b, *, dn_alpha):
    T, D = h2.shape
    E, _, F2 = wgu.shape
    tt = min(512, T)
    assert T % tt == 0
    row = lambda i, e: (i, 0)
    const = lambda i, e: (0, 0)
    exp3 = lambda i, e: (e, 0, 0)
    return pl.pallas_call(
        functools.partial(_moe_kernel, dn_alpha=dn_alpha),
        grid=(T // tt, E),
        in_specs=[pl.BlockSpec((tt, D), row), pl.BlockSpec(wr.shape, const), pl.BlockSpec(br.shape, const),
                  pl.BlockSpec((1, D, F2), exp3), pl.BlockSpec((1, 1, F2), exp3),
                  pl.BlockSpec((1, F2 // 2, D), exp3), pl.BlockSpec((1, 1, D), exp3),
                  pl.BlockSpec((1, D), const), pl.BlockSpec((1, D), const)],
        out_specs=pl.BlockSpec((tt, D), row),
        out_shape=jax.ShapeDtypeStruct((T, D), F32),
        scratch_shapes=[pltpu.VMEM((tt, D), BF16), pltpu.VMEM((tt, LANES), F32)],
        compiler_params=_cparams(("arbitrary", "arbitrary")),
        name="moe_ln2",
    )(h2, wr, br, wgu, bgu, wd, bd, g, b)


def _pack_pairs(x):
    w = x.shape[1] // 2
    bits = pltpu.bitcast(x.astype(BF16).astype(F32), I32)
    return (bits[:, 0:w] & jnp.int32(-65536)) | lax.shift_right_logical(bits[:, w:2 * w], 16)


def _unpack_pairs(p):
    hi = pltpu.bitcast(p & jnp.int32(-65536), F32)
    lo = pltpu.bitcast(lax.shift_left(p, 16), F32)
    return jnp.concatenate([hi, lo], axis=1)


def _mix_route_kernel(x_ref, attn_ref, conv_ref, wo_ref, g_ref, b_ref, wrT_ref, brT_ref,
                      h_ref, hpk_ref, infoT_ref, gcol_ref, cnt_ref, *, dn_alpha):
    cat = jnp.concatenate([attn_ref[...], conv_ref[...]], axis=1)
    h = _layer_norm(dn_alpha * x_ref[...] + _dot(cat, wo_ref[...]), g_ref[...], b_ref[...])
    h_ref[...] = h
    tt = h.shape[0]
    ne = cnt_ref.shape[1]
    hpk_ref[...] = _pack_pairs(h)
    logits = lax.dot_general(wrT_ref[...], h.astype(BF16), (((1,), (1,)), ((), ())),
                             preferred_element_type=F32)
    work = logits[0:ne] + jnp.tile(brT_ref[0:ne, :], (1, tt // LANES))
    sub = lax.broadcasted_iota(I32, (ne, tt), 0)
    vals, idxs, hots = [], [], []
    for _ in range(TOP_K):
        m = jnp.max(work, axis=0, keepdims=True)
        idx = jnp.min(jnp.where(work == m, sub, ne), axis=0, keepdims=True)
        hot = sub == idx
        vals.append(m)
        idxs.append(idx)
        hots.append(hot)
        work = jnp.where(hot, -jnp.inf, work)
    es = [jnp.exp(v - vals[0]) for v in vals]
    den = es[0]
    for e in es[1:]:
        den = den + e
    gates = [e / den for e in es]
    picked = jnp.zeros((ne, tt), F32)
    for hot in hots:
        picked = jnp.where(hot, 1.0, picked)
    earlier = lax.broadcasted_iota(I32, (tt, tt), 0) < lax.broadcasted_iota(I32, (tt, tt), 1)
    rank = _dot(picked.astype(BF16), jnp.where(earlier, 1.0, 0.0).astype(BF16))
    ranks = [jnp.sum(jnp.where(hot, rank, 0.0), axis=0, keepdims=True) for hot in hots]
    pad = jnp.zeros((2 * SUBLANES - 3 * TOP_K, tt), F32)
    infoT_ref[...] = jnp.concatenate([i.astype(F32) for i in idxs] + gates + ranks + [pad], axis=0)
    gcol_ref[...] = jnp.concatenate(gates + [jnp.zeros((LANES - TOP_K, tt), F32)], axis=0).T
    cnt_ref[0] = jnp.broadcast_to(jnp.sum(picked, axis=1, keepdims=True), (ne, LANES))


def _mix_route(x2, attn2, conv2, wo, g, b, wr, br, ne, *, dn_alpha):
    T, D = x2.shape
    tt = ROUTE_TILE
    aw, cw = attn2.shape[1], conv2.shape[1]
    row = lambda i: (i, 0)
    const = lambda i: (0, 0)
    wrT = wr.T
    brT = jnp.broadcast_to(br.reshape(LANES, 1), (LANES, LANES))
    return pl.pallas_call(
        functools.partial(_mix_route_kernel, dn_alpha=dn_alpha),
        grid=(T // tt,),
        in_specs=[pl.BlockSpec((tt, D), row), pl.BlockSpec((tt, aw), row), pl.BlockSpec((tt, cw), row),
                  pl.BlockSpec(wo.shape, const), pl.BlockSpec((1, D), const), pl.BlockSpec((1, D), const),
                  pl.BlockSpec(wrT.shape, const), pl.BlockSpec(brT.shape, const)],
        out_specs=(pl.BlockSpec((tt, D), row), pl.BlockSpec((tt, D // 2), row),
                   pl.BlockSpec((2 * SUBLANES, tt), lambda i: (0, i)),
                   pl.BlockSpec((tt, LANES), row), pl.BlockSpec((1, ne, LANES), lambda i: (i, 0, 0))),
        out_shape=(jax.ShapeDtypeStruct((T, D), F32), jax.ShapeDtypeStruct((T, D // 2), I32),
                   jax.ShapeDtypeStruct((2 * SUBLANES, T), F32),
                   jax.ShapeDtypeStruct((T, LANES), F32), jax.ShapeDtypeStruct((T // tt, ne, LANES), F32)),
        compiler_params=_cparams(("arbitrary",)),
        name="outproj_ln1_route",
    )(x2, attn2, conv2, wo, g, b, wrT, brT)


def _sc_gather_rows(table, idx):
    M, W = idx.shape[0], table.shape[1]
    n_workers = SC_CORES * SC_SUBCORES
    per_w = M // n_workers
    n_ch = per_w // SC_ROWS
    assert M == n_workers * n_ch * SC_ROWS
    mesh = plsc.VectorSubcoreMesh(core_axis_name="c", subcore_axis_name="s")

    @functools.partial(
        pl.kernel, mesh=mesh,
        out_type=jax.ShapeDtypeStruct((M, W), table.dtype),
        scratch_types=[pltpu.VMEM((n_ch, SC_ROWS), I32), pltpu.VMEM((SC_ROWS, W), table.dtype),
                       pltpu.SemaphoreType.DMA],
    )
    def gather(table_hbm, idx_hbm, out_hbm, idx_v, rows_v, sem):
        wid = lax.axis_index("s") * SC_CORES + lax.axis_index("c")
        pltpu.sync_copy(idx_hbm.at[wid], idx_v)

        @pl.loop(0, n_ch)
        def _(j):
            pltpu.async_copy(table_hbm.at[idx_v.at[j]], rows_v, sem).wait()
            pltpu.sync_copy(rows_v, out_hbm.at[pl.ds(wid * per_w + j * SC_ROWS, SC_ROWS)])

    return gather(table, idx.reshape(n_workers, n_ch, SC_ROWS))


def _expert_rows_kernel(te_ref, used_ref, x_ref, wgu_ref, bgu_ref, wd_ref, bd_ref, y_ref):
    @pl.when(pl.program_id(0) < used_ref[0])
    def _():
        xg = _unpack_pairs(x_ref[...]).astype(BF16)
        y_ref[...] = _pack_pairs(_expert_mlp(xg, wgu_ref, bgu_ref, wd_ref, bd_ref))


def _expert_rows(tile_expert, used_tiles, xs, wgu, bgu, wd, bd):
    R, W = xs.shape
    E, D, F2 = wgu.shape
    exp3 = lambda j, te, used: (te[j], 0, 0)
    rows = lambda j, te, used: (j, 0)
    grid_spec = pltpu.PrefetchScalarGridSpec(
        num_scalar_prefetch=2,
        grid=(R // ROW_TILE,),
        in_specs=[pl.BlockSpec((ROW_TILE, W), rows),
                  pl.BlockSpec((1, D, F2), exp3), pl.BlockSpec((1, 1, F2), exp3),
                  pl.BlockSpec((1, F2 // 2, D), exp3), pl.BlockSpec((1, 1, D), exp3)],
        out_specs=pl.BlockSpec((ROW_TILE, W), rows),
    )
    return pl.pallas_call(
        _expert_rows_kernel,
        grid_spec=grid_spec,
        out_shape=jax.ShapeDtypeStruct((R, W), I32),
        compiler_params=_cparams(("arbitrary",)),
        name="moe_expert_rows",
    )(tile_expert, used_tiles, xs, wgu, bgu, wd, bd)


def _combine_kernel(h_ref, yg_ref, gcol_ref, g_ref, b_ref, o_ref, *, dn_alpha):
    acc = dn_alpha * h_ref[...]
    for k in range(TOP_K):
        acc = acc + gcol_ref[:, k:k + 1] * _unpack_pairs(yg_ref[k])
    o_ref[...] = _layer_norm(acc, g_ref[...], b_ref[...])


def _combine(h2, yg, gcol, g, b, *, dn_alpha):
    T, D = h2.shape
    tt = 512
    row = lambda i: (i, 0)
    const = lambda i: (0, 0)
    return pl.pallas_call(
        functools.partial(_combine_kernel, dn_alpha=dn_alpha),
        grid=(T // tt,),
        in_specs=[pl.BlockSpec((tt, D), row), pl.BlockSpec((TOP_K, tt, D // 2), lambda i: (0, i, 0)),
                  pl.BlockSpec((tt, LANES), row),
                  pl.BlockSpec((1, D), const), pl.BlockSpec((1, D), const)],
        out_specs=pl.BlockSpec((tt, D), row),
        out_shape=jax.ShapeDtypeStruct((T, D), F32),
        compiler_params=_cparams(("arbitrary",)),
        name="moe_combine_ln2",
    )(h2, yg, gcol, g, b)


def _dest_kernel(infoT_ref, base_ref, dest_ref):
    tt = infoT_ref.shape[1]
    ne = base_ref.shape[1]
    base = jnp.tile(base_ref[0], (1, tt // LANES))
    sub = lax.broadcasted_iota(I32, (ne, tt), 0)
    rows = []
    for k in range(TOP_K):
        hot = sub == infoT_ref[k:k + 1, :].astype(I32)
        rows.append(jnp.sum(jnp.where(hot, base, 0.0), axis=0, keepdims=True)
                    + infoT_ref[2 * TOP_K + k:2 * TOP_K + k + 1, :])
    rows.append(jnp.zeros((SUBLANES - TOP_K, tt), F32))
    dest_ref[...] = jnp.concatenate(rows, axis=0).astype(I32)


def _dest_rows(infoT, base):
    T = infoT.shape[1]
    tt = ROUTE_TILE
    ne = base.shape[1]
    return pl.pallas_call(
        _dest_kernel,
        grid=(T // tt,),
        in_specs=[pl.BlockSpec((2 * SUBLANES, tt), lambda i: (0, i)), pl.BlockSpec((1, ne, LANES), lambda i: (i, 0, 0))],
        out_specs=pl.BlockSpec((SUBLANES, tt), lambda i: (0, i)),
        out_shape=jax.ShapeDtypeStruct((SUBLANES, T), I32),
        compiler_params=_cparams(("arbitrary",)),
        name="moe_dest_rows",
    )(infoT, base)


def _sc_scatter_rows(src, dest_km, n_rows):
    T, W = src.shape
    K = dest_km.shape[0]
    n_workers = SC_CORES * SC_SUBCORES
    per_w = T // n_workers
    n_ch = per_w // SC_ROWS
    assert T == n_workers * n_ch * SC_ROWS
    mesh = plsc.VectorSubcoreMesh(core_axis_name="c", subcore_axis_name="s")
    idx = dest_km.reshape(K, n_workers, n_ch, SC_ROWS).transpose(1, 0, 2, 3).reshape(n_workers, K * n_ch, SC_ROWS)

    @functools.partial(
        pl.kernel, mesh=mesh,
        out_type=jax.ShapeDtypeStruct((n_rows, W), src.dtype),
        scratch_types=[pltpu.VMEM((K * n_ch, SC_ROWS), I32), pltpu.VMEM((SC_ROWS, W), src.dtype),
                       pltpu.SemaphoreType.DMA],
    )
    def scatter(src_hbm, idx_hbm, out_hbm, idx_v, rows_v, sem):
        wid = lax.axis_index("s") * SC_CORES + lax.axis_index("c")
        pltpu.sync_copy(idx_hbm.at[wid], idx_v)

        @pl.loop(0, n_ch)
        def _(j):
            pltpu.sync_copy(src_hbm.at[pl.ds(wid * per_w + j * SC_ROWS, SC_ROWS)], rows_v)
            for k in range(K):
                pltpu.async_copy(rows_v, out_hbm.at[idx_v.at[k * n_ch + j]], sem).wait()

    return scatter(src, idx)


def _moe_sorted(h2, routed, wgu, bgu, wd, bd, g, b, *, dn_alpha):
    T, D = h2.shape
    E = wgu.shape[0]
    hpk, infoT, gcol, cnt = routed
    cnt = cnt[:, :, 0].astype(I32)
    total = jnp.sum(cnt, axis=0)
    padded = -(-total // ROW_TILE) * ROW_TILE
    ends = jnp.cumsum(padded)
    base = (ends - padded)[None, :] + jnp.cumsum(cnt, axis=0) - cnt
    base = jnp.broadcast_to(base.astype(F32)[:, :, None], cnt.shape + (LANES,))
    dest_km = _dest_rows(infoT, base)[0:TOP_K]
    R = T * TOP_K + E * ROW_TILE
    tile_start = jnp.arange(R // ROW_TILE, dtype=I32) * ROW_TILE
    tile_expert = jnp.minimum(jnp.sum((ends[None, :] <= tile_start[:, None]).astype(I32), axis=1), E - 1)
    used_tiles = (ends[E - 1] // ROW_TILE).reshape(1)
    xs = _sc_scatter_rows(hpk, dest_km, R)
    ys = _expert_rows(tile_expert, used_tiles, xs, wgu, bgu, wd, bd)
    yg = _sc_gather_rows(ys, dest_km.reshape(-1))
    return _combine(h2, yg.reshape(TOP_K, T, D // 2), gcol, g, b, dn_alpha=dn_alpha)


def _split_gu_kernel(w_ref, perm_ref, o_ref, *, band):
    n = w_ref.shape[2]
    for m in range(n // band):
        both = _dot(w_ref[0, :, m * band:(m + 1) * band].astype(BF16), perm_ref[...]).astype(BF16)
        o_ref[0, :, m * (band // 2):(m + 1) * (band // 2)] = both[:, 0:band // 2]
        o_ref[0, :, n // 2 + m * (band // 2):n // 2 + (m + 1) * (band // 2)] = both[:, band // 2:band]


def _split_gu(w_gu):
    E, D, N = w_gu.shape
    tr = min(512, D)
    band = min(512, N)
    j = jnp.arange(band, dtype=I32)
    src = jnp.where(j < band // 2, 2 * j, 2 * (j - band // 2) + 1)
    perm = (jnp.arange(band, dtype=I32)[:, None] == src[None, :]).astype(BF16)
    return pl.pallas_call(
        functools.partial(_split_gu_kernel, band=band),
        grid=(E, D // tr),
        in_specs=[pl.BlockSpec((1, tr, N), lambda e, r: (e, r, 0)),
                  pl.BlockSpec((band, band), lambda e, r: (0, 0))],
        out_specs=pl.BlockSpec((1, tr, N), lambda e, r: (e, r, 0)),
        out_shape=jax.ShapeDtypeStruct((E, D, N), BF16),
        compiler_params=_cparams(("arbitrary", "arbitrary")),
        name="split_gate_up",
    )(w_gu, perm)


def _prep_weights(w_in, w_conv, w_out, ln1_g, ln1_b, w_router, b_router, w_gu, b_gu, w_down, b_down,
                  ln2_g, ln2_b):
    aw = N_HEADS * HEAD_DIM
    kv = N_KV_HEADS * HEAD_DIM
    iw = N_IDX_HEADS * IDX_DIM
    D = w_in.shape[0]
    cw = D - aw
    o_q, o_k, o_v = 0, aw, aw + kv
    o_qi = o_v + kv
    o_ki = o_qi + iw
    o_wi = o_ki + IDX_DIM
    o_c = o_wi + N_IDX_HEADS
    wa = jnp.concatenate([w_in[:, o_q:o_k], w_in[:, o_qi:o_ki], w_in[:, o_k:o_v], w_in[:, o_v:o_qi]],
                         axis=1).astype(BF16)
    ws = jnp.pad(w_in[:, o_ki:o_c], ((0, 0), (0, LANES - IDX_DIM - N_IDX_HEADS))).astype(BF16)
    wc = w_in[:, o_c:o_c + 3 * cw].astype(BF16)
    E = w_router.shape[1]
    wr = jnp.pad(w_router, ((0, 0), (0, LANES - E))).astype(BF16)
    br = jnp.pad(b_router, (0, LANES - E), constant_values=MASK_VALUE).reshape(1, LANES)
    F = w_gu.shape[2] // 2
    return dict(
        wa=wa, ws=ws, wc=wc, wconv=w_conv, wo=w_out.astype(BF16),
        ln1_g=ln1_g.reshape(1, D), ln1_b=ln1_b.reshape(1, D),
        wr=wr, br=br,
        wgu=_split_gu(w_gu),
        bgu=jnp.concatenate([b_gu[:, 0::2], b_gu[:, 1::2]], axis=1).reshape(E, 1, 2 * F),
        wd=w_down.astype(BF16), bd=b_down.reshape(E, 1, D),
        ln2_g=ln2_g.reshape(1, D), ln2_b=ln2_b.reshape(1, D),
    )


def _decode_layouts(pq, small, k_all, v_all, kidx_all, kc, qb):
    B, S, _ = pq.shape
    aw = N_HEADS * HEAD_DIM
    L, kv = k_all.shape[1], k_all.shape[2]
    lp = -(-L // kc) * kc
    nc = lp // kc
    padk = lambda a: jnp.pad(a, ((0, 0), (0, lp - L), (0, 0)))
    padq = lambda a: jnp.pad(a, ((0, 0), (0, 0), (0, 0), (0, qb - S)))
    q = pq[:, :, 0:aw] * QK_SCALE
    qT = padq(q.reshape(B, S, N_HEADS, HEAD_DIM).transpose(0, 2, 3, 1)).astype(BF16)
    qiT = padq(pq[:, :, aw:].reshape(B, S, N_IDX_HEADS, IDX_DIM).transpose(0, 2, 3, 1)).astype(BF16)
    wT = jnp.pad(small[:, :, IDX_DIM:IDX_DIM + N_IDX_HEADS].transpose(0, 2, 1), ((0, 0), (0, 0), (0, qb - S)))
    kb = padk(k_all).reshape(B, nc, kc, N_KV_HEADS, HEAD_DIM).transpose(0, 1, 3, 2, 4).astype(BF16)
    vT = padk(v_all).reshape(B, nc, kc, kv).transpose(0, 1, 3, 2).astype(BF16)
    kidxb = padk(kidx_all).reshape(B, nc, kc, IDX_DIM).astype(BF16)
    knorm = jnp.sum(kb.astype(F32) ** 2, axis=-1)
    return qT, qiT, wT, kidxb, kb, vT, knorm


def _layer(x, past_k, past_v, past_kidx, conv_buf, rel_bias, w, *, dn_alpha):
    B, S, D = x.shape
    prefill = past_k is None
    outs = _project(x, conv_buf, w["wa"], w["ws"], w["wc"], w["wconv"], attn_layouts=prefill)
    k, v, small, conv, new_buf = outs[:5]
    k_idx = small if prefill else small[:, :, 0:IDX_DIM]
    if prefill:
        past, l_true = 0, S
        qb = min(QUERY_BLOCK, S)
        kb, vT, kidxb, qT, qiT, wT, knorm = outs[5:]
    else:
        past = past_k.shape[1]
        l_true = past + S
        qb = -(-S // LANES) * LANES
        qT, qiT, wT, kidxb, kb, vT, knorm = _decode_layouts(
            outs[5], small, jnp.concatenate([past_k, k], axis=1), jnp.concatenate([past_v, v], axis=1),
            jnp.concatenate([past_kidx, k_idx], axis=1), KEY_CHUNK, qb)
    offsets = (-LANES, 0)
    tabs = _bias_tables(rel_bias, offsets)
    attn = _attend(qT, qiT, wT, kidxb, kb, vT, knorm, tabs, qb=qb, past=past, l_true=l_true, offsets=offsets)
    attn = attn[:, :S]
    mix_args = (x.reshape(B * S, D), attn.reshape(B * S, -1), conv.reshape(B * S, -1), w["wo"],
                w["ln1_g"], w["ln1_b"])
    experts = (w["wgu"], w["bgu"], w["wd"], w["bd"], w["ln2_g"], w["ln2_b"])
    sc_unit = SC_CORES * SC_SUBCORES * SC_ROWS
    sortable = ((B * S) % ROUTE_TILE == 0 and (B * S * TOP_K) % sc_unit == 0
                and (N_EXPERTS * ROW_TILE) % sc_unit == 0)
    if sortable and B * S >= SORTED_MIN_TOKENS:
        h, *routed = _mix_route(*mix_args, w["wr"], w["br"], N_EXPERTS, dn_alpha=dn_alpha)
        y = _moe_sorted(h, routed, *experts, dn_alpha=dn_alpha)
    else:
        h = _mix(*mix_args, dn_alpha=dn_alpha)
        y = _moe(h, w["wr"], w["br"], *experts, dn_alpha=dn_alpha)
    return (y.reshape(B, S, D), k.reshape(B, S, N_KV_HEADS, HEAD_DIM), v.reshape(B, S, N_KV_HEADS, HEAD_DIM),
            k_idx, new_buf)


def kernel(x_prompt, x_sample, cache_k, cache_v, cache_kidx, state_conv, rel_bias, w_in, w_conv, w_out,
           ln1_g, ln1_b, w_router, b_router, w_gu, b_gu, w_down, b_down, ln2_g, ln2_b):
    depth = w_in.shape[0]
    assert depth == 1
    dn_alpha = (2 * depth) ** 0.25
    kv = N_KV_HEADS * HEAD_DIM
    w = _prep_weights(w_in[0], w_conv[0], w_out[0], ln1_g[0], ln1_b[0], w_router[0], b_router[0],
                      w_gu[0], b_gu[0], w_down[0], b_down[0], ln2_g[0], ln2_b[0])
    Bp = x_prompt.shape[0]
    cw = w_conv.shape[2]
    zero_buf = jnp.zeros((Bp, CONV_K - 1, cw), F32)
    yp, k1, v1, i1, c1 = _layer(x_prompt, None, None, None, zero_buf, rel_bias, w, dn_alpha=dn_alpha)
    Bs, P = cache_k.shape[1], cache_k.shape[2]
    ys, k2, v2, i2, c2 = _layer(x_sample, cache_k[0].reshape(Bs, P, kv), cache_v[0].reshape(Bs, P, kv),
                                cache_kidx[0], state_conv[0], rel_bias, w, dn_alpha=dn_alpha)
    return (yp, ys, k1[None], v1[None], i1[None], c1[None], k2[None], v2[None], i2[None], c2[None])
```

```python
import functools
import math

import jax
import jax.numpy as jnp
from jax import lax
from jax.experimental import pallas as pl
from jax.experimental.pallas import tpu as pltpu
from jax.experimental.pallas import tpu_sc as plsc

F32 = jnp.float32
BF16 = jnp.bfloat16
I32 = jnp.int32
I16 = jnp.int16

CHUNK = 64
N_HEADS = 8
HEAD_DIM = 64
N_KV_HEADS = 2
GROUP = N_HEADS // N_KV_HEADS
N_IDX_HEADS = 8
IDX_DIM = 32
TOPK_MAX = 256
CONV_K = 3
N_BUCKETS = 32
MAX_DISTANCE = 128
N_EXPERTS = 32
TOP_K = 4
SWIGLU_LIMIT = 7.0
SWIGLU_ALPHA = 1.702
LN_EPS = 1e-5
MASK_VALUE = -1e30
QK_SCALE = HEAD_DIM ** -0.5 * math.log2(math.e)
EXP2_SAFE = 96.0

LANES = 128
SUBLANES = 8
KEY_CHUNK = 512
QUERY_BLOCK = 512
SLAB = 64
ROW_TILE = 512
ROUTE_TILE = 1024
SC_CORES = 2
SC_SUBCORES = 16
SC_ROWS = 64
SORTED_MIN_TOKENS = 2048
INT_MIN = -(2 ** 31)
VMEM_LIMIT = 56 * 1024 * 1024


def _cparams(sem):
    return pltpu.CompilerParams(dimension_semantics=sem, vmem_limit_bytes=VMEM_LIMIT)


def _dot(a, b):
    return jnp.dot(a, b, preferred_element_type=F32)


def _layer_norm(z, g, b):
    mu = jnp.mean(z, axis=-1, keepdims=True)
    zc = z - mu
    var = jnp.mean(zc * zc, axis=-1, keepdims=True)
    return zc * lax.rsqrt(var + LN_EPS) * g + b


def _bucket_thresholds():
    nb = N_BUCKETS // 2
    max_exact = nb // 2
    out = []
    for j in range(1, nb - max_exact):
        out.append(math.ceil(max_exact * (MAX_DISTANCE / max_exact) ** (j / (nb - max_exact)) - 1e-9))
    return tuple(out)


def _bias_table_kernel(rel_ref, tab_ref, *, offsets):
    nb = N_BUCKETS // 2
    max_exact = nb // 2
    thr = _bucket_thresholds()
    ii = lax.broadcasted_iota(I32, (LANES, LANES), 0)
    jj = lax.broadcasted_iota(I32, (LANES, LANES), 1)
    for d, off in enumerate(offsets):
        rel = off + ii - jj
        n = jnp.abs(rel)
        large = jnp.full((LANES, LANES), max_exact, I32)
        for t in thr:
            large = large + jnp.where(n >= t, 1, 0)
        bucket = jnp.where(rel > 0, nb, 0) + jnp.where(n < max_exact, n, large)
        for h in range(N_HEADS):
            acc = jnp.zeros((LANES, LANES), F32)
            for b in range(N_BUCKETS):
                acc = jnp.where(bucket == b, rel_ref[b, h], acc)
            tab_ref[d, h] = (acc - rel_ref[nb - 1, h]) * math.log2(math.e)


def _bias_tables(rel_bias, offsets):
    return pl.pallas_call(
        functools.partial(_bias_table_kernel, offsets=offsets),
        out_shape=jax.ShapeDtypeStruct((len(offsets), N_HEADS, LANES, LANES), F32),
        in_specs=[pl.BlockSpec(memory_space=pltpu.SMEM)],
        out_specs=pl.BlockSpec(memory_space=pltpu.VMEM),
        name="bias_tables",
    )(rel_bias)


def _proj_kernel(x_ref, wa_ref, ws_ref, wc_ref, wconv_ref, buf_ref,
                 k_ref, v_ref, small_ref, conv_ref, nbuf_ref, *rest, ts, attn_layouts):
    s = pl.program_id(1)
    xb = x_ref[0].astype(BF16)
    pa = _dot(xb, wa_ref[...])
    ps = _dot(xb, ws_ref[...])
    pc = _dot(xb, wc_ref[...])
    aw = N_HEADS * HEAD_DIM
    iw = N_IDX_HEADS * IDX_DIM
    kv = N_KV_HEADS * HEAD_DIM
    k = pa[:, aw + iw:aw + iw + kv]
    v = pa[:, aw + iw + kv:aw + iw + 2 * kv]
    k_ref[0] = k
    v_ref[0] = v
    small_ref[0] = ps[:, 0:small_ref.shape[2]]
    scale = QK_SCALE
    if attn_layouts:
        kb_ref, vT_ref, kidxb_ref, qT_ref, qiT_ref, wT_ref, knorm_ref, carry_ref = rest
        for n in range(N_KV_HEADS):
            kb_ref[0, 0, n] = k[:, n * HEAD_DIM:(n + 1) * HEAD_DIM].astype(BF16)
        kTf = k.astype(BF16).astype(F32).T
        knorm_ref[0, 0] = jnp.concatenate(
            [jnp.sum(kTf[n * HEAD_DIM:(n + 1) * HEAD_DIM] ** 2, axis=0, keepdims=True) for n in range(N_KV_HEADS)],
            axis=0)
        vT_ref[0, 0] = v.T.astype(BF16)
        kidxb_ref[0, 0] = ps[:, 0:IDX_DIM].astype(BF16)
        qT = (pa[:, 0:aw] * scale).T.astype(BF16)
        for h in range(N_HEADS):
            qT_ref[0, h] = qT[h * HEAD_DIM:(h + 1) * HEAD_DIM]
        qiT = pa[:, aw:aw + iw].T.astype(BF16)
        for h in range(N_IDX_HEADS):
            qiT_ref[0, h] = qiT[h * IDX_DIM:(h + 1) * IDX_DIM]
        wT_ref[0] = ps.T[IDX_DIM:IDX_DIM + N_IDX_HEADS]
    else:
        pq_ref, carry_ref = rest
        pq_ref[0] = pa[:, 0:aw + iw]

    cw = pc.shape[1] // 3
    u = pc[:, cw:2 * cw] * pc[:, 2 * cw:3 * cw]

    @pl.when(s == 0)
    def _():
        carry_ref[6:8, :] = buf_ref[0]

    carry_ref[8:8 + ts, :] = u
    y = (carry_ref[6:6 + ts, :] * wconv_ref[0:1, :]
         + carry_ref[7:7 + ts, :] * wconv_ref[1:2, :]
         + u * wconv_ref[2:3, :])
    conv_ref[0] = (pc[:, 0:cw] * y).astype(BF16)
    nb = carry_ref[ts + 6:ts + 8, :]
    nbuf_ref[0] = nb
    carry_ref[6:8, :] = nb


def _project(x, conv_buf, wa, ws, wc, wconv, *, attn_layouts):
    B, S, D = x.shape
    ts = min(KEY_CHUNK, S)
    assert S % ts == 0 and S >= CONV_K - 1
    ns = S // ts
    cw = wc.shape[1] // 3
    sw = IDX_DIM if attn_layouts else LANES
    kv = N_KV_HEADS * HEAD_DIM
    aw = N_HEADS * HEAD_DIM
    iw = N_IDX_HEADS * IDX_DIM
    out_shape = [
        jax.ShapeDtypeStruct((B, S, kv), F32),
        jax.ShapeDtypeStruct((B, S, kv), F32),
        jax.ShapeDtypeStruct((B, S, sw), F32),
        jax.ShapeDtypeStruct((B, S, cw), BF16),
        jax.ShapeDtypeStruct((B, CONV_K - 1, cw), F32),
    ]
    out_specs = [
        pl.BlockSpec((1, ts, kv), lambda b, s: (b, s, 0)),
        pl.BlockSpec((1, ts, kv), lambda b, s: (b, s, 0)),
        pl.BlockSpec((1, ts, sw), lambda b, s: (b, s, 0)),
        pl.BlockSpec((1, ts, cw), lambda b, s: (b, s, 0)),
        pl.BlockSpec((1, CONV_K - 1, cw), lambda b, s: (b, 0, 0)),
    ]
    if attn_layouts:
        out_shape += [
            jax.ShapeDtypeStruct((B, ns, N_KV_HEADS, ts, HEAD_DIM), BF16),
            jax.ShapeDtypeStruct((B, ns, kv, ts), BF16),
            jax.ShapeDtypeStruct((B, ns, ts, IDX_DIM), BF16),
            jax.ShapeDtypeStruct((B, N_HEADS, HEAD_DIM, S), BF16),
            jax.ShapeDtypeStruct((B, N_IDX_HEADS, IDX_DIM, S), BF16),
            jax.ShapeDtypeStruct((B, N_IDX_HEADS, S), F32),
            jax.ShapeDtypeStruct((B, ns, N_KV_HEADS, ts), F32),
        ]
        out_specs += [
            pl.BlockSpec((1, 1, N_KV_HEADS, ts, HEAD_DIM), lambda b, s: (b, s, 0, 0, 0)),
            pl.BlockSpec((1, 1, kv, ts), lambda b, s: (b, s, 0, 0)),
            pl.BlockSpec((1, 1, ts, IDX_DIM), lambda b, s: (b, s, 0, 0)),
            pl.BlockSpec((1, N_HEADS, HEAD_DIM, ts), lambda b, s: (b, 0, 0, s)),
            pl.BlockSpec((1, N_IDX_HEADS, IDX_DIM, ts), lambda b, s: (b, 0, 0, s)),
            pl.BlockSpec((1, N_IDX_HEADS, ts), lambda b, s: (b, 0, s)),
            pl.BlockSpec((1, 1, N_KV_HEADS, ts), lambda b, s: (b, s, 0, 0)),
        ]
    else:
        out_shape += [jax.ShapeDtypeStruct((B, S, aw + iw), F32)]
        out_specs += [pl.BlockSpec((1, ts, aw + iw), lambda b, s: (b, s, 0))]
    const2 = lambda b, s: (0, 0)
    in_specs = [
        pl.BlockSpec((1, ts, D), lambda b, s: (b, s, 0)),
        pl.BlockSpec(wa.shape, const2),
        pl.BlockSpec(ws.shape, const2),
        pl.BlockSpec(wc.shape, const2),
        pl.BlockSpec(wconv.shape, const2),
        pl.BlockSpec((1, CONV_K - 1, cw), lambda b, s: (b, 0, 0)),
    ]
    return pl.pallas_call(
        functools.partial(_proj_kernel, ts=ts, attn_layouts=attn_layouts),
        grid=(B, ns),
        in_specs=in_specs,
        out_specs=tuple(out_specs),
        out_shape=tuple(out_shape),
        scratch_shapes=[pltpu.VMEM((ts + 8, cw), F32)],
        compiler_params=_cparams(("arbitrary", "arbitrary")),
        name="proj_conv",
    )(x, wa, ws, wc, wconv, conv_buf)


def _attn_kernel(qT_ref, qiT_ref, wT_ref, kidx_ref, kb_ref, vT_ref, knorm_ref, tab_ref,
                 o_ref, skey_ref, hi16_ref, lo16_ref, jb_ref, madd_ref, *head_refs,
                 qb, kc, past, l_true, topk, offsets):
    s_refs, p_refs, m_refs, l_refs, acc_refs = (head_refs[g * N_HEADS:(g + 1) * N_HEADS] for g in range(5))
    i = pl.program_id(1)
    qoff = past + i * qb
    adm_end = jnp.minimum(((qoff + qb - 1) // CHUNK + 1) * CHUNK, l_true)
    nck = (adm_end + kc - 1) // kc
    n_far = jnp.maximum(qoff - LANES, 0) // kc
    idx_bits = int(l_true).bit_length()
    groups = kc // SUBLANES

    rowi = lax.broadcasted_iota(I32, (kc, qb), 0)
    qpos = qoff + lax.broadcasted_iota(I32, (1, qb), 1)
    kmax = jnp.minimum((qpos // CHUNK + 1) * CHUNK, l_true)

    def score_body(c, carry):
        kidx_c = kidx_ref[0, c]
        sc = jnp.zeros((kc, qb), F32)
        for h in range(N_IDX_HEADS):
            d = _dot(kidx_c, qiT_ref[0, h])
            sc = sc + wT_ref[0, h:h + 1, :] * jnp.maximum(d, 0.0)
        bits = pltpu.bitcast(sc, I32)
        skey = jnp.where(bits < 0, bits ^ jnp.int32(0x7FFFFFFF), bits)
        skey = jnp.where(skey == -1, 0, skey)
        skey = jnp.where(rowi < kmax - c * kc, skey, jnp.int32(INT_MIN))
        skey_ref[c] = skey
        hi16_ref[c] = (skey >> 16).astype(I16)
        return carry

    lax.fori_loop(0, nck, score_body, 0)

    def count(pred_fn):
        def body(c, part):
            ind = jnp.where(pred_fn(c, skey_ref[c]), 1, 0)
            return part + jnp.sum(ind.reshape(groups, SUBLANES, qb), axis=0)
        part = lax.fori_loop(0, nck, body, jnp.zeros((SUBLANES, qb), I32))
        return jnp.sum(part, axis=0, keepdims=True)

    def count16(ref, pred_fn):
        pack = 2 * SUBLANES

        def body(c, part):
            ind = jnp.where(pred_fn(ref[c]), jnp.bfloat16(1), jnp.bfloat16(0)).reshape(kc // pack, pack, qb)
            terms = [ind[g] for g in range(kc // pack)]
            while len(terms) > 1:
                terms = [terms[g] + terms[g + 1] for g in range(0, len(terms), 2)]
            return part + terms[0].astype(F32)
        part = lax.fori_loop(0, nck, body, jnp.zeros((pack, qb), F32))
        return jnp.sum(part, axis=0, keepdims=True).astype(I32)

    def search16(ref, target):
        def body(it, t_u):
            cand_u = t_u | (jnp.int32(1) << (15 - it))
            cand = (cand_u - 2 ** 15).astype(I16)
            cnt = count16(ref, lambda k: k >= cand)
            return jnp.where(cnt >= target, cand_u, t_u)
        return lax.fori_loop(0, 16, body, jnp.zeros((1, qb), I32))

    hi_u = search16(hi16_ref, topk)
    hi_s = (hi_u - 2 ** 15).astype(I16)
    rest = topk - count16(hi16_ref, lambda k: k > hi_s)

    def low_body(c, carry):
        low = ((skey_ref[c] & 0xFFFF) - 2 ** 15).astype(I16)
        lo16_ref[c] = jnp.where(hi16_ref[c] == hi_s, low, jnp.int16(-2 ** 15))
        return carry

    lax.fori_loop(0, nck, low_body, 0)
    lo_u = search16(lo16_ref, rest)
    thr = (hi_u - 2 ** 15) * 2 ** 16 + lo_u
    cnt_ge = count(lambda c, sk: sk >= thr)
    cnt_gt = count(lambda c, sk: sk > thr)

    jb_ref[...] = jnp.full(jb_ref.shape, 2 ** 31 - 1, I32)

    @pl.when(jnp.max(cnt_ge) > topk)
    def _():
        need = topk - cnt_gt

        def tie_body(it, jb):
            cand = jb | (jnp.int32(1) << (idx_bits - 1 - it))
            cnt = count(lambda c, sk: jnp.logical_and(sk == thr, c * kc + rowi < cand))
            return jnp.where(cnt <= need, cand, jb)

        jb = lax.fori_loop(0, idx_bits, tie_body, jnp.zeros((1, qb), I32))
        jb_ref[...] = jnp.broadcast_to(jb, jb_ref.shape)

    jbound = jnp.where(thr == jnp.int32(INT_MIN), 0, jb_ref[0:1, :])
    thr_m1 = thr - 1

    for h in range(N_HEADS):
        l_refs[h][...] = jnp.zeros((1, qb), F32)
        acc_refs[h][...] = jnp.zeros((HEAD_DIM, qb), F32)
    n_slabs = kc // SLAB

    def fold(x, op):
        return op(x.reshape(SLAB // SUBLANES, SUBLANES, qb), axis=0)

    def near_bias(c, h):
        tiles = []
        for t in range(kc // LANES):
            row = []
            for u in range(qb // LANES):
                d_tu = c * kc + t * LANES - (qoff + u * LANES)
                delta = jnp.zeros((LANES, LANES), F32)
                for di, off in enumerate(offsets):
                    delta = jnp.where(d_tu == off, tab_ref[di, h], delta)
                row.append(delta)
            tiles.append(jnp.concatenate(row, axis=1))
        return jnp.concatenate(tiles, axis=0)

    def attn_chunk(c, near):
        sk = skey_ref[c]
        t_eff = jnp.where(rowi < jbound - c * kc, thr_m1, thr)
        madd_ref[...] = jnp.where(sk > t_eff, 0.0, MASK_VALUE)
        maccs = []
        for h in range(N_HEADS):
            x = _dot(kb_ref[0, c, h // GROUP], qT_ref[0, h]) + madd_ref[...]
            if near:
                x = x + near_bias(c, h)
            s_refs[h][...] = x
            maccs.append(jnp.max(x.reshape(groups, SUBLANES, qb), axis=0))
        alphas = []
        for h in range(N_HEADS):
            s_ref, p_ref, m_ref = s_refs[h], p_refs[h], m_refs[h]
            m_old = m_ref[...]
            m_new = jnp.maximum(m_old, jnp.max(maccs[h], axis=0, keepdims=True))
            for j in range(n_slabs):
                rows = pl.ds(j * SLAB, SLAB)
                p_ref[rows, :] = jnp.exp2(s_ref[rows, :] - m_new).astype(BF16)
            m_ref[...] = m_new
            alphas.append(jnp.exp2(m_old - m_new))
        ones = jnp.ones((2 * SUBLANES, kc), BF16)
        v_aug = [jnp.concatenate([vT_ref[0, c, n * HEAD_DIM:(n + 1) * HEAD_DIM, :], ones], axis=0)
                 for n in range(N_KV_HEADS)]
        for h in range(N_HEADS):
            pv = _dot(v_aug[h // GROUP], p_refs[h][...])
            acc_refs[h][...] = alphas[h] * acc_refs[h][...] + pv[0:HEAD_DIM]
            l_refs[h][...] = alphas[h] * l_refs[h][...] + pv[HEAD_DIM:HEAD_DIM + 1]

    def attn_chunk_fixed(c, near):
        sk = skey_ref[c]
        t_eff = jnp.where(rowi < jbound - c * kc, thr_m1, thr)
        madd_ref[...] = jnp.where(sk > t_eff, 0.0, MASK_VALUE)
        ones = jnp.ones((2 * SUBLANES, kc), BF16)
        v_aug = [jnp.concatenate([vT_ref[0, c, n * HEAD_DIM:(n + 1) * HEAD_DIM, :], ones], axis=0)
                 for n in range(N_KV_HEADS)]
        for h in range(N_HEADS):
            x = _dot(kb_ref[0, c, h // GROUP], qT_ref[0, h]) + madd_ref[...]
            if near:
                x = x + near_bias(c, h)
            p_refs[h][...] = jnp.exp2(x).astype(BF16)
        for h in range(N_HEADS):
            pv = _dot(v_aug[h // GROUP], p_refs[h][...])
            acc_refs[h][...] += pv[0:HEAD_DIM]
            l_refs[h][...] += pv[HEAD_DIM:HEAD_DIM + 1]

    def run(chunk_fn):
        lax.fori_loop(0, n_far, lambda c, carry: (chunk_fn(c, False), carry)[1], 0)
        lax.fori_loop(n_far, nck, lambda c, carry: (chunk_fn(c, True), carry)[1], 0)

    q_sq = jnp.zeros((1, qb), F32)
    for h in range(N_HEADS):
        qf = qT_ref[0, h].astype(F32)
        q_sq = jnp.maximum(q_sq, jnp.sum(qf * qf, axis=0, keepdims=True))
    bound = jnp.sqrt(jnp.max(q_sq) * jnp.max(knorm_ref[0])) + jnp.max(jnp.abs(tab_ref[...]))
    fixed_shift = bound < EXP2_SAFE

    @pl.when(fixed_shift)
    def _():
        run(attn_chunk_fixed)

    @pl.when(jnp.logical_not(fixed_shift))
    def _():
        for h in range(N_HEADS):
            m_refs[h][...] = jnp.full((1, qb), MASK_VALUE, F32)
        run(attn_chunk)

    oT = jnp.concatenate([acc_refs[h][...] / l_refs[h][...] for h in range(N_HEADS)], axis=0)
    o_ref[0] = oT.T.astype(o_ref.dtype)


def _attend(qT, qiT, wT, kidxb, kb, vT, knorm, tabs, *, qb, past, l_true, offsets):
    B, _, _, S = qT.shape
    nc, kc = vT.shape[1], vT.shape[3]
    assert S % qb == 0 and qb % LANES == 0 and past % LANES == 0 and kc % LANES == 0
    topk = min(TOPK_MAX, l_true // 4)
    kern = functools.partial(_attn_kernel, qb=qb, kc=kc, past=past, l_true=l_true,
                             topk=topk, offsets=offsets)
    in_specs = [
        pl.BlockSpec((1, N_HEADS, HEAD_DIM, qb), lambda b, i: (b, 0, 0, i)),
        pl.BlockSpec((1, N_IDX_HEADS, IDX_DIM, qb), lambda b, i: (b, 0, 0, i)),
        pl.BlockSpec((1, N_IDX_HEADS, qb), lambda b, i: (b, 0, i)),
        pl.BlockSpec((1,) + kidxb.shape[1:], lambda b, i: (b, 0, 0, 0)),
        pl.BlockSpec((1,) + kb.shape[1:], lambda b, i: (b, 0, 0, 0, 0)),
        pl.BlockSpec((1,) + vT.shape[1:], lambda b, i: (b, 0, 0, 0)),
        pl.BlockSpec((1,) + knorm.shape[1:], lambda b, i: (b, 0, 0, 0)),
        pl.BlockSpec(tabs.shape, lambda b, i: (0, 0, 0, 0)),
    ]
    return pl.pallas_call(
        kern,
        grid=(B, S // qb),
        in_specs=in_specs,
        out_specs=pl.BlockSpec((1, qb, N_HEADS * HEAD_DIM), lambda b, i: (b, i, 0)),
        out_shape=jax.ShapeDtypeStruct((B, S, N_HEADS * HEAD_DIM), BF16),
        scratch_shapes=[
            pltpu.VMEM((nc, kc, qb), I32),
            pltpu.VMEM((nc, kc, qb), I16),
            pltpu.VMEM((nc, kc, qb), I16),
            pltpu.VMEM((SUBLANES, qb), I32),
            pltpu.VMEM((kc, qb), F32),
        ] + [pltpu.VMEM((kc, qb), F32)] * N_HEADS
          + [pltpu.VMEM((kc, qb), BF16)] * N_HEADS
          + [pltpu.VMEM((1, qb), F32)] * N_HEADS
          + [pltpu.VMEM((1, qb), F32)] * N_HEADS
          + [pltpu.VMEM((HEAD_DIM, qb), F32)] * N_HEADS,
        compiler_params=_cparams(("arbitrary", "arbitrary")),
        name="dsa_attend",
    )(qT, qiT, wT, kidxb, kb, vT, knorm, tabs)


def _mix_kernel(x_ref, attn_ref, conv_ref, wo_ref, g_ref, b_ref, h_ref, *, dn_alpha):
    cat = jnp.concatenate([attn_ref[...], conv_ref[...]], axis=1)
    mixed = _dot(cat, wo_ref[...])
    h_ref[...] = _layer_norm(dn_alpha * x_ref[...] + mixed, g_ref[...], b_ref[...])


def _mix(x2, attn2, conv2, wo, g, b, *, dn_alpha):
    T, D = x2.shape
    tt = min(1024, T)
    assert T % tt == 0
    aw, cw = attn2.shape[1], conv2.shape[1]
    row = lambda i: (i, 0)
    const = lambda i: (0, 0)
    return pl.pallas_call(
        functools.partial(_mix_kernel, dn_alpha=dn_alpha),
        grid=(T // tt,),
        in_specs=[pl.BlockSpec((tt, D), row), pl.BlockSpec((tt, aw), row), pl.BlockSpec((tt, cw), row),
                  pl.BlockSpec(wo.shape, const), pl.BlockSpec((1, D), const), pl.BlockSpec((1, D), const)],
        out_specs=pl.BlockSpec((tt, D), row),
        out_shape=jax.ShapeDtypeStruct((T, D), F32),
        compiler_params=_cparams(("arbitrary",)),
        name="outproj_ln1",
    )(x2, attn2, conv2, wo, g, b)


def _route_picks(logits):
    lane = lax.broadcasted_iota(I32, logits.shape, 1)
    work = logits
    vals, hots = [], []
    for _ in range(TOP_K):
        m = jnp.max(work, axis=1, keepdims=True)
        idx = jnp.min(jnp.where(work == m, lane, LANES), axis=1, keepdims=True)
        hot = lane == idx
        vals.append(m)
        hots.append(hot)
        work = jnp.where(hot, -jnp.inf, work)
    es = [jnp.exp(v - vals[0]) for v in vals]
    den = es[0]
    for e in es[1:]:
        den = den + e
    return hots, [e / den for e in es]


def _route(logits):
    hots, gates = _route_picks(logits)
    comb = jnp.zeros(logits.shape, F32)
    for gate, hot in zip(gates, hots):
        comb = jnp.where(hot, gate, comb)
    return comb


def _expert_mlp(xb, wgu_ref, bgu_ref, wd_ref, bd_ref):
    ff = wd_ref.shape[1]
    gu = _dot(xb, wgu_ref[0]) + bgu_ref[0]
    gate = jnp.minimum(gu[:, 0:ff], SWIGLU_LIMIT)
    up = jnp.clip(gu[:, ff:2 * ff], -SWIGLU_LIMIT, SWIGLU_LIMIT)
    act = (up + 1.0) * (gate * (1.0 / (1.0 + jnp.exp(-SWIGLU_ALPHA * gate))))
    return _dot(act.astype(BF16), wd_ref[0]) + bd_ref[0]


def _moe_kernel(h_ref, wr_ref, br_ref, wgu_ref, bgu_ref, wd_ref, bd_ref, g_ref, b_ref,
                y_ref, hb_ref, comb_ref, *, dn_alpha):
    e = pl.program_id(1)
    tt = h_ref.shape[0]

    @pl.when(e == 0)
    def _():
        hb = h_ref[...].astype(BF16)
        hb_ref[...] = hb
        comb_ref[...] = _route(_dot(hb, wr_ref[...]) + br_ref[...])
        y_ref[...] = jnp.zeros(y_ref.shape, F32)

    y = _expert_mlp(hb_ref[...], wgu_ref, bgu_ref, wd_ref, bd_ref)
    lane = lax.broadcasted_iota(I32, (tt, LANES), 1)
    y_ref[...] += jnp.sum(jnp.where(lane == e, comb_ref[...], 0.0), axis=1, keepdims=True) * y

    @pl.when(e == pl.num_programs(1) - 1)
    def _():
        y_ref[...] = _layer_norm(dn_alpha * h_ref[...] + y_ref[...], g_ref[...], b_ref[...])


def _moe(h2, wr, br, wgu, bgu, wd, bd, g, b, *, dn_alpha):
    T, D = h2.shape
    E, _, F2 = wgu.shape
    tt = min(512, T)
    assert T % tt == 0
    row = lambda i, e: (i, 0)
    const = lambda i, e: (0, 0)
    exp3 = lambda i, e: (e, 0, 0)
    return pl.pallas_call(
        functools.partial(_moe_kernel, dn_alpha=dn_alpha),
        grid=(T // tt, E),
        in_specs=[pl.BlockSpec((tt, D), row), pl.BlockSpec(wr.shape, const), pl.BlockSpec(br.shape, const),
                  pl.BlockSpec((1, D, F2), exp3), pl.BlockSpec((1, 1, F2), exp3),
                  pl.BlockSpec((1, F2 // 2, D), exp3), pl.BlockSpec((1, 1, D), exp3),
                  pl.BlockSpec((1, D), const), pl.BlockSpec((1, D), const)],
        out_specs=pl.BlockSpec((tt, D), row),
        out_shape=jax.ShapeDtypeStruct((T, D), F32),
        scratch_shapes=[pltpu.VMEM((tt, D), BF16), pltpu.VMEM((tt, LANES), F32)],
        compiler_params=_cparams(("arbitrary", "arbitrary")),
        name="moe_ln2",
    )(h2, wr, br, wgu, bgu, wd, bd, g, b)


def _pack_pairs(x):
    w = x.shape[1] // 2
    bits = pltpu.bitcast(x.astype(BF16).astype(F32), I32)
    return (bits[:, 0:w] & jnp.int32(-65536)) | lax.shift_right_logical(bits[:, w:2 * w], 16)


def _unpack_pairs(p):
    hi = pltpu.bitcast(p & jnp.int32(-65536), F32)
    lo = pltpu.bitcast(lax.shift_left(p, 16), F32)
    return jnp.concatenate([hi, lo], axis=1)


def _mix_route_kernel(x_ref, attn_ref, conv_ref, wo_ref, g_ref, b_ref, wrT_ref, brT_ref,
                      h_ref, hpk_ref, infoT_ref, gcol_ref, cnt_ref, *, dn_alpha):
    cat = jnp.concatenate([attn_ref[...], conv_ref[...]], axis=1)
    h = _layer_norm(dn_alpha * x_ref[...] + _dot(cat, wo_ref[...]), g_ref[...], b_ref[...])
    h_ref[...] = h
    tt = h.shape[0]
    ne = cnt_ref.shape[1]
    hpk_ref[...] = _pack_pairs(h)
    logits = lax.dot_general(wrT_ref[...], h.astype(BF16), (((1,), (1,)), ((), ())),
                             preferred_element_type=F32)
    work = logits[0:ne] + jnp.tile(brT_ref[0:ne, :], (1, tt // LANES))
    sub = lax.broadcasted_iota(I32, (ne, tt), 0)
    vals, idxs, hots = [], [], []
    for _ in range(TOP_K):
        m = jnp.max(work, axis=0, keepdims=True)
        idx = jnp.min(jnp.where(work == m, sub, ne), axis=0, keepdims=True)
        hot = sub == idx
        vals.append(m)
        idxs.append(idx)
        hots.append(hot)
        work = jnp.where(hot, -jnp.inf, work)
    es = [jnp.exp(v - vals[0]) for v in vals]
    den = es[0]
    for e in es[1:]:
        den = den + e
    gates = [e / den for e in es]
    picked = jnp.zeros((ne, tt), F32)
    for hot in hots:
        picked = jnp.where(hot, 1.0, picked)
    earlier = lax.broadcasted_iota(I32, (tt, tt), 0) < lax.broadcasted_iota(I32, (tt, tt), 1)
    rank = _dot(picked.astype(BF16), jnp.where(earlier, 1.0, 0.0).astype(BF16))
    ranks = [jnp.sum(jnp.where(hot, rank, 0.0), axis=0, keepdims=True) for hot in hots]
    pad = jnp.zeros((2 * SUBLANES - 3 * TOP_K, tt), F32)
    infoT_ref[...] = jnp.concatenate([i.astype(F32) for i in idxs] + gates + ranks + [pad], axis=0)
    gcol_ref[...] = jnp.concatenate(gates + [jnp.zeros((LANES - TOP_K, tt), F32)], axis=0).T
    cnt_ref[0] = jnp.broadcast_to(jnp.sum(picked, axis=1, keepdims=True), (ne, LANES))


def _mix_route(x2, attn2, conv2, wo, g, b, wr, br, ne, *, dn_alpha):
    T, D = x2.shape
    tt = ROUTE_TILE
    aw, cw = attn2.shape[1], conv2.shape[1]
    row = lambda i: (i, 0)
    const = lambda i: (0, 0)
    wrT = wr.T
    brT = jnp.broadcast_to(br.reshape(LANES, 1), (LANES, LANES))
    return pl.pallas_call(
        functools.partial(_mix_route_kernel, dn_alpha=dn_alpha),
        grid=(T // tt,),
        in_specs=[pl.BlockSpec((tt, D), row), pl.BlockSpec((tt, aw), row), pl.BlockSpec((tt, cw), row),
                  pl.BlockSpec(wo.shape, const), pl.BlockSpec((1, D), const), pl.BlockSpec((1, D), const),
                  pl.BlockSpec(wrT.shape, const), pl.BlockSpec(brT.shape, const)],
        out_specs=(pl.BlockSpec((tt, D), row), pl.BlockSpec((tt, D // 2), row),
                   pl.BlockSpec((2 * SUBLANES, tt), lambda i: (0, i)),
                   pl.BlockSpec((tt, LANES), row), pl.BlockSpec((1, ne, LANES), lambda i: (i, 0, 0))),
        out_shape=(jax.ShapeDtypeStruct((T, D), F32), jax.ShapeDtypeStruct((T, D // 2), I32),
                   jax.ShapeDtypeStruct((2 * SUBLANES, T), F32),
                   jax.ShapeDtypeStruct((T, LANES), F32), jax.ShapeDtypeStruct((T // tt, ne, LANES), F32)),
        compiler_params=_cparams(("arbitrary",)),
        name="outproj_ln1_route",
    )(x2, attn2, conv2, wo, g, b, wrT, brT)


def _sc_gather_rows(table, idx):
    M, W = idx.shape[0], table.shape[1]
    n_workers = SC_CORES * SC_SUBCORES
    per_w = M // n_workers
    n_ch = per_w // SC_ROWS
    assert M == n_workers * n_ch * SC_ROWS
    mesh = plsc.VectorSubcoreMesh(core_axis_name="c", subcore_axis_name="s")

    @functools.partial(
        pl.kernel, mesh=mesh,
        out_type=jax.ShapeDtypeStruct((M, W), table.dtype),
        scratch_types=[pltpu.VMEM((n_ch, SC_ROWS), I32), pltpu.VMEM((SC_ROWS, W), table.dtype),
                       pltpu.SemaphoreType.DMA],
    )
    def gather(table_hbm, idx_hbm, out_hbm, idx_v, rows_v, sem):
        wid = lax.axis_index("s") * SC_CORES + lax.axis_index("c")
        pltpu.sync_copy(idx_hbm.at[wid], idx_v)

        @pl.loop(0, n_ch)
        def _(j):
            pltpu.async_copy(table_hbm.at[idx_v.at[j]], rows_v, sem).wait()
            pltpu.sync_copy(rows_v, out_hbm.at[pl.ds(wid * per_w + j * SC_ROWS, SC_ROWS)])

    return gather(table, idx.reshape(n_workers, n_ch, SC_ROWS))


def _expert_rows_kernel(te_ref, used_ref, x_ref, wgu_ref, bgu_ref, wd_ref, bd_ref, y_ref):
    @pl.when(pl.program_id(0) < used_ref[0])
    def _():
        xg = _unpack_pairs(x_ref[...]).astype(BF16)
        y_ref[...] = _pack_pairs(_expert_mlp(xg, wgu_ref, bgu_ref, wd_ref, bd_ref))


def _expert_rows(tile_expert, used_tiles, xs, wgu, bgu, wd, bd):
    R, W = xs.shape
    E, D, F2 = wgu.shape
    exp3 = lambda j, te, used: (te[j], 0, 0)
    rows = lambda j, te, used: (j, 0)
    grid_spec = pltpu.PrefetchScalarGridSpec(
        num_scalar_prefetch=2,
        grid=(R // ROW_TILE,),
        in_specs=[pl.BlockSpec((ROW_TILE, W), rows),
                  pl.BlockSpec((1, D, F2), exp3), pl.BlockSpec((1, 1, F2), exp3),
                  pl.BlockSpec((1, F2 // 2, D), exp3), pl.BlockSpec((1, 1, D), exp3)],
        out_specs=pl.BlockSpec((ROW_TILE, W), rows),
    )
    return pl.pallas_call(
        _expert_rows_kernel,
        grid_spec=grid_spec,
        out_shape=jax.ShapeDtypeStruct((R, W), I32),
        compiler_params=_cparams(("arbitrary",)),
        name="moe_expert_rows",
    )(tile_expert, used_tiles, xs, wgu, bgu, wd, bd)


def _combine_kernel(h_ref, yg_ref, gcol_ref, g_ref, b_ref, o_ref, *, dn_alpha):
    acc = dn_alpha * h_ref[...]
    for k in range(TOP_K):
        acc = acc + gcol_ref[:, k:k + 1] * _unpack_pairs(yg_ref[k])
    o_ref[...] = _layer_norm(acc, g_ref[...], b_ref[...])


def _combine(h2, yg, gcol, g, b, *, dn_alpha):
    T, D = h2.shape
    tt = 512
    row = lambda i: (i, 0)
    const = lambda i: (0, 0)
    return pl.pallas_call(
        functools.partial(_combine_kernel, dn_alpha=dn_alpha),
        grid=(T // tt,),
        in_specs=[pl.BlockSpec((tt, D), row), pl.BlockSpec((TOP_K, tt, D // 2), lambda i: (0, i, 0)),
                  pl.BlockSpec((tt, LANES), row),
                  pl.BlockSpec((1, D), const), pl.BlockSpec((1, D), const)],
        out_specs=pl.BlockSpec((tt, D), row),
        out_shape=jax.ShapeDtypeStruct((T, D), F32),
        compiler_params=_cparams(("arbitrary",)),
        name="moe_combine_ln2",
    )(h2, yg, gcol, g, b)


def _dest_kernel(infoT_ref, base_ref, dest_ref):
    tt = infoT_ref.shape[1]
    ne = base_ref.shape[1]
    base = jnp.tile(base_ref[0], (1, tt // LANES))
    sub = lax.broadcasted_iota(I32, (ne, tt), 0)
    rows = []
    for k in range(TOP_K):
        hot = sub == infoT_ref[k:k + 1, :].astype(I32)
        rows.append(jnp.sum(jnp.where(hot, base, 0.0), axis=0, keepdims=True)
                    + infoT_ref[2 * TOP_K + k:2 * TOP_K + k + 1, :])
    rows.append(jnp.zeros((SUBLANES - TOP_K, tt), F32))
    dest_ref[...] = jnp.concatenate(rows, axis=0).astype(I32)


def _dest_rows(infoT, base):
    T = infoT.shape[1]
    tt = ROUTE_TILE
    ne = base.shape[1]
    return pl.pallas_call(
        _dest_kernel,
        grid=(T // tt,),
        in_specs=[pl.BlockSpec((2 * SUBLANES, tt), lambda i: (0, i)), pl.BlockSpec((1, ne, LANES), lambda i: (i, 0, 0))],
        out_specs=pl.BlockSpec((SUBLANES, tt), lambda i: (0, i)),
        out_shape=jax.ShapeDtypeStruct((SUBLANES, T), I32),
        compiler_params=_cparams(("arbitrary",)),
        name="moe_dest_rows",
    )(infoT, base)


def _sc_scatter_rows(src, dest_km, n_rows):
    T, W = src.shape
    K = dest_km.shape[0]
    n_workers = SC_CORES * SC_SUBCORES
    per_w = T // n_workers
    n_ch = per_w // SC_ROWS
    assert T == n_workers * n_ch * SC_ROWS
    mesh = plsc.VectorSubcoreMesh(core_axis_name="c", subcore_axis_name="s")
    idx = dest_km.reshape(K, n_workers, n_ch, SC_ROWS).transpose(1, 0, 2, 3).reshape(n_workers, K * n_ch, SC_ROWS)

    @functools.partial(
        pl.kernel, mesh=mesh,
        out_type=jax.ShapeDtypeStruct((n_rows, W), src.dtype),
        scratch_types=[pltpu.VMEM((K * n_ch, SC_ROWS), I32), pltpu.VMEM((SC_ROWS, W), src.dtype),
                       pltpu.SemaphoreType.DMA],
    )
    def scatter(src_hbm, idx_hbm, out_hbm, idx_v, rows_v, sem):
        wid = lax.axis_index("s") * SC_CORES + lax.axis_index("c")
        pltpu.sync_copy(idx_hbm.at[wid], idx_v)

        @pl.loop(0, n_ch)
        def _(j):
            pltpu.sync_copy(src_hbm.at[pl.ds(wid * per_w + j * SC_ROWS, SC_ROWS)], rows_v)
            for k in range(K):
                pltpu.async_copy(rows_v, out_hbm.at[idx_v.at[k * n_ch + j]], sem).wait()

    return scatter(src, idx)


def _moe_sorted(h2, routed, wgu, bgu, wd, bd, g, b, *, dn_alpha):
    T, D = h2.shape
    E = wgu.shape[0]
    hpk, infoT, gcol, cnt = routed
    cnt = cnt[:, :, 0].astype(I32)
    total = jnp.sum(cnt, axis=0)
    padded = -(-total // ROW_TILE) * ROW_TILE
    ends = jnp.cumsum(padded)
    base = (ends - padded)[None, :] + jnp.cumsum(cnt, axis=0) - cnt
    base = jnp.broadcast_to(base.astype(F32)[:, :, None], cnt.shape + (LANES,))
    dest_km = _dest_rows(infoT, base)[0:TOP_K]
    R = T * TOP_K + E * ROW_TILE
    tile_start = jnp.arange(R // ROW_TILE, dtype=I32) * ROW_TILE
    tile_expert = jnp.minimum(jnp.sum((ends[None, :] <= tile_start[:, None]).astype(I32), axis=1), E - 1)
    used_tiles = (ends[E - 1] // ROW_TILE).reshape(1)
    xs = _sc_scatter_rows(hpk, dest_km, R)
    ys = _expert_rows(tile_expert, used_tiles, xs, wgu, bgu, wd, bd)
    yg = _sc_gather_rows(ys, dest_km.reshape(-1))
    return _combine(h2, yg.reshape(TOP_K, T, D // 2), gcol, g, b, dn_alpha=dn_alpha)


def _split_gu_kernel(w_ref, perm_ref, o_ref, *, band):
    n = w_ref.shape[2]
    for m in range(n // band):
        both = _dot(w_ref[0, :, m * band:(m + 1) * band].astype(BF16), perm_ref[...]).astype(BF16)
        o_ref[0, :, m * (band // 2):(m + 1) * (band // 2)] = both[:, 0:band // 2]
        o_ref[0, :, n // 2 + m * (band // 2):n // 2 + (m + 1) * (band // 2)] = both[:, band // 2:band]


def _split_gu(w_gu):
    E, D, N = w_gu.shape
    tr = min(512, D)
    band = min(512, N)
    j = jnp.arange(band, dtype=I32)
    src = jnp.where(j < band // 2, 2 * j, 2 * (j - band // 2) + 1)
    perm = (jnp.arange(band, dtype=I32)[:, None] == src[None, :]).astype(BF16)
    return pl.pallas_call(
        functools.partial(_split_gu_kernel, band=band),
        grid=(E, D // tr),
        in_specs=[pl.BlockSpec((1, tr, N), lambda e, r: (e, r, 0)),
                  pl.BlockSpec((band, band), lambda e, r: (0, 0))],
        out_specs=pl.BlockSpec((1, tr, N), lambda e, r: (e, r, 0)),
        out_shape=jax.ShapeDtypeStruct((E, D, N), BF16),
        compiler_params=_cparams(("arbitrary", "arbitrary")),
        name="split_gate_up",
    )(w_gu, perm)


def _prep_weights(w_in, w_conv, w_out, ln1_g, ln1_b, w_router, b_router, w_gu, b_gu, w_down, b_down,
                  ln2_g, ln2_b):
    aw = N_HEADS * HEAD_DIM
    kv = N_KV_HEADS * HEAD_DIM
    iw = N_IDX_HEADS * IDX_DIM
    D = w_in.shape[0]
    cw = D - aw
    o_q, o_k, o_v = 0, aw, aw + kv
    o_qi = o_v + kv
    o_ki = o_qi + iw
    o_wi = o_ki + IDX_DIM
    o_c = o_wi + N_IDX_HEADS
    wa = jnp.concatenate([w_in[:, o_q:o_k], w_in[:, o_qi:o_ki], w_in[:, o_k:o_v], w_in[:, o_v:o_qi]],
                         axis=1).astype(BF16)
    ws = jnp.pad(w_in[:, o_ki:o_c], ((0, 0), (0, LANES - IDX_DIM - N_IDX_HEADS))).astype(BF16)
    wc = w_in[:, o_c:o_c + 3 * cw].astype(BF16)
    E = w_router.shape[1]
    wr = jnp.pad(w_router, ((0, 0), (0, LANES - E))).astype(BF16)
    br = jnp.pad(b_router, (0, LANES - E), constant_values=MASK_VALUE).reshape(1, LANES)
    F = w_gu.shape[2] // 2
    return dict(
        wa=wa, ws=ws, wc=wc, wconv=w_conv, wo=w_out.astype(BF16),
        ln1_g=ln1_g.reshape(1, D), ln1_b=ln1_b.reshape(1, D),
        wr=wr, br=br,
        wgu=_split_gu(w_gu),
        bgu=jnp.concatenate([b_gu[:, 0::2], b_gu[:, 1::2]], axis=1).reshape(E, 1, 2 * F),
        wd=w_down.astype(BF16), bd=b_down.reshape(E, 1, D),
        ln2_g=ln2_g.reshape(1, D), ln2_b=ln2_b.reshape(1, D),
    )


def _decode_layouts(pq, small, k_all, v_all, kidx_all, kc, qb):
    B, S, _ = pq.shape
    aw = N_HEADS * HEAD_DIM
    L, kv = k_all.shape[1], k_all.shape[2]
    lp = -(-L // kc) * kc
    nc = lp // kc
    padk = lambda a: jnp.pad(a, ((0, 0), (0, lp - L), (0, 0)))
    padq = lambda a: jnp.pad(a, ((0, 0), (0, 0), (0, 0), (0, qb - S)))
    q = pq[:, :, 0:aw] * QK_SCALE
    qT = padq(q.reshape(B, S, N_HEADS, HEAD_DIM).transpose(0, 2, 3, 1)).astype(BF16)
    qiT = padq(pq[:, :, aw:].reshape(B, S, N_IDX_HEADS, IDX_DIM).transpose(0, 2, 3, 1)).astype(BF16)
    wT = jnp.pad(small[:, :, IDX_DIM:IDX_DIM + N_IDX_HEADS].transpose(0, 2, 1), ((0, 0), (0, 0), (0, qb - S)))
    kb = padk(k_all).reshape(B, nc, kc, N_KV_HEADS, HEAD_DIM).transpose(0, 1, 3, 2, 4).astype(BF16)
    vT = padk(v_all).reshape(B, nc, kc, kv).transpose(0, 1, 3, 2).astype(BF16)
    kidxb = padk(kidx_all).reshape(B, nc, kc, IDX_DIM).astype(BF16)
    knorm = jnp.sum(kb.astype(F32) ** 2, axis=-1)
    return qT, qiT, wT, kidxb, kb, vT, knorm


def _layer(x, past_k, past_v, past_kidx, conv_buf, rel_bias, w, *, dn_alpha):
    B, S, D = x.shape
    prefill = past_k is None
    outs = _project(x, conv_buf, w["wa"], w["ws"], w["wc"], w["wconv"], attn_layouts=prefill)
    k, v, small, conv, new_buf = outs[:5]
    k_idx = small if prefill else small[:, :, 0:IDX_DIM]
    if prefill:
        past, l_true = 0, S
        qb = min(QUERY_BLOCK, S)
        kb, vT, kidxb, qT, qiT, wT, knorm = outs[5:]
    else:
        past = past_k.shape[1]
        l_true = past + S
        qb = -(-S // LANES) * LANES
        qT, qiT, wT, kidxb, kb, vT, knorm = _decode_layouts(
            outs[5], small, jnp.concatenate([past_k, k], axis=1), jnp.concatenate([past_v, v], axis=1),
            jnp.concatenate([past_kidx, k_idx], axis=1), KEY_CHUNK, qb)
    offsets = (-LANES, 0)
    tabs = _bias_tables(rel_bias, offsets)
    attn = _attend(qT, qiT, wT, kidxb, kb, vT, knorm, tabs, qb=qb, past=past, l_true=l_true, offsets=offsets)
    attn = attn[:, :S]
    mix_args = (x.reshape(B * S, D), attn.reshape(B * S, -1), conv.reshape(B * S, -1), w["wo"],
                w["ln1_g"], w["ln1_b"])
    experts = (w["wgu"], w["bgu"], w["wd"], w["bd"], w["ln2_g"], w["ln2_b"])
    sc_unit = SC_CORES * SC_SUBCORES * SC_ROWS
    sortable = ((B * S) % ROUTE_TILE == 0 and (B * S * TOP_K) % sc_unit == 0
                and (N_EXPERTS * ROW_TILE) % sc_unit == 0)
    if sortable and B * S >= SORTED_MIN_TOKENS:
        h, *routed = _mix_route(*mix_args, w["wr"], w["br"], N_EXPERTS, dn_alpha=dn_alpha)
        y = _moe_sorted(h, routed, *experts, dn_alpha=dn_alpha)
    else:
        h = _mix(*mix_args, dn_alpha=dn_alpha)
        y = _moe(h, w["wr"], w["br"], *experts, dn_alpha=dn_alpha)
    return (y.reshape(B, S, D), k.reshape(B, S, N_KV_HEADS, HEAD_DIM), v.reshape(B, S, N_KV_HEADS, HEAD_DIM),
            k_idx, new_buf)


def kernel(x_prompt, x_sample, cache_k, cache_v, cache_kidx, state_conv, rel_bias, w_in, w_conv, w_out,
           ln1_g, ln1_b, w_router, b_router, w_gu, b_gu, w_down, b_down, ln2_g, ln2_b):
    depth = w_in.shape[0]
    assert depth == 1
    dn_alpha = (2 * depth) ** 0.25
    kv = N_KV_HEADS * HEAD_DIM
    w = _prep_weights(w_in[0], w_conv[0], w_out[0], ln1_g[0], ln1_b[0], w_router[0], b_router[0],
                      w_gu[0], b_gu[0], w_down[0], b_down[0], ln2_g[0], ln2_b[0])
    Bp = x_prompt.shape[0]
    cw = w_conv.shape[2]
    zero_buf = jnp.zeros((Bp, CONV_K - 1, cw), F32)
    yp, k1, v1, i1, c1 = _layer(x_prompt, None, None, None, zero_buf, rel_bias, w, dn_alpha=dn_alpha)
    Bs, P = cache_k.shape[1], cache_k.shape[2]
    ys, k2, v2, i2, c2 = _layer(x_sample, cache_k[0].reshape(Bs, P, kv), cache_v[0].reshape(Bs, P, kv),
                                cache_kidx[0], state_conv[0], rel_bias, w, dn_alpha=dn_alpha)
    return (yp, ys, k1[None], v1[None], i1[None], c1[None], k2[None], v2[None], i2[None], c2[None])
```

```python
import functools
import math

import jax
import jax.numpy as jnp
from jax import lax
from jax.experimental import pallas as pl
from jax.experimental.pallas import tpu as pltpu
from jax.experimental.pallas import tpu_sc as plsc

F32 = jnp.float32
BF16 = jnp.bfloat16
I32 = jnp.int32
I16 = jnp.int16

CHUNK = 64
N_HEADS = 8
HEAD_DIM = 64
N_KV_HEADS = 2
GROUP = N_HEADS // N_KV_HEADS
N_IDX_HEADS = 8
IDX_DIM = 32
TOPK_MAX = 256
CONV_K = 3
N_BUCKETS = 32
MAX_DISTANCE = 128
N_EXPERTS = 32
TOP_K = 4
SWIGLU_LIMIT = 7.0
SWIGLU_ALPHA = 1.702
LN_EPS = 1e-5
MASK_VALUE = -1e30
QK_SCALE = HEAD_DIM ** -0.5 * math.log2(math.e)
EXP2_SAFE = 96.0

LANES = 128
SUBLANES = 8
KEY_CHUNK = 512
QUERY_BLOCK = 512
SLAB = 64
ROW_TILE = 512
ROUTE_TILE = 1024
SC_CORES = 2
SC_SUBCORES = 16
SC_ROWS = 64
SORTED_MIN_TOKENS = 2048
INT_MIN = -(2 ** 31)
VMEM_LIMIT = 56 * 1024 * 1024


def _cparams(sem):
    return pltpu.CompilerParams(dimension_semantics=sem, vmem_limit_bytes=VMEM_LIMIT)


def _dot(a, b):
    return jnp.dot(a, b, preferred_element_type=F32)


def _layer_norm(z, g, b):
    mu = jnp.mean(z, axis=-1, keepdims=True)
    zc = z - mu
    var = jnp.mean(zc * zc, axis=-1, keepdims=True)
    return zc * lax.rsqrt(var + LN_EPS) * g + b


def _bucket_thresholds():
    nb = N_BUCKETS // 2
    max_exact = nb // 2
    out = []
    for j in range(1, nb - max_exact):
        out.append(math.ceil(max_exact * (MAX_DISTANCE / max_exact) ** (j / (nb - max_exact)) - 1e-9))
    return tuple(out)


def _bias_table_kernel(rel_ref, tab_ref, *, offsets):
    nb = N_BUCKETS // 2
    max_exact = nb // 2
    thr = _bucket_thresholds()
    ii = lax.broadcasted_iota(I32, (LANES, LANES), 0)
    jj = lax.broadcasted_iota(I32, (LANES, LANES), 1)
    for d, off in enumerate(offsets):
        rel = off + ii - jj
        n = jnp.abs(rel)
        large = jnp.full((LANES, LANES), max_exact, I32)
        for t in thr:
            large = large + jnp.where(n >= t, 1, 0)
        bucket = jnp.where(rel > 0, nb, 0) + jnp.where(n < max_exact, n, large)
        for h in range(N_HEADS):
            acc = jnp.zeros((LANES, LANES), F32)
            for b in range(N_BUCKETS):
                acc = jnp.where(bucket == b, rel_ref[b, h], acc)
            tab_ref[d, h] = (acc - rel_ref[nb - 1, h]) * math.log2(math.e)


def _bias_tables(rel_bias, offsets):
    return pl.pallas_call(
        functools.partial(_bias_table_kernel, offsets=offsets),
        out_shape=jax.ShapeDtypeStruct((len(offsets), N_HEADS, LANES, LANES), F32),
        in_specs=[pl.BlockSpec(memory_space=pltpu.SMEM)],
        out_specs=pl.BlockSpec(memory_space=pltpu.VMEM),
        name="bias_tables",
    )(rel_bias)


def _proj_kernel(x_ref, wa_ref, ws_ref, wc_ref, wconv_ref, buf_ref,
                 k_ref, v_ref, small_ref, conv_ref, nbuf_ref, *rest, ts, attn_layouts):
    s = pl.program_id(1)
    xb = x_ref[0].astype(BF16)
    pa = _dot(xb, wa_ref[...])
    ps = _dot(xb, ws_ref[...])
    pc = _dot(xb, wc_ref[...])
    aw = N_HEADS * HEAD_DIM
    iw = N_IDX_HEADS * IDX_DIM
    kv = N_KV_HEADS * HEAD_DIM
    k = pa[:, aw + iw:aw + iw + kv]
    v = pa[:, aw + iw + kv:aw + iw + 2 * kv]
    k_ref[0] = k
    v_ref[0] = v
    small_ref[0] = ps[:, 0:small_ref.shape[2]]
    scale = QK_SCALE
    if attn_layouts:
        kb_ref, vT_ref, kidxb_ref, qT_ref, qiT_ref, wT_ref, knorm_ref, carry_ref = rest
        for n in range(N_KV_HEADS):
            kb_ref[0, 0, n] = k[:, n * HEAD_DIM:(n + 1) * HEAD_DIM].astype(BF16)
        kTf = k.astype(BF16).astype(F32).T
        knorm_ref[0, 0] = jnp.concatenate(
            [jnp.sum(kTf[n * HEAD_DIM:(n + 1) * HEAD_DIM] ** 2, axis=0, keepdims=True) for n in range(N_KV_HEADS)],
            axis=0)
        vT_ref[0, 0] = v.T.astype(BF16)
        kidxb_ref[0, 0] = ps[:, 0:IDX_DIM].astype(BF16)
        qT = (pa[:, 0:aw] * scale).T.astype(BF16)
        for h in range(N_HEADS):
            qT_ref[0, h] = qT[h * HEAD_DIM:(h + 1) * HEAD_DIM]
        qiT = pa[:, aw:aw + iw].T.astype(BF16)
        for h in range(N_IDX_HEADS):
            qiT_ref[0, h] = qiT[h * IDX_DIM:(h + 1) * IDX_DIM]
        wT_ref[0] = ps.T[IDX_DIM:IDX_DIM + N_IDX_HEADS]
    else:
        pq_ref, carry_ref = rest
        pq_ref[0] = pa[:, 0:aw + iw]

    cw = pc.shape[1] // 3
    u = pc[:, cw:2 * cw] * pc[:, 2 * cw:3 * cw]

    @pl.when(s == 0)
    def _():
        carry_ref[6:8, :] = buf_ref[0]

    carry_ref[8:8 + ts, :] = u
    y = (carry_ref[6:6 + ts, :] * wconv_ref[0:1, :]
         + carry_ref[7:7 + ts, :] * wconv_ref[1:2, :]
         + u * wconv_ref[2:3, :])
    conv_ref[0] = (pc[:, 0:cw] * y).astype(BF16)
    nb = carry_ref[ts + 6:ts + 8, :]
    nbuf_ref[0] = nb
    carry_ref[6:8, :] = nb


def _project(x, conv_buf, wa, ws, wc, wconv, *, attn_layouts):
    B, S, D = x.shape
    ts = min(KEY_CHUNK, S)
    assert S % ts == 0 and S >= CONV_K - 1
    ns = S // ts
    cw = wc.shape[1] // 3
    sw = IDX_DIM if attn_layouts else LANES
    kv = N_KV_HEADS * HEAD_DIM
    aw = N_HEADS * HEAD_DIM
    iw = N_IDX_HEADS * IDX_DIM
    out_shape = [
        jax.ShapeDtypeStruct((B, S, kv), F32),
        jax.ShapeDtypeStruct((B, S, kv), F32),
        jax.ShapeDtypeStruct((B, S, sw), F32),
        jax.ShapeDtypeStruct((B, S, cw), BF16),
        jax.ShapeDtypeStruct((B, CONV_K - 1, cw), F32),
    ]
    out_specs = [
        pl.BlockSpec((1, ts, kv), lambda b, s: (b, s, 0)),
        pl.BlockSpec((1, ts, kv), lambda b, s: (b, s, 0)),
        pl.BlockSpec((1, ts, sw), lambda b, s: (b, s, 0)),
        pl.BlockSpec((1, ts, cw), lambda b, s: (b, s, 0)),
        pl.BlockSpec((1, CONV_K - 1, cw), lambda b, s: (b, 0, 0)),
    ]
    if attn_layouts:
        out_shape += [
            jax.ShapeDtypeStruct((B, ns, N_KV_HEADS, ts, HEAD_DIM), BF16),
            jax.ShapeDtypeStruct((B, ns, kv, ts), BF16),
            jax.ShapeDtypeStruct((B, ns, ts, IDX_DIM), BF16),
            jax.ShapeDtypeStruct((B, N_HEADS, HEAD_DIM, S), BF16),
            jax.ShapeDtypeStruct((B, N_IDX_HEADS, IDX_DIM, S), BF16),
            jax.ShapeDtypeStruct((B, N_IDX_HEADS, S), F32),
            jax.ShapeDtypeStruct((B, ns, N_KV_HEADS, ts), F32),
        ]
        out_specs += [
            pl.BlockSpec((1, 1, N_KV_HEADS, ts, HEAD_DIM), lambda b, s: (b, s, 0, 0, 0)),
            pl.BlockSpec((1, 1, kv, ts), lambda b, s: (b, s, 0, 0)),
            pl.BlockSpec((1, 1, ts, IDX_DIM), lambda b, s: (b, s, 0, 0)),
            pl.BlockSpec((1, N_HEADS, HEAD_DIM, ts), lambda b, s: (b, 0, 0, s)),
            pl.BlockSpec((1, N_IDX_HEADS, IDX_DIM, ts), lambda b, s: (b, 0, 0, s)),
            pl.BlockSpec((1, N_IDX_HEADS, ts), lambda b, s: (b, 0, s)),
            pl.BlockSpec((1, 1, N_KV_HEADS, ts), lambda b, s: (b, s, 0, 0)),
        ]
    else:
        out_shape += [jax.ShapeDtypeStruct((B, S, aw + iw), F32)]
        out_specs += [pl.BlockSpec((1, ts, aw + iw), lambda b, s: (b, s, 0))]
    const2 = lambda b, s: (0, 0)
    in_specs = [
        pl.BlockSpec((1, ts, D), lambda b, s: (b, s, 0)),
        pl.BlockSpec(wa.shape, const2),
        pl.BlockSpec(ws.shape, const2),
        pl.BlockSpec(wc.shape, const2),
        pl.BlockSpec(wconv.shape, const2),
        pl.BlockSpec((1, CONV_K - 1, cw), lambda b, s: (b, 0, 0)),
    ]
    return pl.pallas_call(
        functools.partial(_proj_kernel, ts=ts, attn_layouts=attn_layouts),
        grid=(B, ns),
        in_specs=in_specs,
        out_specs=tuple(out_specs),
        out_shape=tuple(out_shape),
        scratch_shapes=[pltpu.VMEM((ts + 8, cw), F32)],
        compiler_params=_cparams(("arbitrary", "arbitrary")),
        name="proj_conv",
    )(x, wa, ws, wc, wconv, conv_buf)


def _attn_kernel(qT_ref, qiT_ref, wT_ref, kidx_ref, kb_ref, vT_ref, knorm_ref, tab_ref,
                 o_ref, skey_ref, hi16_ref, lo16_ref, jb_ref, madd_ref, *head_refs,
                 qb, kc, past, l_true, topk, offsets):
    s_refs, p_refs, m_refs, l_refs, acc_refs = (head_refs[g * N_HEADS:(g + 1) * N_HEADS] for g in range(5))
    i = pl.program_id(1)
    qoff = past + i * qb
    adm_end = jnp.minimum(((qoff + qb - 1) // CHUNK + 1) * CHUNK, l_true)
    nck = (adm_end + kc - 1) // kc
    idx_bits = int(l_true).bit_length()
    groups = kc // SUBLANES

    rowi = lax.broadcasted_iota(I32, (kc, qb), 0)
    qpos = qoff + lax.broadcasted_iota(I32, (1, qb), 1)
    kmax = jnp.minimum((qpos // CHUNK + 1) * CHUNK, l_true)

    def score_body(c, carry):
        kidx_c = kidx_ref[0, c]
        sc = jnp.zeros((kc, qb), F32)
        for h in range(N_IDX_HEADS):
            d = _dot(kidx_c, qiT_ref[0, h])
            sc = sc + wT_ref[0, h:h + 1, :] * jnp.maximum(d, 0.0)
        bits = pltpu.bitcast(sc, I32)
        skey = jnp.where(bits < 0, bits ^ jnp.int32(0x7FFFFFFF), bits)
        skey = jnp.where(skey == -1, 0, skey)
        skey = jnp.where(rowi < kmax - c * kc, skey, jnp.int32(INT_MIN))
        skey_ref[c] = skey
        hi16_ref[c] = (skey >> 16).astype(I16)
        return carry

    lax.fori_loop(0, nck, score_body, 0)

    def count(pred_fn):
        def body(c, part):
            ind = jnp.where(pred_fn(c, skey_ref[c]), 1, 0)
            return part + jnp.sum(ind.reshape(groups, SUBLANES, qb), axis=0)
        part = lax.fori_loop(0, nck, body, jnp.zeros((SUBLANES, qb), I32))
        return jnp.sum(part, axis=0, keepdims=True)

    def count16(ref, pred_fn):
        pack = 2 * SUBLANES

        def body(c, part):
            ind = jnp.where(pred_fn(ref[c]), jnp.bfloat16(1), jnp.bfloat16(0)).reshape(kc // pack, pack, qb)
            terms = [ind[g] for g in range(kc // pack)]
            while len(terms) > 1:
                terms = [terms[g] + terms[g + 1] for g in range(0, len(terms), 2)]
            return part + terms[0].astype(F32)
        part = lax.fori_loop(0, nck, body, jnp.zeros((pack, qb), F32))
        return jnp.sum(part, axis=0, keepdims=True).astype(I32)

    def search16(ref, target, count_all):
        def body(it, carry):
            t_u, c_t = carry
            cand_u = t_u | (jnp.int32(1) << (15 - it))
            cand = (cand_u - 2 ** 15).astype(I16)
            cnt = count16(ref, lambda k: k >= cand)
            keep = cnt >= target
            return jnp.where(keep, cand_u, t_u), jnp.where(keep, cnt, c_t)
        return lax.fori_loop(0, 16, body, (jnp.zeros((1, qb), I32), count_all))

    stored = jnp.broadcast_to(nck * kc, (1, qb)).astype(I32)
    hi_u, n_ge_hi = search16(hi16_ref, topk, stored)
    hi_s = (hi_u - 2 ** 15).astype(I16)
    n_gt_hi = count16(hi16_ref, lambda k: k > hi_s)
    rest = topk - n_gt_hi

    def low_body(c, carry):
        low = ((skey_ref[c] & 0xFFFF) - 2 ** 15).astype(I16)
        lo16_ref[c] = jnp.where(hi16_ref[c] == hi_s, low, jnp.int16(-2 ** 15))
        return carry

    lax.fori_loop(0, nck, low_body, 0)
    lo_u, n_ge_lo = search16(lo16_ref, rest, n_ge_hi - n_gt_hi)
    thr = (hi_u - 2 ** 15) * 2 ** 16 + lo_u
    lo_s = (lo_u - 2 ** 15).astype(I16)
    cnt_ge = n_gt_hi + n_ge_lo
    cnt_gt = n_gt_hi + count16(lo16_ref, lambda k: k > lo_s)

    jb_ref[...] = jnp.full(jb_ref.shape, 2 ** 31 - 1, I32)

    @pl.when(jnp.max(cnt_ge) > topk)
    def _():
        need = topk - cnt_gt

        def tie_body(it, jb):
            cand = jb | (jnp.int32(1) << (idx_bits - 1 - it))
            cnt = count(lambda c, sk: jnp.logical_and(sk == thr, c * kc + rowi < cand))
            return jnp.where(cnt <= need, cand, jb)

        jb = lax.fori_loop(0, idx_bits, tie_body, jnp.zeros((1, qb), I32))
        jb_ref[...] = jnp.broadcast_to(jb, jb_ref.shape)

    jbound = jnp.where(thr == jnp.int32(INT_MIN), 0, jb_ref[0:1, :])
    thr_m1 = thr - 1

    for h in range(N_HEADS):
        l_refs[h][...] = jnp.zeros((1, qb), F32)
        acc_refs[h][...] = jnp.zeros((HEAD_DIM, qb), F32)
    n_slabs = kc // SLAB

    def fold(x, op):
        return op(x.reshape(SLAB // SUBLANES, SUBLANES, qb), axis=0)

    def near_bias(chunk_offset, h):
        tiles = []
        for t in range(kc // LANES):
            row = []
            for u in range(qb // LANES):
                d_tu = chunk_offset + (t - u) * LANES
                row.append(tab_ref[offsets.index(d_tu), h] if d_tu in offsets else jnp.zeros((LANES, LANES), F32))
            tiles.append(jnp.concatenate(row, axis=1))
        return jnp.concatenate(tiles, axis=0)

    def attn_chunk(c, near):
        sk = skey_ref[c]
        t_eff = jnp.where(rowi < jbound - c * kc, thr_m1, thr)
        madd_ref[...] = jnp.where(sk > t_eff, 0.0, MASK_VALUE)
        maccs = []
        for h in range(N_HEADS):
            x = _dot(kb_ref[0, c, h // GROUP], qT_ref[0, h]) + madd_ref[...]
            if near is not None:
                x = x + near_bias(near, h)
            s_refs[h][...] = x
            maccs.append(jnp.max(x.reshape(groups, SUBLANES, qb), axis=0))
        alphas = []
        for h in range(N_HEADS):
            s_ref, p_ref, m_ref = s_refs[h], p_refs[h], m_refs[h]
            m_old = m_ref[...]
            m_new = jnp.maximum(m_old, jnp.max(maccs[h], axis=0, keepdims=True))
            for j in range(n_slabs):
                rows = pl.ds(j * SLAB, SLAB)
                p_ref[rows, :] = jnp.exp2(s_ref[rows, :] - m_new).astype(BF16)
            m_ref[...] = m_new
            alphas.append(jnp.exp2(m_old - m_new))
        ones = jnp.ones((2 * SUBLANES, kc), BF16)
        v_aug = [jnp.concatenate([vT_ref[0, c, n * HEAD_DIM:(n + 1) * HEAD_DIM, :], ones], axis=0)
                 for n in range(N_KV_HEADS)]
        for h in range(N_HEADS):
            pv = _dot(v_aug[h // GROUP], p_refs[h][...])
            acc_refs[h][...] = alphas[h] * acc_refs[h][...] + pv[0:HEAD_DIM]
            l_refs[h][...] = alphas[h] * l_refs[h][...] + pv[HEAD_DIM:HEAD_DIM + 1]

    def attn_chunk_fixed(c, near):
        sk = skey_ref[c]
        t_eff = jnp.where(rowi < jbound - c * kc, thr_m1, thr)
        madd_ref[...] = jnp.where(sk > t_eff, 0.0, MASK_VALUE)
        ones = jnp.ones((2 * SUBLANES, kc), BF16)
        v_aug = [jnp.concatenate([vT_ref[0, c, n * HEAD_DIM:(n + 1) * HEAD_DIM, :], ones], axis=0)
                 for n in range(N_KV_HEADS)]
        for h in range(N_HEADS):
            x = _dot(kb_ref[0, c, h // GROUP], qT_ref[0, h]) + madd_ref[...]
            if near is not None:
                x = x + near_bias(near, h)
            p_refs[h][...] = jnp.exp2(x).astype(BF16)
        for h in range(N_HEADS):
            pv = _dot(v_aug[h // GROUP], p_refs[h][...])
            acc_refs[h][...] += pv[0:HEAD_DIM]
            l_refs[h][...] += pv[HEAD_DIM:HEAD_DIM + 1]

    def run(chunk_fn):
        c_diag = qoff // kc
        lax.fori_loop(0, c_diag - 1, lambda c, carry: (chunk_fn(c, None), carry)[1], 0)

        @pl.when(c_diag >= 1)
        def _():
            chunk_fn(c_diag - 1, -kc)

        chunk_fn(c_diag, 0)

    q_sq = jnp.zeros((1, qb), F32)
    for h in range(N_HEADS):
        qf = qT_ref[0, h].astype(F32)
        q_sq = jnp.maximum(q_sq, jnp.sum(qf * qf, axis=0, keepdims=True))
    bound = jnp.sqrt(jnp.max(q_sq) * jnp.max(knorm_ref[0])) + jnp.max(jnp.abs(tab_ref[...]))
    fixed_shift = bound < EXP2_SAFE

    @pl.when(fixed_shift)
    def _():
        run(attn_chunk_fixed)

    @pl.when(jnp.logical_not(fixed_shift))
    def _():
        for h in range(N_HEADS):
            m_refs[h][...] = jnp.full((1, qb), MASK_VALUE, F32)
        run(attn_chunk)

    oT = jnp.concatenate([acc_refs[h][...] / l_refs[h][...] for h in range(N_HEADS)], axis=0)
    o_ref[0] = oT.T.astype(o_ref.dtype)


def _attend(qT, qiT, wT, kidxb, kb, vT, knorm, tabs, *, qb, past, l_true, offsets):
    B, _, _, S = qT.shape
    nc, kc = vT.shape[1], vT.shape[3]
    assert S % qb == 0 and qb % LANES == 0 and kc % LANES == 0
    assert past % kc == 0 and qb <= kc and (qb == kc or S == qb)
    topk = min(TOPK_MAX, l_true // 4)
    kern = functools.partial(_attn_kernel, qb=qb, kc=kc, past=past, l_true=l_true,
                             topk=topk, offsets=offsets)
    in_specs = [
        pl.BlockSpec((1, N_HEADS, HEAD_DIM, qb), lambda b, i: (b, 0, 0, i)),
        pl.BlockSpec((1, N_IDX_HEADS, IDX_DIM, qb), lambda b, i: (b, 0, 0, i)),
        pl.BlockSpec((1, N_IDX_HEADS, qb), lambda b, i: (b, 0, i)),
        pl.BlockSpec((1,) + kidxb.shape[1:], lambda b, i: (b, 0, 0, 0)),
        pl.BlockSpec((1,) + kb.shape[1:], lambda b, i: (b, 0, 0, 0, 0)),
        pl.BlockSpec((1,) + vT.shape[1:], lambda b, i: (b, 0, 0, 0)),
        pl.BlockSpec((1,) + knorm.shape[1:], lambda b, i: (b, 0, 0, 0)),
        pl.BlockSpec(tabs.shape, lambda b, i: (0, 0, 0, 0)),
    ]
    return pl.pallas_call(
        kern,
        grid=(B, S // qb),
        in_specs=in_specs,
        out_specs=pl.BlockSpec((1, qb, N_HEADS * HEAD_DIM), lambda b, i: (b, i, 0)),
        out_shape=jax.ShapeDtypeStruct((B, S, N_HEADS * HEAD_DIM), BF16),
        scratch_shapes=[
            pltpu.VMEM((nc, kc, qb), I32),
            pltpu.VMEM((nc, kc, qb), I16),
            pltpu.VMEM((nc, kc, qb), I16),
            pltpu.VMEM((SUBLANES, qb), I32),
            pltpu.VMEM((kc, qb), F32),
        ] + [pltpu.VMEM((kc, qb), F32)] * N_HEADS
          + [pltpu.VMEM((kc, qb), BF16)] * N_HEADS
          + [pltpu.VMEM((1, qb), F32)] * N_HEADS
          + [pltpu.VMEM((1, qb), F32)] * N_HEADS
          + [pltpu.VMEM((HEAD_DIM, qb), F32)] * N_HEADS,
        compiler_params=_cparams(("arbitrary", "arbitrary")),
        name="dsa_attend",
    )(qT, qiT, wT, kidxb, kb, vT, knorm, tabs)


def _mix_kernel(x_ref, attn_ref, conv_ref, wo_ref, g_ref, b_ref, h_ref, *, dn_alpha):
    cat = jnp.concatenate([attn_ref[...], conv_ref[...]], axis=1)
    mixed = _dot(cat, wo_ref[...])
    h_ref[...] = _layer_norm(dn_alpha * x_ref[...] + mixed, g_ref[...], b_ref[...])


def _mix(x2, attn2, conv2, wo, g, b, *, dn_alpha):
    T, D = x2.shape
    tt = min(1024, T)
    assert T % tt == 0
    aw, cw = attn2.shape[1], conv2.shape[1]
    row = lambda i: (i, 0)
    const = lambda i: (0, 0)
    return pl.pallas_call(
        functools.partial(_mix_kernel, dn_alpha=dn_alpha),
        grid=(T // tt,),
        in_specs=[pl.BlockSpec((tt, D), row), pl.BlockSpec((tt, aw), row), pl.BlockSpec((tt, cw), row),
                  pl.BlockSpec(wo.shape, const), pl.BlockSpec((1, D), const), pl.BlockSpec((1, D), const)],
        out_specs=pl.BlockSpec((tt, D), row),
        out_shape=jax.ShapeDtypeStruct((T, D), F32),
        compiler_params=_cparams(("arbitrary",)),
        name="outproj_ln1",
    )(x2, attn2, conv2, wo, g, b)


def _route_picks(logits):
    lane = lax.broadcasted_iota(I32, logits.shape, 1)
    work = logits
    vals, hots = [], []
    for _ in range(TOP_K):
        m = jnp.max(work, axis=1, keepdims=True)
        idx = jnp.min(jnp.where(work == m, lane, LANES), axis=1, keepdims=True)
        hot = lane == idx
        vals.append(m)
        hots.append(hot)
        work = jnp.where(hot, -jnp.inf, work)
    es = [jnp.exp(v - vals[0]) for v in vals]
    den = es[0]
    for e in es[1:]:
        den = den + e
    return hots, [e / den for e in es]


def _route(logits):
    hots, gates = _route_picks(logits)
    comb = jnp.zeros(logits.shape, F32)
    for gate, hot in zip(gates, hots):
        comb = jnp.where(hot, gate, comb)
    return comb


def _expert_mlp(xb, wgu_ref, bgu_ref, wd_ref, bd_ref):
    ff = wd_ref.shape[1]
    gu = _dot(xb, wgu_ref[0]) + bgu_ref[0]
    gate = jnp.minimum(gu[:, 0:ff], SWIGLU_LIMIT)
    up = jnp.clip(gu[:, ff:2 * ff], -SWIGLU_LIMIT, SWIGLU_LIMIT)
    act = (up + 1.0) * (gate * (1.0 / (1.0 + jnp.exp(-SWIGLU_ALPHA * gate))))
    return _dot(act.astype(BF16), wd_ref[0]) + bd_ref[0]


def _moe_kernel(h_ref, wr_ref, br_ref, wgu_ref, bgu_ref, wd_ref, bd_ref, g_ref, b_ref,
                y_ref, hb_ref, comb_ref, *, dn_alpha):
    e = pl.program_id(1)
    tt = h_ref.shape[0]

    @pl.when(e == 0)
    def _():
        hb = h_ref[...].astype(BF16)
        hb_ref[...] = hb
        comb_ref[...] = _route(_dot(hb, wr_ref[...]) + br_ref[...])
        y_ref[...] = jnp.zeros(y_ref.shape, F32)

    y = _expert_mlp(hb_ref[...], wgu_ref, bgu_ref, wd_ref, bd_ref)
    lane = lax.broadcasted_iota(I32, (tt, LANES), 1)
    y_ref[...] += jnp.sum(jnp.where(lane == e, comb_ref[...], 0.0), axis=1, keepdims=True) * y

    @pl.when(e == pl.num_programs(1) - 1)
    def _():
        y_ref[...] = _layer_norm(dn_alpha * h_ref[...] + y_ref[...], g_ref[...], b_ref[...])


def _moe(h2, wr, br, wgu, bgu, wd, bd, g, b, *, dn_alpha):
    T, D = h2.shape
    E, _, F2 = wgu.shape
    tt = min(512, T)
    assert T % tt == 0
    row = lambda i, e: (i, 0)
    const = lambda i, e: (0, 0)
    exp3 = lambda i, e: (e, 0, 0)
    return pl.pallas_call(
        functools.partial(_moe_kernel, dn_alpha=dn_alpha),
        grid=(T // tt, E),
        in_specs=[pl.BlockSpec((tt, D), row), pl.BlockSpec(wr.shape, const), pl.BlockSpec(br.shape, const),
                  pl.BlockSpec((1, D, F2), exp3), pl.BlockSpec((1, 1, F2), exp3),
                  pl.BlockSpec((1, F2 // 2, D), exp3), pl.BlockSpec((1, 1, D), exp3),
                  pl.BlockSpec((1, D), const), pl.BlockSpec((1, D), const)],
        out_specs=pl.BlockSpec((tt, D), row),
        out_shape=jax.ShapeDtypeStruct((T, D), F32),
        scratch_shapes=[pltpu.VMEM((tt, D), BF16), pltpu.VMEM((tt, LANES), F32)],
        compiler_params=_cparams(("arbitrary", "arbitrary")),
        name="moe_ln2",
    )(h2, wr, br, wgu, bgu, wd, bd, g, b)


def _pack_pairs(x):
    w = x.shape[1] // 2
    bits = pltpu.bitcast(x.astype(BF16).astype(F32), I32)
    return (bits[:, 0:w] & jnp.int32(-65536)) | lax.shift_right_logical(bits[:, w:2 * w], 16)


def _unpack_pairs(p):
    hi = pltpu.bitcast(p & jnp.int32(-65536), F32)
    lo = pltpu.bitcast(lax.shift_left(p, 16), F32)
    return jnp.concatenate([hi, lo], axis=1)


def _mix_route_kernel(x_ref, attn_ref, conv_ref, wo_ref, g_ref, b_ref, wrT_ref, brT_ref,
                      h_ref, hpk_ref, infoT_ref, gcol_ref, cnt_ref, *, dn_alpha):
    cat = jnp.concatenate([attn_ref[...], conv_ref[...]], axis=1)
    h = _layer_norm(dn_alpha * x_ref[...] + _dot(cat, wo_ref[...]), g_ref[...], b_ref[...])
    h_ref[...] = h
    tt = h.shape[0]
    ne = cnt_ref.shape[1]
    hpk_ref[...] = _pack_pairs(h)
    logits = lax.dot_general(wrT_ref[...], h.astype(BF16), (((1,), (1,)), ((), ())),
                             preferred_element_type=F32)
    work = logits[0:ne] + jnp.tile(brT_ref[0:ne, :], (1, tt // LANES))
    sub = lax.broadcasted_iota(I32, (ne, tt), 0)
    vals, idxs, hots = [], [], []
    for _ in range(TOP_K):
        m = jnp.max(work, axis=0, keepdims=True)
        idx = jnp.min(jnp.where(work == m, sub, ne), axis=0, keepdims=True)
        hot = sub == idx
        vals.append(m)
        idxs.append(idx)
        hots.append(hot)
        work = jnp.where(hot, -jnp.inf, work)
    es = [jnp.exp(v - vals[0]) for v in vals]
    den = es[0]
    for e in es[1:]:
        den = den + e
    gates = [e / den for e in es]
    picked = jnp.zeros((ne, tt), F32)
    for hot in hots:
        picked = jnp.where(hot, 1.0, picked)
    earlier = lax.broadcasted_iota(I32, (tt, tt), 0) < lax.broadcasted_iota(I32, (tt, tt), 1)
    rank = _dot(picked.astype(BF16), jnp.where(earlier, 1.0, 0.0).astype(BF16))
    ranks = [jnp.sum(jnp.where(hot, rank, 0.0), axis=0, keepdims=True) for hot in hots]
    pad = jnp.zeros((2 * SUBLANES - 3 * TOP_K, tt), F32)
    infoT_ref[...] = jnp.concatenate([i.astype(F32) for i in idxs] + gates + ranks + [pad], axis=0)
    gcol_ref[...] = jnp.concatenate(gates + [jnp.zeros((LANES - TOP_K, tt), F32)], axis=0).T
    cnt_ref[0] = jnp.broadcast_to(jnp.sum(picked, axis=1, keepdims=True), (ne, LANES))


def _mix_route(x2, attn2, conv2, wo, g, b, wr, br, ne, *, dn_alpha):
    T, D = x2.shape
    tt = ROUTE_TILE
    aw, cw = attn2.shape[1], conv2.shape[1]
    row = lambda i: (i, 0)
    const = lambda i: (0, 0)
    wrT = wr.T
    brT = jnp.broadcast_to(br.reshape(LANES, 1), (LANES, LANES))
    return pl.pallas_call(
        functools.partial(_mix_route_kernel, dn_alpha=dn_alpha),
        grid=(T // tt,),
        in_specs=[pl.BlockSpec((tt, D), row), pl.BlockSpec((tt, aw), row), pl.BlockSpec((tt, cw), row),
                  pl.BlockSpec(wo.shape, const), pl.BlockSpec((1, D), const), pl.BlockSpec((1, D), const),
                  pl.BlockSpec(wrT.shape, const), pl.BlockSpec(brT.shape, const)],
        out_specs=(pl.BlockSpec((tt, D), row), pl.BlockSpec((tt, D // 2), row),
                   pl.BlockSpec((2 * SUBLANES, tt), lambda i: (0, i)),
                   pl.BlockSpec((tt, LANES), row), pl.BlockSpec((1, ne, LANES), lambda i: (i, 0, 0))),
        out_shape=(jax.ShapeDtypeStruct((T, D), F32), jax.ShapeDtypeStruct((T, D // 2), I32),
                   jax.ShapeDtypeStruct((2 * SUBLANES, T), F32),
                   jax.ShapeDtypeStruct((T, LANES), F32), jax.ShapeDtypeStruct((T // tt, ne, LANES), F32)),
        compiler_params=_cparams(("arbitrary",)),
        name="outproj_ln1_route",
    )(x2, attn2, conv2, wo, g, b, wrT, brT)


def _sc_gather_rows(table, idx):
    M, W = idx.shape[0], table.shape[1]
    n_workers = SC_CORES * SC_SUBCORES
    per_w = M // n_workers
    n_ch = per_w // SC_ROWS
    assert M == n_workers * n_ch * SC_ROWS
    mesh = plsc.VectorSubcoreMesh(core_axis_name="c", subcore_axis_name="s")

    @functools.partial(
        pl.kernel, mesh=mesh,
        out_type=jax.ShapeDtypeStruct((M, W), table.dtype),
        scratch_types=[pltpu.VMEM((n_ch, SC_ROWS), I32), pltpu.VMEM((SC_ROWS, W), table.dtype),
                       pltpu.SemaphoreType.DMA],
    )
    def gather(table_hbm, idx_hbm, out_hbm, idx_v, rows_v, sem):
        wid = lax.axis_index("s") * SC_CORES + lax.axis_index("c")
        pltpu.sync_copy(idx_hbm.at[wid], idx_v)

        @pl.loop(0, n_ch)
        def _(j):
            pltpu.async_copy(table_hbm.at[idx_v.at[j]], rows_v, sem).wait()
            pltpu.sync_copy(rows_v, out_hbm.at[pl.ds(wid * per_w + j * SC_ROWS, SC_ROWS)])

    return gather(table, idx.reshape(n_workers, n_ch, SC_ROWS))


def _expert_rows_kernel(te_ref, used_ref, x_ref, wgu_ref, bgu_ref, wd_ref, bd_ref, y_ref):
    @pl.when(pl.program_id(0) < used_ref[0])
    def _():
        xg = _unpack_pairs(x_ref[...]).astype(BF16)
        y_ref[...] = _pack_pairs(_expert_mlp(xg, wgu_ref, bgu_ref, wd_ref, bd_ref))


def _expert_rows(tile_expert, used_tiles, xs, wgu, bgu, wd, bd):
    R, W = xs.shape
    E, D, F2 = wgu.shape
    exp3 = lambda j, te, used: (te[j], 0, 0)
    rows = lambda j, te, used: (j, 0)
    grid_spec = pltpu.PrefetchScalarGridSpec(
        num_scalar_prefetch=2,
        grid=(R // ROW_TILE,),
        in_specs=[pl.BlockSpec((ROW_TILE, W), rows),
                  pl.BlockSpec((1, D, F2), exp3), pl.BlockSpec((1, 1, F2), exp3),
                  pl.BlockSpec((1, F2 // 2, D), exp3), pl.BlockSpec((1, 1, D), exp3)],
        out_specs=pl.BlockSpec((ROW_TILE, W), rows),
    )
    return pl.pallas_call(
        _expert_rows_kernel,
        grid_spec=grid_spec,
        out_shape=jax.ShapeDtypeStruct((R, W), I32),
        compiler_params=_cparams(("arbitrary",)),
        name="moe_expert_rows",
    )(tile_expert, used_tiles, xs, wgu, bgu, wd, bd)


def _combine_kernel(h_ref, yg_ref, gcol_ref, g_ref, b_ref, o_ref, *, dn_alpha):
    acc = dn_alpha * h_ref[...]
    for k in range(TOP_K):
        acc = acc + gcol_ref[:, k:k + 1] * _unpack_pairs(yg_ref[k])
    o_ref[...] = _layer_norm(acc, g_ref[...], b_ref[...])


def _combine(h2, yg, gcol, g, b, *, dn_alpha):
    T, D = h2.shape
    tt = 512
    row = lambda i: (i, 0)
    const = lambda i: (0, 0)
    return pl.pallas_call(
        functools.partial(_combine_kernel, dn_alpha=dn_alpha),
        grid=(T // tt,),
        in_specs=[pl.BlockSpec((tt, D), row), pl.BlockSpec((TOP_K, tt, D // 2), lambda i: (0, i, 0)),
                  pl.BlockSpec((tt, LANES), row),
                  pl.BlockSpec((1, D), const), pl.BlockSpec((1, D), const)],
        out_specs=pl.BlockSpec((tt, D), row),
        out_shape=jax.ShapeDtypeStruct((T, D), F32),
        compiler_params=_cparams(("arbitrary",)),
        name="moe_combine_ln2",
    )(h2, yg, gcol, g, b)


def _dest_kernel(infoT_ref, base_ref, dest_ref):
    tt = infoT_ref.shape[1]
    ne = base_ref.shape[1]
    base = jnp.tile(base_ref[0], (1, tt // LANES))
    sub = lax.broadcasted_iota(I32, (ne, tt), 0)
    rows = []
    for k in range(TOP_K):
        hot = sub == infoT_ref[k:k + 1, :].astype(I32)
        rows.append(jnp.sum(jnp.where(hot, base, 0.0), axis=0, keepdims=True)
                    + infoT_ref[2 * TOP_K + k:2 * TOP_K + k + 1, :])
    rows.append(jnp.zeros((SUBLANES - TOP_K, tt), F32))
    dest_ref[...] = jnp.concatenate(rows, axis=0).astype(I32)


def _dest_rows(infoT, base):
    T = infoT.shape[1]
    tt = ROUTE_TILE
    ne = base.shape[1]
    return pl.pallas_call(
        _dest_kernel,
        grid=(T // tt,),
        in_specs=[pl.BlockSpec((2 * SUBLANES, tt), lambda i: (0, i)), pl.BlockSpec((1, ne, LANES), lambda i: (i, 0, 0))],
        out_specs=pl.BlockSpec((SUBLANES, tt), lambda i: (0, i)),
        out_shape=jax.ShapeDtypeStruct((SUBLANES, T), I32),
        compiler_params=_cparams(("arbitrary",)),
        name="moe_dest_rows",
    )(infoT, base)


def _sc_scatter_rows(src, dest_km, n_rows):
    T, W = src.shape
    K = dest_km.shape[0]
    n_workers = SC_CORES * SC_SUBCORES
    per_w = T // n_workers
    n_ch = per_w // SC_ROWS
    assert T == n_workers * n_ch * SC_ROWS
    mesh = plsc.VectorSubcoreMesh(core_axis_name="c", subcore_axis_name="s")
    idx = dest_km.reshape(K, n_workers, n_ch, SC_ROWS).transpose(1, 0, 2, 3).reshape(n_workers, K * n_ch, SC_ROWS)

    @functools.partial(
        pl.kernel, mesh=mesh,
        out_type=jax.ShapeDtypeStruct((n_rows, W), src.dtype),
        scratch_types=[pltpu.VMEM((K * n_ch, SC_ROWS), I32), pltpu.VMEM((SC_ROWS, W), src.dtype),
                       pltpu.SemaphoreType.DMA],
    )
    def scatter(src_hbm, idx_hbm, out_hbm, idx_v, rows_v, sem):
        wid = lax.axis_index("s") * SC_CORES + lax.axis_index("c")
        pltpu.sync_copy(idx_hbm.at[wid], idx_v)

        @pl.loop(0, n_ch)
        def _(j):
            pltpu.sync_copy(src_hbm.at[pl.ds(wid * per_w + j * SC_ROWS, SC_ROWS)], rows_v)
            for k in range(K):
                pltpu.async_copy(rows_v, out_hbm.at[idx_v.at[k * n_ch + j]], sem).wait()

    return scatter(src, idx)


def _moe_sorted(h2, routed, wgu, bgu, wd, bd, g, b, *, dn_alpha):
    T, D = h2.shape
    E = wgu.shape[0]
    hpk, infoT, gcol, cnt = routed
    cnt = cnt[:, :, 0].astype(I32)
    total = jnp.sum(cnt, axis=0)
    padded = -(-total // ROW_TILE) * ROW_TILE
    ends = jnp.cumsum(padded)
    base = (ends - padded)[None, :] + jnp.cumsum(cnt, axis=0) - cnt
    base = jnp.broadcast_to(base.astype(F32)[:, :, None], cnt.shape + (LANES,))
    dest_km = _dest_rows(infoT, base)[0:TOP_K]
    R = T * TOP_K + E * ROW_TILE
    tile_start = jnp.arange(R // ROW_TILE, dtype=I32) * ROW_TILE
    tile_expert = jnp.minimum(jnp.sum((ends[None, :] <= tile_start[:, None]).astype(I32), axis=1), E - 1)
    used_tiles = (ends[E - 1] // ROW_TILE).reshape(1)
    xs = _sc_scatter_rows(hpk, dest_km, R)
    ys = _expert_rows(tile_expert, used_tiles, xs, wgu, bgu, wd, bd)
    yg = _sc_gather_rows(ys, dest_km.reshape(-1))
    return _combine(h2, yg.reshape(TOP_K, T, D // 2), gcol, g, b, dn_alpha=dn_alpha)


def _split_gu_kernel(w_ref, perm_ref, o_ref, *, band):
    n = w_ref.shape[2]
    for m in range(n // band):
        both = _dot(w_ref[0, :, m * band:(m + 1) * band].astype(BF16), perm_ref[...]).astype(BF16)
        o_ref[0, :, m * (band // 2):(m + 1) * (band // 2)] = both[:, 0:band // 2]
        o_ref[0, :, n // 2 + m * (band // 2):n // 2 + (m + 1) * (band // 2)] = both[:, band // 2:band]


def _split_gu(w_gu):
    E, D, N = w_gu.shape
    tr = min(1024, D)
    band = min(512, N)
    j = jnp.arange(band, dtype=I32)
    src = jnp.where(j < band // 2, 2 * j, 2 * (j - band // 2) + 1)
    perm = (jnp.arange(band, dtype=I32)[:, None] == src[None, :]).astype(BF16)
    return pl.pallas_call(
        functools.partial(_split_gu_kernel, band=band),
        grid=(E, D // tr),
        in_specs=[pl.BlockSpec((1, tr, N), lambda e, r: (e, r, 0)),
                  pl.BlockSpec((band, band), lambda e, r: (0, 0))],
        out_specs=pl.BlockSpec((1, tr, N), lambda e, r: (e, r, 0)),
        out_shape=jax.ShapeDtypeStruct((E, D, N), BF16),
        compiler_params=_cparams(("arbitrary", "arbitrary")),
        name="split_gate_up",
    )(w_gu, perm)


def _prep_weights(w_in, w_conv, w_out, ln1_g, ln1_b, w_router, b_router, w_gu, b_gu, w_down, b_down,
                  ln2_g, ln2_b):
    aw = N_HEADS * HEAD_DIM
    kv = N_KV_HEADS * HEAD_DIM
    iw = N_IDX_HEADS * IDX_DIM
    D = w_in.shape[0]
    cw = D - aw
    o_q, o_k, o_v = 0, aw, aw + kv
    o_qi = o_v + kv
    o_ki = o_qi + iw
    o_wi = o_ki + IDX_DIM
    o_c = o_wi + N_IDX_HEADS
    wa = jnp.concatenate([w_in[:, o_q:o_k], w_in[:, o_qi:o_ki], w_in[:, o_k:o_v], w_in[:, o_v:o_qi]],
                         axis=1).astype(BF16)
    ws = jnp.pad(w_in[:, o_ki:o_c], ((0, 0), (0, LANES - IDX_DIM - N_IDX_HEADS))).astype(BF16)
    wc = w_in[:, o_c:o_c + 3 * cw].astype(BF16)
    E = w_router.shape[1]
    wr = jnp.pad(w_router, ((0, 0), (0, LANES - E))).astype(BF16)
    br = jnp.pad(b_router, (0, LANES - E), constant_values=MASK_VALUE).reshape(1, LANES)
    F = w_gu.shape[2] // 2
    return dict(
        wa=wa, ws=ws, wc=wc, wconv=w_conv, wo=w_out.astype(BF16),
        ln1_g=ln1_g.reshape(1, D), ln1_b=ln1_b.reshape(1, D),
        wr=wr, br=br,
        wgu=_split_gu(w_gu),
        bgu=jnp.concatenate([b_gu[:, 0::2], b_gu[:, 1::2]], axis=1).reshape(E, 1, 2 * F),
        wd=w_down.astype(BF16), bd=b_down.reshape(E, 1, D),
        ln2_g=ln2_g.reshape(1, D), ln2_b=ln2_b.reshape(1, D),
    )


def _decode_layouts(pq, small, k_all, v_all, kidx_all, kc, qb):
    B, S, _ = pq.shape
    aw = N_HEADS * HEAD_DIM
    L, kv = k_all.shape[1], k_all.shape[2]
    lp = -(-L // kc) * kc
    nc = lp // kc
    padk = lambda a: jnp.pad(a, ((0, 0), (0, lp - L), (0, 0)))
    padq = lambda a: jnp.pad(a, ((0, 0), (0, 0), (0, 0), (0, qb - S)))
    q = pq[:, :, 0:aw] * QK_SCALE
    qT = padq(q.reshape(B, S, N_HEADS, HEAD_DIM).transpose(0, 2, 3, 1)).astype(BF16)
    qiT = padq(pq[:, :, aw:].reshape(B, S, N_IDX_HEADS, IDX_DIM).transpose(0, 2, 3, 1)).astype(BF16)
    wT = jnp.pad(small[:, :, IDX_DIM:IDX_DIM + N_IDX_HEADS].transpose(0, 2, 1), ((0, 0), (0, 0), (0, qb - S)))
    kb = padk(k_all).reshape(B, nc, kc, N_KV_HEADS, HEAD_DIM).transpose(0, 1, 3, 2, 4).astype(BF16)
    vT = padk(v_all).reshape(B, nc, kc, kv).transpose(0, 1, 3, 2).astype(BF16)
    kidxb = padk(kidx_all).reshape(B, nc, kc, IDX_DIM).astype(BF16)
    knorm = jnp.sum(kb.astype(F32) ** 2, axis=-1)
    return qT, qiT, wT, kidxb, kb, vT, knorm


def _layer(x, past_k, past_v, past_kidx, conv_buf, rel_bias, w, *, dn_alpha):
    B, S, D = x.shape
    prefill = past_k is None
    outs = _project(x, conv_buf, w["wa"], w["ws"], w["wc"], w["wconv"], attn_layouts=prefill)
    k, v, small, conv, new_buf = outs[:5]
    k_idx = small if prefill else small[:, :, 0:IDX_DIM]
    if prefill:
        past, l_true = 0, S
        qb = min(QUERY_BLOCK, S)
        kb, vT, kidxb, qT, qiT, wT, knorm = outs[5:]
    else:
        past = past_k.shape[1]
        l_true = past + S
        qb = -(-S // LANES) * LANES
        qT, qiT, wT, kidxb, kb, vT, knorm = _decode_layouts(
            outs[5], small, jnp.concatenate([past_k, k], axis=1), jnp.concatenate([past_v, v], axis=1),
            jnp.concatenate([past_kidx, k_idx], axis=1), math.gcd(past, KEY_CHUNK), qb)
    offsets = (-LANES, 0)
    tabs = _bias_tables(rel_bias, offsets)
    attn = _attend(qT, qiT, wT, kidxb, kb, vT, knorm, tabs, qb=qb, past=past, l_true=l_true, offsets=offsets)
    attn = attn[:, :S]
    mix_args = (x.reshape(B * S, D), attn.reshape(B * S, -1), conv.reshape(B * S, -1), w["wo"],
                w["ln1_g"], w["ln1_b"])
    experts = (w["wgu"], w["bgu"], w["wd"], w["bd"], w["ln2_g"], w["ln2_b"])
    sc_unit = SC_CORES * SC_SUBCORES * SC_ROWS
    sortable = ((B * S) % ROUTE_TILE == 0 and (B * S * TOP_K) % sc_unit == 0
                and (N_EXPERTS * ROW_TILE) % sc_unit == 0)
    if sortable and B * S >= SORTED_MIN_TOKENS:
        h, *routed = _mix_route(*mix_args, w["wr"], w["br"], N_EXPERTS, dn_alpha=dn_alpha)
        y = _moe_sorted(h, routed, *experts, dn_alpha=dn_alpha)
    else:
        h = _mix(*mix_args, dn_alpha=dn_alpha)
        y = _moe(h, w["wr"], w["br"], *experts, dn_alpha=dn_alpha)
    return (y.reshape(B, S, D), k.reshape(B, S, N_KV_HEADS, HEAD_DIM), v.reshape(B, S, N_KV_HEADS, HEAD_DIM),
            k_idx, new_buf)


def kernel(x_prompt, x_sample, cache_k, cache_v, cache_kidx, state_conv, rel_bias, w_in, w_conv, w_out,
           ln1_g, ln1_b, w_router, b_router, w_gu, b_gu, w_down, b_down, ln2_g, ln2_b):
    depth = w_in.shape[0]
    assert depth == 1
    dn_alpha = (2 * depth) ** 0.25
    kv = N_KV_HEADS * HEAD_DIM
    w = _prep_weights(w_in[0], w_conv[0], w_out[0], ln1_g[0], ln1_b[0], w_router[0], b_router[0],
                      w_gu[0], b_gu[0], w_down[0], b_down[0], ln2_g[0], ln2_b[0])
    Bp = x_prompt.shape[0]
    cw = w_conv.shape[2]
    zero_buf = jnp.zeros((Bp, CONV_K - 1, cw), F32)
    yp, k1, v1, i1, c1 = _layer(x_prompt, None, None, None, zero_buf, rel_bias, w, dn_alpha=dn_alpha)
    Bs, P = cache_k.shape[1], cache_k.shape[2]
    ys, k2, v2, i2, c2 = _layer(x_sample, cache_k[0].reshape(Bs, P, kv), cache_v[0].reshape(Bs, P, kv),
                                cache_kidx[0], state_conv[0], rel_bias, w, dn_alpha=dn_alpha)
    return (yp, ys, k1[None], v1[None], i1[None], c1[None], k2[None], v2[None], i2[None], c2[None])
```

```python
import functools
import math

import jax
import jax.numpy as jnp
from jax import lax
from jax.experimental import pallas as pl
from jax.experimental.pallas import tpu as pltpu
from jax.experimental.pallas import tpu_sc as plsc

F32 = jnp.float32
BF16 = jnp.bfloat16
I32 = jnp.int32
I16 = jnp.int16

CHUNK = 64
N_HEADS = 8
HEAD_DIM = 64
N_KV_HEADS = 2
GROUP = N_HEADS // N_KV_HEADS
N_IDX_HEADS = 8
IDX_DIM = 32
TOPK_MAX = 256
CONV_K = 3
N_BUCKETS = 32
MAX_DISTANCE = 128
N_EXPERTS = 32
TOP_K = 4
SWIGLU_LIMIT = 7.0
SWIGLU_ALPHA = 1.702
LN_EPS = 1e-5
MASK_VALUE = -1e30
QK_SCALE = HEAD_DIM ** -0.5 * math.log2(math.e)
EXP2_SAFE = 96.0

LANES = 128
SUBLANES = 8
KEY_CHUNK = 512
QUERY_BLOCK = 512
SLAB = 64
ROW_TILE = 512
ROUTE_TILE = 1024
SC_CORES = 2
SC_SUBCORES = 16
SC_ROWS = 64
SORTED_MIN_TOKENS = 2048
INT_MIN = -(2 ** 31)
VMEM_LIMIT = 56 * 1024 * 1024


def _cparams(sem):
    return pltpu.CompilerParams(dimension_semantics=sem, vmem_limit_bytes=VMEM_LIMIT)


def _dot(a, b):
    return jnp.dot(a, b, preferred_element_type=F32)


def _layer_norm(z, g, b):
    mu = jnp.mean(z, axis=-1, keepdims=True)
    zc = z - mu
    var = jnp.mean(zc * zc, axis=-1, keepdims=True)
    return zc * lax.rsqrt(var + LN_EPS) * g + b


def _bucket_thresholds():
    nb = N_BUCKETS // 2
    max_exact = nb // 2
    out = []
    for j in range(1, nb - max_exact):
        out.append(math.ceil(max_exact * (MAX_DISTANCE / max_exact) ** (j / (nb - max_exact)) - 1e-9))
    return tuple(out)


def _bias_table_kernel(rel_ref, tab_ref, *, offsets):
    nb = N_BUCKETS // 2
    max_exact = nb // 2
    thr = _bucket_thresholds()
    ii = lax.broadcasted_iota(I32, (LANES, LANES), 0)
    jj = lax.broadcasted_iota(I32, (LANES, LANES), 1)
    for d, off in enumerate(offsets):
        rel = off + ii - jj
        n = jnp.abs(rel)
        large = jnp.full((LANES, LANES), max_exact, I32)
        for t in thr:
            large = large + jnp.where(n >= t, 1, 0)
        bucket = jnp.where(rel > 0, nb, 0) + jnp.where(n < max_exact, n, large)
        for h in range(N_HEADS):
            acc = jnp.zeros((LANES, LANES), F32)
            for b in range(N_BUCKETS):
                acc = jnp.where(bucket == b, rel_ref[b, h], acc)
            tab_ref[d, h] = (acc - rel_ref[nb - 1, h]) * math.log2(math.e)


def _bias_tables(rel_bias, offsets):
    return pl.pallas_call(
        functools.partial(_bias_table_kernel, offsets=offsets),
        out_shape=jax.ShapeDtypeStruct((len(offsets), N_HEADS, LANES, LANES), F32),
        in_specs=[pl.BlockSpec(memory_space=pltpu.SMEM)],
        out_specs=pl.BlockSpec(memory_space=pltpu.VMEM),
        name="bias_tables",
    )(rel_bias)


def _proj_kernel(x_ref, wa_ref, ws_ref, wc_ref, wconv_ref, buf_ref,
                 k_ref, v_ref, small_ref, conv_ref, nbuf_ref, *rest, ts, attn_layouts):
    s = pl.program_id(1)
    xb = x_ref[0].astype(BF16)
    pa = _dot(xb, wa_ref[...])
    ps = _dot(xb, ws_ref[...])
    pc = _dot(xb, wc_ref[...])
    aw = N_HEADS * HEAD_DIM
    iw = N_IDX_HEADS * IDX_DIM
    kv = N_KV_HEADS * HEAD_DIM
    k = pa[:, aw + iw:aw + iw + kv]
    v = pa[:, aw + iw + kv:aw + iw + 2 * kv]
    k_ref[0] = k
    v_ref[0] = v
    small_ref[0] = ps[:, 0:small_ref.shape[2]]
    scale = QK_SCALE
    if attn_layouts:
        kb_ref, vT_ref, kidxb_ref, qT_ref, qiT_ref, wT_ref, knorm_ref, carry_ref = rest
        for n in range(N_KV_HEADS):
            kb_ref[0, 0, n] = k[:, n * HEAD_DIM:(n + 1) * HEAD_DIM].astype(BF16)
        kTf = k.astype(BF16).astype(F32).T
        knorm_ref[0, 0] = jnp.concatenate(
            [jnp.sum(kTf[n * HEAD_DIM:(n + 1) * HEAD_DIM] ** 2, axis=0, keepdims=True) for n in range(N_KV_HEADS)],
            axis=0)
        vT_ref[0, 0] = v.T.astype(BF16)
        kidxb_ref[0, 0] = ps[:, 0:IDX_DIM].astype(BF16)
        qT = (pa[:, 0:aw] * scale).T.astype(BF16)
        for h in range(N_HEADS):
            qT_ref[0, h] = qT[h * HEAD_DIM:(h + 1) * HEAD_DIM]
        qiT = pa[:, aw:aw + iw].T.astype(BF16)
        for h in range(N_IDX_HEADS):
            qiT_ref[0, h] = qiT[h * IDX_DIM:(h + 1) * IDX_DIM]
        wT_ref[0] = ps.T[IDX_DIM:IDX_DIM + N_IDX_HEADS]
    else:
        pq_ref, carry_ref = rest
        pq_ref[0] = pa[:, 0:aw + iw]

    cw = pc.shape[1] // 3
    u = pc[:, cw:2 * cw] * pc[:, 2 * cw:3 * cw]

    @pl.when(s == 0)
    def _():
        carry_ref[6:8, :] = buf_ref[0]

    carry_ref[8:8 + ts, :] = u
    y = (carry_ref[6:6 + ts, :] * wconv_ref[0:1, :]
         + carry_ref[7:7 + ts, :] * wconv_ref[1:2, :]
         + u * wconv_ref[2:3, :])
    conv_ref[0] = (pc[:, 0:cw] * y).astype(BF16)
    nb = carry_ref[ts + 6:ts + 8, :]
    nbuf_ref[0] = nb
    carry_ref[6:8, :] = nb


def _project(x, conv_buf, wa, ws, wc, wconv, *, attn_layouts):
    B, S, D = x.shape
    ts = min(KEY_CHUNK, S)
    assert S % ts == 0 and S >= CONV_K - 1
    ns = S // ts
    cw = wc.shape[1] // 3
    sw = IDX_DIM if attn_layouts else LANES
    kv = N_KV_HEADS * HEAD_DIM
    aw = N_HEADS * HEAD_DIM
    iw = N_IDX_HEADS * IDX_DIM
    out_shape = [
        jax.ShapeDtypeStruct((B, S, kv), F32),
        jax.ShapeDtypeStruct((B, S, kv), F32),
        jax.ShapeDtypeStruct((B, S, sw), F32),
        jax.ShapeDtypeStruct((B, S, cw), BF16),
        jax.ShapeDtypeStruct((B, CONV_K - 1, cw), F32),
    ]
    out_specs = [
        pl.BlockSpec((1, ts, kv), lambda b, s: (b, s, 0)),
        pl.BlockSpec((1, ts, kv), lambda b, s: (b, s, 0)),
        pl.BlockSpec((1, ts, sw), lambda b, s: (b, s, 0)),
        pl.BlockSpec((1, ts, cw), lambda b, s: (b, s, 0)),
        pl.BlockSpec((1, CONV_K - 1, cw), lambda b, s: (b, 0, 0)),
    ]
    if attn_layouts:
        out_shape += [
            jax.ShapeDtypeStruct((B, ns, N_KV_HEADS, ts, HEAD_DIM), BF16),
            jax.ShapeDtypeStruct((B, ns, kv, ts), BF16),
            jax.ShapeDtypeStruct((B, ns, ts, IDX_DIM), BF16),
            jax.ShapeDtypeStruct((B, N_HEADS, HEAD_DIM, S), BF16),
            jax.ShapeDtypeStruct((B, N_IDX_HEADS, IDX_DIM, S), BF16),
            jax.ShapeDtypeStruct((B, N_IDX_HEADS, S), F32),
            jax.ShapeDtypeStruct((B, ns, N_KV_HEADS, ts), F32),
        ]
        out_specs += [
            pl.BlockSpec((1, 1, N_KV_HEADS, ts, HEAD_DIM), lambda b, s: (b, s, 0, 0, 0)),
            pl.BlockSpec((1, 1, kv, ts), lambda b, s: (b, s, 0, 0)),
            pl.BlockSpec((1, 1, ts, IDX_DIM), lambda b, s: (b, s, 0, 0)),
            pl.BlockSpec((1, N_HEADS, HEAD_DIM, ts), lambda b, s: (b, 0, 0, s)),
            pl.BlockSpec((1, N_IDX_HEADS, IDX_DIM, ts), lambda b, s: (b, 0, 0, s)),
            pl.BlockSpec((1, N_IDX_HEADS, ts), lambda b, s: (b, 0, s)),
            pl.BlockSpec((1, 1, N_KV_HEADS, ts), lambda b, s: (b, s, 0, 0)),
        ]
    else:
        out_shape += [jax.ShapeDtypeStruct((B, S, aw + iw), F32)]
        out_specs += [pl.BlockSpec((1, ts, aw + iw), lambda b, s: (b, s, 0))]
    const2 = lambda b, s: (0, 0)
    in_specs = [
        pl.BlockSpec((1, ts, D), lambda b, s: (b, s, 0)),
        pl.BlockSpec(wa.shape, const2),
        pl.BlockSpec(ws.shape, const2),
        pl.BlockSpec(wc.shape, const2),
        pl.BlockSpec(wconv.shape, const2),
        pl.BlockSpec((1, CONV_K - 1, cw), lambda b, s: (b, 0, 0)),
    ]
    return pl.pallas_call(
        functools.partial(_proj_kernel, ts=ts, attn_layouts=attn_layouts),
        grid=(B, ns),
        in_specs=in_specs,
        out_specs=tuple(out_specs),
        out_shape=tuple(out_shape),
        scratch_shapes=[pltpu.VMEM((ts + 8, cw), F32)],
        compiler_params=_cparams(("arbitrary", "arbitrary")),
        name="proj_conv",
    )(x, wa, ws, wc, wconv, conv_buf)


def _attn_kernel(qT_ref, qiT_ref, wT_ref, kidx_ref, kb_ref, vT_ref, knorm_ref, tab_ref,
                 o_ref, skey_ref, hi16_ref, lo16_ref, jb_ref, madd_ref, *head_refs,
                 qb, kc, past, l_true, topk, offsets):
    s_refs, p_refs, m_refs, l_refs, acc_refs = (head_refs[g * N_HEADS:(g + 1) * N_HEADS] for g in range(5))
    i = pl.program_id(1)
    qoff = past + i * qb
    adm_end = jnp.minimum(((qoff + qb - 1) // CHUNK + 1) * CHUNK, l_true)
    nck = (adm_end + kc - 1) // kc
    idx_bits = int(l_true).bit_length()
    groups = kc // SUBLANES

    rowi = lax.broadcasted_iota(I32, (kc, qb), 0)
    qpos = qoff + lax.broadcasted_iota(I32, (1, qb), 1)
    kmax = jnp.minimum((qpos // CHUNK + 1) * CHUNK, l_true)

    def score_body(c, carry):
        kidx_c = kidx_ref[0, c]
        sc = jnp.zeros((kc, qb), F32)
        for h in range(N_IDX_HEADS):
            d = _dot(kidx_c, qiT_ref[0, h])
            sc = sc + wT_ref[0, h:h + 1, :] * jnp.maximum(d, 0.0)
        bits = pltpu.bitcast(sc, I32)
        skey = jnp.where(bits < 0, bits ^ jnp.int32(0x7FFFFFFF), bits)
        skey = jnp.where(skey == -1, 0, skey)
        skey = jnp.where(rowi < kmax - c * kc, skey, jnp.int32(INT_MIN))
        skey_ref[c] = skey
        hi16_ref[c] = (skey >> 16).astype(I16)
        return carry

    lax.fori_loop(0, nck, score_body, 0)

    def count(pred_fn):
        def body(c, part):
            ind = jnp.where(pred_fn(c, skey_ref[c]), 1, 0)
            return part + jnp.sum(ind.reshape(groups, SUBLANES, qb), axis=0)
        part = lax.fori_loop(0, nck, body, jnp.zeros((SUBLANES, qb), I32))
        return jnp.sum(part, axis=0, keepdims=True)

    def count16(ref, pred_fn):
        pack = 2 * SUBLANES

        def body(c, part):
            accs = [None] * 4
            for g in range(kc // pack):
                ind = jnp.where(pred_fn(ref[c, g * pack:(g + 1) * pack, :]), jnp.bfloat16(1), jnp.bfloat16(0))
                accs[g % 4] = ind if accs[g % 4] is None else accs[g % 4] + ind
            return part + ((accs[0] + accs[1]) + (accs[2] + accs[3])).astype(F32)
        part = lax.fori_loop(0, nck, body, jnp.zeros((pack, qb), F32))
        return jnp.sum(part, axis=0, keepdims=True).astype(I32)

    def search16(ref, target, count_all):
        def body(it, carry):
            t_u, c_t = carry
            cand_u = t_u | (jnp.int32(1) << (15 - it))
            cand = (cand_u - 2 ** 15).astype(I16)
            cnt = count16(ref, lambda k: k >= cand)
            keep = cnt >= target
            return jnp.where(keep, cand_u, t_u), jnp.where(keep, cnt, c_t)
        return lax.fori_loop(0, 16, body, (jnp.zeros((1, qb), I32), count_all))

    stored = jnp.broadcast_to(nck * kc, (1, qb)).astype(I32)
    hi_u, n_ge_hi = search16(hi16_ref, topk, stored)
    hi_s = (hi_u - 2 ** 15).astype(I16)
    n_gt_hi = count16(hi16_ref, lambda k: k > hi_s)
    rest = topk - n_gt_hi

    def low_body(c, carry):
        low = ((skey_ref[c] & 0xFFFF) - 2 ** 15).astype(I16)
        lo16_ref[c] = jnp.where(hi16_ref[c] == hi_s, low, jnp.int16(-2 ** 15))
        return carry

    lax.fori_loop(0, nck, low_body, 0)
    lo_u, n_ge_lo = search16(lo16_ref, rest, n_ge_hi - n_gt_hi)
    thr = (hi_u - 2 ** 15) * 2 ** 16 + lo_u
    lo_s = (lo_u - 2 ** 15).astype(I16)
    cnt_ge = n_gt_hi + n_ge_lo
    cnt_gt = n_gt_hi + count16(lo16_ref, lambda k: k > lo_s)

    jb_ref[...] = jnp.full(jb_ref.shape, 2 ** 31 - 1, I32)

    @pl.when(jnp.max(cnt_ge) > topk)
    def _():
        need = topk - cnt_gt

        def tie_body(it, jb):
            cand = jb | (jnp.int32(1) << (idx_bits - 1 - it))
            cnt = count(lambda c, sk: jnp.logical_and(sk == thr, c * kc + rowi < cand))
            return jnp.where(cnt <= need, cand, jb)

        jb = lax.fori_loop(0, idx_bits, tie_body, jnp.zeros((1, qb), I32))
        jb_ref[...] = jnp.broadcast_to(jb, jb_ref.shape)

    jbound = jnp.where(thr == jnp.int32(INT_MIN), 0, jb_ref[0:1, :])
    thr_m1 = thr - 1

    for h in range(N_HEADS):
        l_refs[h][...] = jnp.zeros((1, qb), F32)
        acc_refs[h][...] = jnp.zeros((HEAD_DIM, qb), F32)
    n_slabs = kc // SLAB

    def fold(x, op):
        return op(x.reshape(SLAB // SUBLANES, SUBLANES, qb), axis=0)

    def near_bias(chunk_offset, h):
        tiles = []
        for t in range(kc // LANES):
            row = []
            for u in range(qb // LANES):
                d_tu = chunk_offset + (t - u) * LANES
                row.append(tab_ref[offsets.index(d_tu), h] if d_tu in offsets else jnp.zeros((LANES, LANES), F32))
            tiles.append(jnp.concatenate(row, axis=1))
        return jnp.concatenate(tiles, axis=0)

    def attn_chunk(c, near):
        sk = skey_ref[c]
        t_eff = jnp.where(rowi < jbound - c * kc, thr_m1, thr)
        madd_ref[...] = jnp.where(sk > t_eff, 0.0, MASK_VALUE)
        maccs = []
        for h in range(N_HEADS):
            x = _dot(kb_ref[0, c, h // GROUP], qT_ref[0, h]) + madd_ref[...]
            if near is not None:
                x = x + near_bias(near, h)
            s_refs[h][...] = x
            maccs.append(jnp.max(x.reshape(groups, SUBLANES, qb), axis=0))
        alphas = []
        for h in range(N_HEADS):
            s_ref, p_ref, m_ref = s_refs[h], p_refs[h], m_refs[h]
            m_old = m_ref[...]
            m_new = jnp.maximum(m_old, jnp.max(maccs[h], axis=0, keepdims=True))
            for j in range(n_slabs):
                rows = pl.ds(j * SLAB, SLAB)
                p_ref[rows, :] = jnp.exp2(s_ref[rows, :] - m_new).astype(BF16)
            m_ref[...] = m_new
            alphas.append(jnp.exp2(m_old - m_new))
        ones = jnp.ones((2 * SUBLANES, kc), BF16)
        v_aug = [jnp.concatenate([vT_ref[0, c, n * HEAD_DIM:(n + 1) * HEAD_DIM, :], ones], axis=0)
                 for n in range(N_KV_HEADS)]
        for h in range(N_HEADS):
            pv = _dot(v_aug[h // GROUP], p_refs[h][...])
            acc_refs[h][...] = alphas[h] * acc_refs[h][...] + pv[0:HEAD_DIM]
            l_refs[h][...] = alphas[h] * l_refs[h][...] + pv[HEAD_DIM:HEAD_DIM + 1]

    def attn_chunk_fixed(c, near):
        sk = skey_ref[c]
        t_eff = jnp.where(rowi < jbound - c * kc, thr_m1, thr)
        madd_ref[...] = jnp.where(sk > t_eff, 0.0, MASK_VALUE)
        ones = jnp.ones((2 * SUBLANES, kc), BF16)
        v_aug = [jnp.concatenate([vT_ref[0, c, n * HEAD_DIM:(n + 1) * HEAD_DIM, :], ones], axis=0)
                 for n in range(N_KV_HEADS)]
        for h in range(N_HEADS):
            x = _dot(kb_ref[0, c, h // GROUP], qT_ref[0, h]) + madd_ref[...]
            if near is not None:
                x = x + near_bias(near, h)
            p_refs[h][...] = jnp.exp2(x).astype(BF16)
        for h in range(N_HEADS):
            pv = _dot(v_aug[h // GROUP], p_refs[h][...])
            acc_refs[h][...] += pv[0:HEAD_DIM]
            l_refs[h][...] += pv[HEAD_DIM:HEAD_DIM + 1]

    def run(chunk_fn):
        c_diag = qoff // kc
        lax.fori_loop(0, c_diag - 1, lambda c, carry: (chunk_fn(c, None), carry)[1], 0)

        @pl.when(c_diag >= 1)
        def _():
            chunk_fn(c_diag - 1, -kc)

        chunk_fn(c_diag, 0)

    q_sq = jnp.zeros((1, qb), F32)
    for h in range(N_HEADS):
        qf = qT_ref[0, h].astype(F32)
        q_sq = jnp.maximum(q_sq, jnp.sum(qf * qf, axis=0, keepdims=True))
    bound = jnp.sqrt(jnp.max(q_sq) * jnp.max(knorm_ref[0])) + jnp.max(jnp.abs(tab_ref[...]))
    fixed_shift = bound < EXP2_SAFE

    @pl.when(fixed_shift)
    def _():
        run(attn_chunk_fixed)

    @pl.when(jnp.logical_not(fixed_shift))
    def _():
        for h in range(N_HEADS):
            m_refs[h][...] = jnp.full((1, qb), MASK_VALUE, F32)
        run(attn_chunk)

    oT = jnp.concatenate([acc_refs[h][...] / l_refs[h][...] for h in range(N_HEADS)], axis=0)
    o_ref[0] = oT.T.astype(o_ref.dtype)


def _attend(qT, qiT, wT, kidxb, kb, vT, knorm, tabs, *, qb, past, l_true, offsets):
    B, _, _, S = qT.shape
    nc, kc = vT.shape[1], vT.shape[3]
    assert S % qb == 0 and qb % LANES == 0 and kc % LANES == 0
    assert past % kc == 0 and qb <= kc and (qb == kc or S == qb)
    topk = min(TOPK_MAX, l_true // 4)
    kern = functools.partial(_attn_kernel, qb=qb, kc=kc, past=past, l_true=l_true,
                             topk=topk, offsets=offsets)
    in_specs = [
        pl.BlockSpec((1, N_HEADS, HEAD_DIM, qb), lambda b, i: (b, 0, 0, i)),
        pl.BlockSpec((1, N_IDX_HEADS, IDX_DIM, qb), lambda b, i: (b, 0, 0, i)),
        pl.BlockSpec((1, N_IDX_HEADS, qb), lambda b, i: (b, 0, i)),
        pl.BlockSpec((1,) + kidxb.shape[1:], lambda b, i: (b, 0, 0, 0)),
        pl.BlockSpec((1,) + kb.shape[1:], lambda b, i: (b, 0, 0, 0, 0)),
        pl.BlockSpec((1,) + vT.shape[1:], lambda b, i: (b, 0, 0, 0)),
        pl.BlockSpec((1,) + knorm.shape[1:], lambda b, i: (b, 0, 0, 0)),
        pl.BlockSpec(tabs.shape, lambda b, i: (0, 0, 0, 0)),
    ]
    return pl.pallas_call(
        kern,
        grid=(B, S // qb),
        in_specs=in_specs,
        out_specs=pl.BlockSpec((1, qb, N_HEADS * HEAD_DIM), lambda b, i: (b, i, 0)),
        out_shape=jax.ShapeDtypeStruct((B, S, N_HEADS * HEAD_DIM), BF16),
        scratch_shapes=[
            pltpu.VMEM((nc, kc, qb), I32),
            pltpu.VMEM((nc, kc, qb), I16),
            pltpu.VMEM((nc, kc, qb), I16),
            pltpu.VMEM((SUBLANES, qb), I32),
            pltpu.VMEM((kc, qb), F32),
        ] + [pltpu.VMEM((kc, qb), F32)] * N_HEADS
          + [pltpu.VMEM((kc, qb), BF16)] * N_HEADS
          + [pltpu.VMEM((1, qb), F32)] * N_HEADS
          + [pltpu.VMEM((1, qb), F32)] * N_HEADS
          + [pltpu.VMEM((HEAD_DIM, qb), F32)] * N_HEADS,
        compiler_params=_cparams(("arbitrary", "arbitrary")),
        name="dsa_attend",
    )(qT, qiT, wT, kidxb, kb, vT, knorm, tabs)


def _mix_kernel(x_ref, attn_ref, conv_ref, wo_ref, g_ref, b_ref, h_ref, *, dn_alpha):
    cat = jnp.concatenate([attn_ref[...], conv_ref[...]], axis=1)
    mixed = _dot(cat, wo_ref[...])
    h_ref[...] = _layer_norm(dn_alpha * x_ref[...] + mixed, g_ref[...], b_ref[...])


def _mix(x2, attn2, conv2, wo, g, b, *, dn_alpha):
    T, D = x2.shape
    tt = min(1024, T)
    assert T % tt == 0
    aw, cw = attn2.shape[1], conv2.shape[1]
    row = lambda i: (i, 0)
    const = lambda i: (0, 0)
    return pl.pallas_call(
        functools.partial(_mix_kernel, dn_alpha=dn_alpha),
        grid=(T // tt,),
        in_specs=[pl.BlockSpec((tt, D), row), pl.BlockSpec((tt, aw), row), pl.BlockSpec((tt, cw), row),
                  pl.BlockSpec(wo.shape, const), pl.BlockSpec((1, D), const), pl.BlockSpec((1, D), const)],
        out_specs=pl.BlockSpec((tt, D), row),
        out_shape=jax.ShapeDtypeStruct((T, D), F32),
        compiler_params=_cparams(("arbitrary",)),
        name="outproj_ln1",
    )(x2, attn2, conv2, wo, g, b)


def _route_picks(logits):
    lane = lax.broadcasted_iota(I32, logits.shape, 1)
    work = logits
    vals, hots = [], []
    for _ in range(TOP_K):
        m = jnp.max(work, axis=1, keepdims=True)
        idx = jnp.min(jnp.where(work == m, lane, LANES), axis=1, keepdims=True)
        hot = lane == idx
        vals.append(m)
        hots.append(hot)
        work = jnp.where(hot, -jnp.inf, work)
    es = [jnp.exp(v - vals[0]) for v in vals]
    den = es[0]
    for e in es[1:]:
        den = den + e
    return hots, [e / den for e in es]


def _route(logits):
    hots, gates = _route_picks(logits)
    comb = jnp.zeros(logits.shape, F32)
    for gate, hot in zip(gates, hots):
        comb = jnp.where(hot, gate, comb)
    return comb


def _expert_mlp(xb, wgu_ref, bgu_ref, wd_ref, bd_ref):
    ff = wd_ref.shape[1]
    gu = _dot(xb, wgu_ref[0]) + bgu_ref[0]
    gate = jnp.minimum(gu[:, 0:ff], SWIGLU_LIMIT)
    up = jnp.clip(gu[:, ff:2 * ff], -SWIGLU_LIMIT, SWIGLU_LIMIT)
    act = (up + 1.0) * (gate * (1.0 / (1.0 + jnp.exp(-SWIGLU_ALPHA * gate))))
    return _dot(act.astype(BF16), wd_ref[0]) + bd_ref[0]


def _moe_kernel(h_ref, wr_ref, br_ref, wgu_ref, bgu_ref, wd_ref, bd_ref, g_ref, b_ref,
                y_ref, hb_ref, comb_ref, *, dn_alpha):
    e = pl.program_id(1)
    tt = h_ref.shape[0]

    @pl.when(e == 0)
    def _():
        hb = h_ref[...].astype(BF16)
        hb_ref[...] = hb
        comb_ref[...] = _route(_dot(hb, wr_ref[...]) + br_ref[...])
        y_ref[...] = jnp.zeros(y_ref.shape, F32)

    y = _expert_mlp(hb_ref[...], wgu_ref, bgu_ref, wd_ref, bd_ref)
    lane = lax.broadcasted_iota(I32, (tt, LANES), 1)
    y_ref[...] += jnp.sum(jnp.where(lane == e, comb_ref[...], 0.0), axis=1, keepdims=True) * y

    @pl.when(e == pl.num_programs(1) - 1)
    def _():
        y_ref[...] = _layer_norm(dn_alpha * h_ref[...] + y_ref[...], g_ref[...], b_ref[...])


def _moe(h2, wr, br, wgu, bgu, wd, bd, g, b, *, dn_alpha):
    T, D = h2.shape
    E, _, F2 = wgu.shape
    tt = min(512, T)
    assert T % tt == 0
    row = lambda i, e: (i, 0)
    const = lambda i, e: (0, 0)
    exp3 = lambda i, e: (e, 0, 0)
    return pl.pallas_call(
        functools.partial(_moe_kernel, dn_alpha=dn_alpha),
        grid=(T // tt, E),
        in_specs=[pl.BlockSpec((tt, D), row), pl.BlockSpec(wr.shape, const), pl.BlockSpec(br.shape, const),
                  pl.BlockSpec((1, D, F2), exp3), pl.BlockSpec((1, 1, F2), exp3),
                  pl.BlockSpec((1, F2 // 2, D), exp3), pl.BlockSpec((1, 1, D), exp3),
                  pl.BlockSpec((1, D), const), pl.BlockSpec((1, D), const)],
        out_specs=pl.BlockSpec((tt, D), row),
        out_shape=jax.ShapeDtypeStruct((T, D), F32),
        scratch_shapes=[pltpu.VMEM((tt, D), BF16), pltpu.VMEM((tt, LANES), F32)],
        compiler_params=_cparams(("arbitrary", "arbitrary")),
        name="moe_ln2",
    )(h2, wr, br, wgu, bgu, wd, bd, g, b)


def _pack_pairs(x):
    w = x.shape[1] // 2
    bits = pltpu.bitcast(x.astype(BF16).astype(F32), I32)
    return (bits[:, 0:w] & jnp.int32(-65536)) | lax.shift_right_logical(bits[:, w:2 * w], 16)


def _unpack_pairs(p):
    hi = pltpu.bitcast(p & jnp.int32(-65536), F32)
    lo = pltpu.bitcast(lax.shift_left(p, 16), F32)
    return jnp.concatenate([hi, lo], axis=1)


def _mix_route_kernel(x_ref, attn_ref, conv_ref, wo_ref, g_ref, b_ref, wrT_ref, brT_ref,
                      h_ref, hpk_ref, infoT_ref, gcol_ref, cnt_ref, *, dn_alpha):
    cat = jnp.concatenate([attn_ref[...], conv_ref[...]], axis=1)
    h = _layer_norm(dn_alpha * x_ref[...] + _dot(cat, wo_ref[...]), g_ref[...], b_ref[...])
    h_ref[...] = h
    tt = h.shape[0]
    ne = cnt_ref.shape[1]
    hpk_ref[...] = _pack_pairs(h)
    logits = lax.dot_general(wrT_ref[...], h.astype(BF16), (((1,), (1,)), ((), ())),
                             preferred_element_type=F32)
    work = logits[0:ne] + jnp.tile(brT_ref[0:ne, :], (1, tt // LANES))
    sub = lax.broadcasted_iota(I32, (ne, tt), 0)
    vals, idxs, hots = [], [], []
    for _ in range(TOP_K):
        m = jnp.max(work, axis=0, keepdims=True)
        idx = jnp.min(jnp.where(work == m, sub, ne), axis=0, keepdims=True)
        hot = sub == idx
        vals.append(m)
        idxs.append(idx)
        hots.append(hot)
        work = jnp.where(hot, -jnp.inf, work)
    es = [jnp.exp(v - vals[0]) for v in vals]
    den = es[0]
    for e in es[1:]:
        den = den + e
    gates = [e / den for e in es]
    picked = jnp.zeros((ne, tt), F32)
    for hot in hots:
        picked = jnp.where(hot, 1.0, picked)
    earlier = lax.broadcasted_iota(I32, (tt, tt), 0) < lax.broadcasted_iota(I32, (tt, tt), 1)
    rank = _dot(picked.astype(BF16), jnp.where(earlier, 1.0, 0.0).astype(BF16))
    ranks = [jnp.sum(jnp.where(hot, rank, 0.0), axis=0, keepdims=True) for hot in hots]
    pad = jnp.zeros((2 * SUBLANES - 3 * TOP_K, tt), F32)
    infoT_ref[...] = jnp.concatenate([i.astype(F32) for i in idxs] + gates + ranks + [pad], axis=0)
    gcol_ref[...] = jnp.concatenate(gates + [jnp.zeros((LANES - TOP_K, tt), F32)], axis=0).T
    cnt_ref[0] = jnp.broadcast_to(jnp.sum(picked, axis=1, keepdims=True), (ne, LANES))


def _mix_route(x2, attn2, conv2, wo, g, b, wr, br, ne, *, dn_alpha):
    T, D = x2.shape
    tt = ROUTE_TILE
    aw, cw = attn2.shape[1], conv2.shape[1]
    row = lambda i: (i, 0)
    const = lambda i: (0, 0)
    wrT = wr.T
    brT = jnp.broadcast_to(br.reshape(LANES, 1), (LANES, LANES))
    return pl.pallas_call(
        functools.partial(_mix_route_kernel, dn_alpha=dn_alpha),
        grid=(T // tt,),
        in_specs=[pl.BlockSpec((tt, D), row), pl.BlockSpec((tt, aw), row), pl.BlockSpec((tt, cw), row),
                  pl.BlockSpec(wo.shape, const), pl.BlockSpec((1, D), const), pl.BlockSpec((1, D), const),
                  pl.BlockSpec(wrT.shape, const), pl.BlockSpec(brT.shape, const)],
        out_specs=(pl.BlockSpec((tt, D), row), pl.BlockSpec((tt, D // 2), row),
                   pl.BlockSpec((2 * SUBLANES, tt), lambda i: (0, i)),
                   pl.BlockSpec((tt, LANES), row), pl.BlockSpec((1, ne, LANES), lambda i: (i, 0, 0))),
        out_shape=(jax.ShapeDtypeStruct((T, D), F32), jax.ShapeDtypeStruct((T, D // 2), I32),
                   jax.ShapeDtypeStruct((2 * SUBLANES, T), F32),
                   jax.ShapeDtypeStruct((T, LANES), F32), jax.ShapeDtypeStruct((T // tt, ne, LANES), F32)),
        compiler_params=_cparams(("arbitrary",)),
        name="outproj_ln1_route",
    )(x2, attn2, conv2, wo, g, b, wrT, brT)


def _sc_gather_rows(table, idx):
    M, W = idx.shape[0], table.shape[1]
    n_workers = SC_CORES * SC_SUBCORES
    per_w = M // n_workers
    n_ch = per_w // SC_ROWS
    assert M == n_workers * n_ch * SC_ROWS
    mesh = plsc.VectorSubcoreMesh(core_axis_name="c", subcore_axis_name="s")

    @functools.partial(
        pl.kernel, mesh=mesh,
        out_type=jax.ShapeDtypeStruct((M, W), table.dtype),
        scratch_types=[pltpu.VMEM((n_ch, SC_ROWS), I32), pltpu.VMEM((SC_ROWS, W), table.dtype),
                       pltpu.SemaphoreType.DMA],
    )
    def gather(table_hbm, idx_hbm, out_hbm, idx_v, rows_v, sem):
        wid = lax.axis_index("s") * SC_CORES + lax.axis_index("c")
        pltpu.sync_copy(idx_hbm.at[wid], idx_v)

        @pl.loop(0, n_ch)
        def _(j):
            pltpu.async_copy(table_hbm.at[idx_v.at[j]], rows_v, sem).wait()
            pltpu.sync_copy(rows_v, out_hbm.at[pl.ds(wid * per_w + j * SC_ROWS, SC_ROWS)])

    return gather(table, idx.reshape(n_workers, n_ch, SC_ROWS))


def _expert_rows_kernel(te_ref, used_ref, x_ref, wgu_ref, bgu_ref, wd_ref, bd_ref, y_ref):
    @pl.when(pl.program_id(0) < used_ref[0])
    def _():
        xg = _unpack_pairs(x_ref[...]).astype(BF16)
        y_ref[...] = _pack_pairs(_expert_mlp(xg, wgu_ref, bgu_ref, wd_ref, bd_ref))


def _expert_rows(tile_expert, used_tiles, xs, wgu, bgu, wd, bd):
    R, W = xs.shape
    E, D, F2 = wgu.shape
    exp3 = lambda j, te, used: (te[j], 0, 0)
    rows = lambda j, te, used: (j, 0)
    grid_spec = pltpu.PrefetchScalarGridSpec(
        num_scalar_prefetch=2,
        grid=(R // ROW_TILE,),
        in_specs=[pl.BlockSpec((ROW_TILE, W), rows),
                  pl.BlockSpec((1, D, F2), exp3), pl.BlockSpec((1, 1, F2), exp3),
                  pl.BlockSpec((1, F2 // 2, D), exp3), pl.BlockSpec((1, 1, D), exp3)],
        out_specs=pl.BlockSpec((ROW_TILE, W), rows),
    )
    return pl.pallas_call(
        _expert_rows_kernel,
        grid_spec=grid_spec,
        out_shape=jax.ShapeDtypeStruct((R, W), I32),
        compiler_params=_cparams(("arbitrary",)),
        name="moe_expert_rows",
    )(tile_expert, used_tiles, xs, wgu, bgu, wd, bd)


def _combine_kernel(h_ref, yg_ref, gcol_ref, g_ref, b_ref, o_ref, *, dn_alpha):
    acc = dn_alpha * h_ref[...]
    for k in range(TOP_K):
        acc = acc + gcol_ref[:, k:k + 1] * _unpack_pairs(yg_ref[k])
    o_ref[...] = _layer_norm(acc, g_ref[...], b_ref[...])


def _combine(h2, yg, gcol, g, b, *, dn_alpha):
    T, D = h2.shape
    tt = 512
    row = lambda i: (i, 0)
    const = lambda i: (0, 0)
    return pl.pallas_call(
        functools.partial(_combine_kernel, dn_alpha=dn_alpha),
        grid=(T // tt,),
        in_specs=[pl.BlockSpec((tt, D), row), pl.BlockSpec((TOP_K, tt, D // 2), lambda i: (0, i, 0)),
                  pl.BlockSpec((tt, LANES), row),
                  pl.BlockSpec((1, D), const), pl.BlockSpec((1, D), const)],
        out_specs=pl.BlockSpec((tt, D), row),
        out_shape=jax.ShapeDtypeStruct((T, D), F32),
        compiler_params=_cparams(("arbitrary",)),
        name="moe_combine_ln2",
    )(h2, yg, gcol, g, b)


def _dest_kernel(infoT_ref, base_ref, dest_ref):
    tt = infoT_ref.shape[1]
    ne = base_ref.shape[1]
    base = jnp.tile(base_ref[0], (1, tt // LANES))
    sub = lax.broadcasted_iota(I32, (ne, tt), 0)
    rows = []
    for k in range(TOP_K):
        hot = sub == infoT_ref[k:k + 1, :].astype(I32)
        rows.append(jnp.sum(jnp.where(hot, base, 0.0), axis=0, keepdims=True)
                    + infoT_ref[2 * TOP_K + k:2 * TOP_K + k + 1, :])
    rows.append(jnp.zeros((SUBLANES - TOP_K, tt), F32))
    dest_ref[...] = jnp.concatenate(rows, axis=0).astype(I32)


def _dest_rows(infoT, base):
    T = infoT.shape[1]
    tt = ROUTE_TILE
    ne = base.shape[1]
    return pl.pallas_call(
        _dest_kernel,
        grid=(T // tt,),
        in_specs=[pl.BlockSpec((2 * SUBLANES, tt), lambda i: (0, i)), pl.BlockSpec((1, ne, LANES), lambda i: (i, 0, 0))],
        out_specs=pl.BlockSpec((SUBLANES, tt), lambda i: (0, i)),
        out_shape=jax.ShapeDtypeStruct((SUBLANES, T), I32),
        compiler_params=_cparams(("arbitrary",)),
        name="moe_dest_rows",
    )(infoT, base)


def _sc_scatter_rows(src, dest_km, n_rows):
    T, W = src.shape
    K = dest_km.shape[0]
    n_workers = SC_CORES * SC_SUBCORES
    per_w = T // n_workers
    n_ch = per_w // SC_ROWS
    assert T == n_workers * n_ch * SC_ROWS
    mesh = plsc.VectorSubcoreMesh(core_axis_name="c", subcore_axis_name="s")
    idx = dest_km.reshape(K, n_workers, n_ch, SC_ROWS).transpose(1, 0, 2, 3).reshape(n_workers, K * n_ch, SC_ROWS)

    @functools.partial(
        pl.kernel, mesh=mesh,
        out_type=jax.ShapeDtypeStruct((n_rows, W), src.dtype),
        scratch_types=[pltpu.VMEM((K * n_ch, SC_ROWS), I32), pltpu.VMEM((SC_ROWS, W), src.dtype),
                       pltpu.SemaphoreType.DMA],
    )
    def scatter(src_hbm, idx_hbm, out_hbm, idx_v, rows_v, sem):
        wid = lax.axis_index("s") * SC_CORES + lax.axis_index("c")
        pltpu.sync_copy(idx_hbm.at[wid], idx_v)

        @pl.loop(0, n_ch)
        def _(j):
            pltpu.sync_copy(src_hbm.at[pl.ds(wid * per_w + j * SC_ROWS, SC_ROWS)], rows_v)
            for k in range(K):
                pltpu.async_copy(rows_v, out_hbm.at[idx_v.at[k * n_ch + j]], sem).wait()

    return scatter(src, idx)


def _moe_sorted(h2, routed, wgu, bgu, wd, bd, g, b, *, dn_alpha):
    T, D = h2.shape
    E = wgu.shape[0]
    hpk, infoT, gcol, cnt = routed
    cnt = cnt[:, :, 0].astype(I32)
    total = jnp.sum(cnt, axis=0)
    padded = -(-total // ROW_TILE) * ROW_TILE
    ends = jnp.cumsum(padded)
    base = (ends - padded)[None, :] + jnp.cumsum(cnt, axis=0) - cnt
    base = jnp.broadcast_to(base.astype(F32)[:, :, None], cnt.shape + (LANES,))
    dest_km = _dest_rows(infoT, base)[0:TOP_K]
    R = T * TOP_K + E * ROW_TILE
    tile_start = jnp.arange(R // ROW_TILE, dtype=I32) * ROW_TILE
    tile_expert = jnp.minimum(jnp.sum((ends[None, :] <= tile_start[:, None]).astype(I32), axis=1), E - 1)
    used_tiles = (ends[E - 1] // ROW_TILE).reshape(1)
    xs = _sc_scatter_rows(hpk, dest_km, R)
    ys = _expert_rows(tile_expert, used_tiles, xs, wgu, bgu, wd, bd)
    yg = _sc_gather_rows(ys, dest_km.reshape(-1))
    return _combine(h2, yg.reshape(TOP_K, T, D // 2), gcol, g, b, dn_alpha=dn_alpha)


def _split_gu_kernel(w_ref, perm_ref, o_ref, *, band):
    n = w_ref.shape[2]
    for m in range(n // band):
        both = _dot(w_ref[0, :, m * band:(m + 1) * band].astype(BF16), perm_ref[...]).astype(BF16)
        o_ref[0, :, m * (band // 2):(m + 1) * (band // 2)] = both[:, 0:band // 2]
        o_ref[0, :, n // 2 + m * (band // 2):n // 2 + (m + 1) * (band // 2)] = both[:, band // 2:band]


def _split_gu(w_gu):
    E, D, N = w_gu.shape
    tr = min(1024, D)
    band = min(512, N)
    j = jnp.arange(band, dtype=I32)
    src = jnp.where(j < band // 2, 2 * j, 2 * (j - band // 2) + 1)
    perm = (jnp.arange(band, dtype=I32)[:, None] == src[None, :]).astype(BF16)
    return pl.pallas_call(
        functools.partial(_split_gu_kernel, band=band),
        grid=(E, D // tr),
        in_specs=[pl.BlockSpec((1, tr, N), lambda e, r: (e, r, 0)),
                  pl.BlockSpec((band, band), lambda e, r: (0, 0))],
        out_specs=pl.BlockSpec((1, tr, N), lambda e, r: (e, r, 0)),
        out_shape=jax.ShapeDtypeStruct((E, D, N), BF16),
        compiler_params=_cparams(("arbitrary", "arbitrary")),
        name="split_gate_up",
    )(w_gu, perm)


def _prep_weights(w_in, w_conv, w_out, ln1_g, ln1_b, w_router, b_router, w_gu, b_gu, w_down, b_down,
                  ln2_g, ln2_b):
    aw = N_HEADS * HEAD_DIM
    kv = N_KV_HEADS * HEAD_DIM
    iw = N_IDX_HEADS * IDX_DIM
    D = w_in.shape[0]
    cw = D - aw
    o_q, o_k, o_v = 0, aw, aw + kv
    o_qi = o_v + kv
    o_ki = o_qi + iw
    o_wi = o_ki + IDX_DIM
    o_c = o_wi + N_IDX_HEADS
    wa = jnp.concatenate([w_in[:, o_q:o_k], w_in[:, o_qi:o_ki], w_in[:, o_k:o_v], w_in[:, o_v:o_qi]],
                         axis=1).astype(BF16)
    ws = jnp.pad(w_in[:, o_ki:o_c], ((0, 0), (0, LANES - IDX_DIM - N_IDX_HEADS))).astype(BF16)
    wc = w_in[:, o_c:o_c + 3 * cw].astype(BF16)
    E = w_router.shape[1]
    wr = jnp.pad(w_router, ((0, 0), (0, LANES - E))).astype(BF16)
    br = jnp.pad(b_router, (0, LANES - E), constant_values=MASK_VALUE).reshape(1, LANES)
    F = w_gu.shape[2] // 2
    return dict(
        wa=wa, ws=ws, wc=wc, wconv=w_conv, wo=w_out.astype(BF16),
        ln1_g=ln1_g.reshape(1, D), ln1_b=ln1_b.reshape(1, D),
        wr=wr, br=br,
        wgu=_split_gu(w_gu),
        bgu=jnp.concatenate([b_gu[:, 0::2], b_gu[:, 1::2]], axis=1).reshape(E, 1, 2 * F),
        wd=w_down.astype(BF16), bd=b_down.reshape(E, 1, D),
        ln2_g=ln2_g.reshape(1, D), ln2_b=ln2_b.reshape(1, D),
    )


def _decode_layouts(pq, small, k_all, v_all, kidx_all, kc, qb):
    B, S, _ = pq.shape
    aw = N_HEADS * HEAD_DIM
    L, kv = k_all.shape[1], k_all.shape[2]
    lp = -(-L // kc) * kc
    nc = lp // kc
    padk = lambda a: jnp.pad(a, ((0, 0), (0, lp - L), (0, 0)))
    padq = lambda a: jnp.pad(a, ((0, 0), (0, 0), (0, 0), (0, qb - S)))
    q = pq[:, :, 0:aw] * QK_SCALE
    qT = padq(q.reshape(B, S, N_HEADS, HEAD_DIM).transpose(0, 2, 3, 1)).astype(BF16)
    qiT = padq(pq[:, :, aw:].reshape(B, S, N_IDX_HEADS, IDX_DIM).transpose(0, 2, 3, 1)).astype(BF16)
    wT = jnp.pad(small[:, :, IDX_DIM:IDX_DIM + N_IDX_HEADS].transpose(0, 2, 1), ((0, 0), (0, 0), (0, qb - S)))
    kb = padk(k_all).reshape(B, nc, kc, N_KV_HEADS, HEAD_DIM).transpose(0, 1, 3, 2, 4).astype(BF16)
    vT = padk(v_all).reshape(B, nc, kc, kv).transpose(0, 1, 3, 2).astype(BF16)
    kidxb = padk(kidx_all).reshape(B, nc, kc, IDX_DIM).astype(BF16)
    knorm = jnp.sum(kb.astype(F32) ** 2, axis=-1)
    return qT, qiT, wT, kidxb, kb, vT, knorm


def _layer(x, past_k, past_v, past_kidx, conv_buf, rel_bias, w, *, dn_alpha):
    B, S, D = x.shape
    prefill = past_k is None
    outs = _project(x, conv_buf, w["wa"], w["ws"], w["wc"], w["wconv"], attn_layouts=prefill)
    k, v, small, conv, new_buf = outs[:5]
    k_idx = small if prefill else small[:, :, 0:IDX_DIM]
    if prefill:
        past, l_true = 0, S
        qb = min(QUERY_BLOCK, S)
        kb, vT, kidxb, qT, qiT, wT, knorm = outs[5:]
    else:
        past = past_k.shape[1]
        l_true = past + S
        qb = -(-S // LANES) * LANES
        qT, qiT, wT, kidxb, kb, vT, knorm = _decode_layouts(
            outs[5], small, jnp.concatenate([past_k, k], axis=1), jnp.concatenate([past_v, v], axis=1),
            jnp.concatenate([past_kidx, k_idx], axis=1), math.gcd(past, KEY_CHUNK), qb)
    offsets = (-LANES, 0)
    tabs = _bias_tables(rel_bias, offsets)
    attn = _attend(qT, qiT, wT, kidxb, kb, vT, knorm, tabs, qb=qb, past=past, l_true=l_true, offsets=offsets)
    attn = attn[:, :S]
    mix_args = (x.reshape(B * S, D), attn.reshape(B * S, -1), conv.reshape(B * S, -1), w["wo"],
                w["ln1_g"], w["ln1_b"])
    experts = (w["wgu"], w["bgu"], w["wd"], w["bd"], w["ln2_g"], w["ln2_b"])
    sc_unit = SC_CORES * SC_SUBCORES * SC_ROWS
    sortable = ((B * S) % ROUTE_TILE == 0 and (B * S * TOP_K) % sc_unit == 0
                and (N_EXPERTS * ROW_TILE) % sc_unit == 0)
    if sortable and B * S >= SORTED_MIN_TOKENS:
        h, *routed = _mix_route(*mix_args, w["wr"], w["br"], N_EXPERTS, dn_alpha=dn_alpha)
        y = _moe_sorted(h, routed, *experts, dn_alpha=dn_alpha)
    else:
        h = _mix(*mix_args, dn_alpha=dn_alpha)
        y = _moe(h, w["wr"], w["br"], *experts, dn_alpha=dn_alpha)
    return (y.reshape(B, S, D), k.reshape(B, S, N_KV_HEADS, HEAD_DIM), v.reshape(B, S, N_KV_HEADS, HEAD_DIM),
            k_idx, new_buf)


def kernel(x_prompt, x_sample, cache_k, cache_v, cache_kidx, state_conv, rel_bias, w_in, w_conv, w_out,
           ln1_g, ln1_b, w_router, b_router, w_gu, b_gu, w_down, b_down, ln2_g, ln2_b):
    depth = w_in.shape[0]
    assert depth == 1
    dn_alpha = (2 * depth) ** 0.25
    kv = N_KV_HEADS * HEAD_DIM
    w = _prep_weights(w_in[0], w_conv[0], w_out[0], ln1_g[0], ln1_b[0], w_router[0], b_router[0],
                      w_gu[0], b_gu[0], w_down[0], b_down[0], ln2_g[0], ln2_b[0])
    Bp = x_prompt.shape[0]
    cw = w_conv.shape[2]
    zero_buf = jnp.zeros((Bp, CONV_K - 1, cw), F32)
    yp, k1, v1, i1, c1 = _layer(x_prompt, None, None, None, zero_buf, rel_bias, w, dn_alpha=dn_alpha)
    Bs, P = cache_k.shape[1], cache_k.shape[2]
    ys, k2, v2, i2, c2 = _layer(x_sample, cache_k[0].reshape(Bs, P, kv), cache_v[0].reshape(Bs, P, kv),
                                cache_kidx[0], state_conv[0], rel_bias, w, dn_alpha=dn_alpha)
    return (yp, ys, k1[None], v1[None], i1[None], c1[None], k2[None], v2[None], i2[None], c2[None])
```

```python
import functools
import math

import jax
import jax.numpy as jnp
from jax import lax
from jax.experimental import pallas as pl
from jax.experimental.pallas import tpu as pltpu
from jax.experimental.pallas import tpu_sc as plsc

F32 = jnp.float32
BF16 = jnp.bfloat16
I32 = jnp.int32
I16 = jnp.int16

CHUNK = 64
N_HEADS = 8
HEAD_DIM = 64
N_KV_HEADS = 2
GROUP = N_HEADS // N_KV_HEADS
N_IDX_HEADS = 8
IDX_DIM = 32
TOPK_MAX = 256
CONV_K = 3
N_BUCKETS = 32
MAX_DISTANCE = 128
N_EXPERTS = 32
TOP_K = 4
SWIGLU_LIMIT = 7.0
SWIGLU_ALPHA = 1.702
LN_EPS = 1e-5
MASK_VALUE = -1e30
QK_SCALE = HEAD_DIM ** -0.5 * math.log2(math.e)
EXP2_SAFE = 96.0

LANES = 128
SUBLANES = 8
KEY_CHUNK = 512
QUERY_BLOCK = 512
SLAB = 64
ROW_TILE = 512
ROUTE_TILE = 1024
SC_CORES = 2
SC_SUBCORES = 16
SC_ROWS = 128
SORTED_MIN_TOKENS = 2048
INT_MIN = -(2 ** 31)
VMEM_LIMIT = 56 * 1024 * 1024


def _cparams(sem):
    return pltpu.CompilerParams(dimension_semantics=sem, vmem_limit_bytes=VMEM_LIMIT)


def _dot(a, b):
    return jnp.dot(a, b, preferred_element_type=F32)


def _layer_norm(z, g, b):
    mu = jnp.mean(z, axis=-1, keepdims=True)
    zc = z - mu
    var = jnp.mean(zc * zc, axis=-1, keepdims=True)
    return zc * lax.rsqrt(var + LN_EPS) * g + b


def _bucket_thresholds():
    nb = N_BUCKETS // 2
    max_exact = nb // 2
    out = []
    for j in range(1, nb - max_exact):
        out.append(math.ceil(max_exact * (MAX_DISTANCE / max_exact) ** (j / (nb - max_exact)) - 1e-9))
    return tuple(out)


def _bias_table_kernel(rel_ref, tab_ref, *, offsets):
    nb = N_BUCKETS // 2
    max_exact = nb // 2
    thr = _bucket_thresholds()
    ii = lax.broadcasted_iota(I32, (LANES, LANES), 0)
    jj = lax.broadcasted_iota(I32, (LANES, LANES), 1)
    for d, off in enumerate(offsets):
        rel = off + ii - jj
        n = jnp.abs(rel)
        large = jnp.full((LANES, LANES), max_exact, I32)
        for t in thr:
            large = large + jnp.where(n >= t, 1, 0)
        bucket = jnp.where(rel > 0, nb, 0) + jnp.where(n < max_exact, n, large)
        for h in range(N_HEADS):
            acc = jnp.zeros((LANES, LANES), F32)
            for b in range(N_BUCKETS):
                acc = jnp.where(bucket == b, rel_ref[b, h], acc)
            tab_ref[d, h] = (acc - rel_ref[nb - 1, h]) * math.log2(math.e)


def _bias_tables(rel_bias, offsets):
    return pl.pallas_call(
        functools.partial(_bias_table_kernel, offsets=offsets),
        out_shape=jax.ShapeDtypeStruct((len(offsets), N_HEADS, LANES, LANES), F32),
        in_specs=[pl.BlockSpec(memory_space=pltpu.SMEM)],
        out_specs=pl.BlockSpec(memory_space=pltpu.VMEM),
        name="bias_tables",
    )(rel_bias)


def _proj_kernel(x_ref, wa_ref, ws_ref, wc_ref, wconv_ref, buf_ref,
                 k_ref, v_ref, small_ref, conv_ref, nbuf_ref, *rest, ts, attn_layouts):
    s = pl.program_id(1)
    xb = x_ref[0].astype(BF16)
    pa = _dot(xb, wa_ref[...])
    ps = _dot(xb, ws_ref[...])
    pc = _dot(xb, wc_ref[...])
    aw = N_HEADS * HEAD_DIM
    iw = N_IDX_HEADS * IDX_DIM
    kv = N_KV_HEADS * HEAD_DIM
    k = pa[:, aw + iw:aw + iw + kv]
    v = pa[:, aw + iw + kv:aw + iw + 2 * kv]
    k_ref[0] = k
    v_ref[0] = v
    small_ref[0] = ps[:, 0:small_ref.shape[2]]
    scale = QK_SCALE
    if attn_layouts:
        kb_ref, vT_ref, kidxb_ref, qT_ref, qiT_ref, wT_ref, knorm_ref, carry_ref = rest
        for n in range(N_KV_HEADS):
            kb_ref[0, 0, n] = k[:, n * HEAD_DIM:(n + 1) * HEAD_DIM].astype(BF16)
        kTf = k.astype(BF16).astype(F32).T
        knorm_ref[0, 0] = jnp.concatenate(
            [jnp.sum(kTf[n * HEAD_DIM:(n + 1) * HEAD_DIM] ** 2, axis=0, keepdims=True) for n in range(N_KV_HEADS)],
            axis=0)
        vT_ref[0, 0] = v.T.astype(BF16)
        kidxb_ref[0, 0] = ps[:, 0:IDX_DIM].astype(BF16)
        qT = (pa[:, 0:aw] * scale).T.astype(BF16)
        for h in range(N_HEADS):
            qT_ref[0, h] = qT[h * HEAD_DIM:(h + 1) * HEAD_DIM]
        qiT = pa[:, aw:aw + iw].T.astype(BF16)
        for h in range(N_IDX_HEADS):
            qiT_ref[0, h] = qiT[h * IDX_DIM:(h + 1) * IDX_DIM]
        wT_ref[0] = ps.T[IDX_DIM:IDX_DIM + N_IDX_HEADS]
    else:
        pq_ref, carry_ref = rest
        pq_ref[0] = pa[:, 0:aw + iw]

    cw = pc.shape[1] // 3
    u = pc[:, cw:2 * cw] * pc[:, 2 * cw:3 * cw]

    @pl.when(s == 0)
    def _():
        carry_ref[6:8, :] = buf_ref[0]

    carry_ref[8:8 + ts, :] = u
    y = (carry_ref[6:6 + ts, :] * wconv_ref[0:1, :]
         + carry_ref[7:7 + ts, :] * wconv_ref[1:2, :]
         + u * wconv_ref[2:3, :])
    conv_ref[0] = (pc[:, 0:cw] * y).astype(BF16)
    nb = carry_ref[ts + 6:ts + 8, :]
    nbuf_ref[0] = nb
    carry_ref[6:8, :] = nb


def _project(x, conv_buf, wa, ws, wc, wconv, *, attn_layouts):
    B, S, D = x.shape
    ts = min(KEY_CHUNK, S)
    assert S % ts == 0 and S >= CONV_K - 1
    ns = S // ts
    cw = wc.shape[1] // 3
    sw = IDX_DIM if attn_layouts else LANES
    kv = N_KV_HEADS * HEAD_DIM
    aw = N_HEADS * HEAD_DIM
    iw = N_IDX_HEADS * IDX_DIM
    out_shape = [
        jax.ShapeDtypeStruct((B, S, kv), F32),
        jax.ShapeDtypeStruct((B, S, kv), F32),
        jax.ShapeDtypeStruct((B, S, sw), F32),
        jax.ShapeDtypeStruct((B, S, cw), BF16),
        jax.ShapeDtypeStruct((B, CONV_K - 1, cw), F32),
    ]
    out_specs = [
        pl.BlockSpec((1, ts, kv), lambda b, s: (b, s, 0)),
        pl.BlockSpec((1, ts, kv), lambda b, s: (b, s, 0)),
        pl.BlockSpec((1, ts, sw), lambda b, s: (b, s, 0)),
        pl.BlockSpec((1, ts, cw), lambda b, s: (b, s, 0)),
        pl.BlockSpec((1, CONV_K - 1, cw), lambda b, s: (b, 0, 0)),
    ]
    if attn_layouts:
        out_shape += [
            jax.ShapeDtypeStruct((B, ns, N_KV_HEADS, ts, HEAD_DIM), BF16),
            jax.ShapeDtypeStruct((B, ns, kv, ts), BF16),
            jax.ShapeDtypeStruct((B, ns, ts, IDX_DIM), BF16),
            jax.ShapeDtypeStruct((B, N_HEADS, HEAD_DIM, S), BF16),
            jax.ShapeDtypeStruct((B, N_IDX_HEADS, IDX_DIM, S), BF16),
            jax.ShapeDtypeStruct((B, N_IDX_HEADS, S), F32),
            jax.ShapeDtypeStruct((B, ns, N_KV_HEADS, ts), F32),
        ]
        out_specs += [
            pl.BlockSpec((1, 1, N_KV_HEADS, ts, HEAD_DIM), lambda b, s: (b, s, 0, 0, 0)),
            pl.BlockSpec((1, 1, kv, ts), lambda b, s: (b, s, 0, 0)),
            pl.BlockSpec((1, 1, ts, IDX_DIM), lambda b, s: (b, s, 0, 0)),
            pl.BlockSpec((1, N_HEADS, HEAD_DIM, ts), lambda b, s: (b, 0, 0, s)),
            pl.BlockSpec((1, N_IDX_HEADS, IDX_DIM, ts), lambda b, s: (b, 0, 0, s)),
            pl.BlockSpec((1, N_IDX_HEADS, ts), lambda b, s: (b, 0, s)),
            pl.BlockSpec((1, 1, N_KV_HEADS, ts), lambda b, s: (b, s, 0, 0)),
        ]
    else:
        out_shape += [jax.ShapeDtypeStruct((B, S, aw + iw), F32)]
        out_specs += [pl.BlockSpec((1, ts, aw + iw), lambda b, s: (b, s, 0))]
    const2 = lambda b, s: (0, 0)
    in_specs = [
        pl.BlockSpec((1, ts, D), lambda b, s: (b, s, 0)),
        pl.BlockSpec(wa.shape, const2),
        pl.BlockSpec(ws.shape, const2),
        pl.BlockSpec(wc.shape, const2),
        pl.BlockSpec(wconv.shape, const2),
        pl.BlockSpec((1, CONV_K - 1, cw), lambda b, s: (b, 0, 0)),
    ]
    return pl.pallas_call(
        functools.partial(_proj_kernel, ts=ts, attn_layouts=attn_layouts),
        grid=(B, ns),
        in_specs=in_specs,
        out_specs=tuple(out_specs),
        out_shape=tuple(out_shape),
        scratch_shapes=[pltpu.VMEM((ts + 8, cw), F32)],
        compiler_params=_cparams(("arbitrary", "arbitrary")),
        name="proj_conv",
    )(x, wa, ws, wc, wconv, conv_buf)


def _attn_kernel(qT_ref, qiT_ref, wT_ref, kidx_ref, kb_ref, vT_ref, knorm_ref, tab_ref,
                 o_ref, skey_ref, hi16_ref, lo16_ref, jb_ref, madd_ref, *head_refs,
                 qb, kc, past, l_true, topk, offsets):
    s_refs, p_refs, m_refs, l_refs, acc_refs = (head_refs[g * N_HEADS:(g + 1) * N_HEADS] for g in range(5))
    i = pl.program_id(1)
    qoff = past + i * qb
    adm_end = jnp.minimum(((qoff + qb - 1) // CHUNK + 1) * CHUNK, l_true)
    nck = (adm_end + kc - 1) // kc
    idx_bits = int(l_true).bit_length()
    groups = kc // SUBLANES

    rowi = lax.broadcasted_iota(I32, (kc, qb), 0)
    qpos = qoff + lax.broadcasted_iota(I32, (1, qb), 1)
    kmax = jnp.minimum((qpos // CHUNK + 1) * CHUNK, l_true)

    def score_body(c, carry):
        kidx_c = kidx_ref[0, c]
        sc = jnp.zeros((kc, qb), F32)
        for h in range(N_IDX_HEADS):
            d = _dot(kidx_c, qiT_ref[0, h])
            sc = sc + wT_ref[0, h:h + 1, :] * jnp.maximum(d, 0.0)
        bits = pltpu.bitcast(sc, I32)
        skey = jnp.where(bits < 0, bits ^ jnp.int32(0x7FFFFFFF), bits)
        skey = jnp.where(skey == -1, 0, skey)
        skey = jnp.where(rowi < kmax - c * kc, skey, jnp.int32(INT_MIN))
        skey_ref[c] = skey
        hi16_ref[c] = (skey >> 16).astype(I16)
        return carry

    lax.fori_loop(0, nck, score_body, 0)

    def count(pred_fn):
        def body(c, part):
            ind = jnp.where(pred_fn(c, skey_ref[c]), 1, 0)
            return part + jnp.sum(ind.reshape(groups, SUBLANES, qb), axis=0)
        part = lax.fori_loop(0, nck, body, jnp.zeros((SUBLANES, qb), I32))
        return jnp.sum(part, axis=0, keepdims=True)

    def count16(ref, pred_fn):
        pack = 2 * SUBLANES

        def body(c, part):
            accs = [None] * 4
            for g in range(kc // pack):
                ind = jnp.where(pred_fn(ref[c, g * pack:(g + 1) * pack, :]), jnp.bfloat16(1), jnp.bfloat16(0))
                accs[g % 4] = ind if accs[g % 4] is None else accs[g % 4] + ind
            return part + ((accs[0] + accs[1]) + (accs[2] + accs[3])).astype(F32)
        part = lax.fori_loop(0, nck, body, jnp.zeros((pack, qb), F32))
        return jnp.sum(part, axis=0, keepdims=True).astype(I32)

    def search16(ref, target, count_all):
        def body(it, carry):
            t_u, c_t = carry
            cand_u = t_u | (jnp.int32(1) << (15 - it))
            cand = (cand_u - 2 ** 15).astype(I16)
            cnt = count16(ref, lambda k: k >= cand)
            keep = cnt >= target
            return jnp.where(keep, cand_u, t_u), jnp.where(keep, cnt, c_t)
        return lax.fori_loop(0, 16, body, (jnp.zeros((1, qb), I32), count_all))

    stored = jnp.broadcast_to(nck * kc, (1, qb)).astype(I32)
    hi_u, n_ge_hi = search16(hi16_ref, topk, stored)
    hi_s = (hi_u - 2 ** 15).astype(I16)
    n_gt_hi = count16(hi16_ref, lambda k: k > hi_s)
    rest = topk - n_gt_hi

    def low_body(c, carry):
        low = ((skey_ref[c] & 0xFFFF) - 2 ** 15).astype(I16)
        lo16_ref[c] = jnp.where(hi16_ref[c] == hi_s, low, jnp.int16(-2 ** 15))
        return carry

    lax.fori_loop(0, nck, low_body, 0)
    lo_u, n_ge_lo = search16(lo16_ref, rest, n_ge_hi - n_gt_hi)
    thr = (hi_u - 2 ** 15) * 2 ** 16 + lo_u
    lo_s = (lo_u - 2 ** 15).astype(I16)
    cnt_ge = n_gt_hi + n_ge_lo
    cnt_gt = n_gt_hi + count16(lo16_ref, lambda k: k > lo_s)

    jb_ref[...] = jnp.full(jb_ref.shape, 2 ** 31 - 1, I32)

    @pl.when(jnp.max(cnt_ge) > topk)
    def _():
        need = topk - cnt_gt

        def tie_body(it, jb):
            cand = jb | (jnp.int32(1) << (idx_bits - 1 - it))
            cnt = count(lambda c, sk: jnp.logical_and(sk == thr, c * kc + rowi < cand))
            return jnp.where(cnt <= need, cand, jb)

        jb = lax.fori_loop(0, idx_bits, tie_body, jnp.zeros((1, qb), I32))
        jb_ref[...] = jnp.broadcast_to(jb, jb_ref.shape)

    jbound = jnp.where(thr == jnp.int32(INT_MIN), 0, jb_ref[0:1, :])
    thr_m1 = thr - 1

    for h in range(N_HEADS):
        l_refs[h][...] = jnp.zeros((1, qb), F32)
        acc_refs[h][...] = jnp.zeros((HEAD_DIM, qb), F32)
    n_slabs = kc // SLAB

    def fold(x, op):
        return op(x.reshape(SLAB // SUBLANES, SUBLANES, qb), axis=0)

    def near_bias(chunk_offset, h):
        tiles = []
        for t in range(kc // LANES):
            row = []
            for u in range(qb // LANES):
                d_tu = chunk_offset + (t - u) * LANES
                row.append(tab_ref[offsets.index(d_tu), h] if d_tu in offsets else jnp.zeros((LANES, LANES), F32))
            tiles.append(jnp.concatenate(row, axis=1))
        return jnp.concatenate(tiles, axis=0)

    def attn_chunk(c, near):
        sk = skey_ref[c]
        t_eff = jnp.where(rowi < jbound - c * kc, thr_m1, thr)
        madd_ref[...] = jnp.where(sk > t_eff, 0.0, MASK_VALUE)
        maccs = []
        for h in range(N_HEADS):
            x = _dot(kb_ref[0, c, h // GROUP], qT_ref[0, h]) + madd_ref[...]
            if near is not None:
                x = x + near_bias(near, h)
            s_refs[h][...] = x
            maccs.append(jnp.max(x.reshape(groups, SUBLANES, qb), axis=0))
        alphas = []
        for h in range(N_HEADS):
            s_ref, p_ref, m_ref = s_refs[h], p_refs[h], m_refs[h]
            m_old = m_ref[...]
            m_new = jnp.maximum(m_old, jnp.max(maccs[h], axis=0, keepdims=True))
            for j in range(n_slabs):
                rows = pl.ds(j * SLAB, SLAB)
                p_ref[rows, :] = jnp.exp2(s_ref[rows, :] - m_new).astype(BF16)
            m_ref[...] = m_new
            alphas.append(jnp.exp2(m_old - m_new))
        ones = jnp.ones((2 * SUBLANES, kc), BF16)
        v_aug = [jnp.concatenate([vT_ref[0, c, n * HEAD_DIM:(n + 1) * HEAD_DIM, :], ones], axis=0)
                 for n in range(N_KV_HEADS)]
        for h in range(N_HEADS):
            pv = _dot(v_aug[h // GROUP], p_refs[h][...])
            acc_refs[h][...] = alphas[h] * acc_refs[h][...] + pv[0:HEAD_DIM]
            l_refs[h][...] = alphas[h] * l_refs[h][...] + pv[HEAD_DIM:HEAD_DIM + 1]

    def attn_chunk_fixed(c, near):
        sk = skey_ref[c]
        t_eff = jnp.where(rowi < jbound - c * kc, thr_m1, thr)
        madd_ref[...] = jnp.where(sk > t_eff, 0.0, MASK_VALUE)
        ones = jnp.ones((2 * SUBLANES, kc), BF16)
        v_aug = [jnp.concatenate([vT_ref[0, c, n * HEAD_DIM:(n + 1) * HEAD_DIM, :], ones], axis=0)
                 for n in range(N_KV_HEADS)]
        for h in range(N_HEADS):
            x = _dot(kb_ref[0, c, h // GROUP], qT_ref[0, h]) + madd_ref[...]
            if near is not None:
                x = x + near_bias(near, h)
            p_refs[h][...] = jnp.exp2(x).astype(BF16)
        for h in range(N_HEADS):
            pv = _dot(v_aug[h // GROUP], p_refs[h][...])
            acc_refs[h][...] += pv[0:HEAD_DIM]
            l_refs[h][...] += pv[HEAD_DIM:HEAD_DIM + 1]

    def run(chunk_fn):
        c_diag = qoff // kc
        lax.fori_loop(0, c_diag - 1, lambda c, carry: (chunk_fn(c, None), carry)[1], 0)

        @pl.when(c_diag >= 1)
        def _():
            chunk_fn(c_diag - 1, -kc)

        chunk_fn(c_diag, 0)

    q_sq = jnp.zeros((1, qb), F32)
    for h in range(N_HEADS):
        qf = qT_ref[0, h].astype(F32)
        q_sq = jnp.maximum(q_sq, jnp.sum(qf * qf, axis=0, keepdims=True))
    bound = jnp.sqrt(jnp.max(q_sq) * jnp.max(knorm_ref[0])) + jnp.max(jnp.abs(tab_ref[...]))
    fixed_shift = bound < EXP2_SAFE

    @pl.when(fixed_shift)
    def _():
        run(attn_chunk_fixed)

    @pl.when(jnp.logical_not(fixed_shift))
    def _():
        for h in range(N_HEADS):
            m_refs[h][...] = jnp.full((1, qb), MASK_VALUE, F32)
        run(attn_chunk)

    oT = jnp.concatenate([acc_refs[h][...] / l_refs[h][...] for h in range(N_HEADS)], axis=0)
    o_ref[0] = oT.T.astype(o_ref.dtype)


def _attend(qT, qiT, wT, kidxb, kb, vT, knorm, tabs, *, qb, past, l_true, offsets):
    B, _, _, S = qT.shape
    nc, kc = vT.shape[1], vT.shape[3]
    assert S % qb == 0 and qb % LANES == 0 and kc % LANES == 0
    assert past % kc == 0 and qb <= kc and (qb == kc or S == qb)
    topk = min(TOPK_MAX, l_true // 4)
    kern = functools.partial(_attn_kernel, qb=qb, kc=kc, past=past, l_true=l_true,
                             topk=topk, offsets=offsets)
    in_specs = [
        pl.BlockSpec((1, N_HEADS, HEAD_DIM, qb), lambda b, i: (b, 0, 0, i)),
        pl.BlockSpec((1, N_IDX_HEADS, IDX_DIM, qb), lambda b, i: (b, 0, 0, i)),
        pl.BlockSpec((1, N_IDX_HEADS, qb), lambda b, i: (b, 0, i)),
        pl.BlockSpec((1,) + kidxb.shape[1:], lambda b, i: (b, 0, 0, 0)),
        pl.BlockSpec((1,) + kb.shape[1:], lambda b, i: (b, 0, 0, 0, 0)),
        pl.BlockSpec((1,) + vT.shape[1:], lambda b, i: (b, 0, 0, 0)),
        pl.BlockSpec((1,) + knorm.shape[1:], lambda b, i: (b, 0, 0, 0)),
        pl.BlockSpec(tabs.shape, lambda b, i: (0, 0, 0, 0)),
    ]
    return pl.pallas_call(
        kern,
        grid=(B, S // qb),
        in_specs=in_specs,
        out_specs=pl.BlockSpec((1, qb, N_HEADS * HEAD_DIM), lambda b, i: (b, i, 0)),
        out_shape=jax.ShapeDtypeStruct((B, S, N_HEADS * HEAD_DIM), BF16),
        scratch_shapes=[
            pltpu.VMEM((nc, kc, qb), I32),
            pltpu.VMEM((nc, kc, qb), I16),
            pltpu.VMEM((nc, kc, qb), I16),
            pltpu.VMEM((SUBLANES, qb), I32),
            pltpu.VMEM((kc, qb), F32),
        ] + [pltpu.VMEM((kc, qb), F32)] * N_HEADS
          + [pltpu.VMEM((kc, qb), BF16)] * N_HEADS
          + [pltpu.VMEM((1, qb), F32)] * N_HEADS
          + [pltpu.VMEM((1, qb), F32)] * N_HEADS
          + [pltpu.VMEM((HEAD_DIM, qb), F32)] * N_HEADS,
        compiler_params=_cparams(("arbitrary", "arbitrary")),
        name="dsa_attend",
    )(qT, qiT, wT, kidxb, kb, vT, knorm, tabs)


def _mix_kernel(x_ref, attn_ref, conv_ref, wo_ref, g_ref, b_ref, h_ref, *, dn_alpha):
    cat = jnp.concatenate([attn_ref[...], conv_ref[...]], axis=1)
    mixed = _dot(cat, wo_ref[...])
    h_ref[...] = _layer_norm(dn_alpha * x_ref[...] + mixed, g_ref[...], b_ref[...])


def _mix(x2, attn2, conv2, wo, g, b, *, dn_alpha):
    T, D = x2.shape
    tt = min(1024, T)
    assert T % tt == 0
    aw, cw = attn2.shape[1], conv2.shape[1]
    row = lambda i: (i, 0)
    const = lambda i: (0, 0)
    return pl.pallas_call(
        functools.partial(_mix_kernel, dn_alpha=dn_alpha),
        grid=(T // tt,),
        in_specs=[pl.BlockSpec((tt, D), row), pl.BlockSpec((tt, aw), row), pl.BlockSpec((tt, cw), row),
                  pl.BlockSpec(wo.shape, const), pl.BlockSpec((1, D), const), pl.BlockSpec((1, D), const)],
        out_specs=pl.BlockSpec((tt, D), row),
        out_shape=jax.ShapeDtypeStruct((T, D), F32),
        compiler_params=_cparams(("arbitrary",)),
        name="outproj_ln1",
    )(x2, attn2, conv2, wo, g, b)


def _route_picks(logits):
    lane = lax.broadcasted_iota(I32, logits.shape, 1)
    work = logits
    vals, hots = [], []
    for _ in range(TOP_K):
        m = jnp.max(work, axis=1, keepdims=True)
        idx = jnp.min(jnp.where(work == m, lane, LANES), axis=1, keepdims=True)
        hot = lane == idx
        vals.append(m)
        hots.append(hot)
        work = jnp.where(hot, -jnp.inf, work)
    es = [jnp.exp(v - vals[0]) for v in vals]
    den = es[0]
    for e in es[1:]:
        den = den + e
    return hots, [e / den for e in es]


def _route(logits):
    hots, gates = _route_picks(logits)
    comb = jnp.zeros(logits.shape, F32)
    for gate, hot in zip(gates, hots):
        comb = jnp.where(hot, gate, comb)
    return comb


def _expert_mlp(xb, wgu_ref, bgu_ref, wd_ref, bd_ref):
    ff = wd_ref.shape[1]
    gu = _dot(xb, wgu_ref[0]) + bgu_ref[0]
    gate = jnp.minimum(gu[:, 0:ff], SWIGLU_LIMIT)
    up = jnp.clip(gu[:, ff:2 * ff], -SWIGLU_LIMIT, SWIGLU_LIMIT)
    act = (up + 1.0) * (gate * (1.0 / (1.0 + jnp.exp(-SWIGLU_ALPHA * gate))))
    return _dot(act.astype(BF16), wd_ref[0]) + bd_ref[0]


def _moe_kernel(h_ref, wr_ref, br_ref, wgu_ref, bgu_ref, wd_ref, bd_ref, g_ref, b_ref,
                y_ref, hb_ref, comb_ref, *, dn_alpha):
    e = pl.program_id(1)
    tt = h_ref.shape[0]

    @pl.when(e == 0)
    def _():
        hb = h_ref[...].astype(BF16)
        hb_ref[...] = hb
        comb_ref[...] = _route(_dot(hb, wr_ref[...]) + br_ref[...])
        y_ref[...] = jnp.zeros(y_ref.shape, F32)

    y = _expert_mlp(hb_ref[...], wgu_ref, bgu_ref, wd_ref, bd_ref)
    lane = lax.broadcasted_iota(I32, (tt, LANES), 1)
    y_ref[...] += jnp.sum(jnp.where(lane == e, comb_ref[...], 0.0), axis=1, keepdims=True) * y

    @pl.when(e == pl.num_programs(1) - 1)
    def _():
        y_ref[...] = _layer_norm(dn_alpha * h_ref[...] + y_ref[...], g_ref[...], b_ref[...])


def _moe(h2, wr, br, wgu, bgu, wd, bd, g, b, *, dn_alpha):
    T, D = h2.shape
    E, _, F2 = wgu.shape
    tt = min(512, T)
    assert T % tt == 0
    row = lambda i, e: (i, 0)
    const = lambda i, e: (0, 0)
    exp3 = lambda i, e: (e, 0, 0)
    return pl.pallas_call(
        functools.partial(_moe_kernel, dn_alpha=dn_alpha),
        grid=(T // tt, E),
        in_specs=[pl.BlockSpec((tt, D), row), pl.BlockSpec(wr.shape, const), pl.BlockSpec(br.shape, const),
                  pl.BlockSpec((1, D, F2), exp3), pl.BlockSpec((1, 1, F2), exp3),
                  pl.BlockSpec((1, F2 // 2, D), exp3), pl.BlockSpec((1, 1, D), exp3),
                  pl.BlockSpec((1, D), const), pl.BlockSpec((1, D), const)],
        out_specs=pl.BlockSpec((tt, D), row),
        out_shape=jax.ShapeDtypeStruct((T, D), F32),
        scratch_shapes=[pltpu.VMEM((tt, D), BF16), pltpu.VMEM((tt, LANES), F32)],
        compiler_params=_cparams(("arbitrary", "arbitrary")),
        name="moe_ln2",
    )(h2, wr, br, wgu, bgu, wd, bd, g, b)


def _pack_pairs(x):
    w = x.shape[1] // 2
    bits = pltpu.bitcast(x.astype(BF16).astype(F32), I32)
    return (bits[:, 0:w] & jnp.int32(-65536)) | lax.shift_right_logical(bits[:, w:2 * w], 16)


def _unpack_pairs(p):
    hi = pltpu.bitcast(p & jnp.int32(-65536), F32)
    lo = pltpu.bitcast(lax.shift_left(p, 16), F32)
    return jnp.concatenate([hi, lo], axis=1)


def _mix_route_kernel(x_ref, attn_ref, conv_ref, wo_ref, g_ref, b_ref, wrT_ref, brT_ref,
                      h_ref, hpk_ref, infoT_ref, gcol_ref, cnt_ref, *, dn_alpha):
    cat = jnp.concatenate([attn_ref[...], conv_ref[...]], axis=1)
    h = _layer_norm(dn_alpha * x_ref[...] + _dot(cat, wo_ref[...]), g_ref[...], b_ref[...])
    h_ref[...] = h
    tt = h.shape[0]
    ne = cnt_ref.shape[1]
    hpk_ref[...] = _pack_pairs(h)
    logits = lax.dot_general(wrT_ref[...], h.astype(BF16), (((1,), (1,)), ((), ())),
                             preferred_element_type=F32)
    work = logits[0:ne] + jnp.tile(brT_ref[0:ne, :], (1, tt // LANES))
    sub = lax.broadcasted_iota(I32, (ne, tt), 0)
    vals, idxs, hots = [], [], []
    for _ in range(TOP_K):
        m = jnp.max(work, axis=0, keepdims=True)
        idx = jnp.min(jnp.where(work == m, sub, ne), axis=0, keepdims=True)
        hot = sub == idx
        vals.append(m)
        idxs.append(idx)
        hots.append(hot)
        work = jnp.where(hot, -jnp.inf, work)
    es = [jnp.exp(v - vals[0]) for v in vals]
    den = es[0]
    for e in es[1:]:
        den = den + e
    gates = [e / den for e in es]
    picked = jnp.zeros((ne, tt), F32)
    for hot in hots:
        picked = jnp.where(hot, 1.0, picked)
    earlier = lax.broadcasted_iota(I32, (tt, tt), 0) < lax.broadcasted_iota(I32, (tt, tt), 1)
    rank = _dot(picked.astype(BF16), jnp.where(earlier, 1.0, 0.0).astype(BF16))
    ranks = [jnp.sum(jnp.where(hot, rank, 0.0), axis=0, keepdims=True) for hot in hots]
    pad = jnp.zeros((2 * SUBLANES - 3 * TOP_K, tt), F32)
    infoT_ref[...] = jnp.concatenate([i.astype(F32) for i in idxs] + gates + ranks + [pad], axis=0)
    gcol_ref[...] = jnp.concatenate(gates + [jnp.zeros((LANES - TOP_K, tt), F32)], axis=0).T
    cnt_ref[0] = jnp.broadcast_to(jnp.sum(picked, axis=1, keepdims=True), (ne, LANES))


def _mix_route(x2, attn2, conv2, wo, g, b, wr, br, ne, *, dn_alpha):
    T, D = x2.shape
    tt = ROUTE_TILE
    aw, cw = attn2.shape[1], conv2.shape[1]
    row = lambda i: (i, 0)
    const = lambda i: (0, 0)
    wrT = wr.T
    brT = jnp.broadcast_to(br.reshape(LANES, 1), (LANES, LANES))
    return pl.pallas_call(
        functools.partial(_mix_route_kernel, dn_alpha=dn_alpha),
        grid=(T // tt,),
        in_specs=[pl.BlockSpec((tt, D), row), pl.BlockSpec((tt, aw), row), pl.BlockSpec((tt, cw), row),
                  pl.BlockSpec(wo.shape, const), pl.BlockSpec((1, D), const), pl.BlockSpec((1, D), const),
                  pl.BlockSpec(wrT.shape, const), pl.BlockSpec(brT.shape, const)],
        out_specs=(pl.BlockSpec((tt, D), row), pl.BlockSpec((tt, D // 2), row),
                   pl.BlockSpec((2 * SUBLANES, tt), lambda i: (0, i)),
                   pl.BlockSpec((tt, LANES), row), pl.BlockSpec((1, ne, LANES), lambda i: (i, 0, 0))),
        out_shape=(jax.ShapeDtypeStruct((T, D), F32), jax.ShapeDtypeStruct((T, D // 2), I32),
                   jax.ShapeDtypeStruct((2 * SUBLANES, T), F32),
                   jax.ShapeDtypeStruct((T, LANES), F32), jax.ShapeDtypeStruct((T // tt, ne, LANES), F32)),
        compiler_params=_cparams(("arbitrary",)),
        name="outproj_ln1_route",
    )(x2, attn2, conv2, wo, g, b, wrT, brT)


def _sc_gather_rows(table, idx):
    M, W = idx.shape[0], table.shape[1]
    n_workers = SC_CORES * SC_SUBCORES
    per_w = M // n_workers
    n_ch = per_w // SC_ROWS
    assert M == n_workers * n_ch * SC_ROWS
    mesh = plsc.VectorSubcoreMesh(core_axis_name="c", subcore_axis_name="s")

    @functools.partial(
        pl.kernel, mesh=mesh,
        out_type=jax.ShapeDtypeStruct((M, W), table.dtype),
        scratch_types=[pltpu.VMEM((n_ch, SC_ROWS), I32), pltpu.VMEM((SC_ROWS, W), table.dtype),
                       pltpu.SemaphoreType.DMA],
    )
    def gather(table_hbm, idx_hbm, out_hbm, idx_v, rows_v, sem):
        wid = lax.axis_index("s") * SC_CORES + lax.axis_index("c")
        pltpu.sync_copy(idx_hbm.at[wid], idx_v)

        @pl.loop(0, n_ch)
        def _(j):
            pltpu.async_copy(table_hbm.at[idx_v.at[j]], rows_v, sem).wait()
            pltpu.sync_copy(rows_v, out_hbm.at[pl.ds(wid * per_w + j * SC_ROWS, SC_ROWS)])

    return gather(table, idx.reshape(n_workers, n_ch, SC_ROWS))


def _expert_rows_kernel(te_ref, used_ref, x_ref, wgu_ref, bgu_ref, wd_ref, bd_ref, y_ref):
    @pl.when(pl.program_id(0) < used_ref[0])
    def _():
        xg = _unpack_pairs(x_ref[...]).astype(BF16)
        y_ref[...] = _pack_pairs(_expert_mlp(xg, wgu_ref, bgu_ref, wd_ref, bd_ref))


def _expert_rows(tile_expert, used_tiles, xs, wgu, bgu, wd, bd):
    R, W = xs.shape
    E, D, F2 = wgu.shape
    exp3 = lambda j, te, used: (te[j], 0, 0)
    rows = lambda j, te, used: (j, 0)
    grid_spec = pltpu.PrefetchScalarGridSpec(
        num_scalar_prefetch=2,
        grid=(R // ROW_TILE,),
        in_specs=[pl.BlockSpec((ROW_TILE, W), rows),
                  pl.BlockSpec((1, D, F2), exp3), pl.BlockSpec((1, 1, F2), exp3),
                  pl.BlockSpec((1, F2 // 2, D), exp3), pl.BlockSpec((1, 1, D), exp3)],
        out_specs=pl.BlockSpec((ROW_TILE, W), rows),
    )
    return pl.pallas_call(
        _expert_rows_kernel,
        grid_spec=grid_spec,
        out_shape=jax.ShapeDtypeStruct((R, W), I32),
        compiler_params=_cparams(("arbitrary",)),
        name="moe_expert_rows",
    )(tile_expert, used_tiles, xs, wgu, bgu, wd, bd)


def _combine_kernel(h_ref, yg_ref, gcol_ref, g_ref, b_ref, o_ref, *, dn_alpha):
    acc = dn_alpha * h_ref[...]
    for k in range(TOP_K):
        acc = acc + gcol_ref[:, k:k + 1] * _unpack_pairs(yg_ref[k])
    o_ref[...] = _layer_norm(acc, g_ref[...], b_ref[...])


def _combine(h2, yg, gcol, g, b, *, dn_alpha):
    T, D = h2.shape
    tt = 512
    row = lambda i: (i, 0)
    const = lambda i: (0, 0)
    return pl.pallas_call(
        functools.partial(_combine_kernel, dn_alpha=dn_alpha),
        grid=(T // tt,),
        in_specs=[pl.BlockSpec((tt, D), row), pl.BlockSpec((TOP_K, tt, D // 2), lambda i: (0, i, 0)),
                  pl.BlockSpec((tt, LANES), row),
                  pl.BlockSpec((1, D), const), pl.BlockSpec((1, D), const)],
        out_specs=pl.BlockSpec((tt, D), row),
        out_shape=jax.ShapeDtypeStruct((T, D), F32),
        compiler_params=_cparams(("arbitrary",)),
        name="moe_combine_ln2",
    )(h2, yg, gcol, g, b)


def _dest_kernel(infoT_ref, base_ref, dest_ref):
    tt = infoT_ref.shape[1]
    ne = base_ref.shape[1]
    base = jnp.tile(base_ref[0], (1, tt // LANES))
    sub = lax.broadcasted_iota(I32, (ne, tt), 0)
    rows = []
    for k in range(TOP_K):
        hot = sub == infoT_ref[k:k + 1, :].astype(I32)
        rows.append(jnp.sum(jnp.where(hot, base, 0.0), axis=0, keepdims=True)
                    + infoT_ref[2 * TOP_K + k:2 * TOP_K + k + 1, :])
    rows.append(jnp.zeros((SUBLANES - TOP_K, tt), F32))
    dest_ref[...] = jnp.concatenate(rows, axis=0).astype(I32)


def _dest_rows(infoT, base):
    T = infoT.shape[1]
    tt = ROUTE_TILE
    ne = base.shape[1]
    return pl.pallas_call(
        _dest_kernel,
        grid=(T // tt,),
        in_specs=[pl.BlockSpec((2 * SUBLANES, tt), lambda i: (0, i)), pl.BlockSpec((1, ne, LANES), lambda i: (i, 0, 0))],
        out_specs=pl.BlockSpec((SUBLANES, tt), lambda i: (0, i)),
        out_shape=jax.ShapeDtypeStruct((SUBLANES, T), I32),
        compiler_params=_cparams(("arbitrary",)),
        name="moe_dest_rows",
    )(infoT, base)


def _sc_scatter_rows(src, dest_km, n_rows):
    T, W = src.shape
    K = dest_km.shape[0]
    n_workers = SC_CORES * SC_SUBCORES
    per_w = T // n_workers
    n_ch = per_w // SC_ROWS
    assert T == n_workers * n_ch * SC_ROWS
    mesh = plsc.VectorSubcoreMesh(core_axis_name="c", subcore_axis_name="s")
    idx = dest_km.reshape(K, n_workers, n_ch, SC_ROWS).transpose(1, 0, 2, 3).reshape(n_workers, K * n_ch, SC_ROWS)

    @functools.partial(
        pl.kernel, mesh=mesh,
        out_type=jax.ShapeDtypeStruct((n_rows, W), src.dtype),
        scratch_types=[pltpu.VMEM((K * n_ch, SC_ROWS), I32), pltpu.VMEM((SC_ROWS, W), src.dtype),
                       pltpu.SemaphoreType.DMA],
    )
    def scatter(src_hbm, idx_hbm, out_hbm, idx_v, rows_v, sem):
        wid = lax.axis_index("s") * SC_CORES + lax.axis_index("c")
        pltpu.sync_copy(idx_hbm.at[wid], idx_v)

        @pl.loop(0, n_ch)
        def _(j):
            pltpu.sync_copy(src_hbm.at[pl.ds(wid * per_w + j * SC_ROWS, SC_ROWS)], rows_v)
            for k in range(K):
                pltpu.async_copy(rows_v, out_hbm.at[idx_v.at[k * n_ch + j]], sem).wait()

    return scatter(src, idx)


def _moe_sorted(h2, routed, wgu, bgu, wd, bd, g, b, *, dn_alpha):
    T, D = h2.shape
    E = wgu.shape[0]
    hpk, infoT, gcol, cnt = routed
    cnt = cnt[:, :, 0].astype(I32)
    total = jnp.sum(cnt, axis=0)
    padded = -(-total // ROW_TILE) * ROW_TILE
    ends = jnp.cumsum(padded)
    base = (ends - padded)[None, :] + jnp.cumsum(cnt, axis=0) - cnt
    base = jnp.broadcast_to(base.astype(F32)[:, :, None], cnt.shape + (LANES,))
    dest_km = _dest_rows(infoT, base)[0:TOP_K]
    R = T * TOP_K + E * ROW_TILE
    tile_start = jnp.arange(R // ROW_TILE, dtype=I32) * ROW_TILE
    tile_expert = jnp.minimum(jnp.sum((ends[None, :] <= tile_start[:, None]).astype(I32), axis=1), E - 1)
    used_tiles = (ends[E - 1] // ROW_TILE).reshape(1)
    xs = _sc_scatter_rows(hpk, dest_km, R)
    ys = _expert_rows(tile_expert, used_tiles, xs, wgu, bgu, wd, bd)
    yg = _sc_gather_rows(ys, dest_km.reshape(-1))
    return _combine(h2, yg.reshape(TOP_K, T, D // 2), gcol, g, b, dn_alpha=dn_alpha)


def _split_gu_kernel(w_ref, perm_ref, o_ref, *, band):
    n = w_ref.shape[2]
    for m in range(n // band):
        both = _dot(w_ref[0, :, m * band:(m + 1) * band].astype(BF16), perm_ref[...]).astype(BF16)
        o_ref[0, :, m * (band // 2):(m + 1) * (band // 2)] = both[:, 0:band // 2]
        o_ref[0, :, n // 2 + m * (band // 2):n // 2 + (m + 1) * (band // 2)] = both[:, band // 2:band]


def _split_gu(w_gu):
    E, D, N = w_gu.shape
    tr = min(1024, D)
    band = min(512, N)
    j = jnp.arange(band, dtype=I32)
    src = jnp.where(j < band // 2, 2 * j, 2 * (j - band // 2) + 1)
    perm = (jnp.arange(band, dtype=I32)[:, None] == src[None, :]).astype(BF16)
    return pl.pallas_call(
        functools.partial(_split_gu_kernel, band=band),
        grid=(E, D // tr),
        in_specs=[pl.BlockSpec((1, tr, N), lambda e, r: (e, r, 0)),
                  pl.BlockSpec((band, band), lambda e, r: (0, 0))],
        out_specs=pl.BlockSpec((1, tr, N), lambda e, r: (e, r, 0)),
        out_shape=jax.ShapeDtypeStruct((E, D, N), BF16),
        compiler_params=_cparams(("arbitrary", "arbitrary")),
        name="split_gate_up",
    )(w_gu, perm)


def _prep_weights(w_in, w_conv, w_out, ln1_g, ln1_b, w_router, b_router, w_gu, b_gu, w_down, b_down,
                  ln2_g, ln2_b):
    aw = N_HEADS * HEAD_DIM
    kv = N_KV_HEADS * HEAD_DIM
    iw = N_IDX_HEADS * IDX_DIM
    D = w_in.shape[0]
    cw = D - aw
    o_q, o_k, o_v = 0, aw, aw + kv
    o_qi = o_v + kv
    o_ki = o_qi + iw
    o_wi = o_ki + IDX_DIM
    o_c = o_wi + N_IDX_HEADS
    wa = jnp.concatenate([w_in[:, o_q:o_k], w_in[:, o_qi:o_ki], w_in[:, o_k:o_v], w_in[:, o_v:o_qi]],
                         axis=1).astype(BF16)
    ws = jnp.pad(w_in[:, o_ki:o_c], ((0, 0), (0, LANES - IDX_DIM - N_IDX_HEADS))).astype(BF16)
    wc = w_in[:, o_c:o_c + 3 * cw].astype(BF16)
    E = w_router.shape[1]
    wr = jnp.pad(w_router, ((0, 0), (0, LANES - E))).astype(BF16)
    br = jnp.pad(b_router, (0, LANES - E), constant_values=MASK_VALUE).reshape(1, LANES)
    F = w_gu.shape[2] // 2
    return dict(
        wa=wa, ws=ws, wc=wc, wconv=w_conv, wo=w_out.astype(BF16),
        ln1_g=ln1_g.reshape(1, D), ln1_b=ln1_b.reshape(1, D),
        wr=wr, br=br,
        wgu=_split_gu(w_gu),
        bgu=jnp.concatenate([b_gu[:, 0::2], b_gu[:, 1::2]], axis=1).reshape(E, 1, 2 * F),
        wd=w_down.astype(BF16), bd=b_down.reshape(E, 1, D),
        ln2_g=ln2_g.reshape(1, D), ln2_b=ln2_b.reshape(1, D),
    )


def _decode_layouts(pq, small, k_all, v_all, kidx_all, kc, qb):
    B, S, _ = pq.shape
    aw = N_HEADS * HEAD_DIM
    L, kv = k_all.shape[1], k_all.shape[2]
    lp = -(-L // kc) * kc
    nc = lp // kc
    padk = lambda a: jnp.pad(a, ((0, 0), (0, lp - L), (0, 0)))
    padq = lambda a: jnp.pad(a, ((0, 0), (0, 0), (0, 0), (0, qb - S)))
    q = pq[:, :, 0:aw] * QK_SCALE
    qT = padq(q.reshape(B, S, N_HEADS, HEAD_DIM).transpose(0, 2, 3, 1)).astype(BF16)
    qiT = padq(pq[:, :, aw:].reshape(B, S, N_IDX_HEADS, IDX_DIM).transpose(0, 2, 3, 1)).astype(BF16)
    wT = jnp.pad(small[:, :, IDX_DIM:IDX_DIM + N_IDX_HEADS].transpose(0, 2, 1), ((0, 0), (0, 0), (0, qb - S)))
    kb = padk(k_all).reshape(B, nc, kc, N_KV_HEADS, HEAD_DIM).transpose(0, 1, 3, 2, 4).astype(BF16)
    vT = padk(v_all).reshape(B, nc, kc, kv).transpose(0, 1, 3, 2).astype(BF16)
    kidxb = padk(kidx_all).reshape(B, nc, kc, IDX_DIM).astype(BF16)
    knorm = jnp.sum(kb.astype(F32) ** 2, axis=-1)
    return qT, qiT, wT, kidxb, kb, vT, knorm


def _layer(x, past_k, past_v, past_kidx, conv_buf, rel_bias, w, *, dn_alpha):
    B, S, D = x.shape
    prefill = past_k is None
    outs = _project(x, conv_buf, w["wa"], w["ws"], w["wc"], w["wconv"], attn_layouts=prefill)
    k, v, small, conv, new_buf = outs[:5]
    k_idx = small if prefill else small[:, :, 0:IDX_DIM]
    if prefill:
        past, l_true = 0, S
        qb = min(QUERY_BLOCK, S)
        kb, vT, kidxb, qT, qiT, wT, knorm = outs[5:]
    else:
        past = past_k.shape[1]
        l_true = past + S
        qb = -(-S // LANES) * LANES
        qT, qiT, wT, kidxb, kb, vT, knorm = _decode_layouts(
            outs[5], small, jnp.concatenate([past_k, k], axis=1), jnp.concatenate([past_v, v], axis=1),
            jnp.concatenate([past_kidx, k_idx], axis=1), math.gcd(past, KEY_CHUNK), qb)
    offsets = (-LANES, 0)
    tabs = _bias_tables(rel_bias, offsets)
    attn = _attend(qT, qiT, wT, kidxb, kb, vT, knorm, tabs, qb=qb, past=past, l_true=l_true, offsets=offsets)
    attn = attn[:, :S]
    mix_args = (x.reshape(B * S, D), attn.reshape(B * S, -1), conv.reshape(B * S, -1), w["wo"],
                w["ln1_g"], w["ln1_b"])
    experts = (w["wgu"], w["bgu"], w["wd"], w["bd"], w["ln2_g"], w["ln2_b"])
    sc_unit = SC_CORES * SC_SUBCORES * SC_ROWS
    sortable = ((B * S) % ROUTE_TILE == 0 and (B * S) % sc_unit == 0
                and (N_EXPERTS * ROW_TILE) % sc_unit == 0)
    if sortable and B * S >= SORTED_MIN_TOKENS:
        h, *routed = _mix_route(*mix_args, w["wr"], w["br"], N_EXPERTS, dn_alpha=dn_alpha)
        y = _moe_sorted(h, routed, *experts, dn_alpha=dn_alpha)
    else:
        h = _mix(*mix_args, dn_alpha=dn_alpha)
        y = _moe(h, w["wr"], w["br"], *experts, dn_alpha=dn_alpha)
    return (y.reshape(B, S, D), k.reshape(B, S, N_KV_HEADS, HEAD_DIM), v.reshape(B, S, N_KV_HEADS, HEAD_DIM),
            k_idx, new_buf)


def kernel(x_prompt, x_sample, cache_k, cache_v, cache_kidx, state_conv, rel_bias, w_in, w_conv, w_out,
           ln1_g, ln1_b, w_router, b_router, w_gu, b_gu, w_down, b_down, ln2_g, ln2_b):
    depth = w_in.shape[0]
    assert depth == 1
    dn_alpha = (2 * depth) ** 0.25
    kv = N_KV_HEADS * HEAD_DIM
    w = _prep_weights(w_in[0], w_conv[0], w_out[0], ln1_g[0], ln1_b[0], w_router[0], b_router[0],
                      w_gu[0], b_gu[0], w_down[0], b_down[0], ln2_g[0], ln2_b[0])
    Bp = x_prompt.shape[0]
    cw = w_conv.shape[2]
    zero_buf = jnp.zeros((Bp, CONV_K - 1, cw), F32)
    yp, k1, v1, i1, c1 = _layer(x_prompt, None, None, None, zero_buf, rel_bias, w, dn_alpha=dn_alpha)
    Bs, P = cache_k.shape[1], cache_k.shape[2]
    ys, k2, v2, i2, c2 = _layer(x_sample, cache_k[0].reshape(Bs, P, kv), cache_v[0].reshape(Bs, P, kv),
                                cache_kidx[0], state_conv[0], rel_bias, w, dn_alpha=dn_alpha)
    return (yp, ys, k1[None], v1[None], i1[None], c1[None], k2[None], v2[None], i2[None], c2[None])
```

```python
import functools
import math

import jax
import jax.numpy as jnp
from jax import lax
from jax.experimental import pallas as pl
from jax.experimental.pallas import tpu as pltpu
from jax.experimental.pallas import tpu_sc as plsc

F32 = jnp.float32
BF16 = jnp.bfloat16
I32 = jnp.int32
I16 = jnp.int16

CHUNK = 64
N_HEADS = 8
HEAD_DIM = 64
N_KV_HEADS = 2
GROUP = N_HEADS // N_KV_HEADS
N_IDX_HEADS = 8
IDX_DIM = 32
TOPK_MAX = 256
CONV_K = 3
N_BUCKETS = 32
MAX_DISTANCE = 128
N_EXPERTS = 32
TOP_K = 4
SWIGLU_LIMIT = 7.0
SWIGLU_ALPHA = 1.702
LN_EPS = 1e-5
MASK_VALUE = -1e30
QK_SCALE = HEAD_DIM ** -0.5 * math.log2(math.e)
EXP2_SAFE = 96.0

LANES = 128
SUBLANES = 8
KEY_CHUNK = 512
QUERY_BLOCK = 512
SLAB = 64
ROW_TILE = 1024
ROUTE_TILE = 1024
SC_CORES = 2
SC_SUBCORES = 16
SC_ROWS = 128
SORTED_MIN_TOKENS = 2048
INT_MIN = -(2 ** 31)
VMEM_LIMIT = 56 * 1024 * 1024


def _cparams(sem):
    return pltpu.CompilerParams(dimension_semantics=sem, vmem_limit_bytes=VMEM_LIMIT)


def _dot(a, b):
    return jnp.dot(a, b, preferred_element_type=F32)


def _layer_norm(z, g, b):
    mu = jnp.mean(z, axis=-1, keepdims=True)
    zc = z - mu
    var = jnp.mean(zc * zc, axis=-1, keepdims=True)
    return zc * lax.rsqrt(var + LN_EPS) * g + b


def _bucket_thresholds():
    nb = N_BUCKETS // 2
    max_exact = nb // 2
    out = []
    for j in range(1, nb - max_exact):
        out.append(math.ceil(max_exact * (MAX_DISTANCE / max_exact) ** (j / (nb - max_exact)) - 1e-9))
    return tuple(out)


def _bias_table_kernel(rel_ref, tab_ref, *, offsets):
    nb = N_BUCKETS // 2
    max_exact = nb // 2
    thr = _bucket_thresholds()
    ii = lax.broadcasted_iota(I32, (LANES, LANES), 0)
    jj = lax.broadcasted_iota(I32, (LANES, LANES), 1)
    for d, off in enumerate(offsets):
        rel = off + ii - jj
        n = jnp.abs(rel)
        large = jnp.full((LANES, LANES), max_exact, I32)
        for t in thr:
            large = large + jnp.where(n >= t, 1, 0)
        bucket = jnp.where(rel > 0, nb, 0) + jnp.where(n < max_exact, n, large)
        for h in range(N_HEADS):
            acc = jnp.zeros((LANES, LANES), F32)
            for b in range(N_BUCKETS):
                acc = jnp.where(bucket == b, rel_ref[b, h], acc)
            tab_ref[d, h] = (acc - rel_ref[nb - 1, h]) * math.log2(math.e)


def _bias_tables(rel_bias, offsets):
    return pl.pallas_call(
        functools.partial(_bias_table_kernel, offsets=offsets),
        out_shape=jax.ShapeDtypeStruct((len(offsets), N_HEADS, LANES, LANES), F32),
        in_specs=[pl.BlockSpec(memory_space=pltpu.SMEM)],
        out_specs=pl.BlockSpec(memory_space=pltpu.VMEM),
        name="bias_tables",
    )(rel_bias)


def _proj_kernel(x_ref, wa_ref, ws_ref, wc_ref, wconv_ref, buf_ref,
                 k_ref, v_ref, small_ref, conv_ref, nbuf_ref, *rest, ts, attn_layouts):
    s = pl.program_id(1)
    xb = x_ref[0].astype(BF16)
    pa = _dot(xb, wa_ref[...])
    ps = _dot(xb, ws_ref[...])
    pc = _dot(xb, wc_ref[...])
    aw = N_HEADS * HEAD_DIM
    iw = N_IDX_HEADS * IDX_DIM
    kv = N_KV_HEADS * HEAD_DIM
    k = pa[:, aw + iw:aw + iw + kv]
    v = pa[:, aw + iw + kv:aw + iw + 2 * kv]
    k_ref[0] = k
    v_ref[0] = v
    small_ref[0] = ps[:, 0:small_ref.shape[2]]
    scale = QK_SCALE
    if attn_layouts:
        kb_ref, vT_ref, kidxb_ref, qT_ref, qiT_ref, wT_ref, knorm_ref, carry_ref = rest
        for n in range(N_KV_HEADS):
            kb_ref[0, 0, n] = k[:, n * HEAD_DIM:(n + 1) * HEAD_DIM].astype(BF16)
        kTf = k.astype(BF16).astype(F32).T
        knorm_ref[0, 0] = jnp.concatenate(
            [jnp.sum(kTf[n * HEAD_DIM:(n + 1) * HEAD_DIM] ** 2, axis=0, keepdims=True) for n in range(N_KV_HEADS)],
            axis=0)
        vT_ref[0, 0] = v.T.astype(BF16)
        kidxb_ref[0, 0] = ps[:, 0:IDX_DIM].astype(BF16)
        qT = (pa[:, 0:aw] * scale).T.astype(BF16)
        for h in range(N_HEADS):
            qT_ref[0, h] = qT[h * HEAD_DIM:(h + 1) * HEAD_DIM]
        qiT = pa[:, aw:aw + iw].T.astype(BF16)
        for h in range(N_IDX_HEADS):
            qiT_ref[0, h] = qiT[h * IDX_DIM:(h + 1) * IDX_DIM]
        wT_ref[0] = ps.T[IDX_DIM:IDX_DIM + N_IDX_HEADS]
    else:
        pq_ref, carry_ref = rest
        pq_ref[0] = pa[:, 0:aw + iw]

    cw = pc.shape[1] // 3
    u = pc[:, cw:2 * cw] * pc[:, 2 * cw:3 * cw]

    @pl.when(s == 0)
    def _():
        carry_ref[6:8, :] = buf_ref[0]

    carry_ref[8:8 + ts, :] = u
    y = (carry_ref[6:6 + ts, :] * wconv_ref[0:1, :]
         + carry_ref[7:7 + ts, :] * wconv_ref[1:2, :]
         + u * wconv_ref[2:3, :])
    conv_ref[0] = (pc[:, 0:cw] * y).astype(BF16)
    nb = carry_ref[ts + 6:ts + 8, :]
    nbuf_ref[0] = nb
    carry_ref[6:8, :] = nb


def _project(x, conv_buf, wa, ws, wc, wconv, *, attn_layouts):
    B, S, D = x.shape
    ts = min(KEY_CHUNK, S)
    assert S % ts == 0 and S >= CONV_K - 1
    ns = S // ts
    cw = wc.shape[1] // 3
    sw = IDX_DIM if attn_layouts else LANES
    kv = N_KV_HEADS * HEAD_DIM
    aw = N_HEADS * HEAD_DIM
    iw = N_IDX_HEADS * IDX_DIM
    out_shape = [
        jax.ShapeDtypeStruct((B, S, kv), F32),
        jax.ShapeDtypeStruct((B, S, kv), F32),
        jax.ShapeDtypeStruct((B, S, sw), F32),
        jax.ShapeDtypeStruct((B, S, cw), BF16),
        jax.ShapeDtypeStruct((B, CONV_K - 1, cw), F32),
    ]
    out_specs = [
        pl.BlockSpec((1, ts, kv), lambda b, s: (b, s, 0)),
        pl.BlockSpec((1, ts, kv), lambda b, s: (b, s, 0)),
        pl.BlockSpec((1, ts, sw), lambda b, s: (b, s, 0)),
        pl.BlockSpec((1, ts, cw), lambda b, s: (b, s, 0)),
        pl.BlockSpec((1, CONV_K - 1, cw), lambda b, s: (b, 0, 0)),
    ]
    if attn_layouts:
        out_shape += [
            jax.ShapeDtypeStruct((B, ns, N_KV_HEADS, ts, HEAD_DIM), BF16),
            jax.ShapeDtypeStruct((B, ns, kv, ts), BF16),
            jax.ShapeDtypeStruct((B, ns, ts, IDX_DIM), BF16),
            jax.ShapeDtypeStruct((B, N_HEADS, HEAD_DIM, S), BF16),
            jax.ShapeDtypeStruct((B, N_IDX_HEADS, IDX_DIM, S), BF16),
            jax.ShapeDtypeStruct((B, N_IDX_HEADS, S), F32),
            jax.ShapeDtypeStruct((B, ns, N_KV_HEADS, ts), F32),
        ]
        out_specs += [
            pl.BlockSpec((1, 1, N_KV_HEADS, ts, HEAD_DIM), lambda b, s: (b, s, 0, 0, 0)),
            pl.BlockSpec((1, 1, kv, ts), lambda b, s: (b, s, 0, 0)),
            pl.BlockSpec((1, 1, ts, IDX_DIM), lambda b, s: (b, s, 0, 0)),
            pl.BlockSpec((1, N_HEADS, HEAD_DIM, ts), lambda b, s: (b, 0, 0, s)),
            pl.BlockSpec((1, N_IDX_HEADS, IDX_DIM, ts), lambda b, s: (b, 0, 0, s)),
            pl.BlockSpec((1, N_IDX_HEADS, ts), lambda b, s: (b, 0, s)),
            pl.BlockSpec((1, 1, N_KV_HEADS, ts), lambda b, s: (b, s, 0, 0)),
        ]
    else:
        out_shape += [jax.ShapeDtypeStruct((B, S, aw + iw), F32)]
        out_specs += [pl.BlockSpec((1, ts, aw + iw), lambda b, s: (b, s, 0))]
    const2 = lambda b, s: (0, 0)
    in_specs = [
        pl.BlockSpec((1, ts, D), lambda b, s: (b, s, 0)),
        pl.BlockSpec(wa.shape, const2),
        pl.BlockSpec(ws.shape, const2),
        pl.BlockSpec(wc.shape, const2),
        pl.BlockSpec(wconv.shape, const2),
        pl.BlockSpec((1, CONV_K - 1, cw), lambda b, s: (b, 0, 0)),
    ]
    return pl.pallas_call(
        functools.partial(_proj_kernel, ts=ts, attn_layouts=attn_layouts),
        grid=(B, ns),
        in_specs=in_specs,
        out_specs=tuple(out_specs),
        out_shape=tuple(out_shape),
        scratch_shapes=[pltpu.VMEM((ts + 8, cw), F32)],
        compiler_params=_cparams(("arbitrary", "arbitrary")),
        name="proj_conv",
    )(x, wa, ws, wc, wconv, conv_buf)


def _attn_kernel(qT_ref, qiT_ref, wT_ref, kidx_ref, kb_ref, vT_ref, knorm_ref, tab_ref,
                 o_ref, skey_ref, hi16_ref, lo16_ref, jb_ref, madd_ref, *head_refs,
                 qb, kc, past, l_true, topk, offsets):
    s_refs, p_refs, m_refs, l_refs, acc_refs = (head_refs[g * N_HEADS:(g + 1) * N_HEADS] for g in range(5))
    i = pl.program_id(1)
    qoff = past + i * qb
    adm_end = jnp.minimum(((qoff + qb - 1) // CHUNK + 1) * CHUNK, l_true)
    nck = (adm_end + kc - 1) // kc
    idx_bits = int(l_true).bit_length()
    groups = kc // SUBLANES

    rowi = lax.broadcasted_iota(I32, (kc, qb), 0)
    qpos = qoff + lax.broadcasted_iota(I32, (1, qb), 1)
    kmax = jnp.minimum((qpos // CHUNK + 1) * CHUNK, l_true)

    def score_body(c, carry):
        kidx_c = kidx_ref[0, c]
        sc = jnp.zeros((kc, qb), F32)
        for h in range(N_IDX_HEADS):
            d = _dot(kidx_c, qiT_ref[0, h])
            sc = sc + wT_ref[0, h:h + 1, :] * jnp.maximum(d, 0.0)
        bits = pltpu.bitcast(sc, I32)
        skey = jnp.where(bits < 0, bits ^ jnp.int32(0x7FFFFFFF), bits)
        skey = jnp.where(skey == -1, 0, skey)
        skey = jnp.where(rowi < kmax - c * kc, skey, jnp.int32(INT_MIN))
        skey_ref[c] = skey
        hi16_ref[c] = (skey >> 16).astype(I16)
        return carry

    lax.fori_loop(0, nck, score_body, 0)

    def count(pred_fn):
        def body(c, part):
            ind = jnp.where(pred_fn(c, skey_ref[c]), 1, 0)
            return part + jnp.sum(ind.reshape(groups, SUBLANES, qb), axis=0)
        part = lax.fori_loop(0, nck, body, jnp.zeros((SUBLANES, qb), I32))
        return jnp.sum(part, axis=0, keepdims=True)

    def count16(ref, pred_fn):
        pack = 2 * SUBLANES

        def body(c, part):
            accs = [None] * 4
            for g in range(kc // pack):
                ind = jnp.where(pred_fn(ref[c, g * pack:(g + 1) * pack, :]), jnp.bfloat16(1), jnp.bfloat16(0))
                accs[g % 4] = ind if accs[g % 4] is None else accs[g % 4] + ind
            return part + ((accs[0] + accs[1]) + (accs[2] + accs[3])).astype(F32)
        part = lax.fori_loop(0, nck, body, jnp.zeros((pack, qb), F32))
        return jnp.sum(part, axis=0, keepdims=True).astype(I32)

    def search16(ref, target, count_all):
        def body(it, carry):
            t_u, c_t = carry
            cand_u = t_u | (jnp.int32(1) << (15 - it))
            cand = (cand_u - 2 ** 15).astype(I16)
            cnt = count16(ref, lambda k: k >= cand)
            keep = cnt >= target
            return jnp.where(keep, cand_u, t_u), jnp.where(keep, cnt, c_t)
        return lax.fori_loop(0, 16, body, (jnp.zeros((1, qb), I32), count_all))

    stored = jnp.broadcast_to(nck * kc, (1, qb)).astype(I32)
    hi_u, n_ge_hi = search16(hi16_ref, topk, stored)
    hi_s = (hi_u - 2 ** 15).astype(I16)
    n_gt_hi = count16(hi16_ref, lambda k: k > hi_s)
    rest = topk - n_gt_hi

    def low_body(c, carry):
        low = ((skey_ref[c] & 0xFFFF) - 2 ** 15).astype(I16)
        lo16_ref[c] = jnp.where(hi16_ref[c] == hi_s, low, jnp.int16(-2 ** 15))
        return carry

    lax.fori_loop(0, nck, low_body, 0)
    lo_u, n_ge_lo = search16(lo16_ref, rest, n_ge_hi - n_gt_hi)
    thr = (hi_u - 2 ** 15) * 2 ** 16 + lo_u
    lo_s = (lo_u - 2 ** 15).astype(I16)
    cnt_ge = n_gt_hi + n_ge_lo
    cnt_gt = n_gt_hi + count16(lo16_ref, lambda k: k > lo_s)

    jb_ref[...] = jnp.full(jb_ref.shape, 2 ** 31 - 1, I32)

    @pl.when(jnp.max(cnt_ge) > topk)
    def _():
        need = topk - cnt_gt

        def tie_body(it, jb):
            cand = jb | (jnp.int32(1) << (idx_bits - 1 - it))
            cnt = count(lambda c, sk: jnp.logical_and(sk == thr, c * kc + rowi < cand))
            return jnp.where(cnt <= need, cand, jb)

        jb = lax.fori_loop(0, idx_bits, tie_body, jnp.zeros((1, qb), I32))
        jb_ref[...] = jnp.broadcast_to(jb, jb_ref.shape)

    jbound = jnp.where(thr == jnp.int32(INT_MIN), 0, jb_ref[0:1, :])
    thr_m1 = thr - 1

    for h in range(N_HEADS):
        l_refs[h][...] = jnp.zeros((1, qb), F32)
        acc_refs[h][...] = jnp.zeros((HEAD_DIM, qb), F32)
    n_slabs = kc // SLAB

    def fold(x, op):
        return op(x.reshape(SLAB // SUBLANES, SUBLANES, qb), axis=0)

    def near_bias(chunk_offset, h):
        tiles = []
        for t in range(kc // LANES):
            row = []
            for u in range(qb // LANES):
                d_tu = chunk_offset + (t - u) * LANES
                row.append(tab_ref[offsets.index(d_tu), h] if d_tu in offsets else jnp.zeros((LANES, LANES), F32))
            tiles.append(jnp.concatenate(row, axis=1))
        return jnp.concatenate(tiles, axis=0)

    def attn_chunk(c, near):
        sk = skey_ref[c]
        t_eff = jnp.where(rowi < jbound - c * kc, thr_m1, thr)
        madd_ref[...] = jnp.where(sk > t_eff, 0.0, MASK_VALUE)
        maccs = []
        for h in range(N_HEADS):
            x = _dot(kb_ref[0, c, h // GROUP], qT_ref[0, h]) + madd_ref[...]
            if near is not None:
                x = x + near_bias(near, h)
            s_refs[h][...] = x
            maccs.append(jnp.max(x.reshape(groups, SUBLANES, qb), axis=0))
        alphas = []
        for h in range(N_HEADS):
            s_ref, p_ref, m_ref = s_refs[h], p_refs[h], m_refs[h]
            m_old = m_ref[...]
            m_new = jnp.maximum(m_old, jnp.max(maccs[h], axis=0, keepdims=True))
            for j in range(n_slabs):
                rows = pl.ds(j * SLAB, SLAB)
                p_ref[rows, :] = jnp.exp2(s_ref[rows, :] - m_new).astype(BF16)
            m_ref[...] = m_new
            alphas.append(jnp.exp2(m_old - m_new))
        ones = jnp.ones((2 * SUBLANES, kc), BF16)
        v_aug = [jnp.concatenate([vT_ref[0, c, n * HEAD_DIM:(n + 1) * HEAD_DIM, :], ones], axis=0)
                 for n in range(N_KV_HEADS)]
        for h in range(N_HEADS):
            pv = _dot(v_aug[h // GROUP], p_refs[h][...])
            acc_refs[h][...] = alphas[h] * acc_refs[h][...] + pv[0:HEAD_DIM]
            l_refs[h][...] = alphas[h] * l_refs[h][...] + pv[HEAD_DIM:HEAD_DIM + 1]

    def attn_chunk_fixed(c, near):
        sk = skey_ref[c]
        t_eff = jnp.where(rowi < jbound - c * kc, thr_m1, thr)
        madd_ref[...] = jnp.where(sk > t_eff, 0.0, MASK_VALUE)
        ones = jnp.ones((2 * SUBLANES, kc), BF16)
        v_aug = [jnp.concatenate([vT_ref[0, c, n * HEAD_DIM:(n + 1) * HEAD_DIM, :], ones], axis=0)
                 for n in range(N_KV_HEADS)]
        for h in range(N_HEADS):
            x = _dot(kb_ref[0, c, h // GROUP], qT_ref[0, h]) + madd_ref[...]
            if near is not None:
                x = x + near_bias(near, h)
            p_refs[h][...] = jnp.exp2(x).astype(BF16)
        for h in range(N_HEADS):
            pv = _dot(v_aug[h // GROUP], p_refs[h][...])
            acc_refs[h][...] += pv[0:HEAD_DIM]
            l_refs[h][...] += pv[HEAD_DIM:HEAD_DIM + 1]

    def run(chunk_fn):
        c_diag = qoff // kc
        lax.fori_loop(0, c_diag - 1, lambda c, carry: (chunk_fn(c, None), carry)[1], 0)

        @pl.when(c_diag >= 1)
        def _():
            chunk_fn(c_diag - 1, -kc)

        chunk_fn(c_diag, 0)

    q_sq = jnp.zeros((1, qb), F32)
    for h in range(N_HEADS):
        qf = qT_ref[0, h].astype(F32)
        q_sq = jnp.maximum(q_sq, jnp.sum(qf * qf, axis=0, keepdims=True))
    bound = jnp.sqrt(jnp.max(q_sq) * jnp.max(knorm_ref[0])) + jnp.max(jnp.abs(tab_ref[...]))
    fixed_shift = bound < EXP2_SAFE

    @pl.when(fixed_shift)
    def _():
        run(attn_chunk_fixed)

    @pl.when(jnp.logical_not(fixed_shift))
    def _():
        for h in range(N_HEADS):
            m_refs[h][...] = jnp.full((1, qb), MASK_VALUE, F32)
        run(attn_chunk)

    oT = jnp.concatenate([acc_refs[h][...] / l_refs[h][...] for h in range(N_HEADS)], axis=0)
    o_ref[0] = oT.T.astype(o_ref.dtype)


def _attend(qT, qiT, wT, kidxb, kb, vT, knorm, tabs, *, qb, past, l_true, offsets):
    B, _, _, S = qT.shape
    nc, kc = vT.shape[1], vT.shape[3]
    assert S % qb == 0 and qb % LANES == 0 and kc % LANES == 0
    assert past % kc == 0 and qb <= kc and (qb == kc or S == qb)
    topk = min(TOPK_MAX, l_true // 4)
    kern = functools.partial(_attn_kernel, qb=qb, kc=kc, past=past, l_true=l_true,
                             topk=topk, offsets=offsets)
    in_specs = [
        pl.BlockSpec((1, N_HEADS, HEAD_DIM, qb), lambda b, i: (b, 0, 0, i)),
        pl.BlockSpec((1, N_IDX_HEADS, IDX_DIM, qb), lambda b, i: (b, 0, 0, i)),
        pl.BlockSpec((1, N_IDX_HEADS, qb), lambda b, i: (b, 0, i)),
        pl.BlockSpec((1,) + kidxb.shape[1:], lambda b, i: (b, 0, 0, 0)),
        pl.BlockSpec((1,) + kb.shape[1:], lambda b, i: (b, 0, 0, 0, 0)),
        pl.BlockSpec((1,) + vT.shape[1:], lambda b, i: (b, 0, 0, 0)),
        pl.BlockSpec((1,) + knorm.shape[1:], lambda b, i: (b, 0, 0, 0)),
        pl.BlockSpec(tabs.shape, lambda b, i: (0, 0, 0, 0)),
    ]
    return pl.pallas_call(
        kern,
        grid=(B, S // qb),
        in_specs=in_specs,
        out_specs=pl.BlockSpec((1, qb, N_HEADS * HEAD_DIM), lambda b, i: (b, i, 0)),
        out_shape=jax.ShapeDtypeStruct((B, S, N_HEADS * HEAD_DIM), BF16),
        scratch_shapes=[
            pltpu.VMEM((nc, kc, qb), I32),
            pltpu.VMEM((nc, kc, qb), I16),
            pltpu.VMEM((nc, kc, qb), I16),
            pltpu.VMEM((SUBLANES, qb), I32),
            pltpu.VMEM((kc, qb), F32),
        ] + [pltpu.VMEM((kc, qb), F32)] * N_HEADS
          + [pltpu.VMEM((kc, qb), BF16)] * N_HEADS
          + [pltpu.VMEM((1, qb), F32)] * N_HEADS
          + [pltpu.VMEM((1, qb), F32)] * N_HEADS
          + [pltpu.VMEM((HEAD_DIM, qb), F32)] * N_HEADS,
        compiler_params=_cparams(("arbitrary", "arbitrary")),
        name="dsa_attend",
    )(qT, qiT, wT, kidxb, kb, vT, knorm, tabs)


def _mix_kernel(x_ref, attn_ref, conv_ref, wo_ref, g_ref, b_ref, h_ref, *, dn_alpha):
    cat = jnp.concatenate([attn_ref[...], conv_ref[...]], axis=1)
    mixed = _dot(cat, wo_ref[...])
    h_ref[...] = _layer_norm(dn_alpha * x_ref[...] + mixed, g_ref[...], b_ref[...])


def _mix(x2, attn2, conv2, wo, g, b, *, dn_alpha):
    T, D = x2.shape
    tt = min(1024, T)
    assert T % tt == 0
    aw, cw = attn2.shape[1], conv2.shape[1]
    row = lambda i: (i, 0)
    const = lambda i: (0, 0)
    return pl.pallas_call(
        functools.partial(_mix_kernel, dn_alpha=dn_alpha),
        grid=(T // tt,),
        in_specs=[pl.BlockSpec((tt, D), row), pl.BlockSpec((tt, aw), row), pl.BlockSpec((tt, cw), row),
                  pl.BlockSpec(wo.shape, const), pl.BlockSpec((1, D), const), pl.BlockSpec((1, D), const)],
        out_specs=pl.BlockSpec((tt, D), row),
        out_shape=jax.ShapeDtypeStruct((T, D), F32),
        compiler_params=_cparams(("arbitrary",)),
        name="outproj_ln1",
    )(x2, attn2, conv2, wo, g, b)


def _route_picks(logits):
    lane = lax.broadcasted_iota(I32, logits.shape, 1)
    work = logits
    vals, hots = [], []
    for _ in range(TOP_K):
        m = jnp.max(work, axis=1, keepdims=True)
        idx = jnp.min(jnp.where(work == m, lane, LANES), axis=1, keepdims=True)
        hot = lane == idx
        vals.append(m)
        hots.append(hot)
        work = jnp.where(hot, -jnp.inf, work)
    es = [jnp.exp(v - vals[0]) for v in vals]
    den = es[0]
    for e in es[1:]:
        den = den + e
    return hots, [e / den for e in es]


def _route(logits):
    hots, gates = _route_picks(logits)
    comb = jnp.zeros(logits.shape, F32)
    for gate, hot in zip(gates, hots):
        comb = jnp.where(hot, gate, comb)
    return comb


def _expert_mlp(xb, wgu_ref, bgu_ref, wd_ref, bd_ref):
    ff = wd_ref.shape[1]
    gu = _dot(xb, wgu_ref[0]) + bgu_ref[0]
    gate = jnp.minimum(gu[:, 0:ff], SWIGLU_LIMIT)
    up = jnp.clip(gu[:, ff:2 * ff], -SWIGLU_LIMIT, SWIGLU_LIMIT)
    act = (up + 1.0) * (gate * (1.0 / (1.0 + jnp.exp(-SWIGLU_ALPHA * gate))))
    return _dot(act.astype(BF16), wd_ref[0]) + bd_ref[0]


def _moe_kernel(h_ref, wr_ref, br_ref, wgu_ref, bgu_ref, wd_ref, bd_ref, g_ref, b_ref,
                y_ref, hb_ref, comb_ref, *, dn_alpha):
    e = pl.program_id(1)
    tt = h_ref.shape[0]

    @pl.when(e == 0)
    def _():
        hb = h_ref[...].astype(BF16)
        hb_ref[...] = hb
        comb_ref[...] = _route(_dot(hb, wr_ref[...]) + br_ref[...])
        y_ref[...] = jnp.zeros(y_ref.shape, F32)

    y = _expert_mlp(hb_ref[...], wgu_ref, bgu_ref, wd_ref, bd_ref)
    lane = lax.broadcasted_iota(I32, (tt, LANES), 1)
    y_ref[...] += jnp.sum(jnp.where(lane == e, comb_ref[...], 0.0), axis=1, keepdims=True) * y

    @pl.when(e == pl.num_programs(1) - 1)
    def _():
        y_ref[...] = _layer_norm(dn_alpha * h_ref[...] + y_ref[...], g_ref[...], b_ref[...])


def _moe(h2, wr, br, wgu, bgu, wd, bd, g, b, *, dn_alpha):
    T, D = h2.shape
    E, _, F2 = wgu.shape
    tt = min(512, T)
    assert T % tt == 0
    row = lambda i, e: (i, 0)
    const = lambda i, e: (0, 0)
    exp3 = lambda i, e: (e, 0, 0)
    return pl.pallas_call(
        functools.partial(_moe_kernel, dn_alpha=dn_alpha),
        grid=(T // tt, E),
        in_specs=[pl.BlockSpec((tt, D), row), pl.BlockSpec(wr.shape, const), pl.BlockSpec(br.shape, const),
                  pl.BlockSpec((1, D, F2), exp3), pl.BlockSpec((1, 1, F2), exp3),
                  pl.BlockSpec((1, F2 // 2, D), exp3), pl.BlockSpec((1, 1, D), exp3),
                  pl.BlockSpec((1, D), const), pl.BlockSpec((1, D), const)],
        out_specs=pl.BlockSpec((tt, D), row),
        out_shape=jax.ShapeDtypeStruct((T, D), F32),
        scratch_shapes=[pltpu.VMEM((tt, D), BF16), pltpu.VMEM((tt, LANES), F32)],
        compiler_params=_cparams(("arbitrary", "arbitrary")),
        name="moe_ln2",
    )(h2, wr, br, wgu, bgu, wd, bd, g, b)


def _pack_pairs(x):
    w = x.shape[1] // 2
    bits = pltpu.bitcast(x.astype(BF16).astype(F32), I32)
    return (bits[:, 0:w] & jnp.int32(-65536)) | lax.shift_right_logical(bits[:, w:2 * w], 16)


def _unpack_pairs(p):
    hi = pltpu.bitcast(p & jnp.int32(-65536), F32)
    lo = pltpu.bitcast(lax.shift_left(p, 16), F32)
    return jnp.concatenate([hi, lo], axis=1)


def _mix_route_kernel(x_ref, attn_ref, conv_ref, wo_ref, g_ref, b_ref, wrT_ref, brT_ref,
                      h_ref, hpk_ref, infoT_ref, gcol_ref, cnt_ref, *, dn_alpha):
    cat = jnp.concatenate([attn_ref[...], conv_ref[...]], axis=1)
    h = _layer_norm(dn_alpha * x_ref[...] + _dot(cat, wo_ref[...]), g_ref[...], b_ref[...])
    h_ref[...] = h
    tt = h.shape[0]
    ne = cnt_ref.shape[1]
    hpk_ref[...] = _pack_pairs(h)
    logits = lax.dot_general(wrT_ref[...], h.astype(BF16), (((1,), (1,)), ((), ())),
                             preferred_element_type=F32)
    work = logits[0:ne] + jnp.tile(brT_ref[0:ne, :], (1, tt // LANES))
    sub = lax.broadcasted_iota(I32, (ne, tt), 0)
    vals, idxs, hots = [], [], []
    for _ in range(TOP_K):
        m = jnp.max(work, axis=0, keepdims=True)
        idx = jnp.min(jnp.where(work == m, sub, ne), axis=0, keepdims=True)
        hot = sub == idx
        vals.append(m)
        idxs.append(idx)
        hots.append(hot)
        work = jnp.where(hot, -jnp.inf, work)
    es = [jnp.exp(v - vals[0]) for v in vals]
    den = es[0]
    for e in es[1:]:
        den = den + e
    gates = [e / den for e in es]
    picked = jnp.zeros((ne, tt), F32)
    for hot in hots:
        picked = jnp.where(hot, 1.0, picked)
    earlier = lax.broadcasted_iota(I32, (tt, tt), 0) < lax.broadcasted_iota(I32, (tt, tt), 1)
    rank = _dot(picked.astype(BF16), jnp.where(earlier, 1.0, 0.0).astype(BF16))
    ranks = [jnp.sum(jnp.where(hot, rank, 0.0), axis=0, keepdims=True) for hot in hots]
    pad = jnp.zeros((2 * SUBLANES - 3 * TOP_K, tt), F32)
    infoT_ref[...] = jnp.concatenate([i.astype(F32) for i in idxs] + gates + ranks + [pad], axis=0)
    gcol_ref[...] = jnp.concatenate(gates + [jnp.zeros((LANES - TOP_K, tt), F32)], axis=0).T
    cnt_ref[0] = jnp.broadcast_to(jnp.sum(picked, axis=1, keepdims=True), (ne, LANES))


def _mix_route(x2, attn2, conv2, wo, g, b, wr, br, ne, *, dn_alpha):
    T, D = x2.shape
    tt = ROUTE_TILE
    aw, cw = attn2.shape[1], conv2.shape[1]
    row = lambda i: (i, 0)
    const = lambda i: (0, 0)
    wrT = wr.T
    brT = jnp.broadcast_to(br.reshape(LANES, 1), (LANES, LANES))
    return pl.pallas_call(
        functools.partial(_mix_route_kernel, dn_alpha=dn_alpha),
        grid=(T // tt,),
        in_specs=[pl.BlockSpec((tt, D), row), pl.BlockSpec((tt, aw), row), pl.BlockSpec((tt, cw), row),
                  pl.BlockSpec(wo.shape, const), pl.BlockSpec((1, D), const), pl.BlockSpec((1, D), const),
                  pl.BlockSpec(wrT.shape, const), pl.BlockSpec(brT.shape, const)],
        out_specs=(pl.BlockSpec((tt, D), row), pl.BlockSpec((tt, D // 2), row),
                   pl.BlockSpec((2 * SUBLANES, tt), lambda i: (0, i)),
                   pl.BlockSpec((tt, LANES), row), pl.BlockSpec((1, ne, LANES), lambda i: (i, 0, 0))),
        out_shape=(jax.ShapeDtypeStruct((T, D), F32), jax.ShapeDtypeStruct((T, D // 2), I32),
                   jax.ShapeDtypeStruct((2 * SUBLANES, T), F32),
                   jax.ShapeDtypeStruct((T, LANES), F32), jax.ShapeDtypeStruct((T // tt, ne, LANES), F32)),
        compiler_params=_cparams(("arbitrary",)),
        name="outproj_ln1_route",
    )(x2, attn2, conv2, wo, g, b, wrT, brT)


def _sc_gather_rows(table, idx):
    M, W = idx.shape[0], table.shape[1]
    n_workers = SC_CORES * SC_SUBCORES
    per_w = M // n_workers
    n_ch = per_w // SC_ROWS
    assert M == n_workers * n_ch * SC_ROWS
    mesh = plsc.VectorSubcoreMesh(core_axis_name="c", subcore_axis_name="s")

    @functools.partial(
        pl.kernel, mesh=mesh,
        out_type=jax.ShapeDtypeStruct((M, W), table.dtype),
        scratch_types=[pltpu.VMEM((n_ch, SC_ROWS), I32), pltpu.VMEM((SC_ROWS, W), table.dtype),
                       pltpu.SemaphoreType.DMA],
    )
    def gather(table_hbm, idx_hbm, out_hbm, idx_v, rows_v, sem):
        wid = lax.axis_index("s") * SC_CORES + lax.axis_index("c")
        pltpu.sync_copy(idx_hbm.at[wid], idx_v)

        @pl.loop(0, n_ch)
        def _(j):
            pltpu.async_copy(table_hbm.at[idx_v.at[j]], rows_v, sem).wait()
            pltpu.sync_copy(rows_v, out_hbm.at[pl.ds(wid * per_w + j * SC_ROWS, SC_ROWS)])

    return gather(table, idx.reshape(n_workers, n_ch, SC_ROWS))


def _expert_rows_kernel(te_ref, used_ref, x_ref, wgu_ref, bgu_ref, wd_ref, bd_ref, y_ref):
    @pl.when(pl.program_id(0) < used_ref[0])
    def _():
        xg = _unpack_pairs(x_ref[...]).astype(BF16)
        y_ref[...] = _pack_pairs(_expert_mlp(xg, wgu_ref, bgu_ref, wd_ref, bd_ref))


def _expert_rows(tile_expert, used_tiles, xs, wgu, bgu, wd, bd):
    R, W = xs.shape
    E, D, F2 = wgu.shape
    exp3 = lambda j, te, used: (te[j], 0, 0)
    rows = lambda j, te, used: (j, 0)
    grid_spec = pltpu.PrefetchScalarGridSpec(
        num_scalar_prefetch=2,
        grid=(R // ROW_TILE,),
        in_specs=[pl.BlockSpec((ROW_TILE, W), rows),
                  pl.BlockSpec((1, D, F2), exp3), pl.BlockSpec((1, 1, F2), exp3),
                  pl.BlockSpec((1, F2 // 2, D), exp3), pl.BlockSpec((1, 1, D), exp3)],
        out_specs=pl.BlockSpec((ROW_TILE, W), rows),
    )
    return pl.pallas_call(
        _expert_rows_kernel,
        grid_spec=grid_spec,
        out_shape=jax.ShapeDtypeStruct((R, W), I32),
        compiler_params=_cparams(("arbitrary",)),
        name="moe_expert_rows",
    )(tile_expert, used_tiles, xs, wgu, bgu, wd, bd)


def _combine_kernel(h_ref, yg_ref, gcol_ref, g_ref, b_ref, o_ref, *, dn_alpha):
    acc = dn_alpha * h_ref[...]
    for k in range(TOP_K):
        acc = acc + gcol_ref[:, k:k + 1] * _unpack_pairs(yg_ref[k])
    o_ref[...] = _layer_norm(acc, g_ref[...], b_ref[...])


def _combine(h2, yg, gcol, g, b, *, dn_alpha):
    T, D = h2.shape
    tt = 512
    row = lambda i: (i, 0)
    const = lambda i: (0, 0)
    return pl.pallas_call(
        functools.partial(_combine_kernel, dn_alpha=dn_alpha),
        grid=(T // tt,),
        in_specs=[pl.BlockSpec((tt, D), row), pl.BlockSpec((TOP_K, tt, D // 2), lambda i: (0, i, 0)),
                  pl.BlockSpec((tt, LANES), row),
                  pl.BlockSpec((1, D), const), pl.BlockSpec((1, D), const)],
        out_specs=pl.BlockSpec((tt, D), row),
        out_shape=jax.ShapeDtypeStruct((T, D), F32),
        compiler_params=_cparams(("arbitrary",)),
        name="moe_combine_ln2",
    )(h2, yg, gcol, g, b)


def _dest_kernel(infoT_ref, base_ref, dest_ref):
    tt = infoT_ref.shape[1]
    ne = base_ref.shape[1]
    base = jnp.tile(base_ref[0], (1, tt // LANES))
    sub = lax.broadcasted_iota(I32, (ne, tt), 0)
    rows = []
    for k in range(TOP_K):
        hot = sub == infoT_ref[k:k + 1, :].astype(I32)
        rows.append(jnp.sum(jnp.where(hot, base, 0.0), axis=0, keepdims=True)
                    + infoT_ref[2 * TOP_K + k:2 * TOP_K + k + 1, :])
    rows.append(jnp.zeros((SUBLANES - TOP_K, tt), F32))
    dest_ref[...] = jnp.concatenate(rows, axis=0).astype(I32)


def _dest_rows(infoT, base):
    T = infoT.shape[1]
    tt = ROUTE_TILE
    ne = base.shape[1]
    return pl.pallas_call(
        _dest_kernel,
        grid=(T // tt,),
        in_specs=[pl.BlockSpec((2 * SUBLANES, tt), lambda i: (0, i)), pl.BlockSpec((1, ne, LANES), lambda i: (i, 0, 0))],
        out_specs=pl.BlockSpec((SUBLANES, tt), lambda i: (0, i)),
        out_shape=jax.ShapeDtypeStruct((SUBLANES, T), I32),
        compiler_params=_cparams(("arbitrary",)),
        name="moe_dest_rows",
    )(infoT, base)


def _sc_scatter_rows(src, dest_km, n_rows):
    T, W = src.shape
    K = dest_km.shape[0]
    n_workers = SC_CORES * SC_SUBCORES
    per_w = T // n_workers
    n_ch = per_w // SC_ROWS
    assert T == n_workers * n_ch * SC_ROWS
    mesh = plsc.VectorSubcoreMesh(core_axis_name="c", subcore_axis_name="s")
    idx = dest_km.reshape(K, n_workers, n_ch, SC_ROWS).transpose(1, 0, 2, 3).reshape(n_workers, K * n_ch, SC_ROWS)

    @functools.partial(
        pl.kernel, mesh=mesh,
        out_type=jax.ShapeDtypeStruct((n_rows, W), src.dtype),
        scratch_types=[pltpu.VMEM((K * n_ch, SC_ROWS), I32), pltpu.VMEM((SC_ROWS, W), src.dtype),
                       pltpu.SemaphoreType.DMA],
    )
    def scatter(src_hbm, idx_hbm, out_hbm, idx_v, rows_v, sem):
        wid = lax.axis_index("s") * SC_CORES + lax.axis_index("c")
        pltpu.sync_copy(idx_hbm.at[wid], idx_v)

        @pl.loop(0, n_ch)
        def _(j):
            pltpu.sync_copy(src_hbm.at[pl.ds(wid * per_w + j * SC_ROWS, SC_ROWS)], rows_v)
            for k in range(K):
                pltpu.async_copy(rows_v, out_hbm.at[idx_v.at[k * n_ch + j]], sem).wait()

    return scatter(src, idx)


def _moe_sorted(h2, routed, wgu, bgu, wd, bd, g, b, *, dn_alpha):
    T, D = h2.shape
    E = wgu.shape[0]
    hpk, infoT, gcol, cnt = routed
    cnt = cnt[:, :, 0].astype(I32)
    total = jnp.sum(cnt, axis=0)
    padded = -(-total // ROW_TILE) * ROW_TILE
    ends = jnp.cumsum(padded)
    base = (ends - padded)[None, :] + jnp.cumsum(cnt, axis=0) - cnt
    base = jnp.broadcast_to(base.astype(F32)[:, :, None], cnt.shape + (LANES,))
    dest_km = _dest_rows(infoT, base)[0:TOP_K]
    R = T * TOP_K + E * ROW_TILE
    tile_start = jnp.arange(R // ROW_TILE, dtype=I32) * ROW_TILE
    tile_expert = jnp.minimum(jnp.sum((ends[None, :] <= tile_start[:, None]).astype(I32), axis=1), E - 1)
    used_tiles = (ends[E - 1] // ROW_TILE).reshape(1)
    xs = _sc_scatter_rows(hpk, dest_km, R)
    ys = _expert_rows(tile_expert, used_tiles, xs, wgu, bgu, wd, bd)
    yg = _sc_gather_rows(ys, dest_km.reshape(-1))
    return _combine(h2, yg.reshape(TOP_K, T, D // 2), gcol, g, b, dn_alpha=dn_alpha)


def _split_gu_kernel(w_ref, perm_ref, o_ref, *, band):
    n = w_ref.shape[2]
    for m in range(n // band):
        both = _dot(w_ref[0, :, m * band:(m + 1) * band].astype(BF16), perm_ref[...]).astype(BF16)
        o_ref[0, :, m * (band // 2):(m + 1) * (band // 2)] = both[:, 0:band // 2]
        o_ref[0, :, n // 2 + m * (band // 2):n // 2 + (m + 1) * (band // 2)] = both[:, band // 2:band]


def _split_gu(w_gu):
    E, D, N = w_gu.shape
    tr = min(1024, D)
    band = min(512, N)
    j = jnp.arange(band, dtype=I32)
    src = jnp.where(j < band // 2, 2 * j, 2 * (j - band // 2) + 1)
    perm = (jnp.arange(band, dtype=I32)[:, None] == src[None, :]).astype(BF16)
    return pl.pallas_call(
        functools.partial(_split_gu_kernel, band=band),
        grid=(E, D // tr),
        in_specs=[pl.BlockSpec((1, tr, N), lambda e, r: (e, r, 0)),
                  pl.BlockSpec((band, band), lambda e, r: (0, 0))],
        out_specs=pl.BlockSpec((1, tr, N), lambda e, r: (e, r, 0)),
        out_shape=jax.ShapeDtypeStruct((E, D, N), BF16),
        compiler_params=_cparams(("arbitrary", "arbitrary")),
        name="split_gate_up",
    )(w_gu, perm)


def _prep_weights(w_in, w_conv, w_out, ln1_g, ln1_b, w_router, b_router, w_gu, b_gu, w_down, b_down,
                  ln2_g, ln2_b):
    aw = N_HEADS * HEAD_DIM
    kv = N_KV_HEADS * HEAD_DIM
    iw = N_IDX_HEADS * IDX_DIM
    D = w_in.shape[0]
    cw = D - aw
    o_q, o_k, o_v = 0, aw, aw + kv
    o_qi = o_v + kv
    o_ki = o_qi + iw
    o_wi = o_ki + IDX_DIM
    o_c = o_wi + N_IDX_HEADS
    wa = jnp.concatenate([w_in[:, o_q:o_k], w_in[:, o_qi:o_ki], w_in[:, o_k:o_v], w_in[:, o_v:o_qi]],
                         axis=1).astype(BF16)
    ws = jnp.pad(w_in[:, o_ki:o_c], ((0, 0), (0, LANES - IDX_DIM - N_IDX_HEADS))).astype(BF16)
    wc = w_in[:, o_c:o_c + 3 * cw].astype(BF16)
    E = w_router.shape[1]
    wr = jnp.pad(w_router, ((0, 0), (0, LANES - E))).astype(BF16)
    br = jnp.pad(b_router, (0, LANES - E), constant_values=MASK_VALUE).reshape(1, LANES)
    F = w_gu.shape[2] // 2
    return dict(
        wa=wa, ws=ws, wc=wc, wconv=w_conv, wo=w_out.astype(BF16),
        ln1_g=ln1_g.reshape(1, D), ln1_b=ln1_b.reshape(1, D),
        wr=wr, br=br,
        wgu=_split_gu(w_gu),
        bgu=jnp.concatenate([b_gu[:, 0::2], b_gu[:, 1::2]], axis=1).reshape(E, 1, 2 * F),
        wd=w_down.astype(BF16), bd=b_down.reshape(E, 1, D),
        ln2_g=ln2_g.reshape(1, D), ln2_b=ln2_b.reshape(1, D),
    )


def _decode_layouts(pq, small, k_all, v_all, kidx_all, kc, qb):
    B, S, _ = pq.shape
    aw = N_HEADS * HEAD_DIM
    L, kv = k_all.shape[1], k_all.shape[2]
    lp = -(-L // kc) * kc
    nc = lp // kc
    padk = lambda a: jnp.pad(a, ((0, 0), (0, lp - L), (0, 0)))
    padq = lambda a: jnp.pad(a, ((0, 0), (0, 0), (0, 0), (0, qb - S)))
    q = pq[:, :, 0:aw] * QK_SCALE
    qT = padq(q.reshape(B, S, N_HEADS, HEAD_DIM).transpose(0, 2, 3, 1)).astype(BF16)
    qiT = padq(pq[:, :, aw:].reshape(B, S, N_IDX_HEADS, IDX_DIM).transpose(0, 2, 3, 1)).astype(BF16)
    wT = jnp.pad(small[:, :, IDX_DIM:IDX_DIM + N_IDX_HEADS].transpose(0, 2, 1), ((0, 0), (0, 0), (0, qb - S)))
    kb = padk(k_all).reshape(B, nc, kc, N_KV_HEADS, HEAD_DIM).transpose(0, 1, 3, 2, 4).astype(BF16)
    vT = padk(v_all).reshape(B, nc, kc, kv).transpose(0, 1, 3, 2).astype(BF16)
    kidxb = padk(kidx_all).reshape(B, nc, kc, IDX_DIM).astype(BF16)
    knorm = jnp.sum(kb.astype(F32) ** 2, axis=-1)
    return qT, qiT, wT, kidxb, kb, vT, knorm


def _layer(x, past_k, past_v, past_kidx, conv_buf, rel_bias, w, *, dn_alpha):
    B, S, D = x.shape
    prefill = past_k is None
    outs = _project(x, conv_buf, w["wa"], w["ws"], w["wc"], w["wconv"], attn_layouts=prefill)
    k, v, small, conv, new_buf = outs[:5]
    k_idx = small if prefill else small[:, :, 0:IDX_DIM]
    if prefill:
        past, l_true = 0, S
        qb = min(QUERY_BLOCK, S)
        kb, vT, kidxb, qT, qiT, wT, knorm = outs[5:]
    else:
        past = past_k.shape[1]
        l_true = past + S
        qb = -(-S // LANES) * LANES
        qT, qiT, wT, kidxb, kb, vT, knorm = _decode_layouts(
            outs[5], small, jnp.concatenate([past_k, k], axis=1), jnp.concatenate([past_v, v], axis=1),
            jnp.concatenate([past_kidx, k_idx], axis=1), math.gcd(past, KEY_CHUNK), qb)
    offsets = (-LANES, 0)
    tabs = _bias_tables(rel_bias, offsets)
    attn = _attend(qT, qiT, wT, kidxb, kb, vT, knorm, tabs, qb=qb, past=past, l_true=l_true, offsets=offsets)
    attn = attn[:, :S]
    mix_args = (x.reshape(B * S, D), attn.reshape(B * S, -1), conv.reshape(B * S, -1), w["wo"],
                w["ln1_g"], w["ln1_b"])
    experts = (w["wgu"], w["bgu"], w["wd"], w["bd"], w["ln2_g"], w["ln2_b"])
    sc_unit = SC_CORES * SC_SUBCORES * SC_ROWS
    sortable = ((B * S) % ROUTE_TILE == 0 and (B * S) % sc_unit == 0
                and (N_EXPERTS * ROW_TILE) % sc_unit == 0)
    if sortable and B * S >= SORTED_MIN_TOKENS:
        h, *routed = _mix_route(*mix_args, w["wr"], w["br"], N_EXPERTS, dn_alpha=dn_alpha)
        y = _moe_sorted(h, routed, *experts, dn_alpha=dn_alpha)
    else:
        h = _mix(*mix_args, dn_alpha=dn_alpha)
        y = _moe(h, w["wr"], w["br"], *experts, dn_alpha=dn_alpha)
    return (y.reshape(B, S, D), k.reshape(B, S, N_KV_HEADS, HEAD_DIM), v.reshape(B, S, N_KV_HEADS, HEAD_DIM),
            k_idx, new_buf)


def kernel(x_prompt, x_sample, cache_k, cache_v, cache_kidx, state_conv, rel_bias, w_in, w_conv, w_out,
           ln1_g, ln1_b, w_router, b_router, w_gu, b_gu, w_down, b_down, ln2_g, ln2_b):
    depth = w_in.shape[0]
    assert depth == 1
    dn_alpha = (2 * depth) ** 0.25
    kv = N_KV_HEADS * HEAD_DIM
    w = _prep_weights(w_in[0], w_conv[0], w_out[0], ln1_g[0], ln1_b[0], w_router[0], b_router[0],
                      w_gu[0], b_gu[0], w_down[0], b_down[0], ln2_g[0], ln2_b[0])
    Bp = x_prompt.shape[0]
    cw = w_conv.shape[2]
    zero_buf = jnp.zeros((Bp, CONV_K - 1, cw), F32)
    yp, k1, v1, i1, c1 = _layer(x_prompt, None, None, None, zero_buf, rel_bias, w, dn_alpha=dn_alpha)
    Bs, P = cache_k.shape[1], cache_k.shape[2]
    ys, k2, v2, i2, c2 = _layer(x_sample, cache_k[0].reshape(Bs, P, kv), cache_v[0].reshape(Bs, P, kv),
                                cache_kidx[0], state_conv[0], rel_bias, w, dn_alpha=dn_alpha)
    return (yp, ys, k1[None], v1[None], i1[None], c1[None], k2[None], v2[None], i2[None], c2[None])
```

```python
import functools
import math

import jax
import jax.numpy as jnp
from jax import lax
from jax.experimental import pallas as pl
from jax.experimental.pallas import tpu as pltpu
from jax.experimental.pallas import tpu_sc as plsc

F32 = jnp.float32
BF16 = jnp.bfloat16
I32 = jnp.int32
I16 = jnp.int16

CHUNK = 64
N_HEADS = 8
HEAD_DIM = 64
N_KV_HEADS = 2
GROUP = N_HEADS // N_KV_HEADS
N_IDX_HEADS = 8
IDX_DIM = 32
TOPK_MAX = 256
CONV_K = 3
N_BUCKETS = 32
MAX_DISTANCE = 128
N_EXPERTS = 32
TOP_K = 4
SWIGLU_LIMIT = 7.0
SWIGLU_ALPHA = 1.702
LN_EPS = 1e-5
MASK_VALUE = -1e30
QK_SCALE = HEAD_DIM ** -0.5 * math.log2(math.e)
EXP2_SAFE = 96.0

LANES = 128
SUBLANES = 8
KEY_CHUNK = 512
QUERY_BLOCK = 512
SLAB = 64
ROW_TILE = 1024
ROUTE_TILE = 1024
SC_CORES = 2
SC_SUBCORES = 16
SC_ROWS = 128
SORTED_MIN_TOKENS = 2048
INT_MIN = -(2 ** 31)
VMEM_LIMIT = 56 * 1024 * 1024


def _cparams(sem):
    return pltpu.CompilerParams(dimension_semantics=sem, vmem_limit_bytes=VMEM_LIMIT)


def _dot(a, b):
    return jnp.dot(a, b, preferred_element_type=F32)


def _layer_norm(z, g, b):
    mu = jnp.mean(z, axis=-1, keepdims=True)
    zc = z - mu
    var = jnp.mean(zc * zc, axis=-1, keepdims=True)
    return zc * lax.rsqrt(var + LN_EPS) * g + b


def _bucket_thresholds():
    nb = N_BUCKETS // 2
    max_exact = nb // 2
    out = []
    for j in range(1, nb - max_exact):
        out.append(math.ceil(max_exact * (MAX_DISTANCE / max_exact) ** (j / (nb - max_exact)) - 1e-9))
    return tuple(out)


def _bias_table_kernel(rel_ref, tab_ref, *, offsets):
    nb = N_BUCKETS // 2
    max_exact = nb // 2
    thr = _bucket_thresholds()
    ii = lax.broadcasted_iota(I32, (LANES, LANES), 0)
    jj = lax.broadcasted_iota(I32, (LANES, LANES), 1)
    for d, off in enumerate(offsets):
        rel = off + ii - jj
        n = jnp.abs(rel)
        large = jnp.full((LANES, LANES), max_exact, I32)
        for t in thr:
            large = large + jnp.where(n >= t, 1, 0)
        bucket = jnp.where(rel > 0, nb, 0) + jnp.where(n < max_exact, n, large)
        for h in range(N_HEADS):
            acc = jnp.zeros((LANES, LANES), F32)
            for b in range(N_BUCKETS):
                acc = jnp.where(bucket == b, rel_ref[b, h], acc)
            tab_ref[d, h] = (acc - rel_ref[nb - 1, h]) * math.log2(math.e)


def _bias_tables(rel_bias, offsets):
    return pl.pallas_call(
        functools.partial(_bias_table_kernel, offsets=offsets),
        out_shape=jax.ShapeDtypeStruct((len(offsets), N_HEADS, LANES, LANES), F32),
        in_specs=[pl.BlockSpec(memory_space=pltpu.SMEM)],
        out_specs=pl.BlockSpec(memory_space=pltpu.VMEM),
        name="bias_tables",
    )(rel_bias)


def _proj_kernel(x_ref, wa_ref, ws_ref, wc_ref, wconv_ref, buf_ref,
                 k_ref, v_ref, small_ref, conv_ref, nbuf_ref, *rest, ts, attn_layouts):
    s = pl.program_id(1)
    xb = x_ref[0].astype(BF16)
    pa = _dot(xb, wa_ref[...])
    ps = _dot(xb, ws_ref[...])
    pc = _dot(xb, wc_ref[...])
    aw = N_HEADS * HEAD_DIM
    iw = N_IDX_HEADS * IDX_DIM
    kv = N_KV_HEADS * HEAD_DIM
    k = pa[:, aw + iw:aw + iw + kv]
    v = pa[:, aw + iw + kv:aw + iw + 2 * kv]
    k_ref[0] = k
    v_ref[0] = v
    small_ref[0] = ps[:, 0:small_ref.shape[2]]
    scale = QK_SCALE
    if attn_layouts:
        kb_ref, vT_ref, kidxb_ref, qT_ref, qiT_ref, wT_ref, knorm_ref, carry_ref = rest
        for n in range(N_KV_HEADS):
            kb_ref[0, 0, n] = k[:, n * HEAD_DIM:(n + 1) * HEAD_DIM].astype(BF16)
        kTf = k.astype(BF16).astype(F32).T
        knorm_ref[0, 0] = jnp.concatenate(
            [jnp.sum(kTf[n * HEAD_DIM:(n + 1) * HEAD_DIM] ** 2, axis=0, keepdims=True) for n in range(N_KV_HEADS)],
            axis=0)
        vT_ref[0, 0] = v.T.astype(BF16)
        kidxb_ref[0, 0] = ps[:, 0:IDX_DIM].astype(BF16)
        qT = (pa[:, 0:aw] * scale).T.astype(BF16)
        for h in range(N_HEADS):
            qT_ref[0, h] = qT[h * HEAD_DIM:(h + 1) * HEAD_DIM]
        qiT = pa[:, aw:aw + iw].T.astype(BF16)
        for h in range(N_IDX_HEADS):
            qiT_ref[0, h] = qiT[h * IDX_DIM:(h + 1) * IDX_DIM]
        wT_ref[0] = ps.T[IDX_DIM:IDX_DIM + N_IDX_HEADS]
    else:
        pq_ref, carry_ref = rest
        pq_ref[0] = pa[:, 0:aw + iw]

    cw = pc.shape[1] // 3
    u = pc[:, cw:2 * cw] * pc[:, 2 * cw:3 * cw]

    @pl.when(s == 0)
    def _():
        carry_ref[6:8, :] = buf_ref[0]

    carry_ref[8:8 + ts, :] = u
    y = (carry_ref[6:6 + ts, :] * wconv_ref[0:1, :]
         + carry_ref[7:7 + ts, :] * wconv_ref[1:2, :]
         + u * wconv_ref[2:3, :])
    conv_ref[0] = (pc[:, 0:cw] * y).astype(BF16)
    nb = carry_ref[ts + 6:ts + 8, :]
    nbuf_ref[0] = nb
    carry_ref[6:8, :] = nb


def _project(x, conv_buf, wa, ws, wc, wconv, *, attn_layouts):
    B, S, D = x.shape
    ts = min(KEY_CHUNK, S)
    assert S % ts == 0 and S >= CONV_K - 1
    ns = S // ts
    cw = wc.shape[1] // 3
    sw = IDX_DIM if attn_layouts else LANES
    kv = N_KV_HEADS * HEAD_DIM
    aw = N_HEADS * HEAD_DIM
    iw = N_IDX_HEADS * IDX_DIM
    out_shape = [
        jax.ShapeDtypeStruct((B, S, kv), F32),
        jax.ShapeDtypeStruct((B, S, kv), F32),
        jax.ShapeDtypeStruct((B, S, sw), F32),
        jax.ShapeDtypeStruct((B, S, cw), BF16),
        jax.ShapeDtypeStruct((B, CONV_K - 1, cw), F32),
    ]
    out_specs = [
        pl.BlockSpec((1, ts, kv), lambda b, s: (b, s, 0)),
        pl.BlockSpec((1, ts, kv), lambda b, s: (b, s, 0)),
        pl.BlockSpec((1, ts, sw), lambda b, s: (b, s, 0)),
        pl.BlockSpec((1, ts, cw), lambda b, s: (b, s, 0)),
        pl.BlockSpec((1, CONV_K - 1, cw), lambda b, s: (b, 0, 0)),
    ]
    if attn_layouts:
        out_shape += [
            jax.ShapeDtypeStruct((B, ns, N_KV_HEADS, ts, HEAD_DIM), BF16),
            jax.ShapeDtypeStruct((B, ns, kv, ts), BF16),
            jax.ShapeDtypeStruct((B, ns, ts, IDX_DIM), BF16),
            jax.ShapeDtypeStruct((B, N_HEADS, HEAD_DIM, S), BF16),
            jax.ShapeDtypeStruct((B, N_IDX_HEADS, IDX_DIM, S), BF16),
            jax.ShapeDtypeStruct((B, N_IDX_HEADS, S), F32),
            jax.ShapeDtypeStruct((B, ns, N_KV_HEADS, ts), F32),
        ]
        out_specs += [
            pl.BlockSpec((1, 1, N_KV_HEADS, ts, HEAD_DIM), lambda b, s: (b, s, 0, 0, 0)),
            pl.BlockSpec((1, 1, kv, ts), lambda b, s: (b, s, 0, 0)),
            pl.BlockSpec((1, 1, ts, IDX_DIM), lambda b, s: (b, s, 0, 0)),
            pl.BlockSpec((1, N_HEADS, HEAD_DIM, ts), lambda b, s: (b, 0, 0, s)),
            pl.BlockSpec((1, N_IDX_HEADS, IDX_DIM, ts), lambda b, s: (b, 0, 0, s)),
            pl.BlockSpec((1, N_IDX_HEADS, ts), lambda b, s: (b, 0, s)),
            pl.BlockSpec((1, 1, N_KV_HEADS, ts), lambda b, s: (b, s, 0, 0)),
        ]
    else:
        out_shape += [jax.ShapeDtypeStruct((B, S, aw + iw), F32)]
        out_specs += [pl.BlockSpec((1, ts, aw + iw), lambda b, s: (b, s, 0))]
    const2 = lambda b, s: (0, 0)
    in_specs = [
        pl.BlockSpec((1, ts, D), lambda b, s: (b, s, 0)),
        pl.BlockSpec(wa.shape, const2),
        pl.BlockSpec(ws.shape, const2),
        pl.BlockSpec(wc.shape, const2),
        pl.BlockSpec(wconv.shape, const2),
        pl.BlockSpec((1, CONV_K - 1, cw), lambda b, s: (b, 0, 0)),
    ]
    return pl.pallas_call(
        functools.partial(_proj_kernel, ts=ts, attn_layouts=attn_layouts),
        grid=(B, ns),
        in_specs=in_specs,
        out_specs=tuple(out_specs),
        out_shape=tuple(out_shape),
        scratch_shapes=[pltpu.VMEM((ts + 8, cw), F32)],
        compiler_params=_cparams(("arbitrary", "arbitrary")),
        name="proj_conv",
    )(x, wa, ws, wc, wconv, conv_buf)


def _attn_kernel(qT_ref, qiT_ref, wT_ref, kidx_ref, kb_ref, vT_ref, knorm_ref, tab_ref,
                 o_ref, skey_ref, hi16_ref, lo16_ref, jb_ref, madd_ref, *head_refs,
                 qb, kc, past, l_true, topk, offsets):
    s_refs, p_refs, m_refs, l_refs, acc_refs = (head_refs[g * N_HEADS:(g + 1) * N_HEADS] for g in range(5))
    i = pl.program_id(1)
    qoff = past + i * qb
    adm_end = jnp.minimum(((qoff + qb - 1) // CHUNK + 1) * CHUNK, l_true)
    nck = (adm_end + kc - 1) // kc
    idx_bits = int(l_true).bit_length()
    groups = kc // SUBLANES

    rowi = lax.broadcasted_iota(I32, (kc, qb), 0)
    qpos = qoff + lax.broadcasted_iota(I32, (1, qb), 1)
    kmax = jnp.minimum((qpos // CHUNK + 1) * CHUNK, l_true)

    def score_body(c, carry):
        kidx_c = kidx_ref[0, c]
        sc = jnp.zeros((kc, qb), F32)
        for h in range(N_IDX_HEADS):
            d = _dot(kidx_c, qiT_ref[0, h])
            sc = sc + wT_ref[0, h:h + 1, :] * jnp.maximum(d, 0.0)
        bits = pltpu.bitcast(sc, I32)
        skey = jnp.where(bits < 0, bits ^ jnp.int32(0x7FFFFFFF), bits)
        skey = jnp.where(skey == -1, 0, skey)
        skey = jnp.where(rowi < kmax - c * kc, skey, jnp.int32(INT_MIN))
        skey_ref[c] = skey
        hi16_ref[c] = (skey >> 16).astype(I16)
        return carry

    lax.fori_loop(0, nck, score_body, 0)

    def count(pred_fn):
        def body(c, part):
            ind = jnp.where(pred_fn(c, skey_ref[c]), 1, 0)
            return part + jnp.sum(ind.reshape(groups, SUBLANES, qb), axis=0)
        part = lax.fori_loop(0, nck, body, jnp.zeros((SUBLANES, qb), I32))
        return jnp.sum(part, axis=0, keepdims=True)

    def count16(ref, pred_fn):
        pack = 2 * SUBLANES

        def body(c, part):
            accs = [None] * 4
            for g in range(kc // pack):
                ind = jnp.where(pred_fn(ref[c, g * pack:(g + 1) * pack, :]), jnp.bfloat16(1), jnp.bfloat16(0))
                accs[g % 4] = ind if accs[g % 4] is None else accs[g % 4] + ind
            return part + ((accs[0] + accs[1]) + (accs[2] + accs[3])).astype(F32)
        part = lax.fori_loop(0, nck, body, jnp.zeros((pack, qb), F32))
        return jnp.sum(part, axis=0, keepdims=True).astype(I32)

    def search16(ref, target, count_all):
        def body(it, carry):
            t_u, c_t = carry
            cand_u = t_u | (jnp.int32(1) << (15 - it))
            cand = (cand_u - 2 ** 15).astype(I16)
            cnt = count16(ref, lambda k: k >= cand)
            keep = cnt >= target
            return jnp.where(keep, cand_u, t_u), jnp.where(keep, cnt, c_t)
        return lax.fori_loop(0, 16, body, (jnp.zeros((1, qb), I32), count_all))

    stored = jnp.broadcast_to(nck * kc, (1, qb)).astype(I32)
    hi_u, n_ge_hi = search16(hi16_ref, topk, stored)
    hi_s = (hi_u - 2 ** 15).astype(I16)
    n_gt_hi = count16(hi16_ref, lambda k: k > hi_s)
    rest = topk - n_gt_hi

    def low_body(c, carry):
        low = ((skey_ref[c] & 0xFFFF) - 2 ** 15).astype(I16)
        lo16_ref[c] = jnp.where(hi16_ref[c] == hi_s, low, jnp.int16(-2 ** 15))
        return carry

    lax.fori_loop(0, nck, low_body, 0)
    lo_u, n_ge_lo = search16(lo16_ref, rest, n_ge_hi - n_gt_hi)
    thr = (hi_u - 2 ** 15) * 2 ** 16 + lo_u
    lo_s = (lo_u - 2 ** 15).astype(I16)
    cnt_ge = n_gt_hi + n_ge_lo
    cnt_gt = n_gt_hi + count16(lo16_ref, lambda k: k > lo_s)

    jb_ref[...] = jnp.full(jb_ref.shape, 2 ** 31 - 1, I32)

    @pl.when(jnp.max(cnt_ge) > topk)
    def _():
        need = topk - cnt_gt

        def tie_body(it, jb):
            cand = jb | (jnp.int32(1) << (idx_bits - 1 - it))
            cnt = count(lambda c, sk: jnp.logical_and(sk == thr, c * kc + rowi < cand))
            return jnp.where(cnt <= need, cand, jb)

        jb = lax.fori_loop(0, idx_bits, tie_body, jnp.zeros((1, qb), I32))
        jb_ref[...] = jnp.broadcast_to(jb, jb_ref.shape)

    jbound = jnp.where(thr == jnp.int32(INT_MIN), 0, jb_ref[0:1, :])
    thr_m1 = thr - 1

    for h in range(N_HEADS):
        l_refs[h][...] = jnp.zeros((1, qb), F32)
        acc_refs[h][...] = jnp.zeros((HEAD_DIM, qb), F32)
    n_slabs = kc // SLAB

    def fold(x, op):
        return op(x.reshape(SLAB // SUBLANES, SUBLANES, qb), axis=0)

    def near_bias(chunk_offset, h):
        tiles = []
        for t in range(kc // LANES):
            row = []
            for u in range(qb // LANES):
                d_tu = chunk_offset + (t - u) * LANES
                row.append(tab_ref[offsets.index(d_tu), h] if d_tu in offsets else jnp.zeros((LANES, LANES), F32))
            tiles.append(jnp.concatenate(row, axis=1))
        return jnp.concatenate(tiles, axis=0)

    def attn_chunk(c, near):
        sk = skey_ref[c]
        t_eff = jnp.where(rowi < jbound - c * kc, thr_m1, thr)
        madd_ref[...] = jnp.where(sk > t_eff, 0.0, MASK_VALUE)
        maccs = []
        for h in range(N_HEADS):
            x = _dot(kb_ref[0, c, h // GROUP], qT_ref[0, h]) + madd_ref[...]
            if near is not None:
                x = x + near_bias(near, h)
            s_refs[h][...] = x
            maccs.append(jnp.max(x.reshape(groups, SUBLANES, qb), axis=0))
        alphas = []
        for h in range(N_HEADS):
            s_ref, p_ref, m_ref = s_refs[h], p_refs[h], m_refs[h]
            m_old = m_ref[...]
            m_new = jnp.maximum(m_old, jnp.max(maccs[h], axis=0, keepdims=True))
            for j in range(n_slabs):
                rows = pl.ds(j * SLAB, SLAB)
                p_ref[rows, :] = jnp.exp2(s_ref[rows, :] - m_new).astype(BF16)
            m_ref[...] = m_new
            alphas.append(jnp.exp2(m_old - m_new))
        ones = jnp.ones((2 * SUBLANES, kc), BF16)
        v_aug = [jnp.concatenate([vT_ref[0, c, n * HEAD_DIM:(n + 1) * HEAD_DIM, :], ones], axis=0)
                 for n in range(N_KV_HEADS)]
        for h in range(N_HEADS):
            pv = _dot(v_aug[h // GROUP], p_refs[h][...])
            acc_refs[h][...] = alphas[h] * acc_refs[h][...] + pv[0:HEAD_DIM]
            l_refs[h][...] = alphas[h] * l_refs[h][...] + pv[HEAD_DIM:HEAD_DIM + 1]

    def attn_chunk_fixed(c, near):
        sk = skey_ref[c]
        t_eff = jnp.where(rowi < jbound - c * kc, thr_m1, thr)
        madd_ref[...] = jnp.where(sk > t_eff, 0.0, MASK_VALUE)
        ones = jnp.ones((2 * SUBLANES, kc), BF16)
        v_aug = [jnp.concatenate([vT_ref[0, c, n * HEAD_DIM:(n + 1) * HEAD_DIM, :], ones], axis=0)
                 for n in range(N_KV_HEADS)]
        for h in range(N_HEADS):
            x = _dot(kb_ref[0, c, h // GROUP], qT_ref[0, h]) + madd_ref[...]
            if near is not None:
                x = x + near_bias(near, h)
            p_refs[h][...] = jnp.exp2(x).astype(BF16)
        for h in range(N_HEADS):
            pv = _dot(v_aug[h // GROUP], p_refs[h][...])
            acc_refs[h][...] += pv[0:HEAD_DIM]
            l_refs[h][...] += pv[HEAD_DIM:HEAD_DIM + 1]

    def run(chunk_fn):
        c_diag = qoff // kc
        lax.fori_loop(0, c_diag - 1, lambda c, carry: (chunk_fn(c, None), carry)[1], 0)

        @pl.when(c_diag >= 1)
        def _():
            chunk_fn(c_diag - 1, -kc)

        chunk_fn(c_diag, 0)

    q_sq = jnp.zeros((1, qb), F32)
    for h in range(N_HEADS):
        qf = qT_ref[0, h].astype(F32)
        q_sq = jnp.maximum(q_sq, jnp.sum(qf * qf, axis=0, keepdims=True))
    bound = jnp.sqrt(jnp.max(q_sq) * jnp.max(knorm_ref[0])) + jnp.max(jnp.abs(tab_ref[...]))
    fixed_shift = bound < EXP2_SAFE

    @pl.when(fixed_shift)
    def _():
        run(attn_chunk_fixed)

    @pl.when(jnp.logical_not(fixed_shift))
    def _():
        for h in range(N_HEADS):
            m_refs[h][...] = jnp.full((1, qb), MASK_VALUE, F32)
        run(attn_chunk)

    oT = jnp.concatenate([acc_refs[h][...] / l_refs[h][...] for h in range(N_HEADS)], axis=0)
    o_ref[0] = oT.T.astype(o_ref.dtype)


def _attend(qT, qiT, wT, kidxb, kb, vT, knorm, tabs, *, qb, past, l_true, offsets):
    B, _, _, S = qT.shape
    nc, kc = vT.shape[1], vT.shape[3]
    assert S % qb == 0 and qb % LANES == 0 and kc % LANES == 0
    assert past % kc == 0 and qb <= kc and (qb == kc or S == qb)
    topk = min(TOPK_MAX, l_true // 4)
    kern = functools.partial(_attn_kernel, qb=qb, kc=kc, past=past, l_true=l_true,
                             topk=topk, offsets=offsets)
    in_specs = [
        pl.BlockSpec((1, N_HEADS, HEAD_DIM, qb), lambda b, i: (b, 0, 0, i)),
        pl.BlockSpec((1, N_IDX_HEADS, IDX_DIM, qb), lambda b, i: (b, 0, 0, i)),
        pl.BlockSpec((1, N_IDX_HEADS, qb), lambda b, i: (b, 0, i)),
        pl.BlockSpec((1,) + kidxb.shape[1:], lambda b, i: (b, 0, 0, 0)),
        pl.BlockSpec((1,) + kb.shape[1:], lambda b, i: (b, 0, 0, 0, 0)),
        pl.BlockSpec((1,) + vT.shape[1:], lambda b, i: (b, 0, 0, 0)),
        pl.BlockSpec((1,) + knorm.shape[1:], lambda b, i: (b, 0, 0, 0)),
        pl.BlockSpec(tabs.shape, lambda b, i: (0, 0, 0, 0)),
    ]
    return pl.pallas_call(
        kern,
        grid=(B, S // qb),
        in_specs=in_specs,
        out_specs=pl.BlockSpec((1, qb, N_HEADS * HEAD_DIM), lambda b, i: (b, i, 0)),
        out_shape=jax.ShapeDtypeStruct((B, S, N_HEADS * HEAD_DIM), BF16),
        scratch_shapes=[
            pltpu.VMEM((nc, kc, qb), I32),
            pltpu.VMEM((nc, kc, qb), I16),
            pltpu.VMEM((nc, kc, qb), I16),
            pltpu.VMEM((SUBLANES, qb), I32),
            pltpu.VMEM((kc, qb), F32),
        ] + [pltpu.VMEM((kc, qb), F32)] * N_HEADS
          + [pltpu.VMEM((kc, qb), BF16)] * N_HEADS
          + [pltpu.VMEM((1, qb), F32)] * N_HEADS
          + [pltpu.VMEM((1, qb), F32)] * N_HEADS
          + [pltpu.VMEM((HEAD_DIM, qb), F32)] * N_HEADS,
        compiler_params=_cparams(("arbitrary", "arbitrary")),
        name="dsa_attend",
    )(qT, qiT, wT, kidxb, kb, vT, knorm, tabs)


def _mix_kernel(x_ref, attn_ref, conv_ref, wo_ref, g_ref, b_ref, h_ref, *, dn_alpha):
    cat = jnp.concatenate([attn_ref[...], conv_ref[...]], axis=1)
    mixed = _dot(cat, wo_ref[...])
    h_ref[...] = _layer_norm(dn_alpha * x_ref[...] + mixed, g_ref[...], b_ref[...])


def _mix(x2, attn2, conv2, wo, g, b, *, dn_alpha):
    T, D = x2.shape
    tt = min(1024, T)
    assert T % tt == 0
    aw, cw = attn2.shape[1], conv2.shape[1]
    row = lambda i: (i, 0)
    const = lambda i: (0, 0)
    return pl.pallas_call(
        functools.partial(_mix_kernel, dn_alpha=dn_alpha),
        grid=(T // tt,),
        in_specs=[pl.BlockSpec((tt, D), row), pl.BlockSpec((tt, aw), row), pl.BlockSpec((tt, cw), row),
                  pl.BlockSpec(wo.shape, const), pl.BlockSpec((1, D), const), pl.BlockSpec((1, D), const)],
        out_specs=pl.BlockSpec((tt, D), row),
        out_shape=jax.ShapeDtypeStruct((T, D), F32),
        compiler_params=_cparams(("arbitrary",)),
        name="outproj_ln1",
    )(x2, attn2, conv2, wo, g, b)


def _route_picks(logits):
    lane = lax.broadcasted_iota(I32, logits.shape, 1)
    work = logits
    vals, hots = [], []
    for _ in range(TOP_K):
        m = jnp.max(work, axis=1, keepdims=True)
        idx = jnp.min(jnp.where(work == m, lane, LANES), axis=1, keepdims=True)
        hot = lane == idx
        vals.append(m)
        hots.append(hot)
        work = jnp.where(hot, -jnp.inf, work)
    es = [jnp.exp(v - vals[0]) for v in vals]
    den = es[0]
    for e in es[1:]:
        den = den + e
    return hots, [e / den for e in es]


def _route(logits):
    hots, gates = _route_picks(logits)
    comb = jnp.zeros(logits.shape, F32)
    for gate, hot in zip(gates, hots):
        comb = jnp.where(hot, gate, comb)
    return comb


def _expert_mlp(xb, wgu_ref, bgu_ref, wd_ref, bd_ref):
    ff = wd_ref.shape[1]
    gu = _dot(xb, wgu_ref[0]) + bgu_ref[0]
    gate = jnp.minimum(gu[:, 0:ff], SWIGLU_LIMIT)
    up = jnp.clip(gu[:, ff:2 * ff], -SWIGLU_LIMIT, SWIGLU_LIMIT)
    act = (up + 1.0) * (gate * (1.0 / (1.0 + jnp.exp(-SWIGLU_ALPHA * gate))))
    return _dot(act.astype(BF16), wd_ref[0]) + bd_ref[0]


def _moe_kernel(h_ref, wr_ref, br_ref, wgu_ref, bgu_ref, wd_ref, bd_ref, g_ref, b_ref,
                y_ref, hb_ref, comb_ref, *, dn_alpha):
    e = pl.program_id(1)
    tt = h_ref.shape[0]

    @pl.when(e == 0)
    def _():
        hb = h_ref[...].astype(BF16)
        hb_ref[...] = hb
        comb_ref[...] = _route(_dot(hb, wr_ref[...]) + br_ref[...])
        y_ref[...] = jnp.zeros(y_ref.shape, F32)

    y = _expert_mlp(hb_ref[...], wgu_ref, bgu_ref, wd_ref, bd_ref)
    lane = lax.broadcasted_iota(I32, (tt, LANES), 1)
    y_ref[...] += jnp.sum(jnp.where(lane == e, comb_ref[...], 0.0), axis=1, keepdims=True) * y

    @pl.when(e == pl.num_programs(1) - 1)
    def _():
        y_ref[...] = _layer_norm(dn_alpha * h_ref[...] + y_ref[...], g_ref[...], b_ref[...])


def _moe(h2, wr, br, wgu, bgu, wd, bd, g, b, *, dn_alpha):
    T, D = h2.shape
    E, _, F2 = wgu.shape
    tt = min(512, T)
    assert T % tt == 0
    row = lambda i, e: (i, 0)
    const = lambda i, e: (0, 0)
    exp3 = lambda i, e: (e, 0, 0)
    return pl.pallas_call(
        functools.partial(_moe_kernel, dn_alpha=dn_alpha),
        grid=(T // tt, E),
        in_specs=[pl.BlockSpec((tt, D), row), pl.BlockSpec(wr.shape, const), pl.BlockSpec(br.shape, const),
                  pl.BlockSpec((1, D, F2), exp3), pl.BlockSpec((1, 1, F2), exp3),
                  pl.BlockSpec((1, F2 // 2, D), exp3), pl.BlockSpec((1, 1, D), exp3),
                  pl.BlockSpec((1, D), const), pl.BlockSpec((1, D), const)],
        out_specs=pl.BlockSpec((tt, D), row),
        out_shape=jax.ShapeDtypeStruct((T, D), F32),
        scratch_shapes=[pltpu.VMEM((tt, D), BF16), pltpu.VMEM((tt, LANES), F32)],
        compiler_params=_cparams(("arbitrary", "arbitrary")),
        name="moe_ln2",
    )(h2, wr, br, wgu, bgu, wd, bd, g, b)


def _pack_pairs(x):
    w = x.shape[1] // 2
    bits = pltpu.bitcast(x.astype(BF16).astype(F32), I32)
    return (bits[:, 0:w] & jnp.int32(-65536)) | lax.shift_right_logical(bits[:, w:2 * w], 16)


def _unpack_pairs(p):
    hi = pltpu.bitcast(p & jnp.int32(-65536), F32)
    lo = pltpu.bitcast(lax.shift_left(p, 16), F32)
    return jnp.concatenate([hi, lo], axis=1)


def _mix_route_kernel(x_ref, attn_ref, conv_ref, wo_ref, g_ref, b_ref, wrT_ref, brT_ref,
                      h_ref, hpk_ref, infoT_ref, gcol_ref, cnt_ref, *, dn_alpha):
    cat = jnp.concatenate([attn_ref[...], conv_ref[...]], axis=1)
    h = _layer_norm(dn_alpha * x_ref[...] + _dot(cat, wo_ref[...]), g_ref[...], b_ref[...])
    h_ref[...] = h
    tt = h.shape[0]
    ne = cnt_ref.shape[1]
    hpk_ref[...] = _pack_pairs(h)
    logits = lax.dot_general(wrT_ref[...], h.astype(BF16), (((1,), (1,)), ((), ())),
                             preferred_element_type=F32)
    work = logits[0:ne] + jnp.tile(brT_ref[0:ne, :], (1, tt // LANES))
    sub = lax.broadcasted_iota(I32, (ne, tt), 0)
    vals, idxs, hots = [], [], []
    for _ in range(TOP_K):
        m = jnp.max(work, axis=0, keepdims=True)
        idx = jnp.min(jnp.where(work == m, sub, ne), axis=0, keepdims=True)
        hot = sub == idx
        vals.append(m)
        idxs.append(idx)
        hots.append(hot)
        work = jnp.where(hot, -jnp.inf, work)
    es = [jnp.exp(v - vals[0]) for v in vals]
    den = es[0]
    for e in es[1:]:
        den = den + e
    gates = [e / den for e in es]
    picked = jnp.zeros((ne, tt), F32)
    for hot in hots:
        picked = jnp.where(hot, 1.0, picked)
    earlier = lax.broadcasted_iota(I32, (tt, tt), 0) < lax.broadcasted_iota(I32, (tt, tt), 1)
    rank = _dot(picked.astype(BF16), jnp.where(earlier, 1.0, 0.0).astype(BF16))
    ranks = [jnp.sum(jnp.where(hot, rank, 0.0), axis=0, keepdims=True) for hot in hots]
    pad = jnp.zeros((2 * SUBLANES - 3 * TOP_K, tt), F32)
    infoT_ref[...] = jnp.concatenate([i.astype(F32) for i in idxs] + gates + ranks + [pad], axis=0)
    gcol_ref[...] = jnp.concatenate(gates + [jnp.zeros((LANES - TOP_K, tt), F32)], axis=0).T
    cnt_ref[0] = jnp.broadcast_to(jnp.sum(picked, axis=1, keepdims=True), (ne, LANES))


def _mix_route(x2, attn2, conv2, wo, g, b, wr, br, ne, *, dn_alpha):
    T, D = x2.shape
    tt = ROUTE_TILE
    aw, cw = attn2.shape[1], conv2.shape[1]
    row = lambda i: (i, 0)
    const = lambda i: (0, 0)
    wrT = wr.T
    brT = jnp.broadcast_to(br.reshape(LANES, 1), (LANES, LANES))
    return pl.pallas_call(
        functools.partial(_mix_route_kernel, dn_alpha=dn_alpha),
        grid=(T // tt,),
        in_specs=[pl.BlockSpec((tt, D), row), pl.BlockSpec((tt, aw), row), pl.BlockSpec((tt, cw), row),
                  pl.BlockSpec(wo.shape, const), pl.BlockSpec((1, D), const), pl.BlockSpec((1, D), const),
                  pl.BlockSpec(wrT.shape, const), pl.BlockSpec(brT.shape, const)],
        out_specs=(pl.BlockSpec((tt, D), row), pl.BlockSpec((tt, D // 2), row),
                   pl.BlockSpec((2 * SUBLANES, tt), lambda i: (0, i)),
                   pl.BlockSpec((tt, LANES), row), pl.BlockSpec((1, ne, LANES), lambda i: (i, 0, 0))),
        out_shape=(jax.ShapeDtypeStruct((T, D), F32), jax.ShapeDtypeStruct((T, D // 2), I32),
                   jax.ShapeDtypeStruct((2 * SUBLANES, T), F32),
                   jax.ShapeDtypeStruct((T, LANES), F32), jax.ShapeDtypeStruct((T // tt, ne, LANES), F32)),
        compiler_params=_cparams(("arbitrary",)),
        name="outproj_ln1_route",
    )(x2, attn2, conv2, wo, g, b, wrT, brT)


def _sc_gather_rows(table, idx):
    M, W = idx.shape[0], table.shape[1]
    n_workers = SC_CORES * SC_SUBCORES
    per_w = M // n_workers
    n_ch = per_w // SC_ROWS
    assert M == n_workers * n_ch * SC_ROWS
    mesh = plsc.VectorSubcoreMesh(core_axis_name="c", subcore_axis_name="s")

    @functools.partial(
        pl.kernel, mesh=mesh,
        out_type=jax.ShapeDtypeStruct((M, W), table.dtype),
        scratch_types=[pltpu.VMEM((n_ch, SC_ROWS), I32), pltpu.VMEM((SC_ROWS, W), table.dtype),
                       pltpu.SemaphoreType.DMA],
    )
    def gather(table_hbm, idx_hbm, out_hbm, idx_v, rows_v, sem):
        wid = lax.axis_index("s") * SC_CORES + lax.axis_index("c")
        pltpu.sync_copy(idx_hbm.at[wid], idx_v)

        @pl.loop(0, n_ch)
        def _(j):
            pltpu.async_copy(table_hbm.at[idx_v.at[j]], rows_v, sem).wait()
            pltpu.sync_copy(rows_v, out_hbm.at[pl.ds(wid * per_w + j * SC_ROWS, SC_ROWS)])

    return gather(table, idx.reshape(n_workers, n_ch, SC_ROWS))


def _expert_rows_kernel(te_ref, used_ref, x_ref, wgu_ref, bgu_ref, wd_ref, bd_ref, y_ref):
    @pl.when(pl.program_id(0) < used_ref[0])
    def _():
        xg = _unpack_pairs(x_ref[...]).astype(BF16)
        y_ref[...] = _pack_pairs(_expert_mlp(xg, wgu_ref, bgu_ref, wd_ref, bd_ref))


def _expert_rows(tile_expert, used_tiles, xs, wgu, bgu, wd, bd):
    R, W = xs.shape
    E, D, F2 = wgu.shape
    exp3 = lambda j, te, used: (te[j], 0, 0)
    rows = lambda j, te, used: (j, 0)
    grid_spec = pltpu.PrefetchScalarGridSpec(
        num_scalar_prefetch=2,
        grid=(R // ROW_TILE,),
        in_specs=[pl.BlockSpec((ROW_TILE, W), rows),
                  pl.BlockSpec((1, D, F2), exp3), pl.BlockSpec((1, 1, F2), exp3),
                  pl.BlockSpec((1, F2 // 2, D), exp3), pl.BlockSpec((1, 1, D), exp3)],
        out_specs=pl.BlockSpec((ROW_TILE, W), rows),
    )
    return pl.pallas_call(
        _expert_rows_kernel,
        grid_spec=grid_spec,
        out_shape=jax.ShapeDtypeStruct((R, W), I32),
        compiler_params=_cparams(("arbitrary",)),
        name="moe_expert_rows",
    )(tile_expert, used_tiles, xs, wgu, bgu, wd, bd)


def _combine_kernel(h_ref, yg_ref, gcol_ref, g_ref, b_ref, o_ref, *, dn_alpha):
    acc = dn_alpha * h_ref[...]
    for k in range(TOP_K):
        acc = acc + gcol_ref[:, k:k + 1] * _unpack_pairs(yg_ref[k])
    o_ref[...] = _layer_norm(acc, g_ref[...], b_ref[...])


def _combine(h2, yg, gcol, g, b, *, dn_alpha):
    T, D = h2.shape
    tt = ROUTE_TILE
    row = lambda i: (i, 0)
    const = lambda i: (0, 0)
    return pl.pallas_call(
        functools.partial(_combine_kernel, dn_alpha=dn_alpha),
        grid=(T // tt,),
        in_specs=[pl.BlockSpec((tt, D), row), pl.BlockSpec((TOP_K, tt, D // 2), lambda i: (0, i, 0)),
                  pl.BlockSpec((tt, LANES), row),
                  pl.BlockSpec((1, D), const), pl.BlockSpec((1, D), const)],
        out_specs=pl.BlockSpec((tt, D), row),
        out_shape=jax.ShapeDtypeStruct((T, D), F32),
        compiler_params=_cparams(("arbitrary",)),
        name="moe_combine_ln2",
    )(h2, yg, gcol, g, b)


def _dest_kernel(infoT_ref, base_ref, dest_ref):
    tt = infoT_ref.shape[1]
    ne = base_ref.shape[1]
    base = jnp.tile(base_ref[0], (1, tt // LANES))
    sub = lax.broadcasted_iota(I32, (ne, tt), 0)
    rows = []
    for k in range(TOP_K):
        hot = sub == infoT_ref[k:k + 1, :].astype(I32)
        rows.append(jnp.sum(jnp.where(hot, base, 0.0), axis=0, keepdims=True)
                    + infoT_ref[2 * TOP_K + k:2 * TOP_K + k + 1, :])
    rows.append(jnp.zeros((SUBLANES - TOP_K, tt), F32))
    dest_ref[...] = jnp.concatenate(rows, axis=0).astype(I32)


def _dest_rows(infoT, base):
    T = infoT.shape[1]
    tt = ROUTE_TILE
    ne = base.shape[1]
    return pl.pallas_call(
        _dest_kernel,
        grid=(T // tt,),
        in_specs=[pl.BlockSpec((2 * SUBLANES, tt), lambda i: (0, i)), pl.BlockSpec((1, ne, LANES), lambda i: (i, 0, 0))],
        out_specs=pl.BlockSpec((SUBLANES, tt), lambda i: (0, i)),
        out_shape=jax.ShapeDtypeStruct((SUBLANES, T), I32),
        compiler_params=_cparams(("arbitrary",)),
        name="moe_dest_rows",
    )(infoT, base)


def _sc_scatter_rows(src, dest_km, n_rows):
    T, W = src.shape
    K = dest_km.shape[0]
    n_workers = SC_CORES * SC_SUBCORES
    per_w = T // n_workers
    n_ch = per_w // SC_ROWS
    assert T == n_workers * n_ch * SC_ROWS
    mesh = plsc.VectorSubcoreMesh(core_axis_name="c", subcore_axis_name="s")
    idx = dest_km.reshape(K, n_workers, n_ch, SC_ROWS).transpose(1, 0, 2, 3).reshape(n_workers, K * n_ch, SC_ROWS)

    @functools.partial(
        pl.kernel, mesh=mesh,
        out_type=jax.ShapeDtypeStruct((n_rows, W), src.dtype),
        scratch_types=[pltpu.VMEM((K * n_ch, SC_ROWS), I32), pltpu.VMEM((SC_ROWS, W), src.dtype),
                       pltpu.SemaphoreType.DMA],
    )
    def scatter(src_hbm, idx_hbm, out_hbm, idx_v, rows_v, sem):
        wid = lax.axis_index("s") * SC_CORES + lax.axis_index("c")
        pltpu.sync_copy(idx_hbm.at[wid], idx_v)

        @pl.loop(0, n_ch)
        def _(j):
            pltpu.sync_copy(src_hbm.at[pl.ds(wid * per_w + j * SC_ROWS, SC_ROWS)], rows_v)
            for k in range(K):
                pltpu.async_copy(rows_v, out_hbm.at[idx_v.at[k * n_ch + j]], sem).wait()

    return scatter(src, idx)


def _moe_sorted(h2, routed, wgu, bgu, wd, bd, g, b, *, dn_alpha):
    T, D = h2.shape
    E = wgu.shape[0]
    hpk, infoT, gcol, cnt = routed
    cnt = cnt[:, :, 0].astype(I32)
    total = jnp.sum(cnt, axis=0)
    padded = -(-total // ROW_TILE) * ROW_TILE
    ends = jnp.cumsum(padded)
    base = (ends - padded)[None, :] + jnp.cumsum(cnt, axis=0) - cnt
    base = jnp.broadcast_to(base.astype(F32)[:, :, None], cnt.shape + (LANES,))
    dest_km = _dest_rows(infoT, base)[0:TOP_K]
    R = T * TOP_K + E * ROW_TILE
    tile_start = jnp.arange(R // ROW_TILE, dtype=I32) * ROW_TILE
    tile_expert = jnp.minimum(jnp.sum((ends[None, :] <= tile_start[:, None]).astype(I32), axis=1), E - 1)
    used_tiles = (ends[E - 1] // ROW_TILE).reshape(1)
    xs = _sc_scatter_rows(hpk, dest_km, R)
    ys = _expert_rows(tile_expert, used_tiles, xs, wgu, bgu, wd, bd)
    yg = _sc_gather_rows(ys, dest_km.reshape(-1))
    return _combine(h2, yg.reshape(TOP_K, T, D // 2), gcol, g, b, dn_alpha=dn_alpha)


def _split_gu_kernel(w_ref, perm_ref, o_ref, *, band):
    n = w_ref.shape[2]
    for m in range(n // band):
        both = _dot(w_ref[0, :, m * band:(m + 1) * band].astype(BF16), perm_ref[...]).astype(BF16)
        o_ref[0, :, m * (band // 2):(m + 1) * (band // 2)] = both[:, 0:band // 2]
        o_ref[0, :, n // 2 + m * (band // 2):n // 2 + (m + 1) * (band // 2)] = both[:, band // 2:band]


def _split_gu(w_gu):
    E, D, N = w_gu.shape
    tr = min(1024, D)
    band = min(512, N)
    j = jnp.arange(band, dtype=I32)
    src = jnp.where(j < band // 2, 2 * j, 2 * (j - band // 2) + 1)
    perm = (jnp.arange(band, dtype=I32)[:, None] == src[None, :]).astype(BF16)
    return pl.pallas_call(
        functools.partial(_split_gu_kernel, band=band),
        grid=(E, D // tr),
        in_specs=[pl.BlockSpec((1, tr, N), lambda e, r: (e, r, 0)),
                  pl.BlockSpec((band, band), lambda e, r: (0, 0))],
        out_specs=pl.BlockSpec((1, tr, N), lambda e, r: (e, r, 0)),
        out_shape=jax.ShapeDtypeStruct((E, D, N), BF16),
        compiler_params=_cparams(("arbitrary", "arbitrary")),
        name="split_gate_up",
    )(w_gu, perm)


def _prep_weights(w_in, w_conv, w_out, ln1_g, ln1_b, w_router, b_router, w_gu, b_gu, w_down, b_down,
                  ln2_g, ln2_b):
    aw = N_HEADS * HEAD_DIM
    kv = N_KV_HEADS * HEAD_DIM
    iw = N_IDX_HEADS * IDX_DIM
    D = w_in.shape[0]
    cw = D - aw
    o_q, o_k, o_v = 0, aw, aw + kv
    o_qi = o_v + kv
    o_ki = o_qi + iw
    o_wi = o_ki + IDX_DIM
    o_c = o_wi + N_IDX_HEADS
    wa = jnp.concatenate([w_in[:, o_q:o_k], w_in[:, o_qi:o_ki], w_in[:, o_k:o_v], w_in[:, o_v:o_qi]],
                         axis=1).astype(BF16)
    ws = jnp.pad(w_in[:, o_ki:o_c], ((0, 0), (0, LANES - IDX_DIM - N_IDX_HEADS))).astype(BF16)
    wc = w_in[:, o_c:o_c + 3 * cw].astype(BF16)
    E = w_router.shape[1]
    wr = jnp.pad(w_router, ((0, 0), (0, LANES - E))).astype(BF16)
    br = jnp.pad(b_router, (0, LANES - E), constant_values=MASK_VALUE).reshape(1, LANES)
    F = w_gu.shape[2] // 2
    return dict(
        wa=wa, ws=ws, wc=wc, wconv=w_conv, wo=w_out.astype(BF16),
        ln1_g=ln1_g.reshape(1, D), ln1_b=ln1_b.reshape(1, D),
        wr=wr, br=br,
        wgu=_split_gu(w_gu),
        bgu=jnp.concatenate([b_gu[:, 0::2], b_gu[:, 1::2]], axis=1).reshape(E, 1, 2 * F),
        wd=w_down.astype(BF16), bd=b_down.reshape(E, 1, D),
        ln2_g=ln2_g.reshape(1, D), ln2_b=ln2_b.reshape(1, D),
    )


def _decode_layouts(pq, small, k_all, v_all, kidx_all, kc, qb):
    B, S, _ = pq.shape
    aw = N_HEADS * HEAD_DIM
    L, kv = k_all.shape[1], k_all.shape[2]
    lp = -(-L // kc) * kc
    nc = lp // kc
    padk = lambda a: jnp.pad(a, ((0, 0), (0, lp - L), (0, 0)))
    padq = lambda a: jnp.pad(a, ((0, 0), (0, 0), (0, 0), (0, qb - S)))
    q = pq[:, :, 0:aw] * QK_SCALE
    qT = padq(q.reshape(B, S, N_HEADS, HEAD_DIM).transpose(0, 2, 3, 1)).astype(BF16)
    qiT = padq(pq[:, :, aw:].reshape(B, S, N_IDX_HEADS, IDX_DIM).transpose(0, 2, 3, 1)).astype(BF16)
    wT = jnp.pad(small[:, :, IDX_DIM:IDX_DIM + N_IDX_HEADS].transpose(0, 2, 1), ((0, 0), (0, 0), (0, qb - S)))
    kb = padk(k_all).reshape(B, nc, kc, N_KV_HEADS, HEAD_DIM).transpose(0, 1, 3, 2, 4).astype(BF16)
    vT = padk(v_all).reshape(B, nc, kc, kv).transpose(0, 1, 3, 2).astype(BF16)
    kidxb = padk(kidx_all).reshape(B, nc, kc, IDX_DIM).astype(BF16)
    knorm = jnp.sum(kb.astype(F32) ** 2, axis=-1)
    return qT, qiT, wT, kidxb, kb, vT, knorm


def _layer(x, past_k, past_v, past_kidx, conv_buf, rel_bias, w, *, dn_alpha):
    B, S, D = x.shape
    prefill = past_k is None
    outs = _project(x, conv_buf, w["wa"], w["ws"], w["wc"], w["wconv"], attn_layouts=prefill)
    k, v, small, conv, new_buf = outs[:5]
    k_idx = small if prefill else small[:, :, 0:IDX_DIM]
    if prefill:
        past, l_true = 0, S
        qb = min(QUERY_BLOCK, S)
        kb, vT, kidxb, qT, qiT, wT, knorm = outs[5:]
    else:
        past = past_k.shape[1]
        l_true = past + S
        qb = -(-S // LANES) * LANES
        qT, qiT, wT, kidxb, kb, vT, knorm = _decode_layouts(
            outs[5], small, jnp.concatenate([past_k, k], axis=1), jnp.concatenate([past_v, v], axis=1),
            jnp.concatenate([past_kidx, k_idx], axis=1), math.gcd(past, KEY_CHUNK), qb)
    offsets = (-LANES, 0)
    tabs = _bias_tables(rel_bias, offsets)
    attn = _attend(qT, qiT, wT, kidxb, kb, vT, knorm, tabs, qb=qb, past=past, l_true=l_true, offsets=offsets)
    attn = attn[:, :S]
    mix_args = (x.reshape(B * S, D), attn.reshape(B * S, -1), conv.reshape(B * S, -1), w["wo"],
                w["ln1_g"], w["ln1_b"])
    experts = (w["wgu"], w["bgu"], w["wd"], w["bd"], w["ln2_g"], w["ln2_b"])
    sc_unit = SC_CORES * SC_SUBCORES * SC_ROWS
    sortable = ((B * S) % ROUTE_TILE == 0 and (B * S) % sc_unit == 0
                and (N_EXPERTS * ROW_TILE) % sc_unit == 0)
    if sortable and B * S >= SORTED_MIN_TOKENS:
        h, *routed = _mix_route(*mix_args, w["wr"], w["br"], N_EXPERTS, dn_alpha=dn_alpha)
        y = _moe_sorted(h, routed, *experts, dn_alpha=dn_alpha)
    else:
        h = _mix(*mix_args, dn_alpha=dn_alpha)
        y = _moe(h, w["wr"], w["br"], *experts, dn_alpha=dn_alpha)
    return (y.reshape(B, S, D), k.reshape(B, S, N_KV_HEADS, HEAD_DIM), v.reshape(B, S, N_KV_HEADS, HEAD_DIM),
            k_idx, new_buf)


def kernel(x_prompt, x_sample, cache_k, cache_v, cache_kidx, state_conv, rel_bias, w_in, w_conv, w_out,
           ln1_g, ln1_b, w_router, b_router, w_gu, b_gu, w_down, b_down, ln2_g, ln2_b):
    depth = w_in.shape[0]
    assert depth == 1
    dn_alpha = (2 * depth) ** 0.25
    kv = N_KV_HEADS * HEAD_DIM
    w = _prep_weights(w_in[0], w_conv[0], w_out[0], ln1_g[0], ln1_b[0], w_router[0], b_router[0],
                      w_gu[0], b_gu[0], w_down[0], b_down[0], ln2_g[0], ln2_b[0])
    Bp = x_prompt.shape[0]
    cw = w_conv.shape[2]
    zero_buf = jnp.zeros((Bp, CONV_K - 1, cw), F32)
    yp, k1, v1, i1, c1 = _layer(x_prompt, None, None, None, zero_buf, rel_bias, w, dn_alpha=dn_alpha)
    Bs, P = cache_k.shape[1], cache_k.shape[2]
    ys, k2, v2, i2, c2 = _layer(x_sample, cache_k[0].reshape(Bs, P, kv), cache_v[0].reshape(Bs, P, kv),
                                cache_kidx[0], state_conv[0], rel_bias, w, dn_alpha=dn_alpha)
    return (yp, ys, k1[None], v1[None], i1[None], c1[None], k2[None], v2[None], i2[None], c2[None])
```

```python
import functools
import math

import jax
import jax.numpy as jnp
from jax import lax
from jax.experimental import pallas as pl
from jax.experimental.pallas import tpu as pltpu
from jax.experimental.pallas import tpu_sc as plsc

F32 = jnp.float32
BF16 = jnp.bfloat16
I32 = jnp.int32
I16 = jnp.int16

CHUNK = 64
N_HEADS = 8
HEAD_DIM = 64
N_KV_HEADS = 2
GROUP = N_HEADS // N_KV_HEADS
N_IDX_HEADS = 8
IDX_DIM = 32
TOPK_MAX = 256
CONV_K = 3
N_BUCKETS = 32
MAX_DISTANCE = 128
N_EXPERTS = 32
TOP_K = 4
SWIGLU_LIMIT = 7.0
SWIGLU_ALPHA = 1.702
LN_EPS = 1e-5
MASK_VALUE = -1e30
QK_SCALE = HEAD_DIM ** -0.5 * math.log2(math.e)
EXP2_SAFE = 96.0

LANES = 128
SUBLANES = 8
KEY_CHUNK = 512
QUERY_BLOCK = 512
SLAB = 64
ROW_TILE = 1024
ROUTE_TILE = 1024
SC_CORES = 2
SC_SUBCORES = 16
SC_ROWS = 128
SORTED_MIN_TOKENS = 2048
INT_MIN = -(2 ** 31)
VMEM_LIMIT = 56 * 1024 * 1024


def _cparams(sem):
    return pltpu.CompilerParams(dimension_semantics=sem, vmem_limit_bytes=VMEM_LIMIT)


def _dot(a, b):
    return jnp.dot(a, b, preferred_element_type=F32)


def _layer_norm(z, g, b):
    mu = jnp.mean(z, axis=-1, keepdims=True)
    zc = z - mu
    var = jnp.mean(zc * zc, axis=-1, keepdims=True)
    return zc * lax.rsqrt(var + LN_EPS) * g + b


def _bucket_thresholds():
    nb = N_BUCKETS // 2
    max_exact = nb // 2
    out = []
    for j in range(1, nb - max_exact):
        out.append(math.ceil(max_exact * (MAX_DISTANCE / max_exact) ** (j / (nb - max_exact)) - 1e-9))
    return tuple(out)


def _bias_table_kernel(rel_ref, tab_ref, *, offsets):
    nb = N_BUCKETS // 2
    max_exact = nb // 2
    thr = _bucket_thresholds()
    ii = lax.broadcasted_iota(I32, (LANES, LANES), 0)
    jj = lax.broadcasted_iota(I32, (LANES, LANES), 1)
    for d, off in enumerate(offsets):
        rel = off + ii - jj
        n = jnp.abs(rel)
        large = jnp.full((LANES, LANES), max_exact, I32)
        for t in thr:
            large = large + jnp.where(n >= t, 1, 0)
        bucket = jnp.where(rel > 0, nb, 0) + jnp.where(n < max_exact, n, large)
        for h in range(N_HEADS):
            acc = jnp.zeros((LANES, LANES), F32)
            for b in range(N_BUCKETS):
                acc = jnp.where(bucket == b, rel_ref[b, h], acc)
            tab_ref[d, h] = (acc - rel_ref[nb - 1, h]) * math.log2(math.e)


def _bias_tables(rel_bias, offsets):
    return pl.pallas_call(
        functools.partial(_bias_table_kernel, offsets=offsets),
        out_shape=jax.ShapeDtypeStruct((len(offsets), N_HEADS, LANES, LANES), F32),
        in_specs=[pl.BlockSpec(memory_space=pltpu.SMEM)],
        out_specs=pl.BlockSpec(memory_space=pltpu.VMEM),
        name="bias_tables",
    )(rel_bias)


def _proj_kernel(x_ref, wa_ref, ws_ref, wc_ref, wconv_ref, buf_ref,
                 k_ref, v_ref, small_ref, conv_ref, nbuf_ref, *rest, ts, attn_layouts):
    s = pl.program_id(1)
    xb = x_ref[0].astype(BF16)
    pa = _dot(xb, wa_ref[...])
    ps = _dot(xb, ws_ref[...])
    pc = _dot(xb, wc_ref[...])
    aw = N_HEADS * HEAD_DIM
    iw = N_IDX_HEADS * IDX_DIM
    kv = N_KV_HEADS * HEAD_DIM
    k = pa[:, aw + iw:aw + iw + kv]
    v = pa[:, aw + iw + kv:aw + iw + 2 * kv]
    k_ref[0] = k
    v_ref[0] = v
    small_ref[0] = ps[:, 0:small_ref.shape[2]]
    scale = QK_SCALE
    if attn_layouts:
        kb_ref, vT_ref, kidxb_ref, qT_ref, qiT_ref, wT_ref, knorm_ref, carry_ref = rest
        for n in range(N_KV_HEADS):
            kb_ref[0, 0, n] = k[:, n * HEAD_DIM:(n + 1) * HEAD_DIM].astype(BF16)
        kTf = k.astype(BF16).astype(F32).T
        knorm_ref[0, 0] = jnp.concatenate(
            [jnp.sum(kTf[n * HEAD_DIM:(n + 1) * HEAD_DIM] ** 2, axis=0, keepdims=True) for n in range(N_KV_HEADS)],
            axis=0)
        vT_ref[0, 0] = v.T.astype(BF16)
        kidxb_ref[0, 0] = ps[:, 0:IDX_DIM].astype(BF16)
        qT = (pa[:, 0:aw] * scale).T.astype(BF16)
        for h in range(N_HEADS):
            qT_ref[0, h] = qT[h * HEAD_DIM:(h + 1) * HEAD_DIM]
        qiT = pa[:, aw:aw + iw].T.astype(BF16)
        for h in range(N_IDX_HEADS):
            qiT_ref[0, h] = qiT[h * IDX_DIM:(h + 1) * IDX_DIM]
        wT_ref[0] = ps.T[IDX_DIM:IDX_DIM + N_IDX_HEADS]
    else:
        pq_ref, carry_ref = rest
        pq_ref[0] = pa[:, 0:aw + iw]

    cw = pc.shape[1] // 3
    u = pc[:, cw:2 * cw] * pc[:, 2 * cw:3 * cw]

    @pl.when(s == 0)
    def _():
        carry_ref[6:8, :] = buf_ref[0]

    carry_ref[8:8 + ts, :] = u
    y = (carry_ref[6:6 + ts, :] * wconv_ref[0:1, :]
         + carry_ref[7:7 + ts, :] * wconv_ref[1:2, :]
         + u * wconv_ref[2:3, :])
    conv_ref[0] = (pc[:, 0:cw] * y).astype(BF16)
    nb = carry_ref[ts + 6:ts + 8, :]
    nbuf_ref[0] = nb
    carry_ref[6:8, :] = nb


def _project(x, conv_buf, wa, ws, wc, wconv, *, attn_layouts):
    B, S, D = x.shape
    ts = min(KEY_CHUNK, S)
    assert S % ts == 0 and S >= CONV_K - 1
    ns = S // ts
    cw = wc.shape[1] // 3
    sw = IDX_DIM if attn_layouts else LANES
    kv = N_KV_HEADS * HEAD_DIM
    aw = N_HEADS * HEAD_DIM
    iw = N_IDX_HEADS * IDX_DIM
    out_shape = [
        jax.ShapeDtypeStruct((B, S, kv), F32),
        jax.ShapeDtypeStruct((B, S, kv), F32),
        jax.ShapeDtypeStruct((B, S, sw), F32),
        jax.ShapeDtypeStruct((B, S, cw), BF16),
        jax.ShapeDtypeStruct((B, CONV_K - 1, cw), F32),
    ]
    out_specs = [
        pl.BlockSpec((1, ts, kv), lambda b, s: (b, s, 0)),
        pl.BlockSpec((1, ts, kv), lambda b, s: (b, s, 0)),
        pl.BlockSpec((1, ts, sw), lambda b, s: (b, s, 0)),
        pl.BlockSpec((1, ts, cw), lambda b, s: (b, s, 0)),
        pl.BlockSpec((1, CONV_K - 1, cw), lambda b, s: (b, 0, 0)),
    ]
    if attn_layouts:
        out_shape += [
            jax.ShapeDtypeStruct((B, ns, N_KV_HEADS, ts, HEAD_DIM), BF16),
            jax.ShapeDtypeStruct((B, ns, kv, ts), BF16),
            jax.ShapeDtypeStruct((B, ns, ts, IDX_DIM), BF16),
            jax.ShapeDtypeStruct((B, N_HEADS, HEAD_DIM, S), BF16),
            jax.ShapeDtypeStruct((B, N_IDX_HEADS, IDX_DIM, S), BF16),
            jax.ShapeDtypeStruct((B, N_IDX_HEADS, S), F32),
            jax.ShapeDtypeStruct((B, ns, N_KV_HEADS, ts), F32),
        ]
        out_specs += [
            pl.BlockSpec((1, 1, N_KV_HEADS, ts, HEAD_DIM), lambda b, s: (b, s, 0, 0, 0)),
            pl.BlockSpec((1, 1, kv, ts), lambda b, s: (b, s, 0, 0)),
            pl.BlockSpec((1, 1, ts, IDX_DIM), lambda b, s: (b, s, 0, 0)),
            pl.BlockSpec((1, N_HEADS, HEAD_DIM, ts), lambda b, s: (b, 0, 0, s)),
            pl.BlockSpec((1, N_IDX_HEADS, IDX_DIM, ts), lambda b, s: (b, 0, 0, s)),
            pl.BlockSpec((1, N_IDX_HEADS, ts), lambda b, s: (b, 0, s)),
            pl.BlockSpec((1, 1, N_KV_HEADS, ts), lambda b, s: (b, s, 0, 0)),
        ]
    else:
        out_shape += [jax.ShapeDtypeStruct((B, S, aw + iw), F32)]
        out_specs += [pl.BlockSpec((1, ts, aw + iw), lambda b, s: (b, s, 0))]
    const2 = lambda b, s: (0, 0)
    in_specs = [
        pl.BlockSpec((1, ts, D), lambda b, s: (b, s, 0)),
        pl.BlockSpec(wa.shape, const2),
        pl.BlockSpec(ws.shape, const2),
        pl.BlockSpec(wc.shape, const2),
        pl.BlockSpec(wconv.shape, const2),
        pl.BlockSpec((1, CONV_K - 1, cw), lambda b, s: (b, 0, 0)),
    ]
    return pl.pallas_call(
        functools.partial(_proj_kernel, ts=ts, attn_layouts=attn_layouts),
        grid=(B, ns),
        in_specs=in_specs,
        out_specs=tuple(out_specs),
        out_shape=tuple(out_shape),
        scratch_shapes=[pltpu.VMEM((ts + 8, cw), F32)],
        compiler_params=_cparams(("arbitrary", "arbitrary")),
        name="proj_conv",
    )(x, wa, ws, wc, wconv, conv_buf)


def _attn_kernel(qT_ref, qiT_ref, wT_ref, kidx_ref, kb_ref, vT_ref, knorm_ref, tab_ref,
                 o_ref, skey_ref, hi16_ref, lo16_ref, jb_ref, madd_ref, *head_refs,
                 qb, kc, past, l_true, topk, offsets):
    s_refs, p_refs, m_refs, l_refs, acc_refs = (head_refs[g * N_HEADS:(g + 1) * N_HEADS] for g in range(5))
    i = pl.program_id(1)
    qoff = past + i * qb
    adm_end = jnp.minimum(((qoff + qb - 1) // CHUNK + 1) * CHUNK, l_true)
    nck = (adm_end + kc - 1) // kc
    idx_bits = int(l_true).bit_length()
    groups = kc // SUBLANES

    rowi = lax.broadcasted_iota(I32, (kc, qb), 0)
    qpos = qoff + lax.broadcasted_iota(I32, (1, qb), 1)
    kmax = jnp.minimum((qpos // CHUNK + 1) * CHUNK, l_true)

    def score_body(c, carry):
        kidx_c = kidx_ref[0, c]
        sc = jnp.zeros((kc, qb), F32)
        for h in range(N_IDX_HEADS):
            d = _dot(kidx_c, qiT_ref[0, h])
            sc = sc + wT_ref[0, h:h + 1, :] * jnp.maximum(d, 0.0)
        bits = pltpu.bitcast(sc, I32)
        skey = jnp.where(bits < 0, bits ^ jnp.int32(0x7FFFFFFF), bits)
        skey = jnp.where(skey == -1, 0, skey)
        skey = jnp.where(rowi < kmax - c * kc, skey, jnp.int32(INT_MIN))
        skey_ref[c] = skey
        hi16_ref[c] = (skey >> 16).astype(I16)
        return carry

    lax.fori_loop(0, nck, score_body, 0)

    def count(pred_fn):
        def body(c, part):
            ind = jnp.where(pred_fn(c, skey_ref[c]), 1, 0)
            return part + jnp.sum(ind.reshape(groups, SUBLANES, qb), axis=0)
        part = lax.fori_loop(0, nck, body, jnp.zeros((SUBLANES, qb), I32))
        return jnp.sum(part, axis=0, keepdims=True)

    def count16(ref, pred_fn):
        pack = 2 * SUBLANES

        def body(c, part):
            accs = [None] * 4
            for g in range(kc // pack):
                ind = jnp.where(pred_fn(ref[c, g * pack:(g + 1) * pack, :]), jnp.bfloat16(1), jnp.bfloat16(0))
                accs[g % 4] = ind if accs[g % 4] is None else accs[g % 4] + ind
            return part + ((accs[0] + accs[1]) + (accs[2] + accs[3])).astype(F32)
        part = lax.fori_loop(0, nck, body, jnp.zeros((pack, qb), F32))
        return jnp.sum(part, axis=0, keepdims=True).astype(I32)

    def search16(ref, target, count_all):
        def body(it, carry):
            t_u, c_t = carry
            cand_u = t_u | (jnp.int32(1) << (15 - it))
            cand = (cand_u - 2 ** 15).astype(I16)
            cnt = count16(ref, lambda k: k >= cand)
            keep = cnt >= target
            return jnp.where(keep, cand_u, t_u), jnp.where(keep, cnt, c_t)
        return lax.fori_loop(0, 16, body, (jnp.zeros((1, qb), I32), count_all))

    stored = jnp.broadcast_to(nck * kc, (1, qb)).astype(I32)
    hi_u, n_ge_hi = search16(hi16_ref, topk, stored)
    hi_s = (hi_u - 2 ** 15).astype(I16)
    n_gt_hi = count16(hi16_ref, lambda k: k > hi_s)
    rest = topk - n_gt_hi

    def low_body(c, carry):
        low = ((skey_ref[c] & 0xFFFF) - 2 ** 15).astype(I16)
        lo16_ref[c] = jnp.where(hi16_ref[c] == hi_s, low, jnp.int16(-2 ** 15))
        return carry

    lax.fori_loop(0, nck, low_body, 0)
    lo_u, n_ge_lo = search16(lo16_ref, rest, n_ge_hi - n_gt_hi)
    thr = (hi_u - 2 ** 15) * 2 ** 16 + lo_u
    lo_s = (lo_u - 2 ** 15).astype(I16)
    cnt_ge = n_gt_hi + n_ge_lo
    cnt_gt = n_gt_hi + count16(lo16_ref, lambda k: k > lo_s)

    jb_ref[...] = jnp.full(jb_ref.shape, 2 ** 31 - 1, I32)

    @pl.when(jnp.max(cnt_ge) > topk)
    def _():
        need = topk - cnt_gt

        def tie_body(it, jb):
            cand = jb | (jnp.int32(1) << (idx_bits - 1 - it))
            cnt = count(lambda c, sk: jnp.logical_and(sk == thr, c * kc + rowi < cand))
            return jnp.where(cnt <= need, cand, jb)

        jb = lax.fori_loop(0, idx_bits, tie_body, jnp.zeros((1, qb), I32))
        jb_ref[...] = jnp.broadcast_to(jb, jb_ref.shape)

    jbound = jnp.where(thr == jnp.int32(INT_MIN), 0, jb_ref[0:1, :])
    thr_m1 = thr - 1

    for h in range(N_HEADS):
        l_refs[h][...] = jnp.zeros((1, qb), F32)
        acc_refs[h][...] = jnp.zeros((HEAD_DIM, qb), F32)
    n_slabs = kc // SLAB

    def fold(x, op):
        return op(x.reshape(SLAB // SUBLANES, SUBLANES, qb), axis=0)

    def near_bias(chunk_offset, h):
        tiles = []
        for t in range(kc // LANES):
            row = []
            for u in range(qb // LANES):
                d_tu = chunk_offset + (t - u) * LANES
                row.append(tab_ref[offsets.index(d_tu), h] if d_tu in offsets else jnp.zeros((LANES, LANES), F32))
            tiles.append(jnp.concatenate(row, axis=1))
        return jnp.concatenate(tiles, axis=0)

    def attn_chunk(c, near):
        sk = skey_ref[c]
        t_eff = jnp.where(rowi < jbound - c * kc, thr_m1, thr)
        madd_ref[...] = jnp.where(sk > t_eff, 0.0, MASK_VALUE)
        maccs = []
        for h in range(N_HEADS):
            x = _dot(kb_ref[0, c, h // GROUP], qT_ref[0, h]) + madd_ref[...]
            if near is not None:
                x = x + near_bias(near, h)
            s_refs[h][...] = x
            maccs.append(jnp.max(x.reshape(groups, SUBLANES, qb), axis=0))
        alphas = []
        for h in range(N_HEADS):
            s_ref, p_ref, m_ref = s_refs[h], p_refs[h], m_refs[h]
            m_old = m_ref[...]
            m_new = jnp.maximum(m_old, jnp.max(maccs[h], axis=0, keepdims=True))
            for j in range(n_slabs):
                rows = pl.ds(j * SLAB, SLAB)
                p_ref[rows, :] = jnp.exp2(s_ref[rows, :] - m_new).astype(BF16)
            m_ref[...] = m_new
            alphas.append(jnp.exp2(m_old - m_new))
        ones = jnp.ones((2 * SUBLANES, kc), BF16)
        v_aug = [jnp.concatenate([vT_ref[0, c, n * HEAD_DIM:(n + 1) * HEAD_DIM, :], ones], axis=0)
                 for n in range(N_KV_HEADS)]
        for h in range(N_HEADS):
            pv = _dot(v_aug[h // GROUP], p_refs[h][...])
            acc_refs[h][...] = alphas[h] * acc_refs[h][...] + pv[0:HEAD_DIM]
            l_refs[h][...] = alphas[h] * l_refs[h][...] + pv[HEAD_DIM:HEAD_DIM + 1]

    def attn_chunk_fixed(c, near):
        sk = skey_ref[c]
        t_eff = jnp.where(rowi < jbound - c * kc, thr_m1, thr)
        madd_ref[...] = jnp.where(sk > t_eff, 0.0, MASK_VALUE)
        ones = jnp.ones((2 * SUBLANES, kc), BF16)
        v_aug = [jnp.concatenate([vT_ref[0, c, n * HEAD_DIM:(n + 1) * HEAD_DIM, :], ones], axis=0)
                 for n in range(N_KV_HEADS)]
        for h in range(N_HEADS):
            x = _dot(kb_ref[0, c, h // GROUP], qT_ref[0, h]) + madd_ref[...]
            if near is not None:
                x = x + near_bias(near, h)
            p_refs[h][...] = jnp.exp2(x).astype(BF16)
        for h in range(N_HEADS):
            pv = _dot(v_aug[h // GROUP], p_refs[h][...])
            acc_refs[h][...] += pv[0:HEAD_DIM]
            l_refs[h][...] += pv[HEAD_DIM:HEAD_DIM + 1]

    def run(chunk_fn):
        c_diag = qoff // kc
        lax.fori_loop(0, c_diag - 1, lambda c, carry: (chunk_fn(c, None), carry)[1], 0)

        @pl.when(c_diag >= 1)
        def _():
            chunk_fn(c_diag - 1, -kc)

        chunk_fn(c_diag, 0)

    q_sq = jnp.zeros((1, qb), F32)
    for h in range(N_HEADS):
        qf = qT_ref[0, h].astype(F32)
        q_sq = jnp.maximum(q_sq, jnp.sum(qf * qf, axis=0, keepdims=True))
    bound = jnp.sqrt(jnp.max(q_sq) * jnp.max(knorm_ref[0])) + jnp.max(jnp.abs(tab_ref[...]))
    fixed_shift = bound < EXP2_SAFE

    @pl.when(fixed_shift)
    def _():
        run(attn_chunk_fixed)

    @pl.when(jnp.logical_not(fixed_shift))
    def _():
        for h in range(N_HEADS):
            m_refs[h][...] = jnp.full((1, qb), MASK_VALUE, F32)
        run(attn_chunk)

    oT = jnp.concatenate([acc_refs[h][...] / l_refs[h][...] for h in range(N_HEADS)], axis=0)
    o_ref[0] = oT.T.astype(o_ref.dtype)


def _attend(qT, qiT, wT, kidxb, kb, vT, knorm, tabs, *, qb, past, l_true, offsets):
    B, _, _, S = qT.shape
    nc, kc = vT.shape[1], vT.shape[3]
    assert S % qb == 0 and qb % LANES == 0 and kc % LANES == 0
    assert past % kc == 0 and qb <= kc and (qb == kc or S == qb)
    topk = min(TOPK_MAX, l_true // 4)
    kern = functools.partial(_attn_kernel, qb=qb, kc=kc, past=past, l_true=l_true,
                             topk=topk, offsets=offsets)
    in_specs = [
        pl.BlockSpec((1, N_HEADS, HEAD_DIM, qb), lambda b, i: (b, 0, 0, i)),
        pl.BlockSpec((1, N_IDX_HEADS, IDX_DIM, qb), lambda b, i: (b, 0, 0, i)),
        pl.BlockSpec((1, N_IDX_HEADS, qb), lambda b, i: (b, 0, i)),
        pl.BlockSpec((1,) + kidxb.shape[1:], lambda b, i: (b, 0, 0, 0)),
        pl.BlockSpec((1,) + kb.shape[1:], lambda b, i: (b, 0, 0, 0, 0)),
        pl.BlockSpec((1,) + vT.shape[1:], lambda b, i: (b, 0, 0, 0)),
        pl.BlockSpec((1,) + knorm.shape[1:], lambda b, i: (b, 0, 0, 0)),
        pl.BlockSpec(tabs.shape, lambda b, i: (0, 0, 0, 0)),
    ]
    return pl.pallas_call(
        kern,
        grid=(B, S // qb),
        in_specs=in_specs,
        out_specs=pl.BlockSpec((1, qb, N_HEADS * HEAD_DIM), lambda b, i: (b, i, 0)),
        out_shape=jax.ShapeDtypeStruct((B, S, N_HEADS * HEAD_DIM), BF16),
        scratch_shapes=[
            pltpu.VMEM((nc, kc, qb), I32),
            pltpu.VMEM((nc, kc, qb), I16),
            pltpu.VMEM((nc, kc, qb), I16),
            pltpu.VMEM((SUBLANES, qb), I32),
            pltpu.VMEM((kc, qb), F32),
        ] + [pltpu.VMEM((kc, qb), F32)] * N_HEADS
          + [pltpu.VMEM((kc, qb), BF16)] * N_HEADS
          + [pltpu.VMEM((1, qb), F32)] * N_HEADS
          + [pltpu.VMEM((1, qb), F32)] * N_HEADS
          + [pltpu.VMEM((HEAD_DIM, qb), F32)] * N_HEADS,
        compiler_params=_cparams(("arbitrary", "arbitrary")),
        name="dsa_attend",
    )(qT, qiT, wT, kidxb, kb, vT, knorm, tabs)


def _mix_kernel(x_ref, attn_ref, conv_ref, wo_ref, g_ref, b_ref, h_ref, *, dn_alpha):
    cat = jnp.concatenate([attn_ref[...], conv_ref[...]], axis=1)
    mixed = _dot(cat, wo_ref[...])
    h_ref[...] = _layer_norm(dn_alpha * x_ref[...] + mixed, g_ref[...], b_ref[...])


def _mix(x2, attn2, conv2, wo, g, b, *, dn_alpha):
    T, D = x2.shape
    tt = min(1024, T)
    assert T % tt == 0
    aw, cw = attn2.shape[1], conv2.shape[1]
    row = lambda i: (i, 0)
    const = lambda i: (0, 0)
    return pl.pallas_call(
        functools.partial(_mix_kernel, dn_alpha=dn_alpha),
        grid=(T // tt,),
        in_specs=[pl.BlockSpec((tt, D), row), pl.BlockSpec((tt, aw), row), pl.BlockSpec((tt, cw), row),
                  pl.BlockSpec(wo.shape, const), pl.BlockSpec((1, D), const), pl.BlockSpec((1, D), const)],
        out_specs=pl.BlockSpec((tt, D), row),
        out_shape=jax.ShapeDtypeStruct((T, D), F32),
        compiler_params=_cparams(("arbitrary",)),
        name="outproj_ln1",
    )(x2, attn2, conv2, wo, g, b)


def _route_picks(logits):
    lane = lax.broadcasted_iota(I32, logits.shape, 1)
    work = logits
    vals, hots = [], []
    for _ in range(TOP_K):
        m = jnp.max(work, axis=1, keepdims=True)
        idx = jnp.min(jnp.where(work == m, lane, LANES), axis=1, keepdims=True)
        hot = lane == idx
        vals.append(m)
        hots.append(hot)
        work = jnp.where(hot, -jnp.inf, work)
    es = [jnp.exp(v - vals[0]) for v in vals]
    den = es[0]
    for e in es[1:]:
        den = den + e
    return hots, [e / den for e in es]


def _route(logits):
    hots, gates = _route_picks(logits)
    comb = jnp.zeros(logits.shape, F32)
    for gate, hot in zip(gates, hots):
        comb = jnp.where(hot, gate, comb)
    return comb


def _expert_mlp(xb, wgu_ref, bgu_ref, wd_ref, bd_ref):
    ff = wd_ref.shape[1]
    gu = _dot(xb, wgu_ref[0]) + bgu_ref[0]
    gate = jnp.minimum(gu[:, 0:ff], SWIGLU_LIMIT)
    up = jnp.clip(gu[:, ff:2 * ff], -SWIGLU_LIMIT, SWIGLU_LIMIT)
    act = (up + 1.0) * (gate * (1.0 / (1.0 + jnp.exp(-SWIGLU_ALPHA * gate))))
    return _dot(act.astype(BF16), wd_ref[0]) + bd_ref[0]


def _moe_kernel(h_ref, wr_ref, br_ref, wgu_ref, bgu_ref, wd_ref, bd_ref, g_ref, b_ref,
                y_ref, hb_ref, comb_ref, *, dn_alpha):
    e = pl.program_id(1)
    tt = h_ref.shape[0]

    @pl.when(e == 0)
    def _():
        hb = h_ref[...].astype(BF16)
        hb_ref[...] = hb
        comb_ref[...] = _route(_dot(hb, wr_ref[...]) + br_ref[...])
        y_ref[...] = jnp.zeros(y_ref.shape, F32)

    y = _expert_mlp(hb_ref[...], wgu_ref, bgu_ref, wd_ref, bd_ref)
    lane = lax.broadcasted_iota(I32, (tt, LANES), 1)
    y_ref[...] += jnp.sum(jnp.where(lane == e, comb_ref[...], 0.0), axis=1, keepdims=True) * y

    @pl.when(e == pl.num_programs(1) - 1)
    def _():
        y_ref[...] = _layer_norm(dn_alpha * h_ref[...] + y_ref[...], g_ref[...], b_ref[...])


def _moe(h2, wr, br, wgu, bgu, wd, bd, g, b, *, dn_alpha):
    T, D = h2.shape
    E, _, F2 = wgu.shape
    tt = min(512, T)
    assert T % tt == 0
    row = lambda i, e: (i, 0)
    const = lambda i, e: (0, 0)
    exp3 = lambda i, e: (e, 0, 0)
    return pl.pallas_call(
        functools.partial(_moe_kernel, dn_alpha=dn_alpha),
        grid=(T // tt, E),
        in_specs=[pl.BlockSpec((tt, D), row), pl.BlockSpec(wr.shape, const), pl.BlockSpec(br.shape, const),
                  pl.BlockSpec((1, D, F2), exp3), pl.BlockSpec((1, 1, F2), exp3),
                  pl.BlockSpec((1, F2 // 2, D), exp3), pl.BlockSpec((1, 1, D), exp3),
                  pl.BlockSpec((1, D), const), pl.BlockSpec((1, D), const)],
        out_specs=pl.BlockSpec((tt, D), row),
        out_shape=jax.ShapeDtypeStruct((T, D), F32),
        scratch_shapes=[pltpu.VMEM((tt, D), BF16), pltpu.VMEM((tt, LANES), F32)],
        compiler_params=_cparams(("arbitrary", "arbitrary")),
        name="moe_ln2",
    )(h2, wr, br, wgu, bgu, wd, bd, g, b)


def _pack_pairs(x):
    w = x.shape[1] // 2
    bits = pltpu.bitcast(x.astype(BF16).astype(F32), I32)
    return (bits[:, 0:w] & jnp.int32(-65536)) | lax.shift_right_logical(bits[:, w:2 * w], 16)


def _unpack_pairs(p):
    hi = pltpu.bitcast(p & jnp.int32(-65536), F32)
    lo = pltpu.bitcast(lax.shift_left(p, 16), F32)
    return jnp.concatenate([hi, lo], axis=1)


def _mix_route_kernel(x_ref, attn_ref, conv_ref, wo_ref, g_ref, b_ref, wrT_ref, brT_ref,
                      h_ref, hpk_ref, infoT_ref, gcol_ref, cnt_ref, *, dn_alpha):
    cat = jnp.concatenate([attn_ref[...], conv_ref[...]], axis=1)
    h = _layer_norm(dn_alpha * x_ref[...] + _dot(cat, wo_ref[...]), g_ref[...], b_ref[...])
    h_ref[...] = h
    tt = h.shape[0]
    ne = cnt_ref.shape[1]
    hpk_ref[...] = _pack_pairs(h)
    logits = lax.dot_general(wrT_ref[...], h.astype(BF16), (((1,), (1,)), ((), ())),
                             preferred_element_type=F32)
    work = logits[0:ne] + jnp.tile(brT_ref[0:ne, :], (1, tt // LANES))
    sub = lax.broadcasted_iota(I32, (ne, tt), 0)
    vals, idxs, hots = [], [], []
    for _ in range(TOP_K):
        m = jnp.max(work, axis=0, keepdims=True)
        idx = jnp.min(jnp.where(work == m, sub, ne), axis=0, keepdims=True)
        hot = sub == idx
        vals.append(m)
        idxs.append(idx)
        hots.append(hot)
        work = jnp.where(hot, -jnp.inf, work)
    es = [jnp.exp(v - vals[0]) for v in vals]
    den = es[0]
    for e in es[1:]:
        den = den + e
    gates = [e / den for e in es]
    picked = jnp.zeros((ne, tt), F32)
    for hot in hots:
        picked = jnp.where(hot, 1.0, picked)
    earlier = lax.broadcasted_iota(I32, (tt, tt), 0) < lax.broadcasted_iota(I32, (tt, tt), 1)
    rank = _dot(picked.astype(BF16), jnp.where(earlier, 1.0, 0.0).astype(BF16))
    ranks = [jnp.sum(jnp.where(hot, rank, 0.0), axis=0, keepdims=True) for hot in hots]
    pad = jnp.zeros((2 * SUBLANES - 3 * TOP_K, tt), F32)
    infoT_ref[...] = jnp.concatenate([i.astype(F32) for i in idxs] + gates + ranks + [pad], axis=0)
    gcol_ref[...] = jnp.concatenate(gates + [jnp.zeros((LANES - TOP_K, tt), F32)], axis=0).T
    cnt_ref[0] = jnp.broadcast_to(jnp.sum(picked, axis=1, keepdims=True), (ne, LANES))


def _mix_route(x2, attn2, conv2, wo, g, b, wr, br, ne, *, dn_alpha):
    T, D = x2.shape
    tt = ROUTE_TILE
    aw, cw = attn2.shape[1], conv2.shape[1]
    row = lambda i: (i, 0)
    const = lambda i: (0, 0)
    wrT = wr.T
    brT = jnp.broadcast_to(br.reshape(LANES, 1), (LANES, LANES))
    return pl.pallas_call(
        functools.partial(_mix_route_kernel, dn_alpha=dn_alpha),
        grid=(T // tt,),
        in_specs=[pl.BlockSpec((tt, D), row), pl.BlockSpec((tt, aw), row), pl.BlockSpec((tt, cw), row),
                  pl.BlockSpec(wo.shape, const), pl.BlockSpec((1, D), const), pl.BlockSpec((1, D), const),
                  pl.BlockSpec(wrT.shape, const), pl.BlockSpec(brT.shape, const)],
        out_specs=(pl.BlockSpec((tt, D), row), pl.BlockSpec((tt, D // 2), row),
                   pl.BlockSpec((2 * SUBLANES, tt), lambda i: (0, i)),
                   pl.BlockSpec((tt, LANES), row), pl.BlockSpec((1, ne, LANES), lambda i: (i, 0, 0))),
        out_shape=(jax.ShapeDtypeStruct((T, D), F32), jax.ShapeDtypeStruct((T, D // 2), I32),
                   jax.ShapeDtypeStruct((2 * SUBLANES, T), F32),
                   jax.ShapeDtypeStruct((T, LANES), F32), jax.ShapeDtypeStruct((T // tt, ne, LANES), F32)),
        compiler_params=_cparams(("arbitrary",)),
        name="outproj_ln1_route",
    )(x2, attn2, conv2, wo, g, b, wrT, brT)


def _sc_gather_rows(table, idx):
    M, W = idx.shape[0], table.shape[1]
    n_workers = SC_CORES * SC_SUBCORES
    rows = SC_ROWS // 2
    per_w = M // n_workers
    n_ch = per_w // rows
    assert M == n_workers * n_ch * rows and n_ch % 2 == 0
    mesh = plsc.VectorSubcoreMesh(core_axis_name="c", subcore_axis_name="s")

    @functools.partial(
        pl.kernel, mesh=mesh,
        out_type=jax.ShapeDtypeStruct((M, W), table.dtype),
        scratch_types=[pltpu.VMEM((n_ch, rows), I32),
                       pltpu.VMEM((rows, W), table.dtype), pltpu.VMEM((rows, W), table.dtype),
                       pltpu.SemaphoreType.DMA, pltpu.SemaphoreType.DMA,
                       pltpu.SemaphoreType.DMA, pltpu.SemaphoreType.DMA],
    )
    def gather(table_hbm, idx_hbm, out_hbm, idx_v, buf0, buf1, g0, g1, w0, w1):
        wid = lax.axis_index("s") * SC_CORES + lax.axis_index("c")
        pltpu.sync_copy(idx_hbm.at[wid], idx_v)
        fetch = lambda j, buf, sem: pltpu.make_async_copy(table_hbm.at[idx_v.at[j]], buf, sem)
        store = lambda j, buf, sem: pltpu.make_async_copy(
            buf, out_hbm.at[pl.ds(wid * per_w + j * rows, rows)], sem)
        fetch(0, buf0, g0).start()

        @pl.loop(0, n_ch, step=2)
        def _(j):
            fetch(j + 1, buf1, g1).start()
            fetch(j, buf0, g0).wait()
            store(j, buf0, w0).start()
            fetch(j + 1, buf1, g1).wait()
            store(j + 1, buf1, w1).start()
            store(j, buf0, w0).wait()

            @pl.when(j + 2 < n_ch)
            def _():
                fetch(j + 2, buf0, g0).start()

            store(j + 1, buf1, w1).wait()

    return gather(table, idx.reshape(n_workers, n_ch, rows))


def _expert_rows_kernel(te_ref, used_ref, x_ref, wgu_ref, bgu_ref, wd_ref, bd_ref, y_ref):
    @pl.when(pl.program_id(0) < used_ref[0])
    def _():
        xg = _unpack_pairs(x_ref[...]).astype(BF16)
        y_ref[...] = _pack_pairs(_expert_mlp(xg, wgu_ref, bgu_ref, wd_ref, bd_ref))


def _expert_rows(tile_expert, used_tiles, xs, wgu, bgu, wd, bd):
    R, W = xs.shape
    E, D, F2 = wgu.shape
    exp3 = lambda j, te, used: (te[j], 0, 0)
    rows = lambda j, te, used: (j, 0)
    grid_spec = pltpu.PrefetchScalarGridSpec(
        num_scalar_prefetch=2,
        grid=(R // ROW_TILE,),
        in_specs=[pl.BlockSpec((ROW_TILE, W), rows),
                  pl.BlockSpec((1, D, F2), exp3), pl.BlockSpec((1, 1, F2), exp3),
                  pl.BlockSpec((1, F2 // 2, D), exp3), pl.BlockSpec((1, 1, D), exp3)],
        out_specs=pl.BlockSpec((ROW_TILE, W), rows),
    )
    return pl.pallas_call(
        _expert_rows_kernel,
        grid_spec=grid_spec,
        out_shape=jax.ShapeDtypeStruct((R, W), I32),
        compiler_params=_cparams(("arbitrary",)),
        name="moe_expert_rows",
    )(tile_expert, used_tiles, xs, wgu, bgu, wd, bd)


def _combine_kernel(h_ref, yg_ref, gcol_ref, g_ref, b_ref, o_ref, *, dn_alpha):
    acc = dn_alpha * h_ref[...]
    for k in range(TOP_K):
        acc = acc + gcol_ref[:, k:k + 1] * _unpack_pairs(yg_ref[k])
    o_ref[...] = _layer_norm(acc, g_ref[...], b_ref[...])


def _combine(h2, yg, gcol, g, b, *, dn_alpha):
    T, D = h2.shape
    tt = ROUTE_TILE
    row = lambda i: (i, 0)
    const = lambda i: (0, 0)
    return pl.pallas_call(
        functools.partial(_combine_kernel, dn_alpha=dn_alpha),
        grid=(T // tt,),
        in_specs=[pl.BlockSpec((tt, D), row), pl.BlockSpec((TOP_K, tt, D // 2), lambda i: (0, i, 0)),
                  pl.BlockSpec((tt, LANES), row),
                  pl.BlockSpec((1, D), const), pl.BlockSpec((1, D), const)],
        out_specs=pl.BlockSpec((tt, D), row),
        out_shape=jax.ShapeDtypeStruct((T, D), F32),
        compiler_params=_cparams(("arbitrary",)),
        name="moe_combine_ln2",
    )(h2, yg, gcol, g, b)


def _dest_kernel(infoT_ref, base_ref, dest_ref):
    tt = infoT_ref.shape[1]
    ne = base_ref.shape[1]
    base = jnp.tile(base_ref[0], (1, tt // LANES))
    sub = lax.broadcasted_iota(I32, (ne, tt), 0)
    rows = []
    for k in range(TOP_K):
        hot = sub == infoT_ref[k:k + 1, :].astype(I32)
        rows.append(jnp.sum(jnp.where(hot, base, 0.0), axis=0, keepdims=True)
                    + infoT_ref[2 * TOP_K + k:2 * TOP_K + k + 1, :])
    rows.append(jnp.zeros((SUBLANES - TOP_K, tt), F32))
    dest_ref[...] = jnp.concatenate(rows, axis=0).astype(I32)


def _dest_rows(infoT, base):
    T = infoT.shape[1]
    tt = ROUTE_TILE
    ne = base.shape[1]
    return pl.pallas_call(
        _dest_kernel,
        grid=(T // tt,),
        in_specs=[pl.BlockSpec((2 * SUBLANES, tt), lambda i: (0, i)), pl.BlockSpec((1, ne, LANES), lambda i: (i, 0, 0))],
        out_specs=pl.BlockSpec((SUBLANES, tt), lambda i: (0, i)),
        out_shape=jax.ShapeDtypeStruct((SUBLANES, T), I32),
        compiler_params=_cparams(("arbitrary",)),
        name="moe_dest_rows",
    )(infoT, base)


def _sc_scatter_rows(src, dest_km, n_rows):
    T, W = src.shape
    K = dest_km.shape[0]
    n_workers = SC_CORES * SC_SUBCORES
    per_w = T // n_workers
    n_ch = per_w // SC_ROWS
    assert T == n_workers * n_ch * SC_ROWS
    mesh = plsc.VectorSubcoreMesh(core_axis_name="c", subcore_axis_name="s")
    idx = dest_km.reshape(K, n_workers, n_ch, SC_ROWS).transpose(1, 0, 2, 3).reshape(n_workers, K * n_ch, SC_ROWS)

    @functools.partial(
        pl.kernel, mesh=mesh,
        out_type=jax.ShapeDtypeStruct((n_rows, W), src.dtype),
        scratch_types=[pltpu.VMEM((K * n_ch, SC_ROWS), I32), pltpu.VMEM((SC_ROWS, W), src.dtype),
                       pltpu.SemaphoreType.DMA],
    )
    def scatter(src_hbm, idx_hbm, out_hbm, idx_v, rows_v, sem):
        wid = lax.axis_index("s") * SC_CORES + lax.axis_index("c")
        pltpu.sync_copy(idx_hbm.at[wid], idx_v)

        @pl.loop(0, n_ch)
        def _(j):
            pltpu.sync_copy(src_hbm.at[pl.ds(wid * per_w + j * SC_ROWS, SC_ROWS)], rows_v)
            for k in range(K):
                pltpu.async_copy(rows_v, out_hbm.at[idx_v.at[k * n_ch + j]], sem).wait()

    return scatter(src, idx)


def _moe_sorted(h2, routed, wgu, bgu, wd, bd, g, b, *, dn_alpha):
    T, D = h2.shape
    E = wgu.shape[0]
    hpk, infoT, gcol, cnt = routed
    cnt = cnt[:, :, 0].astype(I32)
    total = jnp.sum(cnt, axis=0)
    padded = -(-total // ROW_TILE) * ROW_TILE
    ends = jnp.cumsum(padded)
    base = (ends - padded)[None, :] + jnp.cumsum(cnt, axis=0) - cnt
    base = jnp.broadcast_to(base.astype(F32)[:, :, None], cnt.shape + (LANES,))
    dest_km = _dest_rows(infoT, base)[0:TOP_K]
    R = T * TOP_K + E * ROW_TILE
    tile_start = jnp.arange(R // ROW_TILE, dtype=I32) * ROW_TILE
    tile_expert = jnp.minimum(jnp.sum((ends[None, :] <= tile_start[:, None]).astype(I32), axis=1), E - 1)
    used_tiles = (ends[E - 1] // ROW_TILE).reshape(1)
    xs = _sc_scatter_rows(hpk, dest_km, R)
    ys = _expert_rows(tile_expert, used_tiles, xs, wgu, bgu, wd, bd)
    yg = _sc_gather_rows(ys, dest_km.reshape(-1))
    return _combine(h2, yg.reshape(TOP_K, T, D // 2), gcol, g, b, dn_alpha=dn_alpha)


def _split_gu_kernel(w_ref, perm_ref, o_ref, *, band):
    n = w_ref.shape[2]
    for m in range(n // band):
        both = _dot(w_ref[0, :, m * band:(m + 1) * band].astype(BF16), perm_ref[...]).astype(BF16)
        o_ref[0, :, m * (band // 2):(m + 1) * (band // 2)] = both[:, 0:band // 2]
        o_ref[0, :, n // 2 + m * (band // 2):n // 2 + (m + 1) * (band // 2)] = both[:, band // 2:band]


def _split_gu(w_gu):
    E, D, N = w_gu.shape
    tr = min(1024, D)
    band = min(512, N)
    j = jnp.arange(band, dtype=I32)
    src = jnp.where(j < band // 2, 2 * j, 2 * (j - band // 2) + 1)
    perm = (jnp.arange(band, dtype=I32)[:, None] == src[None, :]).astype(BF16)
    return pl.pallas_call(
        functools.partial(_split_gu_kernel, band=band),
        grid=(E, D // tr),
        in_specs=[pl.BlockSpec((1, tr, N), lambda e, r: (e, r, 0)),
                  pl.BlockSpec((band, band), lambda e, r: (0, 0))],
        out_specs=pl.BlockSpec((1, tr, N), lambda e, r: (e, r, 0)),
        out_shape=jax.ShapeDtypeStruct((E, D, N), BF16),
        compiler_params=_cparams(("arbitrary", "arbitrary")),
        name="split_gate_up",
    )(w_gu, perm)


def _prep_weights(w_in, w_conv, w_out, ln1_g, ln1_b, w_router, b_router, w_gu, b_gu, w_down, b_down,
                  ln2_g, ln2_b):
    aw = N_HEADS * HEAD_DIM
    kv = N_KV_HEADS * HEAD_DIM
    iw = N_IDX_HEADS * IDX_DIM
    D = w_in.shape[0]
    cw = D - aw
    o_q, o_k, o_v = 0, aw, aw + kv
    o_qi = o_v + kv
    o_ki = o_qi + iw
    o_wi = o_ki + IDX_DIM
    o_c = o_wi + N_IDX_HEADS
    wa = jnp.concatenate([w_in[:, o_q:o_k], w_in[:, o_qi:o_ki], w_in[:, o_k:o_v], w_in[:, o_v:o_qi]],
                         axis=1).astype(BF16)
    ws = jnp.pad(w_in[:, o_ki:o_c], ((0, 0), (0, LANES - IDX_DIM - N_IDX_HEADS))).astype(BF16)
    wc = w_in[:, o_c:o_c + 3 * cw].astype(BF16)
    E = w_router.shape[1]
    wr = jnp.pad(w_router, ((0, 0), (0, LANES - E))).astype(BF16)
    br = jnp.pad(b_router, (0, LANES - E), constant_values=MASK_VALUE).reshape(1, LANES)
    F = w_gu.shape[2] // 2
    return dict(
        wa=wa, ws=ws, wc=wc, wconv=w_conv, wo=w_out.astype(BF16),
        ln1_g=ln1_g.reshape(1, D), ln1_b=ln1_b.reshape(1, D),
        wr=wr, br=br,
        wgu=_split_gu(w_gu),
        bgu=jnp.concatenate([b_gu[:, 0::2], b_gu[:, 1::2]], axis=1).reshape(E, 1, 2 * F),
        wd=w_down.astype(BF16), bd=b_down.reshape(E, 1, D),
        ln2_g=ln2_g.reshape(1, D), ln2_b=ln2_b.reshape(1, D),
    )


def _decode_layouts(pq, small, k_all, v_all, kidx_all, kc, qb):
    B, S, _ = pq.shape
    aw = N_HEADS * HEAD_DIM
    L, kv = k_all.shape[1], k_all.shape[2]
    lp = -(-L // kc) * kc
    nc = lp // kc
    padk = lambda a: jnp.pad(a, ((0, 0), (0, lp - L), (0, 0)))
    padq = lambda a: jnp.pad(a, ((0, 0), (0, 0), (0, 0), (0, qb - S)))
    q = pq[:, :, 0:aw] * QK_SCALE
    qT = padq(q.reshape(B, S, N_HEADS, HEAD_DIM).transpose(0, 2, 3, 1)).astype(BF16)
    qiT = padq(pq[:, :, aw:].reshape(B, S, N_IDX_HEADS, IDX_DIM).transpose(0, 2, 3, 1)).astype(BF16)
    wT = jnp.pad(small[:, :, IDX_DIM:IDX_DIM + N_IDX_HEADS].transpose(0, 2, 1), ((0, 0), (0, 0), (0, qb - S)))
    kb = padk(k_all).reshape(B, nc, kc, N_KV_HEADS, HEAD_DIM).transpose(0, 1, 3, 2, 4).astype(BF16)
    vT = padk(v_all).reshape(B, nc, kc, kv).transpose(0, 1, 3, 2).astype(BF16)
    kidxb = padk(kidx_all).reshape(B, nc, kc, IDX_DIM).astype(BF16)
    knorm = jnp.sum(kb.astype(F32) ** 2, axis=-1)
    return qT, qiT, wT, kidxb, kb, vT, knorm


def _layer(x, past_k, past_v, past_kidx, conv_buf, rel_bias, w, *, dn_alpha):
    B, S, D = x.shape
    prefill = past_k is None
    outs = _project(x, conv_buf, w["wa"], w["ws"], w["wc"], w["wconv"], attn_layouts=prefill)
    k, v, small, conv, new_buf = outs[:5]
    k_idx = small if prefill else small[:, :, 0:IDX_DIM]
    if prefill:
        past, l_true = 0, S
        qb = min(QUERY_BLOCK, S)
        kb, vT, kidxb, qT, qiT, wT, knorm = outs[5:]
    else:
        past = past_k.shape[1]
        l_true = past + S
        qb = -(-S // LANES) * LANES
        qT, qiT, wT, kidxb, kb, vT, knorm = _decode_layouts(
            outs[5], small, jnp.concatenate([past_k, k], axis=1), jnp.concatenate([past_v, v], axis=1),
            jnp.concatenate([past_kidx, k_idx], axis=1), math.gcd(past, KEY_CHUNK), qb)
    offsets = (-LANES, 0)
    tabs = _bias_tables(rel_bias, offsets)
    attn = _attend(qT, qiT, wT, kidxb, kb, vT, knorm, tabs, qb=qb, past=past, l_true=l_true, offsets=offsets)
    attn = attn[:, :S]
    mix_args = (x.reshape(B * S, D), attn.reshape(B * S, -1), conv.reshape(B * S, -1), w["wo"],
                w["ln1_g"], w["ln1_b"])
    experts = (w["wgu"], w["bgu"], w["wd"], w["bd"], w["ln2_g"], w["ln2_b"])
    sc_unit = SC_CORES * SC_SUBCORES * SC_ROWS
    sortable = ((B * S) % ROUTE_TILE == 0 and (B * S) % sc_unit == 0
                and (N_EXPERTS * ROW_TILE) % sc_unit == 0)
    if sortable and B * S >= SORTED_MIN_TOKENS:
        h, *routed = _mix_route(*mix_args, w["wr"], w["br"], N_EXPERTS, dn_alpha=dn_alpha)
        y = _moe_sorted(h, routed, *experts, dn_alpha=dn_alpha)
    else:
        h = _mix(*mix_args, dn_alpha=dn_alpha)
        y = _moe(h, w["wr"], w["br"], *experts, dn_alpha=dn_alpha)
    return (y.reshape(B, S, D), k.reshape(B, S, N_KV_HEADS, HEAD_DIM), v.reshape(B, S, N_KV_HEADS, HEAD_DIM),
            k_idx, new_buf)


def kernel(x_prompt, x_sample, cache_k, cache_v, cache_kidx, state_conv, rel_bias, w_in, w_conv, w_out,
           ln1_g, ln1_b, w_router, b_router, w_gu, b_gu, w_down, b_down, ln2_g, ln2_b):
    depth = w_in.shape[0]
    assert depth == 1
    dn_alpha = (2 * depth) ** 0.25
    kv = N_KV_HEADS * HEAD_DIM
    w = _prep_weights(w_in[0], w_conv[0], w_out[0], ln1_g[0], ln1_b[0], w_router[0], b_router[0],
                      w_gu[0], b_gu[0], w_down[0], b_down[0], ln2_g[0], ln2_b[0])
    Bp = x_prompt.shape[0]
    cw = w_conv.shape[2]
    zero_buf = jnp.zeros((Bp, CONV_K - 1, cw), F32)
    yp, k1, v1, i1, c1 = _layer(x_prompt, None, None, None, zero_buf, rel_bias, w, dn_alpha=dn_alpha)
    Bs, P = cache_k.shape[1], cache_k.shape[2]
    ys, k2, v2, i2, c2 = _layer(x_sample, cache_k[0].reshape(Bs, P, kv), cache_v[0].reshape(Bs, P, kv),
                                cache_kidx[0], state_conv[0], rel_bias, w, dn_alpha=dn_alpha)
    return (yp, ys, k1[None], v1[None], i1[None], c1[None], k2[None], v2[None], i2[None], c2[None])
```
